```python
import math
import jax, jax.numpy as jnp
from jax import lax
import numpy as np

D_MODEL = 1024
BATCH = 8
SEQ = 8192
DEPTH = 1

MEM_LEN = 256
HEAD_DIM = 64
MIX_W = D_MODEL
ATTN_W = MIX_W // 2
N_Q_HEADS = ATTN_W // HEAD_DIM
N_KV_HEADS = N_Q_HEADS // 4
GQA_GROUP = N_Q_HEADS // N_KV_HEADS
KV_W = N_KV_HEADS * HEAD_DIM
GM_W = MIX_W - ATTN_W
GM_HEADS = GM_W // HEAD_DIM
GM_DH = GM_W // GM_HEADS
IN_COLS = ATTN_W + 2 * KV_W + 2 * GM_W
WINDOW = 128
BLK = 128
CHUNK = 128
ROPE_THETA = 10000.0
XA_HEADS = 4
XA_DH = D_MODEL // XA_HEADS
D_FF = ((8 * D_MODEL // 3 + 127) // 128) * 128
CONV_W = 3
MAX_POS_OFFSET = 1024
EPS = 1e-6

kernel_name = "hybrid_swa_gmlp_xattn_convffn"


def rms_norm(x, g):
    xf = x.astype(jnp.float32)
    y = xf * lax.rsqrt(jnp.mean(xf * xf, axis=-1, keepdims=True) + EPS)
    return (y * g.astype(jnp.float32)).astype(x.dtype)


def rope(x, positions):
    dh = x.shape[-1]
    half = dh // 2
    inv_freq = 1.0 / (ROPE_THETA ** (jnp.arange(half, dtype=jnp.float32) * (2.0 / dh)))
    ang = positions.astype(jnp.float32)[..., None] * inv_freq
    cos = jnp.cos(ang)[:, :, None, :]
    sin = jnp.sin(ang)[:, :, None, :]
    xf = x.astype(jnp.float32)
    x1, x2 = xf[..., :half], xf[..., half:]
    out = jnp.concatenate([x1 * cos - x2 * sin, x2 * cos + x1 * sin], axis=-1)
    return out.astype(x.dtype)


def sliding_window_attn(q, k, v, sinks):
    B, S = q.shape[0], q.shape[1]
    nb = S // BLK
    qb = q.reshape(B, nb, BLK, N_KV_HEADS, GQA_GROUP, HEAD_DIM)
    kb = k.reshape(B, nb, BLK, N_KV_HEADS, HEAD_DIM)
    vb = v.reshape(B, nb, BLK, N_KV_HEADS, HEAD_DIM)
    pad = ((0, 0), (1, 0), (0, 0), (0, 0), (0, 0))
    kk = jnp.concatenate([jnp.pad(kb[:, :-1], pad), kb], axis=2)
    vv = jnp.concatenate([jnp.pad(vb[:, :-1], pad), vb], axis=2)
    scores = jnp.einsum('bnqhgd,bnkhd->bnhgqk', qb, kk).astype(jnp.float32)
    scores = scores * (1.0 / math.sqrt(HEAD_DIM))
    qi = jnp.arange(BLK)[:, None]
    kj = jnp.arange(2 * BLK)[None, :]
    diff = qi + BLK - kj
    band = (diff >= 0) & (diff < WINDOW)
    valid = (jnp.arange(nb)[:, None, None] > 0) | (kj >= BLK)[None]
    mask = (band[None] & valid)[None, :, None, None]
    scores = jnp.where(mask, scores, jnp.finfo(jnp.float32).min)
    sink = sinks.astype(jnp.float32).reshape(N_KV_HEADS, GQA_GROUP)[None, None, :, :, None, None]
    sink = jnp.broadcast_to(sink, scores.shape[:-1] + (1,))
    probs = jax.nn.softmax(jnp.concatenate([scores, sink], axis=-1), axis=-1)[..., :-1]
    out = jnp.einsum('bnhgqk,bnkhd->bnqhgd', probs.astype(v.dtype), vv)
    return out.reshape(B, S, ATTN_W)


def chunked_spatial_gating(u, v, ws, bs):
    B, S = v.shape[0], v.shape[1]
    nc = S // CHUNK
    vb = v.reshape(B, nc, CHUNK, GM_HEADS, GM_DH)
    causal = jnp.tril(jnp.ones((CHUNK, CHUNK), dtype=ws.dtype))
    mixed = jnp.einsum('hts,bnshd->bnthd', ws * causal[None], vb)
    mixed = mixed + bs.T[None, None, :, :, None]
    return u * mixed.reshape(B, S, GM_W)


def parallel_mixer(x, positions, mix_norm, w_in, q_norm, k_norm, attn_sinks,
                   gmlp_v_norm, gmlp_ws, gmlp_bs, attn_out_norm, gmlp_out_norm, w_out):
    B, S, _ = x.shape
    h = rms_norm(x, mix_norm)
    proj = h @ w_in
    q, k, v, gz = jnp.split(proj, [ATTN_W, ATTN_W + KV_W, ATTN_W + 2 * KV_W], axis=-1)
    q = rope(rms_norm(q.reshape(B, S, N_Q_HEADS, HEAD_DIM), q_norm), positions)
    k = rope(rms_norm(k.reshape(B, S, N_KV_HEADS, HEAD_DIM), k_norm), positions)
    v = v.reshape(B, S, N_KV_HEADS, HEAD_DIM)
    attn = sliding_window_attn(q, k, v, attn_sinks)
    gz = jax.nn.gelu(gz)
    gu, gv = jnp.split(gz, 2, axis=-1)
    gm = chunked_spatial_gating(gu, rms_norm(gv, gmlp_v_norm), gmlp_ws, gmlp_bs)
    y = jnp.concatenate([rms_norm(attn, attn_out_norm), rms_norm(gm, gmlp_out_norm)], axis=-1)
    return y @ w_out


def memory_cross_attn(x, mem, xa_norm, mem_norm, xa_wq, xa_wkv, xa_q_norm, xa_k_norm, xa_wo):
    B, S, _ = x.shape
    M = mem.shape[1]
    h = rms_norm(x, xa_norm)
    m = rms_norm(mem, mem_norm)
    q = rms_norm((h @ xa_wq).reshape(B, S, XA_HEADS, XA_DH), xa_q_norm)
    k, v = jnp.split(m @ xa_wkv, 2, axis=-1)
    k = rms_norm(k.reshape(B, M, XA_HEADS, XA_DH), xa_k_norm)
    v = v.reshape(B, M, XA_HEADS, XA_DH)
    scores = jnp.einsum('bshd,bmhd->bhsm', q, k).astype(jnp.float32) * (1.0 / math.sqrt(XA_DH))
    probs = jax.nn.softmax(scores, axis=-1).astype(v.dtype)
    out = jnp.einsum('bhsm,bmhd->bshd', probs, v).reshape(B, S, XA_HEADS * XA_DH)
    return out @ xa_wo


def conv_gated_ffn(x, ffn_norm, ffn_up, ffn_conv, ffn_conv_b, ffn_down):
    h = rms_norm(x, ffn_norm)
    a = h @ ffn_up
    c = lax.conv_general_dilated(
        a, ffn_conv.reshape(CONV_W, 1, 2 * D_FF).astype(a.dtype),
        window_strides=(1,), padding=[(CONV_W - 1, 0)],
        dimension_numbers=('NWC', 'WIO', 'NWC'),
        feature_group_count=2 * D_FF) + ffn_conv_b
    gate, up = jnp.split(c, 2, axis=-1)
    return (jax.nn.gelu(gate) * up) @ ffn_down


def _fwd_setup_inputs(seed: int = 0) -> dict:
    key = jax.random.key(seed)
    ks = iter(jax.random.split(key, 40))
    L = DEPTH

    def nrm(shape, scale):
        return jax.random.normal(next(ks), shape, jnp.float32) * scale

    def gain(shape):
        return 1.0 + 0.02 * jax.random.normal(next(ks), shape, jnp.float32)

    x = nrm((BATCH, SEQ, D_MODEL), 1.0)
    mem = nrm((BATCH, MEM_LEN, D_MODEL), 1.0)
    offs = jax.random.randint(next(ks), (BATCH, 1), 0, MAX_POS_OFFSET, dtype=jnp.int32)
    positions = (offs + jnp.arange(SEQ, dtype=jnp.int32)[None, :]).astype(jnp.int32)
    return {
        "x": x,
        "mem": mem,
        "positions": positions,
        "mix_norm": gain((L, D_MODEL)),
        "w_in": nrm((L, D_MODEL, IN_COLS), D_MODEL ** -0.5),
        "q_norm": gain((L, HEAD_DIM)),
        "k_norm": gain((L, HEAD_DIM)),
        "attn_sinks": nrm((L, N_Q_HEADS), 0.5),
        "gmlp_v_norm": gain((L, GM_W)),
        "gmlp_ws": nrm((L, GM_HEADS, CHUNK, CHUNK), 0.5 * CHUNK ** -0.5),
        "gmlp_bs": 1.0 + nrm((L, GM_HEADS, CHUNK), 0.02),
        "attn_out_norm": gain((L, ATTN_W)),
        "gmlp_out_norm": gain((L, GM_W)),
        "w_out": nrm((L, MIX_W, D_MODEL), MIX_W ** -0.5),
        "xa_norm": gain((L, D_MODEL)),
        "mem_norm": gain((L, D_MODEL)),
        "xa_wq": nrm((L, D_MODEL, XA_HEADS * XA_DH), D_MODEL ** -0.5),
        "xa_wkv": nrm((L, D_MODEL, 2 * XA_HEADS * XA_DH), D_MODEL ** -0.5),
        "xa_q_norm": gain((L, XA_DH)),
        "xa_k_norm": gain((L, XA_DH)),
        "xa_wo": nrm((L, XA_HEADS * XA_DH, D_MODEL), (XA_HEADS * XA_DH) ** -0.5),
        "ffn_norm": gain((L, D_MODEL)),
        "ffn_up": nrm((L, D_MODEL, 2 * D_FF), D_MODEL ** -0.5),
        "ffn_conv": nrm((L, CONV_W, 2 * D_FF), CONV_W ** -0.5),
        "ffn_conv_b": nrm((L, 2 * D_FF), 0.02),
        "ffn_down": nrm((L, D_FF, D_MODEL), D_FF ** -0.5),
    }


def _fwd_reference(x, mem, positions, mix_norm, w_in, q_norm, k_norm, attn_sinks,
              gmlp_v_norm, gmlp_ws, gmlp_bs, attn_out_norm, gmlp_out_norm, w_out,
              xa_norm, mem_norm, xa_wq, xa_wkv, xa_q_norm, xa_k_norm, xa_wo,
              ffn_norm, ffn_up, ffn_conv, ffn_conv_b, ffn_down):
    for l in range(DEPTH):
        x = x + parallel_mixer(x, positions, mix_norm[l], w_in[l], q_norm[l], k_norm[l],
                               attn_sinks[l], gmlp_v_norm[l], gmlp_ws[l], gmlp_bs[l],
                               attn_out_norm[l], gmlp_out_norm[l], w_out[l])
        x = x + memory_cross_attn(x, mem, xa_norm[l], mem_norm[l], xa_wq[l], xa_wkv[l],
                                  xa_q_norm[l], xa_k_norm[l], xa_wo[l])
        x = x + conv_gated_ffn(x, ffn_norm[l], ffn_up[l], ffn_conv[l], ffn_conv_b[l],
                               ffn_down[l])
    return x


import jax as _jax
import jax.numpy as _jnp

TWIN_FORMAT = 'train_step'
FWD_PARAMS = ['x', 'mem', 'positions', 'mix_norm', 'w_in', 'q_norm', 'k_norm', 'attn_sinks', 'gmlp_v_norm', 'gmlp_ws', 'gmlp_bs', 'attn_out_norm', 'gmlp_out_norm', 'w_out', 'xa_norm', 'mem_norm', 'xa_wq', 'xa_wkv', 'xa_q_norm', 'xa_k_norm', 'xa_wo', 'ffn_norm', 'ffn_up', 'ffn_conv', 'ffn_conv_b', 'ffn_down']
TWIN_WEIGHTS = ['mix_norm', 'w_in', 'q_norm', 'k_norm', 'attn_sinks', 'gmlp_v_norm', 'gmlp_ws', 'gmlp_bs', 'attn_out_norm', 'gmlp_out_norm', 'w_out', 'xa_norm', 'mem_norm', 'xa_wq', 'xa_wkv', 'xa_q_norm', 'xa_k_norm', 'xa_wo', 'ffn_norm', 'ffn_up', 'ffn_conv', 'ffn_conv_b', 'ffn_down']
TWIN_DIFF_INPUT = 'x'
TWIN_INPUTS = ['x', 'mem', 'positions', 'mix_norm', 'w_in', 'q_norm', 'k_norm', 'attn_sinks', 'gmlp_v_norm', 'gmlp_ws', 'gmlp_bs', 'attn_out_norm', 'gmlp_out_norm', 'w_out', 'xa_norm', 'mem_norm', 'xa_wq', 'xa_wkv', 'xa_q_norm', 'xa_k_norm', 'xa_wo', 'ffn_norm', 'ffn_up', 'ffn_conv', 'ffn_conv_b', 'ffn_down', 'loss_target', 'm_mix_norm', 'm_w_in', 'm_q_norm', 'm_k_norm', 'm_attn_sinks', 'm_gmlp_v_norm', 'm_gmlp_ws', 'm_gmlp_bs', 'm_attn_out_norm', 'm_gmlp_out_norm', 'm_w_out', 'm_xa_norm', 'm_mem_norm', 'm_xa_wq', 'm_xa_wkv', 'm_xa_q_norm', 'm_xa_k_norm', 'm_xa_wo', 'm_ffn_norm', 'm_ffn_up', 'm_ffn_conv', 'm_ffn_conv_b', 'm_ffn_down', 'v_mix_norm', 'v_w_in', 'v_q_norm', 'v_k_norm', 'v_attn_sinks', 'v_gmlp_v_norm', 'v_gmlp_ws', 'v_gmlp_bs', 'v_attn_out_norm', 'v_gmlp_out_norm', 'v_w_out', 'v_xa_norm', 'v_mem_norm', 'v_xa_wq', 'v_xa_wkv', 'v_xa_q_norm', 'v_xa_k_norm', 'v_xa_wo', 'v_ffn_norm', 'v_ffn_up', 'v_ffn_conv', 'v_ffn_conv_b', 'v_ffn_down']
TWIN_OUTPUTS = ['loss', 'grad_x', 'grad_mix_norm', 'grad_w_in', 'grad_q_norm', 'grad_k_norm', 'grad_attn_sinks', 'grad_gmlp_v_norm', 'grad_gmlp_ws', 'grad_gmlp_bs', 'grad_attn_out_norm', 'grad_gmlp_out_norm', 'grad_w_out', 'grad_xa_norm', 'grad_mem_norm', 'grad_xa_wq', 'grad_xa_wkv', 'grad_xa_q_norm', 'grad_xa_k_norm', 'grad_xa_wo', 'grad_ffn_norm', 'grad_ffn_up', 'grad_ffn_conv', 'grad_ffn_conv_b', 'grad_ffn_down', 'delta_mix_norm', 'delta_w_in', 'delta_q_norm', 'delta_k_norm', 'delta_attn_sinks', 'delta_gmlp_v_norm', 'delta_gmlp_ws', 'delta_gmlp_bs', 'delta_attn_out_norm', 'delta_gmlp_out_norm', 'delta_w_out', 'delta_xa_norm', 'delta_mem_norm', 'delta_xa_wq', 'delta_xa_wkv', 'delta_xa_q_norm', 'delta_xa_k_norm', 'delta_xa_wo', 'delta_ffn_norm', 'delta_ffn_up', 'delta_ffn_conv', 'delta_ffn_conv_b', 'delta_ffn_down', 'new_m_mix_norm', 'new_m_w_in', 'new_m_q_norm', 'new_m_k_norm', 'new_m_attn_sinks', 'new_m_gmlp_v_norm', 'new_m_gmlp_ws', 'new_m_gmlp_bs', 'new_m_attn_out_norm', 'new_m_gmlp_out_norm', 'new_m_w_out', 'new_m_xa_norm', 'new_m_mem_norm', 'new_m_xa_wq', 'new_m_xa_wkv', 'new_m_xa_q_norm', 'new_m_xa_k_norm', 'new_m_xa_wo', 'new_m_ffn_norm', 'new_m_ffn_up', 'new_m_ffn_conv', 'new_m_ffn_conv_b', 'new_m_ffn_down', 'new_v_mix_norm', 'new_v_w_in', 'new_v_q_norm', 'new_v_k_norm', 'new_v_attn_sinks', 'new_v_gmlp_v_norm', 'new_v_gmlp_ws', 'new_v_gmlp_bs', 'new_v_attn_out_norm', 'new_v_gmlp_out_norm', 'new_v_w_out', 'new_v_xa_norm', 'new_v_mem_norm', 'new_v_xa_wq', 'new_v_xa_wkv', 'new_v_xa_q_norm', 'new_v_xa_k_norm', 'new_v_xa_wo', 'new_v_ffn_norm', 'new_v_ffn_up', 'new_v_ffn_conv', 'new_v_ffn_conv_b', 'new_v_ffn_down']
TWIN_LEAF_KINDS = {'loss': 'loss', 'grad_x': 'grad_x', 'grad_mix_norm': 'grad_w', 'grad_w_in': 'grad_w', 'grad_q_norm': 'grad_w', 'grad_k_norm': 'grad_w', 'grad_attn_sinks': 'grad_w', 'grad_gmlp_v_norm': 'grad_w', 'grad_gmlp_ws': 'grad_w', 'grad_gmlp_bs': 'grad_w', 'grad_attn_out_norm': 'grad_w', 'grad_gmlp_out_norm': 'grad_w', 'grad_w_out': 'grad_w', 'grad_xa_norm': 'grad_w', 'grad_mem_norm': 'grad_w', 'grad_xa_wq': 'grad_w', 'grad_xa_wkv': 'grad_w', 'grad_xa_q_norm': 'grad_w', 'grad_xa_k_norm': 'grad_w', 'grad_xa_wo': 'grad_w', 'grad_ffn_norm': 'grad_w', 'grad_ffn_up': 'grad_w', 'grad_ffn_conv': 'grad_w', 'grad_ffn_conv_b': 'grad_w', 'grad_ffn_down': 'grad_w', 'delta_mix_norm': 'delta_w', 'delta_w_in': 'delta_w', 'delta_q_norm': 'delta_w', 'delta_k_norm': 'delta_w', 'delta_attn_sinks': 'delta_w', 'delta_gmlp_v_norm': 'delta_w', 'delta_gmlp_ws': 'delta_w', 'delta_gmlp_bs': 'delta_w', 'delta_attn_out_norm': 'delta_w', 'delta_gmlp_out_norm': 'delta_w', 'delta_w_out': 'delta_w', 'delta_xa_norm': 'delta_w', 'delta_mem_norm': 'delta_w', 'delta_xa_wq': 'delta_w', 'delta_xa_wkv': 'delta_w', 'delta_xa_q_norm': 'delta_w', 'delta_xa_k_norm': 'delta_w', 'delta_xa_wo': 'delta_w', 'delta_ffn_norm': 'delta_w', 'delta_ffn_up': 'delta_w', 'delta_ffn_conv': 'delta_w', 'delta_ffn_conv_b': 'delta_w', 'delta_ffn_down': 'delta_w', 'new_m_mix_norm': 'new_m', 'new_m_w_in': 'new_m', 'new_m_q_norm': 'new_m', 'new_m_k_norm': 'new_m', 'new_m_attn_sinks': 'new_m', 'new_m_gmlp_v_norm': 'new_m', 'new_m_gmlp_ws': 'new_m', 'new_m_gmlp_bs': 'new_m', 'new_m_attn_out_norm': 'new_m', 'new_m_gmlp_out_norm': 'new_m', 'new_m_w_out': 'new_m', 'new_m_xa_norm': 'new_m', 'new_m_mem_norm': 'new_m', 'new_m_xa_wq': 'new_m', 'new_m_xa_wkv': 'new_m', 'new_m_xa_q_norm': 'new_m', 'new_m_xa_k_norm': 'new_m', 'new_m_xa_wo': 'new_m', 'new_m_ffn_norm': 'new_m', 'new_m_ffn_up': 'new_m', 'new_m_ffn_conv': 'new_m', 'new_m_ffn_conv_b': 'new_m', 'new_m_ffn_down': 'new_m', 'new_v_mix_norm': 'new_v', 'new_v_w_in': 'new_v', 'new_v_q_norm': 'new_v', 'new_v_k_norm': 'new_v', 'new_v_attn_sinks': 'new_v', 'new_v_gmlp_v_norm': 'new_v', 'new_v_gmlp_ws': 'new_v', 'new_v_gmlp_bs': 'new_v', 'new_v_attn_out_norm': 'new_v', 'new_v_gmlp_out_norm': 'new_v', 'new_v_w_out': 'new_v', 'new_v_xa_norm': 'new_v', 'new_v_mem_norm': 'new_v', 'new_v_xa_wq': 'new_v', 'new_v_xa_wkv': 'new_v', 'new_v_xa_q_norm': 'new_v', 'new_v_xa_k_norm': 'new_v', 'new_v_xa_wo': 'new_v', 'new_v_ffn_norm': 'new_v', 'new_v_ffn_up': 'new_v', 'new_v_ffn_conv': 'new_v', 'new_v_ffn_conv_b': 'new_v', 'new_v_ffn_down': 'new_v'}


def _forward(args):
    return _fwd_reference(*[args[k] for k in FWD_PARAMS])


def _output_shape():
    def fwd():
        inp = _fwd_setup_inputs(0)
        return _fwd_reference(*[inp[k] for k in FWD_PARAMS])
    out = _jax.eval_shape(fwd)
    return out.shape, out.dtype

N_MICROBATCH = 1
ADAM_LR = 0.001
ADAM_B1 = 0.9
ADAM_B2 = 0.999
ADAM_EPS = 1e-08
ADAM_WD = 0.01
ADAM_STEP = 10
PER_EXAMPLE_BATCH_AXIS = {'x': 0, 'mem': 0, 'positions': 0, 'loss_target': 0}
SHARED_INPUTS = []
_WEIGHT_DTYPES = {'mix_norm': _jnp.float32, 'w_in': _jnp.float32, 'q_norm': _jnp.float32, 'k_norm': _jnp.float32, 'attn_sinks': _jnp.float32, 'gmlp_v_norm': _jnp.float32, 'gmlp_ws': _jnp.float32, 'gmlp_bs': _jnp.float32, 'attn_out_norm': _jnp.float32, 'gmlp_out_norm': _jnp.float32, 'w_out': _jnp.float32, 'xa_norm': _jnp.float32, 'mem_norm': _jnp.float32, 'xa_wq': _jnp.float32, 'xa_wkv': _jnp.float32, 'xa_q_norm': _jnp.float32, 'xa_k_norm': _jnp.float32, 'xa_wo': _jnp.float32, 'ffn_norm': _jnp.float32, 'ffn_up': _jnp.float32, 'ffn_conv': _jnp.float32, 'ffn_conv_b': _jnp.float32, 'ffn_down': _jnp.float32}
MOMENT_SCALE = {'mix_norm': 1.678877e+00, 'w_in': 1.158848e+00, 'q_norm': 3.138748e+00, 'k_norm': 2.275568e+00, 'attn_sinks': 1.619828e-01, 'gmlp_v_norm': 3.180884e-01, 'gmlp_ws': 3.453446e-01, 'gmlp_bs': 7.047881e-01, 'attn_out_norm': 6.355536e+01, 'gmlp_out_norm': 6.576576e+01, 'w_out': 7.993166e+00, 'xa_norm': 2.149835e-01, 'mem_norm': 1.023248e+00, 'xa_wq': 2.141199e-01, 'xa_wkv': 4.448370e-01, 'xa_q_norm': 2.458637e+00, 'xa_k_norm': 2.454371e+00, 'xa_wo': 5.886054e-01, 'ffn_norm': 6.361336e+01, 'ffn_up': 2.570608e+00, 'ffn_conv': 9.814037e+00, 'ffn_conv_b': 9.036267e+00, 'ffn_down': 1.475815e+00}


def _to_microbatches(a, axis):
    t = _jnp.moveaxis(a, axis, 0)
    t = t.reshape((N_MICROBATCH, t.shape[0] // N_MICROBATCH) + t.shape[1:])
    return _jnp.moveaxis(t, 1, axis + 1)


def setup_inputs(seed: int = 0) -> dict:
    inp = _fwd_setup_inputs(seed)
    key = _jax.random.fold_in(_jax.random.key(seed), 7919)
    shape, _ = _output_shape()
    out = dict(inp)
    out["loss_target"] = _jax.random.normal(_jax.random.fold_in(key, 0), shape, _jnp.float32)
    for i, name in enumerate(TWIN_WEIGHTS):
        w = inp[name].astype(_jnp.float32)
        if MOMENT_SCALE is None:
            s = _jnp.sqrt(_jnp.mean(_jnp.square(w)) + 1e-30)
        else:
            s = MOMENT_SCALE[name]
        km, kv = _jax.random.split(_jax.random.fold_in(key, i + 1))
        out[name] = w
        out["m_" + name] = s * _jax.random.normal(km, w.shape, _jnp.float32)
        out["v_" + name] = (s * s) * _jax.random.uniform(kv, w.shape, _jnp.float32, 0.5, 1.5)
    if N_MICROBATCH > 1:
        for name, axis in PER_EXAMPLE_BATCH_AXIS.items():
            out[name] = _to_microbatches(out[name], axis)
    return {'x': out['x'], 'mem': out['mem'], 'positions': out['positions'], 'mix_norm': out['mix_norm'], 'w_in': out['w_in'], 'q_norm': out['q_norm'], 'k_norm': out['k_norm'], 'attn_sinks': out['attn_sinks'], 'gmlp_v_norm': out['gmlp_v_norm'], 'gmlp_ws': out['gmlp_ws'], 'gmlp_bs': out['gmlp_bs'], 'attn_out_norm': out['attn_out_norm'], 'gmlp_out_norm': out['gmlp_out_norm'], 'w_out': out['w_out'], 'xa_norm': out['xa_norm'], 'mem_norm': out['mem_norm'], 'xa_wq': out['xa_wq'], 'xa_wkv': out['xa_wkv'], 'xa_q_norm': out['xa_q_norm'], 'xa_k_norm': out['xa_k_norm'], 'xa_wo': out['xa_wo'], 'ffn_norm': out['ffn_norm'], 'ffn_up': out['ffn_up'], 'ffn_conv': out['ffn_conv'], 'ffn_conv_b': out['ffn_conv_b'], 'ffn_down': out['ffn_down'], 'loss_target': out['loss_target'], 'm_mix_norm': out['m_mix_norm'], 'm_w_in': out['m_w_in'], 'm_q_norm': out['m_q_norm'], 'm_k_norm': out['m_k_norm'], 'm_attn_sinks': out['m_attn_sinks'], 'm_gmlp_v_norm': out['m_gmlp_v_norm'], 'm_gmlp_ws': out['m_gmlp_ws'], 'm_gmlp_bs': out['m_gmlp_bs'], 'm_attn_out_norm': out['m_attn_out_norm'], 'm_gmlp_out_norm': out['m_gmlp_out_norm'], 'm_w_out': out['m_w_out'], 'm_xa_norm': out['m_xa_norm'], 'm_mem_norm': out['m_mem_norm'], 'm_xa_wq': out['m_xa_wq'], 'm_xa_wkv': out['m_xa_wkv'], 'm_xa_q_norm': out['m_xa_q_norm'], 'm_xa_k_norm': out['m_xa_k_norm'], 'm_xa_wo': out['m_xa_wo'], 'm_ffn_norm': out['m_ffn_norm'], 'm_ffn_up': out['m_ffn_up'], 'm_ffn_conv': out['m_ffn_conv'], 'm_ffn_conv_b': out['m_ffn_conv_b'], 'm_ffn_down': out['m_ffn_down'], 'v_mix_norm': out['v_mix_norm'], 'v_w_in': out['v_w_in'], 'v_q_norm': out['v_q_norm'], 'v_k_norm': out['v_k_norm'], 'v_attn_sinks': out['v_attn_sinks'], 'v_gmlp_v_norm': out['v_gmlp_v_norm'], 'v_gmlp_ws': out['v_gmlp_ws'], 'v_gmlp_bs': out['v_gmlp_bs'], 'v_attn_out_norm': out['v_attn_out_norm'], 'v_gmlp_out_norm': out['v_gmlp_out_norm'], 'v_w_out': out['v_w_out'], 'v_xa_norm': out['v_xa_norm'], 'v_mem_norm': out['v_mem_norm'], 'v_xa_wq': out['v_xa_wq'], 'v_xa_wkv': out['v_xa_wkv'], 'v_xa_q_norm': out['v_xa_q_norm'], 'v_xa_k_norm': out['v_xa_k_norm'], 'v_xa_wo': out['v_xa_wo'], 'v_ffn_norm': out['v_ffn_norm'], 'v_ffn_up': out['v_ffn_up'], 'v_ffn_conv': out['v_ffn_conv'], 'v_ffn_conv_b': out['v_ffn_conv_b'], 'v_ffn_down': out['v_ffn_down']}


def _loss(weights, diff, rest, loss_target):
    with _jax.named_scope("forward"):
        args = {**rest, TWIN_DIFF_INPUT: diff, **{k: w.astype(_WEIGHT_DTYPES[k]) for k, w in weights.items()}}
        y = _forward(args)
    with _jax.named_scope("loss_head"):
        err = _jnp.square(y.astype(_jnp.float32) - loss_target)
        return 0.5 * _jnp.sum(_jnp.mean(err, axis=-1)) if err.ndim else 0.5 * err


def _adamw(w, g, m, v):
    m = ADAM_B1 * m + (1.0 - ADAM_B1) * g
    v = ADAM_B2 * v + (1.0 - ADAM_B2) * _jnp.square(g)
    m_hat = m / (1.0 - ADAM_B1 ** ADAM_STEP)
    v_hat = v / (1.0 - ADAM_B2 ** ADAM_STEP)
    delta = -ADAM_LR * (m_hat / (_jnp.sqrt(v_hat) + ADAM_EPS) + ADAM_WD * w)
    return delta, m, v


def reference(x, mem, positions, mix_norm, w_in, q_norm, k_norm, attn_sinks, gmlp_v_norm, gmlp_ws, gmlp_bs, attn_out_norm, gmlp_out_norm, w_out, xa_norm, mem_norm, xa_wq, xa_wkv, xa_q_norm, xa_k_norm, xa_wo, ffn_norm, ffn_up, ffn_conv, ffn_conv_b, ffn_down, loss_target, m_mix_norm, m_w_in, m_q_norm, m_k_norm, m_attn_sinks, m_gmlp_v_norm, m_gmlp_ws, m_gmlp_bs, m_attn_out_norm, m_gmlp_out_norm, m_w_out, m_xa_norm, m_mem_norm, m_xa_wq, m_xa_wkv, m_xa_q_norm, m_xa_k_norm, m_xa_wo, m_ffn_norm, m_ffn_up, m_ffn_conv, m_ffn_conv_b, m_ffn_down, v_mix_norm, v_w_in, v_q_norm, v_k_norm, v_attn_sinks, v_gmlp_v_norm, v_gmlp_ws, v_gmlp_bs, v_attn_out_norm, v_gmlp_out_norm, v_w_out, v_xa_norm, v_mem_norm, v_xa_wq, v_xa_wkv, v_xa_q_norm, v_xa_k_norm, v_xa_wo, v_ffn_norm, v_ffn_up, v_ffn_conv, v_ffn_conv_b, v_ffn_down):
    given = dict(x=x, mem=mem, positions=positions, mix_norm=mix_norm, w_in=w_in, q_norm=q_norm, k_norm=k_norm, attn_sinks=attn_sinks, gmlp_v_norm=gmlp_v_norm, gmlp_ws=gmlp_ws, gmlp_bs=gmlp_bs, attn_out_norm=attn_out_norm, gmlp_out_norm=gmlp_out_norm, w_out=w_out, xa_norm=xa_norm, mem_norm=mem_norm, xa_wq=xa_wq, xa_wkv=xa_wkv, xa_q_norm=xa_q_norm, xa_k_norm=xa_k_norm, xa_wo=xa_wo, ffn_norm=ffn_norm, ffn_up=ffn_up, ffn_conv=ffn_conv, ffn_conv_b=ffn_conv_b, ffn_down=ffn_down, loss_target=loss_target, m_mix_norm=m_mix_norm, m_w_in=m_w_in, m_q_norm=m_q_norm, m_k_norm=m_k_norm, m_attn_sinks=m_attn_sinks, m_gmlp_v_norm=m_gmlp_v_norm, m_gmlp_ws=m_gmlp_ws, m_gmlp_bs=m_gmlp_bs, m_attn_out_norm=m_attn_out_norm, m_gmlp_out_norm=m_gmlp_out_norm, m_w_out=m_w_out, m_xa_norm=m_xa_norm, m_mem_norm=m_mem_norm, m_xa_wq=m_xa_wq, m_xa_wkv=m_xa_wkv, m_xa_q_norm=m_xa_q_norm, m_xa_k_norm=m_xa_k_norm, m_xa_wo=m_xa_wo, m_ffn_norm=m_ffn_norm, m_ffn_up=m_ffn_up, m_ffn_conv=m_ffn_conv, m_ffn_conv_b=m_ffn_conv_b, m_ffn_down=m_ffn_down, v_mix_norm=v_mix_norm, v_w_in=v_w_in, v_q_norm=v_q_norm, v_k_norm=v_k_norm, v_attn_sinks=v_attn_sinks, v_gmlp_v_norm=v_gmlp_v_norm, v_gmlp_ws=v_gmlp_ws, v_gmlp_bs=v_gmlp_bs, v_attn_out_norm=v_attn_out_norm, v_gmlp_out_norm=v_gmlp_out_norm, v_w_out=v_w_out, v_xa_norm=v_xa_norm, v_mem_norm=v_mem_norm, v_xa_wq=v_xa_wq, v_xa_wkv=v_xa_wkv, v_xa_q_norm=v_xa_q_norm, v_xa_k_norm=v_xa_k_norm, v_xa_wo=v_xa_wo, v_ffn_norm=v_ffn_norm, v_ffn_up=v_ffn_up, v_ffn_conv=v_ffn_conv, v_ffn_conv_b=v_ffn_conv_b, v_ffn_down=v_ffn_down)
    weights = {n: given[n] for n in TWIN_WEIGHTS}
    shared = {n: given[n] for n in SHARED_INPUTS}
    per_example = {n: given[n] for n in ['x', 'mem', 'positions']}
    grad_fn = _jax.value_and_grad(_loss, argnums=(0, 1))

    def one_microbatch(ex, loss_target):
        ex = dict(ex)
        diff = ex.pop(TWIN_DIFF_INPUT)
        return grad_fn(weights, diff, {**shared, **ex}, loss_target)

    if N_MICROBATCH == 1:
        loss, (grad_w, grad_x) = one_microbatch(per_example, given["loss_target"])
    else:
        def body(carry, xs):
            loss_sum, grad_sum = carry
            l_k, (gw_k, gx_k) = one_microbatch(xs[0], xs[1])
            with _jax.named_scope("update"):
                return (loss_sum + l_k, _jax.tree.map(_jnp.add, grad_sum, gw_k)), gx_k

        init = (_jnp.zeros((), _jnp.float32), _jax.tree.map(_jnp.zeros_like, weights))
        (loss, grad_w), grad_x = _jax.lax.scan(body, init, (per_example, given["loss_target"]))
    with _jax.named_scope("update"):
        delta_w, new_m, new_v = {}, {}, {}
        for n in TWIN_WEIGHTS:
            delta_w[n], new_m[n], new_v[n] = _adamw(weights[n], grad_w[n], given["m_" + n], given["v_" + n])
    return (loss, grad_x, *[grad_w[n] for n in TWIN_WEIGHTS], *[delta_w[n] for n in TWIN_WEIGHTS],
            *[new_m[n] for n in TWIN_WEIGHTS], *[new_v[n] for n in TWIN_WEIGHTS])
```

```python
import functools
import math

import jax
import jax.numpy as jnp
from jax import lax
from jax.experimental import pallas as pl
from jax.experimental.pallas import tpu as pltpu

F32 = jnp.float32
BF16 = jnp.bfloat16

D = 1024
HD = 64
AW = 512
KW = 128
GW = 512
IN = AW + 2 * KW + 2 * GW
BLK = 128
MEM = 256
XH = 4
XD = 256
FF = 2816
EPS = 1e-6
ROPE_THETA = 10000.0
NDEV = 8
LR, B1, B2, AEPS, WD, STEP = 0.001, 0.9, 0.999, 1e-08, 0.01, 10

TM = 512
VMEM_LIMIT = 56 * 1024 * 1024
NEG = float(jnp.finfo(jnp.float32).min)
GELU_C0 = math.sqrt(2.0 / math.pi)
GELU_C1 = 0.044715
AXES = ("x", "y", "c")


def _dot(a, b):
    return jnp.dot(a, b, preferred_element_type=F32)


def _dot_nt(a, b):
    return lax.dot_general(a, b, (((1,), (1,)), ((), ())), preferred_element_type=F32)


def _dot_tn(a, b):
    return lax.dot_general(a, b, (((0,), (0,)), ((), ())), preferred_element_type=F32)


def _rs(x):
    return lax.rsqrt(jnp.mean(x * x, axis=-1, keepdims=True) + EPS)


def _rms_bwd(dy, x, r, g):
    xh = x * r
    dxh = dy * g
    dx = r * (dxh - xh * jnp.mean(dxh * xh, axis=-1, keepdims=True))
    return dx, dy * xh


def _lane(shape):
    return lax.broadcasted_iota(jnp.int32, shape, len(shape) - 1)


def _gsum64(v):
    w = v.shape[-1]
    lane = _lane(v.shape)
    s = v
    for sh in (1, 2, 4, 8, 16, 32):
        s = s + jnp.where((lane & sh) != 0, pltpu.roll(s, sh, 1), pltpu.roll(s, w - sh, 1))
    return s


def _rs64(x):
    return lax.rsqrt(_gsum64(x * x) * (1.0 / HD) + EPS)


def _rms64_bwd(dy, x, r, g):
    xh = x * r
    dxh = dy * g
    dx = r * (dxh - xh * (_gsum64(dxh * xh) * (1.0 / HD)))
    return dx, dy * xh


def _rot_half(v):
    w = v.shape[-1]
    return jnp.where((_lane(v.shape) & 32) == 0, pltpu.roll(v, w - 32, 1), pltpu.roll(v, 32, 1))


def _rope(v, cos, sin_signed):
    return v * cos + _rot_half(v) * sin_signed


def _rope_bwd(dv, cos, sin_signed):
    return dv * cos + _rot_half(dv * sin_signed)


def _gelu(z):
    return 0.5 * z * (1.0 + jnp.tanh(GELU_C0 * (z + GELU_C1 * z * z * z)))


def _gelu_grad(z):
    t = jnp.tanh(GELU_C0 * (z + GELU_C1 * z * z * z))
    return 0.5 * (1.0 + t) + 0.5 * z * (1.0 - t * t) * (GELU_C0 * (1.0 + 3.0 * GELU_C1 * z * z))


def _colsum8(v):
    s = jnp.sum(v, axis=0, keepdims=True)
    row = lax.broadcasted_iota(jnp.int32, (8, v.shape[1]), 0)
    return jnp.where(row == 0, jnp.broadcast_to(s, (8, v.shape[1])), 0.0)


def _params(n_axes=1):
    return pltpu.CompilerParams(dimension_semantics=("arbitrary",) * n_axes, vmem_limit_bytes=VMEM_LIMIT)


def _rows(tm, w):
    return pl.BlockSpec((tm, w), lambda i: (i, 0))


def _const(shape):
    nd = len(shape)
    return pl.BlockSpec(shape, lambda *_: (0,) * nd)


def _sds(shape, dtype=F32):
    return jax.ShapeDtypeStruct(shape, dtype)


def _mm_nn(a, b, tn, name):
    m, k = a.shape
    n = b.shape[1]

    def body(a_ref, b_ref, o_ref):
        o_ref[...] = _dot(a_ref[...].astype(BF16), b_ref[...])

    return pl.pallas_call(
        body, name=name, grid=(n // tn, m // TM),
        in_specs=[pl.BlockSpec((TM, k), lambda j, i: (i, 0)), pl.BlockSpec((k, tn), lambda j, i: (0, j))],
        out_specs=pl.BlockSpec((TM, tn), lambda j, i: (i, j)),
        out_shape=_sds((m, n)), compiler_params=_params(2))(a, b)


def _mm_nt(a, b, name):
    m, k = a.shape
    n = b.shape[0]

    def body(a_ref, b_ref, o_ref):
        o_ref[...] = _dot_nt(a_ref[...].astype(BF16), b_ref[...])

    return pl.pallas_call(
        body, name=name, grid=(m // TM,),
        in_specs=[_rows(TM, k), _const((n, k))], out_specs=_rows(TM, n),
        out_shape=_sds((m, n)), compiler_params=_params())(a, b)


def _mm_tn(a, b, tn, name):
    t, m = a.shape
    n = b.shape[1]

    def body(a_ref, b_ref, o_ref):
        @pl.when(pl.program_id(1) == 0)
        def _():
            o_ref[...] = jnp.zeros_like(o_ref)

        o_ref[...] += _dot_tn(a_ref[...].astype(BF16), b_ref[...].astype(BF16))

    return pl.pallas_call(
        body, name=name, grid=(n // tn, t // TM),
        in_specs=[pl.BlockSpec((TM, m), lambda j, i: (i, 0)), pl.BlockSpec((TM, tn), lambda j, i: (i, j))],
        out_specs=pl.BlockSpec((m, tn), lambda j, i: (0, j)),
        out_shape=_sds((m, n)), compiler_params=_params(2))(a, b)


def _rope_tables(pos, inv_freq):
    t = pos.shape[0]

    def body(pos_ref, f_ref, cos_ref, sin_ref):
        ang = pos_ref[...].astype(F32) * f_ref[...]
        sign = jnp.where((_lane(ang.shape) & 32) == 0, -1.0, 1.0)
        cos_ref[...] = jnp.cos(ang)
        sin_ref[...] = jnp.sin(ang) * sign

    return pl.pallas_call(
        body, name="rope_tables", grid=(t // TM,),
        in_specs=[_rows(TM, 1), _const((1, 128))], out_specs=[_rows(TM, 128), _rows(TM, 128)],
        out_shape=[_sds((t, 128)), _sds((t, 128))], compiler_params=_params())(pos, inv_freq)


def _mixer_in_fwd(x, mix_norm, w_in, qn, kn, gvw, cos, sin):
    t = x.shape[0]

    def body(x_ref, g_ref, w_ref, qn_ref, kn_ref, gvw_ref, cos_ref, sin_ref,
             h_ref, qk_ref, gz_ref, q_ref, k_ref, v_ref, gu_ref, gvn_ref):
        x = x_ref[...]
        h = (x * _rs(x) * g_ref[...]).astype(BF16)
        h_ref[...] = h
        proj = _dot(h, w_ref[...])
        qk = proj[:, :AW + KW]
        qk_ref[...] = qk
        gz = proj[:, AW + 2 * KW:]
        gz_ref[...] = gz
        cos2, sin2 = cos_ref[...], sin_ref[...]
        q = qk[:, :AW]
        q = q * _rs64(q) * qn_ref[...]
        q_ref[...] = _rope(q, jnp.tile(cos2, (1, 4)), jnp.tile(sin2, (1, 4))).astype(BF16)
        k = qk[:, AW:]
        k = k * _rs64(k) * kn_ref[...]
        k_ref[...] = _rope(k, cos2, sin2).astype(BF16)
        v_ref[...] = proj[:, AW + KW:AW + 2 * KW].astype(BF16)
        gu_ref[...] = _gelu(gz[:, :GW])
        gv = _gelu(gz[:, GW:])
        gvn_ref[...] = (gv * _rs(gv) * gvw_ref[...]).astype(BF16)

    return pl.pallas_call(
        body, name="mixer_in_fwd", grid=(t // TM,),
        in_specs=[_rows(TM, D), _const((1, D)), _const((D, IN)), _const((1, AW)), _const((1, KW)),
                  _const((1, GW)), _rows(TM, 128), _rows(TM, 128)],
        out_specs=[_rows(TM, D), _rows(TM, AW + KW), _rows(TM, 2 * GW), _rows(TM, AW), _rows(TM, KW),
                   _rows(TM, KW), _rows(TM, GW), _rows(TM, GW)],
        out_shape=[_sds((t, D), BF16), _sds((t, AW + KW)), _sds((t, 2 * GW)), _sds((t, AW), BF16),
                   _sds((t, KW), BF16), _sds((t, KW), BF16), _sds((t, GW)), _sds((t, GW), BF16)],
        compiler_params=_params())(x, mix_norm, w_in, qn, kn, gvw, cos, sin)


def _dup_half(kk, g):
    lane = _lane(kk.shape)
    other = pltpu.roll(kk, 64, 1)
    keep = (lane < 64) if g == 0 else (lane >= 64)
    return jnp.where(keep, kk, other).astype(BF16)


def _swa_mask(first_block):
    qi = lax.broadcasted_iota(jnp.int32, (4 * BLK, 2 * BLK), 0) & (BLK - 1)
    kj = lax.broadcasted_iota(jnp.int32, (4 * BLK, 2 * BLK), 1)
    diff = qi + BLK - kj
    band = (diff >= 0) & (diff < BLK)
    return band & (jnp.logical_not(first_block) | (kj >= BLK))


def _stack_heads(a2, b2):
    lo = _lane(a2.shape) < 64
    z = jnp.zeros_like(a2)
    return jnp.concatenate([jnp.where(lo, a2, z), jnp.where(lo, z, a2), jnp.where(lo, b2, z), jnp.where(lo, z, b2)], axis=0)


def _unstack_heads(o):
    lo = _lane((BLK, 128)) < 64
    return jnp.where(lo, o[0:BLK], o[BLK:2 * BLK]), jnp.where(lo, o[2 * BLK:3 * BLK], o[3 * BLK:4 * BLK])


def _sink_col(sink_ref, g):
    row = lax.broadcasted_iota(jnp.int32, (4 * BLK, 1), 0)
    s = [sink_ref[0, 4 * g + j] for j in range(4)]
    return jnp.where(row < BLK, s[0], jnp.where(row < 2 * BLK, s[1], jnp.where(row < 3 * BLK, s[2], s[3])))


def _swa_probs(qs, kd, mask, sink):
    s = _dot_nt(qs, kd) * (1.0 / math.sqrt(HD))
    s = jnp.where(mask, s, NEG)
    m = jnp.maximum(jnp.max(s, axis=-1, keepdims=True), sink)
    p = jnp.exp(s - m)
    ps = jnp.exp(sink - m)
    inv = 1.0 / (jnp.sum(p, axis=-1, keepdims=True) + ps)
    return p * inv, ps * inv


def _swa_fwd(q, k, v, sinks):
    t = q.shape[0]
    nb = t // BLK

    def body(sink_ref, q_ref, kc_ref, kp_ref, vc_ref, vp_ref, o_ref):
        i = pl.program_id(0)
        mask = _swa_mask(i == 0)
        kk = jnp.concatenate([kp_ref[...], kc_ref[...]], axis=0).astype(F32)
        vv = jnp.concatenate([vp_ref[...], vc_ref[...]], axis=0).astype(F32)
        for g in range(2):
            qs = _stack_heads(q_ref[:, 256 * g:256 * g + 128], q_ref[:, 256 * g + 128:256 * g + 256])
            pn, _ = _swa_probs(qs, _dup_half(kk, g), mask, _sink_col(sink_ref, g))
            oa, ob = _unstack_heads(_dot(pn.astype(BF16), _dup_half(vv, g)))
            o_ref[:, 256 * g:256 * g + 128] = oa
            o_ref[:, 256 * g + 128:256 * g + 256] = ob

    cur = lambda i: (i, 0)
    prev = lambda i: (jnp.maximum(i - 1, 0), 0)
    return pl.pallas_call(
        body, name="swa_fwd", grid=(nb,),
        in_specs=[pl.BlockSpec(memory_space=pltpu.SMEM), pl.BlockSpec((BLK, AW), cur),
                  pl.BlockSpec((BLK, KW), cur), pl.BlockSpec((BLK, KW), prev),
                  pl.BlockSpec((BLK, KW), cur), pl.BlockSpec((BLK, KW), prev)],
        out_specs=pl.BlockSpec((BLK, AW), cur), out_shape=_sds((t, AW)),
        compiler_params=_params())(sinks, q, k, k, v, v)


def _causal_bf16(w_ref, h, transposed):
    r = lax.broadcasted_iota(jnp.int32, (BLK, BLK), 0)
    c = lax.broadcasted_iota(jnp.int32, (BLK, BLK), 1)
    keep = (r <= c) if transposed else (c <= r)
    return jnp.where(keep, w_ref[h], 0.0).astype(BF16)


def _gmlp_mix(w_ref, xin, transposed):
    lo = _lane((BLK, 128)) < 64
    wm = [_causal_bf16(w_ref, h, transposed) for h in range(8)]
    rows = []
    for c in range(xin.shape[0] // BLK):
        cols = []
        for j in range(4):
            xs = xin[c * BLK:(c + 1) * BLK, 128 * j:128 * (j + 1)]
            cols.append(jnp.where(lo, _dot(wm[2 * j], xs), _dot(wm[2 * j + 1], xs)))
        rows.append(jnp.concatenate(cols, axis=1))
    return jnp.concatenate(rows, axis=0)


def _gmlp_fwd(gvn, gu, ws, bfull):
    t = gvn.shape[0]

    def body(x_ref, gu_ref, w_ref, b_ref, o_ref):
        mixed = _gmlp_mix(w_ref, x_ref[...], False) + jnp.tile(b_ref[...], (TM // BLK, 1))
        o_ref[...] = gu_ref[...] * mixed

    return pl.pallas_call(
        body, name="gmlp_fwd", grid=(t // TM,),
        in_specs=[_rows(TM, GW), _rows(TM, GW), _const((8, BLK, BLK)), _const((BLK, GW))],
        out_specs=_rows(TM, GW), out_shape=_sds((t, GW)), compiler_params=_params())(gvn, gu, ws, bfull)


def _mixer_out_fwd(attn, gm, x, w_out, aon, gon, xan):
    t = x.shape[0]

    def body(a_ref, g_ref, x_ref, w_ref, aon_ref, gon_ref, xan_ref, y_ref, x1_ref, h2_ref):
        a, g = a_ref[...], g_ref[...]
        y = jnp.concatenate([a * _rs(a) * aon_ref[...], g * _rs(g) * gon_ref[...]], axis=1).astype(BF16)
        y_ref[...] = y
        x1 = x_ref[...] + _dot(y, w_ref[...])
        x1_ref[...] = x1
        h2_ref[...] = (x1 * _rs(x1) * xan_ref[...]).astype(BF16)

    return pl.pallas_call(
        body, name="mixer_out_fwd", grid=(t // TM,),
        in_specs=[_rows(TM, AW), _rows(TM, GW), _rows(TM, D), _const((D, D)), _const((1, AW)), _const((1, GW)),
                  _const((1, D))],
        out_specs=[_rows(TM, D), _rows(TM, D), _rows(TM, D)],
        out_shape=[_sds((t, D), BF16), _sds((t, D)), _sds((t, D), BF16)],
        compiler_params=_params())(attn, gm, x, w_out, aon, gon, xan)


def _mem_kv_fwd(mem, mem_norm, wkv, kn4):
    def body(m_ref, g_ref, w_ref, kn_ref, mh_ref, kpre_ref, k_ref, v_ref):
        m = m_ref[...]
        mh = (m * _rs(m) * g_ref[...]).astype(BF16)
        mh_ref[...] = mh
        kv = _dot(mh, w_ref[...])
        kpre_ref[...] = kv[:, :D]
        v_ref[...] = kv[:, D:].astype(BF16)
        for h in range(XH):
            kh = kv[:, XD * h:XD * (h + 1)]
            k_ref[:, XD * h:XD * (h + 1)] = (kh * _rs(kh) * kn_ref[:, XD * h:XD * (h + 1)]).astype(BF16)

    return pl.pallas_call(
        body, name="mem_kv_fwd",
        out_shape=[_sds((MEM, D), BF16), _sds((MEM, D)), _sds((MEM, D), BF16), _sds((MEM, D), BF16)],
        compiler_params=pltpu.CompilerParams(vmem_limit_bytes=VMEM_LIMIT))(mem, mem_norm, wkv, kn4)


def _xattn_probs(qpre_h, qn_h, k_h):
    rq = _rs(qpre_h)
    q2 = (qpre_h * rq * qn_h).astype(BF16)
    s = _dot_nt(q2, k_h) * (1.0 / math.sqrt(XD))
    p = jnp.exp(s - jnp.max(s, axis=-1, keepdims=True))
    return p * (1.0 / jnp.sum(p, axis=-1, keepdims=True)), q2, rq


def _xattn_fwd(h2, x1, wq, qn4, k2, v2, wo, ffn_norm):
    t = x1.shape[0]

    def body(h_ref, x_ref, wq_ref, qn_ref, k_ref, v_ref, wo_ref, fn_ref, qpre_ref, o_ref, x2_ref, h3_ref):
        qpre = _dot(h_ref[...], wq_ref[...])
        qpre_ref[...] = qpre
        outs = []
        for h in range(XH):
            sl = slice(XD * h, XD * (h + 1))
            pn, _, _ = _xattn_probs(qpre[:, sl], qn_ref[:, sl], k_ref[:, sl])
            outs.append(_dot(pn.astype(BF16), v_ref[:, sl]))
        o = jnp.concatenate(outs, axis=1).astype(BF16)
        o_ref[...] = o
        x2 = x_ref[...] + _dot(o, wo_ref[...])
        x2_ref[...] = x2
        h3_ref[...] = (x2 * _rs(x2) * fn_ref[...]).astype(BF16)

    return pl.pallas_call(
        body, name="xattn_fwd", grid=(t // TM,),
        in_specs=[_rows(TM, D), _rows(TM, D), _const((D, D)), _const((1, D)), _const((MEM, D)), _const((MEM, D)),
                  _const((D, D)), _const((1, D))],
        out_specs=[_rows(TM, D)] * 4,
        out_shape=[_sds((t, D)), _sds((t, D), BF16), _sds((t, D)), _sds((t, D), BF16)],
        compiler_params=_params())(h2, x1, wq, qn4, k2, v2, wo, ffn_norm)


CT = FF // 2
CM = 256


def _conv(ext, w_ref, b_ref):
    n = ext.shape[0]
    c = w_ref[2:3, :] * ext + w_ref[1:2, :] * pltpu.roll(ext, 1, 0) + w_ref[0:1, :] * pltpu.roll(ext, 2, 0)
    return c[8:n] + b_ref[...]


def _conv_gate_fwd(a, conv, conv_b):
    t = a.shape[0]
    nj = FF // CT

    def body(ag_ref, au_ref, pg_ref, pu_ref, wg_ref, wu_ref, bg_ref, bu_ref, u_ref):
        live = (pl.program_id(1) > 0).astype(F32)
        eg = jnp.concatenate([pg_ref[...] * live, ag_ref[...]], axis=0)
        eu = jnp.concatenate([pu_ref[...] * live, au_ref[...]], axis=0)
        u_ref[...] = (_gelu(_conv(eg, wg_ref, bg_ref)) * _conv(eu, wu_ref, bu_ref)).astype(BF16)

    r8 = CM // 8
    return pl.pallas_call(
        body, name="conv_gate_fwd", grid=(nj, t // CM),
        in_specs=[pl.BlockSpec((CM, CT), lambda j, i: (i, j)), pl.BlockSpec((CM, CT), lambda j, i: (i, j + nj)),
                  pl.BlockSpec((8, CT), lambda j, i: (jnp.maximum(i * r8 - 1, 0), j)),
                  pl.BlockSpec((8, CT), lambda j, i: (jnp.maximum(i * r8 - 1, 0), j + nj)),
                  pl.BlockSpec((3, CT), lambda j, i: (0, j)), pl.BlockSpec((3, CT), lambda j, i: (0, j + nj)),
                  pl.BlockSpec((1, CT), lambda j, i: (0, j)), pl.BlockSpec((1, CT), lambda j, i: (0, j + nj))],
        out_specs=pl.BlockSpec((CM, CT), lambda j, i: (i, j)), out_shape=_sds((t, FF), BF16),
        compiler_params=_params(2))(a, a, a, a, conv, conv, conv_b, conv_b)


def _ffn_down_loss(u, down, x2, target):
    t = x2.shape[0]

    def body(u_ref, w_ref, x_ref, t_ref, dy_ref, loss_ref, acc_ref):
        i = pl.program_id(0)

        @pl.when(i == 0)
        def _():
            acc_ref[...] = jnp.zeros_like(acc_ref)

        err = x_ref[...] + _dot(u_ref[...], w_ref[...]) - t_ref[...]
        dy_ref[...] = err * (1.0 / D)
        acc_ref[...] += jnp.sum(err * err, axis=0, keepdims=True)

        @pl.when(i == pl.num_programs(0) - 1)
        def _():
            loss_ref[...] = jnp.full((8, 128), 0.5 / D, F32) * jnp.sum(acc_ref[...])

    return pl.pallas_call(
        body, name="ffn_down_loss", grid=(t // TM,),
        in_specs=[_rows(TM, FF), _const((FF, D)), _rows(TM, D), _rows(TM, D)],
        out_specs=[_rows(TM, D), _const((8, 128))], out_shape=[_sds((t, D)), _sds((8, 128))],
        scratch_shapes=[pltpu.VMEM((1, D), F32)], compiler_params=_params())(u, down, x2, target)


def _conv_gate_bwd(a, du, conv, conv_b):
    t = a.shape[0]
    nj = FF // CT
    nt = t // CM

    def body(ag_ref, au_ref, pg_ref, pu_ref, ng_ref, nu_ref, du_ref, dun_ref, wg_ref, wu_ref, bg_ref, bu_ref,
             dag_ref, dau_ref, sg_ref, su_ref):
        i = pl.program_id(1)

        @pl.when(i == 0)
        def _():
            sg_ref[...] = jnp.zeros_like(sg_ref)
            su_ref[...] = jnp.zeros_like(su_ref)

        head = (i > 0).astype(F32)
        tail = (i < nt - 1).astype(F32)
        eg = jnp.concatenate([pg_ref[...] * head, ag_ref[...], ng_ref[...] * tail], axis=0)
        eu = jnp.concatenate([pu_ref[...] * head, au_ref[...], nu_ref[...] * tail], axis=0)
        cg = _conv(eg, wg_ref, bg_ref)
        cu = _conv(eu, wu_ref, bu_ref)
        due = jnp.concatenate([du_ref[...], dun_ref[...] * tail], axis=0)
        row = lax.broadcasted_iota(jnp.int32, (8, CT), 0)
        for c_other, e, w_ref, d_ref, s_ref, is_gate in ((cu, eg, wg_ref, dag_ref, sg_ref, True),
                                                          (cg, eu, wu_ref, dau_ref, su_ref, False)):
            dc = due * c_other * _gelu_grad(cg) if is_gate else due * _gelu(c_other)
            n = dc.shape[0]
            da = w_ref[2:3, :] * dc + w_ref[1:2, :] * pltpu.roll(dc, n - 1, 0) + w_ref[0:1, :] * pltpu.roll(dc, n - 2, 0)
            d_ref[...] = da[0:CM].astype(BF16)
            dc0 = dc[0:CM]
            e8 = e[8:8 + CM]
            e7 = pltpu.roll(e, 1, 0)[8:8 + CM]
            e6 = pltpu.roll(e, 2, 0)[8:8 + CM]
            sums = [jnp.sum(dc0 * e6, axis=0, keepdims=True), jnp.sum(dc0 * e7, axis=0, keepdims=True),
                    jnp.sum(dc0 * e8, axis=0, keepdims=True), jnp.sum(dc0, axis=0, keepdims=True)]
            upd = jnp.zeros((8, CT), F32)
            for r, s in enumerate(sums):
                upd = jnp.where(row == r, jnp.broadcast_to(s, (8, CT)), upd)
            s_ref[...] += upd

    r8 = CM // 8
    last8 = t // 8 - 1
    prev = lambda j, i: (jnp.maximum(i * r8 - 1, 0), j)
    prev_u = lambda j, i: (jnp.maximum(i * r8 - 1, 0), j + nj)
    nxt = lambda j, i: (jnp.minimum((i + 1) * r8, last8), j)
    nxt_u = lambda j, i: (jnp.minimum((i + 1) * r8, last8), j + nj)
    return pl.pallas_call(
        body, name="conv_gate_bwd", grid=(nj, nt),
        in_specs=[pl.BlockSpec((CM, CT), lambda j, i: (i, j)), pl.BlockSpec((CM, CT), lambda j, i: (i, j + nj)),
                  pl.BlockSpec((8, CT), prev), pl.BlockSpec((8, CT), prev_u),
                  pl.BlockSpec((8, CT), nxt), pl.BlockSpec((8, CT), nxt_u),
                  pl.BlockSpec((CM, CT), lambda j, i: (i, j)), pl.BlockSpec((8, CT), nxt),
                  pl.BlockSpec((3, CT), lambda j, i: (0, j)), pl.BlockSpec((3, CT), lambda j, i: (0, j + nj)),
                  pl.BlockSpec((1, CT), lambda j, i: (0, j)), pl.BlockSpec((1, CT), lambda j, i: (0, j + nj))],
        out_specs=[pl.BlockSpec((CM, CT), lambda j, i: (i, j)), pl.BlockSpec((CM, CT), lambda j, i: (i, j)),
                   pl.BlockSpec((8, CT), lambda j, i: (0, j)), pl.BlockSpec((8, CT), lambda j, i: (0, j))],
        out_shape=[_sds((t, FF), BF16), _sds((t, FF), BF16), _sds((8, FF)), _sds((8, FF))],
        compiler_params=_params(2))(a, a, a, a, a, a, du, du, conv, conv, conv_b, conv_b)


BT = 256


def _ffn_up_bwd(dag, dau, up, x2, dy, ffn_norm):
    t = x2.shape[0]

    def body(dg_ref, du_ref, w_ref, x_ref, dy_ref, g_ref, dx_ref, dn_ref):
        @pl.when(pl.program_id(0) == 0)
        def _():
            dn_ref[...] = jnp.zeros_like(dn_ref)

        dh = _dot_nt(dg_ref[...], w_ref[:, :FF]) + _dot_nt(du_ref[...], w_ref[:, FF:])
        x = x_ref[...]
        dx, dg = _rms_bwd(dh, x, _rs(x), g_ref[...])
        dx_ref[...] = dy_ref[...] + dx
        dn_ref[...] += _colsum8(dg)

    return pl.pallas_call(
        body, name="ffn_up_bwd", grid=(t // BT,),
        in_specs=[_rows(BT, FF), _rows(BT, FF), _const((D, 2 * FF)), _rows(BT, D), _rows(BT, D), _const((1, D))],
        out_specs=[_rows(BT, D), _const((8, D))], out_shape=[_sds((t, D)), _sds((8, D))],
        compiler_params=_params())(dag, dau, up, x2, dy, ffn_norm)


def _xattn_bwd(dx2, x1, qpre, k2, v2, wq, wo, qn4, xan):
    t = x1.shape[0]

    def body(dx2_ref, x1_ref, qpre_ref, k_ref, v_ref, wq_ref, wo_ref, qn_ref, xan_ref,
             dx1_ref, dqpre_ref, dk_ref, dv_ref, dqn_ref, dxan_ref):
        @pl.when(pl.program_id(0) == 0)
        def _():
            for r in (dk_ref, dv_ref, dqn_ref, dxan_ref):
                r[...] = jnp.zeros_like(r)

        dx2 = dx2_ref[...]
        do = _dot_nt(dx2.astype(BF16), wo_ref[...])
        dqs = []
        for h in range(XH):
            sl = slice(XD * h, XD * (h + 1))
            qpre_h = qpre_ref[:, sl]
            pn, q2, rq = _xattn_probs(qpre_h, qn_ref[:, sl], k_ref[:, sl])
            do_h = do[:, sl].astype(BF16)
            dp = _dot_nt(do_h, v_ref[:, sl])
            ds = (pn * (dp - jnp.sum(pn * dp, axis=-1, keepdims=True)) * (1.0 / math.sqrt(XD))).astype(BF16)
            dq2 = _dot(ds, k_ref[:, sl])
            dk_ref[:, sl] += _dot_tn(ds, q2)
            dv_ref[:, sl] += _dot_tn(pn.astype(BF16), do_h)
            dqh, dg = _rms_bwd(dq2, qpre_h, rq, qn_ref[:, sl])
            dqn_ref[...] += _colsum8(dg)
            dqs.append(dqh)
        dqpre = jnp.concatenate(dqs, axis=1).astype(BF16)
        dqpre_ref[...] = dqpre
        dh2 = _dot_nt(dqpre, wq_ref[...])
        x1 = x1_ref[...]
        dx, dg = _rms_bwd(dh2, x1, _rs(x1), xan_ref[...])
        dx1_ref[...] = dx2 + dx
        dxan_ref[...] += _colsum8(dg)

    return pl.pallas_call(
        body, name="xattn_bwd", grid=(t // BT,),
        in_specs=[_rows(BT, D), _rows(BT, D), _rows(BT, D), _const((MEM, D)), _const((MEM, D)), _const((D, D)),
                  _const((D, D)), _const((1, D)), _const((1, D))],
        out_specs=[_rows(BT, D), _rows(BT, D), _const((MEM, D)), _const((MEM, D)), _const((8, XD)), _const((8, D))],
        out_shape=[_sds((t, D)), _sds((t, D), BF16), _sds((MEM, D)), _sds((MEM, D)), _sds((8, XD)), _sds((8, D))],
        compiler_params=_params())(dx2, x1, qpre, k2, v2, wq, wo, qn4, xan)


def _mem_kv_bwd(mem, mh, kpre, dk2, dv2, wkv, kn4, mem_norm):
    def body(m_ref, mh_ref, kpre_ref, dk_ref, dv_ref, w_ref, kn_ref, g_ref, dw_ref, dkn_ref, dmn_ref):
        dks = []
        dkn = jnp.zeros((8, XD), F32)
        for h in range(XH):
            sl = slice(XD * h, XD * (h + 1))
            kh = kpre_ref[:, sl]
            dkh, dg = _rms_bwd(dk_ref[:, sl], kh, _rs(kh), kn_ref[:, sl])
            dkn = dkn + _colsum8(dg)
            dks.append(dkh)
        dkn_ref[...] = dkn
        dkv = jnp.concatenate(dks + [dv_ref[...]], axis=1).astype(BF16)
        dw_ref[...] = _dot_tn(mh_ref[...], dkv)
        dm = _dot_nt(dkv, w_ref[...])
        m = m_ref[...]
        _, dg = _rms_bwd(dm, m, _rs(m), g_ref[...])
        dmn_ref[...] = _colsum8(dg)

    return pl.pallas_call(
        body, name="mem_kv_bwd", out_shape=[_sds((D, 2 * D)), _sds((8, XD)), _sds((8, D))],
        compiler_params=pltpu.CompilerParams(vmem_limit_bytes=VMEM_LIMIT))(mem, mh, kpre, dk2, dv2, wkv, kn4, mem_norm)


def _mixer_out_bwd(dx1, attn, gm, w_out, aon, gon):
    t = dx1.shape[0]

    def body(dx_ref, a_ref, g_ref, w_ref, aon_ref, gon_ref, da_ref, dg_ref, dan_ref, dgn_ref):
        @pl.when(pl.program_id(0) == 0)
        def _():
            dan_ref[...] = jnp.zeros_like(dan_ref)
            dgn_ref[...] = jnp.zeros_like(dgn_ref)

        dy = _dot_nt(dx_ref[...].astype(BF16), w_ref[...])
        a, g = a_ref[...], g_ref[...]
        da, dna = _rms_bwd(dy[:, :AW], a, _rs(a), aon_ref[...])
        dg, dng = _rms_bwd(dy[:, AW:], g, _rs(g), gon_ref[...])
        da_ref[...] = da
        dg_ref[...] = dg
        dan_ref[...] += _colsum8(dna)
        dgn_ref[...] += _colsum8(dng)

    return pl.pallas_call(
        body, name="mixer_out_bwd", grid=(t // TM,),
        in_specs=[_rows(TM, D), _rows(TM, AW), _rows(TM, GW), _const((D, D)), _const((1, AW)), _const((1, GW))],
        out_specs=[_rows(TM, AW), _rows(TM, GW), _const((8, AW)), _const((8, GW))],
        out_shape=[_sds((t, AW)), _sds((t, GW)), _sds((8, AW)), _sds((8, GW))],
        compiler_params=_params())(dx1, attn, gm, w_out, aon, gon)


def _gmlp_bwd(dgm, gu, gvn, gz, ws, wst, bfull, gvw):
    t = dgm.shape[0]
    nc = TM // BLK

    def body(dgm_ref, gu_ref, x_ref, gz_ref, w_ref, wt_ref, b_ref, gvw_ref, dgz_ref, dw_ref, db_ref, dgvw_ref):
        @pl.when(pl.program_id(0) == 0)
        def _():
            for r in (dw_ref, db_ref, dgvw_ref):
                r[...] = jnp.zeros_like(r)

        xin = x_ref[...]
        dgm = dgm_ref[...]
        mixed = _gmlp_mix(w_ref, xin, False) + jnp.tile(b_ref[...], (nc, 1))
        dgu = dgm * mixed
        dmixed = dgm * gu_ref[...]
        lo = _lane((BLK, 128)) < 64
        dbias = jnp.zeros((BLK, GW), F32)
        for c in range(nc):
            dmc = dmixed[c * BLK:(c + 1) * BLK]
            dbias = dbias + dmc
            for j in range(4):
                dm2 = dmc[:, 128 * j:128 * (j + 1)]
                xs = xin[c * BLK:(c + 1) * BLK, 128 * j:128 * (j + 1)]
                z = jnp.zeros_like(dm2)
                dw_ref[2 * j] += _dot_nt(jnp.where(lo, dm2, z).astype(BF16), xs)
                dw_ref[2 * j + 1] += _dot_nt(jnp.where(lo, z, dm2).astype(BF16), xs)
        db_ref[...] += dbias
        dgvn = _gmlp_mix(wt_ref, dmixed.astype(BF16), True)
        gz_u, gz_v = gz_ref[:, :GW], gz_ref[:, GW:]
        gv = _gelu(gz_v)
        dgv, dg = _rms_bwd(dgvn, gv, _rs(gv), gvw_ref[...])
        dgvw_ref[...] += _colsum8(dg)
        dgz_ref[:, :GW] = (dgu * _gelu_grad(gz_u)).astype(BF16)
        dgz_ref[:, GW:] = (dgv * _gelu_grad(gz_v)).astype(BF16)

        @pl.when(pl.program_id(0) == pl.num_programs(0) - 1)
        def _():
            db_ref[...] = _gsum64(db_ref[...])
            r = lax.broadcasted_iota(jnp.int32, (BLK, BLK), 0)
            c = lax.broadcasted_iota(jnp.int32, (BLK, BLK), 1)
            for h in range(8):
                dw_ref[h] = jnp.where(c <= r, dw_ref[h], 0.0)

    return pl.pallas_call(
        body, name="gmlp_bwd", grid=(t // TM,),
        in_specs=[_rows(TM, GW), _rows(TM, GW), _rows(TM, GW), _rows(TM, 2 * GW), _const((8, BLK, BLK)),
                  _const((8, BLK, BLK)), _const((BLK, GW)), _const((1, GW))],
        out_specs=[_rows(TM, 2 * GW), _const((8, BLK, BLK)), _const((BLK, GW)), _const((8, GW))],
        out_shape=[_sds((t, 2 * GW), BF16), _sds((8, BLK, BLK)), _sds((BLK, GW)), _sds((8, GW))],
        compiler_params=_params())(dgm, gu, gvn, gz, ws, wst, bfull, gvw)


def _fold_half(v):
    return v + pltpu.roll(v, 64, 1)


def _swa_bwd(q, k, v, dattn, sinks):
    t = q.shape[0]
    nb = t // BLK

    def body(sink_ref, q_ref, kc_ref, kp_ref, vc_ref, vp_ref, do_ref, dq_ref, dk_ref, dv_ref, ds_ref,
             ck_ref, cv_ref, sacc_ref):
        i = pl.program_id(0)

        @pl.when(i == 0)
        def _():
            ck_ref[...] = jnp.zeros_like(ck_ref)
            cv_ref[...] = jnp.zeros_like(cv_ref)
            sacc_ref[...] = jnp.zeros_like(sacc_ref)

        @pl.when(i < nb)
        def _():
            mask = _swa_mask(i == 0)
            kk = jnp.concatenate([kp_ref[...], kc_ref[...]], axis=0).astype(F32)
            vv = jnp.concatenate([vp_ref[...], vc_ref[...]], axis=0).astype(F32)
            lo256 = _lane((2 * BLK, 128)) < 64
            dkk = jnp.zeros((2 * BLK, 128), F32)
            dvv = jnp.zeros((2 * BLK, 128), F32)
            for g in range(2):
                qs = _stack_heads(q_ref[:, 256 * g:256 * g + 128], q_ref[:, 256 * g + 128:256 * g + 256])
                dos = _stack_heads(do_ref[:, 256 * g:256 * g + 128],
                                   do_ref[:, 256 * g + 128:256 * g + 256]).astype(BF16)
                kd = _dup_half(kk, g)
                pn, psn = _swa_probs(qs, kd, mask, _sink_col(sink_ref, g))
                dp = _dot_nt(dos, _dup_half(vv, g))
                dd = jnp.sum(pn * dp, axis=-1, keepdims=True)
                ds = (pn * (dp - dd) * (1.0 / math.sqrt(HD))).astype(BF16)
                sacc_ref[g] += jnp.broadcast_to(-psn * dd, (4 * BLK, 128))
                dqa, dqb = _unstack_heads(_dot(ds, kd))
                dq_ref[:, 256 * g:256 * g + 128] = dqa
                dq_ref[:, 256 * g + 128:256 * g + 256] = dqb
                dkg = _fold_half(_dot_tn(ds, qs))
                dvg = _fold_half(_dot_tn(pn.astype(BF16), dos))
                keep = lo256 if g == 0 else jnp.logical_not(lo256)
                dkk = jnp.where(keep, dkg, dkk)
                dvv = jnp.where(keep, dvg, dvv)
            dk_ref[...] = ck_ref[...] + dkk[0:BLK]
            dv_ref[...] = cv_ref[...] + dvv[0:BLK]
            ck_ref[...] = dkk[BLK:]
            cv_ref[...] = dvv[BLK:]

        @pl.when(i == nb)
        def _():
            dk_ref[...] = ck_ref[...]
            dv_ref[...] = cv_ref[...]
            for g in range(2):
                for j in range(4):
                    ds_ref[4 * g + j:4 * g + j + 1, :] = jnp.sum(sacc_ref[g, j * BLK:(j + 1) * BLK, :], axis=0,
                                                                 keepdims=True)

    cur = lambda i: (jnp.minimum(i, nb - 1), 0)
    prev = lambda i: (jnp.clip(i - 1, 0, nb - 1), 0)
    return pl.pallas_call(
        body, name="swa_bwd", grid=(nb + 1,),
        in_specs=[pl.BlockSpec(memory_space=pltpu.SMEM), pl.BlockSpec((BLK, AW), cur),
                  pl.BlockSpec((BLK, KW), cur), pl.BlockSpec((BLK, KW), prev),
                  pl.BlockSpec((BLK, KW), cur), pl.BlockSpec((BLK, KW), prev), pl.BlockSpec((BLK, AW), cur)],
        out_specs=[pl.BlockSpec((BLK, AW), cur), pl.BlockSpec((BLK, KW), prev), pl.BlockSpec((BLK, KW), prev),
                   _const((8, 128))],
        out_shape=[_sds((t, AW)), _sds((t, KW)), _sds((t, KW)), _sds((8, 128))],
        scratch_shapes=[pltpu.VMEM((BLK, KW), F32), pltpu.VMEM((BLK, KW), F32), pltpu.VMEM((2, 4 * BLK, 128), F32)],
        compiler_params=_params())(sinks, q, k, k, v, v, dattn)


def _mixer_in_bwd(dq, dk, dv, dgz, qk, cos, sin, x, dx1, w_in, mix_norm, qn, kn):
    t = x.shape[0]

    def body(dq_ref, dk_ref, dv_ref, dgz_ref, qk_ref, cos_ref, sin_ref, x_ref, dx1_ref, w_ref, g_ref, qn_ref, kn_ref,
             gx_ref, dproj_ref, dmn_ref, dqn_ref, dkn_ref, qacc_ref, kacc_ref):
        i = pl.program_id(0)

        @pl.when(i == 0)
        def _():
            dmn_ref[...] = jnp.zeros_like(dmn_ref)
            qacc_ref[...] = jnp.zeros_like(qacc_ref)
            kacc_ref[...] = jnp.zeros_like(kacc_ref)

        cos2, sin2 = cos_ref[...], sin_ref[...]
        qpre, kpre = qk_ref[:, :AW], qk_ref[:, AW:]
        dqh = _rope_bwd(dq_ref[...], jnp.tile(cos2, (1, 4)), jnp.tile(sin2, (1, 4)))
        dqpre, dgq = _rms64_bwd(dqh, qpre, _rs64(qpre), qn_ref[...])
        dkh = _rope_bwd(dk_ref[...], cos2, sin2)
        dkpre, dgk = _rms64_bwd(dkh, kpre, _rs64(kpre), kn_ref[...])
        qacc_ref[...] += jnp.sum(dgq, axis=0, keepdims=True)
        kacc_ref[...] += jnp.sum(dgk, axis=0, keepdims=True)
        dproj = jnp.concatenate([dqpre.astype(BF16), dkpre.astype(BF16), dv_ref[...].astype(BF16), dgz_ref[...]], axis=1)
        dproj_ref[...] = dproj
        dh = _dot_nt(dproj, w_ref[...])
        xv = x_ref[...]
        dx, dg = _rms_bwd(dh, xv, _rs(xv), g_ref[...])
        gx_ref[...] = dx1_ref[...] + dx
        dmn_ref[...] += _colsum8(dg)

        @pl.when(i == pl.num_programs(0) - 1)
        def _():
            qa = qacc_ref[...]
            q4 = qa[:, 0:128] + qa[:, 128:256] + qa[:, 256:384] + qa[:, 384:512]
            dqn_ref[...] = jnp.broadcast_to(_fold_half(q4), (8, 128))
            dkn_ref[...] = jnp.broadcast_to(_fold_half(kacc_ref[...]), (8, 128))

    return pl.pallas_call(
        body, name="mixer_in_bwd", grid=(t // TM,),
        in_specs=[_rows(TM, AW), _rows(TM, KW), _rows(TM, KW), _rows(TM, 2 * GW), _rows(TM, AW + KW), _rows(TM, 128),
                  _rows(TM, 128), _rows(TM, D), _rows(TM, D), _const((D, IN)), _const((1, D)), _const((1, AW)),
                  _const((1, KW))],
        out_specs=[_rows(TM, D), _rows(TM, IN), _const((8, D)), _const((8, 128)), _const((8, 128))],
        out_shape=[_sds((t, D)), _sds((t, IN), BF16), _sds((8, D)), _sds((8, 128)), _sds((8, 128))],
        scratch_shapes=[pltpu.VMEM((1, AW), F32), pltpu.VMEM((1, KW), F32)],
        compiler_params=_params())(dq, dk, dv, dgz, qk, cos, sin, x, dx1, w_in, mix_norm, qn, kn)


def _local_step(x, mem, pos, target, p):
    inv_freq = 1.0 / (ROPE_THETA ** (jnp.arange(HD // 2, dtype=F32) * (2.0 / HD)))
    cos, sin = _rope_tables(pos, jnp.tile(inv_freq, 4).reshape(1, 128))
    qn = jnp.tile(p["q_norm"], (1, AW // HD))
    kn = jnp.tile(p["k_norm"], (1, KW // HD))
    qn4 = jnp.tile(p["xa_q_norm"], (1, XH))
    kn4 = jnp.tile(p["xa_k_norm"], (1, XH))
    ws = p["gmlp_ws"]
    wst = jnp.swapaxes(ws, 1, 2)
    bfull = jnp.repeat(p["gmlp_bs"].T, HD, axis=1)
    conv, conv_b = p["ffn_conv"], p["ffn_conv_b"]

    h1, qk, gz, q, k, v, gu, gvn = _mixer_in_fwd(x, p["mix_norm"], p["w_in"], qn, kn, p["gmlp_v_norm"], cos, sin)
    attn = _swa_fwd(q, k, v, p["attn_sinks"])
    gm = _gmlp_fwd(gvn, gu, ws, bfull)
    ycat, x1, h2 = _mixer_out_fwd(attn, gm, x, p["w_out"], p["attn_out_norm"], p["gmlp_out_norm"], p["xa_norm"])
    mh, kpre, k2, v2 = _mem_kv_fwd(mem, p["mem_norm"], p["xa_wkv"], kn4)
    qpre, o, x2, h3 = _xattn_fwd(h2, x1, p["xa_wq"], qn4, k2, v2, p["xa_wo"], p["ffn_norm"])
    a = _mm_nn(h3, p["ffn_up"], 2 * FF // 4, "ffn_up_fwd")
    u = _conv_gate_fwd(a, conv, conv_b)
    dy, loss8 = _ffn_down_loss(u, p["ffn_down"], x2, target)

    g = {}
    du = _mm_nt(dy, p["ffn_down"], "ffn_down_bwd_x")
    g["ffn_down"] = _mm_tn(u, dy, D, "ffn_down_bwd_w")
    dag, dau, sg, su = _conv_gate_bwd(a, du, conv, conv_b)
    g["ffn_conv"] = jnp.concatenate([sg[0:3], su[0:3]], axis=1)
    g["ffn_conv_b"] = jnp.concatenate([sg[3:4], su[3:4]], axis=1)
    dx2, dfn = _ffn_up_bwd(dag, dau, p["ffn_up"], x2, dy, p["ffn_norm"])
    g["ffn_norm"] = dfn[0:1]
    g["ffn_up"] = jnp.concatenate([_mm_tn(h3, dag, FF // 2, "ffn_up_bwd_wg"), _mm_tn(h3, dau, FF // 2, "ffn_up_bwd_wu")],
                                  axis=1)
    dx1, dqpre, dk2, dv2, dqn4, dxan = _xattn_bwd(dx2, x1, qpre, k2, v2, p["xa_wq"], p["xa_wo"], qn4, p["xa_norm"])
    g["xa_norm"] = dxan[0:1]
    g["xa_q_norm"] = dqn4[0:1]
    g["xa_wo"] = _mm_tn(o, dx2, D, "xa_wo_bwd_w")
    g["xa_wq"] = _mm_tn(h2, dqpre, D, "xa_wq_bwd_w")
    g["xa_wkv"], dkn4, dmn = _mem_kv_bwd(mem, mh, kpre, dk2, dv2, p["xa_wkv"], kn4, p["mem_norm"])
    g["xa_k_norm"] = dkn4[0:1]
    g["mem_norm"] = dmn[0:1]
    dattn, dgm, dan, dgn = _mixer_out_bwd(dx1, attn, gm, p["w_out"], p["attn_out_norm"], p["gmlp_out_norm"])
    g["attn_out_norm"] = dan[0:1]
    g["gmlp_out_norm"] = dgn[0:1]
    g["w_out"] = _mm_tn(ycat, dx1, D, "w_out_bwd_w")
    dgz, dws, dbfull, dgvw = _gmlp_bwd(dgm, gu, gvn, gz, ws, wst, bfull, p["gmlp_v_norm"])
    g["gmlp_ws"] = dws
    g["gmlp_bs"] = dbfull[:, ::HD].T
    g["gmlp_v_norm"] = dgvw[0:1]
    dq, dk, dv, dsink = _swa_bwd(q, k, v, dattn, p["attn_sinks"])
    g["attn_sinks"] = dsink[:, 0].reshape(1, 8)
    grad_x, dproj, dmix, dqn, dkn = _mixer_in_bwd(dq, dk, dv, dgz, qk, cos, sin, x, dx1, p["w_in"], p["mix_norm"], qn, kn)
    g["mix_norm"] = dmix[0:1]
    g["q_norm"] = dqn[0:1, :HD]
    g["k_norm"] = dkn[0:1, :HD]
    g["w_in"] = _mm_tn(h1, dproj, IN // 2, "w_in_bwd_w")
    return loss8[0, 0], grad_x, g


def _exchange(src, name):
    bcast = src.ndim == 2
    r, c = src.shape[-2:]

    def body(src_ref, out_ref, send_sems, recv_sems, local_sem):
        x, y, cc = lax.axis_index("x"), lax.axis_index("y"), lax.axis_index("c")
        me = 4 * x + 2 * y + cc
        copies = []
        for k in range(1, NDEV):
            px = 1 - x if k & 4 else x
            py = 1 - y if k & 2 else y
            pc = 1 - cc if k & 1 else cc
            block = src_ref if bcast else src_ref.at[4 * px + 2 * py + pc]
            cp = pltpu.make_async_remote_copy(
                src_ref=block, dst_ref=out_ref.at[me], send_sem=send_sems.at[k - 1], recv_sem=recv_sems.at[k - 1],
                device_id=(px, py, pc), device_id_type=pl.DeviceIdType.MESH)
            cp.start()
            copies.append(cp)
        own = pltpu.make_async_copy(src_ref if bcast else src_ref.at[me], out_ref.at[me], local_sem)
        own.start()
        for cp in copies:
            cp.wait()
        own.wait()

    return pl.pallas_call(
        body, name=name, out_shape=_sds((NDEV, r, c), src.dtype),
        in_specs=[pl.BlockSpec(memory_space=pl.ANY)], out_specs=pl.BlockSpec(memory_space=pl.ANY),
        scratch_shapes=[pltpu.SemaphoreType.DMA((NDEV - 1,)), pltpu.SemaphoreType.DMA((NDEV - 1,)),
                        pltpu.SemaphoreType.DMA(())])(src)


def _adam(parts, w, m, v, name):
    def body(p_ref, w_ref, m_ref, v_ref, g_ref, d_ref, nm_ref, nv_ref):
        g = p_ref[0].astype(F32)
        for j in range(1, NDEV):
            g = g + p_ref[j].astype(F32)
        g_ref[...] = g
        nm = B1 * m_ref[...] + (1.0 - B1) * g
        nv = B2 * v_ref[...] + (1.0 - B2) * (g * g)
        nm_ref[...] = nm
        nv_ref[...] = nv
        m_hat = nm / (1.0 - B1 ** STEP)
        v_hat = nv / (1.0 - B2 ** STEP)
        d_ref[...] = -LR * (m_hat / (jnp.sqrt(v_hat) + AEPS) + WD * w_ref[...])

    return pl.pallas_call(
        body, name=name, out_shape=[_sds(w.shape)] * 4,
        compiler_params=pltpu.CompilerParams(vmem_limit_bytes=VMEM_LIMIT))(parts, w, m, v)


BIG = (("w_in", 224, True), ("w_out", 128, False), ("xa_wq", 128, False), ("xa_wkv", 256, True),
       ("xa_wo", 128, False), ("ffn_up", 704, True), ("ffn_down", 352, False))
CONV_ROWS = 16
CONV_N = 3 * 704
SMALL = ("mix_norm", "q_norm", "k_norm", "attn_sinks", "gmlp_v_norm", "gmlp_bs", "attn_out_norm", "gmlp_out_norm",
         "xa_norm", "mem_norm", "xa_q_norm", "xa_k_norm", "ffn_norm", "ffn_conv_b", "gmlp_ws")


def _bits_rows(a):
    lead = a.shape[:-2]
    b = lax.bitcast_convert_type(a, BF16).reshape(lead + (2 * CONV_N,))
    b = jnp.pad(b, [(0, 0)] * len(lead) + [(0, CONV_ROWS * D - 2 * CONV_N)])
    return b.reshape(lead + (CONV_ROWS, D))


def _rows_bits(b):
    lead = b.shape[:-2]
    b = b.reshape(lead + (CONV_ROWS * D,))[..., :2 * CONV_N].reshape(lead + (3, 704, 2))
    return lax.bitcast_convert_type(b, F32)


def _pack_small(d):
    rows = []
    for n in SMALL:
        flat = d[n].reshape(-1)
        rows.append(jnp.pad(flat, (0, -flat.shape[0] % D)).reshape(-1, D))
    buf = jnp.concatenate(rows, axis=0)
    return jnp.pad(buf, ((0, -buf.shape[0] % 8), (0, 0)))


def _unpack_small(buf, shapes):
    out, r = {}, 0
    for n in SMALL:
        size = math.prod(shapes[n])
        nr = -(-size // D)
        out[n] = buf[r:r + nr].reshape(-1)[:size].reshape(shapes[n])
        r += nr
    return out


def kernel(x, mem, positions, mix_norm, w_in, q_norm, k_norm, attn_sinks, gmlp_v_norm, gmlp_ws, gmlp_bs, attn_out_norm, gmlp_out_norm, w_out, xa_norm, mem_norm, xa_wq, xa_wkv, xa_q_norm, xa_k_norm, xa_wo, ffn_norm, ffn_up, ffn_conv, ffn_conv_b, ffn_down, loss_target, m_mix_norm, m_w_in, m_q_norm, m_k_norm, m_attn_sinks, m_gmlp_v_norm, m_gmlp_ws, m_gmlp_bs, m_attn_out_norm, m_gmlp_out_norm, m_w_out, m_xa_norm, m_mem_norm, m_xa_wq, m_xa_wkv, m_xa_q_norm, m_xa_k_norm, m_xa_wo, m_ffn_norm, m_ffn_up, m_ffn_conv, m_ffn_conv_b, m_ffn_down, v_mix_norm, v_w_in, v_q_norm, v_k_norm, v_attn_sinks, v_gmlp_v_norm, v_gmlp_ws, v_gmlp_bs, v_attn_out_norm, v_gmlp_out_norm, v_w_out, v_xa_norm, v_mem_norm, v_xa_wq, v_xa_wkv, v_xa_q_norm, v_xa_k_norm, v_xa_wo, v_ffn_norm, v_ffn_up, v_ffn_conv, v_ffn_conv_b, v_ffn_down):
    names = ("mix_norm", "w_in", "q_norm", "k_norm", "attn_sinks", "gmlp_v_norm", "gmlp_ws", "gmlp_bs", "attn_out_norm",
             "gmlp_out_norm", "w_out", "xa_norm", "mem_norm", "xa_wq", "xa_wkv", "xa_q_norm", "xa_k_norm", "xa_wo",
             "ffn_norm", "ffn_up", "ffn_conv", "ffn_conv_b", "ffn_down")
    w = dict(zip(names, (mix_norm, w_in, q_norm, k_norm, attn_sinks, gmlp_v_norm, gmlp_ws, gmlp_bs, attn_out_norm,
                         gmlp_out_norm, w_out, xa_norm, mem_norm, xa_wq, xa_wkv, xa_q_norm, xa_k_norm, xa_wo, ffn_norm,
                         ffn_up, ffn_conv, ffn_conv_b, ffn_down)))
    m = dict(zip(names, (m_mix_norm, m_w_in, m_q_norm, m_k_norm, m_attn_sinks, m_gmlp_v_norm, m_gmlp_ws, m_gmlp_bs,
                         m_attn_out_norm, m_gmlp_out_norm, m_w_out, m_xa_norm, m_mem_norm, m_xa_wq, m_xa_wkv,
                         m_xa_q_norm, m_xa_k_norm, m_xa_wo, m_ffn_norm, m_ffn_up, m_ffn_conv, m_ffn_conv_b, m_ffn_down)))
    v = dict(zip(names, (v_mix_norm, v_w_in, v_q_norm, v_k_norm, v_attn_sinks, v_gmlp_v_norm, v_gmlp_ws, v_gmlp_bs,
                         v_attn_out_norm, v_gmlp_out_norm, v_w_out, v_xa_norm, v_mem_norm, v_xa_wq, v_xa_wkv,
                         v_xa_q_norm, v_xa_k_norm, v_xa_wo, v_ffn_norm, v_ffn_up, v_ffn_conv, v_ffn_conv_b, v_ffn_down)))
    t = x.shape[1]

    mine = [w[n][0].astype(BF16).reshape(rows, D) for n, rows, _ in BIG] + [_bits_rows(w["ffn_conv"][0])]
    got = _exchange(jnp.concatenate(mine, axis=0), "gather_weights")
    p, r = {}, 0
    for n, rows, by_col in BIG:
        blk = got[:, r:r + rows]
        if by_col:
            p[n] = blk.reshape(NDEV, D, rows).transpose(1, 0, 2).reshape(D, NDEV * rows)
        else:
            p[n] = blk.reshape(NDEV * rows, D)
        r += rows
    p["ffn_conv"] = _rows_bits(got[:, r:r + CONV_ROWS]).transpose(1, 0, 2).reshape(3, NDEV * 704)
    for n in SMALL:
        p[n] = w[n] if n == "gmlp_ws" or n == "gmlp_bs" else w[n].reshape(1, -1)
    p["gmlp_ws"], p["gmlp_bs"] = w["gmlp_ws"][0], w["gmlp_bs"][0]

    loss, grad_x, g = _local_step(x[0], mem[0], positions.reshape(t, 1), loss_target[0], p)
    loss = lax.psum(loss, AXES)

    send = []
    for n, rows, by_col in BIG:
        gn = g[n].astype(BF16)
        if by_col:
            gn = gn.reshape(D, NDEV, rows).transpose(1, 0, 2)
        send.append(gn.reshape(NDEV, rows, D))
    send.append(_bits_rows(g["ffn_conv"].reshape(3, NDEV, 704).transpose(1, 0, 2)))
    parts = _exchange(jnp.concatenate(send, axis=1), "scatter_grads")
    small_parts = _exchange(_pack_small(g), "share_small_grads")

    res, r = {}, 0
    for n, rows, by_col in BIG:
        shp = w[n].shape[1:]
        res[n] = _adam(parts[:, r:r + rows].reshape((NDEV,) + shp), w[n][0], m[n][0], v[n][0], "adam_" + n)
        r += rows
    res["ffn_conv"] = _adam(_rows_bits(parts[:, r:r + CONV_ROWS]), w["ffn_conv"][0], m["ffn_conv"][0],
                            v["ffn_conv"][0], "adam_ffn_conv")
    small = _adam(small_parts, _pack_small(w), _pack_small(m), _pack_small(v), "adam_small")
    shapes = {n: w[n].shape for n in SMALL}
    small = [_unpack_small(b, shapes) for b in small]
    for n in SMALL:
        res[n] = tuple(b[n] for b in small)

    outs = [loss, grad_x[None]]
    for j in range(4):
        outs += [res[n][j].reshape(w[n].shape) for n in names]
    return tuple(outs)
```

```python
import functools
import math

import jax
import jax.numpy as jnp
from jax import lax
from jax.experimental import pallas as pl
from jax.experimental.pallas import tpu as pltpu

F32 = jnp.float32
BF16 = jnp.bfloat16

D = 1024
HD = 64
AW = 512
KW = 128
GW = 512
IN = AW + 2 * KW + 2 * GW
BLK = 128
MEM = 256
XH = 4
XD = 256
FF = 2816
EPS = 1e-6
ROPE_THETA = 10000.0
NDEV = 8
LR, B1, B2, AEPS, WD, STEP = 0.001, 0.9, 0.999, 1e-08, 0.01, 10

TM = 512
VMEM_LIMIT = 56 * 1024 * 1024
NEG = float(jnp.finfo(jnp.float32).min)
GELU_C0 = math.sqrt(2.0 / math.pi)
GELU_C1 = 0.044715
AXES = ("x", "y", "c")


def _dot(a, b):
    return jnp.dot(a, b, preferred_element_type=F32)


def _dot_nt(a, b):
    return lax.dot_general(a, b, (((1,), (1,)), ((), ())), preferred_element_type=F32)


def _dot_tn(a, b):
    return lax.dot_general(a, b, (((0,), (0,)), ((), ())), preferred_element_type=F32)


def _rs(x):
    return lax.rsqrt(jnp.mean(x * x, axis=-1, keepdims=True) + EPS)


def _rms_bwd(dy, x, r, g):
    xh = x * r
    dxh = dy * g
    dx = r * (dxh - xh * jnp.mean(dxh * xh, axis=-1, keepdims=True))
    return dx, dy * xh


def _lane(shape):
    return lax.broadcasted_iota(jnp.int32, shape, len(shape) - 1)


def _gsum64(v):
    w = v.shape[-1]
    lane = _lane(v.shape)
    s = v
    for sh in (1, 2, 4, 8, 16, 32):
        s = s + jnp.where((lane & sh) != 0, pltpu.roll(s, sh, 1), pltpu.roll(s, w - sh, 1))
    return s


def _rs64(x):
    return lax.rsqrt(_gsum64(x * x) * (1.0 / HD) + EPS)


def _rms64_bwd(dy, x, r, g):
    xh = x * r
    dxh = dy * g
    dx = r * (dxh - xh * (_gsum64(dxh * xh) * (1.0 / HD)))
    return dx, dy * xh


def _rot_half(v):
    w = v.shape[-1]
    return jnp.where((_lane(v.shape) & 32) == 0, pltpu.roll(v, w - 32, 1), pltpu.roll(v, 32, 1))


def _rope(v, cos, sin_signed):
    return v * cos + _rot_half(v) * sin_signed


def _rope_bwd(dv, cos, sin_signed):
    return dv * cos + _rot_half(dv * sin_signed)


def _gelu(z):
    return 0.5 * z * (1.0 + jnp.tanh(GELU_C0 * (z + GELU_C1 * z * z * z)))


def _gelu_grad(z):
    t = jnp.tanh(GELU_C0 * (z + GELU_C1 * z * z * z))
    return 0.5 * (1.0 + t) + 0.5 * z * (1.0 - t * t) * (GELU_C0 * (1.0 + 3.0 * GELU_C1 * z * z))


def _colsum8(v):
    s = jnp.sum(v, axis=0, keepdims=True)
    row = lax.broadcasted_iota(jnp.int32, (8, v.shape[1]), 0)
    return jnp.where(row == 0, jnp.broadcast_to(s, (8, v.shape[1])), 0.0)


def _params(n_axes=1):
    return pltpu.CompilerParams(dimension_semantics=("arbitrary",) * n_axes, vmem_limit_bytes=VMEM_LIMIT)


def _rows(tm, w):
    return pl.BlockSpec((tm, w), lambda i: (i, 0))


def _const(shape):
    nd = len(shape)
    return pl.BlockSpec(shape, lambda *_: (0,) * nd)


def _sds(shape, dtype=F32):
    return jax.ShapeDtypeStruct(shape, dtype)


def _mm_blocks(a, b, name):
    t, k = a.shape
    g, _, n = b.shape

    def body(a_ref, b_ref, o_ref):
        o_ref[...] = _dot(a_ref[...], b_ref[...])

    return pl.pallas_call(
        body, name=name, grid=(g, t // TM),
        in_specs=[pl.BlockSpec((TM, k), lambda j, i: (i, 0)), pl.BlockSpec((None, k, n), lambda j, i: (j, 0, 0))],
        out_specs=pl.BlockSpec((None, TM, n), lambda j, i: (j, i, 0)),
        out_shape=_sds((g, t, n)), compiler_params=_params(2))(a, b)


def _mm_tn(a, b, name):
    g = max(a.shape[0] if a.ndim == 3 else 1, b.shape[0] if b.ndim == 3 else 1)
    t, m = a.shape[-2:]
    n = b.shape[-1]

    def body(a_ref, b_ref, o_ref, acc_ref):
        i = pl.program_id(1)

        @pl.when(i == 0)
        def _():
            acc_ref[...] = jnp.zeros_like(acc_ref)

        acc_ref[...] += _dot_tn(a_ref[...].astype(BF16), b_ref[...].astype(BF16))

        @pl.when(i == pl.num_programs(1) - 1)
        def _():
            o_ref[...] = acc_ref[...].astype(BF16)

    def spec(v):
        w = v.shape[-1]
        if v.ndim == 3:
            return pl.BlockSpec((None, TM, w), lambda j, i: (j, i, 0))
        return pl.BlockSpec((TM, w), lambda j, i: (i, 0))

    return pl.pallas_call(
        body, name=name, grid=(g, t // TM), in_specs=[spec(a), spec(b)],
        out_specs=pl.BlockSpec((None, m, n), lambda j, i: (j, 0, 0)), out_shape=_sds((g, m, n), BF16),
        scratch_shapes=[pltpu.VMEM((m, n), F32)], compiler_params=_params(2))(a, b)


def _rope_tables(pos, inv_freq):
    t = pos.shape[0]

    def body(pos_ref, f_ref, cos_ref, sin_ref):
        ang = pos_ref[...].astype(F32) * f_ref[...]
        sign = jnp.where((_lane(ang.shape) & 32) == 0, -1.0, 1.0)
        cos_ref[...] = jnp.cos(ang)
        sin_ref[...] = jnp.sin(ang) * sign

    return pl.pallas_call(
        body, name="rope_tables", grid=(t // TM,),
        in_specs=[_rows(TM, 1), _const((1, 128))], out_specs=[_rows(TM, 128), _rows(TM, 128)],
        out_shape=[_sds((t, 128)), _sds((t, 128))], compiler_params=_params())(pos, inv_freq)


def _mixer_in_fwd(x, mix_norm, w_in, qn, kn, gvw, cos, sin):
    t = x.shape[0]

    def body(x_ref, g_ref, w_ref, qn_ref, kn_ref, gvw_ref, cos_ref, sin_ref,
             h_ref, qk_ref, gz_ref, q_ref, k_ref, v_ref, gu_ref, gvn_ref):
        x = x_ref[...]
        h = (x * _rs(x) * g_ref[...]).astype(BF16)
        h_ref[...] = h
        proj = _dot(h, w_ref[...])
        qk = proj[:, :AW + KW]
        qk_ref[...] = qk
        gz = proj[:, AW + 2 * KW:]
        gz_ref[...] = gz
        cos2, sin2 = cos_ref[...], sin_ref[...]
        q = qk[:, :AW]
        q = q * _rs64(q) * qn_ref[...]
        q_ref[...] = _rope(q, jnp.tile(cos2, (1, 4)), jnp.tile(sin2, (1, 4))).astype(BF16)
        k = qk[:, AW:]
        k = k * _rs64(k) * kn_ref[...]
        k_ref[...] = _rope(k, cos2, sin2).astype(BF16)
        v_ref[...] = proj[:, AW + KW:AW + 2 * KW].astype(BF16)
        gu_ref[...] = _gelu(gz[:, :GW])
        gv = _gelu(gz[:, GW:])
        gvn_ref[...] = (gv * _rs(gv) * gvw_ref[...]).astype(BF16)

    return pl.pallas_call(
        body, name="mixer_in_fwd", grid=(t // TM,),
        in_specs=[_rows(TM, D), _const((1, D)), _const((D, IN)), _const((1, AW)), _const((1, KW)),
                  _const((1, GW)), _rows(TM, 128), _rows(TM, 128)],
        out_specs=[_rows(TM, D), _rows(TM, AW + KW), _rows(TM, 2 * GW), _rows(TM, AW), _rows(TM, KW),
                   _rows(TM, KW), _rows(TM, GW), _rows(TM, GW)],
        out_shape=[_sds((t, D), BF16), _sds((t, AW + KW)), _sds((t, 2 * GW)), _sds((t, AW), BF16),
                   _sds((t, KW), BF16), _sds((t, KW), BF16), _sds((t, GW)), _sds((t, GW), BF16)],
        compiler_params=_params())(x, mix_norm, w_in, qn, kn, gvw, cos, sin)


def _dup_half(kk, g):
    lane = _lane(kk.shape)
    other = pltpu.roll(kk, 64, 1)
    keep = (lane < 64) if g == 0 else (lane >= 64)
    return jnp.where(keep, kk, other).astype(BF16)


def _swa_mask(first_block):
    qi = lax.broadcasted_iota(jnp.int32, (4 * BLK, 2 * BLK), 0) & (BLK - 1)
    kj = lax.broadcasted_iota(jnp.int32, (4 * BLK, 2 * BLK), 1)
    diff = qi + BLK - kj
    band = (diff >= 0) & (diff < BLK)
    return band & (jnp.logical_not(first_block) | (kj >= BLK))


def _stack_heads(a2, b2):
    lo = _lane(a2.shape) < 64
    z = jnp.zeros_like(a2)
    return jnp.concatenate([jnp.where(lo, a2, z), jnp.where(lo, z, a2), jnp.where(lo, b2, z), jnp.where(lo, z, b2)], axis=0)


def _unstack_heads(o):
    lo = _lane((BLK, 128)) < 64
    return jnp.where(lo, o[0:BLK], o[BLK:2 * BLK]), jnp.where(lo, o[2 * BLK:3 * BLK], o[3 * BLK:4 * BLK])


def _sink_col(sink_ref, g):
    row = lax.broadcasted_iota(jnp.int32, (4 * BLK, 1), 0)
    s = [sink_ref[0, 4 * g + j] for j in range(4)]
    return jnp.where(row < BLK, s[0], jnp.where(row < 2 * BLK, s[1], jnp.where(row < 3 * BLK, s[2], s[3])))


def _swa_probs(qs, kd, mask, sink):
    s = _dot_nt(qs, kd) * (1.0 / math.sqrt(HD))
    s = jnp.where(mask, s, NEG)
    m = jnp.maximum(jnp.max(s, axis=-1, keepdims=True), sink)
    p = jnp.exp(s - m)
    ps = jnp.exp(sink - m)
    inv = 1.0 / (jnp.sum(p, axis=-1, keepdims=True) + ps)
    return p * inv, ps * inv


def _swa_fwd(q, k, v, sinks):
    t = q.shape[0]
    nb = t // BLK

    def body(sink_ref, q_ref, kc_ref, kp_ref, vc_ref, vp_ref, o_ref):
        i = pl.program_id(0)
        mask = _swa_mask(i == 0)
        kk = jnp.concatenate([kp_ref[...], kc_ref[...]], axis=0).astype(F32)
        vv = jnp.concatenate([vp_ref[...], vc_ref[...]], axis=0).astype(F32)
        for g in range(2):
            qs = _stack_heads(q_ref[:, 256 * g:256 * g + 128], q_ref[:, 256 * g + 128:256 * g + 256])
            pn, _ = _swa_probs(qs, _dup_half(kk, g), mask, _sink_col(sink_ref, g))
            oa, ob = _unstack_heads(_dot(pn.astype(BF16), _dup_half(vv, g)))
            o_ref[:, 256 * g:256 * g + 128] = oa
            o_ref[:, 256 * g + 128:256 * g + 256] = ob

    cur = lambda i: (i, 0)
    prev = lambda i: (jnp.maximum(i - 1, 0), 0)
    return pl.pallas_call(
        body, name="swa_fwd", grid=(nb,),
        in_specs=[pl.BlockSpec(memory_space=pltpu.SMEM), pl.BlockSpec((BLK, AW), cur),
                  pl.BlockSpec((BLK, KW), cur), pl.BlockSpec((BLK, KW), prev),
                  pl.BlockSpec((BLK, KW), cur), pl.BlockSpec((BLK, KW), prev)],
        out_specs=pl.BlockSpec((BLK, AW), cur), out_shape=_sds((t, AW)),
        compiler_params=_params())(sinks, q, k, k, v, v)


def _causal_bf16(w_ref, h, transposed):
    r = lax.broadcasted_iota(jnp.int32, (BLK, BLK), 0)
    c = lax.broadcasted_iota(jnp.int32, (BLK, BLK), 1)
    keep = (r <= c) if transposed else (c <= r)
    return jnp.where(keep, w_ref[h], 0.0).astype(BF16)


def _gmlp_mix(w_ref, xin, transposed):
    lo = _lane((BLK, 128)) < 64
    wm = [_causal_bf16(w_ref, h, transposed) for h in range(8)]
    rows = []
    for c in range(xin.shape[0] // BLK):
        cols = []
        for j in range(4):
            xs = xin[c * BLK:(c + 1) * BLK, 128 * j:128 * (j + 1)]
            cols.append(jnp.where(lo, _dot(wm[2 * j], xs), _dot(wm[2 * j + 1], xs)))
        rows.append(jnp.concatenate(cols, axis=1))
    return jnp.concatenate(rows, axis=0)


def _gmlp_fwd(gvn, gu, ws, bfull):
    t = gvn.shape[0]

    def body(x_ref, gu_ref, w_ref, b_ref, o_ref):
        mixed = _gmlp_mix(w_ref, x_ref[...], False) + jnp.tile(b_ref[...], (TM // BLK, 1))
        o_ref[...] = gu_ref[...] * mixed

    return pl.pallas_call(
        body, name="gmlp_fwd", grid=(t // TM,),
        in_specs=[_rows(TM, GW), _rows(TM, GW), _const((8, BLK, BLK)), _const((BLK, GW))],
        out_specs=_rows(TM, GW), out_shape=_sds((t, GW)), compiler_params=_params())(gvn, gu, ws, bfull)


def _mixer_out_fwd(attn, gm, x, w_out, aon, gon, xan):
    t = x.shape[0]

    def body(a_ref, g_ref, x_ref, w_ref, aon_ref, gon_ref, xan_ref, y_ref, x1_ref, h2_ref):
        a, g = a_ref[...], g_ref[...]
        y = jnp.concatenate([a * _rs(a) * aon_ref[...], g * _rs(g) * gon_ref[...]], axis=1).astype(BF16)
        y_ref[...] = y
        x1 = x_ref[...] + _dot(y, w_ref[...])
        x1_ref[...] = x1
        h2_ref[...] = (x1 * _rs(x1) * xan_ref[...]).astype(BF16)

    return pl.pallas_call(
        body, name="mixer_out_fwd", grid=(t // TM,),
        in_specs=[_rows(TM, AW), _rows(TM, GW), _rows(TM, D), _const((D, D)), _const((1, AW)), _const((1, GW)),
                  _const((1, D))],
        out_specs=[_rows(TM, D), _rows(TM, D), _rows(TM, D)],
        out_shape=[_sds((t, D), BF16), _sds((t, D)), _sds((t, D), BF16)],
        compiler_params=_params())(attn, gm, x, w_out, aon, gon, xan)


def _mem_kv_fwd(mem, mem_norm, wkv, kn4):
    def body(m_ref, g_ref, w_ref, kn_ref, mh_ref, kpre_ref, k_ref, v_ref):
        m = m_ref[...]
        mh = (m * _rs(m) * g_ref[...]).astype(BF16)
        mh_ref[...] = mh
        for h in range(XH):
            sl = slice(XD * h, XD * (h + 1))
            kh = _dot(mh, w_ref[h])
            kpre_ref[:, sl] = kh
            k_ref[:, sl] = (kh * _rs(kh) * kn_ref[:, sl]).astype(BF16)
            v_ref[:, sl] = _dot(mh, w_ref[XH + h]).astype(BF16)

    return pl.pallas_call(
        body, name="mem_kv_fwd",
        out_shape=[_sds((MEM, D), BF16), _sds((MEM, D)), _sds((MEM, D), BF16), _sds((MEM, D), BF16)],
        compiler_params=pltpu.CompilerParams(vmem_limit_bytes=VMEM_LIMIT))(mem, mem_norm, wkv, kn4)


def _xattn_probs(qpre_h, qn_h, k_h):
    rq = _rs(qpre_h)
    q2 = (qpre_h * rq * qn_h).astype(BF16)
    s = _dot_nt(q2, k_h) * (1.0 / math.sqrt(XD))
    p = jnp.exp(s - jnp.max(s, axis=-1, keepdims=True))
    return p * (1.0 / jnp.sum(p, axis=-1, keepdims=True)), q2, rq


def _xattn_fwd(h2, x1, wq, qn4, k2, v2, wo, ffn_norm):
    t = x1.shape[0]

    def body(h_ref, x_ref, wq_ref, qn_ref, k_ref, v_ref, wo_ref, fn_ref, qpre_ref, o_ref, x2_ref, h3_ref):
        qpre = _dot(h_ref[...], wq_ref[...])
        qpre_ref[...] = qpre
        outs = []
        for h in range(XH):
            sl = slice(XD * h, XD * (h + 1))
            pn, _, _ = _xattn_probs(qpre[:, sl], qn_ref[:, sl], k_ref[:, sl])
            outs.append(_dot(pn.astype(BF16), v_ref[:, sl]))
        o = jnp.concatenate(outs, axis=1).astype(BF16)
        o_ref[...] = o
        x2 = x_ref[...] + _dot(o, wo_ref[...])
        x2_ref[...] = x2
        h3_ref[...] = (x2 * _rs(x2) * fn_ref[...]).astype(BF16)

    return pl.pallas_call(
        body, name="xattn_fwd", grid=(t // TM,),
        in_specs=[_rows(TM, D), _rows(TM, D), _const((D, D)), _const((1, D)), _const((MEM, D)), _const((MEM, D)),
                  _const((D, D)), _const((1, D))],
        out_specs=[_rows(TM, D)] * 4,
        out_shape=[_sds((t, D)), _sds((t, D), BF16), _sds((t, D)), _sds((t, D), BF16)],
        compiler_params=_params())(h2, x1, wq, qn4, k2, v2, wo, ffn_norm)


SW = 704
NG = FF // SW
CM = 256


def _conv(ext, w, b):
    n = ext.shape[0]
    c = w[2:3, :] * ext + w[1:2, :] * pltpu.roll(ext, 1, 0) + w[0:1, :] * pltpu.roll(ext, 2, 0)
    return c[8:n] + b


def _pair(rows, w):
    return (2, None, rows, w)


def _conv_gate_fwd(a, conv, conv_b):
    t = a.shape[2]

    def body(a_ref, p_ref, w_ref, b_ref, u_ref):
        live = (pl.program_id(1) > 0).astype(F32)
        c = [_conv(jnp.concatenate([p_ref[s] * live, a_ref[s]], axis=0), w_ref[s], b_ref[s]) for s in range(2)]
        u_ref[...] = (_gelu(c[0]) * c[1]).astype(BF16)

    r8 = CM // 8
    return pl.pallas_call(
        body, name="conv_gate_fwd", grid=(NG, t // CM),
        in_specs=[pl.BlockSpec(_pair(CM, SW), lambda j, i: (0, j, i, 0)),
                  pl.BlockSpec(_pair(8, SW), lambda j, i: (0, j, jnp.maximum(i * r8 - 1, 0), 0)),
                  pl.BlockSpec(_pair(3, SW), lambda j, i: (0, j, 0, 0)),
                  pl.BlockSpec(_pair(1, SW), lambda j, i: (0, j, 0, 0))],
        out_specs=pl.BlockSpec((None, CM, SW), lambda j, i: (j, i, 0)), out_shape=_sds((NG, t, SW), BF16),
        compiler_params=_params(2))(a, a, conv, conv_b)


def _ffn_down_loss(u, down, x2, target):
    t = x2.shape[0]

    def body(u_ref, w_ref, x_ref, t_ref, dy_ref, loss_ref, acc_ref):
        i = pl.program_id(0)

        @pl.when(i == 0)
        def _():
            acc_ref[...] = jnp.zeros_like(acc_ref)

        err = x_ref[...] - t_ref[...]
        for d in range(NG):
            err = err + _dot(u_ref[d], w_ref[d])
        dy_ref[...] = err * (1.0 / D)
        acc_ref[...] += jnp.sum(err * err, axis=0, keepdims=True)

        @pl.when(i == pl.num_programs(0) - 1)
        def _():
            loss_ref[...] = jnp.full((8, 128), 0.5 / D, F32) * jnp.sum(acc_ref[...])

    return pl.pallas_call(
        body, name="ffn_down_loss", grid=(t // TM,),
        in_specs=[pl.BlockSpec((NG, TM, SW), lambda i: (0, i, 0)), _const((NG, SW, D)), _rows(TM, D), _rows(TM, D)],
        out_specs=[_rows(TM, D), _const((8, 128))], out_shape=[_sds((t, D)), _sds((8, 128))],
        scratch_shapes=[pltpu.VMEM((1, D), F32)], compiler_params=_params())(u, down, x2, target)


def _conv_gate_bwd(a, du, conv, conv_b):
    t = a.shape[2]
    nt = t // CM

    def body(a_ref, p_ref, n_ref, du_ref, dun_ref, w_ref, b_ref, da_ref, s_ref):
        i = pl.program_id(1)

        @pl.when(i == 0)
        def _():
            s_ref[...] = jnp.zeros_like(s_ref)

        head = (i > 0).astype(F32)
        tail = (i < nt - 1).astype(F32)
        e = [jnp.concatenate([p_ref[s] * head, a_ref[s], n_ref[s] * tail], axis=0) for s in range(2)]
        c = [_conv(e[s], w_ref[s], b_ref[s]) for s in range(2)]
        due = jnp.concatenate([du_ref[...], dun_ref[...] * tail], axis=0)
        row = lax.broadcasted_iota(jnp.int32, (8, SW), 0)
        dcs = (due * c[1] * _gelu_grad(c[0]), due * _gelu(c[0]))
        for s in range(2):
            dc, w = dcs[s], w_ref[s]
            n = dc.shape[0]
            da = w[2:3, :] * dc + w[1:2, :] * pltpu.roll(dc, n - 1, 0) + w[0:1, :] * pltpu.roll(dc, n - 2, 0)
            da_ref[s] = da[0:CM].astype(BF16)
            dc0 = dc[0:CM]
            e8 = e[s][8:8 + CM]
            e7 = pltpu.roll(e[s], 1, 0)[8:8 + CM]
            e6 = pltpu.roll(e[s], 2, 0)[8:8 + CM]
            sums = [jnp.sum(dc0 * e6, axis=0, keepdims=True), jnp.sum(dc0 * e7, axis=0, keepdims=True),
                    jnp.sum(dc0 * e8, axis=0, keepdims=True), jnp.sum(dc0, axis=0, keepdims=True)]
            upd = jnp.zeros((8, SW), F32)
            for r, v in enumerate(sums):
                upd = jnp.where(row == r, jnp.broadcast_to(v, (8, SW)), upd)
            s_ref[s] += upd

    r8 = CM // 8
    last8 = t // 8 - 1
    prev = lambda i: jnp.maximum(i * r8 - 1, 0)
    nxt = lambda i: jnp.minimum((i + 1) * r8, last8)
    return pl.pallas_call(
        body, name="conv_gate_bwd", grid=(NG, nt),
        in_specs=[pl.BlockSpec(_pair(CM, SW), lambda j, i: (0, j, i, 0)),
                  pl.BlockSpec(_pair(8, SW), lambda j, i: (0, j, prev(i), 0)),
                  pl.BlockSpec(_pair(8, SW), lambda j, i: (0, j, nxt(i), 0)),
                  pl.BlockSpec((None, CM, SW), lambda j, i: (j, i, 0)),
                  pl.BlockSpec((None, 8, SW), lambda j, i: (j, nxt(i), 0)),
                  pl.BlockSpec(_pair(3, SW), lambda j, i: (0, j, 0, 0)),
                  pl.BlockSpec(_pair(1, SW), lambda j, i: (0, j, 0, 0))],
        out_specs=[pl.BlockSpec(_pair(CM, SW), lambda j, i: (0, j, i, 0)),
                   pl.BlockSpec(_pair(8, SW), lambda j, i: (0, j, 0, 0))],
        out_shape=[_sds((2, NG, t, SW), BF16), _sds((2, NG, 8, SW))],
        compiler_params=_params(2))(a, a, a, du, du, conv, conv_b)


BT = 256


def _ffn_up_bwd(da, up, x2, dy, ffn_norm):
    t = x2.shape[0]

    def body(da_ref, w_ref, x_ref, dy_ref, g_ref, dx_ref, dn_ref):
        @pl.when(pl.program_id(0) == 0)
        def _():
            dn_ref[...] = jnp.zeros_like(dn_ref)

        dh = _dot_nt(da_ref[0], w_ref[0])
        for d in range(1, NDEV):
            dh = dh + _dot_nt(da_ref[d], w_ref[d])
        x = x_ref[...]
        dx, dg = _rms_bwd(dh, x, _rs(x), g_ref[...])
        dx_ref[...] = dy_ref[...] + dx
        dn_ref[...] += _colsum8(dg)

    return pl.pallas_call(
        body, name="ffn_up_bwd", grid=(t // BT,),
        in_specs=[pl.BlockSpec((NDEV, BT, SW), lambda i: (0, i, 0)), _const((NDEV, D, SW)), _rows(BT, D), _rows(BT, D),
                  _const((1, D))],
        out_specs=[_rows(BT, D), _const((8, D))], out_shape=[_sds((t, D)), _sds((8, D))],
        compiler_params=_params())(da, up, x2, dy, ffn_norm)


def _ffn_down_bwd_x(dy, down):
    t = dy.shape[0]

    def body(dy_ref, w_ref, du_ref):
        dyb = dy_ref[...].astype(BF16)
        for d in range(NG):
            du_ref[d] = _dot_nt(dyb, w_ref[d])

    return pl.pallas_call(
        body, name="ffn_down_bwd_x", grid=(t // TM,),
        in_specs=[_rows(TM, D), _const((NG, SW, D))], out_specs=pl.BlockSpec((NG, TM, SW), lambda i: (0, i, 0)),
        out_shape=_sds((NG, t, SW)), compiler_params=_params())(dy, down)


def _xattn_bwd(dx2, x1, qpre, k2, v2, wq, wo, qn4, xan):
    t = x1.shape[0]

    def body(dx2_ref, x1_ref, qpre_ref, k_ref, v_ref, wq_ref, wo_ref, qn_ref, xan_ref,
             dx1_ref, dqpre_ref, dk_ref, dv_ref, dqn_ref, dxan_ref):
        @pl.when(pl.program_id(0) == 0)
        def _():
            for r in (dk_ref, dv_ref, dqn_ref, dxan_ref):
                r[...] = jnp.zeros_like(r)

        dx2 = dx2_ref[...]
        do = _dot_nt(dx2.astype(BF16), wo_ref[...])
        dqs = []
        for h in range(XH):
            sl = slice(XD * h, XD * (h + 1))
            qpre_h = qpre_ref[:, sl]
            pn, q2, rq = _xattn_probs(qpre_h, qn_ref[:, sl], k_ref[:, sl])
            do_h = do[:, sl].astype(BF16)
            dp = _dot_nt(do_h, v_ref[:, sl])
            ds = (pn * (dp - jnp.sum(pn * dp, axis=-1, keepdims=True)) * (1.0 / math.sqrt(XD))).astype(BF16)
            dq2 = _dot(ds, k_ref[:, sl])
            dk_ref[:, sl] += _dot_tn(ds, q2)
            dv_ref[:, sl] += _dot_tn(pn.astype(BF16), do_h)
            dqh, dg = _rms_bwd(dq2, qpre_h, rq, qn_ref[:, sl])
            dqn_ref[...] += _colsum8(dg)
            dqs.append(dqh)
        dqpre = jnp.concatenate(dqs, axis=1).astype(BF16)
        dqpre_ref[...] = dqpre
        dh2 = _dot_nt(dqpre, wq_ref[...])
        x1 = x1_ref[...]
        dx, dg = _rms_bwd(dh2, x1, _rs(x1), xan_ref[...])
        dx1_ref[...] = dx2 + dx
        dxan_ref[...] += _colsum8(dg)

    return pl.pallas_call(
        body, name="xattn_bwd", grid=(t // BT,),
        in_specs=[_rows(BT, D), _rows(BT, D), _rows(BT, D), _const((MEM, D)), _const((MEM, D)), _const((D, D)),
                  _const((D, D)), _const((1, D)), _const((1, D))],
        out_specs=[_rows(BT, D), _rows(BT, D), _const((MEM, D)), _const((MEM, D)), _const((8, XD)), _const((8, D))],
        out_shape=[_sds((t, D)), _sds((t, D), BF16), _sds((MEM, D)), _sds((MEM, D)), _sds((8, XD)), _sds((8, D))],
        compiler_params=_params())(dx2, x1, qpre, k2, v2, wq, wo, qn4, xan)


def _mem_kv_bwd(mem, mh, kpre, dk2, dv2, wkv, kn4, mem_norm):
    def body(m_ref, mh_ref, kpre_ref, dk_ref, dv_ref, w_ref, kn_ref, g_ref, dw_ref, dkn_ref, dmn_ref):
        dkn = jnp.zeros((8, XD), F32)
        dm = jnp.zeros((MEM, D), F32)
        mh = mh_ref[...]
        for h in range(XH):
            sl = slice(XD * h, XD * (h + 1))
            kh = kpre_ref[:, sl]
            dkh, dg = _rms_bwd(dk_ref[:, sl], kh, _rs(kh), kn_ref[:, sl])
            dkn = dkn + _colsum8(dg)
            dkh = dkh.astype(BF16)
            dvh = dv_ref[:, sl].astype(BF16)
            dw_ref[h] = _dot_tn(mh, dkh).astype(BF16)
            dw_ref[XH + h] = _dot_tn(mh, dvh).astype(BF16)
            dm = dm + _dot_nt(dkh, w_ref[h]) + _dot_nt(dvh, w_ref[XH + h])
        dkn_ref[...] = dkn
        m = m_ref[...]
        _, dg = _rms_bwd(dm, m, _rs(m), g_ref[...])
        dmn_ref[...] = _colsum8(dg)

    return pl.pallas_call(
        body, name="mem_kv_bwd", out_shape=[_sds((2 * XH, D, XD), BF16), _sds((8, XD)), _sds((8, D))],
        compiler_params=pltpu.CompilerParams(vmem_limit_bytes=VMEM_LIMIT))(mem, mh, kpre, dk2, dv2, wkv, kn4, mem_norm)


def _mixer_out_bwd(dx1, attn, gm, w_out, aon, gon):
    t = dx1.shape[0]

    def body(dx_ref, a_ref, g_ref, w_ref, aon_ref, gon_ref, da_ref, dg_ref, dan_ref, dgn_ref):
        @pl.when(pl.program_id(0) == 0)
        def _():
            dan_ref[...] = jnp.zeros_like(dan_ref)
            dgn_ref[...] = jnp.zeros_like(dgn_ref)

        dy = _dot_nt(dx_ref[...].astype(BF16), w_ref[...])
        a, g = a_ref[...], g_ref[...]
        da, dna = _rms_bwd(dy[:, :AW], a, _rs(a), aon_ref[...])
        dg, dng = _rms_bwd(dy[:, AW:], g, _rs(g), gon_ref[...])
        da_ref[...] = da
        dg_ref[...] = dg
        dan_ref[...] += _colsum8(dna)
        dgn_ref[...] += _colsum8(dng)

    return pl.pallas_call(
        body, name="mixer_out_bwd", grid=(t // TM,),
        in_specs=[_rows(TM, D), _rows(TM, AW), _rows(TM, GW), _const((D, D)), _const((1, AW)), _const((1, GW))],
        out_specs=[_rows(TM, AW), _rows(TM, GW), _const((8, AW)), _const((8, GW))],
        out_shape=[_sds((t, AW)), _sds((t, GW)), _sds((8, AW)), _sds((8, GW))],
        compiler_params=_params())(dx1, attn, gm, w_out, aon, gon)


def _gmlp_bwd(dgm, gu, gvn, gz, ws, wst, bfull, gvw):
    t = dgm.shape[0]
    nc = TM // BLK

    def body(dgm_ref, gu_ref, x_ref, gz_ref, w_ref, wt_ref, b_ref, gvw_ref, dgz_ref, dw_ref, db_ref, dgvw_ref,
             dbacc_ref):
        @pl.when(pl.program_id(0) == 0)
        def _():
            for r in (dw_ref, dbacc_ref, dgvw_ref):
                r[...] = jnp.zeros_like(r)

        xin = x_ref[...]
        dgm = dgm_ref[...]
        mixed = _gmlp_mix(w_ref, xin, False) + jnp.tile(b_ref[...], (nc, 1))
        dgu = dgm * mixed
        dmixed = dgm * gu_ref[...]
        lo = _lane((BLK, 128)) < 64
        dbias = jnp.zeros((BLK, GW), F32)
        for c in range(nc):
            dmc = dmixed[c * BLK:(c + 1) * BLK]
            dbias = dbias + dmc
            for j in range(4):
                dm2 = dmc[:, 128 * j:128 * (j + 1)]
                xs = xin[c * BLK:(c + 1) * BLK, 128 * j:128 * (j + 1)]
                z = jnp.zeros_like(dm2)
                dw_ref[2 * j] += _dot_nt(jnp.where(lo, dm2, z).astype(BF16), xs)
                dw_ref[2 * j + 1] += _dot_nt(jnp.where(lo, z, dm2).astype(BF16), xs)
        dbacc_ref[...] += dbias
        dgvn = _gmlp_mix(wt_ref, dmixed.astype(BF16), True)
        gz_u, gz_v = gz_ref[:, :GW], gz_ref[:, GW:]
        gv = _gelu(gz_v)
        dgv, dg = _rms_bwd(dgvn, gv, _rs(gv), gvw_ref[...])
        dgvw_ref[...] += _colsum8(dg)
        dgz_ref[:, :GW] = (dgu * _gelu_grad(gz_u)).astype(BF16)
        dgz_ref[:, GW:] = (dgv * _gelu_grad(gz_v)).astype(BF16)

        @pl.when(pl.program_id(0) == pl.num_programs(0) - 1)
        def _():
            s = dbacc_ref[...]
            sel = (lax.broadcasted_iota(jnp.int32, (8, GW), 1) // HD
                   == lax.broadcasted_iota(jnp.int32, (8, GW), 0)).astype(BF16)
            hi = s.astype(BF16)
            r1 = s - hi.astype(F32)
            mid = r1.astype(BF16)
            lo = (r1 - mid.astype(F32)).astype(BF16)
            db_ref[...] = _dot_nt(sel, hi) + _dot_nt(sel, mid) + _dot_nt(sel, lo)
            r = lax.broadcasted_iota(jnp.int32, (BLK, BLK), 0)
            c = lax.broadcasted_iota(jnp.int32, (BLK, BLK), 1)
            for h in range(8):
                dw_ref[h] = jnp.where(c <= r, dw_ref[h], 0.0)

    return pl.pallas_call(
        body, name="gmlp_bwd", grid=(t // TM,),
        in_specs=[_rows(TM, GW), _rows(TM, GW), _rows(TM, GW), _rows(TM, 2 * GW), _const((8, BLK, BLK)),
                  _const((8, BLK, BLK)), _const((BLK, GW)), _const((1, GW))],
        out_specs=[_rows(TM, 2 * GW), _const((8, BLK, BLK)), _const((8, BLK)), _const((8, GW))],
        out_shape=[_sds((t, 2 * GW), BF16), _sds((8, BLK, BLK)), _sds((8, BLK)), _sds((8, GW))],
        scratch_shapes=[pltpu.VMEM((BLK, GW), F32)],
        compiler_params=_params())(dgm, gu, gvn, gz, ws, wst, bfull, gvw)


def _fold_half(v):
    return v + pltpu.roll(v, 64, 1)


def _swa_bwd(q, k, v, dattn, sinks):
    t = q.shape[0]
    nb = t // BLK

    def body(sink_ref, q_ref, kc_ref, kp_ref, vc_ref, vp_ref, do_ref, dq_ref, dk_ref, dv_ref, ds_ref,
             ck_ref, cv_ref, sacc_ref):
        i = pl.program_id(0)

        @pl.when(i == 0)
        def _():
            ck_ref[...] = jnp.zeros_like(ck_ref)
            cv_ref[...] = jnp.zeros_like(cv_ref)
            sacc_ref[...] = jnp.zeros_like(sacc_ref)

        @pl.when(i < nb)
        def _():
            mask = _swa_mask(i == 0)
            kk = jnp.concatenate([kp_ref[...], kc_ref[...]], axis=0).astype(F32)
            vv = jnp.concatenate([vp_ref[...], vc_ref[...]], axis=0).astype(F32)
            lo256 = _lane((2 * BLK, 128)) < 64
            dkk = jnp.zeros((2 * BLK, 128), F32)
            dvv = jnp.zeros((2 * BLK, 128), F32)
            for g in range(2):
                qs = _stack_heads(q_ref[:, 256 * g:256 * g + 128], q_ref[:, 256 * g + 128:256 * g + 256])
                dos = _stack_heads(do_ref[:, 256 * g:256 * g + 128],
                                   do_ref[:, 256 * g + 128:256 * g + 256]).astype(BF16)
                kd = _dup_half(kk, g)
                pn, psn = _swa_probs(qs, kd, mask, _sink_col(sink_ref, g))
                dp = _dot_nt(dos, _dup_half(vv, g))
                dd = jnp.sum(pn * dp, axis=-1, keepdims=True)
                ds = (pn * (dp - dd) * (1.0 / math.sqrt(HD))).astype(BF16)
                sacc_ref[g] += jnp.broadcast_to(-psn * dd, (4 * BLK, 128))
                dqa, dqb = _unstack_heads(_dot(ds, kd))
                dq_ref[:, 256 * g:256 * g + 128] = dqa
                dq_ref[:, 256 * g + 128:256 * g + 256] = dqb
                dkg = _fold_half(_dot_tn(ds, qs))
                dvg = _fold_half(_dot_tn(pn.astype(BF16), dos))
                keep = lo256 if g == 0 else jnp.logical_not(lo256)
                dkk = jnp.where(keep, dkg, dkk)
                dvv = jnp.where(keep, dvg, dvv)
            dk_ref[...] = ck_ref[...] + dkk[0:BLK]
            dv_ref[...] = cv_ref[...] + dvv[0:BLK]
            ck_ref[...] = dkk[BLK:]
            cv_ref[...] = dvv[BLK:]

        @pl.when(i == nb)
        def _():
            dk_ref[...] = ck_ref[...]
            dv_ref[...] = cv_ref[...]
            lane = _lane((8, 128))
            acc = jnp.zeros((8, 128), F32)
            for g in range(2):
                for j in range(4):
                    val = jnp.sum(sacc_ref[g, j * BLK:(j + 1) * BLK, :], axis=0, keepdims=True)
                    acc = jnp.where(lane == 4 * g + j, jnp.broadcast_to(val, (8, 128)), acc)
            ds_ref[...] = acc

    cur = lambda i: (jnp.minimum(i, nb - 1), 0)
    prev = lambda i: (jnp.clip(i - 1, 0, nb - 1), 0)
    return pl.pallas_call(
        body, name="swa_bwd", grid=(nb + 1,),
        in_specs=[pl.BlockSpec(memory_space=pltpu.SMEM), pl.BlockSpec((BLK, AW), cur),
                  pl.BlockSpec((BLK, KW), cur), pl.BlockSpec((BLK, KW), prev),
                  pl.BlockSpec((BLK, KW), cur), pl.BlockSpec((BLK, KW), prev), pl.BlockSpec((BLK, AW), cur)],
        out_specs=[pl.BlockSpec((BLK, AW), cur), pl.BlockSpec((BLK, KW), prev), pl.BlockSpec((BLK, KW), prev),
                   _const((8, 128))],
        out_shape=[_sds((t, AW)), _sds((t, KW)), _sds((t, KW)), _sds((8, 128))],
        scratch_shapes=[pltpu.VMEM((BLK, KW), F32), pltpu.VMEM((BLK, KW), F32), pltpu.VMEM((2, 4 * BLK, 128), F32)],
        compiler_params=_params())(sinks, q, k, k, v, v, dattn)


def _mixer_in_bwd(dq, dk, dv, dgz, qk, cos, sin, x, dx1, w_in, mix_norm, qn, kn):
    t = x.shape[0]

    def body(dq_ref, dk_ref, dv_ref, dgz_ref, qk_ref, cos_ref, sin_ref, x_ref, dx1_ref, w_ref, g_ref, qn_ref, kn_ref,
             gx_ref, dproj_ref, dmn_ref, dqn_ref, dkn_ref, qacc_ref, kacc_ref):
        i = pl.program_id(0)

        @pl.when(i == 0)
        def _():
            dmn_ref[...] = jnp.zeros_like(dmn_ref)
            qacc_ref[...] = jnp.zeros_like(qacc_ref)
            kacc_ref[...] = jnp.zeros_like(kacc_ref)

        cos2, sin2 = cos_ref[...], sin_ref[...]
        qpre, kpre = qk_ref[:, :AW], qk_ref[:, AW:]
        dqh = _rope_bwd(dq_ref[...], jnp.tile(cos2, (1, 4)), jnp.tile(sin2, (1, 4)))
        dqpre, dgq = _rms64_bwd(dqh, qpre, _rs64(qpre), qn_ref[...])
        dkh = _rope_bwd(dk_ref[...], cos2, sin2)
        dkpre, dgk = _rms64_bwd(dkh, kpre, _rs64(kpre), kn_ref[...])
        qacc_ref[...] += jnp.sum(dgq, axis=0, keepdims=True)
        kacc_ref[...] += jnp.sum(dgk, axis=0, keepdims=True)
        dproj = jnp.concatenate([dqpre.astype(BF16), dkpre.astype(BF16), dv_ref[...].astype(BF16), dgz_ref[...]], axis=1)
        dproj_ref[...] = dproj
        dh = _dot_nt(dproj, w_ref[...])
        xv = x_ref[...]
        dx, dg = _rms_bwd(dh, xv, _rs(xv), g_ref[...])
        gx_ref[...] = dx1_ref[...] + dx
        dmn_ref[...] += _colsum8(dg)

        @pl.when(i == pl.num_programs(0) - 1)
        def _():
            qa = qacc_ref[...]
            q4 = qa[:, 0:128] + qa[:, 128:256] + qa[:, 256:384] + qa[:, 384:512]
            dqn_ref[...] = jnp.broadcast_to(_fold_half(q4), (8, 128))
            dkn_ref[...] = jnp.broadcast_to(_fold_half(kacc_ref[...]), (8, 128))

    return pl.pallas_call(
        body, name="mixer_in_bwd", grid=(t // TM,),
        in_specs=[_rows(TM, AW), _rows(TM, KW), _rows(TM, KW), _rows(TM, 2 * GW), _rows(TM, AW + KW), _rows(TM, 128),
                  _rows(TM, 128), _rows(TM, D), _rows(TM, D), _const((D, IN)), _const((1, D)), _const((1, AW)),
                  _const((1, KW))],
        out_specs=[_rows(TM, D), _rows(TM, IN), _const((8, D)), _const((8, 128)), _const((8, 128))],
        out_shape=[_sds((t, D)), _sds((t, IN), BF16), _sds((8, D)), _sds((8, 128)), _sds((8, 128))],
        scratch_shapes=[pltpu.VMEM((1, AW), F32), pltpu.VMEM((1, KW), F32)],
        compiler_params=_params())(dq, dk, dv, dgz, qk, cos, sin, x, dx1, w_in, mix_norm, qn, kn)


def _local_step(x, mem, pos, target, p):
    t = x.shape[0]
    inv_freq = 1.0 / (ROPE_THETA ** (jnp.arange(HD // 2, dtype=F32) * (2.0 / HD)))
    cos, sin = _rope_tables(pos, jnp.tile(inv_freq, 4).reshape(1, 128))
    qn = jnp.tile(p["q_norm"], (1, AW // HD))
    kn = jnp.tile(p["k_norm"], (1, KW // HD))
    qn4 = jnp.tile(p["xa_q_norm"], (1, XH))
    kn4 = jnp.tile(p["xa_k_norm"], (1, XH))
    ws = p["gmlp_ws"]
    wst = jnp.swapaxes(ws, 1, 2)
    bfull = jnp.repeat(p["gmlp_bs"].T, HD, axis=1)
    conv, conv_b = p["ffn_conv"], p["ffn_conv_b"]

    h1, qk, gz, q, k, v, gu, gvn = _mixer_in_fwd(x, p["mix_norm"], p["w_in"], qn, kn, p["gmlp_v_norm"], cos, sin)
    attn = _swa_fwd(q, k, v, p["attn_sinks"])
    gm = _gmlp_fwd(gvn, gu, ws, bfull)
    ycat, x1, h2 = _mixer_out_fwd(attn, gm, x, p["w_out"], p["attn_out_norm"], p["gmlp_out_norm"], p["xa_norm"])
    mh, kpre, k2, v2 = _mem_kv_fwd(mem, p["mem_norm"], p["xa_wkv"], kn4)
    qpre, o, x2, h3 = _xattn_fwd(h2, x1, p["xa_wq"], qn4, k2, v2, p["xa_wo"], p["ffn_norm"])
    a = _mm_blocks(h3, p["ffn_up"], "ffn_up_fwd").reshape(2, NG, t, SW)
    u = _conv_gate_fwd(a, conv, conv_b)
    dy, loss8 = _ffn_down_loss(u, p["ffn_down"], x2, target)

    g, raw = {}, {}
    du = _ffn_down_bwd_x(dy, p["ffn_down"])
    g["ffn_down"] = _mm_tn(u, dy, "ffn_down_bwd_w")
    da, raw["conv_sums"] = _conv_gate_bwd(a, du, conv, conv_b)
    da = da.reshape(NDEV, t, SW)
    dx2, raw["ffn_norm"] = _ffn_up_bwd(da, p["ffn_up"], x2, dy, p["ffn_norm"])
    g["ffn_up"] = _mm_tn(h3, da, "ffn_up_bwd_w")
    dx1, dqpre, dk2, dv2, raw["xa_q_norm"], raw["xa_norm"] = _xattn_bwd(dx2, x1, qpre, k2, v2, p["xa_wq"], p["xa_wo"],
                                                                       qn4, p["xa_norm"])
    g["xa_wo"] = _mm_tn(o, dx2, "xa_wo_bwd_w")
    g["xa_wq"] = _mm_tn(h2, dqpre, "xa_wq_bwd_w")
    g["xa_wkv"], raw["xa_k_norm"], raw["mem_norm"] = _mem_kv_bwd(mem, mh, kpre, dk2, dv2, p["xa_wkv"], kn4, p["mem_norm"])
    dattn, dgm, raw["attn_out_norm"], raw["gmlp_out_norm"] = _mixer_out_bwd(dx1, attn, gm, p["w_out"],
                                                                            p["attn_out_norm"], p["gmlp_out_norm"])
    g["w_out"] = _mm_tn(ycat, dx1, "w_out_bwd_w")
    dgz, raw["gmlp_ws"], raw["gmlp_bs"], raw["gmlp_v_norm"] = _gmlp_bwd(dgm, gu, gvn, gz, ws, wst, bfull, p["gmlp_v_norm"])
    dq, dk, dv, raw["attn_sinks"] = _swa_bwd(q, k, v, dattn, p["attn_sinks"])
    grad_x, dproj, raw["mix_norm"], raw["q_norm"], raw["k_norm"] = _mixer_in_bwd(
        dq, dk, dv, dgz, qk, cos, sin, x, dx1, p["w_in"], p["mix_norm"], qn, kn)
    g["w_in"] = _mm_tn(h1, dproj, "w_in_bwd_w")
    return loss8[0, 0], grad_x, g, raw


def _all_to_all(block_for, out_refs, send_sems, recv_sems, local_sems):
    x, y, cc = lax.axis_index("x"), lax.axis_index("y"), lax.axis_index("c")
    me = 4 * x + 2 * y + cc
    copies = []
    for a, out_ref in enumerate(out_refs):
        for k in range(1, NDEV):
            px = 1 - x if k & 4 else x
            py = 1 - y if k & 2 else y
            pc = 1 - cc if k & 1 else cc
            cp = pltpu.make_async_remote_copy(
                src_ref=block_for(a, 4 * px + 2 * py + pc), dst_ref=out_ref.at[me],
                send_sem=send_sems.at[a, k - 1], recv_sem=recv_sems.at[a, k - 1],
                device_id=(px, py, pc), device_id_type=pl.DeviceIdType.MESH)
            cp.start()
            copies.append(cp)
        own = pltpu.make_async_copy(block_for(a, me), out_ref.at[me], local_sems.at[a])
        own.start()
        copies.append(own)
    for cp in copies:
        cp.wait()


def _exchange_sems(n):
    return [pltpu.SemaphoreType.DMA((n, NDEV - 1)), pltpu.SemaphoreType.DMA((n, NDEV - 1)), pltpu.SemaphoreType.DMA((n,))]


def _gather_weights(shards):
    n = len(shards)
    cast = [s.shape[1] >= 8 for s in shards]

    def body(*refs):
        ins, outs, bufs = refs[:n], refs[n:2 * n], refs[2 * n:3 * n]
        for a in range(n):
            if cast[a]:
                bufs[a][...] = ins[a][0].astype(BF16)
        _all_to_all(lambda a, dest: bufs[a] if cast[a] else ins[a].at[0], outs, *refs[3 * n:])

    return pl.pallas_call(
        body, name="gather_weights",
        out_shape=[_sds((NDEV,) + s.shape[1:], BF16 if c else F32) for s, c in zip(shards, cast)],
        in_specs=[pl.BlockSpec(memory_space=pltpu.VMEM)] * n, out_specs=[pl.BlockSpec(memory_space=pl.ANY)] * n,
        scratch_shapes=[pltpu.VMEM(s.shape[1:] if c else (8, 128), BF16) for s, c in zip(shards, cast)] + _exchange_sems(n),
        compiler_params=pltpu.CompilerParams(vmem_limit_bytes=VMEM_LIMIT))(*shards)


def _scatter_grads(by_dest, for_all):
    srcs = list(by_dest) + list(for_all)
    n, nd = len(srcs), len(by_dest)

    def body(*refs):
        ins, outs = refs[:n], refs[n:2 * n]
        _all_to_all(lambda a, dest: ins[a].at[dest] if a < nd else ins[a], outs, *refs[2 * n:])

    shapes = [s.shape[1:] for s in by_dest] + [s.shape for s in for_all]
    return pl.pallas_call(
        body, name="scatter_grads", out_shape=[_sds((NDEV,) + shp, s.dtype) for shp, s in zip(shapes, srcs)],
        in_specs=[pl.BlockSpec(memory_space=pl.ANY)] * n, out_specs=[pl.BlockSpec(memory_space=pl.ANY)] * n,
        scratch_shapes=_exchange_sems(n))(*srcs)


def _adam(parts, w, m, v, name):
    def body(p_ref, w_ref, m_ref, v_ref, g_ref, d_ref, nm_ref, nv_ref):
        g = _sum_parts(p_ref)
        g_ref[0] = g
        d_ref[0], nm_ref[0], nv_ref[0] = _adam_math(g, w_ref[0], m_ref[0], v_ref[0])

    return pl.pallas_call(
        body, name=name, out_shape=[_sds(w.shape)] * 4,
        compiler_params=pltpu.CompilerParams(vmem_limit_bytes=VMEM_LIMIT))(parts, w, m, v)


BIG = ("w_in", "w_out", "xa_wkv", "xa_wq", "xa_wo", "ffn_up", "ffn_conv", "ffn_down")
VECS = (("mix_norm", D), ("q_norm", HD), ("k_norm", HD), ("attn_sinks", 8), ("gmlp_v_norm", GW), ("attn_out_norm", AW),
        ("gmlp_out_norm", GW), ("xa_norm", D), ("mem_norm", D), ("xa_q_norm", XD), ("xa_k_norm", XD), ("ffn_norm", D))
BS_ROW = 16
VEC_ROWS = 24
SMALL = tuple(n for n, _ in VECS) + ("gmlp_bs", "gmlp_ws", "ffn_conv_b")


def _pack_small(raw):
    names = [n for n, _ in VECS] + ["gmlp_bs", "conv_sums"]

    def body(*refs):
        ins = dict(zip(names, refs))
        vec_ref, conv_ref, cb_ref = refs[len(names):]
        vec_ref[...] = jnp.zeros_like(vec_ref)
        for r, (n, w) in enumerate(VECS):
            vec_ref[r:r + 1, 0:w] = ins[n][0:1, 0:w]
        vec_ref[BS_ROW:BS_ROW + 8, 0:BLK] = ins["gmlp_bs"][...]
        for s in range(2):
            for d in range(NG):
                conv_ref[s, d] = ins["conv_sums"][s, d, 0:3, :]
                cb_ref[s, d] = ins["conv_sums"][s, d, 3:4, :]

    return pl.pallas_call(
        body, name="pack_small", out_shape=[_sds((VEC_ROWS, D)), _sds((2, NG, 3, SW)), _sds((2, NG, 1, SW))])(
            *[raw[n] for n in names])


def _adam_math(g, w, m, v):
    nm = B1 * m + (1.0 - B1) * g
    nv = B2 * v + (1.0 - B2) * (g * g)
    m_hat = nm / (1.0 - B1 ** STEP)
    v_hat = nv / (1.0 - B2 ** STEP)
    return -LR * (m_hat / (jnp.sqrt(v_hat) + AEPS) + WD * w), nm, nv


def _sum_parts(p_ref):
    g = p_ref[0].astype(F32)
    for j in range(1, NDEV):
        g = g + p_ref[j].astype(F32)
    return g


def _adam_small(parts_vec, parts_ws, parts_cb, w, m, v):
    def body(*refs):
        pv_ref, pws_ref, pcb_ref = refs[:3]
        ins = refs[3:3 + 3 * len(SMALL)]
        outs = refs[3 + 3 * len(SMALL):]
        gv = _sum_parts(pv_ref)
        for j, n in enumerate(SMALL):
            w_ref, m_ref, v_ref = ins[3 * j:3 * j + 3]
            o = outs[4 * j:4 * j + 4]
            if n == "gmlp_ws":
                g = _sum_parts(pws_ref)
            elif n == "ffn_conv_b":
                g = _sum_parts(pcb_ref)
            elif n == "gmlp_bs":
                g = gv[BS_ROW:BS_ROW + 8, 0:BLK]
            else:
                g = gv[j:j + 1, 0:VECS[j][1]]
            lead = n in ("gmlp_ws", "gmlp_bs")
            res = (g,) + _adam_math(g, w_ref[0] if lead else w_ref[...], m_ref[0] if lead else m_ref[...],
                                    v_ref[0] if lead else v_ref[...])
            for o_ref, val in zip(o, res):
                if lead:
                    o_ref[0] = val
                else:
                    o_ref[...] = val

    args = [parts_vec, parts_ws, parts_cb] + [d[n] for n in SMALL for d in (w, m, v)]
    res = pl.pallas_call(body, name="adam_small", out_shape=[_sds(w[n].shape) for n in SMALL for _ in range(4)],
                         compiler_params=pltpu.CompilerParams(vmem_limit_bytes=VMEM_LIMIT))(*args)
    return {n: tuple(res[4 * j:4 * j + 4]) for j, n in enumerate(SMALL)}


def kernel(x, mem, positions, mix_norm, w_in, q_norm, k_norm, attn_sinks, gmlp_v_norm, gmlp_ws, gmlp_bs, attn_out_norm, gmlp_out_norm, w_out, xa_norm, mem_norm, xa_wq, xa_wkv, xa_q_norm, xa_k_norm, xa_wo, ffn_norm, ffn_up, ffn_conv, ffn_conv_b, ffn_down, loss_target, m_mix_norm, m_w_in, m_q_norm, m_k_norm, m_attn_sinks, m_gmlp_v_norm, m_gmlp_ws, m_gmlp_bs, m_attn_out_norm, m_gmlp_out_norm, m_w_out, m_xa_norm, m_mem_norm, m_xa_wq, m_xa_wkv, m_xa_q_norm, m_xa_k_norm, m_xa_wo, m_ffn_norm, m_ffn_up, m_ffn_conv, m_ffn_conv_b, m_ffn_down, v_mix_norm, v_w_in, v_q_norm, v_k_norm, v_attn_sinks, v_gmlp_v_norm, v_gmlp_ws, v_gmlp_bs, v_attn_out_norm, v_gmlp_out_norm, v_w_out, v_xa_norm, v_mem_norm, v_xa_wq, v_xa_wkv, v_xa_q_norm, v_xa_k_norm, v_xa_wo, v_ffn_norm, v_ffn_up, v_ffn_conv, v_ffn_conv_b, v_ffn_down):
    names = ("mix_norm", "w_in", "q_norm", "k_norm", "attn_sinks", "gmlp_v_norm", "gmlp_ws", "gmlp_bs", "attn_out_norm",
             "gmlp_out_norm", "w_out", "xa_norm", "mem_norm", "xa_wq", "xa_wkv", "xa_q_norm", "xa_k_norm", "xa_wo",
             "ffn_norm", "ffn_up", "ffn_conv", "ffn_conv_b", "ffn_down")
    w = dict(zip(names, (mix_norm, w_in, q_norm, k_norm, attn_sinks, gmlp_v_norm, gmlp_ws, gmlp_bs, attn_out_norm,
                         gmlp_out_norm, w_out, xa_norm, mem_norm, xa_wq, xa_wkv, xa_q_norm, xa_k_norm, xa_wo, ffn_norm,
                         ffn_up, ffn_conv, ffn_conv_b, ffn_down)))
    m = dict(zip(names, (m_mix_norm, m_w_in, m_q_norm, m_k_norm, m_attn_sinks, m_gmlp_v_norm, m_gmlp_ws, m_gmlp_bs,
                         m_attn_out_norm, m_gmlp_out_norm, m_w_out, m_xa_norm, m_mem_norm, m_xa_wq, m_xa_wkv,
                         m_xa_q_norm, m_xa_k_norm, m_xa_wo, m_ffn_norm, m_ffn_up, m_ffn_conv, m_ffn_conv_b, m_ffn_down)))
    v = dict(zip(names, (v_mix_norm, v_w_in, v_q_norm, v_k_norm, v_attn_sinks, v_gmlp_v_norm, v_gmlp_ws, v_gmlp_bs,
                         v_attn_out_norm, v_gmlp_out_norm, v_w_out, v_xa_norm, v_mem_norm, v_xa_wq, v_xa_wkv,
                         v_xa_q_norm, v_xa_k_norm, v_xa_wo, v_ffn_norm, v_ffn_up, v_ffn_conv, v_ffn_conv_b, v_ffn_down)))
    t = x.shape[1]

    p = dict(zip(BIG, _gather_weights([w[n] for n in BIG])))
    p["w_in"] = p["w_in"].transpose(1, 0, 2).reshape(D, IN)
    for n in ("w_out", "xa_wq", "xa_wo"):
        p[n] = p[n].reshape(D, D)
    p["ffn_down"] = p["ffn_down"].reshape(NG, SW, D)
    p["ffn_conv"] = p["ffn_conv"].reshape(2, NG, 3, SW)
    conv_b = {k: d["ffn_conv_b"].reshape(NDEV, 1, SW) for k, d in (("w", w), ("m", m), ("v", v))}
    p["ffn_conv_b"] = conv_b["w"].reshape(2, NG, 1, SW)
    for n in SMALL[:-1]:
        p[n] = w[n]
    p["gmlp_ws"], p["gmlp_bs"] = w["gmlp_ws"][0], w["gmlp_bs"][0]

    loss, grad_x, g, raw = _local_step(x[0], mem[0], positions.reshape(t, 1), loss_target[0], p)
    loss = lax.psum(loss, AXES)

    vec, g["ffn_conv"], cb = _pack_small(raw)
    g["w_in"] = g["w_in"].reshape(D, NDEV, IN // NDEV).transpose(1, 0, 2)
    by_dest = [g[n].reshape((NDEV,) + w[n].shape[1:]) for n in BIG]
    parts = _scatter_grads(by_dest, [vec, raw["gmlp_ws"], cb.reshape(NDEV, 1, SW)])
    res = {n: _adam(parts[j], w[n], m[n], v[n], "adam_" + n) for j, n in enumerate(BIG)}
    small = lambda d, k: {**{n: d[n] for n in SMALL[:-1]}, "ffn_conv_b": conv_b[k]}
    res.update(_adam_small(*parts[len(BIG):], small(w, "w"), small(m, "m"), small(v, "v")))

    outs = [loss, grad_x[None]]
    for j in range(4):
        outs += [res[n][j].reshape(w[n].shape) for n in names]
    return tuple(outs)
```

```python
import functools
import math

import jax
import jax.numpy as jnp
from jax import lax
from jax.experimental import pallas as pl
from jax.experimental.pallas import tpu as pltpu

F32 = jnp.float32
BF16 = jnp.bfloat16

D = 1024
HD = 64
AW = 512
KW = 128
GW = 512
IN = AW + 2 * KW + 2 * GW
BLK = 128
MEM = 256
XH = 4
XD = 256
FF = 2816
EPS = 1e-6
ROPE_THETA = 10000.0
NDEV = 8
LR, B1, B2, AEPS, WD, STEP = 0.001, 0.9, 0.999, 1e-08, 0.01, 10

TM = 512
VMEM_LIMIT = 56 * 1024 * 1024
NEG = float(jnp.finfo(jnp.float32).min)
GELU_C0 = math.sqrt(2.0 / math.pi)
GELU_C1 = 0.044715
AXES = ("x", "y", "c")


def _dot(a, b):
    return jnp.dot(a, b, preferred_element_type=F32)


def _dot_nt(a, b):
    return lax.dot_general(a, b, (((1,), (1,)), ((), ())), preferred_element_type=F32)


def _dot_tn(a, b):
    return lax.dot_general(a, b, (((0,), (0,)), ((), ())), preferred_element_type=F32)


def _rs(x):
    return lax.rsqrt(jnp.mean(x * x, axis=-1, keepdims=True) + EPS)


def _rms_bwd(dy, x, r, g):
    xh = x * r
    dxh = dy * g
    dx = r * (dxh - xh * jnp.mean(dxh * xh, axis=-1, keepdims=True))
    return dx, dy * xh


def _lane(shape):
    return lax.broadcasted_iota(jnp.int32, shape, len(shape) - 1)


def _gsum64(v):
    w = v.shape[-1]
    lane = _lane(v.shape)
    s = v
    for sh in (1, 2, 4, 8, 16, 32):
        s = s + jnp.where((lane & sh) != 0, pltpu.roll(s, sh, 1), pltpu.roll(s, w - sh, 1))
    return s


def _rs64(x):
    return lax.rsqrt(_gsum64(x * x) * (1.0 / HD) + EPS)


def _rms64_bwd(dy, x, r, g):
    xh = x * r
    dxh = dy * g
    dx = r * (dxh - xh * (_gsum64(dxh * xh) * (1.0 / HD)))
    return dx, dy * xh


def _rot_half(v):
    w = v.shape[-1]
    return jnp.where((_lane(v.shape) & 32) == 0, pltpu.roll(v, w - 32, 1), pltpu.roll(v, 32, 1))


def _rope(v, cos, sin_signed):
    return v * cos + _rot_half(v) * sin_signed


def _rope_bwd(dv, cos, sin_signed):
    return dv * cos + _rot_half(dv * sin_signed)


def _gelu(z):
    return 0.5 * z * (1.0 + jnp.tanh(GELU_C0 * (z + GELU_C1 * z * z * z)))


def _gelu_grad(z):
    t = jnp.tanh(GELU_C0 * (z + GELU_C1 * z * z * z))
    return 0.5 * (1.0 + t) + 0.5 * z * (1.0 - t * t) * (GELU_C0 * (1.0 + 3.0 * GELU_C1 * z * z))


def _colsum8(v):
    s = jnp.sum(v, axis=0, keepdims=True)
    row = lax.broadcasted_iota(jnp.int32, (8, v.shape[1]), 0)
    return jnp.where(row == 0, jnp.broadcast_to(s, (8, v.shape[1])), 0.0)


def _params(n_axes=1):
    return pltpu.CompilerParams(dimension_semantics=("arbitrary",) * n_axes, vmem_limit_bytes=VMEM_LIMIT)


def _rows(tm, w):
    return pl.BlockSpec((tm, w), lambda i: (i, 0))


def _const(shape):
    nd = len(shape)
    return pl.BlockSpec(shape, lambda *_: (0,) * nd)


def _sds(shape, dtype=F32):
    return jax.ShapeDtypeStruct(shape, dtype)


def _mm_blocks(a, b, name):
    t, k = a.shape
    g, _, n = b.shape

    def body(a_ref, b_ref, o_ref):
        o_ref[...] = _dot(a_ref[...], b_ref[...])

    return pl.pallas_call(
        body, name=name, grid=(g, t // TM),
        in_specs=[pl.BlockSpec((TM, k), lambda j, i: (i, 0)), pl.BlockSpec((None, k, n), lambda j, i: (j, 0, 0))],
        out_specs=pl.BlockSpec((None, TM, n), lambda j, i: (j, i, 0)),
        out_shape=_sds((g, t, n)), compiler_params=_params(2))(a, b)


def _mm_tn(a, b, name):
    g = max(a.shape[0] if a.ndim == 3 else 1, b.shape[0] if b.ndim == 3 else 1)
    t, m = a.shape[-2:]
    n = b.shape[-1]

    def body(a_ref, b_ref, o_ref, acc_ref):
        i = pl.program_id(1)

        @pl.when(i == 0)
        def _():
            acc_ref[...] = jnp.zeros_like(acc_ref)

        acc_ref[...] += _dot_tn(a_ref[...].astype(BF16), b_ref[...].astype(BF16))

        @pl.when(i == pl.num_programs(1) - 1)
        def _():
            o_ref[...] = acc_ref[...].astype(BF16)

    def spec(v):
        w = v.shape[-1]
        if v.ndim == 3:
            return pl.BlockSpec((None, TM, w), lambda j, i: (j, i, 0))
        return pl.BlockSpec((TM, w), lambda j, i: (i, 0))

    return pl.pallas_call(
        body, name=name, grid=(g, t // TM), in_specs=[spec(a), spec(b)],
        out_specs=pl.BlockSpec((None, m, n), lambda j, i: (j, 0, 0)), out_shape=_sds((g, m, n), BF16),
        scratch_shapes=[pltpu.VMEM((m, n), F32)], compiler_params=_params(2))(a, b)


def _rope_tables(pos, inv_freq):
    t = pos.shape[0]

    def body(pos_ref, f_ref, cos_ref, sin_ref):
        ang = pos_ref[...].astype(F32) * f_ref[...]
        sign = jnp.where((_lane(ang.shape) & 32) == 0, -1.0, 1.0)
        cos_ref[...] = jnp.cos(ang)
        sin_ref[...] = jnp.sin(ang) * sign

    return pl.pallas_call(
        body, name="rope_tables", grid=(t // TM,),
        in_specs=[_rows(TM, 1), _const((1, 128))], out_specs=[_rows(TM, 128), _rows(TM, 128)],
        out_shape=[_sds((t, 128)), _sds((t, 128))], compiler_params=_params())(pos, inv_freq)


def _mixer_in_fwd(x, mix_norm, w_in, qn, kn, gvw, cos, sin):
    t = x.shape[0]

    def body(x_ref, g_ref, w_ref, qn_ref, kn_ref, gvw_ref, cos_ref, sin_ref,
             h_ref, qk_ref, gz_ref, q_ref, k_ref, v_ref, gu_ref, gvn_ref):
        x = x_ref[...]
        h = (x * _rs(x) * g_ref[...]).astype(BF16)
        h_ref[...] = h
        proj = _dot(h, w_ref[...])
        qk = proj[:, :AW + KW]
        qk_ref[...] = qk
        gz = proj[:, AW + 2 * KW:]
        gz_ref[...] = gz
        cos2, sin2 = cos_ref[...], sin_ref[...]
        q = qk[:, :AW]
        q = q * _rs64(q) * qn_ref[...]
        q_ref[...] = _rope(q, jnp.tile(cos2, (1, 4)), jnp.tile(sin2, (1, 4))).astype(BF16)
        k = qk[:, AW:]
        k = k * _rs64(k) * kn_ref[...]
        k_ref[...] = _rope(k, cos2, sin2).astype(BF16)
        v_ref[...] = proj[:, AW + KW:AW + 2 * KW].astype(BF16)
        gu_ref[...] = _gelu(gz[:, :GW])
        gv = _gelu(gz[:, GW:])
        gvn_ref[...] = (gv * _rs(gv) * gvw_ref[...]).astype(BF16)

    return pl.pallas_call(
        body, name="mixer_in_fwd", grid=(t // TM,),
        in_specs=[_rows(TM, D), _const((1, D)), _const((D, IN)), _const((1, AW)), _const((1, KW)),
                  _const((1, GW)), _rows(TM, 128), _rows(TM, 128)],
        out_specs=[_rows(TM, D), _rows(TM, AW + KW), _rows(TM, 2 * GW), _rows(TM, AW), _rows(TM, KW),
                   _rows(TM, KW), _rows(TM, GW), _rows(TM, GW)],
        out_shape=[_sds((t, D), BF16), _sds((t, AW + KW)), _sds((t, 2 * GW)), _sds((t, AW), BF16),
                   _sds((t, KW), BF16), _sds((t, KW), BF16), _sds((t, GW)), _sds((t, GW), BF16)],
        compiler_params=_params())(x, mix_norm, w_in, qn, kn, gvw, cos, sin)


def _dup_half(kk, g):
    lane = _lane(kk.shape)
    other = pltpu.roll(kk, 64, 1)
    keep = (lane < 64) if g == 0 else (lane >= 64)
    return jnp.where(keep, kk, other).astype(BF16)


def _swa_mask(first_block):
    qi = lax.broadcasted_iota(jnp.int32, (4 * BLK, 2 * BLK), 0) & (BLK - 1)
    kj = lax.broadcasted_iota(jnp.int32, (4 * BLK, 2 * BLK), 1)
    diff = qi + BLK - kj
    band = (diff >= 0) & (diff < BLK)
    return band & (jnp.logical_not(first_block) | (kj >= BLK))


def _stack_heads(a2, b2):
    lo = _lane(a2.shape) < 64
    z = jnp.zeros_like(a2)
    return jnp.concatenate([jnp.where(lo, a2, z), jnp.where(lo, z, a2), jnp.where(lo, b2, z), jnp.where(lo, z, b2)], axis=0)


def _unstack_heads(o):
    lo = _lane((BLK, 128)) < 64
    return jnp.where(lo, o[0:BLK], o[BLK:2 * BLK]), jnp.where(lo, o[2 * BLK:3 * BLK], o[3 * BLK:4 * BLK])


def _sink_col(sink_ref, g):
    row = lax.broadcasted_iota(jnp.int32, (4 * BLK, 1), 0)
    s = [sink_ref[0, 4 * g + j] for j in range(4)]
    return jnp.where(row < BLK, s[0], jnp.where(row < 2 * BLK, s[1], jnp.where(row < 3 * BLK, s[2], s[3])))


def _swa_probs(qs, kd, mask, sink):
    s = _dot_nt(qs, kd) * (1.0 / math.sqrt(HD))
    s = jnp.where(mask, s, NEG)
    m = jnp.maximum(jnp.max(s, axis=-1, keepdims=True), sink)
    p = jnp.exp(s - m)
    ps = jnp.exp(sink - m)
    inv = 1.0 / (jnp.sum(p, axis=-1, keepdims=True) + ps)
    return p * inv, ps * inv


def _swa_fwd(q, k, v, sinks):
    t = q.shape[0]
    nb = t // BLK

    def body(sink_ref, q_ref, kc_ref, kp_ref, vc_ref, vp_ref, o_ref):
        i = pl.program_id(0)
        mask = _swa_mask(i == 0)
        kk = jnp.concatenate([kp_ref[...], kc_ref[...]], axis=0).astype(F32)
        vv = jnp.concatenate([vp_ref[...], vc_ref[...]], axis=0).astype(F32)
        for g in range(2):
            qs = _stack_heads(q_ref[:, 256 * g:256 * g + 128], q_ref[:, 256 * g + 128:256 * g + 256])
            pn, _ = _swa_probs(qs, _dup_half(kk, g), mask, _sink_col(sink_ref, g))
            oa, ob = _unstack_heads(_dot(pn.astype(BF16), _dup_half(vv, g)))
            o_ref[:, 256 * g:256 * g + 128] = oa
            o_ref[:, 256 * g + 128:256 * g + 256] = ob

    cur = lambda i: (i, 0)
    prev = lambda i: (jnp.maximum(i - 1, 0), 0)
    return pl.pallas_call(
        body, name="swa_fwd", grid=(nb,),
        in_specs=[pl.BlockSpec(memory_space=pltpu.SMEM), pl.BlockSpec((BLK, AW), cur),
                  pl.BlockSpec((BLK, KW), cur), pl.BlockSpec((BLK, KW), prev),
                  pl.BlockSpec((BLK, KW), cur), pl.BlockSpec((BLK, KW), prev)],
        out_specs=pl.BlockSpec((BLK, AW), cur), out_shape=_sds((t, AW)),
        compiler_params=_params())(sinks, q, k, k, v, v)


def _causal_bf16(w_ref, h, transposed):
    r = lax.broadcasted_iota(jnp.int32, (BLK, BLK), 0)
    c = lax.broadcasted_iota(jnp.int32, (BLK, BLK), 1)
    keep = (r <= c) if transposed else (c <= r)
    return jnp.where(keep, w_ref[h], 0.0).astype(BF16)


def _gmlp_mix(w_ref, xin, transposed):
    lo = _lane((BLK, 128)) < 64
    wm = [_causal_bf16(w_ref, h, transposed) for h in range(8)]
    rows = []
    for c in range(xin.shape[0] // BLK):
        cols = []
        for j in range(4):
            xs = xin[c * BLK:(c + 1) * BLK, 128 * j:128 * (j + 1)]
            cols.append(jnp.where(lo, _dot(wm[2 * j], xs), _dot(wm[2 * j + 1], xs)))
        rows.append(jnp.concatenate(cols, axis=1))
    return jnp.concatenate(rows, axis=0)


def _gmlp_fwd(gvn, gu, ws, bfull):
    t = gvn.shape[0]

    def body(x_ref, gu_ref, w_ref, b_ref, o_ref):
        mixed = _gmlp_mix(w_ref, x_ref[...], False) + jnp.tile(b_ref[...], (TM // BLK, 1))
        o_ref[...] = gu_ref[...] * mixed

    return pl.pallas_call(
        body, name="gmlp_fwd", grid=(t // TM,),
        in_specs=[_rows(TM, GW), _rows(TM, GW), _const((8, BLK, BLK)), _const((BLK, GW))],
        out_specs=_rows(TM, GW), out_shape=_sds((t, GW)), compiler_params=_params())(gvn, gu, ws, bfull)


def _mixer_out_fwd(attn, gm, x, w_out, aon, gon, xan):
    t = x.shape[0]

    def body(a_ref, g_ref, x_ref, w_ref, aon_ref, gon_ref, xan_ref, y_ref, x1_ref, h2_ref):
        a, g = a_ref[...], g_ref[...]
        y = jnp.concatenate([a * _rs(a) * aon_ref[...], g * _rs(g) * gon_ref[...]], axis=1).astype(BF16)
        y_ref[...] = y
        x1 = x_ref[...] + _dot(y, w_ref[...])
        x1_ref[...] = x1
        h2_ref[...] = (x1 * _rs(x1) * xan_ref[...]).astype(BF16)

    return pl.pallas_call(
        body, name="mixer_out_fwd", grid=(t // TM,),
        in_specs=[_rows(TM, AW), _rows(TM, GW), _rows(TM, D), _const((D, D)), _const((1, AW)), _const((1, GW)),
                  _const((1, D))],
        out_specs=[_rows(TM, D), _rows(TM, D), _rows(TM, D)],
        out_shape=[_sds((t, D), BF16), _sds((t, D)), _sds((t, D), BF16)],
        compiler_params=_params())(attn, gm, x, w_out, aon, gon, xan)


def _mem_kv_fwd(mem, mem_norm, wkv, kn4):
    def body(m_ref, g_ref, w_ref, kn_ref, mh_ref, kpre_ref, k_ref, v_ref):
        m = m_ref[...]
        mh = (m * _rs(m) * g_ref[...]).astype(BF16)
        mh_ref[...] = mh
        for h in range(XH):
            sl = slice(XD * h, XD * (h + 1))
            kh = _dot(mh, w_ref[h])
            kpre_ref[:, sl] = kh
            k_ref[:, sl] = (kh * _rs(kh) * kn_ref[:, sl]).astype(BF16)
            v_ref[:, sl] = _dot(mh, w_ref[XH + h]).astype(BF16)

    return pl.pallas_call(
        body, name="mem_kv_fwd",
        out_shape=[_sds((MEM, D), BF16), _sds((MEM, D)), _sds((MEM, D), BF16), _sds((MEM, D), BF16)],
        compiler_params=pltpu.CompilerParams(vmem_limit_bytes=VMEM_LIMIT))(mem, mem_norm, wkv, kn4)


def _xattn_probs(qpre_h, qn_h, k_h):
    rq = _rs(qpre_h)
    q2 = (qpre_h * rq * qn_h).astype(BF16)
    s = _dot_nt(q2, k_h) * (1.0 / math.sqrt(XD))
    p = jnp.exp(s - jnp.max(s, axis=-1, keepdims=True))
    return p * (1.0 / jnp.sum(p, axis=-1, keepdims=True)), q2, rq


def _xattn_fwd(h2, x1, wq, qn4, k2, v2, wo, ffn_norm):
    t = x1.shape[0]

    def body(h_ref, x_ref, wq_ref, qn_ref, k_ref, v_ref, wo_ref, fn_ref, qpre_ref, o_ref, x2_ref, h3_ref):
        qpre = _dot(h_ref[...], wq_ref[...])
        qpre_ref[...] = qpre
        outs = []
        for h in range(XH):
            sl = slice(XD * h, XD * (h + 1))
            pn, _, _ = _xattn_probs(qpre[:, sl], qn_ref[:, sl], k_ref[:, sl])
            outs.append(_dot(pn.astype(BF16), v_ref[:, sl]))
        o = jnp.concatenate(outs, axis=1).astype(BF16)
        o_ref[...] = o
        x2 = x_ref[...] + _dot(o, wo_ref[...])
        x2_ref[...] = x2
        h3_ref[...] = (x2 * _rs(x2) * fn_ref[...]).astype(BF16)

    return pl.pallas_call(
        body, name="xattn_fwd", grid=(t // TM,),
        in_specs=[_rows(TM, D), _rows(TM, D), _const((D, D)), _const((1, D)), _const((MEM, D)), _const((MEM, D)),
                  _const((D, D)), _const((1, D))],
        out_specs=[_rows(TM, D)] * 4,
        out_shape=[_sds((t, D)), _sds((t, D), BF16), _sds((t, D)), _sds((t, D), BF16)],
        compiler_params=_params())(h2, x1, wq, qn4, k2, v2, wo, ffn_norm)


SW = 704
NG = FF // SW
CM = 256


def _conv(ext, w, b):
    n = ext.shape[0]
    c = w[2:3, :] * ext + w[1:2, :] * pltpu.roll(ext, 1, 0) + w[0:1, :] * pltpu.roll(ext, 2, 0)
    return c[8:n] + b


def _pair(rows, w):
    return (2, None, rows, w)


def _conv_gate_fwd(a, conv, conv_b):
    t = a.shape[2]

    def body(a_ref, p_ref, w_ref, b_ref, u_ref):
        live = (pl.program_id(1) > 0).astype(F32)
        c = [_conv(jnp.concatenate([p_ref[s] * live, a_ref[s]], axis=0), w_ref[s], b_ref[s]) for s in range(2)]
        u_ref[...] = (_gelu(c[0]) * c[1]).astype(BF16)

    r8 = CM // 8
    return pl.pallas_call(
        body, name="conv_gate_fwd", grid=(NG, t // CM),
        in_specs=[pl.BlockSpec(_pair(CM, SW), lambda j, i: (0, j, i, 0)),
                  pl.BlockSpec(_pair(8, SW), lambda j, i: (0, j, jnp.maximum(i * r8 - 1, 0), 0)),
                  pl.BlockSpec(_pair(3, SW), lambda j, i: (0, j, 0, 0)),
                  pl.BlockSpec(_pair(1, SW), lambda j, i: (0, j, 0, 0))],
        out_specs=pl.BlockSpec((None, CM, SW), lambda j, i: (j, i, 0)), out_shape=_sds((NG, t, SW), BF16),
        compiler_params=_params(2))(a, a, conv, conv_b)


def _ffn_down_loss(u, down, x2, target):
    t = x2.shape[0]

    def body(u_ref, w_ref, x_ref, t_ref, dy_ref, loss_ref, acc_ref):
        i = pl.program_id(0)

        @pl.when(i == 0)
        def _():
            acc_ref[...] = jnp.zeros_like(acc_ref)

        err = x_ref[...] - t_ref[...]
        for d in range(NG):
            err = err + _dot(u_ref[d], w_ref[d])
        dy_ref[...] = err * (1.0 / D)
        acc_ref[...] += jnp.sum(err * err, axis=0, keepdims=True)

        @pl.when(i == pl.num_programs(0) - 1)
        def _():
            loss_ref[...] = jnp.full((8, 128), 0.5 / D, F32) * jnp.sum(acc_ref[...])

    return pl.pallas_call(
        body, name="ffn_down_loss", grid=(t // TM,),
        in_specs=[pl.BlockSpec((NG, TM, SW), lambda i: (0, i, 0)), _const((NG, SW, D)), _rows(TM, D), _rows(TM, D)],
        out_specs=[_rows(TM, D), _const((8, 128))], out_shape=[_sds((t, D)), _sds((8, 128))],
        scratch_shapes=[pltpu.VMEM((1, D), F32)], compiler_params=_params())(u, down, x2, target)


def _conv_gate_bwd(a, du, conv, conv_b):
    t = a.shape[2]
    nt = t // CM

    def body(a_ref, p_ref, n_ref, du_ref, dun_ref, w_ref, b_ref, da_ref, s_ref):
        i = pl.program_id(1)

        @pl.when(i == 0)
        def _():
            s_ref[...] = jnp.zeros_like(s_ref)

        head = (i > 0).astype(F32)
        tail = (i < nt - 1).astype(F32)
        e = [jnp.concatenate([p_ref[s] * head, a_ref[s], n_ref[s] * tail], axis=0) for s in range(2)]
        c = [_conv(e[s], w_ref[s], b_ref[s]) for s in range(2)]
        due = jnp.concatenate([du_ref[...], dun_ref[...] * tail], axis=0)
        row = lax.broadcasted_iota(jnp.int32, (8, SW), 0)
        dcs = (due * c[1] * _gelu_grad(c[0]), due * _gelu(c[0]))
        for s in range(2):
            dc, w = dcs[s], w_ref[s]
            n = dc.shape[0]
            da = w[2:3, :] * dc + w[1:2, :] * pltpu.roll(dc, n - 1, 0) + w[0:1, :] * pltpu.roll(dc, n - 2, 0)
            da_ref[s] = da[0:CM].astype(BF16)
            dc0 = dc[0:CM]
            e8 = e[s][8:8 + CM]
            e7 = pltpu.roll(e[s], 1, 0)[8:8 + CM]
            e6 = pltpu.roll(e[s], 2, 0)[8:8 + CM]
            sums = [jnp.sum(dc0 * e6, axis=0, keepdims=True), jnp.sum(dc0 * e7, axis=0, keepdims=True),
                    jnp.sum(dc0 * e8, axis=0, keepdims=True), jnp.sum(dc0, axis=0, keepdims=True)]
            upd = jnp.zeros((8, SW), F32)
            for r, v in enumerate(sums):
                upd = jnp.where(row == r, jnp.broadcast_to(v, (8, SW)), upd)
            s_ref[s] += upd

    r8 = CM // 8
    last8 = t // 8 - 1
    prev = lambda i: jnp.maximum(i * r8 - 1, 0)
    nxt = lambda i: jnp.minimum((i + 1) * r8, last8)
    return pl.pallas_call(
        body, name="conv_gate_bwd", grid=(NG, nt),
        in_specs=[pl.BlockSpec(_pair(CM, SW), lambda j, i: (0, j, i, 0)),
                  pl.BlockSpec(_pair(8, SW), lambda j, i: (0, j, prev(i), 0)),
                  pl.BlockSpec(_pair(8, SW), lambda j, i: (0, j, nxt(i), 0)),
                  pl.BlockSpec((None, CM, SW), lambda j, i: (j, i, 0)),
                  pl.BlockSpec((None, 8, SW), lambda j, i: (j, nxt(i), 0)),
                  pl.BlockSpec(_pair(3, SW), lambda j, i: (0, j, 0, 0)),
                  pl.BlockSpec(_pair(1, SW), lambda j, i: (0, j, 0, 0))],
        out_specs=[pl.BlockSpec(_pair(CM, SW), lambda j, i: (0, j, i, 0)),
                   pl.BlockSpec(_pair(8, SW), lambda j, i: (0, j, 0, 0))],
        out_shape=[_sds((2, NG, t, SW), BF16), _sds((2, NG, 8, SW))],
        compiler_params=_params(2))(a, a, a, du, du, conv, conv_b)


BT = 256


def _ffn_up_bwd(da, up, x2, dy, ffn_norm):
    t = x2.shape[0]

    def body(da_ref, w_ref, x_ref, dy_ref, g_ref, dx_ref, dn_ref):
        @pl.when(pl.program_id(0) == 0)
        def _():
            dn_ref[...] = jnp.zeros_like(dn_ref)

        dh = _dot_nt(da_ref[0], w_ref[0])
        for d in range(1, NDEV):
            dh = dh + _dot_nt(da_ref[d], w_ref[d])
        x = x_ref[...]
        dx, dg = _rms_bwd(dh, x, _rs(x), g_ref[...])
        dx_ref[...] = dy_ref[...] + dx
        dn_ref[...] += _colsum8(dg)

    return pl.pallas_call(
        body, name="ffn_up_bwd", grid=(t // BT,),
        in_specs=[pl.BlockSpec((NDEV, BT, SW), lambda i: (0, i, 0)), _const((NDEV, D, SW)), _rows(BT, D), _rows(BT, D),
                  _const((1, D))],
        out_specs=[_rows(BT, D), _const((8, D))], out_shape=[_sds((t, D)), _sds((8, D))],
        compiler_params=_params())(da, up, x2, dy, ffn_norm)


def _ffn_down_bwd_x(dy, down):
    t = dy.shape[0]

    def body(dy_ref, w_ref, du_ref):
        dyb = dy_ref[...].astype(BF16)
        for d in range(NG):
            du_ref[d] = _dot_nt(dyb, w_ref[d])

    return pl.pallas_call(
        body, name="ffn_down_bwd_x", grid=(t // TM,),
        in_specs=[_rows(TM, D), _const((NG, SW, D))], out_specs=pl.BlockSpec((NG, TM, SW), lambda i: (0, i, 0)),
        out_shape=_sds((NG, t, SW)), compiler_params=_params())(dy, down)


def _xattn_bwd(dx2, x1, qpre, k2, v2, wq, wo, qn4, xan):
    t = x1.shape[0]

    def body(dx2_ref, x1_ref, qpre_ref, k_ref, v_ref, wq_ref, wo_ref, qn_ref, xan_ref,
             dx1_ref, dqpre_ref, dk_ref, dv_ref, dqn_ref, dxan_ref):
        @pl.when(pl.program_id(0) == 0)
        def _():
            for r in (dk_ref, dv_ref, dqn_ref, dxan_ref):
                r[...] = jnp.zeros_like(r)

        dx2 = dx2_ref[...]
        do = _dot_nt(dx2.astype(BF16), wo_ref[...])
        dqs = []
        for h in range(XH):
            sl = slice(XD * h, XD * (h + 1))
            qpre_h = qpre_ref[:, sl]
            pn, q2, rq = _xattn_probs(qpre_h, qn_ref[:, sl], k_ref[:, sl])
            do_h = do[:, sl].astype(BF16)
            dp = _dot_nt(do_h, v_ref[:, sl])
            ds = (pn * (dp - jnp.sum(pn * dp, axis=-1, keepdims=True)) * (1.0 / math.sqrt(XD))).astype(BF16)
            dq2 = _dot(ds, k_ref[:, sl])
            dk_ref[:, sl] += _dot_tn(ds, q2)
            dv_ref[:, sl] += _dot_tn(pn.astype(BF16), do_h)
            dqh, dg = _rms_bwd(dq2, qpre_h, rq, qn_ref[:, sl])
            dqn_ref[...] += _colsum8(dg)
            dqs.append(dqh)
        dqpre = jnp.concatenate(dqs, axis=1).astype(BF16)
        dqpre_ref[...] = dqpre
        dh2 = _dot_nt(dqpre, wq_ref[...])
        x1 = x1_ref[...]
        dx, dg = _rms_bwd(dh2, x1, _rs(x1), xan_ref[...])
        dx1_ref[...] = dx2 + dx
        dxan_ref[...] += _colsum8(dg)

    return pl.pallas_call(
        body, name="xattn_bwd", grid=(t // BT,),
        in_specs=[_rows(BT, D), _rows(BT, D), _rows(BT, D), _const((MEM, D)), _const((MEM, D)), _const((D, D)),
                  _const((D, D)), _const((1, D)), _const((1, D))],
        out_specs=[_rows(BT, D), _rows(BT, D), _const((MEM, D)), _const((MEM, D)), _const((8, XD)), _const((8, D))],
        out_shape=[_sds((t, D)), _sds((t, D), BF16), _sds((MEM, D)), _sds((MEM, D)), _sds((8, XD)), _sds((8, D))],
        compiler_params=_params())(dx2, x1, qpre, k2, v2, wq, wo, qn4, xan)


def _mem_kv_bwd(mem, mh, kpre, dk2, dv2, wkv, kn4, mem_norm):
    def body(m_ref, mh_ref, kpre_ref, dk_ref, dv_ref, w_ref, kn_ref, g_ref, dw_ref, dkn_ref, dmn_ref):
        dkn = jnp.zeros((8, XD), F32)
        dm = jnp.zeros((MEM, D), F32)
        mh = mh_ref[...]
        for h in range(XH):
            sl = slice(XD * h, XD * (h + 1))
            kh = kpre_ref[:, sl]
            dkh, dg = _rms_bwd(dk_ref[:, sl], kh, _rs(kh), kn_ref[:, sl])
            dkn = dkn + _colsum8(dg)
            dkh = dkh.astype(BF16)
            dvh = dv_ref[:, sl].astype(BF16)
            dw_ref[h] = _dot_tn(mh, dkh).astype(BF16)
            dw_ref[XH + h] = _dot_tn(mh, dvh).astype(BF16)
            dm = dm + _dot_nt(dkh, w_ref[h]) + _dot_nt(dvh, w_ref[XH + h])
        dkn_ref[...] = dkn
        m = m_ref[...]
        _, dg = _rms_bwd(dm, m, _rs(m), g_ref[...])
        dmn_ref[...] = _colsum8(dg)

    return pl.pallas_call(
        body, name="mem_kv_bwd", out_shape=[_sds((2 * XH, D, XD), BF16), _sds((8, XD)), _sds((8, D))],
        compiler_params=pltpu.CompilerParams(vmem_limit_bytes=VMEM_LIMIT))(mem, mh, kpre, dk2, dv2, wkv, kn4, mem_norm)


def _mixer_out_bwd(dx1, attn, gm, w_out, aon, gon):
    t = dx1.shape[0]

    def body(dx_ref, a_ref, g_ref, w_ref, aon_ref, gon_ref, da_ref, dg_ref, dan_ref, dgn_ref):
        @pl.when(pl.program_id(0) == 0)
        def _():
            dan_ref[...] = jnp.zeros_like(dan_ref)
            dgn_ref[...] = jnp.zeros_like(dgn_ref)

        dy = _dot_nt(dx_ref[...].astype(BF16), w_ref[...])
        a, g = a_ref[...], g_ref[...]
        da, dna = _rms_bwd(dy[:, :AW], a, _rs(a), aon_ref[...])
        dg, dng = _rms_bwd(dy[:, AW:], g, _rs(g), gon_ref[...])
        da_ref[...] = da
        dg_ref[...] = dg
        dan_ref[...] += _colsum8(dna)
        dgn_ref[...] += _colsum8(dng)

    return pl.pallas_call(
        body, name="mixer_out_bwd", grid=(t // TM,),
        in_specs=[_rows(TM, D), _rows(TM, AW), _rows(TM, GW), _const((D, D)), _const((1, AW)), _const((1, GW))],
        out_specs=[_rows(TM, AW), _rows(TM, GW), _const((8, AW)), _const((8, GW))],
        out_shape=[_sds((t, AW)), _sds((t, GW)), _sds((8, AW)), _sds((8, GW))],
        compiler_params=_params())(dx1, attn, gm, w_out, aon, gon)


def _gmlp_bwd(dgm, gu, gvn, gz, ws, wst, bfull, gvw):
    t = dgm.shape[0]
    nc = TM // BLK

    def body(dgm_ref, gu_ref, x_ref, gz_ref, w_ref, wt_ref, b_ref, gvw_ref, dgz_ref, dw_ref, db_ref, dgvw_ref,
             dbacc_ref):
        @pl.when(pl.program_id(0) == 0)
        def _():
            for r in (dw_ref, dbacc_ref, dgvw_ref):
                r[...] = jnp.zeros_like(r)

        xin = x_ref[...]
        dgm = dgm_ref[...]
        mixed = _gmlp_mix(w_ref, xin, False) + jnp.tile(b_ref[...], (nc, 1))
        dgu = dgm * mixed
        dmixed = dgm * gu_ref[...]
        lo = _lane((BLK, 128)) < 64
        dbias = jnp.zeros((BLK, GW), F32)
        for c in range(nc):
            dmc = dmixed[c * BLK:(c + 1) * BLK]
            dbias = dbias + dmc
            for j in range(4):
                dm2 = dmc[:, 128 * j:128 * (j + 1)]
                xs = xin[c * BLK:(c + 1) * BLK, 128 * j:128 * (j + 1)]
                z = jnp.zeros_like(dm2)
                dw_ref[2 * j] += _dot_nt(jnp.where(lo, dm2, z).astype(BF16), xs)
                dw_ref[2 * j + 1] += _dot_nt(jnp.where(lo, z, dm2).astype(BF16), xs)
        dbacc_ref[...] += dbias
        dgvn = _gmlp_mix(wt_ref, dmixed.astype(BF16), True)
        gz_u, gz_v = gz_ref[:, :GW], gz_ref[:, GW:]
        gv = _gelu(gz_v)
        dgv, dg = _rms_bwd(dgvn, gv, _rs(gv), gvw_ref[...])
        dgvw_ref[...] += _colsum8(dg)
        dgz_ref[:, :GW] = (dgu * _gelu_grad(gz_u)).astype(BF16)
        dgz_ref[:, GW:] = (dgv * _gelu_grad(gz_v)).astype(BF16)

        @pl.when(pl.program_id(0) == pl.num_programs(0) - 1)
        def _():
            s = dbacc_ref[...]
            sel = (lax.broadcasted_iota(jnp.int32, (8, GW), 1) // HD
                   == lax.broadcasted_iota(jnp.int32, (8, GW), 0)).astype(BF16)
            hi = s.astype(BF16)
            r1 = s - hi.astype(F32)
            mid = r1.astype(BF16)
            lo = (r1 - mid.astype(F32)).astype(BF16)
            db_ref[...] = _dot_nt(sel, hi) + _dot_nt(sel, mid) + _dot_nt(sel, lo)
            r = lax.broadcasted_iota(jnp.int32, (BLK, BLK), 0)
            c = lax.broadcasted_iota(jnp.int32, (BLK, BLK), 1)
            for h in range(8):
                dw_ref[h] = jnp.where(c <= r, dw_ref[h], 0.0)

    return pl.pallas_call(
        body, name="gmlp_bwd", grid=(t // TM,),
        in_specs=[_rows(TM, GW), _rows(TM, GW), _rows(TM, GW), _rows(TM, 2 * GW), _const((8, BLK, BLK)),
                  _const((8, BLK, BLK)), _const((BLK, GW)), _const((1, GW))],
        out_specs=[_rows(TM, 2 * GW), _const((8, BLK, BLK)), _const((8, BLK)), _const((8, GW))],
        out_shape=[_sds((t, 2 * GW), BF16), _sds((8, BLK, BLK)), _sds((8, BLK)), _sds((8, GW))],
        scratch_shapes=[pltpu.VMEM((BLK, GW), F32)],
        compiler_params=_params())(dgm, gu, gvn, gz, ws, wst, bfull, gvw)


def _fold_half(v):
    return v + pltpu.roll(v, 64, 1)


def _swa_bwd(q, k, v, dattn, sinks):
    t = q.shape[0]
    nb = t // BLK

    def body(sink_ref, q_ref, kc_ref, kp_ref, vc_ref, vp_ref, do_ref, dq_ref, dk_ref, dv_ref, ds_ref,
             ck_ref, cv_ref, sacc_ref):
        i = pl.program_id(0)

        @pl.when(i == 0)
        def _():
            ck_ref[...] = jnp.zeros_like(ck_ref)
            cv_ref[...] = jnp.zeros_like(cv_ref)
            sacc_ref[...] = jnp.zeros_like(sacc_ref)

        @pl.when(i < nb)
        def _():
            mask = _swa_mask(i == 0)
            kk = jnp.concatenate([kp_ref[...], kc_ref[...]], axis=0).astype(F32)
            vv = jnp.concatenate([vp_ref[...], vc_ref[...]], axis=0).astype(F32)
            lo256 = _lane((2 * BLK, 128)) < 64
            dkk = jnp.zeros((2 * BLK, 128), F32)
            dvv = jnp.zeros((2 * BLK, 128), F32)
            for g in range(2):
                qs = _stack_heads(q_ref[:, 256 * g:256 * g + 128], q_ref[:, 256 * g + 128:256 * g + 256])
                dos = _stack_heads(do_ref[:, 256 * g:256 * g + 128],
                                   do_ref[:, 256 * g + 128:256 * g + 256]).astype(BF16)
                kd = _dup_half(kk, g)
                pn, psn = _swa_probs(qs, kd, mask, _sink_col(sink_ref, g))
                dp = _dot_nt(dos, _dup_half(vv, g))
                dd = jnp.sum(pn * dp, axis=-1, keepdims=True)
                ds = (pn * (dp - dd) * (1.0 / math.sqrt(HD))).astype(BF16)
                sacc_ref[g] += jnp.broadcast_to(-psn * dd, (4 * BLK, 128))
                dqa, dqb = _unstack_heads(_dot(ds, kd))
                dq_ref[:, 256 * g:256 * g + 128] = dqa
                dq_ref[:, 256 * g + 128:256 * g + 256] = dqb
                dkg = _fold_half(_dot_tn(ds, qs))
                dvg = _fold_half(_dot_tn(pn.astype(BF16), dos))
                keep = lo256 if g == 0 else jnp.logical_not(lo256)
                dkk = jnp.where(keep, dkg, dkk)
                dvv = jnp.where(keep, dvg, dvv)
            dk_ref[...] = ck_ref[...] + dkk[0:BLK]
            dv_ref[...] = cv_ref[...] + dvv[0:BLK]
            ck_ref[...] = dkk[BLK:]
            cv_ref[...] = dvv[BLK:]

        @pl.when(i == nb)
        def _():
            dk_ref[...] = ck_ref[...]
            dv_ref[...] = cv_ref[...]
            lane = _lane((8, 128))
            acc = jnp.zeros((8, 128), F32)
            for g in range(2):
                for j in range(4):
                    val = jnp.sum(sacc_ref[g, j * BLK:(j + 1) * BLK, :], axis=0, keepdims=True)
                    acc = jnp.where(lane == 4 * g + j, jnp.broadcast_to(val, (8, 128)), acc)
            ds_ref[...] = acc

    cur = lambda i: (jnp.minimum(i, nb - 1), 0)
    prev = lambda i: (jnp.clip(i - 1, 0, nb - 1), 0)
    return pl.pallas_call(
        body, name="swa_bwd", grid=(nb + 1,),
        in_specs=[pl.BlockSpec(memory_space=pltpu.SMEM), pl.BlockSpec((BLK, AW), cur),
                  pl.BlockSpec((BLK, KW), cur), pl.BlockSpec((BLK, KW), prev),
                  pl.BlockSpec((BLK, KW), cur), pl.BlockSpec((BLK, KW), prev), pl.BlockSpec((BLK, AW), cur)],
        out_specs=[pl.BlockSpec((BLK, AW), cur), pl.BlockSpec((BLK, KW), prev), pl.BlockSpec((BLK, KW), prev),
                   _const((8, 128))],
        out_shape=[_sds((t, AW)), _sds((t, KW)), _sds((t, KW)), _sds((8, 128))],
        scratch_shapes=[pltpu.VMEM((BLK, KW), F32), pltpu.VMEM((BLK, KW), F32), pltpu.VMEM((2, 4 * BLK, 128), F32)],
        compiler_params=_params())(sinks, q, k, k, v, v, dattn)


def _mixer_in_bwd(dq, dk, dv, dgz, qk, cos, sin, x, dx1, w_in, mix_norm, qn, kn):
    t = x.shape[0]

    def body(dq_ref, dk_ref, dv_ref, dgz_ref, qk_ref, cos_ref, sin_ref, x_ref, dx1_ref, w_ref, g_ref, qn_ref, kn_ref,
             gx_ref, dproj_ref, dmn_ref, dqn_ref, dkn_ref, qacc_ref, kacc_ref):
        i = pl.program_id(0)

        @pl.when(i == 0)
        def _():
            dmn_ref[...] = jnp.zeros_like(dmn_ref)
            qacc_ref[...] = jnp.zeros_like(qacc_ref)
            kacc_ref[...] = jnp.zeros_like(kacc_ref)

        cos2, sin2 = cos_ref[...], sin_ref[...]
        qpre, kpre = qk_ref[:, :AW], qk_ref[:, AW:]
        dqh = _rope_bwd(dq_ref[...], jnp.tile(cos2, (1, 4)), jnp.tile(sin2, (1, 4)))
        dqpre, dgq = _rms64_bwd(dqh, qpre, _rs64(qpre), qn_ref[...])
        dkh = _rope_bwd(dk_ref[...], cos2, sin2)
        dkpre, dgk = _rms64_bwd(dkh, kpre, _rs64(kpre), kn_ref[...])
        qacc_ref[...] += jnp.sum(dgq, axis=0, keepdims=True)
        kacc_ref[...] += jnp.sum(dgk, axis=0, keepdims=True)
        dproj = jnp.concatenate([dqpre.astype(BF16), dkpre.astype(BF16), dv_ref[...].astype(BF16), dgz_ref[...]], axis=1)
        dproj_ref[...] = dproj
        dh = _dot_nt(dproj, w_ref[...])
        xv = x_ref[...]
        dx, dg = _rms_bwd(dh, xv, _rs(xv), g_ref[...])
        gx_ref[...] = dx1_ref[...] + dx
        dmn_ref[...] += _colsum8(dg)

        @pl.when(i == pl.num_programs(0) - 1)
        def _():
            qa = qacc_ref[...]
            q4 = qa[:, 0:128] + qa[:, 128:256] + qa[:, 256:384] + qa[:, 384:512]
            dqn_ref[...] = jnp.broadcast_to(_fold_half(q4), (8, 128))
            dkn_ref[...] = jnp.broadcast_to(_fold_half(kacc_ref[...]), (8, 128))

    return pl.pallas_call(
        body, name="mixer_in_bwd", grid=(t // TM,),
        in_specs=[_rows(TM, AW), _rows(TM, KW), _rows(TM, KW), _rows(TM, 2 * GW), _rows(TM, AW + KW), _rows(TM, 128),
                  _rows(TM, 128), _rows(TM, D), _rows(TM, D), _const((D, IN)), _const((1, D)), _const((1, AW)),
                  _const((1, KW))],
        out_specs=[_rows(TM, D), _rows(TM, IN), _const((8, D)), _const((8, 128)), _const((8, 128))],
        out_shape=[_sds((t, D)), _sds((t, IN), BF16), _sds((8, D)), _sds((8, 128)), _sds((8, 128))],
        scratch_shapes=[pltpu.VMEM((1, AW), F32), pltpu.VMEM((1, KW), F32)],
        compiler_params=_params())(dq, dk, dv, dgz, qk, cos, sin, x, dx1, w_in, mix_norm, qn, kn)


def _local_step(x, mem, pos, target, p, fetch, ship):
    t = x.shape[0]
    p = dict(p)
    p.update(fetch(0, None))
    inv_freq = 1.0 / (ROPE_THETA ** (jnp.arange(HD // 2, dtype=F32) * (2.0 / HD)))
    cos, sin = _rope_tables(pos, jnp.tile(inv_freq, 4).reshape(1, 128))
    qn = jnp.tile(p["q_norm"], (1, AW // HD))
    kn = jnp.tile(p["k_norm"], (1, KW // HD))
    qn4 = jnp.tile(p["xa_q_norm"], (1, XH))
    kn4 = jnp.tile(p["xa_k_norm"], (1, XH))
    ws = p["gmlp_ws"]
    wst = jnp.swapaxes(ws, 1, 2)
    bfull = jnp.repeat(p["gmlp_bs"].T, HD, axis=1)
    conv_b = p["ffn_conv_b"]

    h1, qk, gz, q, k, v, gu, gvn = _mixer_in_fwd(x, p["mix_norm"], p["w_in"], qn, kn, p["gmlp_v_norm"], cos, sin)
    attn = _swa_fwd(q, k, v, p["attn_sinks"])
    gm = _gmlp_fwd(gvn, gu, ws, bfull)
    ycat, x1, h2 = _mixer_out_fwd(attn, gm, x, p["w_out"], p["attn_out_norm"], p["gmlp_out_norm"], p["xa_norm"])
    p.update(fetch(1, h2))
    mh, kpre, k2, v2 = _mem_kv_fwd(mem, p["mem_norm"], p["xa_wkv"], kn4)
    qpre, o, x2, h3 = _xattn_fwd(h2, x1, p["xa_wq"], qn4, k2, v2, p["xa_wo"], p["ffn_norm"])
    p.update(fetch(2, h3))
    conv = p["ffn_conv"]
    a = _mm_blocks(h3, p["ffn_up"], "ffn_up_fwd").reshape(2, NG, t, SW)
    u = _conv_gate_fwd(a, conv, conv_b)
    dy, loss8 = _ffn_down_loss(u, p["ffn_down"], x2, target)

    raw = {}
    du = _ffn_down_bwd_x(dy, p["ffn_down"])
    d_down = _mm_tn(u, dy, "ffn_down_bwd_w")
    da, raw["conv_sums"] = _conv_gate_bwd(a, du, conv, conv_b)
    da = da.reshape(NDEV, t, SW)
    dx2, raw["ffn_norm"] = _ffn_up_bwd(da, p["ffn_up"], x2, dy, p["ffn_norm"])
    d_up = _mm_tn(h3, da, "ffn_up_bwd_w")
    token = ship(0, {"ffn_down": d_down, "ffn_up": d_up, "ffn_conv": raw["conv_sums"][:, :, 0:3]})
    dx1, dqpre, dk2, dv2, raw["xa_q_norm"], raw["xa_norm"] = _xattn_bwd(
        dx2, x1, qpre, k2, v2, p["xa_wq"], p["xa_wo"], qn4 + jnp.tile(token[0:1], (1, D // 128)), p["xa_norm"])
    d_wo = _mm_tn(o, dx2, "xa_wo_bwd_w")
    d_wq = _mm_tn(h2, dqpre, "xa_wq_bwd_w")
    d_wkv, raw["xa_k_norm"], raw["mem_norm"] = _mem_kv_bwd(mem, mh, kpre, dk2, dv2, p["xa_wkv"], kn4, p["mem_norm"])
    token = ship(1, {"xa_wo": d_wo, "xa_wq": d_wq, "xa_wkv": d_wkv})
    dattn, dgm, raw["attn_out_norm"], raw["gmlp_out_norm"] = _mixer_out_bwd(
        dx1, attn, gm, p["w_out"], p["attn_out_norm"] + jnp.tile(token[0:1], (1, AW // 128)), p["gmlp_out_norm"])
    d_w_out = _mm_tn(ycat, dx1, "w_out_bwd_w")
    dgz, raw["gmlp_ws"], raw["gmlp_bs"], raw["gmlp_v_norm"] = _gmlp_bwd(dgm, gu, gvn, gz, ws, wst, bfull, p["gmlp_v_norm"])
    dq, dk, dv, raw["attn_sinks"] = _swa_bwd(q, k, v, dattn, p["attn_sinks"])
    grad_x, dproj, raw["mix_norm"], raw["q_norm"], raw["k_norm"] = _mixer_in_bwd(
        dq, dk, dv, dgz, qk, cos, sin, x, dx1, p["w_in"], p["mix_norm"], qn, kn)
    d_w_in = _mm_tn(h1, dproj, "w_in_bwd_w")
    return loss8[0, 0], grad_x, {"w_out": d_w_out, "w_in": d_w_in}, raw


def _cast_shards(shards):
    def body(*refs):
        n = len(refs) // 2
        for i_ref, o_ref in zip(refs[:n], refs[n:]):
            o_ref[...] = i_ref[0].astype(BF16)

    return pl.pallas_call(body, name="cast_shards", out_shape=[_sds(s.shape[1:], BF16) for s in shards],
                          compiler_params=pltpu.CompilerParams(vmem_limit_bytes=VMEM_LIMIT))(*shards)


HBM_SPEC = pl.BlockSpec(memory_space=pltpu.HBM)
SEM_SPEC = pl.BlockSpec(memory_space=pltpu.SEMAPHORE)


def _remote_copies(src_refs, land_refs, send_refs, recv_refs, nd):
    x, y, cc = lax.axis_index("x"), lax.axis_index("y"), lax.axis_index("c")
    me = 4 * x + 2 * y + cc
    copies = []
    for a, (src_ref, land_ref) in enumerate(zip(src_refs, land_refs)):
        for k in range(1, NDEV):
            px = 1 - x if k & 4 else x
            py = 1 - y if k & 2 else y
            pc = 1 - cc if k & 1 else cc
            copies.append(pltpu.make_async_remote_copy(
                src_ref=src_ref.at[4 * px + 2 * py + pc] if a < nd else src_ref, dst_ref=land_ref.at[me],
                send_sem=send_refs[a].at[k - 1], recv_sem=recv_refs[a].at[k - 1],
                device_id=(px, py, pc), device_id_type=pl.DeviceIdType.MESH))
    return copies


def _own_slot(src, by_dest, me):
    block = lax.dynamic_index_in_dim(src, me, 0, keepdims=True) if by_dest else src[None]
    return lax.dynamic_update_index_in_dim(lax.empty((NDEV,) + block.shape[1:], src.dtype), block, me, 0)


def _exchange_start(by_dest, for_all, me, name):
    srcs = list(by_dest) + list(for_all)
    n, nd = len(srcs), len(by_dest)
    lands = [_own_slot(s, a < nd, me) for a, s in enumerate(srcs)]

    def body(*refs):
        for cp in _remote_copies(refs[:n], refs[n:2 * n], refs[2 * n:3 * n], refs[3 * n:4 * n], nd):
            cp.start()
        refs[-1][...] = jnp.zeros((8, 128), F32)

    sems = [pltpu.SemaphoreType.DMA((NDEV - 1,))] * (2 * n)
    thru = [pltpu.HBM(v.shape, v.dtype) for v in srcs + lands]
    res = pl.pallas_call(
        body, name=name, out_shape=sems + thru + [_sds((8, 128))],
        in_specs=[HBM_SPEC] * (2 * n), out_specs=[SEM_SPEC] * (2 * n) + [HBM_SPEC] * (2 * n) + [pl.BlockSpec(memory_space=pltpu.VMEM)],
        input_output_aliases={i: 2 * n + i for i in range(2 * n)},
        compiler_params=pltpu.CompilerParams(has_side_effects=pltpu.SideEffectType.DATAFLOW_SIDE_EFFECTING))(
            *[pltpu.with_memory_space_constraint(v, pltpu.HBM) for v in srcs + lands])
    return (res[:2 * n], res[2 * n:4 * n], nd), res[-1]


def _exchange_wait(state, after, name):
    sems, thru, nd = state
    n = len(thru) // 2

    def body(*refs):
        for cp in _remote_copies(refs[:n], refs[n:2 * n], refs[2 * n:3 * n], refs[3 * n:4 * n], nd):
            cp.wait_send()
            cp.wait_recv()

    res = pl.pallas_call(
        body, name=name, out_shape=[pltpu.HBM(v.shape, v.dtype) for v in thru],
        in_specs=[HBM_SPEC] * (2 * n) + [SEM_SPEC] * (2 * n) + [pl.BlockSpec(memory_space=pl.ANY)],
        out_specs=[HBM_SPEC] * (2 * n), input_output_aliases={i: i for i in range(2 * n)},
        compiler_params=pltpu.CompilerParams(has_side_effects=pltpu.SideEffectType.DATAFLOW_SIDE_EFFECTING))(
            *thru, *sems, after)
    return res[n:]


def _adam(parts, w, m, v, name):
    def body(p_ref, w_ref, m_ref, v_ref, g_ref, d_ref, nm_ref, nv_ref):
        g = _sum_parts(p_ref)
        g_ref[0] = g
        d_ref[0], nm_ref[0], nv_ref[0] = _adam_math(g, w_ref[0], m_ref[0], v_ref[0])

    return pl.pallas_call(
        body, name=name, out_shape=[_sds(w.shape)] * 4,
        compiler_params=pltpu.CompilerParams(vmem_limit_bytes=VMEM_LIMIT))(parts, w, m, v)


GATHER_GROUPS = (("w_in", "w_out"), ("xa_wkv", "xa_wq", "xa_wo"), ("ffn_up", "ffn_conv", "ffn_down"))
SCATTER_GROUPS = (("ffn_down", "ffn_up", "ffn_conv"), ("xa_wo", "xa_wq", "xa_wkv"), ("w_out", "w_in"))
BIG = tuple(n for grp in GATHER_GROUPS for n in grp)
VECS = (("mix_norm", D), ("q_norm", HD), ("k_norm", HD), ("attn_sinks", 8), ("gmlp_v_norm", GW), ("attn_out_norm", AW),
        ("gmlp_out_norm", GW), ("xa_norm", D), ("mem_norm", D), ("xa_q_norm", XD), ("xa_k_norm", XD), ("ffn_norm", D))
BS_ROW = 16
VEC_ROWS = 24
SMALL = tuple(n for n, _ in VECS) + ("gmlp_bs", "gmlp_ws", "ffn_conv_b")


def _pack_small(raw):
    names = [n for n, _ in VECS] + ["gmlp_bs", "conv_sums"]

    def body(*refs):
        ins = dict(zip(names, refs))
        vec_ref, cb_ref = refs[len(names):]
        vec_ref[...] = jnp.zeros_like(vec_ref)
        for r, (n, w) in enumerate(VECS):
            vec_ref[r:r + 1, 0:w] = ins[n][0:1, 0:w]
        vec_ref[BS_ROW:BS_ROW + 8, 0:BLK] = ins["gmlp_bs"][...]
        for s in range(2):
            for d in range(NG):
                cb_ref[s, d] = ins["conv_sums"][s, d, 3:4, :]

    return pl.pallas_call(body, name="pack_small", out_shape=[_sds((VEC_ROWS, D)), _sds((2, NG, 1, SW))])(
        *[raw[n] for n in names])


def _adam_math(g, w, m, v):
    nm = B1 * m + (1.0 - B1) * g
    nv = B2 * v + (1.0 - B2) * (g * g)
    m_hat = nm / (1.0 - B1 ** STEP)
    v_hat = nv / (1.0 - B2 ** STEP)
    return -LR * (m_hat / (jnp.sqrt(v_hat) + AEPS) + WD * w), nm, nv


def _sum_parts(p_ref):
    g = p_ref[0].astype(F32)
    for j in range(1, NDEV):
        g = g + p_ref[j].astype(F32)
    return g


def _adam_small(parts_vec, parts_ws, parts_cb, w, m, v):
    def body(*refs):
        pv_ref, pws_ref, pcb_ref = refs[:3]
        ins = refs[3:3 + 3 * len(SMALL)]
        outs = refs[3 + 3 * len(SMALL):]
        gv = _sum_parts(pv_ref)
        for j, n in enumerate(SMALL):
            w_ref, m_ref, v_ref = ins[3 * j:3 * j + 3]
            o = outs[4 * j:4 * j + 4]
            if n == "gmlp_ws":
                g = _sum_parts(pws_ref)
            elif n == "ffn_conv_b":
                g = _sum_parts(pcb_ref)
            elif n == "gmlp_bs":
                g = gv[BS_ROW:BS_ROW + 8, 0:BLK]
            else:
                g = gv[j:j + 1, 0:VECS[j][1]]
            lead = n in ("gmlp_ws", "gmlp_bs")
            res = (g,) + _adam_math(g, w_ref[0] if lead else w_ref[...], m_ref[0] if lead else m_ref[...],
                                    v_ref[0] if lead else v_ref[...])
            for o_ref, val in zip(o, res):
                if lead:
                    o_ref[0] = val
                else:
                    o_ref[...] = val

    args = [parts_vec, parts_ws, parts_cb] + [d[n] for n in SMALL for d in (w, m, v)]
    res = pl.pallas_call(body, name="adam_small", out_shape=[_sds(w[n].shape) for n in SMALL for _ in range(4)],
                         compiler_params=pltpu.CompilerParams(vmem_limit_bytes=VMEM_LIMIT))(*args)
    return {n: tuple(res[4 * j:4 * j + 4]) for j, n in enumerate(SMALL)}


def kernel(x, mem, positions, mix_norm, w_in, q_norm, k_norm, attn_sinks, gmlp_v_norm, gmlp_ws, gmlp_bs, attn_out_norm, gmlp_out_norm, w_out, xa_norm, mem_norm, xa_wq, xa_wkv, xa_q_norm, xa_k_norm, xa_wo, ffn_norm, ffn_up, ffn_conv, ffn_conv_b, ffn_down, loss_target, m_mix_norm, m_w_in, m_q_norm, m_k_norm, m_attn_sinks, m_gmlp_v_norm, m_gmlp_ws, m_gmlp_bs, m_attn_out_norm, m_gmlp_out_norm, m_w_out, m_xa_norm, m_mem_norm, m_xa_wq, m_xa_wkv, m_xa_q_norm, m_xa_k_norm, m_xa_wo, m_ffn_norm, m_ffn_up, m_ffn_conv, m_ffn_conv_b, m_ffn_down, v_mix_norm, v_w_in, v_q_norm, v_k_norm, v_attn_sinks, v_gmlp_v_norm, v_gmlp_ws, v_gmlp_bs, v_attn_out_norm, v_gmlp_out_norm, v_w_out, v_xa_norm, v_mem_norm, v_xa_wq, v_xa_wkv, v_xa_q_norm, v_xa_k_norm, v_xa_wo, v_ffn_norm, v_ffn_up, v_ffn_conv, v_ffn_conv_b, v_ffn_down):
    names = ("mix_norm", "w_in", "q_norm", "k_norm", "attn_sinks", "gmlp_v_norm", "gmlp_ws", "gmlp_bs", "attn_out_norm",
             "gmlp_out_norm", "w_out", "xa_norm", "mem_norm", "xa_wq", "xa_wkv", "xa_q_norm", "xa_k_norm", "xa_wo",
             "ffn_norm", "ffn_up", "ffn_conv", "ffn_conv_b", "ffn_down")
    w = dict(zip(names, (mix_norm, w_in, q_norm, k_norm, attn_sinks, gmlp_v_norm, gmlp_ws, gmlp_bs, attn_out_norm,
                         gmlp_out_norm, w_out, xa_norm, mem_norm, xa_wq, xa_wkv, xa_q_norm, xa_k_norm, xa_wo, ffn_norm,
                         ffn_up, ffn_conv, ffn_conv_b, ffn_down)))
    m = dict(zip(names, (m_mix_norm, m_w_in, m_q_norm, m_k_norm, m_attn_sinks, m_gmlp_v_norm, m_gmlp_ws, m_gmlp_bs,
                         m_attn_out_norm, m_gmlp_out_norm, m_w_out, m_xa_norm, m_mem_norm, m_xa_wq, m_xa_wkv,
                         m_xa_q_norm, m_xa_k_norm, m_xa_wo, m_ffn_norm, m_ffn_up, m_ffn_conv, m_ffn_conv_b, m_ffn_down)))
    v = dict(zip(names, (v_mix_norm, v_w_in, v_q_norm, v_k_norm, v_attn_sinks, v_gmlp_v_norm, v_gmlp_ws, v_gmlp_bs,
                         v_attn_out_norm, v_gmlp_out_norm, v_w_out, v_xa_norm, v_mem_norm, v_xa_wq, v_xa_wkv,
                         v_xa_q_norm, v_xa_k_norm, v_xa_wo, v_ffn_norm, v_ffn_up, v_ffn_conv, v_ffn_conv_b, v_ffn_down)))
    t = x.shape[1]

    me = 4 * lax.axis_index("x") + 2 * lax.axis_index("y") + lax.axis_index("c")

    mats = [n for n in BIG if n != "ffn_conv"]
    shard = dict(zip(mats, _cast_shards([w[n] for n in mats])), ffn_conv=w["ffn_conv"][0])
    gathers, tokens = zip(*[_exchange_start([], [shard[n] for n in grp], me, "gather_start_%d" % i)
                            for i, grp in enumerate(GATHER_GROUPS)])

    def fetch(i, after):
        after = tokens[0] + tokens[1] + tokens[2] if after is None else after
        got = dict(zip(GATHER_GROUPS[i], _exchange_wait(gathers[i], after, "gather_wait_%d" % i)))
        if "w_in" in got:
            got["w_in"] = got["w_in"].transpose(1, 0, 2).reshape(D, IN)
        for n in ("w_out", "xa_wq", "xa_wo"):
            if n in got:
                got[n] = got[n].reshape(D, D)
        if "ffn_down" in got:
            got["ffn_down"] = got["ffn_down"].reshape(NG, SW, D)
            got["ffn_conv"] = got["ffn_conv"].reshape(2, NG, 3, SW)
        return got

    scatters = []

    def ship(i, grads, for_all=()):
        by_dest = [grads[n].reshape((NDEV,) + w[n].shape[1:]) for n in SCATTER_GROUPS[i]]
        state, token = _exchange_start(by_dest, for_all, me, "scatter_start_%d" % i)
        scatters.append(state)
        return token

    conv_b = {k: d["ffn_conv_b"].reshape(NDEV, 1, SW) for k, d in (("w", w), ("m", m), ("v", v))}
    p = {n: w[n] for n in SMALL[:-1]}
    p["gmlp_ws"], p["gmlp_bs"] = w["gmlp_ws"][0], w["gmlp_bs"][0]
    p["ffn_conv_b"] = conv_b["w"].reshape(2, NG, 1, SW)
    loss, grad_x, g, raw = _local_step(x[0], mem[0], positions.reshape(t, 1), loss_target[0], p, fetch, ship)
    loss = lax.psum(loss, AXES)

    vec, cb = _pack_small(raw)
    g["w_in"] = g["w_in"].reshape(D, NDEV, IN // NDEV).transpose(1, 0, 2)
    token = ship(2, g, [vec, raw["gmlp_ws"], cb.reshape(NDEV, 1, SW)])
    parts = {}
    for i, grp in enumerate(SCATTER_GROUPS):
        got = _exchange_wait(scatters[i], token, "scatter_wait_%d" % i)
        parts.update(zip(grp, got))
        rest = got[len(grp):]
    res = {n: _adam(parts[n], w[n], m[n], v[n], "adam_" + n) for n in BIG}
    small = lambda d, k: {**{n: d[n] for n in SMALL[:-1]}, "ffn_conv_b": conv_b[k]}
    res.update(_adam_small(*rest, small(w, "w"), small(m, "m"), small(v, "v")))

    outs = [loss, grad_x[None]]
    for j in range(4):
        outs += [res[n][j].reshape(w[n].shape) for n in names]
    return tuple(outs)
```

```python
import functools
import math

import jax
import jax.numpy as jnp
from jax import lax
from jax.experimental import pallas as pl
from jax.experimental.pallas import tpu as pltpu

F32 = jnp.float32
BF16 = jnp.bfloat16

D = 1024
HD = 64
AW = 512
KW = 128
GW = 512
IN = AW + 2 * KW + 2 * GW
BLK = 128
MEM = 256
XH = 4
XD = 256
FF = 2816
EPS = 1e-6
ROPE_THETA = 10000.0
NDEV = 8
LR, B1, B2, AEPS, WD, STEP = 0.001, 0.9, 0.999, 1e-08, 0.01, 10

TM = 512
VMEM_LIMIT = 56 * 1024 * 1024
NEG = float(jnp.finfo(jnp.float32).min)
GELU_C0 = math.sqrt(2.0 / math.pi)
GELU_C1 = 0.044715
AXES = ("x", "y", "c")


def _dot(a, b):
    return jnp.dot(a, b, preferred_element_type=F32)


def _dot_nt(a, b):
    return lax.dot_general(a, b, (((1,), (1,)), ((), ())), preferred_element_type=F32)


def _dot_tn(a, b):
    return lax.dot_general(a, b, (((0,), (0,)), ((), ())), preferred_element_type=F32)


def _rs(x):
    return lax.rsqrt(jnp.mean(x * x, axis=-1, keepdims=True) + EPS)


def _rms_bwd(dy, x, r, g):
    xh = x * r
    dxh = dy * g
    dx = r * (dxh - xh * jnp.mean(dxh * xh, axis=-1, keepdims=True))
    return dx, dy * xh


def _lane(shape):
    return lax.broadcasted_iota(jnp.int32, shape, len(shape) - 1)


def _gsum64(v, ones_ref):
    w = v.shape[-1]
    ones = ones_ref[0:w, 0:w]
    hi = v.astype(BF16)
    lo = (v - hi.astype(F32)).astype(BF16)
    return _dot(hi, ones) + _dot(lo, ones)


def _head_ones():
    i = jnp.arange(AW) // HD
    return (i[:, None] == i[None, :]).astype(BF16)


def _rs64(x, ones_ref):
    return lax.rsqrt(_gsum64(x * x, ones_ref) * (1.0 / HD) + EPS)


def _rms64_bwd(dy, x, r, g, ones_ref):
    xh = x * r
    dxh = dy * g
    dx = r * (dxh - xh * (_gsum64(dxh * xh, ones_ref) * (1.0 / HD)))
    return dx, dy * xh


def _rot_half(v):
    w = v.shape[-1]
    return jnp.where((_lane(v.shape) & 32) == 0, pltpu.roll(v, w - 32, 1), pltpu.roll(v, 32, 1))


def _rope(v, cos, sin_signed):
    return v * cos + _rot_half(v) * sin_signed


def _rope_bwd(dv, cos, sin_signed):
    return dv * cos + _rot_half(dv * sin_signed)


def _gelu(z):
    return 0.5 * z * (1.0 + jnp.tanh(GELU_C0 * (z + GELU_C1 * z * z * z)))


def _gelu_grad(z):
    t = jnp.tanh(GELU_C0 * (z + GELU_C1 * z * z * z))
    return 0.5 * (1.0 + t) + 0.5 * z * (1.0 - t * t) * (GELU_C0 * (1.0 + 3.0 * GELU_C1 * z * z))


def _colsum8(v):
    s = jnp.sum(v, axis=0, keepdims=True)
    row = lax.broadcasted_iota(jnp.int32, (8, v.shape[1]), 0)
    return jnp.where(row == 0, jnp.broadcast_to(s, (8, v.shape[1])), 0.0)


def _params(n_axes=1):
    return pltpu.CompilerParams(dimension_semantics=("arbitrary",) * n_axes, vmem_limit_bytes=VMEM_LIMIT)


def _rows(tm, w):
    return pl.BlockSpec((tm, w), lambda i: (i, 0))


def _const(shape):
    nd = len(shape)
    return pl.BlockSpec(shape, lambda *_: (0,) * nd)


def _sds(shape, dtype=F32):
    return jax.ShapeDtypeStruct(shape, dtype)


def _mm_tn(a, b, name):
    g = max(a.shape[0] if a.ndim == 3 else 1, b.shape[0] if b.ndim == 3 else 1)
    t, m = a.shape[-2:]
    n = b.shape[-1]

    def body(a_ref, b_ref, o_ref, acc_ref):
        i = pl.program_id(1)

        @pl.when(i == 0)
        def _():
            acc_ref[...] = jnp.zeros_like(acc_ref)

        acc_ref[...] += _dot_tn(a_ref[...].astype(BF16), b_ref[...].astype(BF16))

        @pl.when(i == pl.num_programs(1) - 1)
        def _():
            o_ref[...] = acc_ref[...].astype(BF16)

    def spec(v):
        w = v.shape[-1]
        if v.ndim == 3:
            return pl.BlockSpec((None, TM, w), lambda j, i: (j, i, 0))
        return pl.BlockSpec((TM, w), lambda j, i: (i, 0))

    return pl.pallas_call(
        body, name=name, grid=(g, t // TM), in_specs=[spec(a), spec(b)],
        out_specs=pl.BlockSpec((None, m, n), lambda j, i: (j, 0, 0)), out_shape=_sds((g, m, n), BF16),
        scratch_shapes=[pltpu.VMEM((m, n), F32)], compiler_params=_params(2))(a, b)


def _rope_tables(pos, inv_freq):
    t = pos.shape[0]

    def body(pos_ref, f_ref, cos_ref, sin_ref):
        ang = pos_ref[...].astype(F32) * f_ref[...]
        sign = jnp.where((_lane(ang.shape) & 32) == 0, -1.0, 1.0)
        cos_ref[...] = jnp.cos(ang)
        sin_ref[...] = jnp.sin(ang) * sign

    return pl.pallas_call(
        body, name="rope_tables", grid=(t // TM,),
        in_specs=[_rows(TM, 1), _const((1, 128))], out_specs=[_rows(TM, 128), _rows(TM, 128)],
        out_shape=[_sds((t, 128)), _sds((t, 128))], compiler_params=_params())(pos, inv_freq)


def _mixer_in_fwd(x, mix_norm, w_in, qn, kn, gvw, cos, sin):
    t = x.shape[0]

    def body(x_ref, g_ref, w_ref, qn_ref, kn_ref, gvw_ref, cos_ref, sin_ref, ones_ref,
             h_ref, qk_ref, gz_ref, q_ref, k_ref, v_ref, gu_ref, gvn_ref):
        x = x_ref[...]
        h = (x * _rs(x) * g_ref[...]).astype(BF16)
        h_ref[...] = h
        proj = _dot(h, w_ref[...])
        qk = proj[:, :AW + KW]
        qk_ref[...] = qk
        gz = proj[:, AW + 2 * KW:]
        gz_ref[...] = gz
        cos2, sin2 = cos_ref[...], sin_ref[...]
        q = qk[:, :AW]
        q = q * _rs64(q, ones_ref) * qn_ref[...]
        q_ref[...] = _rope(q, jnp.tile(cos2, (1, 4)), jnp.tile(sin2, (1, 4))).astype(BF16)
        k = qk[:, AW:]
        k = k * _rs64(k, ones_ref) * kn_ref[...]
        k_ref[...] = _rope(k, cos2, sin2).astype(BF16)
        v_ref[...] = proj[:, AW + KW:AW + 2 * KW].astype(BF16)
        gu_ref[...] = _gelu(gz[:, :GW])
        gv = _gelu(gz[:, GW:])
        gvn_ref[...] = (gv * _rs(gv) * gvw_ref[...]).astype(BF16)

    return pl.pallas_call(
        body, name="mixer_in_fwd", grid=(t // TM,),
        in_specs=[_rows(TM, D), _const((1, D)), _const((D, IN)), _const((1, AW)), _const((1, KW)),
                  _const((1, GW)), _rows(TM, 128), _rows(TM, 128), _const((AW, AW))],
        out_specs=[_rows(TM, D), _rows(TM, AW + KW), _rows(TM, 2 * GW), _rows(TM, AW), _rows(TM, KW),
                   _rows(TM, KW), _rows(TM, GW), _rows(TM, GW)],
        out_shape=[_sds((t, D), BF16), _sds((t, AW + KW)), _sds((t, 2 * GW)), _sds((t, AW), BF16),
                   _sds((t, KW), BF16), _sds((t, KW), BF16), _sds((t, GW)), _sds((t, GW), BF16)],
        compiler_params=_params())(x, mix_norm, w_in, qn, kn, gvw, cos, sin, _head_ones())


def _dup_half(kk, g):
    lane = _lane(kk.shape)
    other = pltpu.roll(kk, 64, 1)
    keep = (lane < 64) if g == 0 else (lane >= 64)
    return jnp.where(keep, kk, other).astype(BF16)


def _swa_mask(first_block):
    qi = lax.broadcasted_iota(jnp.int32, (4 * BLK, 2 * BLK), 0) & (BLK - 1)
    kj = lax.broadcasted_iota(jnp.int32, (4 * BLK, 2 * BLK), 1)
    diff = qi + BLK - kj
    band = (diff >= 0) & (diff < BLK)
    return band & (jnp.logical_not(first_block) | (kj >= BLK))


def _stack_heads(a2, b2):
    lo = _lane(a2.shape) < 64
    z = jnp.zeros_like(a2)
    return jnp.concatenate([jnp.where(lo, a2, z), jnp.where(lo, z, a2), jnp.where(lo, b2, z), jnp.where(lo, z, b2)], axis=0)


def _unstack_heads(o):
    lo = _lane((BLK, 128)) < 64
    return jnp.where(lo, o[0:BLK], o[BLK:2 * BLK]), jnp.where(lo, o[2 * BLK:3 * BLK], o[3 * BLK:4 * BLK])


def _sink_col(sink_ref, g):
    row = lax.broadcasted_iota(jnp.int32, (4 * BLK, 1), 0)
    s = [sink_ref[0, 4 * g + j] for j in range(4)]
    return jnp.where(row < BLK, s[0], jnp.where(row < 2 * BLK, s[1], jnp.where(row < 3 * BLK, s[2], s[3])))


def _swa_probs(qs, kd, mask, sink):
    s = _dot_nt(qs, kd) * (1.0 / math.sqrt(HD))
    s = jnp.where(mask, s, NEG)
    m = jnp.maximum(jnp.max(s, axis=-1, keepdims=True), sink)
    p = jnp.exp(s - m)
    ps = jnp.exp(sink - m)
    inv = 1.0 / (jnp.sum(p, axis=-1, keepdims=True) + ps)
    return p * inv, ps * inv


def _swa_fwd(q, k, v, sinks):
    t = q.shape[0]
    nb = t // BLK

    def body(sink_ref, q_ref, kc_ref, kp_ref, vc_ref, vp_ref, o_ref):
        i = pl.program_id(0)
        mask = _swa_mask(i == 0)
        kk = jnp.concatenate([kp_ref[...], kc_ref[...]], axis=0).astype(F32)
        vv = jnp.concatenate([vp_ref[...], vc_ref[...]], axis=0).astype(F32)
        for g in range(2):
            qs = _stack_heads(q_ref[:, 256 * g:256 * g + 128], q_ref[:, 256 * g + 128:256 * g + 256])
            pn, _ = _swa_probs(qs, _dup_half(kk, g), mask, _sink_col(sink_ref, g))
            oa, ob = _unstack_heads(_dot(pn.astype(BF16), _dup_half(vv, g)))
            o_ref[:, 256 * g:256 * g + 128] = oa
            o_ref[:, 256 * g + 128:256 * g + 256] = ob

    cur = lambda i: (i, 0)
    prev = lambda i: (jnp.maximum(i - 1, 0), 0)
    return pl.pallas_call(
        body, name="swa_fwd", grid=(nb,),
        in_specs=[pl.BlockSpec(memory_space=pltpu.SMEM), pl.BlockSpec((BLK, AW), cur),
                  pl.BlockSpec((BLK, KW), cur), pl.BlockSpec((BLK, KW), prev),
                  pl.BlockSpec((BLK, KW), cur), pl.BlockSpec((BLK, KW), prev)],
        out_specs=pl.BlockSpec((BLK, AW), cur), out_shape=_sds((t, AW)),
        compiler_params=_params())(sinks, q, k, k, v, v)


def _causal_bf16(w_ref, h, transposed):
    r = lax.broadcasted_iota(jnp.int32, (BLK, BLK), 0)
    c = lax.broadcasted_iota(jnp.int32, (BLK, BLK), 1)
    keep = (r <= c) if transposed else (c <= r)
    return jnp.where(keep, w_ref[h], 0.0).astype(BF16)


def _gmlp_mix(w_ref, xin, transposed):
    lo = _lane((BLK, 128)) < 64
    wm = [_causal_bf16(w_ref, h, transposed) for h in range(8)]
    rows = []
    for c in range(xin.shape[0] // BLK):
        cols = []
        for j in range(4):
            xs = xin[c * BLK:(c + 1) * BLK, 128 * j:128 * (j + 1)]
            cols.append(jnp.where(lo, _dot(wm[2 * j], xs), _dot(wm[2 * j + 1], xs)))
        rows.append(jnp.concatenate(cols, axis=1))
    return jnp.concatenate(rows, axis=0)


def _gmlp_fwd(gvn, gu, ws, bfull):
    t = gvn.shape[0]

    def body(x_ref, gu_ref, w_ref, b_ref, o_ref):
        mixed = _gmlp_mix(w_ref, x_ref[...], False) + jnp.tile(b_ref[...], (TM // BLK, 1))
        o_ref[...] = gu_ref[...] * mixed

    return pl.pallas_call(
        body, name="gmlp_fwd", grid=(t // TM,),
        in_specs=[_rows(TM, GW), _rows(TM, GW), _const((8, BLK, BLK)), _const((BLK, GW))],
        out_specs=_rows(TM, GW), out_shape=_sds((t, GW)), compiler_params=_params())(gvn, gu, ws, bfull)


def _mixer_out_fwd(attn, gm, x, w_out, aon, gon, xan):
    t = x.shape[0]

    def body(a_ref, g_ref, x_ref, w_ref, aon_ref, gon_ref, xan_ref, y_ref, x1_ref, h2_ref):
        a, g = a_ref[...], g_ref[...]
        y = jnp.concatenate([a * _rs(a) * aon_ref[...], g * _rs(g) * gon_ref[...]], axis=1).astype(BF16)
        y_ref[...] = y
        x1 = x_ref[...] + _dot(y, w_ref[...])
        x1_ref[...] = x1
        h2_ref[...] = (x1 * _rs(x1) * xan_ref[...]).astype(BF16)

    return pl.pallas_call(
        body, name="mixer_out_fwd", grid=(t // TM,),
        in_specs=[_rows(TM, AW), _rows(TM, GW), _rows(TM, D), _const((D, D)), _const((1, AW)), _const((1, GW)),
                  _const((1, D))],
        out_specs=[_rows(TM, D), _rows(TM, D), _rows(TM, D)],
        out_shape=[_sds((t, D), BF16), _sds((t, D)), _sds((t, D), BF16)],
        compiler_params=_params())(attn, gm, x, w_out, aon, gon, xan)


def _mem_kv_fwd(mem, mem_norm, wkv, kn4):
    def body(m_ref, g_ref, w_ref, kn_ref, mh_ref, kpre_ref, k_ref, v_ref):
        m = m_ref[...]
        mh = (m * _rs(m) * g_ref[...]).astype(BF16)
        mh_ref[...] = mh
        for h in range(XH):
            sl = slice(XD * h, XD * (h + 1))
            kh = _dot(mh, w_ref[h])
            kpre_ref[:, sl] = kh
            k_ref[:, sl] = (kh * _rs(kh) * kn_ref[:, sl]).astype(BF16)
            v_ref[:, sl] = _dot(mh, w_ref[XH + h]).astype(BF16)

    return pl.pallas_call(
        body, name="mem_kv_fwd",
        out_shape=[_sds((MEM, D), BF16), _sds((MEM, D)), _sds((MEM, D), BF16), _sds((MEM, D), BF16)],
        compiler_params=pltpu.CompilerParams(vmem_limit_bytes=VMEM_LIMIT))(mem, mem_norm, wkv, kn4)


def _xattn_probs(qpre_h, qn_h, k_h):
    rq = _rs(qpre_h)
    q2 = (qpre_h * rq * qn_h).astype(BF16)
    s = _dot_nt(q2, k_h) * (1.0 / math.sqrt(XD))
    p = jnp.exp(s - jnp.max(s, axis=-1, keepdims=True))
    return p * (1.0 / jnp.sum(p, axis=-1, keepdims=True)), q2, rq


def _xattn_fwd(h2, x1, wq, qn4, k2, v2, wo, ffn_norm):
    t = x1.shape[0]

    def body(h_ref, x_ref, wq_ref, qn_ref, k_ref, v_ref, wo_ref, fn_ref, qpre_ref, o_ref, x2_ref, h3_ref):
        qpre = _dot(h_ref[...], wq_ref[...])
        qpre_ref[...] = qpre
        outs = []
        for h in range(XH):
            sl = slice(XD * h, XD * (h + 1))
            pn, _, _ = _xattn_probs(qpre[:, sl], qn_ref[:, sl], k_ref[:, sl])
            outs.append(_dot(pn.astype(BF16), v_ref[:, sl]))
        o = jnp.concatenate(outs, axis=1).astype(BF16)
        o_ref[...] = o
        x2 = x_ref[...] + _dot(o, wo_ref[...])
        x2_ref[...] = x2
        h3_ref[...] = (x2 * _rs(x2) * fn_ref[...]).astype(BF16)

    return pl.pallas_call(
        body, name="xattn_fwd", grid=(t // TM,),
        in_specs=[_rows(TM, D), _rows(TM, D), _const((D, D)), _const((1, D)), _const((MEM, D)), _const((MEM, D)),
                  _const((D, D)), _const((1, D))],
        out_specs=[_rows(TM, D)] * 4,
        out_shape=[_sds((t, D)), _sds((t, D), BF16), _sds((t, D)), _sds((t, D), BF16)],
        compiler_params=_params())(h2, x1, wq, qn4, k2, v2, wo, ffn_norm)


SW = 704
NG = FF // SW
FM = 256
HALO = 16


def _resident(shape):
    nd = len(shape)
    return pl.BlockSpec(shape, lambda *_: (0,) * nd, pipeline_mode=pl.Buffered(1))


def _halo_before(i):
    return jnp.maximum(i * (FM // HALO) - 1, 0)


def _ffn_fwd(h3, x2, target, up, conv, conv_b, down):
    t = x2.shape[0]

    def body(h_ref, hp_ref, x_ref, t_ref, up_ref, w_ref, b_ref, dn_ref, a_ref, u_ref, dy_ref, loss_ref, acc_ref):
        i = pl.program_id(0)

        @pl.when(i == 0)
        def _():
            acc_ref[...] = jnp.zeros_like(acc_ref)

        before = jnp.where(i > 0, hp_ref[...], jnp.zeros_like(hp_ref))
        he = jnp.concatenate([before, h_ref[...]], axis=0)
        err = x_ref[...] - t_ref[...]
        for d in range(NG):
            c = []
            for s in range(2):
                a = _dot(he, up_ref[s * NG + d]).astype(BF16)
                a_ref[s * NG + d] = a[HALO:]
                a = a.astype(F32)
                w = w_ref[s, d]
                cc = w[2:3, :] * a + w[1:2, :] * pltpu.roll(a, 1, 0) + w[0:1, :] * pltpu.roll(a, 2, 0)
                c.append(cc[HALO:] + b_ref[s, d])
            u = (_gelu(c[0]) * c[1]).astype(BF16)
            u_ref[d] = u
            err = err + _dot(u, dn_ref[d])
        dy_ref[...] = err * (1.0 / D)
        acc_ref[...] += jnp.sum(err * err, axis=0, keepdims=True)

        @pl.when(i == pl.num_programs(0) - 1)
        def _():
            loss_ref[...] = jnp.full((8, 128), 0.5 / D, F32) * jnp.sum(acc_ref[...])

    return pl.pallas_call(
        body, name="ffn_fwd", grid=(t // FM,),
        in_specs=[_rows(FM, D), pl.BlockSpec((HALO, D), lambda i: (_halo_before(i), 0)), _rows(FM, D), _rows(FM, D),
                  _resident((NDEV, D, SW)), _resident((2, NG, 3, SW)), _resident((2, NG, 1, SW)), _resident((NG, SW, D))],
        out_specs=[pl.BlockSpec((NDEV, FM, SW), lambda i: (0, i, 0)), pl.BlockSpec((NG, FM, SW), lambda i: (0, i, 0)),
                   _rows(FM, D), _const((8, 128))],
        out_shape=[_sds((NDEV, t, SW), BF16), _sds((NG, t, SW), BF16), _sds((t, D)), _sds((8, 128))],
        scratch_shapes=[pltpu.VMEM((1, D), F32)], compiler_params=_params())(h3, h3, x2, target, up, conv, conv_b, down)


def _gelu_and_grad(z):
    z2 = z * z
    t = jnp.tanh(GELU_C0 * (z + GELU_C1 * z * z2))
    phi = 0.5 * (1.0 + t)
    return z * phi, phi + 0.5 * z * (1.0 - t * t) * (GELU_C0 * (1.0 + 3.0 * GELU_C1 * z2))


def _ffn_bwd(dy, a, x2, up, conv, conv_b, down, ffn_norm):
    t = x2.shape[0]
    nt = t // FM
    n = FM + HALO

    def body(dy_ref, dyn_ref, a_ref, ap_ref, an_ref, x_ref, up_ref, w_ref, b_ref, dn_ref, g_ref,
             dx_ref, da_ref, s_ref, dfn_ref):
        i = pl.program_id(0)

        @pl.when(i == 0)
        def _():
            s_ref[...] = jnp.zeros_like(s_ref)
            dfn_ref[...] = jnp.zeros_like(dfn_ref)

        first, last = i == 0, i == nt - 1
        dy = dy_ref[...]
        dye = jnp.concatenate([dy, jnp.where(last, 0.0, dyn_ref[...])], axis=0).astype(BF16)
        dh = jnp.zeros((FM, D), F32)
        row = lax.broadcasted_iota(jnp.int32, (8, SW), 0)
        for d in range(NG):
            du = _dot_nt(dye, dn_ref[d])
            taps, c = [], []
            for s in range(2):
                j = s * NG + d
                before = jnp.where(first, jnp.zeros_like(ap_ref[j]), ap_ref[j])
                after = jnp.where(last, jnp.zeros_like(an_ref[j]), an_ref[j])
                e = jnp.concatenate([before, a_ref[j], after], axis=0).astype(F32)
                e1, e2 = pltpu.roll(e, 1, 0), pltpu.roll(e, 2, 0)
                w = w_ref[s, d]
                c.append((w[2:3, :] * e + w[1:2, :] * e1 + w[0:1, :] * e2)[HALO:] + b_ref[s, d])
                taps.append((e2[HALO:HALO + FM], e1[HALO:HALO + FM], e[HALO:HALO + FM]))
            gl, gg = _gelu_and_grad(c[0])
            dcs = (du * c[1] * gg, du * gl)
            for s in range(2):
                j = s * NG + d
                dc, w = dcs[s], w_ref[s, d]
                da = (w[2:3, :] * dc + w[1:2, :] * pltpu.roll(dc, n - 1, 0) + w[0:1, :] * pltpu.roll(dc, n - 2, 0))
                da = da[0:FM].astype(BF16)
                da_ref[j] = da
                dh = dh + _dot_nt(da, up_ref[j])
                dc0 = dc[0:FM]
                sums = [jnp.sum(dc0 * tap, axis=0, keepdims=True) for tap in taps[s]]
                sums.append(jnp.sum(dc0, axis=0, keepdims=True))
                upd = jnp.zeros((8, SW), F32)
                for r, v in enumerate(sums):
                    upd = jnp.where(row == r, jnp.broadcast_to(v, (8, SW)), upd)
                s_ref[s, d] += upd
        x = x_ref[...]
        dx, dg = _rms_bwd(dh, x, _rs(x), g_ref[...])
        dx_ref[...] = dy + dx
        dfn_ref[...] += _colsum8(dg)

    last_halo = t // HALO - 1
    after = lambda i: jnp.minimum((i + 1) * (FM // HALO), last_halo)
    return pl.pallas_call(
        body, name="ffn_bwd", grid=(nt,),
        in_specs=[_rows(FM, D), pl.BlockSpec((HALO, D), lambda i: (after(i), 0)),
                  pl.BlockSpec((NDEV, FM, SW), lambda i: (0, i, 0)),
                  pl.BlockSpec((NDEV, HALO, SW), lambda i: (0, _halo_before(i), 0)),
                  pl.BlockSpec((NDEV, HALO, SW), lambda i: (0, after(i), 0)),
                  _rows(FM, D), _resident((NDEV, D, SW)), _resident((2, NG, 3, SW)), _resident((2, NG, 1, SW)),
                  _resident((NG, SW, D)), _const((1, D))],
        out_specs=[_rows(FM, D), pl.BlockSpec((NDEV, FM, SW), lambda i: (0, i, 0)), _const((2, NG, 8, SW)), _const((8, D))],
        out_shape=[_sds((t, D)), _sds((NDEV, t, SW), BF16), _sds((2, NG, 8, SW)), _sds((8, D))],
        compiler_params=_params())(dy, dy, a, a, a, x2, up, conv, conv_b, down, ffn_norm)


BT = 256


def _xattn_bwd(dx2, x1, qpre, k2, v2, wq, wo, qn4, xan):
    t = x1.shape[0]

    def body(dx2_ref, x1_ref, qpre_ref, k_ref, v_ref, wq_ref, wo_ref, qn_ref, xan_ref,
             dx1_ref, dqpre_ref, dk_ref, dv_ref, dqn_ref, dxan_ref):
        @pl.when(pl.program_id(0) == 0)
        def _():
            for r in (dk_ref, dv_ref, dqn_ref, dxan_ref):
                r[...] = jnp.zeros_like(r)

        dx2 = dx2_ref[...]
        do = _dot_nt(dx2.astype(BF16), wo_ref[...])
        dqs = []
        for h in range(XH):
            sl = slice(XD * h, XD * (h + 1))
            qpre_h = qpre_ref[:, sl]
            pn, q2, rq = _xattn_probs(qpre_h, qn_ref[:, sl], k_ref[:, sl])
            do_h = do[:, sl].astype(BF16)
            dp = _dot_nt(do_h, v_ref[:, sl])
            ds = (pn * (dp - jnp.sum(pn * dp, axis=-1, keepdims=True)) * (1.0 / math.sqrt(XD))).astype(BF16)
            dq2 = _dot(ds, k_ref[:, sl])
            dk_ref[:, sl] += _dot_tn(ds, q2)
            dv_ref[:, sl] += _dot_tn(pn.astype(BF16), do_h)
            dqh, dg = _rms_bwd(dq2, qpre_h, rq, qn_ref[:, sl])
            dqn_ref[...] += _colsum8(dg)
            dqs.append(dqh)
        dqpre = jnp.concatenate(dqs, axis=1).astype(BF16)
        dqpre_ref[...] = dqpre
        dh2 = _dot_nt(dqpre, wq_ref[...])
        x1 = x1_ref[...]
        dx, dg = _rms_bwd(dh2, x1, _rs(x1), xan_ref[...])
        dx1_ref[...] = dx2 + dx
        dxan_ref[...] += _colsum8(dg)

    return pl.pallas_call(
        body, name="xattn_bwd", grid=(t // BT,),
        in_specs=[_rows(BT, D), _rows(BT, D), _rows(BT, D), _const((MEM, D)), _const((MEM, D)), _const((D, D)),
                  _const((D, D)), _const((1, D)), _const((1, D))],
        out_specs=[_rows(BT, D), _rows(BT, D), _const((MEM, D)), _const((MEM, D)), _const((8, XD)), _const((8, D))],
        out_shape=[_sds((t, D)), _sds((t, D), BF16), _sds((MEM, D)), _sds((MEM, D)), _sds((8, XD)), _sds((8, D))],
        compiler_params=_params())(dx2, x1, qpre, k2, v2, wq, wo, qn4, xan)


def _mem_kv_bwd(mem, mh, kpre, dk2, dv2, wkv, kn4, mem_norm):
    def body(m_ref, mh_ref, kpre_ref, dk_ref, dv_ref, w_ref, kn_ref, g_ref, dw_ref, dkn_ref, dmn_ref):
        dkn = jnp.zeros((8, XD), F32)
        dm = jnp.zeros((MEM, D), F32)
        mh = mh_ref[...]
        for h in range(XH):
            sl = slice(XD * h, XD * (h + 1))
            kh = kpre_ref[:, sl]
            dkh, dg = _rms_bwd(dk_ref[:, sl], kh, _rs(kh), kn_ref[:, sl])
            dkn = dkn + _colsum8(dg)
            dkh = dkh.astype(BF16)
            dvh = dv_ref[:, sl].astype(BF16)
            dw_ref[h] = _dot_tn(mh, dkh).astype(BF16)
            dw_ref[XH + h] = _dot_tn(mh, dvh).astype(BF16)
            dm = dm + _dot_nt(dkh, w_ref[h]) + _dot_nt(dvh, w_ref[XH + h])
        dkn_ref[...] = dkn
        m = m_ref[...]
        _, dg = _rms_bwd(dm, m, _rs(m), g_ref[...])
        dmn_ref[...] = _colsum8(dg)

    return pl.pallas_call(
        body, name="mem_kv_bwd", out_shape=[_sds((2 * XH, D, XD), BF16), _sds((8, XD)), _sds((8, D))],
        compiler_params=pltpu.CompilerParams(vmem_limit_bytes=VMEM_LIMIT))(mem, mh, kpre, dk2, dv2, wkv, kn4, mem_norm)


def _mixer_out_bwd(dx1, attn, gm, w_out, aon, gon):
    t = dx1.shape[0]

    def body(dx_ref, a_ref, g_ref, w_ref, aon_ref, gon_ref, da_ref, dg_ref, dan_ref, dgn_ref):
        @pl.when(pl.program_id(0) == 0)
        def _():
            dan_ref[...] = jnp.zeros_like(dan_ref)
            dgn_ref[...] = jnp.zeros_like(dgn_ref)

        dy = _dot_nt(dx_ref[...].astype(BF16), w_ref[...])
        a, g = a_ref[...], g_ref[...]
        da, dna = _rms_bwd(dy[:, :AW], a, _rs(a), aon_ref[...])
        dg, dng = _rms_bwd(dy[:, AW:], g, _rs(g), gon_ref[...])
        da_ref[...] = da
        dg_ref[...] = dg
        dan_ref[...] += _colsum8(dna)
        dgn_ref[...] += _colsum8(dng)

    return pl.pallas_call(
        body, name="mixer_out_bwd", grid=(t // TM,),
        in_specs=[_rows(TM, D), _rows(TM, AW), _rows(TM, GW), _const((D, D)), _const((1, AW)), _const((1, GW))],
        out_specs=[_rows(TM, AW), _rows(TM, GW), _const((8, AW)), _const((8, GW))],
        out_shape=[_sds((t, AW)), _sds((t, GW)), _sds((8, AW)), _sds((8, GW))],
        compiler_params=_params())(dx1, attn, gm, w_out, aon, gon)


def _gmlp_bwd(dgm, gu, gvn, gz, ws, wst, bfull, gvw):
    t = dgm.shape[0]
    nc = TM // BLK

    def body(dgm_ref, gu_ref, x_ref, gz_ref, w_ref, wt_ref, b_ref, gvw_ref, dgz_ref, dw_ref, db_ref, dgvw_ref,
             dbacc_ref):
        @pl.when(pl.program_id(0) == 0)
        def _():
            for r in (dw_ref, dbacc_ref, dgvw_ref):
                r[...] = jnp.zeros_like(r)

        xin = x_ref[...]
        dgm = dgm_ref[...]
        mixed = _gmlp_mix(w_ref, xin, False) + jnp.tile(b_ref[...], (nc, 1))
        dgu = dgm * mixed
        dmixed = dgm * gu_ref[...]
        lo = _lane((BLK, 128)) < 64
        dbias = jnp.zeros((BLK, GW), F32)
        for c in range(nc):
            dmc = dmixed[c * BLK:(c + 1) * BLK]
            dbias = dbias + dmc
            for j in range(4):
                dm2 = dmc[:, 128 * j:128 * (j + 1)]
                xs = xin[c * BLK:(c + 1) * BLK, 128 * j:128 * (j + 1)]
                z = jnp.zeros_like(dm2)
                dw_ref[2 * j] += _dot_nt(jnp.where(lo, dm2, z).astype(BF16), xs)
                dw_ref[2 * j + 1] += _dot_nt(jnp.where(lo, z, dm2).astype(BF16), xs)
        dbacc_ref[...] += dbias
        dgvn = _gmlp_mix(wt_ref, dmixed.astype(BF16), True)
        gz_u, gz_v = gz_ref[:, :GW], gz_ref[:, GW:]
        gv = _gelu(gz_v)
        dgv, dg = _rms_bwd(dgvn, gv, _rs(gv), gvw_ref[...])
        dgvw_ref[...] += _colsum8(dg)
        dgz_ref[:, :GW] = (dgu * _gelu_grad(gz_u)).astype(BF16)
        dgz_ref[:, GW:] = (dgv * _gelu_grad(gz_v)).astype(BF16)

        @pl.when(pl.program_id(0) == pl.num_programs(0) - 1)
        def _():
            s = dbacc_ref[...]
            sel = (lax.broadcasted_iota(jnp.int32, (8, GW), 1) // HD
                   == lax.broadcasted_iota(jnp.int32, (8, GW), 0)).astype(BF16)
            hi = s.astype(BF16)
            r1 = s - hi.astype(F32)
            mid = r1.astype(BF16)
            lo = (r1 - mid.astype(F32)).astype(BF16)
            db_ref[...] = _dot_nt(sel, hi) + _dot_nt(sel, mid) + _dot_nt(sel, lo)
            r = lax.broadcasted_iota(jnp.int32, (BLK, BLK), 0)
            c = lax.broadcasted_iota(jnp.int32, (BLK, BLK), 1)
            for h in range(8):
                dw_ref[h] = jnp.where(c <= r, dw_ref[h], 0.0)

    return pl.pallas_call(
        body, name="gmlp_bwd", grid=(t // TM,),
        in_specs=[_rows(TM, GW), _rows(TM, GW), _rows(TM, GW), _rows(TM, 2 * GW), _const((8, BLK, BLK)),
                  _const((8, BLK, BLK)), _const((BLK, GW)), _const((1, GW))],
        out_specs=[_rows(TM, 2 * GW), _const((8, BLK, BLK)), _const((8, BLK)), _const((8, GW))],
        out_shape=[_sds((t, 2 * GW), BF16), _sds((8, BLK, BLK)), _sds((8, BLK)), _sds((8, GW))],
        scratch_shapes=[pltpu.VMEM((BLK, GW), F32)],
        compiler_params=_params())(dgm, gu, gvn, gz, ws, wst, bfull, gvw)


def _fold_half(v):
    return v + pltpu.roll(v, 64, 1)


def _swa_bwd(q, k, v, dattn, sinks):
    t = q.shape[0]
    nb = t // BLK

    def body(sink_ref, q_ref, kc_ref, kp_ref, vc_ref, vp_ref, do_ref, dq_ref, dk_ref, dv_ref, ds_ref,
             ck_ref, cv_ref, sacc_ref):
        i = pl.program_id(0)

        @pl.when(i == 0)
        def _():
            ck_ref[...] = jnp.zeros_like(ck_ref)
            cv_ref[...] = jnp.zeros_like(cv_ref)
            sacc_ref[...] = jnp.zeros_like(sacc_ref)

        @pl.when(i < nb)
        def _():
            mask = _swa_mask(i == 0)
            kk = jnp.concatenate([kp_ref[...], kc_ref[...]], axis=0).astype(F32)
            vv = jnp.concatenate([vp_ref[...], vc_ref[...]], axis=0).astype(F32)
            lo256 = _lane((2 * BLK, 128)) < 64
            dkk = jnp.zeros((2 * BLK, 128), F32)
            dvv = jnp.zeros((2 * BLK, 128), F32)
            for g in range(2):
                qs = _stack_heads(q_ref[:, 256 * g:256 * g + 128], q_ref[:, 256 * g + 128:256 * g + 256])
                dos = _stack_heads(do_ref[:, 256 * g:256 * g + 128],
                                   do_ref[:, 256 * g + 128:256 * g + 256]).astype(BF16)
                kd = _dup_half(kk, g)
                pn, psn = _swa_probs(qs, kd, mask, _sink_col(sink_ref, g))
                dp = _dot_nt(dos, _dup_half(vv, g))
                dd = jnp.sum(pn * dp, axis=-1, keepdims=True)
                ds = (pn * (dp - dd) * (1.0 / math.sqrt(HD))).astype(BF16)
                sacc_ref[g] += jnp.broadcast_to(-psn * dd, (4 * BLK, 128))
                dqa, dqb = _unstack_heads(_dot(ds, kd))
                dq_ref[:, 256 * g:256 * g + 128] = dqa
                dq_ref[:, 256 * g + 128:256 * g + 256] = dqb
                dkg = _fold_half(_dot_tn(ds, qs))
                dvg = _fold_half(_dot_tn(pn.astype(BF16), dos))
                keep = lo256 if g == 0 else jnp.logical_not(lo256)
                dkk = jnp.where(keep, dkg, dkk)
                dvv = jnp.where(keep, dvg, dvv)
            dk_ref[...] = ck_ref[...] + dkk[0:BLK]
            dv_ref[...] = cv_ref[...] + dvv[0:BLK]
            ck_ref[...] = dkk[BLK:]
            cv_ref[...] = dvv[BLK:]

        @pl.when(i == nb)
        def _():
            dk_ref[...] = ck_ref[...]
            dv_ref[...] = cv_ref[...]
            lane = _lane((8, 128))
            acc = jnp.zeros((8, 128), F32)
            for g in range(2):
                for j in range(4):
                    val = jnp.sum(sacc_ref[g, j * BLK:(j + 1) * BLK, :], axis=0, keepdims=True)
                    acc = jnp.where(lane == 4 * g + j, jnp.broadcast_to(val, (8, 128)), acc)
            ds_ref[...] = acc

    cur = lambda i: (jnp.minimum(i, nb - 1), 0)
    prev = lambda i: (jnp.clip(i - 1, 0, nb - 1), 0)
    return pl.pallas_call(
        body, name="swa_bwd", grid=(nb + 1,),
        in_specs=[pl.BlockSpec(memory_space=pltpu.SMEM), pl.BlockSpec((BLK, AW), cur),
                  pl.BlockSpec((BLK, KW), cur), pl.BlockSpec((BLK, KW), prev),
                  pl.BlockSpec((BLK, KW), cur), pl.BlockSpec((BLK, KW), prev), pl.BlockSpec((BLK, AW), cur)],
        out_specs=[pl.BlockSpec((BLK, AW), cur), pl.BlockSpec((BLK, KW), prev), pl.BlockSpec((BLK, KW), prev),
                   _const((8, 128))],
        out_shape=[_sds((t, AW)), _sds((t, KW)), _sds((t, KW)), _sds((8, 128))],
        scratch_shapes=[pltpu.VMEM((BLK, KW), F32), pltpu.VMEM((BLK, KW), F32), pltpu.VMEM((2, 4 * BLK, 128), F32)],
        compiler_params=_params())(sinks, q, k, k, v, v, dattn)


def _mixer_in_bwd(dq, dk, dv, dgz, qk, cos, sin, x, dx1, w_in, mix_norm, qn, kn):
    t = x.shape[0]

    def body(dq_ref, dk_ref, dv_ref, dgz_ref, qk_ref, cos_ref, sin_ref, x_ref, dx1_ref, w_ref, g_ref, qn_ref, kn_ref,
             ones_ref, gx_ref, dproj_ref, dmn_ref, dqn_ref, dkn_ref, qacc_ref, kacc_ref):
        i = pl.program_id(0)

        @pl.when(i == 0)
        def _():
            dmn_ref[...] = jnp.zeros_like(dmn_ref)
            qacc_ref[...] = jnp.zeros_like(qacc_ref)
            kacc_ref[...] = jnp.zeros_like(kacc_ref)

        cos2, sin2 = cos_ref[...], sin_ref[...]
        qpre, kpre = qk_ref[:, :AW], qk_ref[:, AW:]
        dqh = _rope_bwd(dq_ref[...], jnp.tile(cos2, (1, 4)), jnp.tile(sin2, (1, 4)))
        dqpre, dgq = _rms64_bwd(dqh, qpre, _rs64(qpre, ones_ref), qn_ref[...], ones_ref)
        dkh = _rope_bwd(dk_ref[...], cos2, sin2)
        dkpre, dgk = _rms64_bwd(dkh, kpre, _rs64(kpre, ones_ref), kn_ref[...], ones_ref)
        qacc_ref[...] += jnp.sum(dgq, axis=0, keepdims=True)
        kacc_ref[...] += jnp.sum(dgk, axis=0, keepdims=True)
        dproj = jnp.concatenate([dqpre.astype(BF16), dkpre.astype(BF16), dv_ref[...].astype(BF16), dgz_ref[...]], axis=1)
        dproj_ref[...] = dproj
        dh = _dot_nt(dproj, w_ref[...])
        xv = x_ref[...]
        dx, dg = _rms_bwd(dh, xv, _rs(xv), g_ref[...])
        gx_ref[...] = dx1_ref[...] + dx
        dmn_ref[...] += _colsum8(dg)

        @pl.when(i == pl.num_programs(0) - 1)
        def _():
            qa = qacc_ref[...]
            q4 = qa[:, 0:128] + qa[:, 128:256] + qa[:, 256:384] + qa[:, 384:512]
            dqn_ref[...] = jnp.broadcast_to(_fold_half(q4), (8, 128))
            dkn_ref[...] = jnp.broadcast_to(_fold_half(kacc_ref[...]), (8, 128))

    return pl.pallas_call(
        body, name="mixer_in_bwd", grid=(t // TM,),
        in_specs=[_rows(TM, AW), _rows(TM, KW), _rows(TM, KW), _rows(TM, 2 * GW), _rows(TM, AW + KW), _rows(TM, 128),
                  _rows(TM, 128), _rows(TM, D), _rows(TM, D), _const((D, IN)), _const((1, D)), _const((1, AW)),
                  _const((1, KW)), _const((AW, AW))],
        out_specs=[_rows(TM, D), _rows(TM, IN), _const((8, D)), _const((8, 128)), _const((8, 128))],
        out_shape=[_sds((t, D)), _sds((t, IN), BF16), _sds((8, D)), _sds((8, 128)), _sds((8, 128))],
        scratch_shapes=[pltpu.VMEM((1, AW), F32), pltpu.VMEM((1, KW), F32)],
        compiler_params=_params())(dq, dk, dv, dgz, qk, cos, sin, x, dx1, w_in, mix_norm, qn, kn, _head_ones())


def _local_step(x, mem, pos, target, p, fetch, ship):
    t = x.shape[0]
    p = dict(p)
    p.update(fetch(0, None))
    inv_freq = 1.0 / (ROPE_THETA ** (jnp.arange(HD // 2, dtype=F32) * (2.0 / HD)))
    cos, sin = _rope_tables(pos, jnp.tile(inv_freq, 4).reshape(1, 128))
    qn = jnp.tile(p["q_norm"], (1, AW // HD))
    kn = jnp.tile(p["k_norm"], (1, KW // HD))
    qn4 = jnp.tile(p["xa_q_norm"], (1, XH))
    kn4 = jnp.tile(p["xa_k_norm"], (1, XH))
    ws = p["gmlp_ws"]
    wst = jnp.swapaxes(ws, 1, 2)
    bfull = jnp.repeat(p["gmlp_bs"].T, HD, axis=1)
    conv_b = p["ffn_conv_b"]

    h1, qk, gz, q, k, v, gu, gvn = _mixer_in_fwd(x, p["mix_norm"], p["w_in"], qn, kn, p["gmlp_v_norm"], cos, sin)
    attn = _swa_fwd(q, k, v, p["attn_sinks"])
    gm = _gmlp_fwd(gvn, gu, ws, bfull)
    ycat, x1, h2 = _mixer_out_fwd(attn, gm, x, p["w_out"], p["attn_out_norm"], p["gmlp_out_norm"], p["xa_norm"])
    p.update(fetch(1, h2))
    mh, kpre, k2, v2 = _mem_kv_fwd(mem, p["mem_norm"], p["xa_wkv"], kn4)
    qpre, o, x2, h3 = _xattn_fwd(h2, x1, p["xa_wq"], qn4, k2, v2, p["xa_wo"], p["ffn_norm"])
    p.update(fetch(2, h3))
    conv = p["ffn_conv"]
    a, u, dy, loss8 = _ffn_fwd(h3, x2, target, p["ffn_up"], conv, conv_b, p["ffn_down"])

    raw = {}
    d_down = _mm_tn(u, dy, "ffn_down_bwd_w")
    dx2, da, raw["conv_sums"], raw["ffn_norm"] = _ffn_bwd(dy, a, x2, p["ffn_up"], conv, conv_b, p["ffn_down"], p["ffn_norm"])
    d_up = _mm_tn(h3, da, "ffn_up_bwd_w")
    token = ship(0, {"ffn_down": d_down, "ffn_up": d_up, "ffn_conv": raw["conv_sums"][:, :, 0:3]})
    dx1, dqpre, dk2, dv2, raw["xa_q_norm"], raw["xa_norm"] = _xattn_bwd(
        dx2, x1, qpre, k2, v2, p["xa_wq"], p["xa_wo"], qn4 + jnp.tile(token[0:1], (1, D // 128)), p["xa_norm"])
    d_wo = _mm_tn(o, dx2, "xa_wo_bwd_w")
    d_wq = _mm_tn(h2, dqpre, "xa_wq_bwd_w")
    d_wkv, raw["xa_k_norm"], raw["mem_norm"] = _mem_kv_bwd(mem, mh, kpre, dk2, dv2, p["xa_wkv"], kn4, p["mem_norm"])
    token = ship(1, {"xa_wo": d_wo, "xa_wq": d_wq, "xa_wkv": d_wkv})
    dattn, dgm, raw["attn_out_norm"], raw["gmlp_out_norm"] = _mixer_out_bwd(
        dx1, attn, gm, p["w_out"], p["attn_out_norm"] + jnp.tile(token[0:1], (1, AW // 128)), p["gmlp_out_norm"])
    d_w_out = _mm_tn(ycat, dx1, "w_out_bwd_w")
    dgz, raw["gmlp_ws"], raw["gmlp_bs"], raw["gmlp_v_norm"] = _gmlp_bwd(dgm, gu, gvn, gz, ws, wst, bfull, p["gmlp_v_norm"])
    dq, dk, dv, raw["attn_sinks"] = _swa_bwd(q, k, v, dattn, p["attn_sinks"])
    grad_x, dproj, raw["mix_norm"], raw["q_norm"], raw["k_norm"] = _mixer_in_bwd(
        dq, dk, dv, dgz, qk, cos, sin, x, dx1, p["w_in"], p["mix_norm"], qn, kn)
    d_w_in = _mm_tn(h1, dproj, "w_in_bwd_w")
    return loss8[0, 0], grad_x, {"w_out": d_w_out, "w_in": d_w_in}, raw


def _cast_shards(shards):
    def body(*refs):
        n = len(refs) // 2
        for i_ref, o_ref in zip(refs[:n], refs[n:]):
            o_ref[...] = i_ref[0].astype(BF16)

    return pl.pallas_call(body, name="cast_shards", out_shape=[_sds(s.shape[1:], BF16) for s in shards],
                          compiler_params=pltpu.CompilerParams(vmem_limit_bytes=VMEM_LIMIT))(*shards)


HBM_SPEC = pl.BlockSpec(memory_space=pltpu.HBM)
SEM_SPEC = pl.BlockSpec(memory_space=pltpu.SEMAPHORE)


def _remote_copies(src_refs, land_refs, send_refs, recv_refs, nd):
    x, y, cc = lax.axis_index("x"), lax.axis_index("y"), lax.axis_index("c")
    me = 4 * x + 2 * y + cc
    copies = []
    for a, (src_ref, land_ref) in enumerate(zip(src_refs, land_refs)):
        for k in range(1, NDEV):
            px = 1 - x if k & 4 else x
            py = 1 - y if k & 2 else y
            pc = 1 - cc if k & 1 else cc
            copies.append(pltpu.make_async_remote_copy(
                src_ref=src_ref.at[4 * px + 2 * py + pc] if a < nd else src_ref, dst_ref=land_ref.at[me],
                send_sem=send_refs[a].at[k - 1], recv_sem=recv_refs[a].at[k - 1],
                device_id=(px, py, pc), device_id_type=pl.DeviceIdType.MESH))
    return copies


def _own_slot(src, by_dest, me):
    block = lax.dynamic_index_in_dim(src, me, 0, keepdims=True) if by_dest else src[None]
    return lax.dynamic_update_index_in_dim(lax.empty((NDEV,) + block.shape[1:], src.dtype), block, me, 0)


def _exchange_start(by_dest, for_all, me, name):
    srcs = list(by_dest) + list(for_all)
    n, nd = len(srcs), len(by_dest)
    lands = [_own_slot(s, a < nd, me) for a, s in enumerate(srcs)]

    def body(*refs):
        for cp in _remote_copies(refs[:n], refs[n:2 * n], refs[2 * n:3 * n], refs[3 * n:4 * n], nd):
            cp.start()
        refs[-1][...] = jnp.zeros((8, 128), F32)

    sems = [pltpu.SemaphoreType.DMA((NDEV - 1,))] * (2 * n)
    thru = [pltpu.HBM(v.shape, v.dtype) for v in srcs + lands]
    res = pl.pallas_call(
        body, name=name, out_shape=sems + thru + [_sds((8, 128))],
        in_specs=[HBM_SPEC] * (2 * n), out_specs=[SEM_SPEC] * (2 * n) + [HBM_SPEC] * (2 * n) + [pl.BlockSpec(memory_space=pltpu.VMEM)],
        input_output_aliases={i: 2 * n + i for i in range(2 * n)},
        compiler_params=pltpu.CompilerParams(has_side_effects=pltpu.SideEffectType.DATAFLOW_SIDE_EFFECTING))(
            *[pltpu.with_memory_space_constraint(v, pltpu.HBM) for v in srcs + lands])
    return (res[:2 * n], res[2 * n:4 * n], nd), res[-1]


def _exchange_wait(state, after, name):
    sems, thru, nd = state
    n = len(thru) // 2

    def body(*refs):
        for cp in _remote_copies(refs[:n], refs[n:2 * n], refs[2 * n:3 * n], refs[3 * n:4 * n], nd):
            cp.wait_send()
            cp.wait_recv()

    res = pl.pallas_call(
        body, name=name, out_shape=[pltpu.HBM(v.shape, v.dtype) for v in thru],
        in_specs=[HBM_SPEC] * (2 * n) + [SEM_SPEC] * (2 * n) + [pl.BlockSpec(memory_space=pl.ANY)],
        out_specs=[HBM_SPEC] * (2 * n), input_output_aliases={i: i for i in range(2 * n)},
        compiler_params=pltpu.CompilerParams(has_side_effects=pltpu.SideEffectType.DATAFLOW_SIDE_EFFECTING))(
            *thru, *sems, after)
    return res[n:]


def _adam(parts, w, m, v, name):
    def body(p_ref, w_ref, m_ref, v_ref, g_ref, d_ref, nm_ref, nv_ref):
        g = _sum_parts(p_ref)
        g_ref[0] = g
        d_ref[0], nm_ref[0], nv_ref[0] = _adam_math(g, w_ref[0], m_ref[0], v_ref[0])

    return pl.pallas_call(
        body, name=name, out_shape=[_sds(w.shape)] * 4,
        compiler_params=pltpu.CompilerParams(vmem_limit_bytes=VMEM_LIMIT))(parts, w, m, v)


GATHER_GROUPS = (("w_in", "w_out"), ("xa_wkv", "xa_wq", "xa_wo"), ("ffn_up", "ffn_conv", "ffn_down"))
SCATTER_GROUPS = (("ffn_down", "ffn_up", "ffn_conv"), ("xa_wo", "xa_wq", "xa_wkv"), ("w_out", "w_in"))
BIG = tuple(n for grp in GATHER_GROUPS for n in grp)
VECS = (("mix_norm", D), ("q_norm", HD), ("k_norm", HD), ("attn_sinks", 8), ("gmlp_v_norm", GW), ("attn_out_norm", AW),
        ("gmlp_out_norm", GW), ("xa_norm", D), ("mem_norm", D), ("xa_q_norm", XD), ("xa_k_norm", XD), ("ffn_norm", D))
BS_ROW = 16
VEC_ROWS = 24
SMALL = tuple(n for n, _ in VECS) + ("gmlp_bs", "gmlp_ws", "ffn_conv_b")


def _pack_small(raw):
    names = [n for n, _ in VECS] + ["gmlp_bs", "conv_sums"]

    def body(*refs):
        ins = dict(zip(names, refs))
        vec_ref, cb_ref = refs[len(names):]
        vec_ref[...] = jnp.zeros_like(vec_ref)
        for r, (n, w) in enumerate(VECS):
            vec_ref[r:r + 1, 0:w] = ins[n][0:1, 0:w]
        vec_ref[BS_ROW:BS_ROW + 8, 0:BLK] = ins["gmlp_bs"][...]
        for s in range(2):
            for d in range(NG):
                cb_ref[s, d] = ins["conv_sums"][s, d, 3:4, :]

    return pl.pallas_call(body, name="pack_small", out_shape=[_sds((VEC_ROWS, D)), _sds((2, NG, 1, SW))])(
        *[raw[n] for n in names])


def _adam_math(g, w, m, v):
    nm = B1 * m + (1.0 - B1) * g
    nv = B2 * v + (1.0 - B2) * (g * g)
    m_hat = nm / (1.0 - B1 ** STEP)
    v_hat = nv / (1.0 - B2 ** STEP)
    return -LR * (m_hat / (jnp.sqrt(v_hat) + AEPS) + WD * w), nm, nv


def _sum_parts(p_ref):
    g = p_ref[0].astype(F32)
    for j in range(1, NDEV):
        g = g + p_ref[j].astype(F32)
    return g


def _adam_small(parts_vec, parts_ws, parts_cb, w, m, v):
    def body(*refs):
        pv_ref, pws_ref, pcb_ref = refs[:3]
        ins = refs[3:3 + 3 * len(SMALL)]
        outs = refs[3 + 3 * len(SMALL):]
        gv = _sum_parts(pv_ref)
        for j, n in enumerate(SMALL):
            w_ref, m_ref, v_ref = ins[3 * j:3 * j + 3]
            o = outs[4 * j:4 * j + 4]
            if n == "gmlp_ws":
                g = _sum_parts(pws_ref)
            elif n == "ffn_conv_b":
                g = _sum_parts(pcb_ref)
            elif n == "gmlp_bs":
                g = gv[BS_ROW:BS_ROW + 8, 0:BLK]
            else:
                g = gv[j:j + 1, 0:VECS[j][1]]
            lead = n in ("gmlp_ws", "gmlp_bs")
            res = (g,) + _adam_math(g, w_ref[0] if lead else w_ref[...], m_ref[0] if lead else m_ref[...],
                                    v_ref[0] if lead else v_ref[...])
            for o_ref, val in zip(o, res):
                if lead:
                    o_ref[0] = val
                else:
                    o_ref[...] = val

    args = [parts_vec, parts_ws, parts_cb] + [d[n] for n in SMALL for d in (w, m, v)]
    res = pl.pallas_call(body, name="adam_small", out_shape=[_sds(w[n].shape) for n in SMALL for _ in range(4)],
                         compiler_params=pltpu.CompilerParams(vmem_limit_bytes=VMEM_LIMIT))(*args)
    return {n: tuple(res[4 * j:4 * j + 4]) for j, n in enumerate(SMALL)}


def kernel(x, mem, positions, mix_norm, w_in, q_norm, k_norm, attn_sinks, gmlp_v_norm, gmlp_ws, gmlp_bs, attn_out_norm, gmlp_out_norm, w_out, xa_norm, mem_norm, xa_wq, xa_wkv, xa_q_norm, xa_k_norm, xa_wo, ffn_norm, ffn_up, ffn_conv, ffn_conv_b, ffn_down, loss_target, m_mix_norm, m_w_in, m_q_norm, m_k_norm, m_attn_sinks, m_gmlp_v_norm, m_gmlp_ws, m_gmlp_bs, m_attn_out_norm, m_gmlp_out_norm, m_w_out, m_xa_norm, m_mem_norm, m_xa_wq, m_xa_wkv, m_xa_q_norm, m_xa_k_norm, m_xa_wo, m_ffn_norm, m_ffn_up, m_ffn_conv, m_ffn_conv_b, m_ffn_down, v_mix_norm, v_w_in, v_q_norm, v_k_norm, v_attn_sinks, v_gmlp_v_norm, v_gmlp_ws, v_gmlp_bs, v_attn_out_norm, v_gmlp_out_norm, v_w_out, v_xa_norm, v_mem_norm, v_xa_wq, v_xa_wkv, v_xa_q_norm, v_xa_k_norm, v_xa_wo, v_ffn_norm, v_ffn_up, v_ffn_conv, v_ffn_conv_b, v_ffn_down):
    names = ("mix_norm", "w_in", "q_norm", "k_norm", "attn_sinks", "gmlp_v_norm", "gmlp_ws", "gmlp_bs", "attn_out_norm",
             "gmlp_out_norm", "w_out", "xa_norm", "mem_norm", "xa_wq", "xa_wkv", "xa_q_norm", "xa_k_norm", "xa_wo",
             "ffn_norm", "ffn_up", "ffn_conv", "ffn_conv_b", "ffn_down")
    w = dict(zip(names, (mix_norm, w_in, q_norm, k_norm, attn_sinks, gmlp_v_norm, gmlp_ws, gmlp_bs, attn_out_norm,
                         gmlp_out_norm, w_out, xa_norm, mem_norm, xa_wq, xa_wkv, xa_q_norm, xa_k_norm, xa_wo, ffn_norm,
                         ffn_up, ffn_conv, ffn_conv_b, ffn_down)))
    m = dict(zip(names, (m_mix_norm, m_w_in, m_q_norm, m_k_norm, m_attn_sinks, m_gmlp_v_norm, m_gmlp_ws, m_gmlp_bs,
                         m_attn_out_norm, m_gmlp_out_norm, m_w_out, m_xa_norm, m_mem_norm, m_xa_wq, m_xa_wkv,
                         m_xa_q_norm, m_xa_k_norm, m_xa_wo, m_ffn_norm, m_ffn_up, m_ffn_conv, m_ffn_conv_b, m_ffn_down)))
    v = dict(zip(names, (v_mix_norm, v_w_in, v_q_norm, v_k_norm, v_attn_sinks, v_gmlp_v_norm, v_gmlp_ws, v_gmlp_bs,
                         v_attn_out_norm, v_gmlp_out_norm, v_w_out, v_xa_norm, v_mem_norm, v_xa_wq, v_xa_wkv,
                         v_xa_q_norm, v_xa_k_norm, v_xa_wo, v_ffn_norm, v_ffn_up, v_ffn_conv, v_ffn_conv_b, v_ffn_down)))
    t = x.shape[1]

    me = 4 * lax.axis_index("x") + 2 * lax.axis_index("y") + lax.axis_index("c")

    mats = [n for n in BIG if n != "ffn_conv"]
    shard = dict(zip(mats, _cast_shards([w[n] for n in mats])), ffn_conv=w["ffn_conv"][0])
    gathers, tokens = zip(*[_exchange_start([], [shard[n] for n in grp], me, "gather_start_%d" % i)
                            for i, grp in enumerate(GATHER_GROUPS)])

    def fetch(i, after):
        after = tokens[0] + tokens[1] + tokens[2] if after is None else after
        got = dict(zip(GATHER_GROUPS[i], _exchange_wait(gathers[i], after, "gather_wait_%d" % i)))
        if "w_in" in got:
            got["w_in"] = got["w_in"].transpose(1, 0, 2).reshape(D, IN)
        for n in ("w_out", "xa_wq", "xa_wo"):
            if n in got:
                got[n] = got[n].reshape(D, D)
        if "ffn_down" in got:
            got["ffn_down"] = got["ffn_down"].reshape(NG, SW, D)
            got["ffn_conv"] = got["ffn_conv"].reshape(2, NG, 3, SW)
        return got

    scatters = []

    def ship(i, grads, for_all=()):
        by_dest = [grads[n].reshape((NDEV,) + w[n].shape[1:]) for n in SCATTER_GROUPS[i]]
        state, token = _exchange_start(by_dest, for_all, me, "scatter_start_%d" % i)
        scatters.append(state)
        return token

    conv_b = {k: d["ffn_conv_b"].reshape(NDEV, 1, SW) for k, d in (("w", w), ("m", m), ("v", v))}
    p = {n: w[n] for n in SMALL[:-1]}
    p["gmlp_ws"], p["gmlp_bs"] = w["gmlp_ws"][0], w["gmlp_bs"][0]
    p["ffn_conv_b"] = conv_b["w"].reshape(2, NG, 1, SW)
    loss, grad_x, g, raw = _local_step(x[0], mem[0], positions.reshape(t, 1), loss_target[0], p, fetch, ship)
    loss = lax.psum(loss, AXES)

    vec, cb = _pack_small(raw)
    g["w_in"] = g["w_in"].reshape(D, NDEV, IN // NDEV).transpose(1, 0, 2)
    token = ship(2, g, [vec, raw["gmlp_ws"], cb.reshape(NDEV, 1, SW)])
    parts = {}
    for i, grp in enumerate(SCATTER_GROUPS):
        got = _exchange_wait(scatters[i], token, "scatter_wait_%d" % i)
        parts.update(zip(grp, got))
        rest = got[len(grp):]
    res = {n: _adam(parts[n], w[n], m[n], v[n], "adam_" + n) for n in BIG}
    small = lambda d, k: {**{n: d[n] for n in SMALL[:-1]}, "ffn_conv_b": conv_b[k]}
    res.update(_adam_small(*rest, small(w, "w"), small(m, "m"), small(v, "v")))

    outs = [loss, grad_x[None]]
    for j in range(4):
        outs += [res[n][j].reshape(w[n].shape) for n in names]
    return tuple(outs)
```

```python
import functools
import math

import jax
import jax.numpy as jnp
from jax import lax
from jax.experimental import pallas as pl
from jax.experimental.pallas import tpu as pltpu

F32 = jnp.float32
BF16 = jnp.bfloat16

D = 1024
HD = 64
AW = 512
KW = 128
GW = 512
IN = AW + 2 * KW + 2 * GW
BLK = 128
MEM = 256
XH = 4
XD = 256
FF = 2816
EPS = 1e-6
ROPE_THETA = 10000.0
NDEV = 8
LR, B1, B2, AEPS, WD, STEP = 0.001, 0.9, 0.999, 1e-08, 0.01, 10

TM = 512
WK = 2048
VMEM_LIMIT = 56 * 1024 * 1024
NEG = float(jnp.finfo(jnp.float32).min)
GELU_C0 = math.sqrt(2.0 / math.pi)
GELU_C1 = 0.044715
AXES = ("x", "y", "c")


def _dot(a, b):
    return jnp.dot(a, b, preferred_element_type=F32)


def _dot_nt(a, b):
    return lax.dot_general(a, b, (((1,), (1,)), ((), ())), preferred_element_type=F32)


def _dot_tn(a, b):
    return lax.dot_general(a, b, (((0,), (0,)), ((), ())), preferred_element_type=F32)


def _rs(x):
    return lax.rsqrt(jnp.mean(x * x, axis=-1, keepdims=True) + EPS)


def _rms_bwd(dy, x, r, g):
    xh = x * r
    dxh = dy * g
    dx = r * (dxh - xh * jnp.mean(dxh * xh, axis=-1, keepdims=True))
    return dx, dy * xh


def _lane(shape):
    return lax.broadcasted_iota(jnp.int32, shape, len(shape) - 1)


def _gsum64(v, ones_ref):
    w = v.shape[-1]
    ones = ones_ref[0:w, 0:w]
    hi = v.astype(BF16)
    lo = (v - hi.astype(F32)).astype(BF16)
    return _dot(hi, ones) + _dot(lo, ones)


def _head_ones():
    i = jnp.arange(AW) // HD
    return (i[:, None] == i[None, :]).astype(BF16)


def _rs64(x, ones_ref):
    return lax.rsqrt(_gsum64(x * x, ones_ref) * (1.0 / HD) + EPS)


def _rms64_bwd(dy, x, r, g, ones_ref):
    xh = x * r
    dxh = dy * g
    dx = r * (dxh - xh * (_gsum64(dxh * xh, ones_ref) * (1.0 / HD)))
    return dx, dy * xh


def _rot_half(v):
    w = v.shape[-1]
    return jnp.where((_lane(v.shape) & 32) == 0, pltpu.roll(v, w - 32, 1), pltpu.roll(v, 32, 1))


def _rope(v, cos, sin_signed):
    return v * cos + _rot_half(v) * sin_signed


def _rope_bwd(dv, cos, sin_signed):
    return dv * cos + _rot_half(dv * sin_signed)


def _gelu(z):
    return 0.5 * z * (1.0 + jnp.tanh(GELU_C0 * (z + GELU_C1 * z * z * z)))


def _gelu_grad(z):
    t = jnp.tanh(GELU_C0 * (z + GELU_C1 * z * z * z))
    return 0.5 * (1.0 + t) + 0.5 * z * (1.0 - t * t) * (GELU_C0 * (1.0 + 3.0 * GELU_C1 * z * z))


def _colsum8(v):
    s = jnp.sum(v, axis=0, keepdims=True)
    row = lax.broadcasted_iota(jnp.int32, (8, v.shape[1]), 0)
    return jnp.where(row == 0, jnp.broadcast_to(s, (8, v.shape[1])), 0.0)


def _params(n_axes=1):
    return pltpu.CompilerParams(dimension_semantics=("arbitrary",) * n_axes, vmem_limit_bytes=VMEM_LIMIT)


def _rows(tm, w):
    return pl.BlockSpec((tm, w), lambda i: (i, 0))


def _const(shape):
    nd = len(shape)
    return pl.BlockSpec(shape, lambda *_: (0,) * nd)


def _sds(shape, dtype=F32):
    return jax.ShapeDtypeStruct(shape, dtype)


def _mm_tn(a, b, name):
    g = max(a.shape[0] if a.ndim == 3 else 1, b.shape[0] if b.ndim == 3 else 1)
    t, m = a.shape[-2:]
    n = b.shape[-1]

    def body(a_ref, b_ref, o_ref, acc_ref):
        i = pl.program_id(1)

        @pl.when(i == 0)
        def _():
            acc_ref[...] = jnp.zeros_like(acc_ref)

        acc_ref[...] += _dot_tn(a_ref[...].astype(BF16), b_ref[...].astype(BF16))

        @pl.when(i == pl.num_programs(1) - 1)
        def _():
            o_ref[...] = acc_ref[...].astype(BF16)

    tk = min(t, WK)

    def spec(v):
        w = v.shape[-1]
        if v.ndim == 3:
            return pl.BlockSpec((None, tk, w), lambda j, i: (j, i, 0))
        return pl.BlockSpec((tk, w), lambda j, i: (i, 0))

    return pl.pallas_call(
        body, name=name, grid=(g, t // tk), in_specs=[spec(a), spec(b)],
        out_specs=pl.BlockSpec((None, m, n), lambda j, i: (j, 0, 0)), out_shape=_sds((g, m, n), BF16),
        scratch_shapes=[pltpu.VMEM((m, n), F32)], compiler_params=_params(2))(a, b)


def _rope_tables(pos, inv_freq):
    t = pos.shape[0]

    def body(pos_ref, f_ref, cos_ref, sin_ref):
        ang = pos_ref[...].astype(F32) * f_ref[...]
        sign = jnp.where((_lane(ang.shape) & 32) == 0, -1.0, 1.0)
        cos_ref[...] = jnp.cos(ang)
        sin_ref[...] = jnp.sin(ang) * sign

    return pl.pallas_call(
        body, name="rope_tables", grid=(t // TM,),
        in_specs=[_rows(TM, 1), _const((1, 128))], out_specs=[_rows(TM, 128), _rows(TM, 128)],
        out_shape=[_sds((t, 128)), _sds((t, 128))], compiler_params=_params())(pos, inv_freq)


def _mixer_in_fwd(x, mix_norm, w_in, qn, kn, gvw, cos, sin):
    t = x.shape[0]

    def body(x_ref, g_ref, w_ref, qn_ref, kn_ref, gvw_ref, cos_ref, sin_ref, ones_ref,
             h_ref, qk_ref, gz_ref, q_ref, k_ref, v_ref, gu_ref, gvn_ref):
        x = x_ref[...]
        h = (x * _rs(x) * g_ref[...]).astype(BF16)
        h_ref[...] = h
        proj = _dot(h, w_ref[...])
        qk = proj[:, :AW + KW]
        qk_ref[...] = qk
        gz = proj[:, AW + 2 * KW:]
        gz_ref[...] = gz
        cos2, sin2 = cos_ref[...], sin_ref[...]
        q = qk[:, :AW]
        q = q * _rs64(q, ones_ref) * qn_ref[...]
        q_ref[...] = _rope(q, jnp.tile(cos2, (1, 4)), jnp.tile(sin2, (1, 4))).astype(BF16)
        k = qk[:, AW:]
        k = k * _rs64(k, ones_ref) * kn_ref[...]
        k_ref[...] = _rope(k, cos2, sin2).astype(BF16)
        v_ref[...] = proj[:, AW + KW:AW + 2 * KW].astype(BF16)
        gu_ref[...] = _gelu(gz[:, :GW])
        gv = _gelu(gz[:, GW:])
        gvn_ref[...] = (gv * _rs(gv) * gvw_ref[...]).astype(BF16)

    return pl.pallas_call(
        body, name="mixer_in_fwd", grid=(t // TM,),
        in_specs=[_rows(TM, D), _const((1, D)), _const((D, IN)), _const((1, AW)), _const((1, KW)),
                  _const((1, GW)), _rows(TM, 128), _rows(TM, 128), _const((AW, AW))],
        out_specs=[_rows(TM, D), _rows(TM, AW + KW), _rows(TM, 2 * GW), _rows(TM, AW), _rows(TM, KW),
                   _rows(TM, KW), _rows(TM, GW), _rows(TM, GW)],
        out_shape=[_sds((t, D), BF16), _sds((t, AW + KW)), _sds((t, 2 * GW)), _sds((t, AW), BF16),
                   _sds((t, KW), BF16), _sds((t, KW), BF16), _sds((t, GW)), _sds((t, GW), BF16)],
        compiler_params=_params())(x, mix_norm, w_in, qn, kn, gvw, cos, sin, _head_ones())


def _dup_half(kk, g):
    lane = _lane(kk.shape)
    other = pltpu.roll(kk, 64, 1)
    keep = (lane < 64) if g == 0 else (lane >= 64)
    return jnp.where(keep, kk, other).astype(BF16)


def _swa_mask(first_block):
    qi = lax.broadcasted_iota(jnp.int32, (4 * BLK, 2 * BLK), 0) & (BLK - 1)
    kj = lax.broadcasted_iota(jnp.int32, (4 * BLK, 2 * BLK), 1)
    diff = qi + BLK - kj
    band = (diff >= 0) & (diff < BLK)
    return band & (jnp.logical_not(first_block) | (kj >= BLK))


def _stack_heads(a2, b2):
    lo = _lane(a2.shape) < 64
    z = jnp.zeros_like(a2)
    return jnp.concatenate([jnp.where(lo, a2, z), jnp.where(lo, z, a2), jnp.where(lo, b2, z), jnp.where(lo, z, b2)], axis=0)


def _unstack_heads(o):
    lo = _lane((BLK, 128)) < 64
    return jnp.where(lo, o[0:BLK], o[BLK:2 * BLK]), jnp.where(lo, o[2 * BLK:3 * BLK], o[3 * BLK:4 * BLK])


def _sink_col(sink_ref, g):
    row = lax.broadcasted_iota(jnp.int32, (4 * BLK, 1), 0)
    s = [sink_ref[0, 4 * g + j] for j in range(4)]
    return jnp.where(row < BLK, s[0], jnp.where(row < 2 * BLK, s[1], jnp.where(row < 3 * BLK, s[2], s[3])))


def _swa_probs(qs, kd, mask, sink):
    s = _dot_nt(qs, kd) * (1.0 / math.sqrt(HD))
    s = jnp.where(mask, s, NEG)
    m = jnp.maximum(jnp.max(s, axis=-1, keepdims=True), sink)
    p = jnp.exp(s - m)
    ps = jnp.exp(sink - m)
    inv = 1.0 / (jnp.sum(p, axis=-1, keepdims=True) + ps)
    return p * inv, ps * inv


def _swa_fwd(q, k, v, sinks):
    t = q.shape[0]
    nb = t // BLK

    def body(sink_ref, q_ref, kc_ref, kp_ref, vc_ref, vp_ref, o_ref):
        i = pl.program_id(0)
        mask = _swa_mask(i == 0)
        kk = jnp.concatenate([kp_ref[...], kc_ref[...]], axis=0).astype(F32)
        vv = jnp.concatenate([vp_ref[...], vc_ref[...]], axis=0).astype(F32)
        for g in range(2):
            qs = _stack_heads(q_ref[:, 256 * g:256 * g + 128], q_ref[:, 256 * g + 128:256 * g + 256])
            pn, _ = _swa_probs(qs, _dup_half(kk, g), mask, _sink_col(sink_ref, g))
            oa, ob = _unstack_heads(_dot(pn.astype(BF16), _dup_half(vv, g)))
            o_ref[:, 256 * g:256 * g + 128] = oa
            o_ref[:, 256 * g + 128:256 * g + 256] = ob

    cur = lambda i: (i, 0)
    prev = lambda i: (jnp.maximum(i - 1, 0), 0)
    return pl.pallas_call(
        body, name="swa_fwd", grid=(nb,),
        in_specs=[pl.BlockSpec(memory_space=pltpu.SMEM), pl.BlockSpec((BLK, AW), cur),
                  pl.BlockSpec((BLK, KW), cur), pl.BlockSpec((BLK, KW), prev),
                  pl.BlockSpec((BLK, KW), cur), pl.BlockSpec((BLK, KW), prev)],
        out_specs=pl.BlockSpec((BLK, AW), cur), out_shape=_sds((t, AW)),
        compiler_params=_params())(sinks, q, k, k, v, v)


def _causal_bf16(w_ref, h, transposed):
    r = lax.broadcasted_iota(jnp.int32, (BLK, BLK), 0)
    c = lax.broadcasted_iota(jnp.int32, (BLK, BLK), 1)
    keep = (r <= c) if transposed else (c <= r)
    return jnp.where(keep, w_ref[h], 0.0).astype(BF16)


def _gmlp_mix(w_ref, xin, transposed):
    lo = _lane((BLK, 128)) < 64
    wm = [_causal_bf16(w_ref, h, transposed) for h in range(8)]
    rows = []
    for c in range(xin.shape[0] // BLK):
        cols = []
        for j in range(4):
            xs = xin[c * BLK:(c + 1) * BLK, 128 * j:128 * (j + 1)]
            cols.append(jnp.where(lo, _dot(wm[2 * j], xs), _dot(wm[2 * j + 1], xs)))
        rows.append(jnp.concatenate(cols, axis=1))
    return jnp.concatenate(rows, axis=0)


def _gmlp_fwd(gvn, gu, ws, bfull):
    t = gvn.shape[0]

    def body(x_ref, gu_ref, w_ref, b_ref, o_ref):
        mixed = _gmlp_mix(w_ref, x_ref[...], False) + jnp.tile(b_ref[...], (TM // BLK, 1))
        o_ref[...] = gu_ref[...] * mixed

    return pl.pallas_call(
        body, name="gmlp_fwd", grid=(t // TM,),
        in_specs=[_rows(TM, GW), _rows(TM, GW), _const((8, BLK, BLK)), _const((BLK, GW))],
        out_specs=_rows(TM, GW), out_shape=_sds((t, GW)), compiler_params=_params())(gvn, gu, ws, bfull)


def _mixer_out_fwd(attn, gm, x, w_out, aon, gon, xan):
    t = x.shape[0]

    def body(a_ref, g_ref, x_ref, w_ref, aon_ref, gon_ref, xan_ref, y_ref, x1_ref, h2_ref):
        a, g = a_ref[...], g_ref[...]
        y = jnp.concatenate([a * _rs(a) * aon_ref[...], g * _rs(g) * gon_ref[...]], axis=1).astype(BF16)
        y_ref[...] = y
        x1 = x_ref[...] + _dot(y, w_ref[...])
        x1_ref[...] = x1
        h2_ref[...] = (x1 * _rs(x1) * xan_ref[...]).astype(BF16)

    return pl.pallas_call(
        body, name="mixer_out_fwd", grid=(t // TM,),
        in_specs=[_rows(TM, AW), _rows(TM, GW), _rows(TM, D), _const((D, D)), _const((1, AW)), _const((1, GW)),
                  _const((1, D))],
        out_specs=[_rows(TM, D), _rows(TM, D), _rows(TM, D)],
        out_shape=[_sds((t, D), BF16), _sds((t, D)), _sds((t, D), BF16)],
        compiler_params=_params())(attn, gm, x, w_out, aon, gon, xan)


def _mem_kv_fwd(mem, mem_norm, wkv, kn4):
    def body(m_ref, g_ref, w_ref, kn_ref, mh_ref, kpre_ref, k_ref, v_ref):
        m = m_ref[...]
        mh = (m * _rs(m) * g_ref[...]).astype(BF16)
        mh_ref[...] = mh
        for h in range(XH):
            sl = slice(XD * h, XD * (h + 1))
            kh = _dot(mh, w_ref[h])
            kpre_ref[:, sl] = kh
            k_ref[:, sl] = (kh * _rs(kh) * kn_ref[:, sl]).astype(BF16)
            v_ref[:, sl] = _dot(mh, w_ref[XH + h]).astype(BF16)

    return pl.pallas_call(
        body, name="mem_kv_fwd",
        out_shape=[_sds((MEM, D), BF16), _sds((MEM, D)), _sds((MEM, D), BF16), _sds((MEM, D), BF16)],
        compiler_params=pltpu.CompilerParams(vmem_limit_bytes=VMEM_LIMIT))(mem, mem_norm, wkv, kn4)


def _xattn_probs(qpre_h, qn_h, k_h):
    rq = _rs(qpre_h)
    q2 = (qpre_h * rq * qn_h).astype(BF16)
    s = _dot_nt(q2, k_h) * (1.0 / math.sqrt(XD))
    p = jnp.exp(s - jnp.max(s, axis=-1, keepdims=True))
    return p * (1.0 / jnp.sum(p, axis=-1, keepdims=True)), q2, rq


def _xattn_fwd(h2, x1, wq, qn4, k2, v2, wo, ffn_norm):
    t = x1.shape[0]

    def body(h_ref, x_ref, wq_ref, qn_ref, k_ref, v_ref, wo_ref, fn_ref, qpre_ref, o_ref, x2_ref, h3_ref):
        qpre = _dot(h_ref[...], wq_ref[...])
        qpre_ref[...] = qpre
        outs = []
        for h in range(XH):
            sl = slice(XD * h, XD * (h + 1))
            pn, _, _ = _xattn_probs(qpre[:, sl], qn_ref[:, sl], k_ref[:, sl])
            outs.append(_dot(pn.astype(BF16), v_ref[:, sl]))
        o = jnp.concatenate(outs, axis=1).astype(BF16)
        o_ref[...] = o
        x2 = x_ref[...] + _dot(o, wo_ref[...])
        x2_ref[...] = x2
        h3_ref[...] = (x2 * _rs(x2) * fn_ref[...]).astype(BF16)

    return pl.pallas_call(
        body, name="xattn_fwd", grid=(t // TM,),
        in_specs=[_rows(TM, D), _rows(TM, D), _const((D, D)), _const((1, D)), _const((MEM, D)), _const((MEM, D)),
                  _const((D, D)), _const((1, D))],
        out_specs=[_rows(TM, D)] * 4,
        out_shape=[_sds((t, D)), _sds((t, D), BF16), _sds((t, D)), _sds((t, D), BF16)],
        compiler_params=_params())(h2, x1, wq, qn4, k2, v2, wo, ffn_norm)


SW = 704
NG = FF // SW
FM = 256
HALO = 16


def _resident(shape):
    nd = len(shape)
    return pl.BlockSpec(shape, lambda *_: (0,) * nd, pipeline_mode=pl.Buffered(1))


def _halo_before(i):
    return jnp.maximum(i * (FM // HALO) - 1, 0)


def _conv(e, w):
    return w[2:3, :] * e + pltpu.roll(w[1:2, :] * e + pltpu.roll(w[0:1, :] * e, 1, 0), 1, 0)


def _conv_t(dc, w):
    n = dc.shape[0]
    return w[2:3, :] * dc + pltpu.roll(w[1:2, :] * dc + pltpu.roll(w[0:1, :] * dc, n - 1, 0), n - 1, 0)


def _ffn_fwd(h3, x2, target, up, conv, conv_b, down):
    t = x2.shape[0]

    def body(h_ref, hp_ref, x_ref, t_ref, up_ref, w_ref, b_ref, dn_ref, a_ref, u_ref, dy_ref, loss_ref, acc_ref):
        i = pl.program_id(0)

        @pl.when(i == 0)
        def _():
            acc_ref[...] = jnp.zeros_like(acc_ref)

        before = jnp.where(i > 0, hp_ref[...], jnp.zeros_like(hp_ref))
        he = jnp.concatenate([before, h_ref[...]], axis=0)
        err = x_ref[...] - t_ref[...]
        for d in range(NG):
            c = []
            for s in range(2):
                a = _dot(he, up_ref[s * NG + d]).astype(BF16)
                a_ref[s * NG + d] = a[HALO:]
                c.append(_conv(a.astype(F32), w_ref[s, d])[HALO:] + b_ref[s, d])
            u = (_gelu(c[0]) * c[1]).astype(BF16)
            u_ref[d] = u
            err = err + _dot(u, dn_ref[d])
        dy_ref[...] = err * (1.0 / D)
        acc_ref[...] += jnp.sum(err * err, axis=0, keepdims=True)

        @pl.when(i == pl.num_programs(0) - 1)
        def _():
            loss_ref[...] = jnp.full((8, 128), 0.5 / D, F32) * jnp.sum(acc_ref[...])

    return pl.pallas_call(
        body, name="ffn_fwd", grid=(t // FM,),
        in_specs=[_rows(FM, D), pl.BlockSpec((HALO, D), lambda i: (_halo_before(i), 0)), _rows(FM, D), _rows(FM, D),
                  _resident((NDEV, D, SW)), _resident((2, NG, 3, SW)), _resident((2, NG, 1, SW)), _resident((NG, SW, D))],
        out_specs=[pl.BlockSpec((NDEV, FM, SW), lambda i: (0, i, 0)), pl.BlockSpec((NG, FM, SW), lambda i: (0, i, 0)),
                   _rows(FM, D), _const((8, 128))],
        out_shape=[_sds((NDEV, t, SW), BF16), _sds((NG, t, SW), BF16), _sds((t, D)), _sds((8, 128))],
        scratch_shapes=[pltpu.VMEM((1, D), F32)], compiler_params=_params())(h3, h3, x2, target, up, conv, conv_b, down)


def _gelu_and_grad(z):
    z2 = z * z
    t = jnp.tanh(GELU_C0 * (z + GELU_C1 * z * z2))
    phi = 0.5 * (1.0 + t)
    return z * phi, phi + 0.5 * z * (1.0 - t * t) * (GELU_C0 * (1.0 + 3.0 * GELU_C1 * z2))


def _ffn_bwd(dy, a, x2, up, conv, conv_b, down, ffn_norm):
    t = x2.shape[0]
    nt = t // FM
    n = FM + HALO

    def body(dy_ref, dyn_ref, a_ref, ap_ref, an_ref, x_ref, up_ref, w_ref, b_ref, dn_ref, g_ref,
             dx_ref, da_ref, s_ref, dfn_ref):
        i = pl.program_id(0)

        @pl.when(i == 0)
        def _():
            s_ref[...] = jnp.zeros_like(s_ref)
            dfn_ref[...] = jnp.zeros_like(dfn_ref)

        first, last = i == 0, i == nt - 1
        dy = dy_ref[...]
        dye = jnp.concatenate([dy, jnp.where(last, 0.0, dyn_ref[...])], axis=0).astype(BF16)
        dh = jnp.zeros((FM, D), F32)
        row = lax.broadcasted_iota(jnp.int32, (8, SW), 0)
        for d in range(NG):
            du = _dot_nt(dye, dn_ref[d])
            tiles, c = [], []
            for s in range(2):
                j = s * NG + d
                before = jnp.where(first, jnp.zeros_like(ap_ref[j]), ap_ref[j])
                after = jnp.where(last, jnp.zeros_like(an_ref[j]), an_ref[j])
                e = jnp.concatenate([before, a_ref[j], after], axis=0).astype(F32)
                c.append(_conv(e, w_ref[s, d])[HALO:] + b_ref[s, d])
                tiles.append(e[HALO:HALO + FM])
            gl, gg = _gelu_and_grad(c[0])
            dcs = (du * c[1] * gg, du * gl)
            for s in range(2):
                j = s * NG + d
                dc, w = dcs[s], w_ref[s, d]
                d1 = pltpu.roll(dc, n - 1, 0)
                d2 = pltpu.roll(d1, n - 1, 0)
                da = (w[2:3, :] * dc + w[1:2, :] * d1 + w[0:1, :] * d2)[0:FM].astype(BF16)
                da_ref[j] = da
                dh = dh + _dot_nt(da, up_ref[j])
                sums = [jnp.sum(v[0:FM] * tiles[s], axis=0, keepdims=True) for v in (d2, d1, dc)]
                sums.append(jnp.sum(dc[0:FM], axis=0, keepdims=True))
                upd = jnp.zeros((8, SW), F32)
                for r, v in enumerate(sums):
                    upd = jnp.where(row == r, jnp.broadcast_to(v, (8, SW)), upd)
                s_ref[s, d] += upd
        x = x_ref[...]
        dx, dg = _rms_bwd(dh, x, _rs(x), g_ref[...])
        dx_ref[...] = dy + dx
        dfn_ref[...] += _colsum8(dg)

    last_halo = t // HALO - 1
    after = lambda i: jnp.minimum((i + 1) * (FM // HALO), last_halo)
    return pl.pallas_call(
        body, name="ffn_bwd", grid=(nt,),
        in_specs=[_rows(FM, D), pl.BlockSpec((HALO, D), lambda i: (after(i), 0)),
                  pl.BlockSpec((NDEV, FM, SW), lambda i: (0, i, 0)),
                  pl.BlockSpec((NDEV, HALO, SW), lambda i: (0, _halo_before(i), 0)),
                  pl.BlockSpec((NDEV, HALO, SW), lambda i: (0, after(i), 0)),
                  _rows(FM, D), _resident((NDEV, D, SW)), _resident((2, NG, 3, SW)), _resident((2, NG, 1, SW)),
                  _resident((NG, SW, D)), _const((1, D))],
        out_specs=[_rows(FM, D), pl.BlockSpec((NDEV, FM, SW), lambda i: (0, i, 0)), _const((2, NG, 8, SW)), _const((8, D))],
        out_shape=[_sds((t, D)), _sds((NDEV, t, SW), BF16), _sds((2, NG, 8, SW)), _sds((8, D))],
        compiler_params=_params())(dy, dy, a, a, a, x2, up, conv, conv_b, down, ffn_norm)


BT = 256


def _xattn_bwd(dx2, x1, qpre, k2, v2, wq, wo, qn4, xan):
    t = x1.shape[0]

    def body(dx2_ref, x1_ref, qpre_ref, k_ref, v_ref, wq_ref, wo_ref, qn_ref, xan_ref,
             dx1_ref, dqpre_ref, dk_ref, dv_ref, dqn_ref, dxan_ref):
        @pl.when(pl.program_id(0) == 0)
        def _():
            for r in (dk_ref, dv_ref, dqn_ref, dxan_ref):
                r[...] = jnp.zeros_like(r)

        dx2 = dx2_ref[...]
        do = _dot_nt(dx2.astype(BF16), wo_ref[...])
        dqs = []
        for h in range(XH):
            sl = slice(XD * h, XD * (h + 1))
            qpre_h = qpre_ref[:, sl]
            pn, q2, rq = _xattn_probs(qpre_h, qn_ref[:, sl], k_ref[:, sl])
            do_h = do[:, sl].astype(BF16)
            dp = _dot_nt(do_h, v_ref[:, sl])
            ds = (pn * (dp - jnp.sum(pn * dp, axis=-1, keepdims=True)) * (1.0 / math.sqrt(XD))).astype(BF16)
            dq2 = _dot(ds, k_ref[:, sl])
            dk_ref[:, sl] += _dot_tn(ds, q2)
            dv_ref[:, sl] += _dot_tn(pn.astype(BF16), do_h)
            dqh, dg = _rms_bwd(dq2, qpre_h, rq, qn_ref[:, sl])
            dqn_ref[...] += _colsum8(dg)
            dqs.append(dqh)
        dqpre = jnp.concatenate(dqs, axis=1).astype(BF16)
        dqpre_ref[...] = dqpre
        dh2 = _dot_nt(dqpre, wq_ref[...])
        x1 = x1_ref[...]
        dx, dg = _rms_bwd(dh2, x1, _rs(x1), xan_ref[...])
        dx1_ref[...] = dx2 + dx
        dxan_ref[...] += _colsum8(dg)

    return pl.pallas_call(
        body, name="xattn_bwd", grid=(t // BT,),
        in_specs=[_rows(BT, D), _rows(BT, D), _rows(BT, D), _const((MEM, D)), _const((MEM, D)), _const((D, D)),
                  _const((D, D)), _const((1, D)), _const((1, D))],
        out_specs=[_rows(BT, D), _rows(BT, D), _const((MEM, D)), _const((MEM, D)), _const((8, XD)), _const((8, D))],
        out_shape=[_sds((t, D)), _sds((t, D), BF16), _sds((MEM, D)), _sds((MEM, D)), _sds((8, XD)), _sds((8, D))],
        compiler_params=_params())(dx2, x1, qpre, k2, v2, wq, wo, qn4, xan)


def _mem_kv_bwd(mem, mh, kpre, dk2, dv2, wkv, kn4, mem_norm):
    def body(m_ref, mh_ref, kpre_ref, dk_ref, dv_ref, w_ref, kn_ref, g_ref, dw_ref, dkn_ref, dmn_ref):
        dkn = jnp.zeros((8, XD), F32)
        dm = jnp.zeros((MEM, D), F32)
        mh = mh_ref[...]
        for h in range(XH):
            sl = slice(XD * h, XD * (h + 1))
            kh = kpre_ref[:, sl]
            dkh, dg = _rms_bwd(dk_ref[:, sl], kh, _rs(kh), kn_ref[:, sl])
            dkn = dkn + _colsum8(dg)
            dkh = dkh.astype(BF16)
            dvh = dv_ref[:, sl].astype(BF16)
            dw_ref[h] = _dot_tn(mh, dkh).astype(BF16)
            dw_ref[XH + h] = _dot_tn(mh, dvh).astype(BF16)
            dm = dm + _dot_nt(dkh, w_ref[h]) + _dot_nt(dvh, w_ref[XH + h])
        dkn_ref[...] = dkn
        m = m_ref[...]
        _, dg = _rms_bwd(dm, m, _rs(m), g_ref[...])
        dmn_ref[...] = _colsum8(dg)

    return pl.pallas_call(
        body, name="mem_kv_bwd", out_shape=[_sds((2 * XH, D, XD), BF16), _sds((8, XD)), _sds((8, D))],
        compiler_params=pltpu.CompilerParams(vmem_limit_bytes=VMEM_LIMIT))(mem, mh, kpre, dk2, dv2, wkv, kn4, mem_norm)


def _mixer_out_bwd(dx1, attn, gm, w_out, aon, gon):
    t = dx1.shape[0]

    def body(dx_ref, a_ref, g_ref, w_ref, aon_ref, gon_ref, da_ref, dg_ref, dan_ref, dgn_ref):
        @pl.when(pl.program_id(0) == 0)
        def _():
            dan_ref[...] = jnp.zeros_like(dan_ref)
            dgn_ref[...] = jnp.zeros_like(dgn_ref)

        dy = _dot_nt(dx_ref[...].astype(BF16), w_ref[...])
        a, g = a_ref[...], g_ref[...]
        da, dna = _rms_bwd(dy[:, :AW], a, _rs(a), aon_ref[...])
        dg, dng = _rms_bwd(dy[:, AW:], g, _rs(g), gon_ref[...])
        da_ref[...] = da
        dg_ref[...] = dg
        dan_ref[...] += _colsum8(dna)
        dgn_ref[...] += _colsum8(dng)

    return pl.pallas_call(
        body, name="mixer_out_bwd", grid=(t // TM,),
        in_specs=[_rows(TM, D), _rows(TM, AW), _rows(TM, GW), _const((D, D)), _const((1, AW)), _const((1, GW))],
        out_specs=[_rows(TM, AW), _rows(TM, GW), _const((8, AW)), _const((8, GW))],
        out_shape=[_sds((t, AW)), _sds((t, GW)), _sds((8, AW)), _sds((8, GW))],
        compiler_params=_params())(dx1, attn, gm, w_out, aon, gon)


def _gmlp_bwd(dgm, gu, gvn, gz, ws, wst, bfull, gvw):
    t = dgm.shape[0]
    nc = TM // BLK

    def body(dgm_ref, gu_ref, x_ref, gz_ref, w_ref, wt_ref, b_ref, gvw_ref, dgz_ref, dw_ref, db_ref, dgvw_ref,
             dbacc_ref):
        @pl.when(pl.program_id(0) == 0)
        def _():
            for r in (dw_ref, dbacc_ref, dgvw_ref):
                r[...] = jnp.zeros_like(r)

        xin = x_ref[...]
        dgm = dgm_ref[...]
        mixed = _gmlp_mix(w_ref, xin, False) + jnp.tile(b_ref[...], (nc, 1))
        dgu = dgm * mixed
        dmixed = dgm * gu_ref[...]
        lo = _lane((BLK, 128)) < 64
        dbias = jnp.zeros((BLK, GW), F32)
        for c in range(nc):
            dmc = dmixed[c * BLK:(c + 1) * BLK]
            dbias = dbias + dmc
            for j in range(4):
                dm2 = dmc[:, 128 * j:128 * (j + 1)]
                xs = xin[c * BLK:(c + 1) * BLK, 128 * j:128 * (j + 1)]
                z = jnp.zeros_like(dm2)
                dw_ref[2 * j] += _dot_nt(jnp.where(lo, dm2, z).astype(BF16), xs)
                dw_ref[2 * j + 1] += _dot_nt(jnp.where(lo, z, dm2).astype(BF16), xs)
        dbacc_ref[...] += dbias
        dgvn = _gmlp_mix(wt_ref, dmixed.astype(BF16), True)
        gz_u, gz_v = gz_ref[:, :GW], gz_ref[:, GW:]
        gv = _gelu(gz_v)
        dgv, dg = _rms_bwd(dgvn, gv, _rs(gv), gvw_ref[...])
        dgvw_ref[...] += _colsum8(dg)
        dgz_ref[:, :GW] = (dgu * _gelu_grad(gz_u)).astype(BF16)
        dgz_ref[:, GW:] = (dgv * _gelu_grad(gz_v)).astype(BF16)

        @pl.when(pl.program_id(0) == pl.num_programs(0) - 1)
        def _():
            s = dbacc_ref[...]
            sel = (lax.broadcasted_iota(jnp.int32, (8, GW), 1) // HD
                   == lax.broadcasted_iota(jnp.int32, (8, GW), 0)).astype(BF16)
            hi = s.astype(BF16)
            r1 = s - hi.astype(F32)
            mid = r1.astype(BF16)
            lo = (r1 - mid.astype(F32)).astype(BF16)
            db_ref[...] = _dot_nt(sel, hi) + _dot_nt(sel, mid) + _dot_nt(sel, lo)
            r = lax.broadcasted_iota(jnp.int32, (BLK, BLK), 0)
            c = lax.broadcasted_iota(jnp.int32, (BLK, BLK), 1)
            for h in range(8):
                dw_ref[h] = jnp.where(c <= r, dw_ref[h], 0.0)

    return pl.pallas_call(
        body, name="gmlp_bwd", grid=(t // TM,),
        in_specs=[_rows(TM, GW), _rows(TM, GW), _rows(TM, GW), _rows(TM, 2 * GW), _const((8, BLK, BLK)),
                  _const((8, BLK, BLK)), _const((BLK, GW)), _const((1, GW))],
        out_specs=[_rows(TM, 2 * GW), _const((8, BLK, BLK)), _const((8, BLK)), _const((8, GW))],
        out_shape=[_sds((t, 2 * GW), BF16), _sds((8, BLK, BLK)), _sds((8, BLK)), _sds((8, GW))],
        scratch_shapes=[pltpu.VMEM((BLK, GW), F32)],
        compiler_params=_params())(dgm, gu, gvn, gz, ws, wst, bfull, gvw)


def _fold_half(v):
    return v + pltpu.roll(v, 64, 1)


def _swa_bwd(q, k, v, dattn, sinks):
    t = q.shape[0]
    nb = t // BLK

    def body(sink_ref, q_ref, kc_ref, kp_ref, vc_ref, vp_ref, do_ref, dq_ref, dk_ref, dv_ref, ds_ref,
             ck_ref, cv_ref, sacc_ref):
        i = pl.program_id(0)

        @pl.when(i == 0)
        def _():
            ck_ref[...] = jnp.zeros_like(ck_ref)
            cv_ref[...] = jnp.zeros_like(cv_ref)
            sacc_ref[...] = jnp.zeros_like(sacc_ref)

        @pl.when(i < nb)
        def _():
            mask = _swa_mask(i == 0)
            kk = jnp.concatenate([kp_ref[...], kc_ref[...]], axis=0).astype(F32)
            vv = jnp.concatenate([vp_ref[...], vc_ref[...]], axis=0).astype(F32)
            lo256 = _lane((2 * BLK, 128)) < 64
            dkk = jnp.zeros((2 * BLK, 128), F32)
            dvv = jnp.zeros((2 * BLK, 128), F32)
            for g in range(2):
                qs = _stack_heads(q_ref[:, 256 * g:256 * g + 128], q_ref[:, 256 * g + 128:256 * g + 256])
                dos = _stack_heads(do_ref[:, 256 * g:256 * g + 128],
                                   do_ref[:, 256 * g + 128:256 * g + 256]).astype(BF16)
                kd = _dup_half(kk, g)
                pn, psn = _swa_probs(qs, kd, mask, _sink_col(sink_ref, g))
                dp = _dot_nt(dos, _dup_half(vv, g))
                dd = jnp.sum(pn * dp, axis=-1, keepdims=True)
                ds = (pn * (dp - dd) * (1.0 / math.sqrt(HD))).astype(BF16)
                sacc_ref[g] += jnp.broadcast_to(-psn * dd, (4 * BLK, 128))
                dqa, dqb = _unstack_heads(_dot(ds, kd))
                dq_ref[:, 256 * g:256 * g + 128] = dqa
                dq_ref[:, 256 * g + 128:256 * g + 256] = dqb
                dkg = _fold_half(_dot_tn(ds, qs))
                dvg = _fold_half(_dot_tn(pn.astype(BF16), dos))
                keep = lo256 if g == 0 else jnp.logical_not(lo256)
                dkk = jnp.where(keep, dkg, dkk)
                dvv = jnp.where(keep, dvg, dvv)
            dk_ref[...] = ck_ref[...] + dkk[0:BLK]
            dv_ref[...] = cv_ref[...] + dvv[0:BLK]
            ck_ref[...] = dkk[BLK:]
            cv_ref[...] = dvv[BLK:]

        @pl.when(i == nb)
        def _():
            dk_ref[...] = ck_ref[...]
            dv_ref[...] = cv_ref[...]
            lane = _lane((8, 128))
            acc = jnp.zeros((8, 128), F32)
            for g in range(2):
                for j in range(4):
                    val = jnp.sum(sacc_ref[g, j * BLK:(j + 1) * BLK, :], axis=0, keepdims=True)
                    acc = jnp.where(lane == 4 * g + j, jnp.broadcast_to(val, (8, 128)), acc)
            ds_ref[...] = acc

    cur = lambda i: (jnp.minimum(i, nb - 1), 0)
    prev = lambda i: (jnp.clip(i - 1, 0, nb - 1), 0)
    return pl.pallas_call(
        body, name="swa_bwd", grid=(nb + 1,),
        in_specs=[pl.BlockSpec(memory_space=pltpu.SMEM), pl.BlockSpec((BLK, AW), cur),
                  pl.BlockSpec((BLK, KW), cur), pl.BlockSpec((BLK, KW), prev),
                  pl.BlockSpec((BLK, KW), cur), pl.BlockSpec((BLK, KW), prev), pl.BlockSpec((BLK, AW), cur)],
        out_specs=[pl.BlockSpec((BLK, AW), cur), pl.BlockSpec((BLK, KW), prev), pl.BlockSpec((BLK, KW), prev),
                   _const((8, 128))],
        out_shape=[_sds((t, AW)), _sds((t, KW)), _sds((t, KW)), _sds((8, 128))],
        scratch_shapes=[pltpu.VMEM((BLK, KW), F32), pltpu.VMEM((BLK, KW), F32), pltpu.VMEM((2, 4 * BLK, 128), F32)],
        compiler_params=_params())(sinks, q, k, k, v, v, dattn)


def _mixer_in_bwd(dq, dk, dv, dgz, qk, cos, sin, x, dx1, w_in, mix_norm, qn, kn):
    t = x.shape[0]

    def body(dq_ref, dk_ref, dv_ref, dgz_ref, qk_ref, cos_ref, sin_ref, x_ref, dx1_ref, w_ref, g_ref, qn_ref, kn_ref,
             ones_ref, gx_ref, dproj_ref, dmn_ref, dqn_ref, dkn_ref, qacc_ref, kacc_ref):
        i = pl.program_id(0)

        @pl.when(i == 0)
        def _():
            dmn_ref[...] = jnp.zeros_like(dmn_ref)
            qacc_ref[...] = jnp.zeros_like(qacc_ref)
            kacc_ref[...] = jnp.zeros_like(kacc_ref)

        cos2, sin2 = cos_ref[...], sin_ref[...]
        qpre, kpre = qk_ref[:, :AW], qk_ref[:, AW:]
        dqh = _rope_bwd(dq_ref[...], jnp.tile(cos2, (1, 4)), jnp.tile(sin2, (1, 4)))
        dqpre, dgq = _rms64_bwd(dqh, qpre, _rs64(qpre, ones_ref), qn_ref[...], ones_ref)
        dkh = _rope_bwd(dk_ref[...], cos2, sin2)
        dkpre, dgk = _rms64_bwd(dkh, kpre, _rs64(kpre, ones_ref), kn_ref[...], ones_ref)
        qacc_ref[...] += jnp.sum(dgq, axis=0, keepdims=True)
        kacc_ref[...] += jnp.sum(dgk, axis=0, keepdims=True)
        dproj = jnp.concatenate([dqpre.astype(BF16), dkpre.astype(BF16), dv_ref[...].astype(BF16), dgz_ref[...]], axis=1)
        dproj_ref[...] = dproj
        dh = _dot_nt(dproj, w_ref[...])
        xv = x_ref[...]
        dx, dg = _rms_bwd(dh, xv, _rs(xv), g_ref[...])
        gx_ref[...] = dx1_ref[...] + dx
        dmn_ref[...] += _colsum8(dg)

        @pl.when(i == pl.num_programs(0) - 1)
        def _():
            qa = qacc_ref[...]
            q4 = qa[:, 0:128] + qa[:, 128:256] + qa[:, 256:384] + qa[:, 384:512]
            dqn_ref[...] = jnp.broadcast_to(_fold_half(q4), (8, 128))
            dkn_ref[...] = jnp.broadcast_to(_fold_half(kacc_ref[...]), (8, 128))

    return pl.pallas_call(
        body, name="mixer_in_bwd", grid=(t // TM,),
        in_specs=[_rows(TM, AW), _rows(TM, KW), _rows(TM, KW), _rows(TM, 2 * GW), _rows(TM, AW + KW), _rows(TM, 128),
                  _rows(TM, 128), _rows(TM, D), _rows(TM, D), _const((D, IN)), _const((1, D)), _const((1, AW)),
                  _const((1, KW)), _const((AW, AW))],
        out_specs=[_rows(TM, D), _rows(TM, IN), _const((8, D)), _const((8, 128)), _const((8, 128))],
        out_shape=[_sds((t, D)), _sds((t, IN), BF16), _sds((8, D)), _sds((8, 128)), _sds((8, 128))],
        scratch_shapes=[pltpu.VMEM((1, AW), F32), pltpu.VMEM((1, KW), F32)],
        compiler_params=_params())(dq, dk, dv, dgz, qk, cos, sin, x, dx1, w_in, mix_norm, qn, kn, _head_ones())


def _local_step(x, mem, pos, target, p, fetch, ship):
    t = x.shape[0]
    p = dict(p)
    p.update(fetch(0, None))
    inv_freq = 1.0 / (ROPE_THETA ** (jnp.arange(HD // 2, dtype=F32) * (2.0 / HD)))
    cos, sin = _rope_tables(pos, jnp.tile(inv_freq, 4).reshape(1, 128))
    qn = jnp.tile(p["q_norm"], (1, AW // HD))
    kn = jnp.tile(p["k_norm"], (1, KW // HD))
    qn4 = jnp.tile(p["xa_q_norm"], (1, XH))
    kn4 = jnp.tile(p["xa_k_norm"], (1, XH))
    ws = p["gmlp_ws"]
    wst = jnp.swapaxes(ws, 1, 2)
    bfull = jnp.repeat(p["gmlp_bs"].T, HD, axis=1)
    conv_b = p["ffn_conv_b"]

    h1, qk, gz, q, k, v, gu, gvn = _mixer_in_fwd(x, p["mix_norm"], p["w_in"], qn, kn, p["gmlp_v_norm"], cos, sin)
    attn = _swa_fwd(q, k, v, p["attn_sinks"])
    gm = _gmlp_fwd(gvn, gu, ws, bfull)
    ycat, x1, h2 = _mixer_out_fwd(attn, gm, x, p["w_out"], p["attn_out_norm"], p["gmlp_out_norm"], p["xa_norm"])
    p.update(fetch(1, h2))
    mh, kpre, k2, v2 = _mem_kv_fwd(mem, p["mem_norm"], p["xa_wkv"], kn4)
    qpre, o, x2, h3 = _xattn_fwd(h2, x1, p["xa_wq"], qn4, k2, v2, p["xa_wo"], p["ffn_norm"])
    p.update(fetch(2, h3))
    conv = p["ffn_conv"]
    a, u, dy, loss8 = _ffn_fwd(h3, x2, target, p["ffn_up"], conv, conv_b, p["ffn_down"])

    raw = {}
    d_down = _mm_tn(u, dy, "ffn_down_bwd_w")
    dx2, da, raw["conv_sums"], raw["ffn_norm"] = _ffn_bwd(dy, a, x2, p["ffn_up"], conv, conv_b, p["ffn_down"], p["ffn_norm"])
    d_up = _mm_tn(h3, da, "ffn_up_bwd_w")
    token = ship(0, {"ffn_down": d_down, "ffn_up": d_up, "ffn_conv": raw["conv_sums"][:, :, 0:3]})
    dx1, dqpre, dk2, dv2, raw["xa_q_norm"], raw["xa_norm"] = _xattn_bwd(
        dx2, x1, qpre, k2, v2, p["xa_wq"], p["xa_wo"], qn4 + jnp.tile(token[0:1], (1, D // 128)), p["xa_norm"])
    d_wo = _mm_tn(o, dx2, "xa_wo_bwd_w")
    d_wq = _mm_tn(h2, dqpre, "xa_wq_bwd_w")
    d_wkv, raw["xa_k_norm"], raw["mem_norm"] = _mem_kv_bwd(mem, mh, kpre, dk2, dv2, p["xa_wkv"], kn4, p["mem_norm"])
    d_w_out = _mm_tn(ycat, dx1, "w_out_bwd_w")
    token = ship(1, {"xa_wo": d_wo, "xa_wq": d_wq, "xa_wkv": d_wkv, "w_out": d_w_out})
    dattn, dgm, raw["attn_out_norm"], raw["gmlp_out_norm"] = _mixer_out_bwd(
        dx1, attn, gm, p["w_out"], p["attn_out_norm"] + jnp.tile(token[0:1], (1, AW // 128)), p["gmlp_out_norm"])
    dgz, raw["gmlp_ws"], raw["gmlp_bs"], raw["gmlp_v_norm"] = _gmlp_bwd(dgm, gu, gvn, gz, ws, wst, bfull, p["gmlp_v_norm"])
    dq, dk, dv, raw["attn_sinks"] = _swa_bwd(q, k, v, dattn, p["attn_sinks"])
    grad_x, dproj, raw["mix_norm"], raw["q_norm"], raw["k_norm"] = _mixer_in_bwd(
        dq, dk, dv, dgz, qk, cos, sin, x, dx1, p["w_in"], p["mix_norm"], qn, kn)
    d_w_in = _mm_tn(h1, dproj, "w_in_bwd_w")
    return loss8[0, 0], grad_x, {"w_in": d_w_in}, raw


def _cast_shards(shards):
    def body(*refs):
        n = len(refs) // 2
        for i_ref, o_ref in zip(refs[:n], refs[n:]):
            o_ref[...] = i_ref[0].astype(BF16)

    return pl.pallas_call(body, name="cast_shards", out_shape=[_sds(s.shape[1:], BF16) for s in shards],
                          compiler_params=pltpu.CompilerParams(vmem_limit_bytes=VMEM_LIMIT))(*shards)


HBM_SPEC = pl.BlockSpec(memory_space=pltpu.HBM)
SEM_SPEC = pl.BlockSpec(memory_space=pltpu.SEMAPHORE)


def _remote_copies(src_refs, land_refs, send_refs, recv_refs, nd):
    x, y, cc = lax.axis_index("x"), lax.axis_index("y"), lax.axis_index("c")
    me = 4 * x + 2 * y + cc
    copies = []
    for a, (src_ref, land_ref) in enumerate(zip(src_refs, land_refs)):
        for k in range(1, NDEV):
            px = 1 - x if k & 4 else x
            py = 1 - y if k & 2 else y
            pc = 1 - cc if k & 1 else cc
            copies.append(pltpu.make_async_remote_copy(
                src_ref=src_ref.at[4 * px + 2 * py + pc] if a < nd else src_ref, dst_ref=land_ref.at[me],
                send_sem=send_refs[a].at[k - 1], recv_sem=recv_refs[a].at[k - 1],
                device_id=(px, py, pc), device_id_type=pl.DeviceIdType.MESH))
    return copies


def _own_slot(src, by_dest, me):
    block = lax.dynamic_index_in_dim(src, me, 0, keepdims=True) if by_dest else src[None]
    return lax.dynamic_update_index_in_dim(lax.empty((NDEV,) + block.shape[1:], src.dtype), block, me, 0)


def _exchange_start(by_dest, for_all, me, name):
    srcs = list(by_dest) + list(for_all)
    n, nd = len(srcs), len(by_dest)
    lands = [_own_slot(s, a < nd, me) for a, s in enumerate(srcs)]

    def body(*refs):
        for cp in _remote_copies(refs[:n], refs[n:2 * n], refs[2 * n:3 * n], refs[3 * n:4 * n], nd):
            cp.start()
        refs[-1][...] = jnp.zeros((8, 128), F32)

    sems = [pltpu.SemaphoreType.DMA((NDEV - 1,))] * (2 * n)
    thru = [pltpu.HBM(v.shape, v.dtype) for v in srcs + lands]
    res = pl.pallas_call(
        body, name=name, out_shape=sems + thru + [_sds((8, 128))],
        in_specs=[HBM_SPEC] * (2 * n), out_specs=[SEM_SPEC] * (2 * n) + [HBM_SPEC] * (2 * n) + [pl.BlockSpec(memory_space=pltpu.VMEM)],
        input_output_aliases={i: 2 * n + i for i in range(2 * n)},
        compiler_params=pltpu.CompilerParams(has_side_effects=pltpu.SideEffectType.DATAFLOW_SIDE_EFFECTING))(
            *[pltpu.with_memory_space_constraint(v, pltpu.HBM) for v in srcs + lands])
    return (res[:2 * n], res[2 * n:4 * n], nd), res[-1]


def _exchange_wait(state, after, name):
    sems, thru, nd = state
    n = len(thru) // 2

    def body(*refs):
        for cp in _remote_copies(refs[:n], refs[n:2 * n], refs[2 * n:3 * n], refs[3 * n:4 * n], nd):
            cp.wait_send()
            cp.wait_recv()

    res = pl.pallas_call(
        body, name=name, out_shape=[pltpu.HBM(v.shape, v.dtype) for v in thru],
        in_specs=[HBM_SPEC] * (2 * n) + [SEM_SPEC] * (2 * n) + [pl.BlockSpec(memory_space=pl.ANY)],
        out_specs=[HBM_SPEC] * (2 * n), input_output_aliases={i: i for i in range(2 * n)},
        compiler_params=pltpu.CompilerParams(has_side_effects=pltpu.SideEffectType.DATAFLOW_SIDE_EFFECTING))(
            *thru, *sems, after)
    return res[n:]


def _adam(parts, w, m, v, name):
    def body(p_ref, w_ref, m_ref, v_ref, g_ref, d_ref, nm_ref, nv_ref):
        g = _sum_parts(p_ref)
        g_ref[0] = g
        d_ref[0], nm_ref[0], nv_ref[0] = _adam_math(g, w_ref[0], m_ref[0], v_ref[0])

    return pl.pallas_call(
        body, name=name, out_shape=[_sds(w.shape)] * 4,
        compiler_params=pltpu.CompilerParams(vmem_limit_bytes=VMEM_LIMIT))(parts, w, m, v)


GATHER_GROUPS = (("w_in", "w_out"), ("xa_wkv", "xa_wq", "xa_wo"), ("ffn_up", "ffn_conv", "ffn_down"))
SCATTER_GROUPS = (("ffn_down", "ffn_up", "ffn_conv"), ("xa_wo", "xa_wq", "xa_wkv", "w_out"), ("w_in",))
BIG = tuple(n for grp in GATHER_GROUPS for n in grp)
VECS = (("mix_norm", D), ("q_norm", HD), ("k_norm", HD), ("attn_sinks", 8), ("gmlp_v_norm", GW), ("attn_out_norm", AW),
        ("gmlp_out_norm", GW), ("xa_norm", D), ("mem_norm", D), ("xa_q_norm", XD), ("xa_k_norm", XD), ("ffn_norm", D))
BS_ROW = 16
VEC_ROWS = 24
SMALL = tuple(n for n, _ in VECS) + ("gmlp_bs", "gmlp_ws", "ffn_conv_b")


def _pack_small(raw):
    names = [n for n, _ in VECS] + ["gmlp_bs", "conv_sums"]

    def body(*refs):
        ins = dict(zip(names, refs))
        vec_ref, cb_ref = refs[len(names):]
        vec_ref[...] = jnp.zeros_like(vec_ref)
        for r, (n, w) in enumerate(VECS):
            vec_ref[r:r + 1, 0:w] = ins[n][0:1, 0:w]
        vec_ref[BS_ROW:BS_ROW + 8, 0:BLK] = ins["gmlp_bs"][...]
        for s in range(2):
            for d in range(NG):
                cb_ref[s, d] = ins["conv_sums"][s, d, 3:4, :]

    return pl.pallas_call(body, name="pack_small", out_shape=[_sds((VEC_ROWS, D)), _sds((2, NG, 1, SW))])(
        *[raw[n] for n in names])


def _adam_math(g, w, m, v):
    nm = B1 * m + (1.0 - B1) * g
    nv = B2 * v + (1.0 - B2) * (g * g)
    m_hat = nm / (1.0 - B1 ** STEP)
    v_hat = nv / (1.0 - B2 ** STEP)
    return -LR * (m_hat / (jnp.sqrt(v_hat) + AEPS) + WD * w), nm, nv


def _sum_parts(p_ref):
    g = p_ref[0].astype(F32)
    for j in range(1, NDEV):
        g = g + p_ref[j].astype(F32)
    return g


def _adam_small(parts_vec, parts_ws, parts_cb, w, m, v):
    def body(*refs):
        pv_ref, pws_ref, pcb_ref = refs[:3]
        ins = refs[3:3 + 3 * len(SMALL)]
        outs = refs[3 + 3 * len(SMALL):]
        gv = _sum_parts(pv_ref)
        for j, n in enumerate(SMALL):
            w_ref, m_ref, v_ref = ins[3 * j:3 * j + 3]
            o = outs[4 * j:4 * j + 4]
            if n == "gmlp_ws":
                g = _sum_parts(pws_ref)
            elif n == "ffn_conv_b":
                g = _sum_parts(pcb_ref)
            elif n == "gmlp_bs":
                g = gv[BS_ROW:BS_ROW + 8, 0:BLK]
            else:
                g = gv[j:j + 1, 0:VECS[j][1]]
            lead = n in ("gmlp_ws", "gmlp_bs")
            res = (g,) + _adam_math(g, w_ref[0] if lead else w_ref[...], m_ref[0] if lead else m_ref[...],
                                    v_ref[0] if lead else v_ref[...])
            for o_ref, val in zip(o, res):
                if lead:
                    o_ref[0] = val
                else:
                    o_ref[...] = val

    args = [parts_vec, parts_ws, parts_cb] + [d[n] for n in SMALL for d in (w, m, v)]
    res = pl.pallas_call(body, name="adam_small", out_shape=[_sds(w[n].shape) for n in SMALL for _ in range(4)],
                         compiler_params=pltpu.CompilerParams(vmem_limit_bytes=VMEM_LIMIT))(*args)
    return {n: tuple(res[4 * j:4 * j + 4]) for j, n in enumerate(SMALL)}


def kernel(x, mem, positions, mix_norm, w_in, q_norm, k_norm, attn_sinks, gmlp_v_norm, gmlp_ws, gmlp_bs, attn_out_norm, gmlp_out_norm, w_out, xa_norm, mem_norm, xa_wq, xa_wkv, xa_q_norm, xa_k_norm, xa_wo, ffn_norm, ffn_up, ffn_conv, ffn_conv_b, ffn_down, loss_target, m_mix_norm, m_w_in, m_q_norm, m_k_norm, m_attn_sinks, m_gmlp_v_norm, m_gmlp_ws, m_gmlp_bs, m_attn_out_norm, m_gmlp_out_norm, m_w_out, m_xa_norm, m_mem_norm, m_xa_wq, m_xa_wkv, m_xa_q_norm, m_xa_k_norm, m_xa_wo, m_ffn_norm, m_ffn_up, m_ffn_conv, m_ffn_conv_b, m_ffn_down, v_mix_norm, v_w_in, v_q_norm, v_k_norm, v_attn_sinks, v_gmlp_v_norm, v_gmlp_ws, v_gmlp_bs, v_attn_out_norm, v_gmlp_out_norm, v_w_out, v_xa_norm, v_mem_norm, v_xa_wq, v_xa_wkv, v_xa_q_norm, v_xa_k_norm, v_xa_wo, v_ffn_norm, v_ffn_up, v_ffn_conv, v_ffn_conv_b, v_ffn_down):
    names = ("mix_norm", "w_in", "q_norm", "k_norm", "attn_sinks", "gmlp_v_norm", "gmlp_ws", "gmlp_bs", "attn_out_norm",
             "gmlp_out_norm", "w_out", "xa_norm", "mem_norm", "xa_wq", "xa_wkv", "xa_q_norm", "xa_k_norm", "xa_wo",
             "ffn_norm", "ffn_up", "ffn_conv", "ffn_conv_b", "ffn_down")
    w = dict(zip(names, (mix_norm, w_in, q_norm, k_norm, attn_sinks, gmlp_v_norm, gmlp_ws, gmlp_bs, attn_out_norm,
                         gmlp_out_norm, w_out, xa_norm, mem_norm, xa_wq, xa_wkv, xa_q_norm, xa_k_norm, xa_wo, ffn_norm,
                         ffn_up, ffn_conv, ffn_conv_b, ffn_down)))
    m = dict(zip(names, (m_mix_norm, m_w_in, m_q_norm, m_k_norm, m_attn_sinks, m_gmlp_v_norm, m_gmlp_ws, m_gmlp_bs,
                         m_attn_out_norm, m_gmlp_out_norm, m_w_out, m_xa_norm, m_mem_norm, m_xa_wq, m_xa_wkv,
                         m_xa_q_norm, m_xa_k_norm, m_xa_wo, m_ffn_norm, m_ffn_up, m_ffn_conv, m_ffn_conv_b, m_ffn_down)))
    v = dict(zip(names, (v_mix_norm, v_w_in, v_q_norm, v_k_norm, v_attn_sinks, v_gmlp_v_norm, v_gmlp_ws, v_gmlp_bs,
                         v_attn_out_norm, v_gmlp_out_norm, v_w_out, v_xa_norm, v_mem_norm, v_xa_wq, v_xa_wkv,
                         v_xa_q_norm, v_xa_k_norm, v_xa_wo, v_ffn_norm, v_ffn_up, v_ffn_conv, v_ffn_conv_b, v_ffn_down)))
    t = x.shape[1]

    me = 4 * lax.axis_index("x") + 2 * lax.axis_index("y") + lax.axis_index("c")

    mats = [n for n in BIG if n != "ffn_conv"]
    shard = dict(zip(mats, _cast_shards([w[n] for n in mats])), ffn_conv=w["ffn_conv"][0])
    gathers, tokens = zip(*[_exchange_start([], [shard[n] for n in grp], me, "gather_start_%d" % i)
                            for i, grp in enumerate(GATHER_GROUPS)])

    def fetch(i, after):
        after = tokens[0] + tokens[1] + tokens[2] if after is None else after
        got = dict(zip(GATHER_GROUPS[i], _exchange_wait(gathers[i], after, "gather_wait_%d" % i)))
        if "w_in" in got:
            got["w_in"] = got["w_in"].transpose(1, 0, 2).reshape(D, IN)
        for n in ("w_out", "xa_wq", "xa_wo"):
            if n in got:
                got[n] = got[n].reshape(D, D)
        if "ffn_down" in got:
            got["ffn_down"] = got["ffn_down"].reshape(NG, SW, D)
            got["ffn_conv"] = got["ffn_conv"].reshape(2, NG, 3, SW)
        return got

    scatters = []

    def ship(i, grads, for_all=()):
        by_dest = [grads[n].reshape((NDEV,) + w[n].shape[1:]) for n in SCATTER_GROUPS[i]]
        state, token = _exchange_start(by_dest, for_all, me, "scatter_start_%d" % i)
        scatters.append(state)
        return token

    conv_b = {k: d["ffn_conv_b"].reshape(NDEV, 1, SW) for k, d in (("w", w), ("m", m), ("v", v))}
    p = {n: w[n] for n in SMALL[:-1]}
    p["gmlp_ws"], p["gmlp_bs"] = w["gmlp_ws"][0], w["gmlp_bs"][0]
    p["ffn_conv_b"] = conv_b["w"].reshape(2, NG, 1, SW)
    loss, grad_x, g, raw = _local_step(x[0], mem[0], positions.reshape(t, 1), loss_target[0], p, fetch, ship)
    loss = lax.psum(loss, AXES)

    vec, cb = _pack_small(raw)
    g["w_in"] = g["w_in"].reshape(D, NDEV, IN // NDEV).transpose(1, 0, 2)
    token = ship(2, g, [vec, raw["gmlp_ws"], cb.reshape(NDEV, 1, SW)])
    parts = {}
    for i, grp in enumerate(SCATTER_GROUPS):
        got = _exchange_wait(scatters[i], token, "scatter_wait_%d" % i)
        parts.update(zip(grp, got))
        rest = got[len(grp):]
    res = {n: _adam(parts[n], w[n], m[n], v[n], "adam_" + n) for n in BIG}
    small = lambda d, k: {**{n: d[n] for n in SMALL[:-1]}, "ffn_conv_b": conv_b[k]}
    res.update(_adam_small(*rest, small(w, "w"), small(m, "m"), small(v, "v")))

    outs = [loss, grad_x[None]]
    for j in range(4):
        outs += [res[n][j].reshape(w[n].shape) for n in names]
    return tuple(outs)
```

```python
import functools
import math

import jax
import jax.numpy as jnp
from jax import lax
from jax.experimental import pallas as pl
from jax.experimental.pallas import tpu as pltpu

F32 = jnp.float32
BF16 = jnp.bfloat16

D = 1024
HD = 64
AW = 512
KW = 128
GW = 512
IN = AW + 2 * KW + 2 * GW
BLK = 128
MEM = 256
XH = 4
XD = 256
FF = 2816
EPS = 1e-6
ROPE_THETA = 10000.0
NDEV = 8
LR, B1, B2, AEPS, WD, STEP = 0.001, 0.9, 0.999, 1e-08, 0.01, 10

TM = 512
WK = 2048
VMEM_LIMIT = 56 * 1024 * 1024
NEG = float(jnp.finfo(jnp.float32).min)
GELU_C0 = math.sqrt(2.0 / math.pi)
GELU_C1 = 0.044715
AXES = ("x", "y", "c")


def _dot(a, b):
    return jnp.dot(a, b, preferred_element_type=F32)


def _dot_nt(a, b):
    return lax.dot_general(a, b, (((1,), (1,)), ((), ())), preferred_element_type=F32)


def _dot_tn(a, b):
    return lax.dot_general(a, b, (((0,), (0,)), ((), ())), preferred_element_type=F32)


def _rs(x):
    return lax.rsqrt(jnp.mean(x * x, axis=-1, keepdims=True) + EPS)


def _rms_bwd(dy, x, r, g):
    xh = x * r
    dxh = dy * g
    dx = r * (dxh - xh * jnp.mean(dxh * xh, axis=-1, keepdims=True))
    return dx, dy * xh


def _lane(shape):
    return lax.broadcasted_iota(jnp.int32, shape, len(shape) - 1)


def _gsum64(v, ones_ref):
    w = v.shape[-1]
    ones = ones_ref[0:w, 0:w]
    hi = v.astype(BF16)
    lo = (v - hi.astype(F32)).astype(BF16)
    return _dot(hi, ones) + _dot(lo, ones)


def _head_ones():
    i = jnp.arange(AW) // HD
    return (i[:, None] == i[None, :]).astype(BF16)


def _rs64(x, ones_ref):
    return lax.rsqrt(_gsum64(x * x, ones_ref) * (1.0 / HD) + EPS)


def _rms64_bwd(dy, x, r, g, ones_ref):
    xh = x * r
    dxh = dy * g
    dx = r * (dxh - xh * (_gsum64(dxh * xh, ones_ref) * (1.0 / HD)))
    return dx, dy * xh


def _rot_half(v):
    w = v.shape[-1]
    return jnp.where((_lane(v.shape) & 32) == 0, pltpu.roll(v, w - 32, 1), pltpu.roll(v, 32, 1))


def _rope(v, cos, sin_signed):
    return v * cos + _rot_half(v) * sin_signed


def _rope_bwd(dv, cos, sin_signed):
    return dv * cos + _rot_half(dv * sin_signed)


def _gelu(z):
    return 0.5 * z * (1.0 + jnp.tanh(GELU_C0 * (z + GELU_C1 * z * z * z)))


def _gelu_grad(z):
    t = jnp.tanh(GELU_C0 * (z + GELU_C1 * z * z * z))
    return 0.5 * (1.0 + t) + 0.5 * z * (1.0 - t * t) * (GELU_C0 * (1.0 + 3.0 * GELU_C1 * z * z))


def _colsum8(v):
    s = jnp.sum(v, axis=0, keepdims=True)
    row = lax.broadcasted_iota(jnp.int32, (8, v.shape[1]), 0)
    return jnp.where(row == 0, jnp.broadcast_to(s, (8, v.shape[1])), 0.0)


def _params(n_axes=1):
    return pltpu.CompilerParams(dimension_semantics=("arbitrary",) * n_axes, vmem_limit_bytes=VMEM_LIMIT)


def _rows(tm, w):
    return pl.BlockSpec((tm, w), lambda i: (i, 0))


def _const(shape):
    nd = len(shape)
    return pl.BlockSpec(shape, lambda *_: (0,) * nd)


def _sds(shape, dtype=F32):
    return jax.ShapeDtypeStruct(shape, dtype)


def _mm_tn(a, b, name):
    g = max(a.shape[0] if a.ndim == 3 else 1, b.shape[0] if b.ndim == 3 else 1)
    t, m = a.shape[-2:]
    n = b.shape[-1]

    def body(a_ref, b_ref, o_ref, acc_ref):
        i = pl.program_id(1)

        @pl.when(i == 0)
        def _():
            acc_ref[...] = jnp.zeros_like(acc_ref)

        acc_ref[...] += _dot_tn(a_ref[...].astype(BF16), b_ref[...].astype(BF16))

        @pl.when(i == pl.num_programs(1) - 1)
        def _():
            o_ref[...] = acc_ref[...].astype(BF16)

    tk = min(t, WK)

    def spec(v):
        w = v.shape[-1]
        if v.ndim == 3:
            return pl.BlockSpec((None, tk, w), lambda j, i: (j, i, 0))
        return pl.BlockSpec((tk, w), lambda j, i: (i, 0))

    return pl.pallas_call(
        body, name=name, grid=(g, t // tk), in_specs=[spec(a), spec(b)],
        out_specs=pl.BlockSpec((None, m, n), lambda j, i: (j, 0, 0)), out_shape=_sds((g, m, n), BF16),
        scratch_shapes=[pltpu.VMEM((m, n), F32)], compiler_params=_params(2))(a, b)


def _rope_tables(pos, inv_freq):
    t = pos.shape[0]

    def body(pos_ref, f_ref, cos_ref, sin_ref):
        ang = pos_ref[...].astype(F32) * f_ref[...]
        sign = jnp.where((_lane(ang.shape) & 32) == 0, -1.0, 1.0)
        cos_ref[...] = jnp.cos(ang)
        sin_ref[...] = jnp.sin(ang) * sign

    return pl.pallas_call(
        body, name="rope_tables", grid=(t // TM,),
        in_specs=[_rows(TM, 1), _const((1, 128))], out_specs=[_rows(TM, 128), _rows(TM, 128)],
        out_shape=[_sds((t, 128)), _sds((t, 128))], compiler_params=_params())(pos, inv_freq)


def _mixer_in_fwd(x, mix_norm, w_in, qn, kn, gvw, cos, sin):
    t = x.shape[0]

    def body(x_ref, g_ref, w_ref, qn_ref, kn_ref, gvw_ref, cos_ref, sin_ref, ones_ref,
             h_ref, qk_ref, gz_ref, q_ref, k_ref, v_ref, gu_ref, gvn_ref):
        x = x_ref[...]
        h = (x * _rs(x) * g_ref[...]).astype(BF16)
        h_ref[...] = h
        proj = _dot(h, w_ref[...])
        qk = proj[:, :AW + KW]
        qk_ref[...] = qk
        gz = proj[:, AW + 2 * KW:]
        gz_ref[...] = gz
        cos2, sin2 = cos_ref[...], sin_ref[...]
        q = qk[:, :AW]
        q = q * _rs64(q, ones_ref) * qn_ref[...]
        q_ref[...] = _rope(q, jnp.tile(cos2, (1, 4)), jnp.tile(sin2, (1, 4))).astype(BF16)
        k = qk[:, AW:]
        k = k * _rs64(k, ones_ref) * kn_ref[...]
        k_ref[...] = _rope(k, cos2, sin2).astype(BF16)
        v_ref[...] = proj[:, AW + KW:AW + 2 * KW].astype(BF16)
        gu_ref[...] = _gelu(gz[:, :GW])
        gv = _gelu(gz[:, GW:])
        gvn_ref[...] = (gv * _rs(gv) * gvw_ref[...]).astype(BF16)

    return pl.pallas_call(
        body, name="mixer_in_fwd", grid=(t // TM,),
        in_specs=[_rows(TM, D), _const((1, D)), _const((D, IN)), _const((1, AW)), _const((1, KW)),
                  _const((1, GW)), _rows(TM, 128), _rows(TM, 128), _const((AW, AW))],
        out_specs=[_rows(TM, D), _rows(TM, AW + KW), _rows(TM, 2 * GW), _rows(TM, AW), _rows(TM, KW),
                   _rows(TM, KW), _rows(TM, GW), _rows(TM, GW)],
        out_shape=[_sds((t, D), BF16), _sds((t, AW + KW)), _sds((t, 2 * GW)), _sds((t, AW), BF16),
                   _sds((t, KW), BF16), _sds((t, KW), BF16), _sds((t, GW)), _sds((t, GW), BF16)],
        compiler_params=_params())(x, mix_norm, w_in, qn, kn, gvw, cos, sin, _head_ones())


def _dup_half(kk, g):
    lane = _lane(kk.shape)
    other = pltpu.roll(kk, 64, 1)
    keep = (lane < 64) if g == 0 else (lane >= 64)
    return jnp.where(keep, kk, other).astype(BF16)


def _swa_mask(first_block):
    qi = lax.broadcasted_iota(jnp.int32, (4 * BLK, 2 * BLK), 0) & (BLK - 1)
    kj = lax.broadcasted_iota(jnp.int32, (4 * BLK, 2 * BLK), 1)
    diff = qi + BLK - kj
    band = (diff >= 0) & (diff < BLK)
    return band & (jnp.logical_not(first_block) | (kj >= BLK))


def _stack_heads(a2, b2):
    lo = _lane(a2.shape) < 64
    z = jnp.zeros_like(a2)
    return jnp.concatenate([jnp.where(lo, a2, z), jnp.where(lo, z, a2), jnp.where(lo, b2, z), jnp.where(lo, z, b2)], axis=0)


def _unstack_heads(o):
    lo = _lane((BLK, 128)) < 64
    return jnp.where(lo, o[0:BLK], o[BLK:2 * BLK]), jnp.where(lo, o[2 * BLK:3 * BLK], o[3 * BLK:4 * BLK])


def _sink_col(sink_ref, g):
    row = lax.broadcasted_iota(jnp.int32, (4 * BLK, 1), 0)
    s = [sink_ref[0, 4 * g + j] for j in range(4)]
    return jnp.where(row < BLK, s[0], jnp.where(row < 2 * BLK, s[1], jnp.where(row < 3 * BLK, s[2], s[3])))


def _swa_probs(qs, kd, mask, sink):
    s = _dot_nt(qs, kd) * (1.0 / math.sqrt(HD))
    s = jnp.where(mask, s, NEG)
    m = jnp.maximum(jnp.max(s, axis=-1, keepdims=True), sink)
    p = jnp.exp(s - m)
    ps = jnp.exp(sink - m)
    inv = 1.0 / (jnp.sum(p, axis=-1, keepdims=True) + ps)
    return p * inv, ps * inv


def _swa_fwd(q, k, v, sinks):
    t = q.shape[0]
    nb = t // BLK

    def body(sink_ref, q_ref, kc_ref, kp_ref, vc_ref, vp_ref, o_ref):
        i = pl.program_id(0)
        mask = _swa_mask(i == 0)
        kk = jnp.concatenate([kp_ref[...], kc_ref[...]], axis=0).astype(F32)
        vv = jnp.concatenate([vp_ref[...], vc_ref[...]], axis=0).astype(F32)
        for g in range(2):
            qs = _stack_heads(q_ref[:, 256 * g:256 * g + 128], q_ref[:, 256 * g + 128:256 * g + 256])
            pn, _ = _swa_probs(qs, _dup_half(kk, g), mask, _sink_col(sink_ref, g))
            oa, ob = _unstack_heads(_dot(pn.astype(BF16), _dup_half(vv, g)))
            o_ref[:, 256 * g:256 * g + 128] = oa
            o_ref[:, 256 * g + 128:256 * g + 256] = ob

    cur = lambda i: (i, 0)
    prev = lambda i: (jnp.maximum(i - 1, 0), 0)
    return pl.pallas_call(
        body, name="swa_fwd", grid=(nb,),
        in_specs=[pl.BlockSpec(memory_space=pltpu.SMEM), pl.BlockSpec((BLK, AW), cur),
                  pl.BlockSpec((BLK, KW), cur), pl.BlockSpec((BLK, KW), prev),
                  pl.BlockSpec((BLK, KW), cur), pl.BlockSpec((BLK, KW), prev)],
        out_specs=pl.BlockSpec((BLK, AW), cur), out_shape=_sds((t, AW)),
        compiler_params=_params())(sinks, q, k, k, v, v)


def _causal_bf16(w_ref, h, transposed):
    r = lax.broadcasted_iota(jnp.int32, (BLK, BLK), 0)
    c = lax.broadcasted_iota(jnp.int32, (BLK, BLK), 1)
    keep = (r <= c) if transposed else (c <= r)
    return jnp.where(keep, w_ref[h], 0.0).astype(BF16)


def _gmlp_mix(w_ref, xin, transposed):
    lo = _lane((BLK, 128)) < 64
    wm = [_causal_bf16(w_ref, h, transposed) for h in range(8)]
    rows = []
    for c in range(xin.shape[0] // BLK):
        cols = []
        for j in range(4):
            xs = xin[c * BLK:(c + 1) * BLK, 128 * j:128 * (j + 1)]
            cols.append(jnp.where(lo, _dot(wm[2 * j], xs), _dot(wm[2 * j + 1], xs)))
        rows.append(jnp.concatenate(cols, axis=1))
    return jnp.concatenate(rows, axis=0)


def _gmlp_fwd(gvn, gu, ws, bfull):
    t = gvn.shape[0]

    def body(x_ref, gu_ref, w_ref, b_ref, o_ref):
        mixed = _gmlp_mix(w_ref, x_ref[...], False) + jnp.tile(b_ref[...], (TM // BLK, 1))
        o_ref[...] = gu_ref[...] * mixed

    return pl.pallas_call(
        body, name="gmlp_fwd", grid=(t // TM,),
        in_specs=[_rows(TM, GW), _rows(TM, GW), _const((8, BLK, BLK)), _const((BLK, GW))],
        out_specs=_rows(TM, GW), out_shape=_sds((t, GW)), compiler_params=_params())(gvn, gu, ws, bfull)


def _mixer_out_fwd(attn, gm, x, w_out, aon, gon, xan):
    t = x.shape[0]

    def body(a_ref, g_ref, x_ref, w_ref, aon_ref, gon_ref, xan_ref, y_ref, x1_ref, h2_ref):
        a, g = a_ref[...], g_ref[...]
        y = jnp.concatenate([a * _rs(a) * aon_ref[...], g * _rs(g) * gon_ref[...]], axis=1).astype(BF16)
        y_ref[...] = y
        x1 = x_ref[...] + _dot(y, w_ref[...])
        x1_ref[...] = x1
        h2_ref[...] = (x1 * _rs(x1) * xan_ref[...]).astype(BF16)

    return pl.pallas_call(
        body, name="mixer_out_fwd", grid=(t // TM,),
        in_specs=[_rows(TM, AW), _rows(TM, GW), _rows(TM, D), _const((D, D)), _const((1, AW)), _const((1, GW)),
                  _const((1, D))],
        out_specs=[_rows(TM, D), _rows(TM, D), _rows(TM, D)],
        out_shape=[_sds((t, D), BF16), _sds((t, D)), _sds((t, D), BF16)],
        compiler_params=_params())(attn, gm, x, w_out, aon, gon, xan)


def _mem_kv_fwd(mem, mem_norm, wkv, kn4):
    def body(m_ref, g_ref, w_ref, kn_ref, mh_ref, kpre_ref, k_ref, v_ref):
        m = m_ref[...]
        mh = (m * _rs(m) * g_ref[...]).astype(BF16)
        mh_ref[...] = mh
        for h in range(XH):
            sl = slice(XD * h, XD * (h + 1))
            kh = _dot(mh, w_ref[h])
            kpre_ref[:, sl] = kh
            k_ref[:, sl] = (kh * _rs(kh) * kn_ref[:, sl]).astype(BF16)
            v_ref[:, sl] = _dot(mh, w_ref[XH + h]).astype(BF16)

    return pl.pallas_call(
        body, name="mem_kv_fwd",
        out_shape=[_sds((MEM, D), BF16), _sds((MEM, D)), _sds((MEM, D), BF16), _sds((MEM, D), BF16)],
        compiler_params=pltpu.CompilerParams(vmem_limit_bytes=VMEM_LIMIT))(mem, mem_norm, wkv, kn4)


def _xattn_probs(qpre_h, qn_h, k_h):
    rq = _rs(qpre_h)
    q2 = (qpre_h * rq * qn_h).astype(BF16)
    s = _dot_nt(q2, k_h) * (1.0 / math.sqrt(XD))
    p = jnp.exp(s - jnp.max(s, axis=-1, keepdims=True))
    return p * (1.0 / jnp.sum(p, axis=-1, keepdims=True)), q2, rq


def _xattn_fwd(h2, x1, wq, qn4, k2, v2, wo, ffn_norm):
    t = x1.shape[0]

    def body(h_ref, x_ref, wq_ref, qn_ref, k_ref, v_ref, wo_ref, fn_ref, qpre_ref, o_ref, x2_ref, h3_ref):
        qpre = _dot(h_ref[...], wq_ref[...])
        qpre_ref[...] = qpre
        outs = []
        for h in range(XH):
            sl = slice(XD * h, XD * (h + 1))
            pn, _, _ = _xattn_probs(qpre[:, sl], qn_ref[:, sl], k_ref[:, sl])
            outs.append(_dot(pn.astype(BF16), v_ref[:, sl]))
        o = jnp.concatenate(outs, axis=1).astype(BF16)
        o_ref[...] = o
        x2 = x_ref[...] + _dot(o, wo_ref[...])
        x2_ref[...] = x2
        h3_ref[...] = (x2 * _rs(x2) * fn_ref[...]).astype(BF16)

    return pl.pallas_call(
        body, name="xattn_fwd", grid=(t // TM,),
        in_specs=[_rows(TM, D), _rows(TM, D), _const((D, D)), _const((1, D)), _const((MEM, D)), _const((MEM, D)),
                  _const((D, D)), _const((1, D))],
        out_specs=[_rows(TM, D)] * 4,
        out_shape=[_sds((t, D)), _sds((t, D), BF16), _sds((t, D)), _sds((t, D), BF16)],
        compiler_params=_params())(h2, x1, wq, qn4, k2, v2, wo, ffn_norm)


SW = 704
NG = FF // SW
FM = 256
HALO = 16


def _resident(shape):
    nd = len(shape)
    return pl.BlockSpec(shape, lambda *_: (0,) * nd, pipeline_mode=pl.Buffered(1))


def _halo_before(i):
    return jnp.maximum(i * (FM // HALO) - 1, 0)


def _conv(e, w):
    return w[2:3, :] * e + pltpu.roll(w[1:2, :] * e + pltpu.roll(w[0:1, :] * e, 1, 0), 1, 0)


def _conv_t(dc, w):
    n = dc.shape[0]
    return w[2:3, :] * dc + pltpu.roll(w[1:2, :] * dc + pltpu.roll(w[0:1, :] * dc, n - 1, 0), n - 1, 0)


def _ffn_fwd(h3, x2, target, up, conv, conv_b, down):
    t = x2.shape[0]

    def body(h_ref, hp_ref, x_ref, t_ref, up_ref, w_ref, b_ref, dn_ref, a_ref, u_ref, gs_ref, dy_ref, loss_ref, acc_ref):
        i = pl.program_id(0)

        @pl.when(i == 0)
        def _():
            acc_ref[...] = jnp.zeros_like(acc_ref)

        before = jnp.where(i > 0, hp_ref[...], jnp.zeros_like(hp_ref))
        he = jnp.concatenate([before, h_ref[...]], axis=0)
        err = x_ref[...] - t_ref[...]
        for d in range(NG):
            c = []
            for s in range(2):
                a = _dot(he, up_ref[s * NG + d]).astype(BF16)
                a_ref[s * NG + d] = a[HALO:]
                c.append(_conv(a.astype(F32), w_ref[s, d])[HALO:] + b_ref[s, d])
            gl, gg = _gelu_and_grad(c[0])
            gs_ref[d] = gl.astype(BF16)
            gs_ref[NG + d] = (gg * c[1]).astype(BF16)
            u = (gl * c[1]).astype(BF16)
            u_ref[d] = u
            err = err + _dot(u, dn_ref[d])
        dy_ref[...] = err * (1.0 / D)
        acc_ref[...] += jnp.sum(err * err, axis=0, keepdims=True)

        @pl.when(i == pl.num_programs(0) - 1)
        def _():
            loss_ref[...] = jnp.full((8, 128), 0.5 / D, F32) * jnp.sum(acc_ref[...])

    return pl.pallas_call(
        body, name="ffn_fwd", grid=(t // FM,),
        in_specs=[_rows(FM, D), pl.BlockSpec((HALO, D), lambda i: (_halo_before(i), 0)), _rows(FM, D), _rows(FM, D),
                  _resident((NDEV, D, SW)), _resident((2, NG, 3, SW)), _resident((2, NG, 1, SW)), _resident((NG, SW, D))],
        out_specs=[pl.BlockSpec((NDEV, FM, SW), lambda i: (0, i, 0)), pl.BlockSpec((NG, FM, SW), lambda i: (0, i, 0)),
                   pl.BlockSpec((NDEV, FM, SW), lambda i: (0, i, 0)), _rows(FM, D), _const((8, 128))],
        out_shape=[_sds((NDEV, t, SW), BF16), _sds((NG, t, SW), BF16), _sds((NDEV, t, SW), BF16), _sds((t, D)),
                   _sds((8, 128))],
        scratch_shapes=[pltpu.VMEM((1, D), F32)], compiler_params=_params())(h3, h3, x2, target, up, conv, conv_b, down)


def _gelu_and_grad(z):
    z2 = z * z
    t = jnp.tanh(GELU_C0 * (z + GELU_C1 * z * z2))
    phi = 0.5 * (1.0 + t)
    return z * phi, phi + 0.5 * z * (1.0 - t * t) * (GELU_C0 * (1.0 + 3.0 * GELU_C1 * z2))


def _ffn_bwd(dy, a, gs, x2, up, conv, down, ffn_norm):
    t = x2.shape[0]
    nt = t // FM
    n = FM + HALO

    def body(dy_ref, dyn_ref, a_ref, gs_ref, gsn_ref, x_ref, up_ref, w_ref, dn_ref, g_ref,
             dx_ref, da_ref, s_ref, dfn_ref):
        i = pl.program_id(0)

        @pl.when(i == 0)
        def _():
            s_ref[...] = jnp.zeros_like(s_ref)
            dfn_ref[...] = jnp.zeros_like(dfn_ref)

        last = i == nt - 1
        dy = dy_ref[...]
        dye = jnp.concatenate([dy, jnp.where(last, 0.0, dyn_ref[...])], axis=0).astype(BF16)
        dh = jnp.zeros((FM, D), F32)
        row = lax.broadcasted_iota(jnp.int32, (8, SW), 0)
        for d in range(NG):
            du = _dot_nt(dye, dn_ref[d])
            for s in range(2):
                j = s * NG + d
                k = NG + d if s == 0 else d
                dc = du * jnp.concatenate([gs_ref[k], gsn_ref[k]], axis=0).astype(F32)
                w = w_ref[s, d]
                tile = a_ref[j].astype(F32)
                d1 = pltpu.roll(dc, n - 1, 0)
                d2 = pltpu.roll(d1, n - 1, 0)
                da = (w[2:3, :] * dc + w[1:2, :] * d1 + w[0:1, :] * d2)[0:FM].astype(BF16)
                da_ref[j] = da
                dh = dh + _dot_nt(da, up_ref[j])
                sums = [jnp.sum(v[0:FM] * tile, axis=0, keepdims=True) for v in (d2, d1, dc)]
                sums.append(jnp.sum(dc[0:FM], axis=0, keepdims=True))
                upd = jnp.zeros((8, SW), F32)
                for r, v in enumerate(sums):
                    upd = jnp.where(row == r, jnp.broadcast_to(v, (8, SW)), upd)
                s_ref[s, d] += upd
        x = x_ref[...]
        dx, dg = _rms_bwd(dh, x, _rs(x), g_ref[...])
        dx_ref[...] = dy + dx
        dfn_ref[...] += _colsum8(dg)

    last_halo = t // HALO - 1
    after = lambda i: jnp.minimum((i + 1) * (FM // HALO), last_halo)
    return pl.pallas_call(
        body, name="ffn_bwd", grid=(nt,),
        in_specs=[_rows(FM, D), pl.BlockSpec((HALO, D), lambda i: (after(i), 0)),
                  pl.BlockSpec((NDEV, FM, SW), lambda i: (0, i, 0)),
                  pl.BlockSpec((NDEV, FM, SW), lambda i: (0, i, 0)),
                  pl.BlockSpec((NDEV, HALO, SW), lambda i: (0, after(i), 0)),
                  _rows(FM, D), _resident((NDEV, D, SW)), _resident((2, NG, 3, SW)), _resident((NG, SW, D)), _const((1, D))],
        out_specs=[_rows(FM, D), pl.BlockSpec((NDEV, FM, SW), lambda i: (0, i, 0)), _const((2, NG, 8, SW)), _const((8, D))],
        out_shape=[_sds((t, D)), _sds((NDEV, t, SW), BF16), _sds((2, NG, 8, SW)), _sds((8, D))],
        compiler_params=_params())(dy, dy, a, gs, gs, x2, up, conv, down, ffn_norm)


BT = 256


def _xattn_bwd(dx2, x1, qpre, k2, v2, wq, wo, qn4, xan):
    t = x1.shape[0]

    def body(dx2_ref, x1_ref, qpre_ref, k_ref, v_ref, wq_ref, wo_ref, qn_ref, xan_ref,
             dx1_ref, dqpre_ref, dk_ref, dv_ref, dqn_ref, dxan_ref):
        @pl.when(pl.program_id(0) == 0)
        def _():
            for r in (dk_ref, dv_ref, dqn_ref, dxan_ref):
                r[...] = jnp.zeros_like(r)

        dx2 = dx2_ref[...]
        do = _dot_nt(dx2.astype(BF16), wo_ref[...])
        dqs = []
        for h in range(XH):
            sl = slice(XD * h, XD * (h + 1))
            qpre_h = qpre_ref[:, sl]
            pn, q2, rq = _xattn_probs(qpre_h, qn_ref[:, sl], k_ref[:, sl])
            do_h = do[:, sl].astype(BF16)
            dp = _dot_nt(do_h, v_ref[:, sl])
            ds = (pn * (dp - jnp.sum(pn * dp, axis=-1, keepdims=True)) * (1.0 / math.sqrt(XD))).astype(BF16)
            dq2 = _dot(ds, k_ref[:, sl])
            dk_ref[:, sl] += _dot_tn(ds, q2)
            dv_ref[:, sl] += _dot_tn(pn.astype(BF16), do_h)
            dqh, dg = _rms_bwd(dq2, qpre_h, rq, qn_ref[:, sl])
            dqn_ref[...] += _colsum8(dg)
            dqs.append(dqh)
        dqpre = jnp.concatenate(dqs, axis=1).astype(BF16)
        dqpre_ref[...] = dqpre
        dh2 = _dot_nt(dqpre, wq_ref[...])
        x1 = x1_ref[...]
        dx, dg = _rms_bwd(dh2, x1, _rs(x1), xan_ref[...])
        dx1_ref[...] = dx2 + dx
        dxan_ref[...] += _colsum8(dg)

    return pl.pallas_call(
        body, name="xattn_bwd", grid=(t // BT,),
        in_specs=[_rows(BT, D), _rows(BT, D), _rows(BT, D), _const((MEM, D)), _const((MEM, D)), _const((D, D)),
                  _const((D, D)), _const((1, D)), _const((1, D))],
        out_specs=[_rows(BT, D), _rows(BT, D), _const((MEM, D)), _const((MEM, D)), _const((8, XD)), _const((8, D))],
        out_shape=[_sds((t, D)), _sds((t, D), BF16), _sds((MEM, D)), _sds((MEM, D)), _sds((8, XD)), _sds((8, D))],
        compiler_params=_params())(dx2, x1, qpre, k2, v2, wq, wo, qn4, xan)


def _mem_kv_bwd(mem, mh, kpre, dk2, dv2, wkv, kn4, mem_norm):
    def body(m_ref, mh_ref, kpre_ref, dk_ref, dv_ref, w_ref, kn_ref, g_ref, dw_ref, dkn_ref, dmn_ref):
        dkn = jnp.zeros((8, XD), F32)
        dm = jnp.zeros((MEM, D), F32)
        mh = mh_ref[...]
        for h in range(XH):
            sl = slice(XD * h, XD * (h + 1))
            kh = kpre_ref[:, sl]
            dkh, dg = _rms_bwd(dk_ref[:, sl], kh, _rs(kh), kn_ref[:, sl])
            dkn = dkn + _colsum8(dg)
            dkh = dkh.astype(BF16)
            dvh = dv_ref[:, sl].astype(BF16)
            dw_ref[h] = _dot_tn(mh, dkh).astype(BF16)
            dw_ref[XH + h] = _dot_tn(mh, dvh).astype(BF16)
            dm = dm + _dot_nt(dkh, w_ref[h]) + _dot_nt(dvh, w_ref[XH + h])
        dkn_ref[...] = dkn
        m = m_ref[...]
        _, dg = _rms_bwd(dm, m, _rs(m), g_ref[...])
        dmn_ref[...] = _colsum8(dg)

    return pl.pallas_call(
        body, name="mem_kv_bwd", out_shape=[_sds((2 * XH, D, XD), BF16), _sds((8, XD)), _sds((8, D))],
        compiler_params=pltpu.CompilerParams(vmem_limit_bytes=VMEM_LIMIT))(mem, mh, kpre, dk2, dv2, wkv, kn4, mem_norm)


def _mixer_out_bwd(dx1, attn, gm, w_out, aon, gon):
    t = dx1.shape[0]

    def body(dx_ref, a_ref, g_ref, w_ref, aon_ref, gon_ref, da_ref, dg_ref, dan_ref, dgn_ref):
        @pl.when(pl.program_id(0) == 0)
        def _():
            dan_ref[...] = jnp.zeros_like(dan_ref)
            dgn_ref[...] = jnp.zeros_like(dgn_ref)

        dy = _dot_nt(dx_ref[...].astype(BF16), w_ref[...])
        a, g = a_ref[...], g_ref[...]
        da, dna = _rms_bwd(dy[:, :AW], a, _rs(a), aon_ref[...])
        dg, dng = _rms_bwd(dy[:, AW:], g, _rs(g), gon_ref[...])
        da_ref[...] = da
        dg_ref[...] = dg
        dan_ref[...] += _colsum8(dna)
        dgn_ref[...] += _colsum8(dng)

    return pl.pallas_call(
        body, name="mixer_out_bwd", grid=(t // TM,),
        in_specs=[_rows(TM, D), _rows(TM, AW), _rows(TM, GW), _const((D, D)), _const((1, AW)), _const((1, GW))],
        out_specs=[_rows(TM, AW), _rows(TM, GW), _const((8, AW)), _const((8, GW))],
        out_shape=[_sds((t, AW)), _sds((t, GW)), _sds((8, AW)), _sds((8, GW))],
        compiler_params=_params())(dx1, attn, gm, w_out, aon, gon)


def _gmlp_bwd(dgm, gu, gvn, gz, ws, wst, bfull, gvw):
    t = dgm.shape[0]
    nc = TM // BLK

    def body(dgm_ref, gu_ref, x_ref, gz_ref, w_ref, wt_ref, b_ref, gvw_ref, dgz_ref, dw_ref, db_ref, dgvw_ref,
             dbacc_ref):
        @pl.when(pl.program_id(0) == 0)
        def _():
            for r in (dw_ref, dbacc_ref, dgvw_ref):
                r[...] = jnp.zeros_like(r)

        xin = x_ref[...]
        dgm = dgm_ref[...]
        mixed = _gmlp_mix(w_ref, xin, False) + jnp.tile(b_ref[...], (nc, 1))
        dgu = dgm * mixed
        dmixed = dgm * gu_ref[...]
        lo = _lane((BLK, 128)) < 64
        dbias = jnp.zeros((BLK, GW), F32)
        for c in range(nc):
            dmc = dmixed[c * BLK:(c + 1) * BLK]
            dbias = dbias + dmc
            for j in range(4):
                dm2 = dmc[:, 128 * j:128 * (j + 1)]
                xs = xin[c * BLK:(c + 1) * BLK, 128 * j:128 * (j + 1)]
                z = jnp.zeros_like(dm2)
                dw_ref[2 * j] += _dot_nt(jnp.where(lo, dm2, z).astype(BF16), xs)
                dw_ref[2 * j + 1] += _dot_nt(jnp.where(lo, z, dm2).astype(BF16), xs)
        dbacc_ref[...] += dbias
        dgvn = _gmlp_mix(wt_ref, dmixed.astype(BF16), True)
        gz_u, gz_v = gz_ref[:, :GW], gz_ref[:, GW:]
        gv = _gelu(gz_v)
        dgv, dg = _rms_bwd(dgvn, gv, _rs(gv), gvw_ref[...])
        dgvw_ref[...] += _colsum8(dg)
        dgz_ref[:, :GW] = (dgu * _gelu_grad(gz_u)).astype(BF16)
        dgz_ref[:, GW:] = (dgv * _gelu_grad(gz_v)).astype(BF16)

        @pl.when(pl.program_id(0) == pl.num_programs(0) - 1)
        def _():
            s = dbacc_ref[...]
            sel = (lax.broadcasted_iota(jnp.int32, (8, GW), 1) // HD
                   == lax.broadcasted_iota(jnp.int32, (8, GW), 0)).astype(BF16)
            hi = s.astype(BF16)
            r1 = s - hi.astype(F32)
            mid = r1.astype(BF16)
            lo = (r1 - mid.astype(F32)).astype(BF16)
            db_ref[...] = _dot_nt(sel, hi) + _dot_nt(sel, mid) + _dot_nt(sel, lo)
            r = lax.broadcasted_iota(jnp.int32, (BLK, BLK), 0)
            c = lax.broadcasted_iota(jnp.int32, (BLK, BLK), 1)
            for h in range(8):
                dw_ref[h] = jnp.where(c <= r, dw_ref[h], 0.0)

    return pl.pallas_call(
        body, name="gmlp_bwd", grid=(t // TM,),
        in_specs=[_rows(TM, GW), _rows(TM, GW), _rows(TM, GW), _rows(TM, 2 * GW), _const((8, BLK, BLK)),
                  _const((8, BLK, BLK)), _const((BLK, GW)), _const((1, GW))],
        out_specs=[_rows(TM, 2 * GW), _const((8, BLK, BLK)), _const((8, BLK)), _const((8, GW))],
        out_shape=[_sds((t, 2 * GW), BF16), _sds((8, BLK, BLK)), _sds((8, BLK)), _sds((8, GW))],
        scratch_shapes=[pltpu.VMEM((BLK, GW), F32)],
        compiler_params=_params())(dgm, gu, gvn, gz, ws, wst, bfull, gvw)


def _fold_half(v):
    return v + pltpu.roll(v, 64, 1)


def _swa_bwd(q, k, v, dattn, sinks):
    t = q.shape[0]
    nb = t // BLK

    def body(sink_ref, q_ref, kc_ref, kp_ref, vc_ref, vp_ref, do_ref, dq_ref, dk_ref, dv_ref, ds_ref,
             ck_ref, cv_ref, sacc_ref):
        i = pl.program_id(0)

        @pl.when(i == 0)
        def _():
            ck_ref[...] = jnp.zeros_like(ck_ref)
            cv_ref[...] = jnp.zeros_like(cv_ref)
            sacc_ref[...] = jnp.zeros_like(sacc_ref)

        @pl.when(i < nb)
        def _():
            mask = _swa_mask(i == 0)
            kk = jnp.concatenate([kp_ref[...], kc_ref[...]], axis=0).astype(F32)
            vv = jnp.concatenate([vp_ref[...], vc_ref[...]], axis=0).astype(F32)
            lo256 = _lane((2 * BLK, 128)) < 64
            dkk = jnp.zeros((2 * BLK, 128), F32)
            dvv = jnp.zeros((2 * BLK, 128), F32)
            for g in range(2):
                qs = _stack_heads(q_ref[:, 256 * g:256 * g + 128], q_ref[:, 256 * g + 128:256 * g + 256])
                dos = _stack_heads(do_ref[:, 256 * g:256 * g + 128],
                                   do_ref[:, 256 * g + 128:256 * g + 256]).astype(BF16)
                kd = _dup_half(kk, g)
                pn, psn = _swa_probs(qs, kd, mask, _sink_col(sink_ref, g))
                dp = _dot_nt(dos, _dup_half(vv, g))
                dd = jnp.sum(pn * dp, axis=-1, keepdims=True)
                ds = (pn * (dp - dd) * (1.0 / math.sqrt(HD))).astype(BF16)
                sacc_ref[g] += jnp.broadcast_to(-psn * dd, (4 * BLK, 128))
                dqa, dqb = _unstack_heads(_dot(ds, kd))
                dq_ref[:, 256 * g:256 * g + 128] = dqa
                dq_ref[:, 256 * g + 128:256 * g + 256] = dqb
                dkg = _fold_half(_dot_tn(ds, qs))
                dvg = _fold_half(_dot_tn(pn.astype(BF16), dos))
                keep = lo256 if g == 0 else jnp.logical_not(lo256)
                dkk = jnp.where(keep, dkg, dkk)
                dvv = jnp.where(keep, dvg, dvv)
            dk_ref[...] = ck_ref[...] + dkk[0:BLK]
            dv_ref[...] = cv_ref[...] + dvv[0:BLK]
            ck_ref[...] = dkk[BLK:]
            cv_ref[...] = dvv[BLK:]

        @pl.when(i == nb)
        def _():
            dk_ref[...] = ck_ref[...]
            dv_ref[...] = cv_ref[...]
            lane = _lane((8, 128))
            acc = jnp.zeros((8, 128), F32)
            for g in range(2):
                for j in range(4):
                    val = jnp.sum(sacc_ref[g, j * BLK:(j + 1) * BLK, :], axis=0, keepdims=True)
                    acc = jnp.where(lane == 4 * g + j, jnp.broadcast_to(val, (8, 128)), acc)
            ds_ref[...] = acc

    cur = lambda i: (jnp.minimum(i, nb - 1), 0)
    prev = lambda i: (jnp.clip(i - 1, 0, nb - 1), 0)
    return pl.pallas_call(
        body, name="swa_bwd", grid=(nb + 1,),
        in_specs=[pl.BlockSpec(memory_space=pltpu.SMEM), pl.BlockSpec((BLK, AW), cur),
                  pl.BlockSpec((BLK, KW), cur), pl.BlockSpec((BLK, KW), prev),
                  pl.BlockSpec((BLK, KW), cur), pl.BlockSpec((BLK, KW), prev), pl.BlockSpec((BLK, AW), cur)],
        out_specs=[pl.BlockSpec((BLK, AW), cur), pl.BlockSpec((BLK, KW), prev), pl.BlockSpec((BLK, KW), prev),
                   _const((8, 128))],
        out_shape=[_sds((t, AW)), _sds((t, KW)), _sds((t, KW)), _sds((8, 128))],
        scratch_shapes=[pltpu.VMEM((BLK, KW), F32), pltpu.VMEM((BLK, KW), F32), pltpu.VMEM((2, 4 * BLK, 128), F32)],
        compiler_params=_params())(sinks, q, k, k, v, v, dattn)


def _mixer_in_bwd(dq, dk, dv, dgz, qk, cos, sin, x, dx1, w_in, mix_norm, qn, kn):
    t = x.shape[0]

    def body(dq_ref, dk_ref, dv_ref, dgz_ref, qk_ref, cos_ref, sin_ref, x_ref, dx1_ref, w_ref, g_ref, qn_ref, kn_ref,
             ones_ref, gx_ref, dproj_ref, dmn_ref, dqn_ref, dkn_ref, qacc_ref, kacc_ref):
        i = pl.program_id(0)

        @pl.when(i == 0)
        def _():
            dmn_ref[...] = jnp.zeros_like(dmn_ref)
            qacc_ref[...] = jnp.zeros_like(qacc_ref)
            kacc_ref[...] = jnp.zeros_like(kacc_ref)

        cos2, sin2 = cos_ref[...], sin_ref[...]
        qpre, kpre = qk_ref[:, :AW], qk_ref[:, AW:]
        dqh = _rope_bwd(dq_ref[...], jnp.tile(cos2, (1, 4)), jnp.tile(sin2, (1, 4)))
        dqpre, dgq = _rms64_bwd(dqh, qpre, _rs64(qpre, ones_ref), qn_ref[...], ones_ref)
        dkh = _rope_bwd(dk_ref[...], cos2, sin2)
        dkpre, dgk = _rms64_bwd(dkh, kpre, _rs64(kpre, ones_ref), kn_ref[...], ones_ref)
        qacc_ref[...] += jnp.sum(dgq, axis=0, keepdims=True)
        kacc_ref[...] += jnp.sum(dgk, axis=0, keepdims=True)
        dproj = jnp.concatenate([dqpre.astype(BF16), dkpre.astype(BF16), dv_ref[...].astype(BF16), dgz_ref[...]], axis=1)
        dproj_ref[...] = dproj
        dh = _dot_nt(dproj, w_ref[...])
        xv = x_ref[...]
        dx, dg = _rms_bwd(dh, xv, _rs(xv), g_ref[...])
        gx_ref[...] = dx1_ref[...] + dx
        dmn_ref[...] += _colsum8(dg)

        @pl.when(i == pl.num_programs(0) - 1)
        def _():
            qa = qacc_ref[...]
            q4 = qa[:, 0:128] + qa[:, 128:256] + qa[:, 256:384] + qa[:, 384:512]
            dqn_ref[...] = jnp.broadcast_to(_fold_half(q4), (8, 128))
            dkn_ref[...] = jnp.broadcast_to(_fold_half(kacc_ref[...]), (8, 128))

    return pl.pallas_call(
        body, name="mixer_in_bwd", grid=(t // TM,),
        in_specs=[_rows(TM, AW), _rows(TM, KW), _rows(TM, KW), _rows(TM, 2 * GW), _rows(TM, AW + KW), _rows(TM, 128),
                  _rows(TM, 128), _rows(TM, D), _rows(TM, D), _const((D, IN)), _const((1, D)), _const((1, AW)),
                  _const((1, KW)), _const((AW, AW))],
        out_specs=[_rows(TM, D), _rows(TM, IN), _const((8, D)), _const((8, 128)), _const((8, 128))],
        out_shape=[_sds((t, D)), _sds((t, IN), BF16), _sds((8, D)), _sds((8, 128)), _sds((8, 128))],
        scratch_shapes=[pltpu.VMEM((1, AW), F32), pltpu.VMEM((1, KW), F32)],
        compiler_params=_params())(dq, dk, dv, dgz, qk, cos, sin, x, dx1, w_in, mix_norm, qn, kn, _head_ones())


def _local_step(x, mem, pos, target, p, fetch, ship):
    t = x.shape[0]
    p = dict(p)
    p.update(fetch(0, None))
    inv_freq = 1.0 / (ROPE_THETA ** (jnp.arange(HD // 2, dtype=F32) * (2.0 / HD)))
    cos, sin = _rope_tables(pos, jnp.tile(inv_freq, 4).reshape(1, 128))
    qn = jnp.tile(p["q_norm"], (1, AW // HD))
    kn = jnp.tile(p["k_norm"], (1, KW // HD))
    qn4 = jnp.tile(p["xa_q_norm"], (1, XH))
    kn4 = jnp.tile(p["xa_k_norm"], (1, XH))
    ws = p["gmlp_ws"]
    wst = jnp.swapaxes(ws, 1, 2)
    bfull = jnp.repeat(p["gmlp_bs"].T, HD, axis=1)
    conv_b = p["ffn_conv_b"]

    h1, qk, gz, q, k, v, gu, gvn = _mixer_in_fwd(x, p["mix_norm"], p["w_in"], qn, kn, p["gmlp_v_norm"], cos, sin)
    attn = _swa_fwd(q, k, v, p["attn_sinks"])
    gm = _gmlp_fwd(gvn, gu, ws, bfull)
    ycat, x1, h2 = _mixer_out_fwd(attn, gm, x, p["w_out"], p["attn_out_norm"], p["gmlp_out_norm"], p["xa_norm"])
    p.update(fetch(1, h2))
    mh, kpre, k2, v2 = _mem_kv_fwd(mem, p["mem_norm"], p["xa_wkv"], kn4)
    qpre, o, x2, h3 = _xattn_fwd(h2, x1, p["xa_wq"], qn4, k2, v2, p["xa_wo"], p["ffn_norm"])
    p.update(fetch(2, h3))
    conv = p["ffn_conv"]
    a, u, gs, dy, loss8 = _ffn_fwd(h3, x2, target, p["ffn_up"], conv, conv_b, p["ffn_down"])

    raw = {}
    d_down = _mm_tn(u, dy, "ffn_down_bwd_w")
    dx2, da, raw["conv_sums"], raw["ffn_norm"] = _ffn_bwd(dy, a, gs, x2, p["ffn_up"], conv, p["ffn_down"], p["ffn_norm"])
    d_up = _mm_tn(h3, da, "ffn_up_bwd_w")
    token = ship(0, {"ffn_down": d_down, "ffn_up": d_up, "ffn_conv": raw["conv_sums"][:, :, 0:3]})
    dx1, dqpre, dk2, dv2, raw["xa_q_norm"], raw["xa_norm"] = _xattn_bwd(
        dx2, x1, qpre, k2, v2, p["xa_wq"], p["xa_wo"], qn4 + jnp.tile(token[0:1], (1, D // 128)), p["xa_norm"])
    d_wo = _mm_tn(o, dx2, "xa_wo_bwd_w")
    d_wq = _mm_tn(h2, dqpre, "xa_wq_bwd_w")
    d_wkv, raw["xa_k_norm"], raw["mem_norm"] = _mem_kv_bwd(mem, mh, kpre, dk2, dv2, p["xa_wkv"], kn4, p["mem_norm"])
    d_w_out = _mm_tn(ycat, dx1, "w_out_bwd_w")
    token = ship(1, {"xa_wo": d_wo, "xa_wq": d_wq, "xa_wkv": d_wkv, "w_out": d_w_out})
    dattn, dgm, raw["attn_out_norm"], raw["gmlp_out_norm"] = _mixer_out_bwd(
        dx1, attn, gm, p["w_out"], p["attn_out_norm"] + jnp.tile(token[0:1], (1, AW // 128)), p["gmlp_out_norm"])
    dgz, raw["gmlp_ws"], raw["gmlp_bs"], raw["gmlp_v_norm"] = _gmlp_bwd(dgm, gu, gvn, gz, ws, wst, bfull, p["gmlp_v_norm"])
    token = ship(2, {}, [raw["gmlp_ws"]])
    dq, dk, dv, raw["attn_sinks"] = _swa_bwd(q, k, v, dattn, p["attn_sinks"] + token[0:1, 0:8])
    grad_x, dproj, raw["mix_norm"], raw["q_norm"], raw["k_norm"] = _mixer_in_bwd(
        dq, dk, dv, dgz, qk, cos, sin, x, dx1, p["w_in"], p["mix_norm"], qn, kn)
    d_w_in = _mm_tn(h1, dproj, "w_in_bwd_w")
    return loss8[0, 0], grad_x, {"w_in": d_w_in}, raw


def _cast_shards(shards):
    def body(*refs):
        n = len(refs) // 2
        for i_ref, o_ref in zip(refs[:n], refs[n:]):
            o_ref[...] = i_ref[0].astype(BF16)

    return pl.pallas_call(body, name="cast_shards", out_shape=[_sds(s.shape[1:], BF16) for s in shards],
                          compiler_params=pltpu.CompilerParams(vmem_limit_bytes=VMEM_LIMIT))(*shards)


HBM_SPEC = pl.BlockSpec(memory_space=pltpu.HBM)
SEM_SPEC = pl.BlockSpec(memory_space=pltpu.SEMAPHORE)


def _remote_copies(src_refs, land_refs, send_refs, recv_refs, nd):
    x, y, cc = lax.axis_index("x"), lax.axis_index("y"), lax.axis_index("c")
    me = 4 * x + 2 * y + cc
    copies = []
    for a, (src_ref, land_ref) in enumerate(zip(src_refs, land_refs)):
        for k in range(1, NDEV):
            px = 1 - x if k & 4 else x
            py = 1 - y if k & 2 else y
            pc = 1 - cc if k & 1 else cc
            copies.append(pltpu.make_async_remote_copy(
                src_ref=src_ref.at[4 * px + 2 * py + pc] if a < nd else src_ref, dst_ref=land_ref.at[me],
                send_sem=send_refs[a].at[k - 1], recv_sem=recv_refs[a].at[k - 1],
                device_id=(px, py, pc), device_id_type=pl.DeviceIdType.MESH))
    return copies


def _own_slot(src, by_dest, me):
    block = lax.dynamic_index_in_dim(src, me, 0, keepdims=True) if by_dest else src[None]
    return lax.dynamic_update_index_in_dim(lax.empty((NDEV,) + block.shape[1:], src.dtype), block, me, 0)


def _exchange_start(by_dest, for_all, me, name):
    srcs = list(by_dest) + list(for_all)
    n, nd = len(srcs), len(by_dest)
    lands = [_own_slot(s, a < nd, me) for a, s in enumerate(srcs)]

    def body(*refs):
        for cp in _remote_copies(refs[:n], refs[n:2 * n], refs[2 * n:3 * n], refs[3 * n:4 * n], nd):
            cp.start()
        refs[-1][...] = jnp.zeros((8, 128), F32)

    sems = [pltpu.SemaphoreType.DMA((NDEV - 1,))] * (2 * n)
    thru = [pltpu.HBM(v.shape, v.dtype) for v in srcs + lands]
    res = pl.pallas_call(
        body, name=name, out_shape=sems + thru + [_sds((8, 128))],
        in_specs=[HBM_SPEC] * (2 * n), out_specs=[SEM_SPEC] * (2 * n) + [HBM_SPEC] * (2 * n) + [pl.BlockSpec(memory_space=pltpu.VMEM)],
        input_output_aliases={i: 2 * n + i for i in range(2 * n)},
        compiler_params=pltpu.CompilerParams(has_side_effects=pltpu.SideEffectType.DATAFLOW_SIDE_EFFECTING))(
            *[pltpu.with_memory_space_constraint(v, pltpu.HBM) for v in srcs + lands])
    return (res[:2 * n], res[2 * n:4 * n], nd), res[-1]


def _exchange_wait(state, after, name):
    sems, thru, nd = state
    n = len(thru) // 2

    def body(*refs):
        for cp in _remote_copies(refs[:n], refs[n:2 * n], refs[2 * n:3 * n], refs[3 * n:4 * n], nd):
            cp.wait_send()
            cp.wait_recv()

    res = pl.pallas_call(
        body, name=name, out_shape=[pltpu.HBM(v.shape, v.dtype) for v in thru],
        in_specs=[HBM_SPEC] * (2 * n) + [SEM_SPEC] * (2 * n) + [pl.BlockSpec(memory_space=pl.ANY)],
        out_specs=[HBM_SPEC] * (2 * n), input_output_aliases={i: i for i in range(2 * n)},
        compiler_params=pltpu.CompilerParams(has_side_effects=pltpu.SideEffectType.DATAFLOW_SIDE_EFFECTING))(
            *thru, *sems, after)
    return res[n:]


def _adam(parts, w, m, v, name):
    def body(p_ref, w_ref, m_ref, v_ref, g_ref, d_ref, nm_ref, nv_ref):
        g = _sum_parts(p_ref)
        g_ref[0] = g
        d_ref[0], nm_ref[0], nv_ref[0] = _adam_math(g, w_ref[0], m_ref[0], v_ref[0])

    return pl.pallas_call(
        body, name=name, out_shape=[_sds(w.shape)] * 4,
        compiler_params=pltpu.CompilerParams(vmem_limit_bytes=VMEM_LIMIT))(parts, w, m, v)


GATHER_GROUPS = (("w_in", "w_out"), ("xa_wkv", "xa_wq", "xa_wo"), ("ffn_up", "ffn_conv", "ffn_down"))
SCATTER_GROUPS = (("ffn_down", "ffn_up", "ffn_conv"), ("xa_wo", "xa_wq", "xa_wkv", "w_out"), (), ("w_in",))
BIG = tuple(n for grp in GATHER_GROUPS for n in grp)
VECS = (("mix_norm", D), ("q_norm", HD), ("k_norm", HD), ("attn_sinks", 8), ("gmlp_v_norm", GW), ("attn_out_norm", AW),
        ("gmlp_out_norm", GW), ("xa_norm", D), ("mem_norm", D), ("xa_q_norm", XD), ("xa_k_norm", XD), ("ffn_norm", D))
BS_ROW = 16
VEC_ROWS = 24
SMALL = tuple(n for n, _ in VECS) + ("gmlp_bs", "gmlp_ws", "ffn_conv_b")


def _pack_small(raw):
    names = [n for n, _ in VECS] + ["gmlp_bs", "conv_sums"]

    def body(*refs):
        ins = dict(zip(names, refs))
        vec_ref, cb_ref = refs[len(names):]
        vec_ref[...] = jnp.zeros_like(vec_ref)
        for r, (n, w) in enumerate(VECS):
            vec_ref[r:r + 1, 0:w] = ins[n][0:1, 0:w]
        vec_ref[BS_ROW:BS_ROW + 8, 0:BLK] = ins["gmlp_bs"][...]
        for s in range(2):
            for d in range(NG):
                cb_ref[s, d] = ins["conv_sums"][s, d, 3:4, :]

    return pl.pallas_call(body, name="pack_small", out_shape=[_sds((VEC_ROWS, D)), _sds((2, NG, 1, SW))])(
        *[raw[n] for n in names])


def _adam_math(g, w, m, v):
    nm = B1 * m + (1.0 - B1) * g
    nv = B2 * v + (1.0 - B2) * (g * g)
    m_hat = nm / (1.0 - B1 ** STEP)
    v_hat = nv / (1.0 - B2 ** STEP)
    return -LR * (m_hat / (jnp.sqrt(v_hat) + AEPS) + WD * w), nm, nv


def _sum_parts(p_ref):
    g = p_ref[0].astype(F32)
    for j in range(1, NDEV):
        g = g + p_ref[j].astype(F32)
    return g


def _adam_small(parts_vec, parts_ws, parts_cb, w, m, v):
    def body(*refs):
        pv_ref, pws_ref, pcb_ref = refs[:3]
        ins = refs[3:3 + 3 * len(SMALL)]
        outs = refs[3 + 3 * len(SMALL):]
        gv = _sum_parts(pv_ref)
        for j, n in enumerate(SMALL):
            w_ref, m_ref, v_ref = ins[3 * j:3 * j + 3]
            o = outs[4 * j:4 * j + 4]
            if n == "gmlp_ws":
                g = _sum_parts(pws_ref)
            elif n == "ffn_conv_b":
                g = _sum_parts(pcb_ref)
            elif n == "gmlp_bs":
                g = gv[BS_ROW:BS_ROW + 8, 0:BLK]
            else:
                g = gv[j:j + 1, 0:VECS[j][1]]
            lead = n in ("gmlp_ws", "gmlp_bs")
            res = (g,) + _adam_math(g, w_ref[0] if lead else w_ref[...], m_ref[0] if lead else m_ref[...],
                                    v_ref[0] if lead else v_ref[...])
            for o_ref, val in zip(o, res):
                if lead:
                    o_ref[0] = val
                else:
                    o_ref[...] = val

    args = [parts_vec, parts_ws, parts_cb] + [d[n] for n in SMALL for d in (w, m, v)]
    res = pl.pallas_call(body, name="adam_small", out_shape=[_sds(w[n].shape) for n in SMALL for _ in range(4)],
                         compiler_params=pltpu.CompilerParams(vmem_limit_bytes=VMEM_LIMIT))(*args)
    return {n: tuple(res[4 * j:4 * j + 4]) for j, n in enumerate(SMALL)}


def kernel(x, mem, positions, mix_norm, w_in, q_norm, k_norm, attn_sinks, gmlp_v_norm, gmlp_ws, gmlp_bs, attn_out_norm, gmlp_out_norm, w_out, xa_norm, mem_norm, xa_wq, xa_wkv, xa_q_norm, xa_k_norm, xa_wo, ffn_norm, ffn_up, ffn_conv, ffn_conv_b, ffn_down, loss_target, m_mix_norm, m_w_in, m_q_norm, m_k_norm, m_attn_sinks, m_gmlp_v_norm, m_gmlp_ws, m_gmlp_bs, m_attn_out_norm, m_gmlp_out_norm, m_w_out, m_xa_norm, m_mem_norm, m_xa_wq, m_xa_wkv, m_xa_q_norm, m_xa_k_norm, m_xa_wo, m_ffn_norm, m_ffn_up, m_ffn_conv, m_ffn_conv_b, m_ffn_down, v_mix_norm, v_w_in, v_q_norm, v_k_norm, v_attn_sinks, v_gmlp_v_norm, v_gmlp_ws, v_gmlp_bs, v_attn_out_norm, v_gmlp_out_norm, v_w_out, v_xa_norm, v_mem_norm, v_xa_wq, v_xa_wkv, v_xa_q_norm, v_xa_k_norm, v_xa_wo, v_ffn_norm, v_ffn_up, v_ffn_conv, v_ffn_conv_b, v_ffn_down):
    names = ("mix_norm", "w_in", "q_norm", "k_norm", "attn_sinks", "gmlp_v_norm", "gmlp_ws", "gmlp_bs", "attn_out_norm",
             "gmlp_out_norm", "w_out", "xa_norm", "mem_norm", "xa_wq", "xa_wkv", "xa_q_norm", "xa_k_norm", "xa_wo",
             "ffn_norm", "ffn_up", "ffn_conv", "ffn_conv_b", "ffn_down")
    w = dict(zip(names, (mix_norm, w_in, q_norm, k_norm, attn_sinks, gmlp_v_norm, gmlp_ws, gmlp_bs, attn_out_norm,
                         gmlp_out_norm, w_out, xa_norm, mem_norm, xa_wq, xa_wkv, xa_q_norm, xa_k_norm, xa_wo, ffn_norm,
                         ffn_up, ffn_conv, ffn_conv_b, ffn_down)))
    m = dict(zip(names, (m_mix_norm, m_w_in, m_q_norm, m_k_norm, m_attn_sinks, m_gmlp_v_norm, m_gmlp_ws, m_gmlp_bs,
                         m_attn_out_norm, m_gmlp_out_norm, m_w_out, m_xa_norm, m_mem_norm, m_xa_wq, m_xa_wkv,
                         m_xa_q_norm, m_xa_k_norm, m_xa_wo, m_ffn_norm, m_ffn_up, m_ffn_conv, m_ffn_conv_b, m_ffn_down)))
    v = dict(zip(names, (v_mix_norm, v_w_in, v_q_norm, v_k_norm, v_attn_sinks, v_gmlp_v_norm, v_gmlp_ws, v_gmlp_bs,
                         v_attn_out_norm, v_gmlp_out_norm, v_w_out, v_xa_norm, v_mem_norm, v_xa_wq, v_xa_wkv,
                         v_xa_q_norm, v_xa_k_norm, v_xa_wo, v_ffn_norm, v_ffn_up, v_ffn_conv, v_ffn_conv_b, v_ffn_down)))
    t = x.shape[1]

    me = 4 * lax.axis_index("x") + 2 * lax.axis_index("y") + lax.axis_index("c")

    mats = [n for n in BIG if n != "ffn_conv"]
    shard = dict(zip(mats, _cast_shards([w[n] for n in mats])), ffn_conv=w["ffn_conv"][0])
    gathers, tokens = zip(*[_exchange_start([], [shard[n] for n in grp], me, "gather_start_%d" % i)
                            for i, grp in enumerate(GATHER_GROUPS)])

    def fetch(i, after):
        after = tokens[0] + tokens[1] + tokens[2] if after is None else after
        got = dict(zip(GATHER_GROUPS[i], _exchange_wait(gathers[i], after, "gather_wait_%d" % i)))
        if "w_in" in got:
            got["w_in"] = got["w_in"].transpose(1, 0, 2).reshape(D, IN)
        for n in ("w_out", "xa_wq", "xa_wo"):
            if n in got:
                got[n] = got[n].reshape(D, D)
        if "ffn_down" in got:
            got["ffn_down"] = got["ffn_down"].reshape(NG, SW, D)
            got["ffn_conv"] = got["ffn_conv"].reshape(2, NG, 3, SW)
        return got

    scatters = []

    def ship(i, grads, for_all=()):
        by_dest = [grads[n].reshape((NDEV,) + w[n].shape[1:]) for n in SCATTER_GROUPS[i]]
        state, token = _exchange_start(by_dest, for_all, me, "scatter_start_%d" % i)
        scatters.append(state)
        return token

    conv_b = {k: d["ffn_conv_b"].reshape(NDEV, 1, SW) for k, d in (("w", w), ("m", m), ("v", v))}
    p = {n: w[n] for n in SMALL[:-1]}
    p["gmlp_ws"], p["gmlp_bs"] = w["gmlp_ws"][0], w["gmlp_bs"][0]
    p["ffn_conv_b"] = conv_b["w"].reshape(2, NG, 1, SW)
    loss, grad_x, g, raw = _local_step(x[0], mem[0], positions.reshape(t, 1), loss_target[0], p, fetch, ship)
    loss = lax.psum(loss, AXES)

    vec, cb = _pack_small(raw)
    g["w_in"] = g["w_in"].reshape(D, NDEV, IN // NDEV).transpose(1, 0, 2)
    token = ship(3, g, [vec, cb.reshape(NDEV, 1, SW)])
    parts, rest = {}, []
    for i, grp in enumerate(SCATTER_GROUPS):
        got = _exchange_wait(scatters[i], token, "scatter_wait_%d" % i)
        parts.update(zip(grp, got))
        rest += got[len(grp):]
    ws_parts, vec_parts, cb_parts = rest
    res = {n: _adam(parts[n], w[n], m[n], v[n], "adam_" + n) for n in BIG}
    small = lambda d, k: {**{n: d[n] for n in SMALL[:-1]}, "ffn_conv_b": conv_b[k]}
    res.update(_adam_small(vec_parts, ws_parts, cb_parts, small(w, "w"), small(m, "m"), small(v, "v")))

    outs = [loss, grad_x[None]]
    for j in range(4):
        outs += [res[n][j].reshape(w[n].shape) for n in names]
    return tuple(outs)
```

```python
import functools
import math

import jax
import jax.numpy as jnp
from jax import lax
from jax.experimental import pallas as pl
from jax.experimental.pallas import tpu as pltpu

F32 = jnp.float32
BF16 = jnp.bfloat16

D = 1024
HD = 64
AW = 512
KW = 128
GW = 512
IN = AW + 2 * KW + 2 * GW
BLK = 128
MEM = 256
XH = 4
XD = 256
FF = 2816
EPS = 1e-6
ROPE_THETA = 10000.0
NDEV = 8
LR, B1, B2, AEPS, WD, STEP = 0.001, 0.9, 0.999, 1e-08, 0.01, 10

TM = 512
WK = 2048
VMEM_LIMIT = 56 * 1024 * 1024
NEG = float(jnp.finfo(jnp.float32).min)
GELU_C0 = math.sqrt(2.0 / math.pi)
GELU_C1 = 0.044715
AXES = ("x", "y", "c")


def _dot(a, b):
    return jnp.dot(a, b, preferred_element_type=F32)


def _dot_nt(a, b):
    return lax.dot_general(a, b, (((1,), (1,)), ((), ())), preferred_element_type=F32)


def _dot_tn(a, b):
    return lax.dot_general(a, b, (((0,), (0,)), ((), ())), preferred_element_type=F32)


def _rs(x):
    return lax.rsqrt(jnp.mean(x * x, axis=-1, keepdims=True) + EPS)


def _rms_bwd(dy, x, r, g):
    xh = x * r
    dxh = dy * g
    dx = r * (dxh - xh * jnp.mean(dxh * xh, axis=-1, keepdims=True))
    return dx, dy * xh


def _lane(shape):
    return lax.broadcasted_iota(jnp.int32, shape, len(shape) - 1)


def _gsum64(v, ones_ref):
    w = v.shape[-1]
    ones = ones_ref[0:w, 0:w]
    hi = v.astype(BF16)
    lo = (v - hi.astype(F32)).astype(BF16)
    return _dot(hi, ones) + _dot(lo, ones)


def _head_ones():
    i = jnp.arange(AW) // HD
    return (i[:, None] == i[None, :]).astype(BF16)


def _rs64(x, ones_ref):
    return lax.rsqrt(_gsum64(x * x, ones_ref) * (1.0 / HD) + EPS)


def _rms64_bwd(dy, x, r, g, ones_ref):
    xh = x * r
    dxh = dy * g
    dx = r * (dxh - xh * (_gsum64(dxh * xh, ones_ref) * (1.0 / HD)))
    return dx, dy * xh


def _rot_half(v):
    w = v.shape[-1]
    return jnp.where((_lane(v.shape) & 32) == 0, pltpu.roll(v, w - 32, 1), pltpu.roll(v, 32, 1))


def _rope(v, cos, sin_signed):
    return v * cos + _rot_half(v) * sin_signed


def _rope_bwd(dv, cos, sin_signed):
    return dv * cos + _rot_half(dv * sin_signed)


def _gelu(z):
    return 0.5 * z * (1.0 + jnp.tanh(GELU_C0 * (z + GELU_C1 * z * z * z)))


def _gelu_grad(z):
    t = jnp.tanh(GELU_C0 * (z + GELU_C1 * z * z * z))
    return 0.5 * (1.0 + t) + 0.5 * z * (1.0 - t * t) * (GELU_C0 * (1.0 + 3.0 * GELU_C1 * z * z))


def _colsum8(v):
    s = jnp.sum(v, axis=0, keepdims=True)
    row = lax.broadcasted_iota(jnp.int32, (8, v.shape[1]), 0)
    return jnp.where(row == 0, jnp.broadcast_to(s, (8, v.shape[1])), 0.0)


def _params(n_axes=1):
    return pltpu.CompilerParams(dimension_semantics=("arbitrary",) * n_axes, vmem_limit_bytes=VMEM_LIMIT)


def _rows(tm, w):
    return pl.BlockSpec((tm, w), lambda i: (i, 0))


def _const(shape):
    nd = len(shape)
    return pl.BlockSpec(shape, lambda *_: (0,) * nd)


def _sds(shape, dtype=F32):
    return jax.ShapeDtypeStruct(shape, dtype)


def _mm_tn(a, b, name):
    g = max(a.shape[0] if a.ndim == 3 else 1, b.shape[0] if b.ndim == 3 else 1)
    t, m = a.shape[-2:]
    n = b.shape[-1]

    def body(a_ref, b_ref, o_ref, acc_ref):
        i = pl.program_id(1)

        @pl.when(i == 0)
        def _():
            acc_ref[...] = jnp.zeros_like(acc_ref)

        acc_ref[...] += _dot_tn(a_ref[...].astype(BF16), b_ref[...].astype(BF16))

        @pl.when(i == pl.num_programs(1) - 1)
        def _():
            o_ref[...] = acc_ref[...].astype(BF16)

    tk = min(t, WK)

    def spec(v):
        w = v.shape[-1]
        if v.ndim == 3:
            return pl.BlockSpec((None, tk, w), lambda j, i: (j, i, 0))
        return pl.BlockSpec((tk, w), lambda j, i: (i, 0))

    return pl.pallas_call(
        body, name=name, grid=(g, t // tk), in_specs=[spec(a), spec(b)],
        out_specs=pl.BlockSpec((None, m, n), lambda j, i: (j, 0, 0)), out_shape=_sds((g, m, n), BF16),
        scratch_shapes=[pltpu.VMEM((m, n), F32)], compiler_params=_params(2))(a, b)


def _rope_tables(pos, inv_freq):
    t = pos.shape[0]

    def body(pos_ref, f_ref, cos_ref, sin_ref):
        ang = pos_ref[...].astype(F32) * f_ref[...]
        sign = jnp.where((_lane(ang.shape) & 32) == 0, -1.0, 1.0)
        cos_ref[...] = jnp.cos(ang)
        sin_ref[...] = jnp.sin(ang) * sign

    return pl.pallas_call(
        body, name="rope_tables", grid=(t // TM,),
        in_specs=[_rows(TM, 1), _const((1, 128))], out_specs=[_rows(TM, 128), _rows(TM, 128)],
        out_shape=[_sds((t, 128)), _sds((t, 128))], compiler_params=_params())(pos, inv_freq)


def _mixer_in_fwd(x, mix_norm, w_in, qn, kn, gvw, cos, sin):
    t = x.shape[0]

    def body(x_ref, g_ref, w_ref, qn_ref, kn_ref, gvw_ref, cos_ref, sin_ref, ones_ref,
             h_ref, qk_ref, gz_ref, q_ref, k_ref, v_ref, gu_ref, gvn_ref):
        x = x_ref[...]
        h = (x * _rs(x) * g_ref[...]).astype(BF16)
        h_ref[...] = h
        proj = _dot_nt(h, w_ref[...])
        qk = proj[:, :AW + KW]
        qk_ref[...] = qk
        gz = proj[:, AW + 2 * KW:]
        gz_ref[...] = gz
        cos2, sin2 = cos_ref[...], sin_ref[...]
        q = qk[:, :AW]
        q = q * _rs64(q, ones_ref) * qn_ref[...]
        q_ref[...] = _rope(q, jnp.tile(cos2, (1, 4)), jnp.tile(sin2, (1, 4))).astype(BF16)
        k = qk[:, AW:]
        k = k * _rs64(k, ones_ref) * kn_ref[...]
        k_ref[...] = _rope(k, cos2, sin2).astype(BF16)
        v_ref[...] = proj[:, AW + KW:AW + 2 * KW].astype(BF16)
        gu_ref[...] = _gelu(gz[:, :GW])
        gv = _gelu(gz[:, GW:])
        gvn_ref[...] = (gv * _rs(gv) * gvw_ref[...]).astype(BF16)

    return pl.pallas_call(
        body, name="mixer_in_fwd", grid=(t // TM,),
        in_specs=[_rows(TM, D), _const((1, D)), _const((IN, D)), _const((1, AW)), _const((1, KW)),
                  _const((1, GW)), _rows(TM, 128), _rows(TM, 128), _const((AW, AW))],
        out_specs=[_rows(TM, D), _rows(TM, AW + KW), _rows(TM, 2 * GW), _rows(TM, AW), _rows(TM, KW),
                   _rows(TM, KW), _rows(TM, GW), _rows(TM, GW)],
        out_shape=[_sds((t, D), BF16), _sds((t, AW + KW)), _sds((t, 2 * GW)), _sds((t, AW), BF16),
                   _sds((t, KW), BF16), _sds((t, KW), BF16), _sds((t, GW)), _sds((t, GW), BF16)],
        compiler_params=_params())(x, mix_norm, w_in, qn, kn, gvw, cos, sin, _head_ones())


def _dup_half(kk, g):
    lane = _lane(kk.shape)
    other = pltpu.roll(kk, 64, 1)
    keep = (lane < 64) if g == 0 else (lane >= 64)
    return jnp.where(keep, kk, other).astype(BF16)


def _swa_mask(first_block):
    qi = lax.broadcasted_iota(jnp.int32, (4 * BLK, 2 * BLK), 0) & (BLK - 1)
    kj = lax.broadcasted_iota(jnp.int32, (4 * BLK, 2 * BLK), 1)
    diff = qi + BLK - kj
    band = (diff >= 0) & (diff < BLK)
    return band & (jnp.logical_not(first_block) | (kj >= BLK))


def _stack_heads(a2, b2):
    lo = _lane(a2.shape) < 64
    z = jnp.zeros_like(a2)
    return jnp.concatenate([jnp.where(lo, a2, z), jnp.where(lo, z, a2), jnp.where(lo, b2, z), jnp.where(lo, z, b2)], axis=0)


def _unstack_heads(o):
    lo = _lane((BLK, 128)) < 64
    return jnp.where(lo, o[0:BLK], o[BLK:2 * BLK]), jnp.where(lo, o[2 * BLK:3 * BLK], o[3 * BLK:4 * BLK])


def _sink_col(sink_ref, g):
    row = lax.broadcasted_iota(jnp.int32, (4 * BLK, 1), 0)
    s = [sink_ref[0, 4 * g + j] for j in range(4)]
    return jnp.where(row < BLK, s[0], jnp.where(row < 2 * BLK, s[1], jnp.where(row < 3 * BLK, s[2], s[3])))


def _swa_probs(qs, kd, mask, sink):
    s = _dot_nt(qs, kd) * (1.0 / math.sqrt(HD))
    s = jnp.where(mask, s, NEG)
    m = jnp.maximum(jnp.max(s, axis=-1, keepdims=True), sink)
    p = jnp.exp(s - m)
    ps = jnp.exp(sink - m)
    inv = 1.0 / (jnp.sum(p, axis=-1, keepdims=True) + ps)
    return p * inv, ps * inv


def _swa_fwd(q, k, v, sinks):
    t = q.shape[0]
    nb = t // BLK

    def body(sink_ref, q_ref, kc_ref, kp_ref, vc_ref, vp_ref, o_ref):
        i = pl.program_id(0)
        mask = _swa_mask(i == 0)
        kk = jnp.concatenate([kp_ref[...], kc_ref[...]], axis=0).astype(F32)
        vv = jnp.concatenate([vp_ref[...], vc_ref[...]], axis=0).astype(F32)
        for g in range(2):
            qs = _stack_heads(q_ref[:, 256 * g:256 * g + 128], q_ref[:, 256 * g + 128:256 * g + 256])
            pn, _ = _swa_probs(qs, _dup_half(kk, g), mask, _sink_col(sink_ref, g))
            oa, ob = _unstack_heads(_dot(pn.astype(BF16), _dup_half(vv, g)))
            o_ref[:, 256 * g:256 * g + 128] = oa
            o_ref[:, 256 * g + 128:256 * g + 256] = ob

    cur = lambda i: (i, 0)
    prev = lambda i: (jnp.maximum(i - 1, 0), 0)
    return pl.pallas_call(
        body, name="swa_fwd", grid=(nb,),
        in_specs=[pl.BlockSpec(memory_space=pltpu.SMEM), pl.BlockSpec((BLK, AW), cur),
                  pl.BlockSpec((BLK, KW), cur), pl.BlockSpec((BLK, KW), prev),
                  pl.BlockSpec((BLK, KW), cur), pl.BlockSpec((BLK, KW), prev)],
        out_specs=pl.BlockSpec((BLK, AW), cur), out_shape=_sds((t, AW)),
        compiler_params=_params())(sinks, q, k, k, v, v)


def _causal_bf16(w_ref, h, transposed):
    r = lax.broadcasted_iota(jnp.int32, (BLK, BLK), 0)
    c = lax.broadcasted_iota(jnp.int32, (BLK, BLK), 1)
    keep = (r <= c) if transposed else (c <= r)
    return jnp.where(keep, w_ref[h], 0.0).astype(BF16)


def _gmlp_mix(w_ref, xin, transposed):
    lo = _lane((BLK, 128)) < 64
    wm = [_causal_bf16(w_ref, h, transposed) for h in range(8)]
    rows = []
    for c in range(xin.shape[0] // BLK):
        cols = []
        for j in range(4):
            xs = xin[c * BLK:(c + 1) * BLK, 128 * j:128 * (j + 1)]
            cols.append(jnp.where(lo, _dot(wm[2 * j], xs), _dot(wm[2 * j + 1], xs)))
        rows.append(jnp.concatenate(cols, axis=1))
    return jnp.concatenate(rows, axis=0)


def _gmlp_fwd(gvn, gu, ws, bfull):
    t = gvn.shape[0]

    def body(x_ref, gu_ref, w_ref, b_ref, o_ref):
        mixed = _gmlp_mix(w_ref, x_ref[...], False) + jnp.tile(b_ref[...], (TM // BLK, 1))
        o_ref[...] = gu_ref[...] * mixed

    return pl.pallas_call(
        body, name="gmlp_fwd", grid=(t // TM,),
        in_specs=[_rows(TM, GW), _rows(TM, GW), _const((8, BLK, BLK)), _const((BLK, GW))],
        out_specs=_rows(TM, GW), out_shape=_sds((t, GW)), compiler_params=_params())(gvn, gu, ws, bfull)


def _mixer_out_fwd(attn, gm, x, w_out, aon, gon, xan):
    t = x.shape[0]

    def body(a_ref, g_ref, x_ref, w_ref, aon_ref, gon_ref, xan_ref, y_ref, x1_ref, h2_ref):
        a, g = a_ref[...], g_ref[...]
        y = jnp.concatenate([a * _rs(a) * aon_ref[...], g * _rs(g) * gon_ref[...]], axis=1).astype(BF16)
        y_ref[...] = y
        x1 = x_ref[...] + _dot(y, w_ref[...])
        x1_ref[...] = x1
        h2_ref[...] = (x1 * _rs(x1) * xan_ref[...]).astype(BF16)

    return pl.pallas_call(
        body, name="mixer_out_fwd", grid=(t // TM,),
        in_specs=[_rows(TM, AW), _rows(TM, GW), _rows(TM, D), _const((D, D)), _const((1, AW)), _const((1, GW)),
                  _const((1, D))],
        out_specs=[_rows(TM, D), _rows(TM, D), _rows(TM, D)],
        out_shape=[_sds((t, D), BF16), _sds((t, D)), _sds((t, D), BF16)],
        compiler_params=_params())(attn, gm, x, w_out, aon, gon, xan)


def _mem_kv_fwd(mem, mem_norm, wkv, kn4):
    def body(m_ref, g_ref, w_ref, kn_ref, mh_ref, kpre_ref, k_ref, v_ref):
        m = m_ref[...]
        mh = (m * _rs(m) * g_ref[...]).astype(BF16)
        mh_ref[...] = mh
        for h in range(XH):
            sl = slice(XD * h, XD * (h + 1))
            kh = _dot_nt(mh, w_ref[h])
            kpre_ref[:, sl] = kh
            k_ref[:, sl] = (kh * _rs(kh) * kn_ref[:, sl]).astype(BF16)
            v_ref[:, sl] = _dot_nt(mh, w_ref[XH + h]).astype(BF16)

    return pl.pallas_call(
        body, name="mem_kv_fwd",
        out_shape=[_sds((MEM, D), BF16), _sds((MEM, D)), _sds((MEM, D), BF16), _sds((MEM, D), BF16)],
        compiler_params=pltpu.CompilerParams(vmem_limit_bytes=VMEM_LIMIT))(mem, mem_norm, wkv, kn4)


def _xattn_probs(qpre_h, qn_h, k_h):
    rq = _rs(qpre_h)
    q2 = (qpre_h * rq * qn_h).astype(BF16)
    s = _dot_nt(q2, k_h) * (1.0 / math.sqrt(XD))
    p = jnp.exp(s - jnp.max(s, axis=-1, keepdims=True))
    return p * (1.0 / jnp.sum(p, axis=-1, keepdims=True)), q2, rq


def _xattn_fwd(h2, x1, wq, qn4, k2, v2, wo, ffn_norm):
    t = x1.shape[0]

    def body(h_ref, x_ref, wq_ref, qn_ref, k_ref, v_ref, wo_ref, fn_ref, qpre_ref, o_ref, x2_ref, h3_ref):
        qpre = _dot(h_ref[...], wq_ref[...])
        qpre_ref[...] = qpre
        outs = []
        for h in range(XH):
            sl = slice(XD * h, XD * (h + 1))
            pn, _, _ = _xattn_probs(qpre[:, sl], qn_ref[:, sl], k_ref[:, sl])
            outs.append(_dot(pn.astype(BF16), v_ref[:, sl]))
        o = jnp.concatenate(outs, axis=1).astype(BF16)
        o_ref[...] = o
        x2 = x_ref[...] + _dot(o, wo_ref[...])
        x2_ref[...] = x2
        h3_ref[...] = (x2 * _rs(x2) * fn_ref[...]).astype(BF16)

    return pl.pallas_call(
        body, name="xattn_fwd", grid=(t // TM,),
        in_specs=[_rows(TM, D), _rows(TM, D), _const((D, D)), _const((1, D)), _const((MEM, D)), _const((MEM, D)),
                  _const((D, D)), _const((1, D))],
        out_specs=[_rows(TM, D)] * 4,
        out_shape=[_sds((t, D)), _sds((t, D), BF16), _sds((t, D)), _sds((t, D), BF16)],
        compiler_params=_params())(h2, x1, wq, qn4, k2, v2, wo, ffn_norm)


SW = 704
NG = FF // SW
FM = 256
HALO = 16


def _resident(shape):
    nd = len(shape)
    return pl.BlockSpec(shape, lambda *_: (0,) * nd, pipeline_mode=pl.Buffered(1))


def _halo_before(i):
    return jnp.maximum(i * (FM // HALO) - 1, 0)


def _conv(e, w):
    return w[2:3, :] * e + pltpu.roll(w[1:2, :] * e + pltpu.roll(w[0:1, :] * e, 1, 0), 1, 0)


def _conv_t(dc, w):
    n = dc.shape[0]
    return w[2:3, :] * dc + pltpu.roll(w[1:2, :] * dc + pltpu.roll(w[0:1, :] * dc, n - 1, 0), n - 1, 0)


def _ffn_fwd(h3, x2, target, up, conv, conv_b, down):
    t = x2.shape[0]

    def body(h_ref, hp_ref, x_ref, t_ref, up_ref, w_ref, b_ref, dn_ref, a_ref, u_ref, gs_ref, dy_ref, loss_ref, acc_ref):
        i = pl.program_id(0)

        @pl.when(i == 0)
        def _():
            acc_ref[...] = jnp.zeros_like(acc_ref)

        before = jnp.where(i > 0, hp_ref[...], jnp.zeros_like(hp_ref))
        he = jnp.concatenate([before, h_ref[...]], axis=0)
        err = x_ref[...] - t_ref[...]
        for d in range(NG):
            c = []
            for s in range(2):
                a = _dot_nt(he, up_ref[s * NG + d]).astype(BF16)
                a_ref[s * NG + d] = a[HALO:]
                c.append(_conv(a.astype(F32), w_ref[s, d])[HALO:] + b_ref[s, d])
            gl, gg = _gelu_and_grad(c[0])
            gs_ref[d] = gl.astype(BF16)
            gs_ref[NG + d] = (gg * c[1]).astype(BF16)
            u = (gl * c[1]).astype(BF16)
            u_ref[d] = u
            err = err + _dot(u, dn_ref[d])
        dy_ref[...] = err * (1.0 / D)
        acc_ref[...] += jnp.sum(err * err, axis=0, keepdims=True)

        @pl.when(i == pl.num_programs(0) - 1)
        def _():
            loss_ref[...] = jnp.full((8, 128), 0.5 / D, F32) * jnp.sum(acc_ref[...])

    return pl.pallas_call(
        body, name="ffn_fwd", grid=(t // FM,),
        in_specs=[_rows(FM, D), pl.BlockSpec((HALO, D), lambda i: (_halo_before(i), 0)), _rows(FM, D), _rows(FM, D),
                  _resident((NDEV, SW, D)), _resident((2, NG, 3, SW)), _resident((2, NG, 1, SW)), _resident((NG, SW, D))],
        out_specs=[pl.BlockSpec((NDEV, FM, SW), lambda i: (0, i, 0)), pl.BlockSpec((NG, FM, SW), lambda i: (0, i, 0)),
                   pl.BlockSpec((NDEV, FM, SW), lambda i: (0, i, 0)), _rows(FM, D), _const((8, 128))],
        out_shape=[_sds((NDEV, t, SW), BF16), _sds((NG, t, SW), BF16), _sds((NDEV, t, SW), BF16), _sds((t, D)),
                   _sds((8, 128))],
        scratch_shapes=[pltpu.VMEM((1, D), F32)], compiler_params=_params())(h3, h3, x2, target, up, conv, conv_b, down)


def _gelu_and_grad(z):
    z2 = z * z
    t = jnp.tanh(GELU_C0 * (z + GELU_C1 * z * z2))
    phi = 0.5 * (1.0 + t)
    return z * phi, phi + 0.5 * z * (1.0 - t * t) * (GELU_C0 * (1.0 + 3.0 * GELU_C1 * z2))


def _ffn_bwd(dy, a, gs, x2, up, conv, down, ffn_norm):
    t = x2.shape[0]
    nt = t // FM
    n = FM + HALO

    def body(dy_ref, dyn_ref, a_ref, gs_ref, gsn_ref, x_ref, up_ref, w_ref, dn_ref, g_ref,
             dx_ref, da_ref, s_ref, dfn_ref):
        i = pl.program_id(0)

        @pl.when(i == 0)
        def _():
            s_ref[...] = jnp.zeros_like(s_ref)
            dfn_ref[...] = jnp.zeros_like(dfn_ref)

        last = i == nt - 1
        dy = dy_ref[...]
        dye = jnp.concatenate([dy, jnp.where(last, 0.0, dyn_ref[...])], axis=0).astype(BF16)
        dh = jnp.zeros((FM, D), F32)
        row = lax.broadcasted_iota(jnp.int32, (8, SW), 0)
        for d in range(NG):
            du = _dot_nt(dye, dn_ref[d])
            for s in range(2):
                j = s * NG + d
                k = NG + d if s == 0 else d
                dc = du * jnp.concatenate([gs_ref[k], gsn_ref[k]], axis=0).astype(F32)
                w = w_ref[s, d]
                tile = a_ref[j].astype(F32)
                d1 = pltpu.roll(dc, n - 1, 0)
                d2 = pltpu.roll(d1, n - 1, 0)
                da = (w[2:3, :] * dc + w[1:2, :] * d1 + w[0:1, :] * d2)[0:FM].astype(BF16)
                da_ref[j] = da
                dh = dh + _dot(da, up_ref[j])
                sums = [jnp.sum(v[0:FM] * tile, axis=0, keepdims=True) for v in (d2, d1, dc)]
                sums.append(jnp.sum(dc[0:FM], axis=0, keepdims=True))
                upd = jnp.zeros((8, SW), F32)
                for r, v in enumerate(sums):
                    upd = jnp.where(row == r, jnp.broadcast_to(v, (8, SW)), upd)
                s_ref[s, d] += upd
        x = x_ref[...]
        dx, dg = _rms_bwd(dh, x, _rs(x), g_ref[...])
        dx_ref[...] = dy + dx
        dfn_ref[...] += _colsum8(dg)

    last_halo = t // HALO - 1
    after = lambda i: jnp.minimum((i + 1) * (FM // HALO), last_halo)
    return pl.pallas_call(
        body, name="ffn_bwd", grid=(nt,),
        in_specs=[_rows(FM, D), pl.BlockSpec((HALO, D), lambda i: (after(i), 0)),
                  pl.BlockSpec((NDEV, FM, SW), lambda i: (0, i, 0)),
                  pl.BlockSpec((NDEV, FM, SW), lambda i: (0, i, 0)),
                  pl.BlockSpec((NDEV, HALO, SW), lambda i: (0, after(i), 0)),
                  _rows(FM, D), _resident((NDEV, SW, D)), _resident((2, NG, 3, SW)), _resident((NG, SW, D)), _const((1, D))],
        out_specs=[_rows(FM, D), pl.BlockSpec((NDEV, FM, SW), lambda i: (0, i, 0)), _const((2, NG, 8, SW)), _const((8, D))],
        out_shape=[_sds((t, D)), _sds((NDEV, t, SW), BF16), _sds((2, NG, 8, SW)), _sds((8, D))],
        compiler_params=_params())(dy, dy, a, gs, gs, x2, up, conv, down, ffn_norm)


BT = 256


def _xattn_bwd(dx2, x1, qpre, k2, v2, wq, wo, qn4, xan):
    t = x1.shape[0]

    def body(dx2_ref, x1_ref, qpre_ref, k_ref, v_ref, wq_ref, wo_ref, qn_ref, xan_ref,
             dx1_ref, dqpre_ref, dk_ref, dv_ref, dqn_ref, dxan_ref):
        @pl.when(pl.program_id(0) == 0)
        def _():
            for r in (dk_ref, dv_ref, dqn_ref, dxan_ref):
                r[...] = jnp.zeros_like(r)

        dx2 = dx2_ref[...]
        do = _dot_nt(dx2.astype(BF16), wo_ref[...])
        dqs = []
        for h in range(XH):
            sl = slice(XD * h, XD * (h + 1))
            qpre_h = qpre_ref[:, sl]
            pn, q2, rq = _xattn_probs(qpre_h, qn_ref[:, sl], k_ref[:, sl])
            do_h = do[:, sl].astype(BF16)
            dp = _dot_nt(do_h, v_ref[:, sl])
            ds = (pn * (dp - jnp.sum(pn * dp, axis=-1, keepdims=True)) * (1.0 / math.sqrt(XD))).astype(BF16)
            dq2 = _dot(ds, k_ref[:, sl])
            dk_ref[:, sl] += _dot_tn(ds, q2)
            dv_ref[:, sl] += _dot_tn(pn.astype(BF16), do_h)
            dqh, dg = _rms_bwd(dq2, qpre_h, rq, qn_ref[:, sl])
            dqn_ref[...] += _colsum8(dg)
            dqs.append(dqh)
        dqpre = jnp.concatenate(dqs, axis=1).astype(BF16)
        dqpre_ref[...] = dqpre
        dh2 = _dot_nt(dqpre, wq_ref[...])
        x1 = x1_ref[...]
        dx, dg = _rms_bwd(dh2, x1, _rs(x1), xan_ref[...])
        dx1_ref[...] = dx2 + dx
        dxan_ref[...] += _colsum8(dg)

    return pl.pallas_call(
        body, name="xattn_bwd", grid=(t // BT,),
        in_specs=[_rows(BT, D), _rows(BT, D), _rows(BT, D), _const((MEM, D)), _const((MEM, D)), _const((D, D)),
                  _const((D, D)), _const((1, D)), _const((1, D))],
        out_specs=[_rows(BT, D), _rows(BT, D), _const((MEM, D)), _const((MEM, D)), _const((8, XD)), _const((8, D))],
        out_shape=[_sds((t, D)), _sds((t, D), BF16), _sds((MEM, D)), _sds((MEM, D)), _sds((8, XD)), _sds((8, D))],
        compiler_params=_params())(dx2, x1, qpre, k2, v2, wq, wo, qn4, xan)


def _mem_kv_bwd(mem, mh, kpre, dk2, dv2, wkv, kn4, mem_norm):
    def body(m_ref, mh_ref, kpre_ref, dk_ref, dv_ref, w_ref, kn_ref, g_ref, dw_ref, dkn_ref, dmn_ref):
        dkn = jnp.zeros((8, XD), F32)
        dm = jnp.zeros((MEM, D), F32)
        mh = mh_ref[...]
        for h in range(XH):
            sl = slice(XD * h, XD * (h + 1))
            kh = kpre_ref[:, sl]
            dkh, dg = _rms_bwd(dk_ref[:, sl], kh, _rs(kh), kn_ref[:, sl])
            dkn = dkn + _colsum8(dg)
            dkh = dkh.astype(BF16)
            dvh = dv_ref[:, sl].astype(BF16)
            dw_ref[h] = _dot_tn(dkh, mh).astype(BF16)
            dw_ref[XH + h] = _dot_tn(dvh, mh).astype(BF16)
            dm = dm + _dot(dkh, w_ref[h]) + _dot(dvh, w_ref[XH + h])
        dkn_ref[...] = dkn
        m = m_ref[...]
        _, dg = _rms_bwd(dm, m, _rs(m), g_ref[...])
        dmn_ref[...] = _colsum8(dg)

    return pl.pallas_call(
        body, name="mem_kv_bwd", out_shape=[_sds((2 * XH, XD, D), BF16), _sds((8, XD)), _sds((8, D))],
        compiler_params=pltpu.CompilerParams(vmem_limit_bytes=VMEM_LIMIT))(mem, mh, kpre, dk2, dv2, wkv, kn4, mem_norm)


def _mixer_out_bwd(dx1, attn, gm, w_out, aon, gon):
    t = dx1.shape[0]

    def body(dx_ref, a_ref, g_ref, w_ref, aon_ref, gon_ref, da_ref, dg_ref, dan_ref, dgn_ref):
        @pl.when(pl.program_id(0) == 0)
        def _():
            dan_ref[...] = jnp.zeros_like(dan_ref)
            dgn_ref[...] = jnp.zeros_like(dgn_ref)

        dy = _dot_nt(dx_ref[...].astype(BF16), w_ref[...])
        a, g = a_ref[...], g_ref[...]
        da, dna = _rms_bwd(dy[:, :AW], a, _rs(a), aon_ref[...])
        dg, dng = _rms_bwd(dy[:, AW:], g, _rs(g), gon_ref[...])
        da_ref[...] = da
        dg_ref[...] = dg
        dan_ref[...] += _colsum8(dna)
        dgn_ref[...] += _colsum8(dng)

    return pl.pallas_call(
        body, name="mixer_out_bwd", grid=(t // TM,),
        in_specs=[_rows(TM, D), _rows(TM, AW), _rows(TM, GW), _const((D, D)), _const((1, AW)), _const((1, GW))],
        out_specs=[_rows(TM, AW), _rows(TM, GW), _const((8, AW)), _const((8, GW))],
        out_shape=[_sds((t, AW)), _sds((t, GW)), _sds((8, AW)), _sds((8, GW))],
        compiler_params=_params())(dx1, attn, gm, w_out, aon, gon)


def _gmlp_bwd(dgm, gu, gvn, gz, ws, wst, bfull, gvw):
    t = dgm.shape[0]
    nc = TM // BLK

    def body(dgm_ref, gu_ref, x_ref, gz_ref, w_ref, wt_ref, b_ref, gvw_ref, dgz_ref, dw_ref, db_ref, dgvw_ref,
             dbacc_ref):
        @pl.when(pl.program_id(0) == 0)
        def _():
            for r in (dw_ref, dbacc_ref, dgvw_ref):
                r[...] = jnp.zeros_like(r)

        xin = x_ref[...]
        dgm = dgm_ref[...]
        mixed = _gmlp_mix(w_ref, xin, False) + jnp.tile(b_ref[...], (nc, 1))
        dgu = dgm * mixed
        dmixed = dgm * gu_ref[...]
        lo = _lane((BLK, 128)) < 64
        dbias = jnp.zeros((BLK, GW), F32)
        for c in range(nc):
            dmc = dmixed[c * BLK:(c + 1) * BLK]
            dbias = dbias + dmc
            for j in range(4):
                dm2 = dmc[:, 128 * j:128 * (j + 1)]
                xs = xin[c * BLK:(c + 1) * BLK, 128 * j:128 * (j + 1)]
                z = jnp.zeros_like(dm2)
                dw_ref[2 * j] += _dot_nt(jnp.where(lo, dm2, z).astype(BF16), xs)
                dw_ref[2 * j + 1] += _dot_nt(jnp.where(lo, z, dm2).astype(BF16), xs)
        dbacc_ref[...] += dbias
        dgvn = _gmlp_mix(wt_ref, dmixed.astype(BF16), True)
        gz_u, gz_v = gz_ref[:, :GW], gz_ref[:, GW:]
        gv = _gelu(gz_v)
        dgv, dg = _rms_bwd(dgvn, gv, _rs(gv), gvw_ref[...])
        dgvw_ref[...] += _colsum8(dg)
        dgz_ref[:, :GW] = (dgu * _gelu_grad(gz_u)).astype(BF16)
        dgz_ref[:, GW:] = (dgv * _gelu_grad(gz_v)).astype(BF16)

        @pl.when(pl.program_id(0) == pl.num_programs(0) - 1)
        def _():
            s = dbacc_ref[...]
            sel = (lax.broadcasted_iota(jnp.int32, (8, GW), 1) // HD
                   == lax.broadcasted_iota(jnp.int32, (8, GW), 0)).astype(BF16)
            hi = s.astype(BF16)
            r1 = s - hi.astype(F32)
            mid = r1.astype(BF16)
            lo = (r1 - mid.astype(F32)).astype(BF16)
            db_ref[...] = _dot_nt(sel, hi) + _dot_nt(sel, mid) + _dot_nt(sel, lo)
            r = lax.broadcasted_iota(jnp.int32, (BLK, BLK), 0)
            c = lax.broadcasted_iota(jnp.int32, (BLK, BLK), 1)
            for h in range(8):
                dw_ref[h] = jnp.where(c <= r, dw_ref[h], 0.0)

    return pl.pallas_call(
        body, name="gmlp_bwd", grid=(t // TM,),
        in_specs=[_rows(TM, GW), _rows(TM, GW), _rows(TM, GW), _rows(TM, 2 * GW), _const((8, BLK, BLK)),
                  _const((8, BLK, BLK)), _const((BLK, GW)), _const((1, GW))],
        out_specs=[_rows(TM, 2 * GW), _const((8, BLK, BLK)), _const((8, BLK)), _const((8, GW))],
        out_shape=[_sds((t, 2 * GW), BF16), _sds((8, BLK, BLK)), _sds((8, BLK)), _sds((8, GW))],
        scratch_shapes=[pltpu.VMEM((BLK, GW), F32)],
        compiler_params=_params())(dgm, gu, gvn, gz, ws, wst, bfull, gvw)


def _fold_half(v):
    return v + pltpu.roll(v, 64, 1)


def _swa_bwd(q, k, v, dattn, sinks):
    t = q.shape[0]
    nb = t // BLK

    def body(sink_ref, q_ref, kc_ref, kp_ref, vc_ref, vp_ref, do_ref, dq_ref, dk_ref, dv_ref, ds_ref,
             ck_ref, cv_ref, sacc_ref):
        i = pl.program_id(0)

        @pl.when(i == 0)
        def _():
            ck_ref[...] = jnp.zeros_like(ck_ref)
            cv_ref[...] = jnp.zeros_like(cv_ref)
            sacc_ref[...] = jnp.zeros_like(sacc_ref)

        @pl.when(i < nb)
        def _():
            mask = _swa_mask(i == 0)
            kk = jnp.concatenate([kp_ref[...], kc_ref[...]], axis=0).astype(F32)
            vv = jnp.concatenate([vp_ref[...], vc_ref[...]], axis=0).astype(F32)
            lo256 = _lane((2 * BLK, 128)) < 64
            dkk = jnp.zeros((2 * BLK, 128), F32)
            dvv = jnp.zeros((2 * BLK, 128), F32)
            for g in range(2):
                qs = _stack_heads(q_ref[:, 256 * g:256 * g + 128], q_ref[:, 256 * g + 128:256 * g + 256])
                dos = _stack_heads(do_ref[:, 256 * g:256 * g + 128],
                                   do_ref[:, 256 * g + 128:256 * g + 256]).astype(BF16)
                kd = _dup_half(kk, g)
                pn, psn = _swa_probs(qs, kd, mask, _sink_col(sink_ref, g))
                dp = _dot_nt(dos, _dup_half(vv, g))
                dd = jnp.sum(pn * dp, axis=-1, keepdims=True)
                ds = (pn * (dp - dd) * (1.0 / math.sqrt(HD))).astype(BF16)
                sacc_ref[g] += jnp.broadcast_to(-psn * dd, (4 * BLK, 128))
                dqa, dqb = _unstack_heads(_dot(ds, kd))
                dq_ref[:, 256 * g:256 * g + 128] = dqa
                dq_ref[:, 256 * g + 128:256 * g + 256] = dqb
                dkg = _fold_half(_dot_tn(ds, qs))
                dvg = _fold_half(_dot_tn(pn.astype(BF16), dos))
                keep = lo256 if g == 0 else jnp.logical_not(lo256)
                dkk = jnp.where(keep, dkg, dkk)
                dvv = jnp.where(keep, dvg, dvv)
            dk_ref[...] = ck_ref[...] + dkk[0:BLK]
            dv_ref[...] = cv_ref[...] + dvv[0:BLK]
            ck_ref[...] = dkk[BLK:]
            cv_ref[...] = dvv[BLK:]

        @pl.when(i == nb)
        def _():
            dk_ref[...] = ck_ref[...]
            dv_ref[...] = cv_ref[...]
            lane = _lane((8, 128))
            acc = jnp.zeros((8, 128), F32)
            for g in range(2):
                for j in range(4):
                    val = jnp.sum(sacc_ref[g, j * BLK:(j + 1) * BLK, :], axis=0, keepdims=True)
                    acc = jnp.where(lane == 4 * g + j, jnp.broadcast_to(val, (8, 128)), acc)
            ds_ref[...] = acc

    cur = lambda i: (jnp.minimum(i, nb - 1), 0)
    prev = lambda i: (jnp.clip(i - 1, 0, nb - 1), 0)
    return pl.pallas_call(
        body, name="swa_bwd", grid=(nb + 1,),
        in_specs=[pl.BlockSpec(memory_space=pltpu.SMEM), pl.BlockSpec((BLK, AW), cur),
                  pl.BlockSpec((BLK, KW), cur), pl.BlockSpec((BLK, KW), prev),
                  pl.BlockSpec((BLK, KW), cur), pl.BlockSpec((BLK, KW), prev), pl.BlockSpec((BLK, AW), cur)],
        out_specs=[pl.BlockSpec((BLK, AW), cur), pl.BlockSpec((BLK, KW), prev), pl.BlockSpec((BLK, KW), prev),
                   _const((8, 128))],
        out_shape=[_sds((t, AW)), _sds((t, KW)), _sds((t, KW)), _sds((8, 128))],
        scratch_shapes=[pltpu.VMEM((BLK, KW), F32), pltpu.VMEM((BLK, KW), F32), pltpu.VMEM((2, 4 * BLK, 128), F32)],
        compiler_params=_params())(sinks, q, k, k, v, v, dattn)


def _mixer_in_bwd(dq, dk, dv, dgz, qk, cos, sin, x, dx1, w_in, mix_norm, qn, kn):
    t = x.shape[0]

    def body(dq_ref, dk_ref, dv_ref, dgz_ref, qk_ref, cos_ref, sin_ref, x_ref, dx1_ref, w_ref, g_ref, qn_ref, kn_ref,
             ones_ref, gx_ref, dproj_ref, dmn_ref, dqn_ref, dkn_ref, qacc_ref, kacc_ref):
        i = pl.program_id(0)

        @pl.when(i == 0)
        def _():
            dmn_ref[...] = jnp.zeros_like(dmn_ref)
            qacc_ref[...] = jnp.zeros_like(qacc_ref)
            kacc_ref[...] = jnp.zeros_like(kacc_ref)

        cos2, sin2 = cos_ref[...], sin_ref[...]
        qpre, kpre = qk_ref[:, :AW], qk_ref[:, AW:]
        dqh = _rope_bwd(dq_ref[...], jnp.tile(cos2, (1, 4)), jnp.tile(sin2, (1, 4)))
        dqpre, dgq = _rms64_bwd(dqh, qpre, _rs64(qpre, ones_ref), qn_ref[...], ones_ref)
        dkh = _rope_bwd(dk_ref[...], cos2, sin2)
        dkpre, dgk = _rms64_bwd(dkh, kpre, _rs64(kpre, ones_ref), kn_ref[...], ones_ref)
        qacc_ref[...] += jnp.sum(dgq, axis=0, keepdims=True)
        kacc_ref[...] += jnp.sum(dgk, axis=0, keepdims=True)
        dproj = jnp.concatenate([dqpre.astype(BF16), dkpre.astype(BF16), dv_ref[...].astype(BF16), dgz_ref[...]], axis=1)
        dproj_ref[...] = dproj
        dh = _dot(dproj, w_ref[...])
        xv = x_ref[...]
        dx, dg = _rms_bwd(dh, xv, _rs(xv), g_ref[...])
        gx_ref[...] = dx1_ref[...] + dx
        dmn_ref[...] += _colsum8(dg)

        @pl.when(i == pl.num_programs(0) - 1)
        def _():
            qa = qacc_ref[...]
            q4 = qa[:, 0:128] + qa[:, 128:256] + qa[:, 256:384] + qa[:, 384:512]
            dqn_ref[...] = jnp.broadcast_to(_fold_half(q4), (8, 128))
            dkn_ref[...] = jnp.broadcast_to(_fold_half(kacc_ref[...]), (8, 128))

    return pl.pallas_call(
        body, name="mixer_in_bwd", grid=(t // TM,),
        in_specs=[_rows(TM, AW), _rows(TM, KW), _rows(TM, KW), _rows(TM, 2 * GW), _rows(TM, AW + KW), _rows(TM, 128),
                  _rows(TM, 128), _rows(TM, D), _rows(TM, D), _const((IN, D)), _const((1, D)), _const((1, AW)),
                  _const((1, KW)), _const((AW, AW))],
        out_specs=[_rows(TM, D), _rows(TM, IN), _const((8, D)), _const((8, 128)), _const((8, 128))],
        out_shape=[_sds((t, D)), _sds((t, IN), BF16), _sds((8, D)), _sds((8, 128)), _sds((8, 128))],
        scratch_shapes=[pltpu.VMEM((1, AW), F32), pltpu.VMEM((1, KW), F32)],
        compiler_params=_params())(dq, dk, dv, dgz, qk, cos, sin, x, dx1, w_in, mix_norm, qn, kn, _head_ones())


def _local_step(x, mem, pos, target, p, fetch, ship):
    t = x.shape[0]
    p = dict(p)
    p.update(fetch(0, None))
    inv_freq = 1.0 / (ROPE_THETA ** (jnp.arange(HD // 2, dtype=F32) * (2.0 / HD)))
    cos, sin = _rope_tables(pos, jnp.tile(inv_freq, 4).reshape(1, 128))
    qn = jnp.tile(p["q_norm"], (1, AW // HD))
    kn = jnp.tile(p["k_norm"], (1, KW // HD))
    qn4 = jnp.tile(p["xa_q_norm"], (1, XH))
    kn4 = jnp.tile(p["xa_k_norm"], (1, XH))
    ws = p["gmlp_ws"]
    wst = jnp.swapaxes(ws, 1, 2)
    bfull = jnp.repeat(p["gmlp_bs"].T, HD, axis=1)
    conv_b = p["ffn_conv_b"]

    h1, qk, gz, q, k, v, gu, gvn = _mixer_in_fwd(x, p["mix_norm"], p["w_in"], qn, kn, p["gmlp_v_norm"], cos, sin)
    attn = _swa_fwd(q, k, v, p["attn_sinks"])
    gm = _gmlp_fwd(gvn, gu, ws, bfull)
    ycat, x1, h2 = _mixer_out_fwd(attn, gm, x, p["w_out"], p["attn_out_norm"], p["gmlp_out_norm"], p["xa_norm"])
    p.update(fetch(1, h2))
    mh, kpre, k2, v2 = _mem_kv_fwd(mem, p["mem_norm"], p["xa_wkv"], kn4)
    qpre, o, x2, h3 = _xattn_fwd(h2, x1, p["xa_wq"], qn4, k2, v2, p["xa_wo"], p["ffn_norm"])
    p.update(fetch(2, h3))
    conv = p["ffn_conv"]
    a, u, gs, dy, loss8 = _ffn_fwd(h3, x2, target, p["ffn_up"], conv, conv_b, p["ffn_down"])

    raw = {}
    d_down = _mm_tn(u, dy, "ffn_down_bwd_w")
    dx2, da, raw["conv_sums"], raw["ffn_norm"] = _ffn_bwd(dy, a, gs, x2, p["ffn_up"], conv, p["ffn_down"], p["ffn_norm"])
    d_up = _mm_tn(da, h3, "ffn_up_bwd_w")
    token = ship(0, {"ffn_down": d_down, "ffn_up": d_up, "ffn_conv": raw["conv_sums"][:, :, 0:3]})
    dx1, dqpre, dk2, dv2, raw["xa_q_norm"], raw["xa_norm"] = _xattn_bwd(
        dx2, x1, qpre, k2, v2, p["xa_wq"], p["xa_wo"], qn4 + jnp.tile(token[0:1], (1, D // 128)), p["xa_norm"])
    d_wo = _mm_tn(o, dx2, "xa_wo_bwd_w")
    d_wq = _mm_tn(h2, dqpre, "xa_wq_bwd_w")
    d_wkv, raw["xa_k_norm"], raw["mem_norm"] = _mem_kv_bwd(mem, mh, kpre, dk2, dv2, p["xa_wkv"], kn4, p["mem_norm"])
    d_w_out = _mm_tn(ycat, dx1, "w_out_bwd_w")
    token = ship(1, {"xa_wo": d_wo, "xa_wq": d_wq, "xa_wkv": d_wkv, "w_out": d_w_out})
    dattn, dgm, raw["attn_out_norm"], raw["gmlp_out_norm"] = _mixer_out_bwd(
        dx1, attn, gm, p["w_out"], p["attn_out_norm"] + jnp.tile(token[0:1], (1, AW // 128)), p["gmlp_out_norm"])
    dgz, raw["gmlp_ws"], raw["gmlp_bs"], raw["gmlp_v_norm"] = _gmlp_bwd(dgm, gu, gvn, gz, ws, wst, bfull, p["gmlp_v_norm"])
    token = ship(2, {}, [raw["gmlp_ws"]])
    dq, dk, dv, raw["attn_sinks"] = _swa_bwd(q, k, v, dattn, p["attn_sinks"] + token[0:1, 0:8])
    grad_x, dproj, raw["mix_norm"], raw["q_norm"], raw["k_norm"] = _mixer_in_bwd(
        dq, dk, dv, dgz, qk, cos, sin, x, dx1, p["w_in"], p["mix_norm"], qn, kn)
    d_w_in = _mm_tn(dproj, h1, "w_in_bwd_w")
    return loss8[0, 0], grad_x, {"w_in": d_w_in}, raw


def _cast_shards(shards):
    def body(*refs):
        n = len(refs) // 2
        for i_ref, o_ref in zip(refs[:n], refs[n:]):
            o_ref[...] = i_ref[...].astype(BF16)

    return pl.pallas_call(body, name="cast_shards", out_shape=[_sds(s.shape, BF16) for s in shards],
                          compiler_params=pltpu.CompilerParams(vmem_limit_bytes=VMEM_LIMIT))(*shards)


HBM_SPEC = pl.BlockSpec(memory_space=pltpu.HBM)
SEM_SPEC = pl.BlockSpec(memory_space=pltpu.SEMAPHORE)


def _remote_copies(src_refs, land_refs, send_refs, recv_refs, nd):
    x, y, cc = lax.axis_index("x"), lax.axis_index("y"), lax.axis_index("c")
    me = 4 * x + 2 * y + cc
    copies = []
    for a, (src_ref, land_ref) in enumerate(zip(src_refs, land_refs)):
        for k in range(1, NDEV):
            px = 1 - x if k & 4 else x
            py = 1 - y if k & 2 else y
            pc = 1 - cc if k & 1 else cc
            copies.append(pltpu.make_async_remote_copy(
                src_ref=src_ref.at[4 * px + 2 * py + pc] if a < nd else src_ref, dst_ref=land_ref.at[me],
                send_sem=send_refs[a].at[k - 1], recv_sem=recv_refs[a].at[k - 1],
                device_id=(px, py, pc), device_id_type=pl.DeviceIdType.MESH))
    return copies


def _own_slot(src, by_dest, me):
    block = lax.dynamic_index_in_dim(src, me, 0, keepdims=True) if by_dest else src[None]
    return lax.dynamic_update_index_in_dim(lax.empty((NDEV,) + block.shape[1:], src.dtype), block, me, 0)


def _exchange_start(by_dest, for_all, me, name):
    srcs = list(by_dest) + list(for_all)
    n, nd = len(srcs), len(by_dest)
    lands = [_own_slot(s, a < nd, me) for a, s in enumerate(srcs)]

    def body(*refs):
        for cp in _remote_copies(refs[:n], refs[n:2 * n], refs[2 * n:3 * n], refs[3 * n:4 * n], nd):
            cp.start()
        refs[-1][...] = jnp.zeros((8, 128), F32)

    sems = [pltpu.SemaphoreType.DMA((NDEV - 1,))] * (2 * n)
    thru = [pltpu.HBM(v.shape, v.dtype) for v in srcs + lands]
    res = pl.pallas_call(
        body, name=name, out_shape=sems + thru + [_sds((8, 128))],
        in_specs=[HBM_SPEC] * (2 * n), out_specs=[SEM_SPEC] * (2 * n) + [HBM_SPEC] * (2 * n) + [pl.BlockSpec(memory_space=pltpu.VMEM)],
        input_output_aliases={i: 2 * n + i for i in range(2 * n)},
        compiler_params=pltpu.CompilerParams(has_side_effects=pltpu.SideEffectType.DATAFLOW_SIDE_EFFECTING))(
            *[pltpu.with_memory_space_constraint(v, pltpu.HBM) for v in srcs + lands])
    return (res[:2 * n], res[2 * n:4 * n], nd), res[-1]


def _exchange_wait(state, after, name):
    sems, thru, nd = state
    n = len(thru) // 2

    def body(*refs):
        for cp in _remote_copies(refs[:n], refs[n:2 * n], refs[2 * n:3 * n], refs[3 * n:4 * n], nd):
            cp.wait_send()
            cp.wait_recv()

    res = pl.pallas_call(
        body, name=name, out_shape=[pltpu.HBM(v.shape, v.dtype) for v in thru],
        in_specs=[HBM_SPEC] * (2 * n) + [SEM_SPEC] * (2 * n) + [pl.BlockSpec(memory_space=pl.ANY)],
        out_specs=[HBM_SPEC] * (2 * n), input_output_aliases={i: i for i in range(2 * n)},
        compiler_params=pltpu.CompilerParams(has_side_effects=pltpu.SideEffectType.DATAFLOW_SIDE_EFFECTING))(
            *thru, *sems, after)
    return res[n:]


def _adam(parts, w, m, v, name):
    def body(p_ref, w_ref, m_ref, v_ref, g_ref, d_ref, nm_ref, nv_ref):
        g = _sum_parts(p_ref)
        g_ref[...] = g
        d_ref[...], nm_ref[...], nv_ref[...] = _adam_math(g, w_ref[...], m_ref[...], v_ref[...])

    return pl.pallas_call(
        body, name=name, out_shape=[_sds(w.shape)] * 4,
        compiler_params=pltpu.CompilerParams(vmem_limit_bytes=VMEM_LIMIT))(parts, w, m, v)


GATHER_GROUPS = (("w_in", "w_out"), ("xa_wkv", "xa_wq", "xa_wo"), ("ffn_up", "ffn_conv", "ffn_down"))
SCATTER_GROUPS = (("ffn_down", "ffn_up", "ffn_conv"), ("xa_wo", "xa_wq", "xa_wkv", "w_out"), (), ("w_in",))
BIG = tuple(n for grp in GATHER_GROUPS for n in grp)
BY_COLUMN = ("w_in", "xa_wkv", "ffn_up")
VECS = (("mix_norm", D), ("q_norm", HD), ("k_norm", HD), ("attn_sinks", 8), ("gmlp_v_norm", GW), ("attn_out_norm", AW),
        ("gmlp_out_norm", GW), ("xa_norm", D), ("mem_norm", D), ("xa_q_norm", XD), ("xa_k_norm", XD), ("ffn_norm", D))
BS_ROW = 16
VEC_ROWS = 24
SMALL = tuple(n for n, _ in VECS) + ("gmlp_bs", "gmlp_ws", "ffn_conv_b")


def _pack_small(raw):
    names = [n for n, _ in VECS] + ["gmlp_bs", "conv_sums"]

    def body(*refs):
        ins = dict(zip(names, refs))
        vec_ref, cb_ref = refs[len(names):]
        vec_ref[...] = jnp.zeros_like(vec_ref)
        for r, (n, w) in enumerate(VECS):
            vec_ref[r:r + 1, 0:w] = ins[n][0:1, 0:w]
        vec_ref[BS_ROW:BS_ROW + 8, 0:BLK] = ins["gmlp_bs"][...]
        for s in range(2):
            for d in range(NG):
                cb_ref[s, d] = ins["conv_sums"][s, d, 3:4, :]

    return pl.pallas_call(body, name="pack_small", out_shape=[_sds((VEC_ROWS, D)), _sds((2, NG, 1, SW))])(
        *[raw[n] for n in names])


def _adam_math(g, w, m, v):
    nm = B1 * m + (1.0 - B1) * g
    nv = B2 * v + (1.0 - B2) * (g * g)
    m_hat = nm / (1.0 - B1 ** STEP)
    v_hat = nv / (1.0 - B2 ** STEP)
    return -LR * (m_hat / (jnp.sqrt(v_hat) + AEPS) + WD * w), nm, nv


def _sum_parts(p_ref):
    g = p_ref[0].astype(F32)
    for j in range(1, NDEV):
        g = g + p_ref[j].astype(F32)
    return g


def _adam_small(parts_vec, parts_ws, parts_cb, w, m, v):
    def body(*refs):
        pv_ref, pws_ref, pcb_ref = refs[:3]
        ins = refs[3:3 + 3 * len(SMALL)]
        outs = refs[3 + 3 * len(SMALL):]
        gv = _sum_parts(pv_ref)
        for j, n in enumerate(SMALL):
            w_ref, m_ref, v_ref = ins[3 * j:3 * j + 3]
            o = outs[4 * j:4 * j + 4]
            if n == "gmlp_ws":
                g = _sum_parts(pws_ref)
            elif n == "ffn_conv_b":
                g = _sum_parts(pcb_ref)
            elif n == "gmlp_bs":
                g = gv[BS_ROW:BS_ROW + 8, 0:BLK]
            else:
                g = gv[j:j + 1, 0:VECS[j][1]]
            lead = n in ("gmlp_ws", "gmlp_bs")
            res = (g,) + _adam_math(g, w_ref[0] if lead else w_ref[...], m_ref[0] if lead else m_ref[...],
                                    v_ref[0] if lead else v_ref[...])
            for o_ref, val in zip(o, res):
                if lead:
                    o_ref[0] = val
                else:
                    o_ref[...] = val

    args = [parts_vec, parts_ws, parts_cb] + [d[n] for n in SMALL for d in (w, m, v)]
    res = pl.pallas_call(body, name="adam_small", out_shape=[_sds(w[n].shape) for n in SMALL for _ in range(4)],
                         compiler_params=pltpu.CompilerParams(vmem_limit_bytes=VMEM_LIMIT))(*args)
    return {n: tuple(res[4 * j:4 * j + 4]) for j, n in enumerate(SMALL)}


def kernel(x, mem, positions, mix_norm, w_in, q_norm, k_norm, attn_sinks, gmlp_v_norm, gmlp_ws, gmlp_bs, attn_out_norm, gmlp_out_norm, w_out, xa_norm, mem_norm, xa_wq, xa_wkv, xa_q_norm, xa_k_norm, xa_wo, ffn_norm, ffn_up, ffn_conv, ffn_conv_b, ffn_down, loss_target, m_mix_norm, m_w_in, m_q_norm, m_k_norm, m_attn_sinks, m_gmlp_v_norm, m_gmlp_ws, m_gmlp_bs, m_attn_out_norm, m_gmlp_out_norm, m_w_out, m_xa_norm, m_mem_norm, m_xa_wq, m_xa_wkv, m_xa_q_norm, m_xa_k_norm, m_xa_wo, m_ffn_norm, m_ffn_up, m_ffn_conv, m_ffn_conv_b, m_ffn_down, v_mix_norm, v_w_in, v_q_norm, v_k_norm, v_attn_sinks, v_gmlp_v_norm, v_gmlp_ws, v_gmlp_bs, v_attn_out_norm, v_gmlp_out_norm, v_w_out, v_xa_norm, v_mem_norm, v_xa_wq, v_xa_wkv, v_xa_q_norm, v_xa_k_norm, v_xa_wo, v_ffn_norm, v_ffn_up, v_ffn_conv, v_ffn_conv_b, v_ffn_down):
    names = ("mix_norm", "w_in", "q_norm", "k_norm", "attn_sinks", "gmlp_v_norm", "gmlp_ws", "gmlp_bs", "attn_out_norm",
             "gmlp_out_norm", "w_out", "xa_norm", "mem_norm", "xa_wq", "xa_wkv", "xa_q_norm", "xa_k_norm", "xa_wo",
             "ffn_norm", "ffn_up", "ffn_conv", "ffn_conv_b", "ffn_down")
    w = dict(zip(names, (mix_norm, w_in, q_norm, k_norm, attn_sinks, gmlp_v_norm, gmlp_ws, gmlp_bs, attn_out_norm,
                         gmlp_out_norm, w_out, xa_norm, mem_norm, xa_wq, xa_wkv, xa_q_norm, xa_k_norm, xa_wo, ffn_norm,
                         ffn_up, ffn_conv, ffn_conv_b, ffn_down)))
    m = dict(zip(names, (m_mix_norm, m_w_in, m_q_norm, m_k_norm, m_attn_sinks, m_gmlp_v_norm, m_gmlp_ws, m_gmlp_bs,
                         m_attn_out_norm, m_gmlp_out_norm, m_w_out, m_xa_norm, m_mem_norm, m_xa_wq, m_xa_wkv,
                         m_xa_q_norm, m_xa_k_norm, m_xa_wo, m_ffn_norm, m_ffn_up, m_ffn_conv, m_ffn_conv_b, m_ffn_down)))
    v = dict(zip(names, (v_mix_norm, v_w_in, v_q_norm, v_k_norm, v_attn_sinks, v_gmlp_v_norm, v_gmlp_ws, v_gmlp_bs,
                         v_attn_out_norm, v_gmlp_out_norm, v_w_out, v_xa_norm, v_mem_norm, v_xa_wq, v_xa_wkv,
                         v_xa_q_norm, v_xa_k_norm, v_xa_wo, v_ffn_norm, v_ffn_up, v_ffn_conv, v_ffn_conv_b, v_ffn_down)))
    t = x.shape[1]

    me = 4 * lax.axis_index("x") + 2 * lax.axis_index("y") + lax.axis_index("c")

    def rows(a, n):
        return jnp.swapaxes(a[0], 0, 1) if n in BY_COLUMN else a[0]

    mats = [n for n in BIG if n != "ffn_conv"]
    shard = dict(zip(mats, _cast_shards([rows(w[n], n) for n in mats])), ffn_conv=w["ffn_conv"][0])
    gathers, tokens = zip(*[_exchange_start([], [shard[n] for n in grp], me, "gather_start_%d" % i)
                            for i, grp in enumerate(GATHER_GROUPS)])

    def fetch(i, after):
        after = tokens[0] + tokens[1] + tokens[2] if after is None else after
        got = dict(zip(GATHER_GROUPS[i], _exchange_wait(gathers[i], after, "gather_wait_%d" % i)))
        if "w_in" in got:
            got["w_in"] = got["w_in"].reshape(IN, D)
        for n in ("w_out", "xa_wq", "xa_wo"):
            if n in got:
                got[n] = got[n].reshape(D, D)
        if "ffn_down" in got:
            got["ffn_down"] = got["ffn_down"].reshape(NG, SW, D)
            got["ffn_conv"] = got["ffn_conv"].reshape(2, NG, 3, SW)
        return got

    scatters = []

    def ship(i, grads, for_all=()):
        by_dest = [grads[n].reshape((NDEV,) + shard[n].shape) for n in SCATTER_GROUPS[i]]
        state, token = _exchange_start(by_dest, for_all, me, "scatter_start_%d" % i)
        scatters.append(state)
        return token

    conv_b = {k: d["ffn_conv_b"].reshape(NDEV, 1, SW) for k, d in (("w", w), ("m", m), ("v", v))}
    p = {n: w[n] for n in SMALL[:-1]}
    p["gmlp_ws"], p["gmlp_bs"] = w["gmlp_ws"][0], w["gmlp_bs"][0]
    p["ffn_conv_b"] = conv_b["w"].reshape(2, NG, 1, SW)
    loss, grad_x, g, raw = _local_step(x[0], mem[0], positions.reshape(t, 1), loss_target[0], p, fetch, ship)
    loss = lax.psum(loss, AXES)

    vec, cb = _pack_small(raw)
    token = ship(3, g, [vec, cb.reshape(NDEV, 1, SW)])
    parts, rest = {}, []
    for i, grp in enumerate(SCATTER_GROUPS):
        got = _exchange_wait(scatters[i], token, "scatter_wait_%d" % i)
        parts.update(zip(grp, got))
        rest += got[len(grp):]
    ws_parts, vec_parts, cb_parts = rest
    res = {}
    for n in BIG:
        out = _adam(parts[n], rows(w[n], n), rows(m[n], n), rows(v[n], n), "adam_" + n)
        res[n] = [jnp.swapaxes(o, 0, 1) if n in BY_COLUMN else o for o in out]
    small = lambda d, k: {**{n: d[n] for n in SMALL[:-1]}, "ffn_conv_b": conv_b[k]}
    res.update(_adam_small(vec_parts, ws_parts, cb_parts, small(w, "w"), small(m, "m"), small(v, "v")))

    outs = [loss, grad_x[None]]
    for j in range(4):
        outs += [res[n][j].reshape(w[n].shape) for n in names]
    return tuple(outs)
```

```python
import functools
import math

import jax
import jax.numpy as jnp
from jax import lax
from jax.experimental import pallas as pl
from jax.experimental.pallas import tpu as pltpu

F32 = jnp.float32
BF16 = jnp.bfloat16

D = 1024
HD = 64
AW = 512
KW = 128
GW = 512
IN = AW + 2 * KW + 2 * GW
BLK = 128
MEM = 256
XH = 4
XD = 256
FF = 2816
EPS = 1e-6
ROPE_THETA = 10000.0
NDEV = 8
LR, B1, B2, AEPS, WD, STEP = 0.001, 0.9, 0.999, 1e-08, 0.01, 10

TM = 512
WK = 2048
VMEM_LIMIT = 56 * 1024 * 1024
NEG = float(jnp.finfo(jnp.float32).min)
GELU_C0 = math.sqrt(2.0 / math.pi)
GELU_C1 = 0.044715
AXES = ("x", "y", "c")


def _dot(a, b):
    return jnp.dot(a, b, preferred_element_type=F32)


def _dot_nt(a, b):
    return lax.dot_general(a, b, (((1,), (1,)), ((), ())), preferred_element_type=F32)


def _dot_tn(a, b):
    return lax.dot_general(a, b, (((0,), (0,)), ((), ())), preferred_element_type=F32)


def _rs(x):
    return lax.rsqrt(jnp.mean(x * x, axis=-1, keepdims=True) + EPS)


def _rms_bwd(dy, x, r, g):
    xh = x * r
    dxh = dy * g
    dx = r * (dxh - xh * jnp.mean(dxh * xh, axis=-1, keepdims=True))
    return dx, dy * xh


def _lane(shape):
    return lax.broadcasted_iota(jnp.int32, shape, len(shape) - 1)


def _gsum64(v, ones_ref):
    w = v.shape[-1]
    ones = ones_ref[0:w, 0:w]
    hi = v.astype(BF16)
    lo = (v - hi.astype(F32)).astype(BF16)
    return _dot(hi, ones) + _dot(lo, ones)


def _head_ones():
    i = jnp.arange(AW) // HD
    return (i[:, None] == i[None, :]).astype(BF16)


def _rs64(x, ones_ref):
    return lax.rsqrt(_gsum64(x * x, ones_ref) * (1.0 / HD) + EPS)


def _rms64_bwd(dy, x, r, g, ones_ref):
    xh = x * r
    dxh = dy * g
    dx = r * (dxh - xh * (_gsum64(dxh * xh, ones_ref) * (1.0 / HD)))
    return dx, dy * xh


def _rot_half(v):
    w = v.shape[-1]
    return jnp.where((_lane(v.shape) & 32) == 0, pltpu.roll(v, w - 32, 1), pltpu.roll(v, 32, 1))


def _rope(v, cos, sin_signed):
    return v * cos + _rot_half(v) * sin_signed


def _rope_bwd(dv, cos, sin_signed):
    return dv * cos + _rot_half(dv * sin_signed)


def _gelu(z):
    return 0.5 * z * (1.0 + jnp.tanh(GELU_C0 * (z + GELU_C1 * z * z * z)))


def _gelu_grad(z):
    t = jnp.tanh(GELU_C0 * (z + GELU_C1 * z * z * z))
    return 0.5 * (1.0 + t) + 0.5 * z * (1.0 - t * t) * (GELU_C0 * (1.0 + 3.0 * GELU_C1 * z * z))


def _colsum8(v):
    s = jnp.sum(v, axis=0, keepdims=True)
    row = lax.broadcasted_iota(jnp.int32, (8, v.shape[1]), 0)
    return jnp.where(row == 0, jnp.broadcast_to(s, (8, v.shape[1])), 0.0)


def _params(n_axes=1):
    return pltpu.CompilerParams(dimension_semantics=("arbitrary",) * n_axes, vmem_limit_bytes=VMEM_LIMIT)


def _rows(tm, w):
    return pl.BlockSpec((tm, w), lambda i: (i, 0))


def _const(shape):
    nd = len(shape)
    return pl.BlockSpec(shape, lambda *_: (0,) * nd)


def _sds(shape, dtype=F32):
    return jax.ShapeDtypeStruct(shape, dtype)


def _mm_tn(a, b, name):
    g = max(a.shape[0] if a.ndim == 3 else 1, b.shape[0] if b.ndim == 3 else 1)
    t, m = a.shape[-2:]
    n = b.shape[-1]

    def body(a_ref, b_ref, o_ref, acc_ref):
        i = pl.program_id(1)

        @pl.when(i == 0)
        def _():
            acc_ref[...] = jnp.zeros_like(acc_ref)

        acc_ref[...] += _dot_tn(a_ref[...].astype(BF16), b_ref[...].astype(BF16))

        @pl.when(i == pl.num_programs(1) - 1)
        def _():
            o_ref[...] = acc_ref[...].astype(BF16)

    tk = min(t, WK)

    def spec(v):
        w = v.shape[-1]
        if v.ndim == 3:
            return pl.BlockSpec((None, tk, w), lambda j, i: (j, i, 0))
        return pl.BlockSpec((tk, w), lambda j, i: (i, 0))

    return pl.pallas_call(
        body, name=name, grid=(g, t // tk), in_specs=[spec(a), spec(b)],
        out_specs=pl.BlockSpec((None, m, n), lambda j, i: (j, 0, 0)), out_shape=_sds((g, m, n), BF16),
        scratch_shapes=[pltpu.VMEM((m, n), F32)], compiler_params=_params(2))(a, b)


def _rope_tables(pos, inv_freq):
    t = pos.shape[0]

    def body(pos_ref, f_ref, cos_ref, sin_ref):
        ang = pos_ref[...].astype(F32) * f_ref[...]
        sign = jnp.where((_lane(ang.shape) & 32) == 0, -1.0, 1.0)
        cos_ref[...] = jnp.cos(ang)
        sin_ref[...] = jnp.sin(ang) * sign

    return pl.pallas_call(
        body, name="rope_tables", grid=(t // TM,),
        in_specs=[_rows(TM, 1), _const((1, 128))], out_specs=[_rows(TM, 128), _rows(TM, 128)],
        out_shape=[_sds((t, 128)), _sds((t, 128))], compiler_params=_params())(pos, inv_freq)


def _mixer_in_fwd(x, mix_norm, w_in, qn, kn, gvw, cos, sin):
    t = x.shape[0]

    def body(x_ref, g_ref, w_ref, qn_ref, kn_ref, gvw_ref, cos_ref, sin_ref, ones_ref,
             h_ref, qk_ref, gz_ref, q_ref, k_ref, v_ref, gu_ref, gvn_ref):
        x = x_ref[...]
        h = (x * _rs(x) * g_ref[...]).astype(BF16)
        h_ref[...] = h
        proj = _dot_nt(h, w_ref[...])
        qk = proj[:, :AW + KW]
        qk_ref[...] = qk
        gz = proj[:, AW + 2 * KW:]
        gz_ref[...] = gz
        cos2, sin2 = cos_ref[...], sin_ref[...]
        q = qk[:, :AW]
        q = q * _rs64(q, ones_ref) * qn_ref[...]
        q_ref[...] = _rope(q, jnp.tile(cos2, (1, 4)), jnp.tile(sin2, (1, 4))).astype(BF16)
        k = qk[:, AW:]
        k = k * _rs64(k, ones_ref) * kn_ref[...]
        k_ref[...] = _rope(k, cos2, sin2).astype(BF16)
        v_ref[...] = proj[:, AW + KW:AW + 2 * KW].astype(BF16)
        gu_ref[...] = _gelu(gz[:, :GW])
        gv = _gelu(gz[:, GW:])
        gvn_ref[...] = (gv * _rs(gv) * gvw_ref[...]).astype(BF16)

    return pl.pallas_call(
        body, name="mixer_in_fwd", grid=(t // TM,),
        in_specs=[_rows(TM, D), _const((1, D)), _const((IN, D)), _const((1, AW)), _const((1, KW)),
                  _const((1, GW)), _rows(TM, 128), _rows(TM, 128), _const((AW, AW))],
        out_specs=[_rows(TM, D), _rows(TM, AW + KW), _rows(TM, 2 * GW), _rows(TM, AW), _rows(TM, KW),
                   _rows(TM, KW), _rows(TM, GW), _rows(TM, GW)],
        out_shape=[_sds((t, D), BF16), _sds((t, AW + KW)), _sds((t, 2 * GW)), _sds((t, AW), BF16),
                   _sds((t, KW), BF16), _sds((t, KW), BF16), _sds((t, GW)), _sds((t, GW), BF16)],
        compiler_params=_params())(x, mix_norm, w_in, qn, kn, gvw, cos, sin, _head_ones())


def _dup_half(kk, g):
    lane = _lane(kk.shape)
    other = pltpu.roll(kk, 64, 1)
    keep = (lane < 64) if g == 0 else (lane >= 64)
    return jnp.where(keep, kk, other).astype(BF16)


def _swa_mask(first_block):
    qi = lax.broadcasted_iota(jnp.int32, (4 * BLK, 2 * BLK), 0) & (BLK - 1)
    kj = lax.broadcasted_iota(jnp.int32, (4 * BLK, 2 * BLK), 1)
    diff = qi + BLK - kj
    band = (diff >= 0) & (diff < BLK)
    return band & (jnp.logical_not(first_block) | (kj >= BLK))


def _stack_heads(a2, b2):
    lo = _lane(a2.shape) < 64
    z = jnp.zeros_like(a2)
    return jnp.concatenate([jnp.where(lo, a2, z), jnp.where(lo, z, a2), jnp.where(lo, b2, z), jnp.where(lo, z, b2)], axis=0)


def _unstack_heads(o):
    lo = _lane((BLK, 128)) < 64
    return jnp.where(lo, o[0:BLK], o[BLK:2 * BLK]), jnp.where(lo, o[2 * BLK:3 * BLK], o[3 * BLK:4 * BLK])


def _sink_col(sink_ref, g):
    row = lax.broadcasted_iota(jnp.int32, (4 * BLK, 1), 0)
    s = [sink_ref[0, 4 * g + j] for j in range(4)]
    return jnp.where(row < BLK, s[0], jnp.where(row < 2 * BLK, s[1], jnp.where(row < 3 * BLK, s[2], s[3])))


def _swa_probs(qs, kd, mask, sink):
    s = _dot_nt(qs, kd) * (1.0 / math.sqrt(HD))
    s = jnp.where(mask, s, NEG)
    m = jnp.maximum(jnp.max(s, axis=-1, keepdims=True), sink)
    p = jnp.exp(s - m)
    ps = jnp.exp(sink - m)
    inv = 1.0 / (jnp.sum(p, axis=-1, keepdims=True) + ps)
    return p * inv, ps * inv


def _swa_fwd(q, k, v, sinks):
    t = q.shape[0]
    nb = t // BLK

    def body(sink_ref, q_ref, kc_ref, kp_ref, vc_ref, vp_ref, o_ref):
        i = pl.program_id(0)
        mask = _swa_mask(i == 0)
        kk = jnp.concatenate([kp_ref[...], kc_ref[...]], axis=0).astype(F32)
        vv = jnp.concatenate([vp_ref[...], vc_ref[...]], axis=0).astype(F32)
        for g in range(2):
            qs = _stack_heads(q_ref[:, 256 * g:256 * g + 128], q_ref[:, 256 * g + 128:256 * g + 256])
            pn, _ = _swa_probs(qs, _dup_half(kk, g), mask, _sink_col(sink_ref, g))
            oa, ob = _unstack_heads(_dot(pn.astype(BF16), _dup_half(vv, g)))
            o_ref[:, 256 * g:256 * g + 128] = oa
            o_ref[:, 256 * g + 128:256 * g + 256] = ob

    cur = lambda i: (i, 0)
    prev = lambda i: (jnp.maximum(i - 1, 0), 0)
    return pl.pallas_call(
        body, name="swa_fwd", grid=(nb,),
        in_specs=[pl.BlockSpec(memory_space=pltpu.SMEM), pl.BlockSpec((BLK, AW), cur),
                  pl.BlockSpec((BLK, KW), cur), pl.BlockSpec((BLK, KW), prev),
                  pl.BlockSpec((BLK, KW), cur), pl.BlockSpec((BLK, KW), prev)],
        out_specs=pl.BlockSpec((BLK, AW), cur), out_shape=_sds((t, AW)),
        compiler_params=_params())(sinks, q, k, k, v, v)


def _causal_bf16(w_ref, h, transposed):
    r = lax.broadcasted_iota(jnp.int32, (BLK, BLK), 0)
    c = lax.broadcasted_iota(jnp.int32, (BLK, BLK), 1)
    keep = (r <= c) if transposed else (c <= r)
    return jnp.where(keep, w_ref[h], 0.0).astype(BF16)


def _gmlp_mix(w_ref, xin, transposed):
    lo = _lane((BLK, 128)) < 64
    wm = [_causal_bf16(w_ref, h, transposed) for h in range(8)]
    rows = []
    for c in range(xin.shape[0] // BLK):
        cols = []
        for j in range(4):
            xs = xin[c * BLK:(c + 1) * BLK, 128 * j:128 * (j + 1)]
            cols.append(jnp.where(lo, _dot(wm[2 * j], xs), _dot(wm[2 * j + 1], xs)))
        rows.append(jnp.concatenate(cols, axis=1))
    return jnp.concatenate(rows, axis=0)


def _gmlp_fwd(gvn, gu, ws, bfull):
    t = gvn.shape[0]

    def body(x_ref, gu_ref, w_ref, b_ref, o_ref):
        mixed = _gmlp_mix(w_ref, x_ref[...], False) + jnp.tile(b_ref[...], (TM // BLK, 1))
        o_ref[...] = gu_ref[...] * mixed

    return pl.pallas_call(
        body, name="gmlp_fwd", grid=(t // TM,),
        in_specs=[_rows(TM, GW), _rows(TM, GW), _const((8, BLK, BLK)), _const((BLK, GW))],
        out_specs=_rows(TM, GW), out_shape=_sds((t, GW)), compiler_params=_params())(gvn, gu, ws, bfull)


def _mixer_out_fwd(attn, gm, x, w_out, aon, gon, xan):
    t = x.shape[0]

    def body(a_ref, g_ref, x_ref, w_ref, aon_ref, gon_ref, xan_ref, y_ref, x1_ref, h2_ref):
        a, g = a_ref[...], g_ref[...]
        y = jnp.concatenate([a * _rs(a) * aon_ref[...], g * _rs(g) * gon_ref[...]], axis=1).astype(BF16)
        y_ref[...] = y
        x1 = x_ref[...] + _dot(y, w_ref[...])
        x1_ref[...] = x1
        h2_ref[...] = (x1 * _rs(x1) * xan_ref[...]).astype(BF16)

    return pl.pallas_call(
        body, name="mixer_out_fwd", grid=(t // TM,),
        in_specs=[_rows(TM, AW), _rows(TM, GW), _rows(TM, D), _const((D, D)), _const((1, AW)), _const((1, GW)),
                  _const((1, D))],
        out_specs=[_rows(TM, D), _rows(TM, D), _rows(TM, D)],
        out_shape=[_sds((t, D), BF16), _sds((t, D)), _sds((t, D), BF16)],
        compiler_params=_params())(attn, gm, x, w_out, aon, gon, xan)


def _mem_kv_fwd(mem, mem_norm, wkv, kn4):
    def body(m_ref, g_ref, w_ref, kn_ref, mh_ref, kpre_ref, k_ref, v_ref):
        m = m_ref[...]
        mh = (m * _rs(m) * g_ref[...]).astype(BF16)
        mh_ref[...] = mh
        for h in range(XH):
            sl = slice(XD * h, XD * (h + 1))
            kh = _dot_nt(mh, w_ref[h])
            kpre_ref[:, sl] = kh
            k_ref[:, sl] = (kh * _rs(kh) * kn_ref[:, sl]).astype(BF16)
            v_ref[:, sl] = _dot_nt(mh, w_ref[XH + h]).astype(BF16)

    return pl.pallas_call(
        body, name="mem_kv_fwd",
        out_shape=[_sds((MEM, D), BF16), _sds((MEM, D)), _sds((MEM, D), BF16), _sds((MEM, D), BF16)],
        compiler_params=pltpu.CompilerParams(vmem_limit_bytes=VMEM_LIMIT))(mem, mem_norm, wkv, kn4)


def _xattn_probs(qpre_h, qn_h, k_h):
    rq = _rs(qpre_h)
    q2 = (qpre_h * rq * qn_h).astype(BF16)
    s = _dot_nt(q2, k_h) * (1.0 / math.sqrt(XD))
    p = jnp.exp(s - jnp.max(s, axis=-1, keepdims=True))
    return p * (1.0 / jnp.sum(p, axis=-1, keepdims=True)), q2, rq


def _xattn_fwd(h2, x1, wq, qn4, k2, v2, wo, ffn_norm):
    t = x1.shape[0]

    def body(h_ref, x_ref, wq_ref, qn_ref, k_ref, v_ref, wo_ref, fn_ref, qpre_ref, o_ref, x2_ref, h3_ref):
        qpre = _dot(h_ref[...], wq_ref[...])
        qpre_ref[...] = qpre
        outs = []
        for h in range(XH):
            sl = slice(XD * h, XD * (h + 1))
            pn, _, _ = _xattn_probs(qpre[:, sl], qn_ref[:, sl], k_ref[:, sl])
            outs.append(_dot(pn.astype(BF16), v_ref[:, sl]))
        o = jnp.concatenate(outs, axis=1).astype(BF16)
        o_ref[...] = o
        x2 = x_ref[...] + _dot(o, wo_ref[...])
        x2_ref[...] = x2
        h3_ref[...] = (x2 * _rs(x2) * fn_ref[...]).astype(BF16)

    return pl.pallas_call(
        body, name="xattn_fwd", grid=(t // TM,),
        in_specs=[_rows(TM, D), _rows(TM, D), _const((D, D)), _const((1, D)), _const((MEM, D)), _const((MEM, D)),
                  _const((D, D)), _const((1, D))],
        out_specs=[_rows(TM, D)] * 4,
        out_shape=[_sds((t, D)), _sds((t, D), BF16), _sds((t, D)), _sds((t, D), BF16)],
        compiler_params=_params())(h2, x1, wq, qn4, k2, v2, wo, ffn_norm)


SW = 704
NG = FF // SW
FM = 256
HALO = 16


def _resident(shape):
    nd = len(shape)
    return pl.BlockSpec(shape, lambda *_: (0,) * nd, pipeline_mode=pl.Buffered(1))


def _halo_before(i):
    return jnp.maximum(i * (FM // HALO) - 1, 0)


def _conv(e, w):
    return w[2:3, :] * e + pltpu.roll(w[1:2, :] * e + pltpu.roll(w[0:1, :] * e, 1, 0), 1, 0)


def _conv_t(dc, w):
    n = dc.shape[0]
    return w[2:3, :] * dc + pltpu.roll(w[1:2, :] * dc + pltpu.roll(w[0:1, :] * dc, n - 1, 0), n - 1, 0)


def _ffn_fwd(h3, x2, target, up, conv, conv_b, down):
    t = x2.shape[0]

    def body(h_ref, hp_ref, x_ref, t_ref, up_ref, w_ref, b_ref, dn_ref, a_ref, u_ref, gs_ref, dy_ref, loss_ref, acc_ref):
        i = pl.program_id(0)

        @pl.when(i == 0)
        def _():
            acc_ref[...] = jnp.zeros_like(acc_ref)

        before = jnp.where(i > 0, hp_ref[...], jnp.zeros_like(hp_ref))
        he = jnp.concatenate([before, h_ref[...]], axis=0)
        err = x_ref[...] - t_ref[...]
        for d in range(NG):
            c = []
            for s in range(2):
                a = _dot_nt(he, up_ref[s * NG + d])
                a_ref[s * NG + d] = a[HALO:].astype(BF16)
                c.append(_conv(a, w_ref[s, d])[HALO:] + b_ref[s, d])
            gl, gg = _gelu_and_grad(c[0])
            gs_ref[d] = gl.astype(BF16)
            gs_ref[NG + d] = (gg * c[1]).astype(BF16)
            u = (gl * c[1]).astype(BF16)
            u_ref[d] = u
            err = err + _dot(u, dn_ref[d])
        dy_ref[...] = err * (1.0 / D)
        acc_ref[...] += jnp.sum(err * err, axis=0, keepdims=True)

        @pl.when(i == pl.num_programs(0) - 1)
        def _():
            loss_ref[...] = jnp.full((8, 128), 0.5 / D, F32) * jnp.sum(acc_ref[...])

    return pl.pallas_call(
        body, name="ffn_fwd", grid=(t // FM,),
        in_specs=[_rows(FM, D), pl.BlockSpec((HALO, D), lambda i: (_halo_before(i), 0)), _rows(FM, D), _rows(FM, D),
                  _resident((NDEV, SW, D)), _resident((2, NG, 3, SW)), _resident((2, NG, 1, SW)), _resident((NG, SW, D))],
        out_specs=[pl.BlockSpec((NDEV, FM, SW), lambda i: (0, i, 0)), pl.BlockSpec((NG, FM, SW), lambda i: (0, i, 0)),
                   pl.BlockSpec((NDEV, FM, SW), lambda i: (0, i, 0)), _rows(FM, D), _const((8, 128))],
        out_shape=[_sds((NDEV, t, SW), BF16), _sds((NG, t, SW), BF16), _sds((NDEV, t, SW), BF16), _sds((t, D)),
                   _sds((8, 128))],
        scratch_shapes=[pltpu.VMEM((1, D), F32)], compiler_params=_params())(h3, h3, x2, target, up, conv, conv_b, down)


def _gelu_and_grad(z):
    z2 = z * z
    t = jnp.tanh(GELU_C0 * (z + GELU_C1 * z * z2))
    phi = 0.5 * (1.0 + t)
    return z * phi, phi + 0.5 * z * (1.0 - t * t) * (GELU_C0 * (1.0 + 3.0 * GELU_C1 * z2))


def _ffn_bwd(dy, a, gs, x2, up, conv, down, ffn_norm):
    t = x2.shape[0]
    nt = t // FM
    n = FM + HALO

    def body(dy_ref, dyn_ref, a_ref, gs_ref, gsn_ref, x_ref, up_ref, w_ref, dn_ref, g_ref,
             dx_ref, da_ref, s_ref, dfn_ref):
        i = pl.program_id(0)

        @pl.when(i == 0)
        def _():
            s_ref[...] = jnp.zeros_like(s_ref)
            dfn_ref[...] = jnp.zeros_like(dfn_ref)

        last = i == nt - 1
        dy = dy_ref[...]
        dye = jnp.concatenate([dy, jnp.where(last, 0.0, dyn_ref[...])], axis=0).astype(BF16)
        dh = jnp.zeros((FM, D), F32)
        row = lax.broadcasted_iota(jnp.int32, (8, SW), 0)
        for d in range(NG):
            du = _dot_nt(dye, dn_ref[d])
            for s in range(2):
                j = s * NG + d
                k = NG + d if s == 0 else d
                dc = du * jnp.concatenate([gs_ref[k], gsn_ref[k]], axis=0).astype(F32)
                w = w_ref[s, d]
                tile = a_ref[j].astype(F32)
                d1 = pltpu.roll(dc, n - 1, 0)
                d2 = pltpu.roll(d1, n - 1, 0)
                da = (w[2:3, :] * dc + w[1:2, :] * d1 + w[0:1, :] * d2)[0:FM].astype(BF16)
                da_ref[j] = da
                dh = dh + _dot(da, up_ref[j])
                sums = [jnp.sum(v[0:FM] * tile, axis=0, keepdims=True) for v in (d2, d1, dc)]
                sums.append(jnp.sum(dc[0:FM], axis=0, keepdims=True))
                upd = jnp.zeros((8, SW), F32)
                for r, v in enumerate(sums):
                    upd = jnp.where(row == r, jnp.broadcast_to(v, (8, SW)), upd)
                s_ref[s, d] += upd
        x = x_ref[...]
        dx, dg = _rms_bwd(dh, x, _rs(x), g_ref[...])
        dx_ref[...] = dy + dx
        dfn_ref[...] += _colsum8(dg)

    last_halo = t // HALO - 1
    after = lambda i: jnp.minimum((i + 1) * (FM // HALO), last_halo)
    return pl.pallas_call(
        body, name="ffn_bwd", grid=(nt,),
        in_specs=[_rows(FM, D), pl.BlockSpec((HALO, D), lambda i: (after(i), 0)),
                  pl.BlockSpec((NDEV, FM, SW), lambda i: (0, i, 0)),
                  pl.BlockSpec((NDEV, FM, SW), lambda i: (0, i, 0)),
                  pl.BlockSpec((NDEV, HALO, SW), lambda i: (0, after(i), 0)),
                  _rows(FM, D), _resident((NDEV, SW, D)), _resident((2, NG, 3, SW)), _resident((NG, SW, D)), _const((1, D))],
        out_specs=[_rows(FM, D), pl.BlockSpec((NDEV, FM, SW), lambda i: (0, i, 0)), _const((2, NG, 8, SW)), _const((8, D))],
        out_shape=[_sds((t, D)), _sds((NDEV, t, SW), BF16), _sds((2, NG, 8, SW)), _sds((8, D))],
        compiler_params=_params())(dy, dy, a, gs, gs, x2, up, conv, down, ffn_norm)


BT = 512


def _xattn_bwd(dx2, x1, qpre, k2, v2, wq, wo, qn4, xan):
    t = x1.shape[0]

    def body(dx2_ref, x1_ref, qpre_ref, k_ref, v_ref, wq_ref, wo_ref, qn_ref, xan_ref,
             dx1_ref, dqpre_ref, dk_ref, dv_ref, dqn_ref, dxan_ref):
        @pl.when(pl.program_id(0) == 0)
        def _():
            for r in (dk_ref, dv_ref, dqn_ref, dxan_ref):
                r[...] = jnp.zeros_like(r)

        dx2 = dx2_ref[...]
        do = _dot_nt(dx2.astype(BF16), wo_ref[...])
        dqs = []
        for h in range(XH):
            sl = slice(XD * h, XD * (h + 1))
            qpre_h = qpre_ref[:, sl]
            pn, q2, rq = _xattn_probs(qpre_h, qn_ref[:, sl], k_ref[:, sl])
            do_h = do[:, sl].astype(BF16)
            dp = _dot_nt(do_h, v_ref[:, sl])
            ds = (pn * (dp - jnp.sum(pn * dp, axis=-1, keepdims=True)) * (1.0 / math.sqrt(XD))).astype(BF16)
            dq2 = _dot(ds, k_ref[:, sl])
            dk_ref[:, sl] += _dot_tn(ds, q2)
            dv_ref[:, sl] += _dot_tn(pn.astype(BF16), do_h)
            dqh, dg = _rms_bwd(dq2, qpre_h, rq, qn_ref[:, sl])
            dqn_ref[...] += _colsum8(dg)
            dqs.append(dqh)
        dqpre = jnp.concatenate(dqs, axis=1).astype(BF16)
        dqpre_ref[...] = dqpre
        dh2 = _dot_nt(dqpre, wq_ref[...])
        x1 = x1_ref[...]
        dx, dg = _rms_bwd(dh2, x1, _rs(x1), xan_ref[...])
        dx1_ref[...] = dx2 + dx
        dxan_ref[...] += _colsum8(dg)

    return pl.pallas_call(
        body, name="xattn_bwd", grid=(t // BT,),
        in_specs=[_rows(BT, D), _rows(BT, D), _rows(BT, D), _const((MEM, D)), _const((MEM, D)), _const((D, D)),
                  _const((D, D)), _const((1, D)), _const((1, D))],
        out_specs=[_rows(BT, D), _rows(BT, D), _const((MEM, D)), _const((MEM, D)), _const((8, XD)), _const((8, D))],
        out_shape=[_sds((t, D)), _sds((t, D), BF16), _sds((MEM, D)), _sds((MEM, D)), _sds((8, XD)), _sds((8, D))],
        compiler_params=_params())(dx2, x1, qpre, k2, v2, wq, wo, qn4, xan)


def _mem_kv_bwd(mem, mh, kpre, dk2, dv2, wkv, kn4, mem_norm):
    def body(m_ref, mh_ref, kpre_ref, dk_ref, dv_ref, w_ref, kn_ref, g_ref, dw_ref, dkn_ref, dmn_ref):
        dkn = jnp.zeros((8, XD), F32)
        dm = jnp.zeros((MEM, D), F32)
        mh = mh_ref[...]
        for h in range(XH):
            sl = slice(XD * h, XD * (h + 1))
            kh = kpre_ref[:, sl]
            dkh, dg = _rms_bwd(dk_ref[:, sl], kh, _rs(kh), kn_ref[:, sl])
            dkn = dkn + _colsum8(dg)
            dkh = dkh.astype(BF16)
            dvh = dv_ref[:, sl].astype(BF16)
            dw_ref[h] = _dot_tn(dkh, mh).astype(BF16)
            dw_ref[XH + h] = _dot_tn(dvh, mh).astype(BF16)
            dm = dm + _dot(dkh, w_ref[h]) + _dot(dvh, w_ref[XH + h])
        dkn_ref[...] = dkn
        m = m_ref[...]
        _, dg = _rms_bwd(dm, m, _rs(m), g_ref[...])
        dmn_ref[...] = _colsum8(dg)

    return pl.pallas_call(
        body, name="mem_kv_bwd", out_shape=[_sds((2 * XH, XD, D), BF16), _sds((8, XD)), _sds((8, D))],
        compiler_params=pltpu.CompilerParams(vmem_limit_bytes=VMEM_LIMIT))(mem, mh, kpre, dk2, dv2, wkv, kn4, mem_norm)


def _mixer_out_bwd(dx1, attn, gm, w_out, aon, gon):
    t = dx1.shape[0]

    def body(dx_ref, a_ref, g_ref, w_ref, aon_ref, gon_ref, da_ref, dg_ref, dan_ref, dgn_ref):
        @pl.when(pl.program_id(0) == 0)
        def _():
            dan_ref[...] = jnp.zeros_like(dan_ref)
            dgn_ref[...] = jnp.zeros_like(dgn_ref)

        dy = _dot_nt(dx_ref[...].astype(BF16), w_ref[...])
        a, g = a_ref[...], g_ref[...]
        da, dna = _rms_bwd(dy[:, :AW], a, _rs(a), aon_ref[...])
        dg, dng = _rms_bwd(dy[:, AW:], g, _rs(g), gon_ref[...])
        da_ref[...] = da
        dg_ref[...] = dg
        dan_ref[...] += _colsum8(dna)
        dgn_ref[...] += _colsum8(dng)

    return pl.pallas_call(
        body, name="mixer_out_bwd", grid=(t // TM,),
        in_specs=[_rows(TM, D), _rows(TM, AW), _rows(TM, GW), _const((D, D)), _const((1, AW)), _const((1, GW))],
        out_specs=[_rows(TM, AW), _rows(TM, GW), _const((8, AW)), _const((8, GW))],
        out_shape=[_sds((t, AW)), _sds((t, GW)), _sds((8, AW)), _sds((8, GW))],
        compiler_params=_params())(dx1, attn, gm, w_out, aon, gon)


def _gmlp_bwd(dgm, gu, gvn, gz, ws, wst, bfull, gvw):
    t = dgm.shape[0]
    nc = TM // BLK

    def body(dgm_ref, gu_ref, x_ref, gz_ref, w_ref, wt_ref, b_ref, gvw_ref, dgz_ref, dw_ref, db_ref, dgvw_ref,
             dbacc_ref):
        @pl.when(pl.program_id(0) == 0)
        def _():
            for r in (dw_ref, dbacc_ref, dgvw_ref):
                r[...] = jnp.zeros_like(r)

        xin = x_ref[...]
        dgm = dgm_ref[...]
        mixed = _gmlp_mix(w_ref, xin, False) + jnp.tile(b_ref[...], (nc, 1))
        dgu = dgm * mixed
        dmixed = dgm * gu_ref[...]
        lo = _lane((BLK, 128)) < 64
        dbias = jnp.zeros((BLK, GW), F32)
        for c in range(nc):
            dmc = dmixed[c * BLK:(c + 1) * BLK]
            dbias = dbias + dmc
            for j in range(4):
                dm2 = dmc[:, 128 * j:128 * (j + 1)]
                xs = xin[c * BLK:(c + 1) * BLK, 128 * j:128 * (j + 1)]
                z = jnp.zeros_like(dm2)
                dw_ref[2 * j] += _dot_nt(jnp.where(lo, dm2, z).astype(BF16), xs)
                dw_ref[2 * j + 1] += _dot_nt(jnp.where(lo, z, dm2).astype(BF16), xs)
        dbacc_ref[...] += dbias
        dgvn = _gmlp_mix(wt_ref, dmixed.astype(BF16), True)
        gz_u, gz_v = gz_ref[:, :GW], gz_ref[:, GW:]
        gv = _gelu(gz_v)
        dgv, dg = _rms_bwd(dgvn, gv, _rs(gv), gvw_ref[...])
        dgvw_ref[...] += _colsum8(dg)
        dgz_ref[:, :GW] = (dgu * _gelu_grad(gz_u)).astype(BF16)
        dgz_ref[:, GW:] = (dgv * _gelu_grad(gz_v)).astype(BF16)

        @pl.when(pl.program_id(0) == pl.num_programs(0) - 1)
        def _():
            s = dbacc_ref[...]
            sel = (lax.broadcasted_iota(jnp.int32, (8, GW), 1) // HD
                   == lax.broadcasted_iota(jnp.int32, (8, GW), 0)).astype(BF16)
            hi = s.astype(BF16)
            r1 = s - hi.astype(F32)
            mid = r1.astype(BF16)
            lo = (r1 - mid.astype(F32)).astype(BF16)
            db_ref[...] = _dot_nt(sel, hi) + _dot_nt(sel, mid) + _dot_nt(sel, lo)
            r = lax.broadcasted_iota(jnp.int32, (BLK, BLK), 0)
            c = lax.broadcasted_iota(jnp.int32, (BLK, BLK), 1)
            for h in range(8):
                dw_ref[h] = jnp.where(c <= r, dw_ref[h], 0.0)

    return pl.pallas_call(
        body, name="gmlp_bwd", grid=(t // TM,),
        in_specs=[_rows(TM, GW), _rows(TM, GW), _rows(TM, GW), _rows(TM, 2 * GW), _const((8, BLK, BLK)),
                  _const((8, BLK, BLK)), _const((BLK, GW)), _const((1, GW))],
        out_specs=[_rows(TM, 2 * GW), _const((8, BLK, BLK)), _const((8, BLK)), _const((8, GW))],
        out_shape=[_sds((t, 2 * GW), BF16), _sds((8, BLK, BLK)), _sds((8, BLK)), _sds((8, GW))],
        scratch_shapes=[pltpu.VMEM((BLK, GW), F32)],
        compiler_params=_params())(dgm, gu, gvn, gz, ws, wst, bfull, gvw)


def _fold_half(v):
    return v + pltpu.roll(v, 64, 1)


def _swa_bwd(q, k, v, dattn, sinks):
    t = q.shape[0]
    nb = t // BLK

    def body(sink_ref, q_ref, kc_ref, kp_ref, vc_ref, vp_ref, do_ref, dq_ref, dk_ref, dv_ref, ds_ref,
             ck_ref, cv_ref, sacc_ref):
        i = pl.program_id(0)

        @pl.when(i == 0)
        def _():
            ck_ref[...] = jnp.zeros_like(ck_ref)
            cv_ref[...] = jnp.zeros_like(cv_ref)
            sacc_ref[...] = jnp.zeros_like(sacc_ref)

        @pl.when(i < nb)
        def _():
            mask = _swa_mask(i == 0)
            kk = jnp.concatenate([kp_ref[...], kc_ref[...]], axis=0).astype(F32)
            vv = jnp.concatenate([vp_ref[...], vc_ref[...]], axis=0).astype(F32)
            lo256 = _lane((2 * BLK, 128)) < 64
            dkk = jnp.zeros((2 * BLK, 128), F32)
            dvv = jnp.zeros((2 * BLK, 128), F32)
            for g in range(2):
                qs = _stack_heads(q_ref[:, 256 * g:256 * g + 128], q_ref[:, 256 * g + 128:256 * g + 256])
                dos = _stack_heads(do_ref[:, 256 * g:256 * g + 128],
                                   do_ref[:, 256 * g + 128:256 * g + 256]).astype(BF16)
                kd = _dup_half(kk, g)
                pn, psn = _swa_probs(qs, kd, mask, _sink_col(sink_ref, g))
                dp = _dot_nt(dos, _dup_half(vv, g))
                dd = jnp.sum(pn * dp, axis=-1, keepdims=True)
                ds = (pn * (dp - dd) * (1.0 / math.sqrt(HD))).astype(BF16)
                sacc_ref[g] += jnp.broadcast_to(-psn * dd, (4 * BLK, 128))
                dqa, dqb = _unstack_heads(_dot(ds, kd))
                dq_ref[:, 256 * g:256 * g + 128] = dqa
                dq_ref[:, 256 * g + 128:256 * g + 256] = dqb
                dkg = _fold_half(_dot_tn(ds, qs))
                dvg = _fold_half(_dot_tn(pn.astype(BF16), dos))
                keep = lo256 if g == 0 else jnp.logical_not(lo256)
                dkk = jnp.where(keep, dkg, dkk)
                dvv = jnp.where(keep, dvg, dvv)
            dk_ref[...] = ck_ref[...] + dkk[0:BLK]
            dv_ref[...] = cv_ref[...] + dvv[0:BLK]
            ck_ref[...] = dkk[BLK:]
            cv_ref[...] = dvv[BLK:]

        @pl.when(i == nb)
        def _():
            dk_ref[...] = ck_ref[...]
            dv_ref[...] = cv_ref[...]
            lane = _lane((8, 128))
            acc = jnp.zeros((8, 128), F32)
            for g in range(2):
                for j in range(4):
                    val = jnp.sum(sacc_ref[g, j * BLK:(j + 1) * BLK, :], axis=0, keepdims=True)
                    acc = jnp.where(lane == 4 * g + j, jnp.broadcast_to(val, (8, 128)), acc)
            ds_ref[...] = acc

    cur = lambda i: (jnp.minimum(i, nb - 1), 0)
    prev = lambda i: (jnp.clip(i - 1, 0, nb - 1), 0)
    return pl.pallas_call(
        body, name="swa_bwd", grid=(nb + 1,),
        in_specs=[pl.BlockSpec(memory_space=pltpu.SMEM), pl.BlockSpec((BLK, AW), cur),
                  pl.BlockSpec((BLK, KW), cur), pl.BlockSpec((BLK, KW), prev),
                  pl.BlockSpec((BLK, KW), cur), pl.BlockSpec((BLK, KW), prev), pl.BlockSpec((BLK, AW), cur)],
        out_specs=[pl.BlockSpec((BLK, AW), cur), pl.BlockSpec((BLK, KW), prev), pl.BlockSpec((BLK, KW), prev),
                   _const((8, 128))],
        out_shape=[_sds((t, AW)), _sds((t, KW)), _sds((t, KW)), _sds((8, 128))],
        scratch_shapes=[pltpu.VMEM((BLK, KW), F32), pltpu.VMEM((BLK, KW), F32), pltpu.VMEM((2, 4 * BLK, 128), F32)],
        compiler_params=_params())(sinks, q, k, k, v, v, dattn)


def _mixer_in_bwd(dq, dk, dv, dgz, qk, cos, sin, x, dx1, w_in, mix_norm, qn, kn):
    t = x.shape[0]

    def body(dq_ref, dk_ref, dv_ref, dgz_ref, qk_ref, cos_ref, sin_ref, x_ref, dx1_ref, w_ref, g_ref, qn_ref, kn_ref,
             ones_ref, gx_ref, dproj_ref, dmn_ref, dqn_ref, dkn_ref, qacc_ref, kacc_ref):
        i = pl.program_id(0)

        @pl.when(i == 0)
        def _():
            dmn_ref[...] = jnp.zeros_like(dmn_ref)
            qacc_ref[...] = jnp.zeros_like(qacc_ref)
            kacc_ref[...] = jnp.zeros_like(kacc_ref)

        cos2, sin2 = cos_ref[...], sin_ref[...]
        qpre, kpre = qk_ref[:, :AW], qk_ref[:, AW:]
        dqh = _rope_bwd(dq_ref[...], jnp.tile(cos2, (1, 4)), jnp.tile(sin2, (1, 4)))
        dqpre, dgq = _rms64_bwd(dqh, qpre, _rs64(qpre, ones_ref), qn_ref[...], ones_ref)
        dkh = _rope_bwd(dk_ref[...], cos2, sin2)
        dkpre, dgk = _rms64_bwd(dkh, kpre, _rs64(kpre, ones_ref), kn_ref[...], ones_ref)
        qacc_ref[...] += jnp.sum(dgq, axis=0, keepdims=True)
        kacc_ref[...] += jnp.sum(dgk, axis=0, keepdims=True)
        dproj = jnp.concatenate([dqpre.astype(BF16), dkpre.astype(BF16), dv_ref[...].astype(BF16), dgz_ref[...]], axis=1)
        dproj_ref[...] = dproj
        dh = _dot(dproj, w_ref[...])
        xv = x_ref[...]
        dx, dg = _rms_bwd(dh, xv, _rs(xv), g_ref[...])
        gx_ref[...] = dx1_ref[...] + dx
        dmn_ref[...] += _colsum8(dg)

        @pl.when(i == pl.num_programs(0) - 1)
        def _():
            qa = qacc_ref[...]
            q4 = qa[:, 0:128] + qa[:, 128:256] + qa[:, 256:384] + qa[:, 384:512]
            dqn_ref[...] = jnp.broadcast_to(_fold_half(q4), (8, 128))
            dkn_ref[...] = jnp.broadcast_to(_fold_half(kacc_ref[...]), (8, 128))

    return pl.pallas_call(
        body, name="mixer_in_bwd", grid=(t // TM,),
        in_specs=[_rows(TM, AW), _rows(TM, KW), _rows(TM, KW), _rows(TM, 2 * GW), _rows(TM, AW + KW), _rows(TM, 128),
                  _rows(TM, 128), _rows(TM, D), _rows(TM, D), _const((IN, D)), _const((1, D)), _const((1, AW)),
                  _const((1, KW)), _const((AW, AW))],
        out_specs=[_rows(TM, D), _rows(TM, IN), _const((8, D)), _const((8, 128)), _const((8, 128))],
        out_shape=[_sds((t, D)), _sds((t, IN), BF16), _sds((8, D)), _sds((8, 128)), _sds((8, 128))],
        scratch_shapes=[pltpu.VMEM((1, AW), F32), pltpu.VMEM((1, KW), F32)],
        compiler_params=_params())(dq, dk, dv, dgz, qk, cos, sin, x, dx1, w_in, mix_norm, qn, kn, _head_ones())


def _local_step(x, mem, pos, target, p, fetch, ship):
    t = x.shape[0]
    p = dict(p)
    p.update(fetch(0, None))
    inv_freq = 1.0 / (ROPE_THETA ** (jnp.arange(HD // 2, dtype=F32) * (2.0 / HD)))
    cos, sin = _rope_tables(pos, jnp.tile(inv_freq, 4).reshape(1, 128))
    qn = jnp.tile(p["q_norm"], (1, AW // HD))
    kn = jnp.tile(p["k_norm"], (1, KW // HD))
    qn4 = jnp.tile(p["xa_q_norm"], (1, XH))
    kn4 = jnp.tile(p["xa_k_norm"], (1, XH))
    ws = p["gmlp_ws"]
    wst = jnp.swapaxes(ws, 1, 2)
    bfull = jnp.repeat(p["gmlp_bs"].T, HD, axis=1)
    conv_b = p["ffn_conv_b"]

    h1, qk, gz, q, k, v, gu, gvn = _mixer_in_fwd(x, p["mix_norm"], p["w_in"], qn, kn, p["gmlp_v_norm"], cos, sin)
    attn = _swa_fwd(q, k, v, p["attn_sinks"])
    gm = _gmlp_fwd(gvn, gu, ws, bfull)
    ycat, x1, h2 = _mixer_out_fwd(attn, gm, x, p["w_out"], p["attn_out_norm"], p["gmlp_out_norm"], p["xa_norm"])
    p.update(fetch(1, h2))
    mh, kpre, k2, v2 = _mem_kv_fwd(mem, p["mem_norm"], p["xa_wkv"], kn4)
    qpre, o, x2, h3 = _xattn_fwd(h2, x1, p["xa_wq"], qn4, k2, v2, p["xa_wo"], p["ffn_norm"])
    p.update(fetch(2, h3))
    conv = p["ffn_conv"]
    a, u, gs, dy, loss8 = _ffn_fwd(h3, x2, target, p["ffn_up"], conv, conv_b, p["ffn_down"])

    raw = {}
    d_down = _mm_tn(u, dy, "ffn_down_bwd_w")
    dx2, da, raw["conv_sums"], raw["ffn_norm"] = _ffn_bwd(dy, a, gs, x2, p["ffn_up"], conv, p["ffn_down"], p["ffn_norm"])
    d_up = _mm_tn(da, h3, "ffn_up_bwd_w")
    token = ship(0, {"ffn_down": d_down, "ffn_up": d_up, "ffn_conv": raw["conv_sums"][:, :, 0:3]})
    dx1, dqpre, dk2, dv2, raw["xa_q_norm"], raw["xa_norm"] = _xattn_bwd(
        dx2, x1, qpre, k2, v2, p["xa_wq"], p["xa_wo"], qn4 + jnp.tile(token[0:1], (1, D // 128)), p["xa_norm"])
    d_wo = _mm_tn(o, dx2, "xa_wo_bwd_w")
    d_wq = _mm_tn(h2, dqpre, "xa_wq_bwd_w")
    d_wkv, raw["xa_k_norm"], raw["mem_norm"] = _mem_kv_bwd(mem, mh, kpre, dk2, dv2, p["xa_wkv"], kn4, p["mem_norm"])
    d_w_out = _mm_tn(ycat, dx1, "w_out_bwd_w")
    token = ship(1, {"xa_wo": d_wo, "xa_wq": d_wq, "xa_wkv": d_wkv, "w_out": d_w_out})
    dattn, dgm, raw["attn_out_norm"], raw["gmlp_out_norm"] = _mixer_out_bwd(
        dx1, attn, gm, p["w_out"], p["attn_out_norm"] + jnp.tile(token[0:1], (1, AW // 128)), p["gmlp_out_norm"])
    dgz, raw["gmlp_ws"], raw["gmlp_bs"], raw["gmlp_v_norm"] = _gmlp_bwd(dgm, gu, gvn, gz, ws, wst, bfull, p["gmlp_v_norm"])
    token = ship(2, {}, [raw["gmlp_ws"]])
    dq, dk, dv, raw["attn_sinks"] = _swa_bwd(q, k, v, dattn, p["attn_sinks"] + token[0:1, 0:8])
    grad_x, dproj, raw["mix_norm"], raw["q_norm"], raw["k_norm"] = _mixer_in_bwd(
        dq, dk, dv, dgz, qk, cos, sin, x, dx1, p["w_in"], p["mix_norm"], qn, kn)
    d_w_in = _mm_tn(dproj, h1, "w_in_bwd_w")
    return loss8[0, 0], grad_x, {"w_in": d_w_in}, raw


def _cast_shards(shards):
    def body(*refs):
        n = len(refs) // 2
        for i_ref, o_ref in zip(refs[:n], refs[n:]):
            o_ref[...] = i_ref[...].astype(BF16)

    return pl.pallas_call(body, name="cast_shards", out_shape=[_sds(s.shape, BF16) for s in shards],
                          compiler_params=pltpu.CompilerParams(vmem_limit_bytes=VMEM_LIMIT))(*shards)


HBM_SPEC = pl.BlockSpec(memory_space=pltpu.HBM)
SEM_SPEC = pl.BlockSpec(memory_space=pltpu.SEMAPHORE)


def _remote_copies(src_refs, land_refs, send_refs, recv_refs, nd):
    x, y, cc = lax.axis_index("x"), lax.axis_index("y"), lax.axis_index("c")
    me = 4 * x + 2 * y + cc
    copies = []
    for a, (src_ref, land_ref) in enumerate(zip(src_refs, land_refs)):
        for k in range(1, NDEV):
            px = 1 - x if k & 4 else x
            py = 1 - y if k & 2 else y
            pc = 1 - cc if k & 1 else cc
            copies.append(pltpu.make_async_remote_copy(
                src_ref=src_ref.at[4 * px + 2 * py + pc] if a < nd else src_ref, dst_ref=land_ref.at[me],
                send_sem=send_refs[a].at[k - 1], recv_sem=recv_refs[a].at[k - 1],
                device_id=(px, py, pc), device_id_type=pl.DeviceIdType.MESH))
    return copies


def _own_slot(src, by_dest, me):
    block = lax.dynamic_index_in_dim(src, me, 0, keepdims=True) if by_dest else src[None]
    return lax.dynamic_update_index_in_dim(lax.empty((NDEV,) + block.shape[1:], src.dtype), block, me, 0)


def _exchange_start(by_dest, for_all, me, name):
    srcs = list(by_dest) + list(for_all)
    n, nd = len(srcs), len(by_dest)
    lands = [_own_slot(s, a < nd, me) for a, s in enumerate(srcs)]

    def body(*refs):
        for cp in _remote_copies(refs[:n], refs[n:2 * n], refs[2 * n:3 * n], refs[3 * n:4 * n], nd):
            cp.start()
        refs[-1][...] = jnp.zeros((8, 128), F32)

    sems = [pltpu.SemaphoreType.DMA((NDEV - 1,))] * (2 * n)
    thru = [pltpu.HBM(v.shape, v.dtype) for v in srcs + lands]
    res = pl.pallas_call(
        body, name=name, out_shape=sems + thru + [_sds((8, 128))],
        in_specs=[HBM_SPEC] * (2 * n), out_specs=[SEM_SPEC] * (2 * n) + [HBM_SPEC] * (2 * n) + [pl.BlockSpec(memory_space=pltpu.VMEM)],
        input_output_aliases={i: 2 * n + i for i in range(2 * n)},
        compiler_params=pltpu.CompilerParams(has_side_effects=pltpu.SideEffectType.DATAFLOW_SIDE_EFFECTING))(
            *[pltpu.with_memory_space_constraint(v, pltpu.HBM) for v in srcs + lands])
    return (res[:2 * n], res[2 * n:4 * n], nd), res[-1]


def _exchange_wait(state, after, name):
    sems, thru, nd = state
    n = len(thru) // 2

    def body(*refs):
        for cp in _remote_copies(refs[:n], refs[n:2 * n], refs[2 * n:3 * n], refs[3 * n:4 * n], nd):
            cp.wait_send()
            cp.wait_recv()

    res = pl.pallas_call(
        body, name=name, out_shape=[pltpu.HBM(v.shape, v.dtype) for v in thru],
        in_specs=[HBM_SPEC] * (2 * n) + [SEM_SPEC] * (2 * n) + [pl.BlockSpec(memory_space=pl.ANY)],
        out_specs=[HBM_SPEC] * (2 * n), input_output_aliases={i: i for i in range(2 * n)},
        compiler_params=pltpu.CompilerParams(has_side_effects=pltpu.SideEffectType.DATAFLOW_SIDE_EFFECTING))(
            *thru, *sems, after)
    return res[n:]


def _adam(parts, w, m, v, name):
    def body(p_ref, w_ref, m_ref, v_ref, g_ref, d_ref, nm_ref, nv_ref):
        g = _sum_parts(p_ref)
        g_ref[...] = g
        d_ref[...], nm_ref[...], nv_ref[...] = _adam_math(g, w_ref[...], m_ref[...], v_ref[...])

    return pl.pallas_call(
        body, name=name, out_shape=[_sds(w.shape)] * 4,
        compiler_params=pltpu.CompilerParams(vmem_limit_bytes=VMEM_LIMIT))(parts, w, m, v)


GATHER_GROUPS = (("w_in", "w_out"), ("xa_wkv", "xa_wq", "xa_wo"), ("ffn_up", "ffn_conv", "ffn_down"))
SCATTER_GROUPS = (("ffn_down", "ffn_up", "ffn_conv"), ("xa_wo", "xa_wq", "xa_wkv", "w_out"), (), ("w_in",))
BIG = tuple(n for grp in GATHER_GROUPS for n in grp)
BY_COLUMN = ("w_in", "xa_wkv", "ffn_up")
VECS = (("mix_norm", D), ("q_norm", HD), ("k_norm", HD), ("attn_sinks", 8), ("gmlp_v_norm", GW), ("attn_out_norm", AW),
        ("gmlp_out_norm", GW), ("xa_norm", D), ("mem_norm", D), ("xa_q_norm", XD), ("xa_k_norm", XD), ("ffn_norm", D))
BS_ROW = 16
VEC_ROWS = 24
SMALL = tuple(n for n, _ in VECS) + ("gmlp_bs", "gmlp_ws", "ffn_conv_b")


def _pack_small(raw):
    names = [n for n, _ in VECS] + ["gmlp_bs", "conv_sums"]

    def body(*refs):
        ins = dict(zip(names, refs))
        vec_ref, cb_ref = refs[len(names):]
        vec_ref[...] = jnp.zeros_like(vec_ref)
        for r, (n, w) in enumerate(VECS):
            vec_ref[r:r + 1, 0:w] = ins[n][0:1, 0:w]
        vec_ref[BS_ROW:BS_ROW + 8, 0:BLK] = ins["gmlp_bs"][...]
        for s in range(2):
            for d in range(NG):
                cb_ref[s, d] = ins["conv_sums"][s, d, 3:4, :]

    return pl.pallas_call(body, name="pack_small", out_shape=[_sds((VEC_ROWS, D)), _sds((2, NG, 1, SW))])(
        *[raw[n] for n in names])


def _adam_math(g, w, m, v):
    nm = B1 * m + (1.0 - B1) * g
    nv = B2 * v + (1.0 - B2) * (g * g)
    m_hat = nm / (1.0 - B1 ** STEP)
    v_hat = nv / (1.0 - B2 ** STEP)
    return -LR * (m_hat / (jnp.sqrt(v_hat) + AEPS) + WD * w), nm, nv


def _sum_parts(p_ref):
    g = p_ref[0].astype(F32)
    for j in range(1, NDEV):
        g = g + p_ref[j].astype(F32)
    return g


def _adam_small(parts_vec, parts_ws, parts_cb, w, m, v):
    def body(*refs):
        pv_ref, pws_ref, pcb_ref = refs[:3]
        ins = refs[3:3 + 3 * len(SMALL)]
        outs = refs[3 + 3 * len(SMALL):]
        gv = _sum_parts(pv_ref)
        for j, n in enumerate(SMALL):
            w_ref, m_ref, v_ref = ins[3 * j:3 * j + 3]
            o = outs[4 * j:4 * j + 4]
            if n == "gmlp_ws":
                g = _sum_parts(pws_ref)
            elif n == "ffn_conv_b":
                g = _sum_parts(pcb_ref)
            elif n == "gmlp_bs":
                g = gv[BS_ROW:BS_ROW + 8, 0:BLK]
            else:
                g = gv[j:j + 1, 0:VECS[j][1]]
            lead = n in ("gmlp_ws", "gmlp_bs")
            res = (g,) + _adam_math(g, w_ref[0] if lead else w_ref[...], m_ref[0] if lead else m_ref[...],
                                    v_ref[0] if lead else v_ref[...])
            for o_ref, val in zip(o, res):
                if lead:
                    o_ref[0] = val
                else:
                    o_ref[...] = val

    args = [parts_vec, parts_ws, parts_cb] + [d[n] for n in SMALL for d in (w, m, v)]
    res = pl.pallas_call(body, name="adam_small", out_shape=[_sds(w[n].shape) for n in SMALL for _ in range(4)],
                         compiler_params=pltpu.CompilerParams(vmem_limit_bytes=VMEM_LIMIT))(*args)
    return {n: tuple(res[4 * j:4 * j + 4]) for j, n in enumerate(SMALL)}


def kernel(x, mem, positions, mix_norm, w_in, q_norm, k_norm, attn_sinks, gmlp_v_norm, gmlp_ws, gmlp_bs, attn_out_norm, gmlp_out_norm, w_out, xa_norm, mem_norm, xa_wq, xa_wkv, xa_q_norm, xa_k_norm, xa_wo, ffn_norm, ffn_up, ffn_conv, ffn_conv_b, ffn_down, loss_target, m_mix_norm, m_w_in, m_q_norm, m_k_norm, m_attn_sinks, m_gmlp_v_norm, m_gmlp_ws, m_gmlp_bs, m_attn_out_norm, m_gmlp_out_norm, m_w_out, m_xa_norm, m_mem_norm, m_xa_wq, m_xa_wkv, m_xa_q_norm, m_xa_k_norm, m_xa_wo, m_ffn_norm, m_ffn_up, m_ffn_conv, m_ffn_conv_b, m_ffn_down, v_mix_norm, v_w_in, v_q_norm, v_k_norm, v_attn_sinks, v_gmlp_v_norm, v_gmlp_ws, v_gmlp_bs, v_attn_out_norm, v_gmlp_out_norm, v_w_out, v_xa_norm, v_mem_norm, v_xa_wq, v_xa_wkv, v_xa_q_norm, v_xa_k_norm, v_xa_wo, v_ffn_norm, v_ffn_up, v_ffn_conv, v_ffn_conv_b, v_ffn_down):
    names = ("mix_norm", "w_in", "q_norm", "k_norm", "attn_sinks", "gmlp_v_norm", "gmlp_ws", "gmlp_bs", "attn_out_norm",
             "gmlp_out_norm", "w_out", "xa_norm", "mem_norm", "xa_wq", "xa_wkv", "xa_q_norm", "xa_k_norm", "xa_wo",
             "ffn_norm", "ffn_up", "ffn_conv", "ffn_conv_b", "ffn_down")
    w = dict(zip(names, (mix_norm, w_in, q_norm, k_norm, attn_sinks, gmlp_v_norm, gmlp_ws, gmlp_bs, attn_out_norm,
                         gmlp_out_norm, w_out, xa_norm, mem_norm, xa_wq, xa_wkv, xa_q_norm, xa_k_norm, xa_wo, ffn_norm,
                         ffn_up, ffn_conv, ffn_conv_b, ffn_down)))
    m = dict(zip(names, (m_mix_norm, m_w_in, m_q_norm, m_k_norm, m_attn_sinks, m_gmlp_v_norm, m_gmlp_ws, m_gmlp_bs,
                         m_attn_out_norm, m_gmlp_out_norm, m_w_out, m_xa_norm, m_mem_norm, m_xa_wq, m_xa_wkv,
                         m_xa_q_norm, m_xa_k_norm, m_xa_wo, m_ffn_norm, m_ffn_up, m_ffn_conv, m_ffn_conv_b, m_ffn_down)))
    v = dict(zip(names, (v_mix_norm, v_w_in, v_q_norm, v_k_norm, v_attn_sinks, v_gmlp_v_norm, v_gmlp_ws, v_gmlp_bs,
                         v_attn_out_norm, v_gmlp_out_norm, v_w_out, v_xa_norm, v_mem_norm, v_xa_wq, v_xa_wkv,
                         v_xa_q_norm, v_xa_k_norm, v_xa_wo, v_ffn_norm, v_ffn_up, v_ffn_conv, v_ffn_conv_b, v_ffn_down)))
    t = x.shape[1]

    me = 4 * lax.axis_index("x") + 2 * lax.axis_index("y") + lax.axis_index("c")

    def rows(a, n):
        return jnp.swapaxes(a[0], 0, 1) if n in BY_COLUMN else a[0]

    mats = [n for n in BIG if n != "ffn_conv"]
    shard = dict(zip(mats, _cast_shards([rows(w[n], n) for n in mats])), ffn_conv=w["ffn_conv"][0])
    gathers, tokens = zip(*[_exchange_start([], [shard[n] for n in grp], me, "gather_start_%d" % i)
                            for i, grp in enumerate(GATHER_GROUPS)])

    def fetch(i, after):
        after = tokens[0] + tokens[1] + tokens[2] if after is None else after
        got = dict(zip(GATHER_GROUPS[i], _exchange_wait(gathers[i], after, "gather_wait_%d" % i)))
        if "w_in" in got:
            got["w_in"] = got["w_in"].reshape(IN, D)
        for n in ("w_out", "xa_wq", "xa_wo"):
            if n in got:
                got[n] = got[n].reshape(D, D)
        if "ffn_down" in got:
            got["ffn_down"] = got["ffn_down"].reshape(NG, SW, D)
            got["ffn_conv"] = got["ffn_conv"].reshape(2, NG, 3, SW)
        return got

    scatters = []

    def ship(i, grads, for_all=()):
        by_dest = [grads[n].reshape((NDEV,) + shard[n].shape) for n in SCATTER_GROUPS[i]]
        state, token = _exchange_start(by_dest, for_all, me, "scatter_start_%d" % i)
        scatters.append(state)
        return token

    conv_b = {k: d["ffn_conv_b"].reshape(NDEV, 1, SW) for k, d in (("w", w), ("m", m), ("v", v))}
    p = {n: w[n] for n in SMALL[:-1]}
    p["gmlp_ws"], p["gmlp_bs"] = w["gmlp_ws"][0], w["gmlp_bs"][0]
    p["ffn_conv_b"] = conv_b["w"].reshape(2, NG, 1, SW)
    loss, grad_x, g, raw = _local_step(x[0], mem[0], positions.reshape(t, 1), loss_target[0], p, fetch, ship)
    loss = lax.psum(loss, AXES)

    vec, cb = _pack_small(raw)
    after = ship(3, g, [vec, cb.reshape(NDEV, 1, SW)])
    res, rest = {}, []
    for i, grp in enumerate(SCATTER_GROUPS):
        got = _exchange_wait(scatters[i], after, "scatter_wait_%d" % i)
        rest += got[len(grp):]
        for n, parts in zip(grp, got):
            out = _adam(parts, rows(w[n], n), rows(m[n], n), rows(v[n], n), "adam_" + n)
            res[n] = [jnp.swapaxes(o, 0, 1) if n in BY_COLUMN else o for o in out]
            after = out[0]
    ws_parts, vec_parts, cb_parts = rest
    small = lambda d, k: {**{n: d[n] for n in SMALL[:-1]}, "ffn_conv_b": conv_b[k]}
    res.update(_adam_small(vec_parts, ws_parts, cb_parts, small(w, "w"), small(m, "m"), small(v, "v")))

    outs = [loss, grad_x[None]]
    for j in range(4):
        outs += [res[n][j].reshape(w[n].shape) for n in names]
    return tuple(outs)
```

```python
import functools
import math

import jax
import jax.numpy as jnp
from jax import lax
from jax.experimental import pallas as pl
from jax.experimental.pallas import tpu as pltpu

F32 = jnp.float32
BF16 = jnp.bfloat16

D = 1024
HD = 64
AW = 512
KW = 128
GW = 512
IN = AW + 2 * KW + 2 * GW
BLK = 128
MEM = 256
XH = 4
XD = 256
FF = 2816
EPS = 1e-6
ROPE_THETA = 10000.0
NDEV = 8
LR, B1, B2, AEPS, WD, STEP = 0.001, 0.9, 0.999, 1e-08, 0.01, 10

TM = 512
WK = 2048
VMEM_LIMIT = 56 * 1024 * 1024
NEG = float(jnp.finfo(jnp.float32).min)
GELU_C0 = math.sqrt(2.0 / math.pi)
GELU_C1 = 0.044715
AXES = ("x", "y", "c")


def _dot(a, b):
    return jnp.dot(a, b, preferred_element_type=F32)


def _dot_nt(a, b):
    return lax.dot_general(a, b, (((1,), (1,)), ((), ())), preferred_element_type=F32)


def _dot_tn(a, b):
    return lax.dot_general(a, b, (((0,), (0,)), ((), ())), preferred_element_type=F32)


def _rs(x):
    return lax.rsqrt(jnp.mean(x * x, axis=-1, keepdims=True) + EPS)


def _rms_bwd(dy, x, r, g):
    xh = x * r
    dxh = dy * g
    dx = r * (dxh - xh * jnp.mean(dxh * xh, axis=-1, keepdims=True))
    return dx, dy * xh


def _lane(shape):
    return lax.broadcasted_iota(jnp.int32, shape, len(shape) - 1)


def _gsum64(v, ones_ref):
    w = v.shape[-1]
    ones = ones_ref[0:w, 0:w]
    hi = v.astype(BF16)
    lo = (v - hi.astype(F32)).astype(BF16)
    return _dot(hi, ones) + _dot(lo, ones)


def _head_ones():
    i = jnp.arange(AW) // HD
    return (i[:, None] == i[None, :]).astype(BF16)


def _rs64(x, ones_ref):
    return lax.rsqrt(_gsum64(x * x, ones_ref) * (1.0 / HD) + EPS)


def _rms64_bwd(dy, x, r, g, ones_ref):
    xh = x * r
    dxh = dy * g
    dx = r * (dxh - xh * (_gsum64(dxh * xh, ones_ref) * (1.0 / HD)))
    return dx, dy * xh


def _rot_half(v):
    w = v.shape[-1]
    return jnp.where((_lane(v.shape) & 32) == 0, pltpu.roll(v, w - 32, 1), pltpu.roll(v, 32, 1))


def _rope(v, cos, sin_signed):
    return v * cos + _rot_half(v) * sin_signed


def _rope_bwd(dv, cos, sin_signed):
    return dv * cos + _rot_half(dv * sin_signed)


def _gelu(z):
    return 0.5 * z * (1.0 + jnp.tanh(GELU_C0 * (z + GELU_C1 * z * z * z)))


def _gelu_grad(z):
    t = jnp.tanh(GELU_C0 * (z + GELU_C1 * z * z * z))
    return 0.5 * (1.0 + t) + 0.5 * z * (1.0 - t * t) * (GELU_C0 * (1.0 + 3.0 * GELU_C1 * z * z))


def _colsum8(v):
    s = jnp.sum(v, axis=0, keepdims=True)
    row = lax.broadcasted_iota(jnp.int32, (8, v.shape[1]), 0)
    return jnp.where(row == 0, jnp.broadcast_to(s, (8, v.shape[1])), 0.0)


def _params(n_axes=1):
    return pltpu.CompilerParams(dimension_semantics=("arbitrary",) * n_axes, vmem_limit_bytes=VMEM_LIMIT)


def _rows(tm, w):
    return pl.BlockSpec((tm, w), lambda i: (i, 0))


def _const(shape):
    nd = len(shape)
    return pl.BlockSpec(shape, lambda *_: (0,) * nd)


def _sds(shape, dtype=F32):
    return jax.ShapeDtypeStruct(shape, dtype)


def _mm_tn(a, b, name):
    g = max(a.shape[0] if a.ndim == 3 else 1, b.shape[0] if b.ndim == 3 else 1)
    t, m = a.shape[-2:]
    n = b.shape[-1]

    def body(a_ref, b_ref, o_ref, acc_ref):
        i = pl.program_id(1)

        @pl.when(i == 0)
        def _():
            acc_ref[...] = jnp.zeros_like(acc_ref)

        acc_ref[...] += _dot_tn(a_ref[...].astype(BF16), b_ref[...].astype(BF16))

        @pl.when(i == pl.num_programs(1) - 1)
        def _():
            o_ref[...] = acc_ref[...].astype(BF16)

    tk = min(t, WK)

    def spec(v):
        w = v.shape[-1]
        if v.ndim == 3:
            return pl.BlockSpec((None, tk, w), lambda j, i: (j, i, 0))
        return pl.BlockSpec((tk, w), lambda j, i: (i, 0))

    return pl.pallas_call(
        body, name=name, grid=(g, t // tk), in_specs=[spec(a), spec(b)],
        out_specs=pl.BlockSpec((None, m, n), lambda j, i: (j, 0, 0)), out_shape=_sds((g, m, n), BF16),
        scratch_shapes=[pltpu.VMEM((m, n), F32)], compiler_params=_params(2))(a, b)


def _rope_tables(pos, inv_freq):
    t = pos.shape[0]

    def body(pos_ref, f_ref, cos_ref, sin_ref):
        ang = pos_ref[...].astype(F32) * f_ref[...]
        sign = jnp.where((_lane(ang.shape) & 32) == 0, -1.0, 1.0)
        cos_ref[...] = jnp.cos(ang)
        sin_ref[...] = jnp.sin(ang) * sign

    return pl.pallas_call(
        body, name="rope_tables", grid=(t // TM,),
        in_specs=[_rows(TM, 1), _const((1, 128))], out_specs=[_rows(TM, 128), _rows(TM, 128)],
        out_shape=[_sds((t, 128)), _sds((t, 128))], compiler_params=_params())(pos, inv_freq)


def _mixer_in_fwd(x, mix_norm, w_in, qn, kn, gvw, cos, sin):
    t = x.shape[0]

    def body(x_ref, g_ref, w_ref, qn_ref, kn_ref, gvw_ref, cos_ref, sin_ref, ones_ref,
             h_ref, qk_ref, gz_ref, q_ref, k_ref, v_ref, gu_ref, gvn_ref):
        x = x_ref[...]
        h = (x * _rs(x) * g_ref[...]).astype(BF16)
        h_ref[...] = h
        proj = _dot_nt(h, w_ref[...])
        qk = proj[:, :AW + KW]
        qk_ref[...] = qk
        gz = proj[:, AW + 2 * KW:]
        gz_ref[...] = gz
        cos2, sin2 = cos_ref[...], sin_ref[...]
        q = qk[:, :AW]
        q = q * _rs64(q, ones_ref) * qn_ref[...]
        q_ref[...] = _rope(q, jnp.tile(cos2, (1, 4)), jnp.tile(sin2, (1, 4))).astype(BF16)
        k = qk[:, AW:]
        k = k * _rs64(k, ones_ref) * kn_ref[...]
        k_ref[...] = _rope(k, cos2, sin2).astype(BF16)
        v_ref[...] = proj[:, AW + KW:AW + 2 * KW].astype(BF16)
        gu_ref[...] = _gelu(gz[:, :GW])
        gv = _gelu(gz[:, GW:])
        gvn_ref[...] = (gv * _rs(gv) * gvw_ref[...]).astype(BF16)

    return pl.pallas_call(
        body, name="mixer_in_fwd", grid=(t // TM,),
        in_specs=[_rows(TM, D), _const((1, D)), _const((IN, D)), _const((1, AW)), _const((1, KW)),
                  _const((1, GW)), _rows(TM, 128), _rows(TM, 128), _const((AW, AW))],
        out_specs=[_rows(TM, D), _rows(TM, AW + KW), _rows(TM, 2 * GW), _rows(TM, AW), _rows(TM, KW),
                   _rows(TM, KW), _rows(TM, GW), _rows(TM, GW)],
        out_shape=[_sds((t, D), BF16), _sds((t, AW + KW)), _sds((t, 2 * GW)), _sds((t, AW), BF16),
                   _sds((t, KW), BF16), _sds((t, KW), BF16), _sds((t, GW)), _sds((t, GW), BF16)],
        compiler_params=_params())(x, mix_norm, w_in, qn, kn, gvw, cos, sin, _head_ones())


def _dup_half(kk, g):
    lane = _lane(kk.shape)
    other = pltpu.roll(kk, 64, 1)
    keep = (lane < 64) if g == 0 else (lane >= 64)
    return jnp.where(keep, kk, other).astype(BF16)


def _swa_mask(first_block):
    qi = lax.broadcasted_iota(jnp.int32, (4 * BLK, 2 * BLK), 0) & (BLK - 1)
    kj = lax.broadcasted_iota(jnp.int32, (4 * BLK, 2 * BLK), 1)
    diff = qi + BLK - kj
    band = (diff >= 0) & (diff < BLK)
    return band & (jnp.logical_not(first_block) | (kj >= BLK))


def _stack_heads(a2, b2):
    lo = _lane(a2.shape) < 64
    z = jnp.zeros_like(a2)
    return jnp.concatenate([jnp.where(lo, a2, z), jnp.where(lo, z, a2), jnp.where(lo, b2, z), jnp.where(lo, z, b2)], axis=0)


def _unstack_heads(o):
    lo = _lane((BLK, 128)) < 64
    return jnp.where(lo, o[0:BLK], o[BLK:2 * BLK]), jnp.where(lo, o[2 * BLK:3 * BLK], o[3 * BLK:4 * BLK])


def _sink_col(sink_ref, g):
    row = lax.broadcasted_iota(jnp.int32, (4 * BLK, 1), 0)
    s = [sink_ref[0, 4 * g + j] for j in range(4)]
    return jnp.where(row < BLK, s[0], jnp.where(row < 2 * BLK, s[1], jnp.where(row < 3 * BLK, s[2], s[3])))


def _swa_probs(qs, kd, mask, sink):
    s = _dot_nt(qs, kd) * (1.0 / math.sqrt(HD))
    s = jnp.where(mask, s, NEG)
    m = jnp.maximum(jnp.max(s, axis=-1, keepdims=True), sink)
    p = jnp.exp(s - m)
    ps = jnp.exp(sink - m)
    inv = 1.0 / (jnp.sum(p, axis=-1, keepdims=True) + ps)
    return p * inv, ps * inv


SB = 4


def _swa_fwd(q, k, v, sinks):
    t = q.shape[0]
    ts = min(t, SB * BLK)

    def body(sink_ref, q_ref, kc_ref, kp_ref, vc_ref, vp_ref, o_ref):
        i = pl.program_id(0)
        kk = jnp.concatenate([kp_ref[...], kc_ref[...]], axis=0).astype(F32)
        vv = jnp.concatenate([vp_ref[...], vc_ref[...]], axis=0).astype(F32)
        for b in range(ts // BLK):
            r = slice(b * BLK, (b + 1) * BLK)
            kb, vb = kk[b * BLK:(b + 2) * BLK], vv[b * BLK:(b + 2) * BLK]
            mask = _swa_mask(i == 0) if b == 0 else _swa_mask(False)
            for g in range(2):
                qs = _stack_heads(q_ref[r, 256 * g:256 * g + 128], q_ref[r, 256 * g + 128:256 * g + 256])
                pn, _ = _swa_probs(qs, _dup_half(kb, g), mask, _sink_col(sink_ref, g))
                oa, ob = _unstack_heads(_dot(pn.astype(BF16), _dup_half(vb, g)))
                o_ref[r, 256 * g:256 * g + 128] = oa
                o_ref[r, 256 * g + 128:256 * g + 256] = ob

    cur = lambda i: (i, 0)
    prev = lambda i: (jnp.maximum(i * (ts // BLK) - 1, 0), 0)
    return pl.pallas_call(
        body, name="swa_fwd", grid=(t // ts,),
        in_specs=[pl.BlockSpec(memory_space=pltpu.SMEM), pl.BlockSpec((ts, AW), cur),
                  pl.BlockSpec((ts, KW), cur), pl.BlockSpec((BLK, KW), prev),
                  pl.BlockSpec((ts, KW), cur), pl.BlockSpec((BLK, KW), prev)],
        out_specs=pl.BlockSpec((ts, AW), cur), out_shape=_sds((t, AW)),
        compiler_params=_params())(sinks, q, k, k, v, v)


def _causal_bf16(w_ref, h, transposed):
    r = lax.broadcasted_iota(jnp.int32, (BLK, BLK), 0)
    c = lax.broadcasted_iota(jnp.int32, (BLK, BLK), 1)
    keep = (r <= c) if transposed else (c <= r)
    return jnp.where(keep, w_ref[h], 0.0).astype(BF16)


def _gmlp_mix(w_ref, xin, transposed):
    lo = _lane((BLK, 128)) < 64
    wm = [_causal_bf16(w_ref, h, transposed) for h in range(8)]
    rows = []
    for c in range(xin.shape[0] // BLK):
        cols = []
        for j in range(4):
            xs = xin[c * BLK:(c + 1) * BLK, 128 * j:128 * (j + 1)]
            cols.append(jnp.where(lo, _dot(wm[2 * j], xs), _dot(wm[2 * j + 1], xs)))
        rows.append(jnp.concatenate(cols, axis=1))
    return jnp.concatenate(rows, axis=0)


def _gmlp_fwd(gvn, gu, ws, bfull):
    t = gvn.shape[0]

    def body(x_ref, gu_ref, w_ref, b_ref, o_ref):
        mixed = _gmlp_mix(w_ref, x_ref[...], False) + jnp.tile(b_ref[...], (TM // BLK, 1))
        o_ref[...] = gu_ref[...] * mixed

    return pl.pallas_call(
        body, name="gmlp_fwd", grid=(t // TM,),
        in_specs=[_rows(TM, GW), _rows(TM, GW), _const((8, BLK, BLK)), _const((BLK, GW))],
        out_specs=_rows(TM, GW), out_shape=_sds((t, GW)), compiler_params=_params())(gvn, gu, ws, bfull)


def _mixer_out_fwd(attn, gm, x, w_out, aon, gon, xan):
    t = x.shape[0]

    def body(a_ref, g_ref, x_ref, w_ref, aon_ref, gon_ref, xan_ref, y_ref, x1_ref, h2_ref):
        a, g = a_ref[...], g_ref[...]
        y = jnp.concatenate([a * _rs(a) * aon_ref[...], g * _rs(g) * gon_ref[...]], axis=1).astype(BF16)
        y_ref[...] = y
        x1 = x_ref[...] + _dot(y, w_ref[...])
        x1_ref[...] = x1
        h2_ref[...] = (x1 * _rs(x1) * xan_ref[...]).astype(BF16)

    return pl.pallas_call(
        body, name="mixer_out_fwd", grid=(t // TM,),
        in_specs=[_rows(TM, AW), _rows(TM, GW), _rows(TM, D), _const((D, D)), _const((1, AW)), _const((1, GW)),
                  _const((1, D))],
        out_specs=[_rows(TM, D), _rows(TM, D), _rows(TM, D)],
        out_shape=[_sds((t, D), BF16), _sds((t, D)), _sds((t, D), BF16)],
        compiler_params=_params())(attn, gm, x, w_out, aon, gon, xan)


def _mem_kv_fwd(mem, mem_norm, wkv, kn4):
    def body(m_ref, g_ref, w_ref, kn_ref, mh_ref, kpre_ref, k_ref, v_ref):
        m = m_ref[...]
        mh = (m * _rs(m) * g_ref[...]).astype(BF16)
        mh_ref[...] = mh
        for h in range(XH):
            sl = slice(XD * h, XD * (h + 1))
            kh = _dot_nt(mh, w_ref[h])
            kpre_ref[:, sl] = kh
            k_ref[:, sl] = (kh * _rs(kh) * kn_ref[:, sl]).astype(BF16)
            v_ref[:, sl] = _dot_nt(mh, w_ref[XH + h]).astype(BF16)

    return pl.pallas_call(
        body, name="mem_kv_fwd",
        out_shape=[_sds((MEM, D), BF16), _sds((MEM, D)), _sds((MEM, D), BF16), _sds((MEM, D), BF16)],
        compiler_params=pltpu.CompilerParams(vmem_limit_bytes=VMEM_LIMIT))(mem, mem_norm, wkv, kn4)


def _xattn_probs(qpre_h, qn_h, k_h):
    rq = _rs(qpre_h)
    q2 = (qpre_h * rq * qn_h).astype(BF16)
    s = _dot_nt(q2, k_h) * (1.0 / math.sqrt(XD))
    p = jnp.exp(s - jnp.max(s, axis=-1, keepdims=True))
    return p * (1.0 / jnp.sum(p, axis=-1, keepdims=True)), q2, rq


def _xattn_fwd(h2, x1, wq, qn4, k2, v2, wo, ffn_norm):
    t = x1.shape[0]

    def body(h_ref, x_ref, wq_ref, qn_ref, k_ref, v_ref, wo_ref, fn_ref, qpre_ref, o_ref, x2_ref, h3_ref):
        qpre = _dot(h_ref[...], wq_ref[...])
        qpre_ref[...] = qpre
        outs = []
        for h in range(XH):
            sl = slice(XD * h, XD * (h + 1))
            pn, _, _ = _xattn_probs(qpre[:, sl], qn_ref[:, sl], k_ref[:, sl])
            outs.append(_dot(pn.astype(BF16), v_ref[:, sl]))
        o = jnp.concatenate(outs, axis=1).astype(BF16)
        o_ref[...] = o
        x2 = x_ref[...] + _dot(o, wo_ref[...])
        x2_ref[...] = x2
        h3_ref[...] = (x2 * _rs(x2) * fn_ref[...]).astype(BF16)

    return pl.pallas_call(
        body, name="xattn_fwd", grid=(t // TM,),
        in_specs=[_rows(TM, D), _rows(TM, D), _const((D, D)), _const((1, D)), _const((MEM, D)), _const((MEM, D)),
                  _const((D, D)), _const((1, D))],
        out_specs=[_rows(TM, D)] * 4,
        out_shape=[_sds((t, D)), _sds((t, D), BF16), _sds((t, D)), _sds((t, D), BF16)],
        compiler_params=_params())(h2, x1, wq, qn4, k2, v2, wo, ffn_norm)


SW = 704
NG = FF // SW
FM = 256
HALO = 16


def _resident(shape):
    nd = len(shape)
    return pl.BlockSpec(shape, lambda *_: (0,) * nd, pipeline_mode=pl.Buffered(1))


def _halo_before(i):
    return jnp.maximum(i * (FM // HALO) - 1, 0)


def _conv(e, w):
    return w[2:3, :] * e + pltpu.roll(w[1:2, :] * e + pltpu.roll(w[0:1, :] * e, 1, 0), 1, 0)


def _conv_t(dc, w):
    n = dc.shape[0]
    return w[2:3, :] * dc + pltpu.roll(w[1:2, :] * dc + pltpu.roll(w[0:1, :] * dc, n - 1, 0), n - 1, 0)


def _ffn_fwd(h3, x2, target, up, conv, conv_b, down):
    t = x2.shape[0]

    def body(h_ref, hp_ref, x_ref, t_ref, up_ref, w_ref, b_ref, dn_ref, a_ref, u_ref, gs_ref, dy_ref, loss_ref, acc_ref):
        i = pl.program_id(0)

        @pl.when(i == 0)
        def _():
            acc_ref[...] = jnp.zeros_like(acc_ref)

        before = jnp.where(i > 0, hp_ref[...], jnp.zeros_like(hp_ref))
        he = jnp.concatenate([before, h_ref[...]], axis=0)
        err = x_ref[...] - t_ref[...]
        for d in range(NG):
            c = []
            for s in range(2):
                a = _dot_nt(he, up_ref[s * NG + d])
                a_ref[s * NG + d] = a[HALO:].astype(BF16)
                c.append(_conv(a, w_ref[s, d])[HALO:] + b_ref[s, d])
            gl, gg = _gelu_and_grad(c[0])
            gs_ref[d] = gl.astype(BF16)
            gs_ref[NG + d] = (gg * c[1]).astype(BF16)
            u = (gl * c[1]).astype(BF16)
            u_ref[d] = u
            err = err + _dot(u, dn_ref[d])
        dy_ref[...] = err * (1.0 / D)
        acc_ref[...] += jnp.sum(err * err, axis=0, keepdims=True)

        @pl.when(i == pl.num_programs(0) - 1)
        def _():
            loss_ref[...] = jnp.full((8, 128), 0.5 / D, F32) * jnp.sum(acc_ref[...])

    return pl.pallas_call(
        body, name="ffn_fwd", grid=(t // FM,),
        in_specs=[_rows(FM, D), pl.BlockSpec((HALO, D), lambda i: (_halo_before(i), 0)), _rows(FM, D), _rows(FM, D),
                  _resident((NDEV, SW, D)), _resident((2, NG, 3, SW)), _resident((2, NG, 1, SW)), _resident((NG, SW, D))],
        out_specs=[pl.BlockSpec((NDEV, FM, SW), lambda i: (0, i, 0)), pl.BlockSpec((NG, FM, SW), lambda i: (0, i, 0)),
                   pl.BlockSpec((NDEV, FM, SW), lambda i: (0, i, 0)), _rows(FM, D), _const((8, 128))],
        out_shape=[_sds((NDEV, t, SW), BF16), _sds((NG, t, SW), BF16), _sds((NDEV, t, SW), BF16), _sds((t, D)),
                   _sds((8, 128))],
        scratch_shapes=[pltpu.VMEM((1, D), F32)], compiler_params=_params())(h3, h3, x2, target, up, conv, conv_b, down)


def _gelu_and_grad(z):
    z2 = z * z
    t = jnp.tanh(GELU_C0 * (z + GELU_C1 * z * z2))
    phi = 0.5 * (1.0 + t)
    return z * phi, phi + 0.5 * z * (1.0 - t * t) * (GELU_C0 * (1.0 + 3.0 * GELU_C1 * z2))


def _ffn_bwd(dy, a, gs, x2, up, conv, down, ffn_norm):
    t = x2.shape[0]
    nt = t // FM
    n = FM + HALO

    def body(dy_ref, dyn_ref, a_ref, gs_ref, gsn_ref, x_ref, up_ref, w_ref, dn_ref, g_ref,
             dx_ref, da_ref, s_ref, dfn_ref):
        i = pl.program_id(0)

        @pl.when(i == 0)
        def _():
            s_ref[...] = jnp.zeros_like(s_ref)
            dfn_ref[...] = jnp.zeros_like(dfn_ref)

        last = i == nt - 1
        dy = dy_ref[...]
        dye = jnp.concatenate([dy, jnp.where(last, 0.0, dyn_ref[...])], axis=0).astype(BF16)
        dh = jnp.zeros((FM, D), F32)
        row = lax.broadcasted_iota(jnp.int32, (8, SW), 0)
        for d in range(NG):
            du = _dot_nt(dye, dn_ref[d])
            for s in range(2):
                j = s * NG + d
                k = NG + d if s == 0 else d
                dc = du * jnp.concatenate([gs_ref[k], gsn_ref[k]], axis=0).astype(F32)
                w = w_ref[s, d]
                tile = a_ref[j].astype(F32)
                d1 = pltpu.roll(dc, n - 1, 0)
                d2 = pltpu.roll(d1, n - 1, 0)
                da = (w[2:3, :] * dc + w[1:2, :] * d1 + w[0:1, :] * d2)[0:FM].astype(BF16)
                da_ref[j] = da
                dh = dh + _dot(da, up_ref[j])
                sums = [jnp.sum(v[0:FM] * tile, axis=0, keepdims=True) for v in (d2, d1, dc)]
                sums.append(jnp.sum(dc[0:FM], axis=0, keepdims=True))
                upd = jnp.zeros((8, SW), F32)
                for r, v in enumerate(sums):
                    upd = jnp.where(row == r, jnp.broadcast_to(v, (8, SW)), upd)
                s_ref[s, d] += upd
        x = x_ref[...]
        dx, dg = _rms_bwd(dh, x, _rs(x), g_ref[...])
        dx_ref[...] = dy + dx
        dfn_ref[...] += _colsum8(dg)

    last_halo = t // HALO - 1
    after = lambda i: jnp.minimum((i + 1) * (FM // HALO), last_halo)
    return pl.pallas_call(
        body, name="ffn_bwd", grid=(nt,),
        in_specs=[_rows(FM, D), pl.BlockSpec((HALO, D), lambda i: (after(i), 0)),
                  pl.BlockSpec((NDEV, FM, SW), lambda i: (0, i, 0)),
                  pl.BlockSpec((NDEV, FM, SW), lambda i: (0, i, 0)),
                  pl.BlockSpec((NDEV, HALO, SW), lambda i: (0, after(i), 0)),
                  _rows(FM, D), _resident((NDEV, SW, D)), _resident((2, NG, 3, SW)), _resident((NG, SW, D)), _const((1, D))],
        out_specs=[_rows(FM, D), pl.BlockSpec((NDEV, FM, SW), lambda i: (0, i, 0)), _const((2, NG, 8, SW)), _const((8, D))],
        out_shape=[_sds((t, D)), _sds((NDEV, t, SW), BF16), _sds((2, NG, 8, SW)), _sds((8, D))],
        compiler_params=_params())(dy, dy, a, gs, gs, x2, up, conv, down, ffn_norm)


BT = 512


def _xattn_bwd(dx2, x1, qpre, k2, v2, wq, wo, qn4, xan):
    t = x1.shape[0]

    def body(dx2_ref, x1_ref, qpre_ref, k_ref, v_ref, wq_ref, wo_ref, qn_ref, xan_ref,
             dx1_ref, dqpre_ref, dk_ref, dv_ref, dqn_ref, dxan_ref):
        @pl.when(pl.program_id(0) == 0)
        def _():
            for r in (dk_ref, dv_ref, dqn_ref, dxan_ref):
                r[...] = jnp.zeros_like(r)

        dx2 = dx2_ref[...]
        do = _dot_nt(dx2.astype(BF16), wo_ref[...])
        dqs = []
        for h in range(XH):
            sl = slice(XD * h, XD * (h + 1))
            qpre_h = qpre_ref[:, sl]
            pn, q2, rq = _xattn_probs(qpre_h, qn_ref[:, sl], k_ref[:, sl])
            do_h = do[:, sl].astype(BF16)
            dp = _dot_nt(do_h, v_ref[:, sl])
            ds = (pn * (dp - jnp.sum(pn * dp, axis=-1, keepdims=True)) * (1.0 / math.sqrt(XD))).astype(BF16)
            dq2 = _dot(ds, k_ref[:, sl])
            dk_ref[:, sl] += _dot_tn(ds, q2)
            dv_ref[:, sl] += _dot_tn(pn.astype(BF16), do_h)
            dqh, dg = _rms_bwd(dq2, qpre_h, rq, qn_ref[:, sl])
            dqn_ref[...] += _colsum8(dg)
            dqs.append(dqh)
        dqpre = jnp.concatenate(dqs, axis=1).astype(BF16)
        dqpre_ref[...] = dqpre
        dh2 = _dot_nt(dqpre, wq_ref[...])
        x1 = x1_ref[...]
        dx, dg = _rms_bwd(dh2, x1, _rs(x1), xan_ref[...])
        dx1_ref[...] = dx2 + dx
        dxan_ref[...] += _colsum8(dg)

    return pl.pallas_call(
        body, name="xattn_bwd", grid=(t // BT,),
        in_specs=[_rows(BT, D), _rows(BT, D), _rows(BT, D), _const((MEM, D)), _const((MEM, D)), _const((D, D)),
                  _const((D, D)), _const((1, D)), _const((1, D))],
        out_specs=[_rows(BT, D), _rows(BT, D), _const((MEM, D)), _const((MEM, D)), _const((8, XD)), _const((8, D))],
        out_shape=[_sds((t, D)), _sds((t, D), BF16), _sds((MEM, D)), _sds((MEM, D)), _sds((8, XD)), _sds((8, D))],
        compiler_params=_params())(dx2, x1, qpre, k2, v2, wq, wo, qn4, xan)


def _mem_kv_bwd(mem, mh, kpre, dk2, dv2, wkv, kn4, mem_norm):
    def body(m_ref, mh_ref, kpre_ref, dk_ref, dv_ref, w_ref, kn_ref, g_ref, dw_ref, dkn_ref, dmn_ref):
        dkn = jnp.zeros((8, XD), F32)
        dm = jnp.zeros((MEM, D), F32)
        mh = mh_ref[...]
        for h in range(XH):
            sl = slice(XD * h, XD * (h + 1))
            kh = kpre_ref[:, sl]
            dkh, dg = _rms_bwd(dk_ref[:, sl], kh, _rs(kh), kn_ref[:, sl])
            dkn = dkn + _colsum8(dg)
            dkh = dkh.astype(BF16)
            dvh = dv_ref[:, sl].astype(BF16)
            dw_ref[h] = _dot_tn(dkh, mh).astype(BF16)
            dw_ref[XH + h] = _dot_tn(dvh, mh).astype(BF16)
            dm = dm + _dot(dkh, w_ref[h]) + _dot(dvh, w_ref[XH + h])
        dkn_ref[...] = dkn
        m = m_ref[...]
        _, dg = _rms_bwd(dm, m, _rs(m), g_ref[...])
        dmn_ref[...] = _colsum8(dg)

    return pl.pallas_call(
        body, name="mem_kv_bwd", out_shape=[_sds((2 * XH, XD, D), BF16), _sds((8, XD)), _sds((8, D))],
        compiler_params=pltpu.CompilerParams(vmem_limit_bytes=VMEM_LIMIT))(mem, mh, kpre, dk2, dv2, wkv, kn4, mem_norm)


def _mixer_out_bwd(dx1, attn, gm, w_out, aon, gon):
    t = dx1.shape[0]

    def body(dx_ref, a_ref, g_ref, w_ref, aon_ref, gon_ref, da_ref, dg_ref, dan_ref, dgn_ref):
        @pl.when(pl.program_id(0) == 0)
        def _():
            dan_ref[...] = jnp.zeros_like(dan_ref)
            dgn_ref[...] = jnp.zeros_like(dgn_ref)

        dy = _dot_nt(dx_ref[...].astype(BF16), w_ref[...])
        a, g = a_ref[...], g_ref[...]
        da, dna = _rms_bwd(dy[:, :AW], a, _rs(a), aon_ref[...])
        dg, dng = _rms_bwd(dy[:, AW:], g, _rs(g), gon_ref[...])
        da_ref[...] = da
        dg_ref[...] = dg
        dan_ref[...] += _colsum8(dna)
        dgn_ref[...] += _colsum8(dng)

    return pl.pallas_call(
        body, name="mixer_out_bwd", grid=(t // TM,),
        in_specs=[_rows(TM, D), _rows(TM, AW), _rows(TM, GW), _const((D, D)), _const((1, AW)), _const((1, GW))],
        out_specs=[_rows(TM, AW), _rows(TM, GW), _const((8, AW)), _const((8, GW))],
        out_shape=[_sds((t, AW)), _sds((t, GW)), _sds((8, AW)), _sds((8, GW))],
        compiler_params=_params())(dx1, attn, gm, w_out, aon, gon)


def _gmlp_bwd(dgm, gu, gvn, gz, ws, wst, bfull, gvw):
    t = dgm.shape[0]
    nc = TM // BLK

    def body(dgm_ref, gu_ref, x_ref, gz_ref, w_ref, wt_ref, b_ref, gvw_ref, dgz_ref, dw_ref, db_ref, dgvw_ref,
             dbacc_ref):
        @pl.when(pl.program_id(0) == 0)
        def _():
            for r in (dw_ref, dbacc_ref, dgvw_ref):
                r[...] = jnp.zeros_like(r)

        xin = x_ref[...]
        dgm = dgm_ref[...]
        mixed = _gmlp_mix(w_ref, xin, False) + jnp.tile(b_ref[...], (nc, 1))
        dgu = dgm * mixed
        dmixed = dgm * gu_ref[...]
        lo = _lane((BLK, 128)) < 64
        dbias = jnp.zeros((BLK, GW), F32)
        for c in range(nc):
            dmc = dmixed[c * BLK:(c + 1) * BLK]
            dbias = dbias + dmc
            for j in range(4):
                dm2 = dmc[:, 128 * j:128 * (j + 1)]
                xs = xin[c * BLK:(c + 1) * BLK, 128 * j:128 * (j + 1)]
                z = jnp.zeros_like(dm2)
                dw_ref[2 * j] += _dot_nt(jnp.where(lo, dm2, z).astype(BF16), xs)
                dw_ref[2 * j + 1] += _dot_nt(jnp.where(lo, z, dm2).astype(BF16), xs)
        dbacc_ref[...] += dbias
        dgvn = _gmlp_mix(wt_ref, dmixed.astype(BF16), True)
        gz_u, gz_v = gz_ref[:, :GW], gz_ref[:, GW:]
        gv = _gelu(gz_v)
        dgv, dg = _rms_bwd(dgvn, gv, _rs(gv), gvw_ref[...])
        dgvw_ref[...] += _colsum8(dg)
        dgz_ref[:, :GW] = (dgu * _gelu_grad(gz_u)).astype(BF16)
        dgz_ref[:, GW:] = (dgv * _gelu_grad(gz_v)).astype(BF16)

        @pl.when(pl.program_id(0) == pl.num_programs(0) - 1)
        def _():
            s = dbacc_ref[...]
            sel = (lax.broadcasted_iota(jnp.int32, (8, GW), 1) // HD
                   == lax.broadcasted_iota(jnp.int32, (8, GW), 0)).astype(BF16)
            hi = s.astype(BF16)
            r1 = s - hi.astype(F32)
            mid = r1.astype(BF16)
            lo = (r1 - mid.astype(F32)).astype(BF16)
            db_ref[...] = _dot_nt(sel, hi) + _dot_nt(sel, mid) + _dot_nt(sel, lo)
            r = lax.broadcasted_iota(jnp.int32, (BLK, BLK), 0)
            c = lax.broadcasted_iota(jnp.int32, (BLK, BLK), 1)
            for h in range(8):
                dw_ref[h] = jnp.where(c <= r, dw_ref[h], 0.0)

    return pl.pallas_call(
        body, name="gmlp_bwd", grid=(t // TM,),
        in_specs=[_rows(TM, GW), _rows(TM, GW), _rows(TM, GW), _rows(TM, 2 * GW), _const((8, BLK, BLK)),
                  _const((8, BLK, BLK)), _const((BLK, GW)), _const((1, GW))],
        out_specs=[_rows(TM, 2 * GW), _const((8, BLK, BLK)), _const((8, BLK)), _const((8, GW))],
        out_shape=[_sds((t, 2 * GW), BF16), _sds((8, BLK, BLK)), _sds((8, BLK)), _sds((8, GW))],
        scratch_shapes=[pltpu.VMEM((BLK, GW), F32)],
        compiler_params=_params())(dgm, gu, gvn, gz, ws, wst, bfull, gvw)


def _fold_half(v):
    return v + pltpu.roll(v, 64, 1)


def _swa_bwd(q, k, v, dattn, sinks):
    t = q.shape[0]
    nb = t // BLK
    ts = min(t, SB * BLK)
    sb = ts // BLK
    nt = t // ts

    def body(sink_ref, q_ref, kc_ref, kp_ref, vc_ref, vp_ref, do_ref, dq_ref, dk_ref, dv_ref, ds_ref,
             ck_ref, cv_ref, sacc_ref):
        i = pl.program_id(0)

        @pl.when(i == 0)
        def _():
            ck_ref[...] = jnp.zeros_like(ck_ref)
            cv_ref[...] = jnp.zeros_like(cv_ref)
            sacc_ref[...] = jnp.zeros_like(sacc_ref)

        @pl.when(i < nt)
        def _():
            kk = jnp.concatenate([kp_ref[...], kc_ref[...]], axis=0).astype(F32)
            vv = jnp.concatenate([vp_ref[...], vc_ref[...]], axis=0).astype(F32)
            lo256 = _lane((2 * BLK, 128)) < 64
            acc_k = [jnp.zeros((BLK, 128), F32) for _ in range(sb + 1)]
            acc_v = [jnp.zeros((BLK, 128), F32) for _ in range(sb + 1)]
            for b in range(sb):
                r = slice(b * BLK, (b + 1) * BLK)
                kb, vb = kk[b * BLK:(b + 2) * BLK], vv[b * BLK:(b + 2) * BLK]
                mask = _swa_mask(i == 0) if b == 0 else _swa_mask(False)
                dkk = jnp.zeros((2 * BLK, 128), F32)
                dvv = jnp.zeros((2 * BLK, 128), F32)
                for g in range(2):
                    qs = _stack_heads(q_ref[r, 256 * g:256 * g + 128], q_ref[r, 256 * g + 128:256 * g + 256])
                    dos = _stack_heads(do_ref[r, 256 * g:256 * g + 128],
                                       do_ref[r, 256 * g + 128:256 * g + 256]).astype(BF16)
                    kd = _dup_half(kb, g)
                    pn, psn = _swa_probs(qs, kd, mask, _sink_col(sink_ref, g))
                    dp = _dot_nt(dos, _dup_half(vb, g))
                    dd = jnp.sum(pn * dp, axis=-1, keepdims=True)
                    ds = (pn * (dp - dd) * (1.0 / math.sqrt(HD))).astype(BF16)
                    sacc_ref[g] += jnp.broadcast_to(-psn * dd, (4 * BLK, 128))
                    dqa, dqb = _unstack_heads(_dot(ds, kd))
                    dq_ref[r, 256 * g:256 * g + 128] = dqa
                    dq_ref[r, 256 * g + 128:256 * g + 256] = dqb
                    dkg = _fold_half(_dot_tn(ds, qs))
                    dvg = _fold_half(_dot_tn(pn.astype(BF16), dos))
                    keep = lo256 if g == 0 else jnp.logical_not(lo256)
                    dkk = jnp.where(keep, dkg, dkk)
                    dvv = jnp.where(keep, dvg, dvv)
                acc_k[b], acc_k[b + 1] = acc_k[b] + dkk[0:BLK], acc_k[b + 1] + dkk[BLK:]
                acc_v[b], acc_v[b + 1] = acc_v[b] + dvv[0:BLK], acc_v[b + 1] + dvv[BLK:]
            for out_ref, c_ref, acc in ((dk_ref, ck_ref, acc_k), (dv_ref, cv_ref, acc_v)):
                if sb > 1:
                    out_ref[0:ts - BLK] = c_ref[0:ts - BLK]
                out_ref[ts - BLK:ts] = c_ref[ts - BLK:ts] + acc[0]
                for b in range(sb):
                    c_ref[b * BLK:(b + 1) * BLK] = acc[b + 1]

        @pl.when(i == nt)
        def _():
            dk_ref[...] = ck_ref[...]
            dv_ref[...] = cv_ref[...]
            lane = _lane((8, 128))
            acc = jnp.zeros((8, 128), F32)
            for g in range(2):
                for j in range(4):
                    val = jnp.sum(sacc_ref[g, j * BLK:(j + 1) * BLK, :], axis=0, keepdims=True)
                    acc = jnp.where(lane == 4 * g + j, jnp.broadcast_to(val, (8, 128)), acc)
            ds_ref[...] = acc

    cur = lambda i: (jnp.minimum(i, nt - 1), 0)
    before = lambda i: (jnp.clip(i * sb - 1, 0, nb - 1), 0)
    done = lambda i: (jnp.clip(i - 1, 0, nt - 1), 0)
    return pl.pallas_call(
        body, name="swa_bwd", grid=(nt + 1,),
        in_specs=[pl.BlockSpec(memory_space=pltpu.SMEM), pl.BlockSpec((ts, AW), cur),
                  pl.BlockSpec((ts, KW), cur), pl.BlockSpec((BLK, KW), before),
                  pl.BlockSpec((ts, KW), cur), pl.BlockSpec((BLK, KW), before), pl.BlockSpec((ts, AW), cur)],
        out_specs=[pl.BlockSpec((ts, AW), cur), pl.BlockSpec((ts, KW), done), pl.BlockSpec((ts, KW), done),
                   _const((8, 128))],
        out_shape=[_sds((t, AW)), _sds((t, KW)), _sds((t, KW)), _sds((8, 128))],
        scratch_shapes=[pltpu.VMEM((ts, KW), F32), pltpu.VMEM((ts, KW), F32), pltpu.VMEM((2, 4 * BLK, 128), F32)],
        compiler_params=_params())(sinks, q, k, k, v, v, dattn)


def _mixer_in_bwd(dq, dk, dv, dgz, qk, cos, sin, x, dx1, w_in, mix_norm, qn, kn):
    t = x.shape[0]

    def body(dq_ref, dk_ref, dv_ref, dgz_ref, qk_ref, cos_ref, sin_ref, x_ref, dx1_ref, w_ref, g_ref, qn_ref, kn_ref,
             ones_ref, gx_ref, dproj_ref, dmn_ref, dqn_ref, dkn_ref, qacc_ref, kacc_ref):
        i = pl.program_id(0)

        @pl.when(i == 0)
        def _():
            dmn_ref[...] = jnp.zeros_like(dmn_ref)
            qacc_ref[...] = jnp.zeros_like(qacc_ref)
            kacc_ref[...] = jnp.zeros_like(kacc_ref)

        cos2, sin2 = cos_ref[...], sin_ref[...]
        qpre, kpre = qk_ref[:, :AW], qk_ref[:, AW:]
        dqh = _rope_bwd(dq_ref[...], jnp.tile(cos2, (1, 4)), jnp.tile(sin2, (1, 4)))
        dqpre, dgq = _rms64_bwd(dqh, qpre, _rs64(qpre, ones_ref), qn_ref[...], ones_ref)
        dkh = _rope_bwd(dk_ref[...], cos2, sin2)
        dkpre, dgk = _rms64_bwd(dkh, kpre, _rs64(kpre, ones_ref), kn_ref[...], ones_ref)
        qacc_ref[...] += jnp.sum(dgq, axis=0, keepdims=True)
        kacc_ref[...] += jnp.sum(dgk, axis=0, keepdims=True)
        dproj = jnp.concatenate([dqpre.astype(BF16), dkpre.astype(BF16), dv_ref[...].astype(BF16), dgz_ref[...]], axis=1)
        dproj_ref[...] = dproj
        dh = _dot(dproj, w_ref[...])
        xv = x_ref[...]
        dx, dg = _rms_bwd(dh, xv, _rs(xv), g_ref[...])
        gx_ref[...] = dx1_ref[...] + dx
        dmn_ref[...] += _colsum8(dg)

        @pl.when(i == pl.num_programs(0) - 1)
        def _():
            qa = qacc_ref[...]
            q4 = qa[:, 0:128] + qa[:, 128:256] + qa[:, 256:384] + qa[:, 384:512]
            dqn_ref[...] = jnp.broadcast_to(_fold_half(q4), (8, 128))
            dkn_ref[...] = jnp.broadcast_to(_fold_half(kacc_ref[...]), (8, 128))

    return pl.pallas_call(
        body, name="mixer_in_bwd", grid=(t // TM,),
        in_specs=[_rows(TM, AW), _rows(TM, KW), _rows(TM, KW), _rows(TM, 2 * GW), _rows(TM, AW + KW), _rows(TM, 128),
                  _rows(TM, 128), _rows(TM, D), _rows(TM, D), _const((IN, D)), _const((1, D)), _const((1, AW)),
                  _const((1, KW)), _const((AW, AW))],
        out_specs=[_rows(TM, D), _rows(TM, IN), _const((8, D)), _const((8, 128)), _const((8, 128))],
        out_shape=[_sds((t, D)), _sds((t, IN), BF16), _sds((8, D)), _sds((8, 128)), _sds((8, 128))],
        scratch_shapes=[pltpu.VMEM((1, AW), F32), pltpu.VMEM((1, KW), F32)],
        compiler_params=_params())(dq, dk, dv, dgz, qk, cos, sin, x, dx1, w_in, mix_norm, qn, kn, _head_ones())


def _local_step(x, mem, pos, target, p, fetch, ship):
    t = x.shape[0]
    p = dict(p)
    p.update(fetch(0, None))
    inv_freq = 1.0 / (ROPE_THETA ** (jnp.arange(HD // 2, dtype=F32) * (2.0 / HD)))
    cos, sin = _rope_tables(pos, jnp.tile(inv_freq, 4).reshape(1, 128))
    qn = jnp.tile(p["q_norm"], (1, AW // HD))
    kn = jnp.tile(p["k_norm"], (1, KW // HD))
    qn4 = jnp.tile(p["xa_q_norm"], (1, XH))
    kn4 = jnp.tile(p["xa_k_norm"], (1, XH))
    ws = p["gmlp_ws"]
    wst = jnp.swapaxes(ws, 1, 2)
    bfull = jnp.repeat(p["gmlp_bs"].T, HD, axis=1)
    conv_b = p["ffn_conv_b"]

    h1, qk, gz, q, k, v, gu, gvn = _mixer_in_fwd(x, p["mix_norm"], p["w_in"], qn, kn, p["gmlp_v_norm"], cos, sin)
    attn = _swa_fwd(q, k, v, p["attn_sinks"])
    gm = _gmlp_fwd(gvn, gu, ws, bfull)
    ycat, x1, h2 = _mixer_out_fwd(attn, gm, x, p["w_out"], p["attn_out_norm"], p["gmlp_out_norm"], p["xa_norm"])
    p.update(fetch(1, h2))
    mh, kpre, k2, v2 = _mem_kv_fwd(mem, p["mem_norm"], p["xa_wkv"], kn4)
    qpre, o, x2, h3 = _xattn_fwd(h2, x1, p["xa_wq"], qn4, k2, v2, p["xa_wo"], p["ffn_norm"])
    p.update(fetch(2, h3))
    conv = p["ffn_conv"]
    a, u, gs, dy, loss8 = _ffn_fwd(h3, x2, target, p["ffn_up"], conv, conv_b, p["ffn_down"])

    raw = {}
    d_down = _mm_tn(u, dy, "ffn_down_bwd_w")
    dx2, da, raw["conv_sums"], raw["ffn_norm"] = _ffn_bwd(dy, a, gs, x2, p["ffn_up"], conv, p["ffn_down"], p["ffn_norm"])
    d_up = _mm_tn(da, h3, "ffn_up_bwd_w")
    token = ship(0, {"ffn_down": d_down, "ffn_up": d_up, "ffn_conv": raw["conv_sums"][:, :, 0:3]})
    dx1, dqpre, dk2, dv2, raw["xa_q_norm"], raw["xa_norm"] = _xattn_bwd(
        dx2, x1, qpre, k2, v2, p["xa_wq"], p["xa_wo"], qn4 + jnp.tile(token[0:1], (1, D // 128)), p["xa_norm"])
    d_wo = _mm_tn(o, dx2, "xa_wo_bwd_w")
    d_wq = _mm_tn(h2, dqpre, "xa_wq_bwd_w")
    d_wkv, raw["xa_k_norm"], raw["mem_norm"] = _mem_kv_bwd(mem, mh, kpre, dk2, dv2, p["xa_wkv"], kn4, p["mem_norm"])
    d_w_out = _mm_tn(ycat, dx1, "w_out_bwd_w")
    token = ship(1, {"xa_wo": d_wo, "xa_wq": d_wq, "xa_wkv": d_wkv, "w_out": d_w_out})
    dattn, dgm, raw["attn_out_norm"], raw["gmlp_out_norm"] = _mixer_out_bwd(
        dx1, attn, gm, p["w_out"], p["attn_out_norm"] + jnp.tile(token[0:1], (1, AW // 128)), p["gmlp_out_norm"])
    dgz, raw["gmlp_ws"], raw["gmlp_bs"], raw["gmlp_v_norm"] = _gmlp_bwd(dgm, gu, gvn, gz, ws, wst, bfull, p["gmlp_v_norm"])
    token = ship(2, {}, [raw["gmlp_ws"]])
    dq, dk, dv, raw["attn_sinks"] = _swa_bwd(q, k, v, dattn, p["attn_sinks"] + token[0:1, 0:8])
    grad_x, dproj, raw["mix_norm"], raw["q_norm"], raw["k_norm"] = _mixer_in_bwd(
        dq, dk, dv, dgz, qk, cos, sin, x, dx1, p["w_in"], p["mix_norm"], qn, kn)
    d_w_in = _mm_tn(dproj, h1, "w_in_bwd_w")
    return loss8[0, 0], grad_x, {"w_in": d_w_in}, raw


def _cast_shards(shards):
    def body(*refs):
        n = len(refs) // 2
        for i_ref, o_ref in zip(refs[:n], refs[n:]):
            o_ref[...] = i_ref[...].astype(BF16)

    return pl.pallas_call(body, name="cast_shards", out_shape=[_sds(s.shape, BF16) for s in shards],
                          compiler_params=pltpu.CompilerParams(vmem_limit_bytes=VMEM_LIMIT))(*shards)


HBM_SPEC = pl.BlockSpec(memory_space=pltpu.HBM)
SEM_SPEC = pl.BlockSpec(memory_space=pltpu.SEMAPHORE)


def _remote_copies(src_refs, land_refs, send_refs, recv_refs, nd):
    x, y, cc = lax.axis_index("x"), lax.axis_index("y"), lax.axis_index("c")
    me = 4 * x + 2 * y + cc
    copies = []
    for a, (src_ref, land_ref) in enumerate(zip(src_refs, land_refs)):
        for k in range(1, NDEV):
            px = 1 - x if k & 4 else x
            py = 1 - y if k & 2 else y
            pc = 1 - cc if k & 1 else cc
            copies.append(pltpu.make_async_remote_copy(
                src_ref=src_ref.at[4 * px + 2 * py + pc] if a < nd else src_ref, dst_ref=land_ref.at[me],
                send_sem=send_refs[a].at[k - 1], recv_sem=recv_refs[a].at[k - 1],
                device_id=(px, py, pc), device_id_type=pl.DeviceIdType.MESH))
    return copies


def _own_slot(src, by_dest, me):
    block = lax.dynamic_index_in_dim(src, me, 0, keepdims=True) if by_dest else src[None]
    return lax.dynamic_update_index_in_dim(lax.empty((NDEV,) + block.shape[1:], src.dtype), block, me, 0)


def _exchange_start(by_dest, for_all, me, name):
    srcs = list(by_dest) + list(for_all)
    n, nd = len(srcs), len(by_dest)
    lands = [_own_slot(s, a < nd, me) for a, s in enumerate(srcs)]

    def body(*refs):
        for cp in _remote_copies(refs[:n], refs[n:2 * n], refs[2 * n:3 * n], refs[3 * n:4 * n], nd):
            cp.start()
        refs[-1][...] = jnp.zeros((8, 128), F32)

    sems = [pltpu.SemaphoreType.DMA((NDEV - 1,))] * (2 * n)
    thru = [pltpu.HBM(v.shape, v.dtype) for v in srcs + lands]
    res = pl.pallas_call(
        body, name=name, out_shape=sems + thru + [_sds((8, 128))],
        in_specs=[HBM_SPEC] * (2 * n), out_specs=[SEM_SPEC] * (2 * n) + [HBM_SPEC] * (2 * n) + [pl.BlockSpec(memory_space=pltpu.VMEM)],
        input_output_aliases={i: 2 * n + i for i in range(2 * n)},
        compiler_params=pltpu.CompilerParams(has_side_effects=pltpu.SideEffectType.DATAFLOW_SIDE_EFFECTING))(
            *[pltpu.with_memory_space_constraint(v, pltpu.HBM) for v in srcs + lands])
    return (res[:2 * n], res[2 * n:4 * n], nd), res[-1]


def _exchange_wait(state, after, name):
    sems, thru, nd = state
    n = len(thru) // 2

    def body(*refs):
        for cp in _remote_copies(refs[:n], refs[n:2 * n], refs[2 * n:3 * n], refs[3 * n:4 * n], nd):
            cp.wait_send()
            cp.wait_recv()

    res = pl.pallas_call(
        body, name=name, out_shape=[pltpu.HBM(v.shape, v.dtype) for v in thru],
        in_specs=[HBM_SPEC] * (2 * n) + [SEM_SPEC] * (2 * n) + [pl.BlockSpec(memory_space=pl.ANY)],
        out_specs=[HBM_SPEC] * (2 * n), input_output_aliases={i: i for i in range(2 * n)},
        compiler_params=pltpu.CompilerParams(has_side_effects=pltpu.SideEffectType.DATAFLOW_SIDE_EFFECTING))(
            *thru, *sems, after)
    return res[n:]


def _adam(parts, w, m, v, name):
    def body(p_ref, w_ref, m_ref, v_ref, g_ref, d_ref, nm_ref, nv_ref):
        g = _sum_parts(p_ref)
        g_ref[...] = g
        d_ref[...], nm_ref[...], nv_ref[...] = _adam_math(g, w_ref[...], m_ref[...], v_ref[...])

    return pl.pallas_call(
        body, name=name, out_shape=[_sds(w.shape)] * 4,
        compiler_params=pltpu.CompilerParams(vmem_limit_bytes=VMEM_LIMIT))(parts, w, m, v)


GATHER_GROUPS = (("w_in", "w_out"), ("xa_wkv", "xa_wq", "xa_wo"), ("ffn_up", "ffn_conv", "ffn_down"))
SCATTER_GROUPS = (("ffn_down", "ffn_up", "ffn_conv"), ("xa_wo", "xa_wq", "xa_wkv", "w_out"), (), ("w_in",))
BIG = tuple(n for grp in GATHER_GROUPS for n in grp)
BY_COLUMN = ("w_in", "xa_wkv", "ffn_up")
VECS = (("mix_norm", D), ("q_norm", HD), ("k_norm", HD), ("attn_sinks", 8), ("gmlp_v_norm", GW), ("attn_out_norm", AW),
        ("gmlp_out_norm", GW), ("xa_norm", D), ("mem_norm", D), ("xa_q_norm", XD), ("xa_k_norm", XD), ("ffn_norm", D))
BS_ROW = 16
VEC_ROWS = 24
SMALL = tuple(n for n, _ in VECS) + ("gmlp_bs", "gmlp_ws", "ffn_conv_b")


def _pack_small(raw):
    names = [n for n, _ in VECS] + ["gmlp_bs", "conv_sums"]

    def body(*refs):
        ins = dict(zip(names, refs))
        vec_ref, cb_ref = refs[len(names):]
        vec_ref[...] = jnp.zeros_like(vec_ref)
        for r, (n, w) in enumerate(VECS):
            vec_ref[r:r + 1, 0:w] = ins[n][0:1, 0:w]
        vec_ref[BS_ROW:BS_ROW + 8, 0:BLK] = ins["gmlp_bs"][...]
        for s in range(2):
            for d in range(NG):
                cb_ref[s, d] = ins["conv_sums"][s, d, 3:4, :]

    return pl.pallas_call(body, name="pack_small", out_shape=[_sds((VEC_ROWS, D)), _sds((2, NG, 1, SW))])(
        *[raw[n] for n in names])


def _adam_math(g, w, m, v):
    nm = B1 * m + (1.0 - B1) * g
    nv = B2 * v + (1.0 - B2) * (g * g)
    m_hat = nm / (1.0 - B1 ** STEP)
    v_hat = nv / (1.0 - B2 ** STEP)
    return -LR * (m_hat / (jnp.sqrt(v_hat) + AEPS) + WD * w), nm, nv


def _sum_parts(p_ref):
    g = p_ref[0].astype(F32)
    for j in range(1, NDEV):
        g = g + p_ref[j].astype(F32)
    return g


def _adam_small(parts_vec, parts_ws, parts_cb, w, m, v):
    def body(*refs):
        pv_ref, pws_ref, pcb_ref = refs[:3]
        ins = refs[3:3 + 3 * len(SMALL)]
        outs = refs[3 + 3 * len(SMALL):]
        gv = _sum_parts(pv_ref)
        for j, n in enumerate(SMALL):
            w_ref, m_ref, v_ref = ins[3 * j:3 * j + 3]
            o = outs[4 * j:4 * j + 4]
            if n == "gmlp_ws":
                g = _sum_parts(pws_ref)
            elif n == "ffn_conv_b":
                g = _sum_parts(pcb_ref)
            elif n == "gmlp_bs":
                g = gv[BS_ROW:BS_ROW + 8, 0:BLK]
            else:
                g = gv[j:j + 1, 0:VECS[j][1]]
            lead = n in ("gmlp_ws", "gmlp_bs")
            res = (g,) + _adam_math(g, w_ref[0] if lead else w_ref[...], m_ref[0] if lead else m_ref[...],
                                    v_ref[0] if lead else v_ref[...])
            for o_ref, val in zip(o, res):
                if lead:
                    o_ref[0] = val
                else:
                    o_ref[...] = val

    args = [parts_vec, parts_ws, parts_cb] + [d[n] for n in SMALL for d in (w, m, v)]
    res = pl.pallas_call(body, name="adam_small", out_shape=[_sds(w[n].shape) for n in SMALL for _ in range(4)],
                         compiler_params=pltpu.CompilerParams(vmem_limit_bytes=VMEM_LIMIT))(*args)
    return {n: tuple(res[4 * j:4 * j + 4]) for j, n in enumerate(SMALL)}


def kernel(x, mem, positions, mix_norm, w_in, q_norm, k_norm, attn_sinks, gmlp_v_norm, gmlp_ws, gmlp_bs, attn_out_norm, gmlp_out_norm, w_out, xa_norm, mem_norm, xa_wq, xa_wkv, xa_q_norm, xa_k_norm, xa_wo, ffn_norm, ffn_up, ffn_conv, ffn_conv_b, ffn_down, loss_target, m_mix_norm, m_w_in, m_q_norm, m_k_norm, m_attn_sinks, m_gmlp_v_norm, m_gmlp_ws, m_gmlp_bs, m_attn_out_norm, m_gmlp_out_norm, m_w_out, m_xa_norm, m_mem_norm, m_xa_wq, m_xa_wkv, m_xa_q_norm, m_xa_k_norm, m_xa_wo, m_ffn_norm, m_ffn_up, m_ffn_conv, m_ffn_conv_b, m_ffn_down, v_mix_norm, v_w_in, v_q_norm, v_k_norm, v_attn_sinks, v_gmlp_v_norm, v_gmlp_ws, v_gmlp_bs, v_attn_out_norm, v_gmlp_out_norm, v_w_out, v_xa_norm, v_mem_norm, v_xa_wq, v_xa_wkv, v_xa_q_norm, v_xa_k_norm, v_xa_wo, v_ffn_norm, v_ffn_up, v_ffn_conv, v_ffn_conv_b, v_ffn_down):
    names = ("mix_norm", "w_in", "q_norm", "k_norm", "attn_sinks", "gmlp_v_norm", "gmlp_ws", "gmlp_bs", "attn_out_norm",
             "gmlp_out_norm", "w_out", "xa_norm", "mem_norm", "xa_wq", "xa_wkv", "xa_q_norm", "xa_k_norm", "xa_wo",
             "ffn_norm", "ffn_up", "ffn_conv", "ffn_conv_b", "ffn_down")
    w = dict(zip(names, (mix_norm, w_in, q_norm, k_norm, attn_sinks, gmlp_v_norm, gmlp_ws, gmlp_bs, attn_out_norm,
                         gmlp_out_norm, w_out, xa_norm, mem_norm, xa_wq, xa_wkv, xa_q_norm, xa_k_norm, xa_wo, ffn_norm,
                         ffn_up, ffn_conv, ffn_conv_b, ffn_down)))
    m = dict(zip(names, (m_mix_norm, m_w_in, m_q_norm, m_k_norm, m_attn_sinks, m_gmlp_v_norm, m_gmlp_ws, m_gmlp_bs,
                         m_attn_out_norm, m_gmlp_out_norm, m_w_out, m_xa_norm, m_mem_norm, m_xa_wq, m_xa_wkv,
                         m_xa_q_norm, m_xa_k_norm, m_xa_wo, m_ffn_norm, m_ffn_up, m_ffn_conv, m_ffn_conv_b, m_ffn_down)))
    v = dict(zip(names, (v_mix_norm, v_w_in, v_q_norm, v_k_norm, v_attn_sinks, v_gmlp_v_norm, v_gmlp_ws, v_gmlp_bs,
                         v_attn_out_norm, v_gmlp_out_norm, v_w_out, v_xa_norm, v_mem_norm, v_xa_wq, v_xa_wkv,
                         v_xa_q_norm, v_xa_k_norm, v_xa_wo, v_ffn_norm, v_ffn_up, v_ffn_conv, v_ffn_conv_b, v_ffn_down)))
    t = x.shape[1]

    me = 4 * lax.axis_index("x") + 2 * lax.axis_index("y") + lax.axis_index("c")

    def rows(a, n):
        return jnp.swapaxes(a[0], 0, 1) if n in BY_COLUMN else a[0]

    mats = [n for n in BIG if n != "ffn_conv"]
    shard = dict(zip(mats, _cast_shards([rows(w[n], n) for n in mats])), ffn_conv=w["ffn_conv"][0])
    gathers, tokens = zip(*[_exchange_start([], [shard[n] for n in grp], me, "gather_start_%d" % i)
                            for i, grp in enumerate(GATHER_GROUPS)])

    def fetch(i, after):
        after = tokens[0] + tokens[1] + tokens[2] if after is None else after
        got = dict(zip(GATHER_GROUPS[i], _exchange_wait(gathers[i], after, "gather_wait_%d" % i)))
        if "w_in" in got:
            got["w_in"] = got["w_in"].reshape(IN, D)
        for n in ("w_out", "xa_wq", "xa_wo"):
            if n in got:
                got[n] = got[n].reshape(D, D)
        if "ffn_down" in got:
            got["ffn_down"] = got["ffn_down"].reshape(NG, SW, D)
            got["ffn_conv"] = got["ffn_conv"].reshape(2, NG, 3, SW)
        return got

    scatters = []

    def ship(i, grads, for_all=()):
        by_dest = [grads[n].reshape((NDEV,) + shard[n].shape) for n in SCATTER_GROUPS[i]]
        state, token = _exchange_start(by_dest, for_all, me, "scatter_start_%d" % i)
        scatters.append(state)
        return token

    conv_b = {k: d["ffn_conv_b"].reshape(NDEV, 1, SW) for k, d in (("w", w), ("m", m), ("v", v))}
    p = {n: w[n] for n in SMALL[:-1]}
    p["gmlp_ws"], p["gmlp_bs"] = w["gmlp_ws"][0], w["gmlp_bs"][0]
    p["ffn_conv_b"] = conv_b["w"].reshape(2, NG, 1, SW)
    loss, grad_x, g, raw = _local_step(x[0], mem[0], positions.reshape(t, 1), loss_target[0], p, fetch, ship)
    loss = lax.psum(loss, AXES)

    vec, cb = _pack_small(raw)
    after = ship(3, g, [vec, cb.reshape(NDEV, 1, SW)])
    res, rest = {}, []
    for i, grp in enumerate(SCATTER_GROUPS):
        got = _exchange_wait(scatters[i], after, "scatter_wait_%d" % i)
        rest += got[len(grp):]
        for n, parts in zip(grp, got):
            out = _adam(parts, rows(w[n], n), rows(m[n], n), rows(v[n], n), "adam_" + n)
            res[n] = [jnp.swapaxes(o, 0, 1) if n in BY_COLUMN else o for o in out]
            after = out[0]
    ws_parts, vec_parts, cb_parts = rest
    small = lambda d, k: {**{n: d[n] for n in SMALL[:-1]}, "ffn_conv_b": conv_b[k]}
    res.update(_adam_small(vec_parts, ws_parts, cb_parts, small(w, "w"), small(m, "m"), small(v, "v")))

    outs = [loss, grad_x[None]]
    for j in range(4):
        outs += [res[n][j].reshape(w[n].shape) for n in names]
    return tuple(outs)
```

```python
import functools
import math

import jax
import jax.numpy as jnp
from jax import lax
from jax.experimental import pallas as pl
from jax.experimental.pallas import tpu as pltpu

F32 = jnp.float32
BF16 = jnp.bfloat16

D = 1024
HD = 64
AW = 512
KW = 128
GW = 512
IN = AW + 2 * KW + 2 * GW
BLK = 128
MEM = 256
XH = 4
XD = 256
FF = 2816
EPS = 1e-6
ROPE_THETA = 10000.0
NDEV = 8
LR, B1, B2, AEPS, WD, STEP = 0.001, 0.9, 0.999, 1e-08, 0.01, 10

TM = 512
WK = 2048
VMEM_LIMIT = 56 * 1024 * 1024
NEG = float(jnp.finfo(jnp.float32).min)
GELU_C0 = math.sqrt(2.0 / math.pi)
GELU_C1 = 0.044715
AXES = ("x", "y", "c")


def _dot(a, b):
    return jnp.dot(a, b, preferred_element_type=F32)


def _dot_nt(a, b):
    return lax.dot_general(a, b, (((1,), (1,)), ((), ())), preferred_element_type=F32)


def _dot_tn(a, b):
    return lax.dot_general(a, b, (((0,), (0,)), ((), ())), preferred_element_type=F32)


def _rs(x):
    return lax.rsqrt(jnp.mean(x * x, axis=-1, keepdims=True) + EPS)


def _rms_bwd(dy, x, r, g):
    xh = x * r
    dxh = dy * g
    dx = r * (dxh - xh * jnp.mean(dxh * xh, axis=-1, keepdims=True))
    return dx, dy * xh


def _lane(shape):
    return lax.broadcasted_iota(jnp.int32, shape, len(shape) - 1)


def _gsum64(v, ones_ref):
    w = v.shape[-1]
    ones = ones_ref[0:w, 0:w]
    hi = v.astype(BF16)
    lo = (v - hi.astype(F32)).astype(BF16)
    return _dot(hi, ones) + _dot(lo, ones)


def _head_ones():
    i = jnp.arange(AW) // HD
    return (i[:, None] == i[None, :]).astype(BF16)


def _rs64(x, ones_ref):
    return lax.rsqrt(_gsum64(x * x, ones_ref) * (1.0 / HD) + EPS)


def _rms64_bwd(dy, x, r, g, ones_ref):
    xh = x * r
    dxh = dy * g
    dx = r * (dxh - xh * (_gsum64(dxh * xh, ones_ref) * (1.0 / HD)))
    return dx, dy * xh


def _rot_half(v):
    w = v.shape[-1]
    return jnp.where((_lane(v.shape) & 32) == 0, pltpu.roll(v, w - 32, 1), pltpu.roll(v, 32, 1))


def _rope(v, cos, sin_signed):
    return v * cos + _rot_half(v) * sin_signed


def _rope_bwd(dv, cos, sin_signed):
    return dv * cos + _rot_half(dv * sin_signed)


def _gelu(z):
    return 0.5 * z * (1.0 + jnp.tanh(GELU_C0 * (z + GELU_C1 * z * z * z)))


def _gelu_grad(z):
    t = jnp.tanh(GELU_C0 * (z + GELU_C1 * z * z * z))
    return 0.5 * (1.0 + t) + 0.5 * z * (1.0 - t * t) * (GELU_C0 * (1.0 + 3.0 * GELU_C1 * z * z))


def _colsum8(v):
    s = jnp.sum(v, axis=0, keepdims=True)
    row = lax.broadcasted_iota(jnp.int32, (8, v.shape[1]), 0)
    return jnp.where(row == 0, jnp.broadcast_to(s, (8, v.shape[1])), 0.0)


def _params(n_axes=1):
    return pltpu.CompilerParams(dimension_semantics=("arbitrary",) * n_axes, vmem_limit_bytes=VMEM_LIMIT)


def _rows(tm, w):
    return pl.BlockSpec((tm, w), lambda i: (i, 0))


def _const(shape):
    nd = len(shape)
    return pl.BlockSpec(shape, lambda *_: (0,) * nd)


def _sds(shape, dtype=F32):
    return jax.ShapeDtypeStruct(shape, dtype)


def _mm_tn(a, b, name):
    g = max(a.shape[0] if a.ndim == 3 else 1, b.shape[0] if b.ndim == 3 else 1)
    t, m = a.shape[-2:]
    n = b.shape[-1]

    def body(a_ref, b_ref, o_ref, acc_ref):
        i = pl.program_id(1)

        @pl.when(i == 0)
        def _():
            acc_ref[...] = jnp.zeros_like(acc_ref)

        acc_ref[...] += _dot_tn(a_ref[...].astype(BF16), b_ref[...].astype(BF16))

        @pl.when(i == pl.num_programs(1) - 1)
        def _():
            o_ref[...] = acc_ref[...].astype(BF16)

    tk = min(t, WK)

    def spec(v):
        w = v.shape[-1]
        if v.ndim == 3:
            return pl.BlockSpec((None, tk, w), lambda j, i: (j, i, 0))
        return pl.BlockSpec((tk, w), lambda j, i: (i, 0))

    return pl.pallas_call(
        body, name=name, grid=(g, t // tk), in_specs=[spec(a), spec(b)],
        out_specs=pl.BlockSpec((None, m, n), lambda j, i: (j, 0, 0)), out_shape=_sds((g, m, n), BF16),
        scratch_shapes=[pltpu.VMEM((m, n), F32)], compiler_params=_params(2))(a, b)


def _rope_tables(pos, inv_freq):
    t = pos.shape[0]

    def body(pos_ref, f_ref, cos_ref, sin_ref):
        ang = pos_ref[...].astype(F32) * f_ref[...]
        sign = jnp.where((_lane(ang.shape) & 32) == 0, -1.0, 1.0)
        cos_ref[...] = jnp.cos(ang)
        sin_ref[...] = jnp.sin(ang) * sign

    return pl.pallas_call(
        body, name="rope_tables", grid=(t // TM,),
        in_specs=[_rows(TM, 1), _const((1, 128))], out_specs=[_rows(TM, 128), _rows(TM, 128)],
        out_shape=[_sds((t, 128)), _sds((t, 128))], compiler_params=_params())(pos, inv_freq)


def _mixer_in_fwd(x, mix_norm, w_in, qn, kn, gvw, cos, sin):
    t = x.shape[0]

    def body(x_ref, g_ref, w_ref, qn_ref, kn_ref, gvw_ref, cos_ref, sin_ref, ones_ref,
             h_ref, qk_ref, gz_ref, q_ref, k_ref, v_ref, gu_ref, gvn_ref):
        x = x_ref[...]
        h = (x * _rs(x) * g_ref[...]).astype(BF16)
        h_ref[...] = h
        proj = _dot_nt(h, w_ref[...])
        qk = proj[:, :AW + KW]
        qk_ref[...] = qk
        gz = proj[:, AW + 2 * KW:]
        gz_ref[...] = gz
        cos2, sin2 = cos_ref[...], sin_ref[...]
        q = qk[:, :AW]
        q = q * _rs64(q, ones_ref) * qn_ref[...]
        q_ref[...] = _rope(q, jnp.tile(cos2, (1, 4)), jnp.tile(sin2, (1, 4))).astype(BF16)
        k = qk[:, AW:]
        k = k * _rs64(k, ones_ref) * kn_ref[...]
        k_ref[...] = _rope(k, cos2, sin2).astype(BF16)
        v_ref[...] = proj[:, AW + KW:AW + 2 * KW].astype(BF16)
        gu_ref[...] = _gelu(gz[:, :GW])
        gv = _gelu(gz[:, GW:])
        gvn_ref[...] = (gv * _rs(gv) * gvw_ref[...]).astype(BF16)

    return pl.pallas_call(
        body, name="mixer_in_fwd", grid=(t // TM,),
        in_specs=[_rows(TM, D), _const((1, D)), _const((IN, D)), _const((1, AW)), _const((1, KW)),
                  _const((1, GW)), _rows(TM, 128), _rows(TM, 128), _const((AW, AW))],
        out_specs=[_rows(TM, D), _rows(TM, AW + KW), _rows(TM, 2 * GW), _rows(TM, AW), _rows(TM, KW),
                   _rows(TM, KW), _rows(TM, GW), _rows(TM, GW)],
        out_shape=[_sds((t, D), BF16), _sds((t, AW + KW)), _sds((t, 2 * GW)), _sds((t, AW), BF16),
                   _sds((t, KW), BF16), _sds((t, KW), BF16), _sds((t, GW)), _sds((t, GW), BF16)],
        compiler_params=_params())(x, mix_norm, w_in, qn, kn, gvw, cos, sin, _head_ones())


def _dup_half(kk, g):
    lane = _lane(kk.shape)
    other = pltpu.roll(kk, 64, 1)
    keep = (lane < 64) if g == 0 else (lane >= 64)
    return jnp.where(keep, kk, other).astype(BF16)


def _swa_mask(first_block):
    qi = lax.broadcasted_iota(jnp.int32, (4 * BLK, 2 * BLK), 0) & (BLK - 1)
    kj = lax.broadcasted_iota(jnp.int32, (4 * BLK, 2 * BLK), 1)
    diff = qi + BLK - kj
    band = (diff >= 0) & (diff < BLK)
    return band & (jnp.logical_not(first_block) | (kj >= BLK))


def _stack_heads(a2, b2):
    lo = _lane(a2.shape) < 64
    z = jnp.zeros_like(a2)
    return jnp.concatenate([jnp.where(lo, a2, z), jnp.where(lo, z, a2), jnp.where(lo, b2, z), jnp.where(lo, z, b2)], axis=0)


def _unstack_heads(o):
    lo = _lane((BLK, 128)) < 64
    return jnp.where(lo, o[0:BLK], o[BLK:2 * BLK]), jnp.where(lo, o[2 * BLK:3 * BLK], o[3 * BLK:4 * BLK])


def _sink_col(sink_ref, g):
    row = lax.broadcasted_iota(jnp.int32, (4 * BLK, 1), 0)
    s = [sink_ref[0, 4 * g + j] for j in range(4)]
    return jnp.where(row < BLK, s[0], jnp.where(row < 2 * BLK, s[1], jnp.where(row < 3 * BLK, s[2], s[3])))


def _swa_probs(qs, kd, mask, sink):
    s = _dot_nt(qs, kd) * (1.0 / math.sqrt(HD))
    s = jnp.where(mask, s, NEG)
    m = jnp.maximum(jnp.max(s, axis=-1, keepdims=True), sink)
    p = jnp.exp(s - m)
    ps = jnp.exp(sink - m)
    inv = 1.0 / (jnp.sum(p, axis=-1, keepdims=True) + ps)
    return p * inv, ps * inv


SB = 4
SB_BWD = 2


def _swa_fwd(q, k, v, sinks):
    t = q.shape[0]
    ts = min(t, SB * BLK)

    def body(sink_ref, q_ref, kc_ref, kp_ref, vc_ref, vp_ref, o_ref):
        i = pl.program_id(0)
        kk = jnp.concatenate([kp_ref[...], kc_ref[...]], axis=0).astype(F32)
        vv = jnp.concatenate([vp_ref[...], vc_ref[...]], axis=0).astype(F32)
        for b in range(ts // BLK):
            r = slice(b * BLK, (b + 1) * BLK)
            kb, vb = kk[b * BLK:(b + 2) * BLK], vv[b * BLK:(b + 2) * BLK]
            mask = _swa_mask(i == 0) if b == 0 else _swa_mask(False)
            for g in range(2):
                qs = _stack_heads(q_ref[r, 256 * g:256 * g + 128], q_ref[r, 256 * g + 128:256 * g + 256])
                pn, _ = _swa_probs(qs, _dup_half(kb, g), mask, _sink_col(sink_ref, g))
                oa, ob = _unstack_heads(_dot(pn.astype(BF16), _dup_half(vb, g)))
                o_ref[r, 256 * g:256 * g + 128] = oa
                o_ref[r, 256 * g + 128:256 * g + 256] = ob

    cur = lambda i: (i, 0)
    prev = lambda i: (jnp.maximum(i * (ts // BLK) - 1, 0), 0)
    return pl.pallas_call(
        body, name="swa_fwd", grid=(t // ts,),
        in_specs=[pl.BlockSpec(memory_space=pltpu.SMEM), pl.BlockSpec((ts, AW), cur),
                  pl.BlockSpec((ts, KW), cur), pl.BlockSpec((BLK, KW), prev),
                  pl.BlockSpec((ts, KW), cur), pl.BlockSpec((BLK, KW), prev)],
        out_specs=pl.BlockSpec((ts, AW), cur), out_shape=_sds((t, AW)),
        compiler_params=_params())(sinks, q, k, k, v, v)


def _causal_bf16(w_ref, h, transposed):
    r = lax.broadcasted_iota(jnp.int32, (BLK, BLK), 0)
    c = lax.broadcasted_iota(jnp.int32, (BLK, BLK), 1)
    keep = (r <= c) if transposed else (c <= r)
    return jnp.where(keep, w_ref[h], 0.0).astype(BF16)


def _gmlp_mix(w_ref, xin, transposed):
    lo = _lane((BLK, 128)) < 64
    wm = [_causal_bf16(w_ref, h, transposed) for h in range(8)]
    rows = []
    for c in range(xin.shape[0] // BLK):
        cols = []
        for j in range(4):
            xs = xin[c * BLK:(c + 1) * BLK, 128 * j:128 * (j + 1)]
            cols.append(jnp.where(lo, _dot(wm[2 * j], xs), _dot(wm[2 * j + 1], xs)))
        rows.append(jnp.concatenate(cols, axis=1))
    return jnp.concatenate(rows, axis=0)


def _gmlp_fwd(gvn, gu, ws, bfull):
    t = gvn.shape[0]

    def body(x_ref, gu_ref, w_ref, b_ref, o_ref):
        mixed = _gmlp_mix(w_ref, x_ref[...], False) + jnp.tile(b_ref[...], (TM // BLK, 1))
        o_ref[...] = gu_ref[...] * mixed

    return pl.pallas_call(
        body, name="gmlp_fwd", grid=(t // TM,),
        in_specs=[_rows(TM, GW), _rows(TM, GW), _const((8, BLK, BLK)), _const((BLK, GW))],
        out_specs=_rows(TM, GW), out_shape=_sds((t, GW)), compiler_params=_params())(gvn, gu, ws, bfull)


def _mixer_out_fwd(attn, gm, x, w_out, aon, gon, xan):
    t = x.shape[0]

    def body(a_ref, g_ref, x_ref, w_ref, aon_ref, gon_ref, xan_ref, y_ref, x1_ref, h2_ref):
        a, g = a_ref[...], g_ref[...]
        y = jnp.concatenate([a * _rs(a) * aon_ref[...], g * _rs(g) * gon_ref[...]], axis=1).astype(BF16)
        y_ref[...] = y
        x1 = x_ref[...] + _dot(y, w_ref[...])
        x1_ref[...] = x1
        h2_ref[...] = (x1 * _rs(x1) * xan_ref[...]).astype(BF16)

    return pl.pallas_call(
        body, name="mixer_out_fwd", grid=(t // TM,),
        in_specs=[_rows(TM, AW), _rows(TM, GW), _rows(TM, D), _const((D, D)), _const((1, AW)), _const((1, GW)),
                  _const((1, D))],
        out_specs=[_rows(TM, D), _rows(TM, D), _rows(TM, D)],
        out_shape=[_sds((t, D), BF16), _sds((t, D)), _sds((t, D), BF16)],
        compiler_params=_params())(attn, gm, x, w_out, aon, gon, xan)


def _mem_kv_fwd(mem, mem_norm, wkv, kn4):
    def body(m_ref, g_ref, w_ref, kn_ref, mh_ref, kpre_ref, k_ref, v_ref):
        m = m_ref[...]
        mh = (m * _rs(m) * g_ref[...]).astype(BF16)
        mh_ref[...] = mh
        for h in range(XH):
            sl = slice(XD * h, XD * (h + 1))
            kh = _dot(mh, w_ref[h])
            kpre_ref[:, sl] = kh
            k_ref[:, sl] = (kh * _rs(kh) * kn_ref[:, sl]).astype(BF16)
            v_ref[:, sl] = _dot(mh, w_ref[XH + h]).astype(BF16)

    return pl.pallas_call(
        body, name="mem_kv_fwd",
        out_shape=[_sds((MEM, D), BF16), _sds((MEM, D)), _sds((MEM, D), BF16), _sds((MEM, D), BF16)],
        compiler_params=pltpu.CompilerParams(vmem_limit_bytes=VMEM_LIMIT))(mem, mem_norm, wkv, kn4)


def _xattn_probs(qpre_h, qn_h, k_h):
    rq = _rs(qpre_h)
    q2 = (qpre_h * rq * qn_h).astype(BF16)
    s = _dot_nt(q2, k_h) * (1.0 / math.sqrt(XD))
    p = jnp.exp(s - jnp.max(s, axis=-1, keepdims=True))
    return p * (1.0 / jnp.sum(p, axis=-1, keepdims=True)), q2, rq


def _xattn_fwd(h2, x1, wq, qn4, k2, v2, wo, ffn_norm):
    t = x1.shape[0]

    def body(h_ref, x_ref, wq_ref, qn_ref, k_ref, v_ref, wo_ref, fn_ref, qpre_ref, o_ref, x2_ref, h3_ref):
        qpre = _dot(h_ref[...], wq_ref[...])
        qpre_ref[...] = qpre
        outs = []
        for h in range(XH):
            sl = slice(XD * h, XD * (h + 1))
            pn, _, _ = _xattn_probs(qpre[:, sl], qn_ref[:, sl], k_ref[:, sl])
            outs.append(_dot(pn.astype(BF16), v_ref[:, sl]))
        o = jnp.concatenate(outs, axis=1).astype(BF16)
        o_ref[...] = o
        x2 = x_ref[...] + _dot(o, wo_ref[...])
        x2_ref[...] = x2
        h3_ref[...] = (x2 * _rs(x2) * fn_ref[...]).astype(BF16)

    return pl.pallas_call(
        body, name="xattn_fwd", grid=(t // TM,),
        in_specs=[_rows(TM, D), _rows(TM, D), _const((D, D)), _const((1, D)), _const((MEM, D)), _const((MEM, D)),
                  _const((D, D)), _const((1, D))],
        out_specs=[_rows(TM, D)] * 4,
        out_shape=[_sds((t, D)), _sds((t, D), BF16), _sds((t, D)), _sds((t, D), BF16)],
        compiler_params=_params())(h2, x1, wq, qn4, k2, v2, wo, ffn_norm)


SW = 704
NG = FF // SW
FM = 256
HALO = 16


def _resident(shape):
    nd = len(shape)
    return pl.BlockSpec(shape, lambda *_: (0,) * nd, pipeline_mode=pl.Buffered(1))


def _halo_before(i):
    return jnp.maximum(i * (FM // HALO) - 1, 0)


def _conv(e, w):
    return w[2:3, :] * e + pltpu.roll(w[1:2, :] * e + pltpu.roll(w[0:1, :] * e, 1, 0), 1, 0)


def _conv_t(dc, w):
    n = dc.shape[0]
    return w[2:3, :] * dc + pltpu.roll(w[1:2, :] * dc + pltpu.roll(w[0:1, :] * dc, n - 1, 0), n - 1, 0)


def _ffn_fwd(h3, x2, target, up, conv, conv_b, down):
    t = x2.shape[0]

    def body(h_ref, hp_ref, x_ref, t_ref, up_ref, w_ref, b_ref, dn_ref, a_ref, u_ref, gs_ref, dy_ref, loss_ref, acc_ref):
        i = pl.program_id(0)

        @pl.when(i == 0)
        def _():
            acc_ref[...] = jnp.zeros_like(acc_ref)

        before = jnp.where(i > 0, hp_ref[...], jnp.zeros_like(hp_ref))
        he = jnp.concatenate([before, h_ref[...]], axis=0)
        err = x_ref[...] - t_ref[...]
        for d in range(NG):
            c = []
            for s in range(2):
                a = _dot_nt(he, up_ref[s * NG + d])
                a_ref[s * NG + d] = a[HALO:].astype(BF16)
                c.append(_conv(a, w_ref[s, d])[HALO:] + b_ref[s, d])
            gl, gg = _gelu_and_grad(c[0])
            gs_ref[d] = gl.astype(BF16)
            gs_ref[NG + d] = (gg * c[1]).astype(BF16)
            u = (gl * c[1]).astype(BF16)
            u_ref[d] = u
            err = err + _dot(u, dn_ref[d])
        dy_ref[...] = err * (1.0 / D)
        acc_ref[...] += jnp.sum(err * err, axis=0, keepdims=True)

        @pl.when(i == pl.num_programs(0) - 1)
        def _():
            loss_ref[...] = jnp.full((8, 128), 0.5 / D, F32) * jnp.sum(acc_ref[...])

    return pl.pallas_call(
        body, name="ffn_fwd", grid=(t // FM,),
        in_specs=[_rows(FM, D), pl.BlockSpec((HALO, D), lambda i: (_halo_before(i), 0)), _rows(FM, D), _rows(FM, D),
                  _resident((NDEV, SW, D)), _resident((2, NG, 3, SW)), _resident((2, NG, 1, SW)), _resident((NG, SW, D))],
        out_specs=[pl.BlockSpec((NDEV, FM, SW), lambda i: (0, i, 0)), pl.BlockSpec((NG, FM, SW), lambda i: (0, i, 0)),
                   pl.BlockSpec((NDEV, FM, SW), lambda i: (0, i, 0)), _rows(FM, D), _const((8, 128))],
        out_shape=[_sds((NDEV, t, SW), BF16), _sds((NG, t, SW), BF16), _sds((NDEV, t, SW), BF16), _sds((t, D)),
                   _sds((8, 128))],
        scratch_shapes=[pltpu.VMEM((1, D), F32)], compiler_params=_params())(h3, h3, x2, target, up, conv, conv_b, down)


def _gelu_and_grad(z):
    z2 = z * z
    t = jnp.tanh(GELU_C0 * (z + GELU_C1 * z * z2))
    phi = 0.5 * (1.0 + t)
    return z * phi, phi + z * (1.0 - t * t) * (0.5 * GELU_C0 + (1.5 * GELU_C0 * GELU_C1) * z2)


def _ffn_bwd(dy, a, gs, x2, up, conv, down, ffn_norm):
    t = x2.shape[0]
    nt = t // FM
    n = FM + HALO

    def body(dy_ref, dyn_ref, a_ref, gs_ref, gsn_ref, x_ref, up_ref, w_ref, dn_ref, g_ref,
             dx_ref, da_ref, s_ref, dfn_ref):
        i = pl.program_id(0)

        @pl.when(i == 0)
        def _():
            s_ref[...] = jnp.zeros_like(s_ref)
            dfn_ref[...] = jnp.zeros_like(dfn_ref)

        last = i == nt - 1
        dy = dy_ref[...]
        dye = jnp.concatenate([dy, jnp.where(last, 0.0, dyn_ref[...])], axis=0).astype(BF16)
        dh = jnp.zeros((FM, D), F32)
        row = lax.broadcasted_iota(jnp.int32, (8, SW), 0)
        for d in range(NG):
            du = _dot_nt(dye, dn_ref[d])
            for s in range(2):
                j = s * NG + d
                k = NG + d if s == 0 else d
                dc = du * jnp.concatenate([gs_ref[k], gsn_ref[k]], axis=0).astype(F32)
                w = w_ref[s, d]
                tile = a_ref[j].astype(F32)
                d1 = pltpu.roll(dc, n - 1, 0)
                d2 = pltpu.roll(d1, n - 1, 0)
                da = (w[2:3, :] * dc + w[1:2, :] * d1 + w[0:1, :] * d2)[0:FM].astype(BF16)
                da_ref[j] = da
                dh = dh + _dot(da, up_ref[j])
                sums = [jnp.sum(v[0:FM] * tile, axis=0, keepdims=True) for v in (d2, d1, dc)]
                sums.append(jnp.sum(dc[0:FM], axis=0, keepdims=True))
                upd = jnp.zeros((8, SW), F32)
                for r, v in enumerate(sums):
                    upd = jnp.where(row == r, jnp.broadcast_to(v, (8, SW)), upd)
                s_ref[s, d] += upd
        x = x_ref[...]
        dx, dg = _rms_bwd(dh, x, _rs(x), g_ref[...])
        dx_ref[...] = dy + dx
        dfn_ref[...] += _colsum8(dg)

    last_halo = t // HALO - 1
    after = lambda i: jnp.minimum((i + 1) * (FM // HALO), last_halo)
    return pl.pallas_call(
        body, name="ffn_bwd", grid=(nt,),
        in_specs=[_rows(FM, D), pl.BlockSpec((HALO, D), lambda i: (after(i), 0)),
                  pl.BlockSpec((NDEV, FM, SW), lambda i: (0, i, 0)),
                  pl.BlockSpec((NDEV, FM, SW), lambda i: (0, i, 0)),
                  pl.BlockSpec((NDEV, HALO, SW), lambda i: (0, after(i), 0)),
                  _rows(FM, D), _resident((NDEV, SW, D)), _resident((2, NG, 3, SW)), _resident((NG, SW, D)), _const((1, D))],
        out_specs=[_rows(FM, D), pl.BlockSpec((NDEV, FM, SW), lambda i: (0, i, 0)), _const((2, NG, 8, SW)), _const((8, D))],
        out_shape=[_sds((t, D)), _sds((NDEV, t, SW), BF16), _sds((2, NG, 8, SW)), _sds((8, D))],
        compiler_params=_params())(dy, dy, a, gs, gs, x2, up, conv, down, ffn_norm)


BT = 512


def _xattn_bwd(dx2, x1, qpre, k2, v2, wq, wo, qn4, xan):
    t = x1.shape[0]

    def body(dx2_ref, x1_ref, qpre_ref, k_ref, v_ref, wq_ref, wo_ref, qn_ref, xan_ref,
             dx1_ref, dqpre_ref, dk_ref, dv_ref, dqn_ref, dxan_ref):
        @pl.when(pl.program_id(0) == 0)
        def _():
            for r in (dk_ref, dv_ref, dqn_ref, dxan_ref):
                r[...] = jnp.zeros_like(r)

        dx2 = dx2_ref[...]
        do = _dot_nt(dx2.astype(BF16), wo_ref[...])
        dqs = []
        for h in range(XH):
            sl = slice(XD * h, XD * (h + 1))
            qpre_h = qpre_ref[:, sl]
            pn, q2, rq = _xattn_probs(qpre_h, qn_ref[:, sl], k_ref[:, sl])
            do_h = do[:, sl].astype(BF16)
            dp = _dot_nt(do_h, v_ref[:, sl])
            ds = (pn * (dp - jnp.sum(pn * dp, axis=-1, keepdims=True)) * (1.0 / math.sqrt(XD))).astype(BF16)
            dq2 = _dot(ds, k_ref[:, sl])
            dk_ref[:, sl] += _dot_tn(ds, q2)
            dv_ref[:, sl] += _dot_tn(pn.astype(BF16), do_h)
            dqh, dg = _rms_bwd(dq2, qpre_h, rq, qn_ref[:, sl])
            dqn_ref[...] += _colsum8(dg)
            dqs.append(dqh)
        dqpre = jnp.concatenate(dqs, axis=1).astype(BF16)
        dqpre_ref[...] = dqpre
        dh2 = _dot_nt(dqpre, wq_ref[...])
        x1 = x1_ref[...]
        dx, dg = _rms_bwd(dh2, x1, _rs(x1), xan_ref[...])
        dx1_ref[...] = dx2 + dx
        dxan_ref[...] += _colsum8(dg)

    return pl.pallas_call(
        body, name="xattn_bwd", grid=(t // BT,),
        in_specs=[_rows(BT, D), _rows(BT, D), _rows(BT, D), _const((MEM, D)), _const((MEM, D)), _const((D, D)),
                  _const((D, D)), _const((1, D)), _const((1, D))],
        out_specs=[_rows(BT, D), _rows(BT, D), _const((MEM, D)), _const((MEM, D)), _const((8, XD)), _const((8, D))],
        out_shape=[_sds((t, D)), _sds((t, D), BF16), _sds((MEM, D)), _sds((MEM, D)), _sds((8, XD)), _sds((8, D))],
        compiler_params=_params())(dx2, x1, qpre, k2, v2, wq, wo, qn4, xan)


def _mem_kv_bwd(mem, mh, kpre, dk2, dv2, wkv, kn4, mem_norm):
    def body(m_ref, mh_ref, kpre_ref, dk_ref, dv_ref, w_ref, kn_ref, g_ref, dw_ref, dkn_ref, dmn_ref):
        dkn = jnp.zeros((8, XD), F32)
        dm = jnp.zeros((MEM, D), F32)
        mh = mh_ref[...]
        for h in range(XH):
            sl = slice(XD * h, XD * (h + 1))
            kh = kpre_ref[:, sl]
            dkh, dg = _rms_bwd(dk_ref[:, sl], kh, _rs(kh), kn_ref[:, sl])
            dkn = dkn + _colsum8(dg)
            dkh = dkh.astype(BF16)
            dvh = dv_ref[:, sl].astype(BF16)
            dw_ref[h] = _dot_tn(mh, dkh).astype(BF16)
            dw_ref[XH + h] = _dot_tn(mh, dvh).astype(BF16)
            dm = dm + _dot_nt(dkh, w_ref[h]) + _dot_nt(dvh, w_ref[XH + h])
        dkn_ref[...] = dkn
        m = m_ref[...]
        _, dg = _rms_bwd(dm, m, _rs(m), g_ref[...])
        dmn_ref[...] = _colsum8(dg)

    return pl.pallas_call(
        body, name="mem_kv_bwd", out_shape=[_sds((2 * XH, D, XD), BF16), _sds((8, XD)), _sds((8, D))],
        compiler_params=pltpu.CompilerParams(vmem_limit_bytes=VMEM_LIMIT))(mem, mh, kpre, dk2, dv2, wkv, kn4, mem_norm)


def _mixer_out_bwd(dx1, attn, gm, w_out, aon, gon):
    t = dx1.shape[0]

    def body(dx_ref, a_ref, g_ref, w_ref, aon_ref, gon_ref, da_ref, dg_ref, dan_ref, dgn_ref):
        @pl.when(pl.program_id(0) == 0)
        def _():
            dan_ref[...] = jnp.zeros_like(dan_ref)
            dgn_ref[...] = jnp.zeros_like(dgn_ref)

        dy = _dot_nt(dx_ref[...].astype(BF16), w_ref[...])
        a, g = a_ref[...], g_ref[...]
        da, dna = _rms_bwd(dy[:, :AW], a, _rs(a), aon_ref[...])
        dg, dng = _rms_bwd(dy[:, AW:], g, _rs(g), gon_ref[...])
        da_ref[...] = da
        dg_ref[...] = dg
        dan_ref[...] += _colsum8(dna)
        dgn_ref[...] += _colsum8(dng)

    return pl.pallas_call(
        body, name="mixer_out_bwd", grid=(t // TM,),
        in_specs=[_rows(TM, D), _rows(TM, AW), _rows(TM, GW), _const((D, D)), _const((1, AW)), _const((1, GW))],
        out_specs=[_rows(TM, AW), _rows(TM, GW), _const((8, AW)), _const((8, GW))],
        out_shape=[_sds((t, AW)), _sds((t, GW)), _sds((8, AW)), _sds((8, GW))],
        compiler_params=_params())(dx1, attn, gm, w_out, aon, gon)


def _gmlp_bwd(dgm, gu, gvn, gz, ws, wst, bfull, gvw):
    t = dgm.shape[0]
    nc = TM // BLK

    def body(dgm_ref, gu_ref, x_ref, gz_ref, w_ref, wt_ref, b_ref, gvw_ref, dgz_ref, dw_ref, db_ref, dgvw_ref,
             dbacc_ref):
        @pl.when(pl.program_id(0) == 0)
        def _():
            for r in (dw_ref, dbacc_ref, dgvw_ref):
                r[...] = jnp.zeros_like(r)

        xin = x_ref[...]
        dgm = dgm_ref[...]
        mixed = _gmlp_mix(w_ref, xin, False) + jnp.tile(b_ref[...], (nc, 1))
        dgu = dgm * mixed
        dmixed = dgm * gu_ref[...]
        lo = _lane((BLK, 128)) < 64
        dbias = jnp.zeros((BLK, GW), F32)
        for c in range(nc):
            dmc = dmixed[c * BLK:(c + 1) * BLK]
            dbias = dbias + dmc
            for j in range(4):
                dm2 = dmc[:, 128 * j:128 * (j + 1)]
                xs = xin[c * BLK:(c + 1) * BLK, 128 * j:128 * (j + 1)]
                z = jnp.zeros_like(dm2)
                dw_ref[2 * j] += _dot_nt(jnp.where(lo, dm2, z).astype(BF16), xs)
                dw_ref[2 * j + 1] += _dot_nt(jnp.where(lo, z, dm2).astype(BF16), xs)
        dbacc_ref[...] += dbias
        dgvn = _gmlp_mix(wt_ref, dmixed.astype(BF16), True)
        gz_u, gz_v = gz_ref[:, :GW], gz_ref[:, GW:]
        gv = _gelu(gz_v)
        dgv, dg = _rms_bwd(dgvn, gv, _rs(gv), gvw_ref[...])
        dgvw_ref[...] += _colsum8(dg)
        dgz_ref[:, :GW] = (dgu * _gelu_grad(gz_u)).astype(BF16)
        dgz_ref[:, GW:] = (dgv * _gelu_grad(gz_v)).astype(BF16)

        @pl.when(pl.program_id(0) == pl.num_programs(0) - 1)
        def _():
            s = dbacc_ref[...]
            sel = (lax.broadcasted_iota(jnp.int32, (8, GW), 1) // HD
                   == lax.broadcasted_iota(jnp.int32, (8, GW), 0)).astype(BF16)
            hi = s.astype(BF16)
            r1 = s - hi.astype(F32)
            mid = r1.astype(BF16)
            lo = (r1 - mid.astype(F32)).astype(BF16)
            db_ref[...] = _dot_nt(sel, hi) + _dot_nt(sel, mid) + _dot_nt(sel, lo)
            r = lax.broadcasted_iota(jnp.int32, (BLK, BLK), 0)
            c = lax.broadcasted_iota(jnp.int32, (BLK, BLK), 1)
            for h in range(8):
                dw_ref[h] = jnp.where(c <= r, dw_ref[h], 0.0)

    return pl.pallas_call(
        body, name="gmlp_bwd", grid=(t // TM,),
        in_specs=[_rows(TM, GW), _rows(TM, GW), _rows(TM, GW), _rows(TM, 2 * GW), _const((8, BLK, BLK)),
                  _const((8, BLK, BLK)), _const((BLK, GW)), _const((1, GW))],
        out_specs=[_rows(TM, 2 * GW), _const((8, BLK, BLK)), _const((8, BLK)), _const((8, GW))],
        out_shape=[_sds((t, 2 * GW), BF16), _sds((8, BLK, BLK)), _sds((8, BLK)), _sds((8, GW))],
        scratch_shapes=[pltpu.VMEM((BLK, GW), F32)],
        compiler_params=_params())(dgm, gu, gvn, gz, ws, wst, bfull, gvw)


def _fold_half(v):
    return v + pltpu.roll(v, 64, 1)


def _swa_bwd(q, k, v, dattn, sinks):
    t = q.shape[0]
    nb = t // BLK
    ts = min(t, SB_BWD * BLK)
    sb = ts // BLK
    nt = t // ts

    def body(sink_ref, q_ref, kc_ref, kp_ref, vc_ref, vp_ref, do_ref, dq_ref, dk_ref, dv_ref, ds_ref,
             ck_ref, cv_ref, sacc_ref):
        i = pl.program_id(0)

        @pl.when(i == 0)
        def _():
            ck_ref[...] = jnp.zeros_like(ck_ref)
            cv_ref[...] = jnp.zeros_like(cv_ref)
            sacc_ref[...] = jnp.zeros_like(sacc_ref)

        @pl.when(i < nt)
        def _():
            kk = jnp.concatenate([kp_ref[...], kc_ref[...]], axis=0).astype(F32)
            vv = jnp.concatenate([vp_ref[...], vc_ref[...]], axis=0).astype(F32)
            lo256 = _lane((2 * BLK, 128)) < 64
            acc_k = [jnp.zeros((BLK, 128), F32) for _ in range(sb + 1)]
            acc_v = [jnp.zeros((BLK, 128), F32) for _ in range(sb + 1)]
            for b in range(sb):
                r = slice(b * BLK, (b + 1) * BLK)
                kb, vb = kk[b * BLK:(b + 2) * BLK], vv[b * BLK:(b + 2) * BLK]
                mask = _swa_mask(i == 0) if b == 0 else _swa_mask(False)
                dkk = jnp.zeros((2 * BLK, 128), F32)
                dvv = jnp.zeros((2 * BLK, 128), F32)
                for g in range(2):
                    qs = _stack_heads(q_ref[r, 256 * g:256 * g + 128], q_ref[r, 256 * g + 128:256 * g + 256])
                    dos = _stack_heads(do_ref[r, 256 * g:256 * g + 128],
                                       do_ref[r, 256 * g + 128:256 * g + 256]).astype(BF16)
                    kd = _dup_half(kb, g)
                    pn, psn = _swa_probs(qs, kd, mask, _sink_col(sink_ref, g))
                    dp = _dot_nt(dos, _dup_half(vb, g))
                    dd = jnp.sum(pn * dp, axis=-1, keepdims=True)
                    ds = (pn * (dp - dd) * (1.0 / math.sqrt(HD))).astype(BF16)
                    sacc_ref[g] += jnp.broadcast_to(-psn * dd, (4 * BLK, 128))
                    dqa, dqb = _unstack_heads(_dot(ds, kd))
                    dq_ref[r, 256 * g:256 * g + 128] = dqa
                    dq_ref[r, 256 * g + 128:256 * g + 256] = dqb
                    dkg = _fold_half(_dot_tn(ds, qs))
                    dvg = _fold_half(_dot_tn(pn.astype(BF16), dos))
                    keep = lo256 if g == 0 else jnp.logical_not(lo256)
                    dkk = jnp.where(keep, dkg, dkk)
                    dvv = jnp.where(keep, dvg, dvv)
                acc_k[b], acc_k[b + 1] = acc_k[b] + dkk[0:BLK], acc_k[b + 1] + dkk[BLK:]
                acc_v[b], acc_v[b + 1] = acc_v[b] + dvv[0:BLK], acc_v[b + 1] + dvv[BLK:]
            for out_ref, c_ref, acc in ((dk_ref, ck_ref, acc_k), (dv_ref, cv_ref, acc_v)):
                if sb > 1:
                    out_ref[0:ts - BLK] = c_ref[0:ts - BLK]
                out_ref[ts - BLK:ts] = c_ref[ts - BLK:ts] + acc[0]
                for b in range(sb):
                    c_ref[b * BLK:(b + 1) * BLK] = acc[b + 1]

        @pl.when(i == nt)
        def _():
            dk_ref[...] = ck_ref[...]
            dv_ref[...] = cv_ref[...]
            lane = _lane((8, 128))
            acc = jnp.zeros((8, 128), F32)
            for g in range(2):
                for j in range(4):
                    val = jnp.sum(sacc_ref[g, j * BLK:(j + 1) * BLK, :], axis=0, keepdims=True)
                    acc = jnp.where(lane == 4 * g + j, jnp.broadcast_to(val, (8, 128)), acc)
            ds_ref[...] = acc

    cur = lambda i: (jnp.minimum(i, nt - 1), 0)
    before = lambda i: (jnp.clip(i * sb - 1, 0, nb - 1), 0)
    done = lambda i: (jnp.clip(i - 1, 0, nt - 1), 0)
    return pl.pallas_call(
        body, name="swa_bwd", grid=(nt + 1,),
        in_specs=[pl.BlockSpec(memory_space=pltpu.SMEM), pl.BlockSpec((ts, AW), cur),
                  pl.BlockSpec((ts, KW), cur), pl.BlockSpec((BLK, KW), before),
                  pl.BlockSpec((ts, KW), cur), pl.BlockSpec((BLK, KW), before), pl.BlockSpec((ts, AW), cur)],
        out_specs=[pl.BlockSpec((ts, AW), cur), pl.BlockSpec((ts, KW), done), pl.BlockSpec((ts, KW), done),
                   _const((8, 128))],
        out_shape=[_sds((t, AW)), _sds((t, KW)), _sds((t, KW)), _sds((8, 128))],
        scratch_shapes=[pltpu.VMEM((ts, KW), F32), pltpu.VMEM((ts, KW), F32), pltpu.VMEM((2, 4 * BLK, 128), F32)],
        compiler_params=_params())(sinks, q, k, k, v, v, dattn)


def _mixer_in_bwd(dq, dk, dv, dgz, qk, cos, sin, x, dx1, w_in, mix_norm, qn, kn):
    t = x.shape[0]

    def body(dq_ref, dk_ref, dv_ref, dgz_ref, qk_ref, cos_ref, sin_ref, x_ref, dx1_ref, w_ref, g_ref, qn_ref, kn_ref,
             ones_ref, gx_ref, dproj_ref, dmn_ref, dqn_ref, dkn_ref, qacc_ref, kacc_ref):
        i = pl.program_id(0)

        @pl.when(i == 0)
        def _():
            dmn_ref[...] = jnp.zeros_like(dmn_ref)
            qacc_ref[...] = jnp.zeros_like(qacc_ref)
            kacc_ref[...] = jnp.zeros_like(kacc_ref)

        cos2, sin2 = cos_ref[...], sin_ref[...]
        qpre, kpre = qk_ref[:, :AW], qk_ref[:, AW:]
        dqh = _rope_bwd(dq_ref[...], jnp.tile(cos2, (1, 4)), jnp.tile(sin2, (1, 4)))
        dqpre, dgq = _rms64_bwd(dqh, qpre, _rs64(qpre, ones_ref), qn_ref[...], ones_ref)
        dkh = _rope_bwd(dk_ref[...], cos2, sin2)
        dkpre, dgk = _rms64_bwd(dkh, kpre, _rs64(kpre, ones_ref), kn_ref[...], ones_ref)
        qacc_ref[...] += jnp.sum(dgq, axis=0, keepdims=True)
        kacc_ref[...] += jnp.sum(dgk, axis=0, keepdims=True)
        dproj = jnp.concatenate([dqpre.astype(BF16), dkpre.astype(BF16), dv_ref[...].astype(BF16), dgz_ref[...]], axis=1)
        dproj_ref[...] = dproj
        dh = _dot(dproj, w_ref[...])
        xv = x_ref[...]
        dx, dg = _rms_bwd(dh, xv, _rs(xv), g_ref[...])
        gx_ref[...] = dx1_ref[...] + dx
        dmn_ref[...] += _colsum8(dg)

        @pl.when(i == pl.num_programs(0) - 1)
        def _():
            qa = qacc_ref[...]
            q4 = qa[:, 0:128] + qa[:, 128:256] + qa[:, 256:384] + qa[:, 384:512]
            dqn_ref[...] = jnp.broadcast_to(_fold_half(q4), (8, 128))
            dkn_ref[...] = jnp.broadcast_to(_fold_half(kacc_ref[...]), (8, 128))

    return pl.pallas_call(
        body, name="mixer_in_bwd", grid=(t // TM,),
        in_specs=[_rows(TM, AW), _rows(TM, KW), _rows(TM, KW), _rows(TM, 2 * GW), _rows(TM, AW + KW), _rows(TM, 128),
                  _rows(TM, 128), _rows(TM, D), _rows(TM, D), _const((IN, D)), _const((1, D)), _const((1, AW)),
                  _const((1, KW)), _const((AW, AW))],
        out_specs=[_rows(TM, D), _rows(TM, IN), _const((8, D)), _const((8, 128)), _const((8, 128))],
        out_shape=[_sds((t, D)), _sds((t, IN), BF16), _sds((8, D)), _sds((8, 128)), _sds((8, 128))],
        scratch_shapes=[pltpu.VMEM((1, AW), F32), pltpu.VMEM((1, KW), F32)],
        compiler_params=_params())(dq, dk, dv, dgz, qk, cos, sin, x, dx1, w_in, mix_norm, qn, kn, _head_ones())


def _local_step(x, mem, pos, target, p, fetch, ship):
    t = x.shape[0]
    p = dict(p)
    p.update(fetch(0, None))
    inv_freq = 1.0 / (ROPE_THETA ** (jnp.arange(HD // 2, dtype=F32) * (2.0 / HD)))
    cos, sin = _rope_tables(pos, jnp.tile(inv_freq, 4).reshape(1, 128))
    qn = jnp.tile(p["q_norm"], (1, AW // HD))
    kn = jnp.tile(p["k_norm"], (1, KW // HD))
    qn4 = jnp.tile(p["xa_q_norm"], (1, XH))
    kn4 = jnp.tile(p["xa_k_norm"], (1, XH))
    ws = p["gmlp_ws"]
    wst = jnp.swapaxes(ws, 1, 2)
    bfull = jnp.repeat(p["gmlp_bs"].T, HD, axis=1)
    conv_b = p["ffn_conv_b"]

    h1, qk, gz, q, k, v, gu, gvn = _mixer_in_fwd(x, p["mix_norm"], p["w_in"], qn, kn, p["gmlp_v_norm"], cos, sin)
    attn = _swa_fwd(q, k, v, p["attn_sinks"])
    gm = _gmlp_fwd(gvn, gu, ws, bfull)
    ycat, x1, h2 = _mixer_out_fwd(attn, gm, x, p["w_out"], p["attn_out_norm"], p["gmlp_out_norm"], p["xa_norm"])
    p.update(fetch(1, h2))
    mh, kpre, k2, v2 = _mem_kv_fwd(mem, p["mem_norm"], p["xa_wkv"], kn4)
    qpre, o, x2, h3 = _xattn_fwd(h2, x1, p["xa_wq"], qn4, k2, v2, p["xa_wo"], p["ffn_norm"])
    p.update(fetch(2, h3))
    conv = p["ffn_conv"]
    a, u, gs, dy, loss8 = _ffn_fwd(h3, x2, target, p["ffn_up"], conv, conv_b, p["ffn_down"])

    raw = {}
    d_down = _mm_tn(u, dy, "ffn_down_bwd_w")
    dx2, da, raw["conv_sums"], raw["ffn_norm"] = _ffn_bwd(dy, a, gs, x2, p["ffn_up"], conv, p["ffn_down"], p["ffn_norm"])
    d_up = _mm_tn(da, h3, "ffn_up_bwd_w")
    token = ship(0, {"ffn_down": d_down, "ffn_up": d_up, "ffn_conv": raw["conv_sums"][:, :, 0:3]})
    dx1, dqpre, dk2, dv2, raw["xa_q_norm"], raw["xa_norm"] = _xattn_bwd(
        dx2, x1, qpre, k2, v2, p["xa_wq"], p["xa_wo"], qn4 + jnp.tile(token[0:1], (1, D // 128)), p["xa_norm"])
    d_wo = _mm_tn(o, dx2, "xa_wo_bwd_w")
    d_wq = _mm_tn(h2, dqpre, "xa_wq_bwd_w")
    d_wkv, raw["xa_k_norm"], raw["mem_norm"] = _mem_kv_bwd(mem, mh, kpre, dk2, dv2, p["xa_wkv"], kn4, p["mem_norm"])
    d_w_out = _mm_tn(ycat, dx1, "w_out_bwd_w")
    token = ship(1, {"xa_wo": d_wo, "xa_wq": d_wq, "xa_wkv": d_wkv, "w_out": d_w_out})
    dattn, dgm, raw["attn_out_norm"], raw["gmlp_out_norm"] = _mixer_out_bwd(
        dx1, attn, gm, p["w_out"], p["attn_out_norm"] + jnp.tile(token[0:1], (1, AW // 128)), p["gmlp_out_norm"])
    dgz, raw["gmlp_ws"], raw["gmlp_bs"], raw["gmlp_v_norm"] = _gmlp_bwd(dgm, gu, gvn, gz, ws, wst, bfull, p["gmlp_v_norm"])
    token = ship(2, {}, [raw["gmlp_ws"]])
    dq, dk, dv, raw["attn_sinks"] = _swa_bwd(q, k, v, dattn, p["attn_sinks"] + token[0:1, 0:8])
    grad_x, dproj, raw["mix_norm"], raw["q_norm"], raw["k_norm"] = _mixer_in_bwd(
        dq, dk, dv, dgz, qk, cos, sin, x, dx1, p["w_in"], p["mix_norm"], qn, kn)
    d_w_in = _mm_tn(dproj, h1, "w_in_bwd_w")
    return loss8[0, 0], grad_x, {"w_in": d_w_in}, raw


def _cast_shards(shards):
    def body(*refs):
        n = len(refs) // 2
        for i_ref, o_ref in zip(refs[:n], refs[n:]):
            o_ref[...] = i_ref[...].astype(BF16)

    return pl.pallas_call(body, name="cast_shards", out_shape=[_sds(s.shape, BF16) for s in shards],
                          compiler_params=pltpu.CompilerParams(vmem_limit_bytes=VMEM_LIMIT))(*shards)


HBM_SPEC = pl.BlockSpec(memory_space=pltpu.HBM)
SEM_SPEC = pl.BlockSpec(memory_space=pltpu.SEMAPHORE)


def _remote_copies(src_refs, land_refs, send_refs, recv_refs, nd):
    x, y, cc = lax.axis_index("x"), lax.axis_index("y"), lax.axis_index("c")
    me = 4 * x + 2 * y + cc
    copies = []
    for a, (src_ref, land_ref) in enumerate(zip(src_refs, land_refs)):
        for k in range(1, NDEV):
            px = 1 - x if k & 4 else x
            py = 1 - y if k & 2 else y
            pc = 1 - cc if k & 1 else cc
            copies.append(pltpu.make_async_remote_copy(
                src_ref=src_ref.at[4 * px + 2 * py + pc] if a < nd else src_ref, dst_ref=land_ref.at[me],
                send_sem=send_refs[a].at[k - 1], recv_sem=recv_refs[a].at[k - 1],
                device_id=(px, py, pc), device_id_type=pl.DeviceIdType.MESH))
    return copies


def _own_slot(src, by_dest, me):
    block = lax.dynamic_index_in_dim(src, me, 0, keepdims=True) if by_dest else src[None]
    return lax.dynamic_update_index_in_dim(lax.empty((NDEV,) + block.shape[1:], src.dtype), block, me, 0)


def _exchange_start(by_dest, for_all, me, name):
    srcs = list(by_dest) + list(for_all)
    n, nd = len(srcs), len(by_dest)
    lands = [_own_slot(s, a < nd, me) for a, s in enumerate(srcs)]

    def body(*refs):
        for cp in _remote_copies(refs[:n], refs[n:2 * n], refs[2 * n:3 * n], refs[3 * n:4 * n], nd):
            cp.start()
        refs[-1][...] = jnp.zeros((8, 128), F32)

    sems = [pltpu.SemaphoreType.DMA((NDEV - 1,))] * (2 * n)
    thru = [pltpu.HBM(v.shape, v.dtype) for v in srcs + lands]
    res = pl.pallas_call(
        body, name=name, out_shape=sems + thru + [_sds((8, 128))],
        in_specs=[HBM_SPEC] * (2 * n), out_specs=[SEM_SPEC] * (2 * n) + [HBM_SPEC] * (2 * n) + [pl.BlockSpec(memory_space=pltpu.VMEM)],
        input_output_aliases={i: 2 * n + i for i in range(2 * n)},
        compiler_params=pltpu.CompilerParams(has_side_effects=pltpu.SideEffectType.DATAFLOW_SIDE_EFFECTING))(
            *[pltpu.with_memory_space_constraint(v, pltpu.HBM) for v in srcs + lands])
    return (res[:2 * n], res[2 * n:4 * n], nd), res[-1]


def _exchange_wait(state, after, name):
    sems, thru, nd = state
    n = len(thru) // 2

    def body(*refs):
        for cp in _remote_copies(refs[:n], refs[n:2 * n], refs[2 * n:3 * n], refs[3 * n:4 * n], nd):
            cp.wait_send()
            cp.wait_recv()

    res = pl.pallas_call(
        body, name=name, out_shape=[pltpu.HBM(v.shape, v.dtype) for v in thru],
        in_specs=[HBM_SPEC] * (2 * n) + [SEM_SPEC] * (2 * n) + [pl.BlockSpec(memory_space=pl.ANY)],
        out_specs=[HBM_SPEC] * (2 * n), input_output_aliases={i: i for i in range(2 * n)},
        compiler_params=pltpu.CompilerParams(has_side_effects=pltpu.SideEffectType.DATAFLOW_SIDE_EFFECTING))(
            *thru, *sems, after)
    return res[n:]


def _adam(parts, w, m, v, name):
    def body(p_ref, w_ref, m_ref, v_ref, g_ref, d_ref, nm_ref, nv_ref):
        g = _sum_parts(p_ref)
        g_ref[...] = g
        d_ref[...], nm_ref[...], nv_ref[...] = _adam_math(g, w_ref[...], m_ref[...], v_ref[...])

    return pl.pallas_call(
        body, name=name, out_shape=[_sds(w.shape)] * 4,
        compiler_params=pltpu.CompilerParams(vmem_limit_bytes=VMEM_LIMIT))(parts, w, m, v)


GATHER_GROUPS = (("w_in", "w_out"), ("xa_wkv", "xa_wq", "xa_wo"), ("ffn_up", "ffn_conv", "ffn_down"))
SCATTER_GROUPS = (("ffn_down", "ffn_up", "ffn_conv"), ("xa_wo", "xa_wq", "xa_wkv", "w_out"), (), ("w_in",))
BIG = tuple(n for grp in GATHER_GROUPS for n in grp)
BY_COLUMN = ("w_in", "ffn_up")
VECS = (("mix_norm", D), ("q_norm", HD), ("k_norm", HD), ("attn_sinks", 8), ("gmlp_v_norm", GW), ("attn_out_norm", AW),
        ("gmlp_out_norm", GW), ("xa_norm", D), ("mem_norm", D), ("xa_q_norm", XD), ("xa_k_norm", XD), ("ffn_norm", D))
BS_ROW = 16
VEC_ROWS = 24
SMALL = tuple(n for n, _ in VECS) + ("gmlp_bs", "gmlp_ws", "ffn_conv_b")


def _pack_small(raw):
    names = [n for n, _ in VECS] + ["gmlp_bs", "conv_sums"]

    def body(*refs):
        ins = dict(zip(names, refs))
        vec_ref, cb_ref = refs[len(names):]
        vec_ref[...] = jnp.zeros_like(vec_ref)
        for r, (n, w) in enumerate(VECS):
            vec_ref[r:r + 1, 0:w] = ins[n][0:1, 0:w]
        vec_ref[BS_ROW:BS_ROW + 8, 0:BLK] = ins["gmlp_bs"][...]
        for s in range(2):
            for d in range(NG):
                cb_ref[s, d] = ins["conv_sums"][s, d, 3:4, :]

    return pl.pallas_call(body, name="pack_small", out_shape=[_sds((VEC_ROWS, D)), _sds((2, NG, 1, SW))])(
        *[raw[n] for n in names])


def _adam_math(g, w, m, v):
    nm = B1 * m + (1.0 - B1) * g
    nv = B2 * v + (1.0 - B2) * (g * g)
    m_hat = nm / (1.0 - B1 ** STEP)
    v_hat = nv / (1.0 - B2 ** STEP)
    return -LR * (m_hat / (jnp.sqrt(v_hat) + AEPS) + WD * w), nm, nv


def _sum_parts(p_ref):
    g = p_ref[0].astype(F32)
    for j in range(1, NDEV):
        g = g + p_ref[j].astype(F32)
    return g


def _adam_small(parts_vec, parts_ws, parts_cb, w, m, v):
    def body(*refs):
        pv_ref, pws_ref, pcb_ref = refs[:3]
        ins = refs[3:3 + 3 * len(SMALL)]
        outs = refs[3 + 3 * len(SMALL):]
        gv = _sum_parts(pv_ref)
        for j, n in enumerate(SMALL):
            w_ref, m_ref, v_ref = ins[3 * j:3 * j + 3]
            o = outs[4 * j:4 * j + 4]
            if n == "gmlp_ws":
                g = _sum_parts(pws_ref)
            elif n == "ffn_conv_b":
                g = _sum_parts(pcb_ref)
            elif n == "gmlp_bs":
                g = gv[BS_ROW:BS_ROW + 8, 0:BLK]
            else:
                g = gv[j:j + 1, 0:VECS[j][1]]
            lead = n in ("gmlp_ws", "gmlp_bs")
            res = (g,) + _adam_math(g, w_ref[0] if lead else w_ref[...], m_ref[0] if lead else m_ref[...],
                                    v_ref[0] if lead else v_ref[...])
            for o_ref, val in zip(o, res):
                if lead:
                    o_ref[0] = val
                else:
                    o_ref[...] = val

    args = [parts_vec, parts_ws, parts_cb] + [d[n] for n in SMALL for d in (w, m, v)]
    res = pl.pallas_call(body, name="adam_small", out_shape=[_sds(w[n].shape) for n in SMALL for _ in range(4)],
                         compiler_params=pltpu.CompilerParams(vmem_limit_bytes=VMEM_LIMIT))(*args)
    return {n: tuple(res[4 * j:4 * j + 4]) for j, n in enumerate(SMALL)}


def kernel(x, mem, positions, mix_norm, w_in, q_norm, k_norm, attn_sinks, gmlp_v_norm, gmlp_ws, gmlp_bs, attn_out_norm, gmlp_out_norm, w_out, xa_norm, mem_norm, xa_wq, xa_wkv, xa_q_norm, xa_k_norm, xa_wo, ffn_norm, ffn_up, ffn_conv, ffn_conv_b, ffn_down, loss_target, m_mix_norm, m_w_in, m_q_norm, m_k_norm, m_attn_sinks, m_gmlp_v_norm, m_gmlp_ws, m_gmlp_bs, m_attn_out_norm, m_gmlp_out_norm, m_w_out, m_xa_norm, m_mem_norm, m_xa_wq, m_xa_wkv, m_xa_q_norm, m_xa_k_norm, m_xa_wo, m_ffn_norm, m_ffn_up, m_ffn_conv, m_ffn_conv_b, m_ffn_down, v_mix_norm, v_w_in, v_q_norm, v_k_norm, v_attn_sinks, v_gmlp_v_norm, v_gmlp_ws, v_gmlp_bs, v_attn_out_norm, v_gmlp_out_norm, v_w_out, v_xa_norm, v_mem_norm, v_xa_wq, v_xa_wkv, v_xa_q_norm, v_xa_k_norm, v_xa_wo, v_ffn_norm, v_ffn_up, v_ffn_conv, v_ffn_conv_b, v_ffn_down):
    names = ("mix_norm", "w_in", "q_norm", "k_norm", "attn_sinks", "gmlp_v_norm", "gmlp_ws", "gmlp_bs", "attn_out_norm",
             "gmlp_out_norm", "w_out", "xa_norm", "mem_norm", "xa_wq", "xa_wkv", "xa_q_norm", "xa_k_norm", "xa_wo",
             "ffn_norm", "ffn_up", "ffn_conv", "ffn_conv_b", "ffn_down")
    w = dict(zip(names, (mix_norm, w_in, q_norm, k_norm, attn_sinks, gmlp_v_norm, gmlp_ws, gmlp_bs, attn_out_norm,
                         gmlp_out_norm, w_out, xa_norm, mem_norm, xa_wq, xa_wkv, xa_q_norm, xa_k_norm, xa_wo, ffn_norm,
                         ffn_up, ffn_conv, ffn_conv_b, ffn_down)))
    m = dict(zip(names, (m_mix_norm, m_w_in, m_q_norm, m_k_norm, m_attn_sinks, m_gmlp_v_norm, m_gmlp_ws, m_gmlp_bs,
                         m_attn_out_norm, m_gmlp_out_norm, m_w_out, m_xa_norm, m_mem_norm, m_xa_wq, m_xa_wkv,
                         m_xa_q_norm, m_xa_k_norm, m_xa_wo, m_ffn_norm, m_ffn_up, m_ffn_conv, m_ffn_conv_b, m_ffn_down)))
    v = dict(zip(names, (v_mix_norm, v_w_in, v_q_norm, v_k_norm, v_attn_sinks, v_gmlp_v_norm, v_gmlp_ws, v_gmlp_bs,
                         v_attn_out_norm, v_gmlp_out_norm, v_w_out, v_xa_norm, v_mem_norm, v_xa_wq, v_xa_wkv,
                         v_xa_q_norm, v_xa_k_norm, v_xa_wo, v_ffn_norm, v_ffn_up, v_ffn_conv, v_ffn_conv_b, v_ffn_down)))
    t = x.shape[1]

    me = 4 * lax.axis_index("x") + 2 * lax.axis_index("y") + lax.axis_index("c")

    def rows(a, n):
        return jnp.swapaxes(a[0], 0, 1) if n in BY_COLUMN else a[0]

    mats = [n for n in BIG if n != "ffn_conv"]
    shard = dict(zip(mats, _cast_shards([rows(w[n], n) for n in mats])), ffn_conv=w["ffn_conv"][0])
    gathers, tokens = zip(*[_exchange_start([], [shard[n] for n in grp], me, "gather_start_%d" % i)
                            for i, grp in enumerate(GATHER_GROUPS)])

    def fetch(i, after):
        after = tokens[0] + tokens[1] + tokens[2] if after is None else after
        got = dict(zip(GATHER_GROUPS[i], _exchange_wait(gathers[i], after, "gather_wait_%d" % i)))
        if "w_in" in got:
            got["w_in"] = got["w_in"].reshape(IN, D)
        for n in ("w_out", "xa_wq", "xa_wo"):
            if n in got:
                got[n] = got[n].reshape(D, D)
        if "ffn_down" in got:
            got["ffn_down"] = got["ffn_down"].reshape(NG, SW, D)
            got["ffn_conv"] = got["ffn_conv"].reshape(2, NG, 3, SW)
        return got

    scatters = []

    def ship(i, grads, for_all=()):
        by_dest = [grads[n].reshape((NDEV,) + shard[n].shape) for n in SCATTER_GROUPS[i]]
        state, token = _exchange_start(by_dest, for_all, me, "scatter_start_%d" % i)
        scatters.append(state)
        return token

    conv_b = {k: d["ffn_conv_b"].reshape(NDEV, 1, SW) for k, d in (("w", w), ("m", m), ("v", v))}
    p = {n: w[n] for n in SMALL[:-1]}
    p["gmlp_ws"], p["gmlp_bs"] = w["gmlp_ws"][0], w["gmlp_bs"][0]
    p["ffn_conv_b"] = conv_b["w"].reshape(2, NG, 1, SW)
    loss, grad_x, g, raw = _local_step(x[0], mem[0], positions.reshape(t, 1), loss_target[0], p, fetch, ship)
    loss = lax.psum(loss, AXES)

    vec, cb = _pack_small(raw)
    after = ship(3, g, [vec, cb.reshape(NDEV, 1, SW)])
    res, rest = {}, []
    for i, grp in enumerate(SCATTER_GROUPS):
        got = _exchange_wait(scatters[i], after, "scatter_wait_%d" % i)
        rest += got[len(grp):]
        for n, parts in zip(grp, got):
            out = _adam(parts, rows(w[n], n), rows(m[n], n), rows(v[n], n), "adam_" + n)
            res[n] = [jnp.swapaxes(o, 0, 1) if n in BY_COLUMN else o for o in out]
            after = out[0]
    ws_parts, vec_parts, cb_parts = rest
    small = lambda d, k: {**{n: d[n] for n in SMALL[:-1]}, "ffn_conv_b": conv_b[k]}
    res.update(_adam_small(vec_parts, ws_parts, cb_parts, small(w, "w"), small(m, "m"), small(v, "v")))

    outs = [loss, grad_x[None]]
    for j in range(4):
        outs += [res[n][j].reshape(w[n].shape) for n in names]
    return tuple(outs)
```

```python
import functools
import math

import jax
import jax.numpy as jnp
from jax import lax
from jax.experimental import pallas as pl
from jax.experimental.pallas import tpu as pltpu

F32 = jnp.float32
BF16 = jnp.bfloat16

D = 1024
HD = 64
AW = 512
KW = 128
GW = 512
IN = AW + 2 * KW + 2 * GW
BLK = 128
MEM = 256
XH = 4
XD = 256
FF = 2816
EPS = 1e-6
ROPE_THETA = 10000.0
NDEV = 8
LR, B1, B2, AEPS, WD, STEP = 0.001, 0.9, 0.999, 1e-08, 0.01, 10

TM = 512
WK = 2048
VMEM_LIMIT = 56 * 1024 * 1024
NEG = float(jnp.finfo(jnp.float32).min)
GELU_C0 = math.sqrt(2.0 / math.pi)
GELU_C1 = 0.044715
AXES = ("x", "y", "c")


def _dot(a, b):
    return jnp.dot(a, b, preferred_element_type=F32)


def _dot_nt(a, b):
    return lax.dot_general(a, b, (((1,), (1,)), ((), ())), preferred_element_type=F32)


def _dot_tn(a, b):
    return lax.dot_general(a, b, (((0,), (0,)), ((), ())), preferred_element_type=F32)


def _rs(x):
    return lax.rsqrt(jnp.mean(x * x, axis=-1, keepdims=True) + EPS)


def _rms_bwd(dy, x, r, g):
    xh = x * r
    dxh = dy * g
    dx = r * (dxh - xh * jnp.mean(dxh * xh, axis=-1, keepdims=True))
    return dx, dy * xh


def _lane(shape):
    return lax.broadcasted_iota(jnp.int32, shape, len(shape) - 1)


def _gsum64(v, ones_ref):
    w = v.shape[-1]
    ones = ones_ref[0:w, 0:w]
    hi = v.astype(BF16)
    lo = (v - hi.astype(F32)).astype(BF16)
    return _dot(hi, ones) + _dot(lo, ones)


def _head_ones():
    i = jnp.arange(AW) // HD
    return (i[:, None] == i[None, :]).astype(BF16)


def _rs64(x, ones_ref):
    return lax.rsqrt(_gsum64(x * x, ones_ref) * (1.0 / HD) + EPS)


def _rms64_bwd(dy, x, r, g, ones_ref):
    xh = x * r
    dxh = dy * g
    dx = r * (dxh - xh * (_gsum64(dxh * xh, ones_ref) * (1.0 / HD)))
    return dx, dy * xh


def _rot_half(v):
    w = v.shape[-1]
    return jnp.where((_lane(v.shape) & 32) == 0, pltpu.roll(v, w - 32, 1), pltpu.roll(v, 32, 1))


def _rope(v, cos, sin_signed):
    return v * cos + _rot_half(v) * sin_signed


def _rope_bwd(dv, cos, sin_signed):
    return dv * cos + _rot_half(dv * sin_signed)


def _gelu(z):
    return 0.5 * z * (1.0 + jnp.tanh(GELU_C0 * (z + GELU_C1 * z * z * z)))


def _gelu_grad(z):
    t = jnp.tanh(GELU_C0 * (z + GELU_C1 * z * z * z))
    return 0.5 * (1.0 + t) + 0.5 * z * (1.0 - t * t) * (GELU_C0 * (1.0 + 3.0 * GELU_C1 * z * z))


def _colsum8(v):
    s = jnp.sum(v, axis=0, keepdims=True)
    row = lax.broadcasted_iota(jnp.int32, (8, v.shape[1]), 0)
    return jnp.where(row == 0, jnp.broadcast_to(s, (8, v.shape[1])), 0.0)


def _params(n_axes=1):
    return pltpu.CompilerParams(dimension_semantics=("arbitrary",) * n_axes, vmem_limit_bytes=VMEM_LIMIT)


def _rows(tm, w):
    return pl.BlockSpec((tm, w), lambda i: (i, 0))


def _const(shape):
    nd = len(shape)
    return pl.BlockSpec(shape, lambda *_: (0,) * nd)


def _sds(shape, dtype=F32):
    return jax.ShapeDtypeStruct(shape, dtype)


def _mm_tn(a, b, name):
    g = max(a.shape[0] if a.ndim == 3 else 1, b.shape[0] if b.ndim == 3 else 1)
    t, m = a.shape[-2:]
    n = b.shape[-1]

    def body(a_ref, b_ref, o_ref, acc_ref):
        i = pl.program_id(1)

        @pl.when(i == 0)
        def _():
            acc_ref[...] = jnp.zeros_like(acc_ref)

        acc_ref[...] += _dot_tn(a_ref[...].astype(BF16), b_ref[...].astype(BF16))

        @pl.when(i == pl.num_programs(1) - 1)
        def _():
            o_ref[...] = acc_ref[...].astype(BF16)

    tk = min(t, WK)

    def spec(v):
        w = v.shape[-1]
        if v.ndim == 3:
            return pl.BlockSpec((None, tk, w), lambda j, i: (j, i, 0))
        return pl.BlockSpec((tk, w), lambda j, i: (i, 0))

    return pl.pallas_call(
        body, name=name, grid=(g, t // tk), in_specs=[spec(a), spec(b)],
        out_specs=pl.BlockSpec((None, m, n), lambda j, i: (j, 0, 0)), out_shape=_sds((g, m, n), BF16),
        scratch_shapes=[pltpu.VMEM((m, n), F32)], compiler_params=_params(2))(a, b)


def _rope_tables(pos, inv_freq):
    t = pos.shape[0]

    def body(pos_ref, f_ref, cos_ref, sin_ref):
        ang = pos_ref[...].astype(F32) * f_ref[...]
        sign = jnp.where((_lane(ang.shape) & 32) == 0, -1.0, 1.0)
        cos_ref[...] = jnp.cos(ang)
        sin_ref[...] = jnp.sin(ang) * sign

    return pl.pallas_call(
        body, name="rope_tables", grid=(t // TM,),
        in_specs=[_rows(TM, 1), _const((1, 128))], out_specs=[_rows(TM, 128), _rows(TM, 128)],
        out_shape=[_sds((t, 128)), _sds((t, 128))], compiler_params=_params())(pos, inv_freq)


def _mixer_in_fwd(x, mix_norm, w_in, qn, kn, gvw, cos, sin):
    t = x.shape[0]

    def body(x_ref, g_ref, w_ref, qn_ref, kn_ref, gvw_ref, cos_ref, sin_ref, ones_ref,
             h_ref, qk_ref, gz_ref, q_ref, k_ref, v_ref, gu_ref, gvn_ref):
        x = x_ref[...]
        h = (x * _rs(x) * g_ref[...]).astype(BF16)
        h_ref[...] = h
        proj = _dot_nt(h, w_ref[...])
        qk = proj[:, :AW + KW]
        qk_ref[...] = qk
        gz = proj[:, AW + 2 * KW:]
        gz_ref[...] = gz
        cos2, sin2 = cos_ref[...], sin_ref[...]
        q = qk[:, :AW]
        q = q * _rs64(q, ones_ref) * qn_ref[...]
        q_ref[...] = _rope(q, jnp.tile(cos2, (1, 4)), jnp.tile(sin2, (1, 4))).astype(BF16)
        k = qk[:, AW:]
        k = k * _rs64(k, ones_ref) * kn_ref[...]
        k_ref[...] = _rope(k, cos2, sin2).astype(BF16)
        v_ref[...] = proj[:, AW + KW:AW + 2 * KW].astype(BF16)
        gu_ref[...] = _gelu(gz[:, :GW])
        gv = _gelu(gz[:, GW:])
        gvn_ref[...] = (gv * _rs(gv) * gvw_ref[...]).astype(BF16)

    return pl.pallas_call(
        body, name="mixer_in_fwd", grid=(t // TM,),
        in_specs=[_rows(TM, D), _const((1, D)), _const((IN, D)), _const((1, AW)), _const((1, KW)),
                  _const((1, GW)), _rows(TM, 128), _rows(TM, 128), _const((AW, AW))],
        out_specs=[_rows(TM, D), _rows(TM, AW + KW), _rows(TM, 2 * GW), _rows(TM, AW), _rows(TM, KW),
                   _rows(TM, KW), _rows(TM, GW), _rows(TM, GW)],
        out_shape=[_sds((t, D), BF16), _sds((t, AW + KW)), _sds((t, 2 * GW)), _sds((t, AW), BF16),
                   _sds((t, KW), BF16), _sds((t, KW), BF16), _sds((t, GW)), _sds((t, GW), BF16)],
        compiler_params=_params())(x, mix_norm, w_in, qn, kn, gvw, cos, sin, _head_ones())


def _dup_half(kk, g):
    lane = _lane(kk.shape)
    other = pltpu.roll(kk, 64, 1)
    keep = (lane < 64) if g == 0 else (lane >= 64)
    return jnp.where(keep, kk, other).astype(BF16)


def _swa_mask(first_block):
    qi = lax.broadcasted_iota(jnp.int32, (4 * BLK, 2 * BLK), 0) & (BLK - 1)
    kj = lax.broadcasted_iota(jnp.int32, (4 * BLK, 2 * BLK), 1)
    diff = qi + BLK - kj
    band = (diff >= 0) & (diff < BLK)
    return band & (jnp.logical_not(first_block) | (kj >= BLK))


def _stack_heads(a2, b2):
    lo = _lane(a2.shape) < 64
    z = jnp.zeros_like(a2)
    return jnp.concatenate([jnp.where(lo, a2, z), jnp.where(lo, z, a2), jnp.where(lo, b2, z), jnp.where(lo, z, b2)], axis=0)


def _unstack_heads(o):
    lo = _lane((BLK, 128)) < 64
    return jnp.where(lo, o[0:BLK], o[BLK:2 * BLK]), jnp.where(lo, o[2 * BLK:3 * BLK], o[3 * BLK:4 * BLK])


def _sink_col(sink_ref, g):
    row = lax.broadcasted_iota(jnp.int32, (4 * BLK, 1), 0)
    s = [sink_ref[0, 4 * g + j] for j in range(4)]
    return jnp.where(row < BLK, s[0], jnp.where(row < 2 * BLK, s[1], jnp.where(row < 3 * BLK, s[2], s[3])))


def _swa_probs(qs, kd, mask, sink):
    s = _dot_nt(qs, kd) * (1.0 / math.sqrt(HD))
    s = jnp.where(mask, s, NEG)
    m = jnp.maximum(jnp.max(s, axis=-1, keepdims=True), sink)
    p = jnp.exp(s - m)
    ps = jnp.exp(sink - m)
    inv = 1.0 / (jnp.sum(p, axis=-1, keepdims=True) + ps)
    return p * inv, ps * inv


SB = 4
SB_BWD = 2


def _swa_fwd(q, k, v, sinks):
    t = q.shape[0]
    ts = min(t, SB * BLK)

    def body(sink_ref, q_ref, kc_ref, kp_ref, vc_ref, vp_ref, o_ref):
        i = pl.program_id(0)
        kk = jnp.concatenate([kp_ref[...], kc_ref[...]], axis=0).astype(F32)
        vv = jnp.concatenate([vp_ref[...], vc_ref[...]], axis=0).astype(F32)
        for b in range(ts // BLK):
            r = slice(b * BLK, (b + 1) * BLK)
            kb, vb = kk[b * BLK:(b + 2) * BLK], vv[b * BLK:(b + 2) * BLK]
            mask = _swa_mask(i == 0) if b == 0 else _swa_mask(False)
            for g in range(2):
                qs = _stack_heads(q_ref[r, 256 * g:256 * g + 128], q_ref[r, 256 * g + 128:256 * g + 256])
                pn, _ = _swa_probs(qs, _dup_half(kb, g), mask, _sink_col(sink_ref, g))
                oa, ob = _unstack_heads(_dot(pn.astype(BF16), _dup_half(vb, g)))
                o_ref[r, 256 * g:256 * g + 128] = oa
                o_ref[r, 256 * g + 128:256 * g + 256] = ob

    cur = lambda i: (i, 0)
    prev = lambda i: (jnp.maximum(i * (ts // BLK) - 1, 0), 0)
    return pl.pallas_call(
        body, name="swa_fwd", grid=(t // ts,),
        in_specs=[pl.BlockSpec(memory_space=pltpu.SMEM), pl.BlockSpec((ts, AW), cur),
                  pl.BlockSpec((ts, KW), cur), pl.BlockSpec((BLK, KW), prev),
                  pl.BlockSpec((ts, KW), cur), pl.BlockSpec((BLK, KW), prev)],
        out_specs=pl.BlockSpec((ts, AW), cur), out_shape=_sds((t, AW)),
        compiler_params=_params())(sinks, q, k, k, v, v)


def _causal_bf16(w_ref, h, transposed):
    r = lax.broadcasted_iota(jnp.int32, (BLK, BLK), 0)
    c = lax.broadcasted_iota(jnp.int32, (BLK, BLK), 1)
    keep = (r <= c) if transposed else (c <= r)
    return jnp.where(keep, w_ref[h], 0.0).astype(BF16)


def _gmlp_mix(w_ref, xin, transposed):
    lo = _lane((BLK, 128)) < 64
    wm = [_causal_bf16(w_ref, h, transposed) for h in range(8)]
    rows = []
    for c in range(xin.shape[0] // BLK):
        cols = []
        for j in range(4):
            xs = xin[c * BLK:(c + 1) * BLK, 128 * j:128 * (j + 1)]
            cols.append(jnp.where(lo, _dot(wm[2 * j], xs), _dot(wm[2 * j + 1], xs)))
        rows.append(jnp.concatenate(cols, axis=1))
    return jnp.concatenate(rows, axis=0)


def _mixer_out_fwd(attn, gvn, gu, ws, bfull, x, w_out, aon, gon, xan):
    t = x.shape[0]

    def body(a_ref, v_ref, gu_ref, ws_ref, b_ref, x_ref, w_ref, aon_ref, gon_ref, xan_ref, gm_ref, y_ref, x1_ref, h2_ref):
        a = a_ref[...]
        g = gu_ref[...] * (_gmlp_mix(ws_ref, v_ref[...], False) + jnp.tile(b_ref[...], (TM // BLK, 1)))
        gm_ref[...] = g
        y = jnp.concatenate([a * _rs(a) * aon_ref[...], g * _rs(g) * gon_ref[...]], axis=1).astype(BF16)
        y_ref[...] = y
        x1 = x_ref[...] + _dot(y, w_ref[...])
        x1_ref[...] = x1
        h2_ref[...] = (x1 * _rs(x1) * xan_ref[...]).astype(BF16)

    return pl.pallas_call(
        body, name="mixer_out_fwd", grid=(t // TM,),
        in_specs=[_rows(TM, AW), _rows(TM, GW), _rows(TM, GW), _const((8, BLK, BLK)), _const((BLK, GW)), _rows(TM, D),
                  _const((D, D)), _const((1, AW)), _const((1, GW)), _const((1, D))],
        out_specs=[_rows(TM, GW), _rows(TM, D), _rows(TM, D), _rows(TM, D)],
        out_shape=[_sds((t, GW)), _sds((t, D), BF16), _sds((t, D)), _sds((t, D), BF16)],
        compiler_params=_params())(attn, gvn, gu, ws, bfull, x, w_out, aon, gon, xan)


def _mem_kv_fwd(mem, mem_norm, wkv, kn4):
    def body(m_ref, g_ref, w_ref, kn_ref, mh_ref, kpre_ref, k_ref, v_ref):
        m = m_ref[...]
        mh = (m * _rs(m) * g_ref[...]).astype(BF16)
        mh_ref[...] = mh
        for h in range(XH):
            sl = slice(XD * h, XD * (h + 1))
            kh = _dot(mh, w_ref[h])
            kpre_ref[:, sl] = kh
            k_ref[:, sl] = (kh * _rs(kh) * kn_ref[:, sl]).astype(BF16)
            v_ref[:, sl] = _dot(mh, w_ref[XH + h]).astype(BF16)

    return pl.pallas_call(
        body, name="mem_kv_fwd",
        out_shape=[_sds((MEM, D), BF16), _sds((MEM, D)), _sds((MEM, D), BF16), _sds((MEM, D), BF16)],
        compiler_params=pltpu.CompilerParams(vmem_limit_bytes=VMEM_LIMIT))(mem, mem_norm, wkv, kn4)


def _xattn_probs(qpre_h, qn_h, k_h):
    rq = _rs(qpre_h)
    q2 = (qpre_h * rq * qn_h).astype(BF16)
    s = _dot_nt(q2, k_h) * (1.0 / math.sqrt(XD))
    p = jnp.exp(s - jnp.max(s, axis=-1, keepdims=True))
    return p * (1.0 / jnp.sum(p, axis=-1, keepdims=True)), q2, rq


def _xattn_fwd(h2, x1, wq, qn4, k2, v2, wo, ffn_norm):
    t = x1.shape[0]

    def body(h_ref, x_ref, wq_ref, qn_ref, k_ref, v_ref, wo_ref, fn_ref, qpre_ref, o_ref, x2_ref, h3_ref):
        qpre = _dot(h_ref[...], wq_ref[...])
        qpre_ref[...] = qpre
        outs = []
        for h in range(XH):
            sl = slice(XD * h, XD * (h + 1))
            pn, _, _ = _xattn_probs(qpre[:, sl], qn_ref[:, sl], k_ref[:, sl])
            outs.append(_dot(pn.astype(BF16), v_ref[:, sl]))
        o = jnp.concatenate(outs, axis=1).astype(BF16)
        o_ref[...] = o
        x2 = x_ref[...] + _dot(o, wo_ref[...])
        x2_ref[...] = x2
        h3_ref[...] = (x2 * _rs(x2) * fn_ref[...]).astype(BF16)

    return pl.pallas_call(
        body, name="xattn_fwd", grid=(t // TM,),
        in_specs=[_rows(TM, D), _rows(TM, D), _const((D, D)), _const((1, D)), _const((MEM, D)), _const((MEM, D)),
                  _const((D, D)), _const((1, D))],
        out_specs=[_rows(TM, D)] * 4,
        out_shape=[_sds((t, D)), _sds((t, D), BF16), _sds((t, D)), _sds((t, D), BF16)],
        compiler_params=_params())(h2, x1, wq, qn4, k2, v2, wo, ffn_norm)


SW = 704
NG = FF // SW
FM = 256
HALO = 16


def _resident(shape):
    nd = len(shape)
    return pl.BlockSpec(shape, lambda *_: (0,) * nd, pipeline_mode=pl.Buffered(1))


def _halo_before(i):
    return jnp.maximum(i * (FM // HALO) - 1, 0)


def _conv(e, w):
    return w[2:3, :] * e + pltpu.roll(w[1:2, :] * e + pltpu.roll(w[0:1, :] * e, 1, 0), 1, 0)


def _conv_t(dc, w):
    n = dc.shape[0]
    return w[2:3, :] * dc + pltpu.roll(w[1:2, :] * dc + pltpu.roll(w[0:1, :] * dc, n - 1, 0), n - 1, 0)


def _ffn_fwd(h3, x2, target, up, conv, conv_b, down):
    t = x2.shape[0]

    def body(h_ref, hp_ref, x_ref, t_ref, up_ref, w_ref, b_ref, dn_ref, a_ref, u_ref, gs_ref, dy_ref, loss_ref, acc_ref):
        i = pl.program_id(0)

        @pl.when(i == 0)
        def _():
            acc_ref[...] = jnp.zeros_like(acc_ref)

        before = jnp.where(i > 0, hp_ref[...], jnp.zeros_like(hp_ref))
        he = jnp.concatenate([before, h_ref[...]], axis=0)
        err = x_ref[...] - t_ref[...]
        for d in range(NG):
            c = []
            for s in range(2):
                a = _dot_nt(he, up_ref[s * NG + d])
                a_ref[s * NG + d] = a[HALO:].astype(BF16)
                c.append(_conv(a, w_ref[s, d])[HALO:] + b_ref[s, d])
            gl, gg = _gelu_and_grad(c[0])
            gs_ref[d] = gl.astype(BF16)
            gs_ref[NG + d] = (gg * c[1]).astype(BF16)
            u = (gl * c[1]).astype(BF16)
            u_ref[d] = u
            err = err + _dot(u, dn_ref[d])
        dy_ref[...] = err * (1.0 / D)
        acc_ref[...] += jnp.sum(err * err, axis=0, keepdims=True)

        @pl.when(i == pl.num_programs(0) - 1)
        def _():
            loss_ref[...] = jnp.full((8, 128), 0.5 / D, F32) * jnp.sum(acc_ref[...])

    return pl.pallas_call(
        body, name="ffn_fwd", grid=(t // FM,),
        in_specs=[_rows(FM, D), pl.BlockSpec((HALO, D), lambda i: (_halo_before(i), 0)), _rows(FM, D), _rows(FM, D),
                  _resident((NDEV, SW, D)), _resident((2, NG, 3, SW)), _resident((2, NG, 1, SW)), _resident((NG, SW, D))],
        out_specs=[pl.BlockSpec((NDEV, FM, SW), lambda i: (0, i, 0)), pl.BlockSpec((NG, FM, SW), lambda i: (0, i, 0)),
                   pl.BlockSpec((NDEV, FM, SW), lambda i: (0, i, 0)), _rows(FM, D), _const((8, 128))],
        out_shape=[_sds((NDEV, t, SW), BF16), _sds((NG, t, SW), BF16), _sds((NDEV, t, SW), BF16), _sds((t, D)),
                   _sds((8, 128))],
        scratch_shapes=[pltpu.VMEM((1, D), F32)], compiler_params=_params())(h3, h3, x2, target, up, conv, conv_b, down)


def _gelu_and_grad(z):
    z2 = z * z
    t = jnp.tanh(GELU_C0 * (z + GELU_C1 * z * z2))
    phi = 0.5 * (1.0 + t)
    return z * phi, phi + z * (1.0 - t * t) * (0.5 * GELU_C0 + (1.5 * GELU_C0 * GELU_C1) * z2)


def _ffn_bwd(dy, a, gs, x2, up, conv, down, ffn_norm):
    t = x2.shape[0]
    nt = t // FM
    n = FM + HALO

    def body(dy_ref, dyn_ref, a_ref, gs_ref, gsn_ref, x_ref, up_ref, w_ref, dn_ref, g_ref,
             dx_ref, da_ref, s_ref, dfn_ref):
        i = pl.program_id(0)

        @pl.when(i == 0)
        def _():
            s_ref[...] = jnp.zeros_like(s_ref)
            dfn_ref[...] = jnp.zeros_like(dfn_ref)

        last = i == nt - 1
        dy = dy_ref[...]
        dye = jnp.concatenate([dy, jnp.where(last, 0.0, dyn_ref[...])], axis=0).astype(BF16)
        dh = jnp.zeros((FM, D), F32)
        row = lax.broadcasted_iota(jnp.int32, (8, SW), 0)
        for d in range(NG):
            du = _dot_nt(dye, dn_ref[d])
            for s in range(2):
                j = s * NG + d
                k = NG + d if s == 0 else d
                dc = du * jnp.concatenate([gs_ref[k], gsn_ref[k]], axis=0).astype(F32)
                w = w_ref[s, d]
                tile = a_ref[j].astype(F32)
                d1 = pltpu.roll(dc, n - 1, 0)
                d2 = pltpu.roll(d1, n - 1, 0)
                da = (w[2:3, :] * dc + w[1:2, :] * d1 + w[0:1, :] * d2)[0:FM].astype(BF16)
                da_ref[j] = da
                dh = dh + _dot(da, up_ref[j])
                sums = [jnp.sum(v[0:FM] * tile, axis=0, keepdims=True) for v in (d2, d1, dc)]
                sums.append(jnp.sum(dc[0:FM], axis=0, keepdims=True))
                upd = jnp.zeros((8, SW), F32)
                for r, v in enumerate(sums):
                    upd = jnp.where(row == r, jnp.broadcast_to(v, (8, SW)), upd)
                s_ref[s, d] += upd
        x = x_ref[...]
        dx, dg = _rms_bwd(dh, x, _rs(x), g_ref[...])
        dx_ref[...] = dy + dx
        dfn_ref[...] += _colsum8(dg)

    last_halo = t // HALO - 1
    after = lambda i: jnp.minimum((i + 1) * (FM // HALO), last_halo)
    return pl.pallas_call(
        body, name="ffn_bwd", grid=(nt,),
        in_specs=[_rows(FM, D), pl.BlockSpec((HALO, D), lambda i: (after(i), 0)),
                  pl.BlockSpec((NDEV, FM, SW), lambda i: (0, i, 0)),
                  pl.BlockSpec((NDEV, FM, SW), lambda i: (0, i, 0)),
                  pl.BlockSpec((NDEV, HALO, SW), lambda i: (0, after(i), 0)),
                  _rows(FM, D), _resident((NDEV, SW, D)), _resident((2, NG, 3, SW)), _resident((NG, SW, D)), _const((1, D))],
        out_specs=[_rows(FM, D), pl.BlockSpec((NDEV, FM, SW), lambda i: (0, i, 0)), _const((2, NG, 8, SW)), _const((8, D))],
        out_shape=[_sds((t, D)), _sds((NDEV, t, SW), BF16), _sds((2, NG, 8, SW)), _sds((8, D))],
        compiler_params=_params())(dy, dy, a, gs, gs, x2, up, conv, down, ffn_norm)


BT = 512


def _xattn_bwd(dx2, x1, qpre, k2, v2, wq, wo, qn4, xan):
    t = x1.shape[0]

    def body(dx2_ref, x1_ref, qpre_ref, k_ref, v_ref, wq_ref, wo_ref, qn_ref, xan_ref,
             dx1_ref, dqpre_ref, dk_ref, dv_ref, dqn_ref, dxan_ref):
        @pl.when(pl.program_id(0) == 0)
        def _():
            for r in (dk_ref, dv_ref, dqn_ref, dxan_ref):
                r[...] = jnp.zeros_like(r)

        dx2 = dx2_ref[...]
        do = _dot_nt(dx2.astype(BF16), wo_ref[...])
        dqs = []
        for h in range(XH):
            sl = slice(XD * h, XD * (h + 1))
            qpre_h = qpre_ref[:, sl]
            pn, q2, rq = _xattn_probs(qpre_h, qn_ref[:, sl], k_ref[:, sl])
            do_h = do[:, sl].astype(BF16)
            dp = _dot_nt(do_h, v_ref[:, sl])
            ds = (pn * (dp - jnp.sum(pn * dp, axis=-1, keepdims=True)) * (1.0 / math.sqrt(XD))).astype(BF16)
            dq2 = _dot(ds, k_ref[:, sl])
            dk_ref[:, sl] += _dot_tn(ds, q2)
            dv_ref[:, sl] += _dot_tn(pn.astype(BF16), do_h)
            dqh, dg = _rms_bwd(dq2, qpre_h, rq, qn_ref[:, sl])
            dqn_ref[...] += _colsum8(dg)
            dqs.append(dqh)
        dqpre = jnp.concatenate(dqs, axis=1).astype(BF16)
        dqpre_ref[...] = dqpre
        dh2 = _dot_nt(dqpre, wq_ref[...])
        x1 = x1_ref[...]
        dx, dg = _rms_bwd(dh2, x1, _rs(x1), xan_ref[...])
        dx1_ref[...] = dx2 + dx
        dxan_ref[...] += _colsum8(dg)

    return pl.pallas_call(
        body, name="xattn_bwd", grid=(t // BT,),
        in_specs=[_rows(BT, D), _rows(BT, D), _rows(BT, D), _const((MEM, D)), _const((MEM, D)), _const((D, D)),
                  _const((D, D)), _const((1, D)), _const((1, D))],
        out_specs=[_rows(BT, D), _rows(BT, D), _const((MEM, D)), _const((MEM, D)), _const((8, XD)), _const((8, D))],
        out_shape=[_sds((t, D)), _sds((t, D), BF16), _sds((MEM, D)), _sds((MEM, D)), _sds((8, XD)), _sds((8, D))],
        compiler_params=_params())(dx2, x1, qpre, k2, v2, wq, wo, qn4, xan)


def _mem_kv_bwd(mem, mh, kpre, dk2, dv2, wkv, kn4, mem_norm):
    def body(m_ref, mh_ref, kpre_ref, dk_ref, dv_ref, w_ref, kn_ref, g_ref, dw_ref, dkn_ref, dmn_ref):
        dkn = jnp.zeros((8, XD), F32)
        dm = jnp.zeros((MEM, D), F32)
        mh = mh_ref[...]
        for h in range(XH):
            sl = slice(XD * h, XD * (h + 1))
            kh = kpre_ref[:, sl]
            dkh, dg = _rms_bwd(dk_ref[:, sl], kh, _rs(kh), kn_ref[:, sl])
            dkn = dkn + _colsum8(dg)
            dkh = dkh.astype(BF16)
            dvh = dv_ref[:, sl].astype(BF16)
            dw_ref[h] = _dot_tn(mh, dkh).astype(BF16)
            dw_ref[XH + h] = _dot_tn(mh, dvh).astype(BF16)
            dm = dm + _dot_nt(dkh, w_ref[h]) + _dot_nt(dvh, w_ref[XH + h])
        dkn_ref[...] = dkn
        m = m_ref[...]
        _, dg = _rms_bwd(dm, m, _rs(m), g_ref[...])
        dmn_ref[...] = _colsum8(dg)

    return pl.pallas_call(
        body, name="mem_kv_bwd", out_shape=[_sds((2 * XH, D, XD), BF16), _sds((8, XD)), _sds((8, D))],
        compiler_params=pltpu.CompilerParams(vmem_limit_bytes=VMEM_LIMIT))(mem, mh, kpre, dk2, dv2, wkv, kn4, mem_norm)


def _mixer_out_bwd(dx1, attn, gm, w_out, aon, gon):
    t = dx1.shape[0]

    def body(dx_ref, a_ref, g_ref, w_ref, aon_ref, gon_ref, da_ref, dg_ref, dan_ref, dgn_ref):
        @pl.when(pl.program_id(0) == 0)
        def _():
            dan_ref[...] = jnp.zeros_like(dan_ref)
            dgn_ref[...] = jnp.zeros_like(dgn_ref)

        dy = _dot_nt(dx_ref[...].astype(BF16), w_ref[...])
        a, g = a_ref[...], g_ref[...]
        da, dna = _rms_bwd(dy[:, :AW], a, _rs(a), aon_ref[...])
        dg, dng = _rms_bwd(dy[:, AW:], g, _rs(g), gon_ref[...])
        da_ref[...] = da
        dg_ref[...] = dg
        dan_ref[...] += _colsum8(dna)
        dgn_ref[...] += _colsum8(dng)

    return pl.pallas_call(
        body, name="mixer_out_bwd", grid=(t // TM,),
        in_specs=[_rows(TM, D), _rows(TM, AW), _rows(TM, GW), _const((D, D)), _const((1, AW)), _const((1, GW))],
        out_specs=[_rows(TM, AW), _rows(TM, GW), _const((8, AW)), _const((8, GW))],
        out_shape=[_sds((t, AW)), _sds((t, GW)), _sds((8, AW)), _sds((8, GW))],
        compiler_params=_params())(dx1, attn, gm, w_out, aon, gon)


def _gmlp_bwd(dgm, gu, gvn, gz, ws, wst, bfull, gvw):
    t = dgm.shape[0]
    nc = TM // BLK

    def body(dgm_ref, gu_ref, x_ref, gz_ref, w_ref, wt_ref, b_ref, gvw_ref, dgz_ref, dw_ref, db_ref, dgvw_ref,
             dbacc_ref):
        @pl.when(pl.program_id(0) == 0)
        def _():
            for r in (dw_ref, dbacc_ref, dgvw_ref):
                r[...] = jnp.zeros_like(r)

        xin = x_ref[...]
        dgm = dgm_ref[...]
        mixed = _gmlp_mix(w_ref, xin, False) + jnp.tile(b_ref[...], (nc, 1))
        dgu = dgm * mixed
        dmixed = dgm * gu_ref[...]
        lo = _lane((BLK, 128)) < 64
        dbias = jnp.zeros((BLK, GW), F32)
        for c in range(nc):
            dmc = dmixed[c * BLK:(c + 1) * BLK]
            dbias = dbias + dmc
            for j in range(4):
                dm2 = dmc[:, 128 * j:128 * (j + 1)]
                xs = xin[c * BLK:(c + 1) * BLK, 128 * j:128 * (j + 1)]
                z = jnp.zeros_like(dm2)
                dw_ref[2 * j] += _dot_nt(jnp.where(lo, dm2, z).astype(BF16), xs)
                dw_ref[2 * j + 1] += _dot_nt(jnp.where(lo, z, dm2).astype(BF16), xs)
        dbacc_ref[...] += dbias
        dgvn = _gmlp_mix(wt_ref, dmixed.astype(BF16), True)
        gz_u, gz_v = gz_ref[:, :GW], gz_ref[:, GW:]
        gv = _gelu(gz_v)
        dgv, dg = _rms_bwd(dgvn, gv, _rs(gv), gvw_ref[...])
        dgvw_ref[...] += _colsum8(dg)
        dgz_ref[:, :GW] = (dgu * _gelu_grad(gz_u)).astype(BF16)
        dgz_ref[:, GW:] = (dgv * _gelu_grad(gz_v)).astype(BF16)

        @pl.when(pl.program_id(0) == pl.num_programs(0) - 1)
        def _():
            s = dbacc_ref[...]
            sel = (lax.broadcasted_iota(jnp.int32, (8, GW), 1) // HD
                   == lax.broadcasted_iota(jnp.int32, (8, GW), 0)).astype(BF16)
            hi = s.astype(BF16)
            r1 = s - hi.astype(F32)
            mid = r1.astype(BF16)
            lo = (r1 - mid.astype(F32)).astype(BF16)
            db_ref[...] = _dot_nt(sel, hi) + _dot_nt(sel, mid) + _dot_nt(sel, lo)
            r = lax.broadcasted_iota(jnp.int32, (BLK, BLK), 0)
            c = lax.broadcasted_iota(jnp.int32, (BLK, BLK), 1)
            for h in range(8):
                dw_ref[h] = jnp.where(c <= r, dw_ref[h], 0.0)

    return pl.pallas_call(
        body, name="gmlp_bwd", grid=(t // TM,),
        in_specs=[_rows(TM, GW), _rows(TM, GW), _rows(TM, GW), _rows(TM, 2 * GW), _const((8, BLK, BLK)),
                  _const((8, BLK, BLK)), _const((BLK, GW)), _const((1, GW))],
        out_specs=[_rows(TM, 2 * GW), _const((8, BLK, BLK)), _const((8, BLK)), _const((8, GW))],
        out_shape=[_sds((t, 2 * GW), BF16), _sds((8, BLK, BLK)), _sds((8, BLK)), _sds((8, GW))],
        scratch_shapes=[pltpu.VMEM((BLK, GW), F32)],
        compiler_params=_params())(dgm, gu, gvn, gz, ws, wst, bfull, gvw)


def _fold_half(v):
    return v + pltpu.roll(v, 64, 1)


def _swa_bwd(q, k, v, dattn, sinks):
    t = q.shape[0]
    nb = t // BLK
    ts = min(t, SB_BWD * BLK)
    sb = ts // BLK
    nt = t // ts

    def body(sink_ref, q_ref, kc_ref, kp_ref, vc_ref, vp_ref, do_ref, dq_ref, dk_ref, dv_ref, ds_ref,
             ck_ref, cv_ref, sacc_ref):
        i = pl.program_id(0)

        @pl.when(i == 0)
        def _():
            ck_ref[...] = jnp.zeros_like(ck_ref)
            cv_ref[...] = jnp.zeros_like(cv_ref)
            sacc_ref[...] = jnp.zeros_like(sacc_ref)

        @pl.when(i < nt)
        def _():
            kk = jnp.concatenate([kp_ref[...], kc_ref[...]], axis=0).astype(F32)
            vv = jnp.concatenate([vp_ref[...], vc_ref[...]], axis=0).astype(F32)
            lo256 = _lane((2 * BLK, 128)) < 64
            acc_k = [jnp.zeros((BLK, 128), F32) for _ in range(sb + 1)]
            acc_v = [jnp.zeros((BLK, 128), F32) for _ in range(sb + 1)]
            for b in range(sb):
                r = slice(b * BLK, (b + 1) * BLK)
                kb, vb = kk[b * BLK:(b + 2) * BLK], vv[b * BLK:(b + 2) * BLK]
                mask = _swa_mask(i == 0) if b == 0 else _swa_mask(False)
                dkk = jnp.zeros((2 * BLK, 128), F32)
                dvv = jnp.zeros((2 * BLK, 128), F32)
                for g in range(2):
                    qs = _stack_heads(q_ref[r, 256 * g:256 * g + 128], q_ref[r, 256 * g + 128:256 * g + 256])
                    dos = _stack_heads(do_ref[r, 256 * g:256 * g + 128],
                                       do_ref[r, 256 * g + 128:256 * g + 256]).astype(BF16)
                    kd = _dup_half(kb, g)
                    pn, psn = _swa_probs(qs, kd, mask, _sink_col(sink_ref, g))
                    dp = _dot_nt(dos, _dup_half(vb, g))
                    dd = jnp.sum(pn * dp, axis=-1, keepdims=True)
                    ds = (pn * (dp - dd) * (1.0 / math.sqrt(HD))).astype(BF16)
                    sacc_ref[g] += jnp.broadcast_to(-psn * dd, (4 * BLK, 128))
                    dqa, dqb = _unstack_heads(_dot(ds, kd))
                    dq_ref[r, 256 * g:256 * g + 128] = dqa
                    dq_ref[r, 256 * g + 128:256 * g + 256] = dqb
                    dkg = _fold_half(_dot_tn(ds, qs))
                    dvg = _fold_half(_dot_tn(pn.astype(BF16), dos))
                    keep = lo256 if g == 0 else jnp.logical_not(lo256)
                    dkk = jnp.where(keep, dkg, dkk)
                    dvv = jnp.where(keep, dvg, dvv)
                acc_k[b], acc_k[b + 1] = acc_k[b] + dkk[0:BLK], acc_k[b + 1] + dkk[BLK:]
                acc_v[b], acc_v[b + 1] = acc_v[b] + dvv[0:BLK], acc_v[b + 1] + dvv[BLK:]
            for out_ref, c_ref, acc in ((dk_ref, ck_ref, acc_k), (dv_ref, cv_ref, acc_v)):
                if sb > 1:
                    out_ref[0:ts - BLK] = c_ref[0:ts - BLK]
                out_ref[ts - BLK:ts] = c_ref[ts - BLK:ts] + acc[0]
                for b in range(sb):
                    c_ref[b * BLK:(b + 1) * BLK] = acc[b + 1]

        @pl.when(i == nt)
        def _():
            dk_ref[...] = ck_ref[...]
            dv_ref[...] = cv_ref[...]
            lane = _lane((8, 128))
            acc = jnp.zeros((8, 128), F32)
            for g in range(2):
                for j in range(4):
                    val = jnp.sum(sacc_ref[g, j * BLK:(j + 1) * BLK, :], axis=0, keepdims=True)
                    acc = jnp.where(lane == 4 * g + j, jnp.broadcast_to(val, (8, 128)), acc)
            ds_ref[...] = acc

    cur = lambda i: (jnp.minimum(i, nt - 1), 0)
    before = lambda i: (jnp.clip(i * sb - 1, 0, nb - 1), 0)
    done = lambda i: (jnp.clip(i - 1, 0, nt - 1), 0)
    return pl.pallas_call(
        body, name="swa_bwd", grid=(nt + 1,),
        in_specs=[pl.BlockSpec(memory_space=pltpu.SMEM), pl.BlockSpec((ts, AW), cur),
                  pl.BlockSpec((ts, KW), cur), pl.BlockSpec((BLK, KW), before),
                  pl.BlockSpec((ts, KW), cur), pl.BlockSpec((BLK, KW), before), pl.BlockSpec((ts, AW), cur)],
        out_specs=[pl.BlockSpec((ts, AW), cur), pl.BlockSpec((ts, KW), done), pl.BlockSpec((ts, KW), done),
                   _const((8, 128))],
        out_shape=[_sds((t, AW)), _sds((t, KW)), _sds((t, KW)), _sds((8, 128))],
        scratch_shapes=[pltpu.VMEM((ts, KW), F32), pltpu.VMEM((ts, KW), F32), pltpu.VMEM((2, 4 * BLK, 128), F32)],
        compiler_params=_params())(sinks, q, k, k, v, v, dattn)


def _mixer_in_bwd(dq, dk, dv, dgz, qk, cos, sin, x, dx1, w_in, mix_norm, qn, kn):
    t = x.shape[0]

    def body(dq_ref, dk_ref, dv_ref, dgz_ref, qk_ref, cos_ref, sin_ref, x_ref, dx1_ref, w_ref, g_ref, qn_ref, kn_ref,
             ones_ref, gx_ref, dproj_ref, dmn_ref, dqn_ref, dkn_ref, qacc_ref, kacc_ref):
        i = pl.program_id(0)

        @pl.when(i == 0)
        def _():
            dmn_ref[...] = jnp.zeros_like(dmn_ref)
            qacc_ref[...] = jnp.zeros_like(qacc_ref)
            kacc_ref[...] = jnp.zeros_like(kacc_ref)

        cos2, sin2 = cos_ref[...], sin_ref[...]
        qpre, kpre = qk_ref[:, :AW], qk_ref[:, AW:]
        dqh = _rope_bwd(dq_ref[...], jnp.tile(cos2, (1, 4)), jnp.tile(sin2, (1, 4)))
        dqpre, dgq = _rms64_bwd(dqh, qpre, _rs64(qpre, ones_ref), qn_ref[...], ones_ref)
        dkh = _rope_bwd(dk_ref[...], cos2, sin2)
        dkpre, dgk = _rms64_bwd(dkh, kpre, _rs64(kpre, ones_ref), kn_ref[...], ones_ref)
        qacc_ref[...] += jnp.sum(dgq, axis=0, keepdims=True)
        kacc_ref[...] += jnp.sum(dgk, axis=0, keepdims=True)
        dproj = jnp.concatenate([dqpre.astype(BF16), dkpre.astype(BF16), dv_ref[...].astype(BF16), dgz_ref[...]], axis=1)
        dproj_ref[...] = dproj
        dh = _dot(dproj, w_ref[...])
        xv = x_ref[...]
        dx, dg = _rms_bwd(dh, xv, _rs(xv), g_ref[...])
        gx_ref[...] = dx1_ref[...] + dx
        dmn_ref[...] += _colsum8(dg)

        @pl.when(i == pl.num_programs(0) - 1)
        def _():
            qa = qacc_ref[...]
            q4 = qa[:, 0:128] + qa[:, 128:256] + qa[:, 256:384] + qa[:, 384:512]
            dqn_ref[...] = jnp.broadcast_to(_fold_half(q4), (8, 128))
            dkn_ref[...] = jnp.broadcast_to(_fold_half(kacc_ref[...]), (8, 128))

    return pl.pallas_call(
        body, name="mixer_in_bwd", grid=(t // TM,),
        in_specs=[_rows(TM, AW), _rows(TM, KW), _rows(TM, KW), _rows(TM, 2 * GW), _rows(TM, AW + KW), _rows(TM, 128),
                  _rows(TM, 128), _rows(TM, D), _rows(TM, D), _const((IN, D)), _const((1, D)), _const((1, AW)),
                  _const((1, KW)), _const((AW, AW))],
        out_specs=[_rows(TM, D), _rows(TM, IN), _const((8, D)), _const((8, 128)), _const((8, 128))],
        out_shape=[_sds((t, D)), _sds((t, IN), BF16), _sds((8, D)), _sds((8, 128)), _sds((8, 128))],
        scratch_shapes=[pltpu.VMEM((1, AW), F32), pltpu.VMEM((1, KW), F32)],
        compiler_params=_params())(dq, dk, dv, dgz, qk, cos, sin, x, dx1, w_in, mix_norm, qn, kn, _head_ones())


def _local_step(x, mem, pos, target, p, fetch, ship):
    t = x.shape[0]
    p = dict(p)
    p.update(fetch(0, None))
    inv_freq = 1.0 / (ROPE_THETA ** (jnp.arange(HD // 2, dtype=F32) * (2.0 / HD)))
    cos, sin = _rope_tables(pos, jnp.tile(inv_freq, 4).reshape(1, 128))
    qn = jnp.tile(p["q_norm"], (1, AW // HD))
    kn = jnp.tile(p["k_norm"], (1, KW // HD))
    qn4 = jnp.tile(p["xa_q_norm"], (1, XH))
    kn4 = jnp.tile(p["xa_k_norm"], (1, XH))
    ws = p["gmlp_ws"]
    wst = jnp.swapaxes(ws, 1, 2)
    bfull = jnp.repeat(p["gmlp_bs"].T, HD, axis=1)
    conv_b = p["ffn_conv_b"]

    h1, qk, gz, q, k, v, gu, gvn = _mixer_in_fwd(x, p["mix_norm"], p["w_in"], qn, kn, p["gmlp_v_norm"], cos, sin)
    attn = _swa_fwd(q, k, v, p["attn_sinks"])
    p.update(fetch(1, attn))
    gm, ycat, x1, h2 = _mixer_out_fwd(attn, gvn, gu, ws, bfull, x, p["w_out"], p["attn_out_norm"], p["gmlp_out_norm"],
                                      p["xa_norm"])
    mh, kpre, k2, v2 = _mem_kv_fwd(mem, p["mem_norm"], p["xa_wkv"], kn4)
    qpre, o, x2, h3 = _xattn_fwd(h2, x1, p["xa_wq"], qn4, k2, v2, p["xa_wo"], p["ffn_norm"])
    p.update(fetch(2, h3))
    conv = p["ffn_conv"]
    a, u, gs, dy, loss8 = _ffn_fwd(h3, x2, target, p["ffn_up"], conv, conv_b, p["ffn_down"])

    raw = {}
    d_down = _mm_tn(u, dy, "ffn_down_bwd_w")
    dx2, da, raw["conv_sums"], raw["ffn_norm"] = _ffn_bwd(dy, a, gs, x2, p["ffn_up"], conv, p["ffn_down"], p["ffn_norm"])
    d_up = _mm_tn(da, h3, "ffn_up_bwd_w")
    token = ship(0, {"ffn_down": d_down, "ffn_up": d_up, "ffn_conv": raw["conv_sums"][:, :, 0:3]})
    dx1, dqpre, dk2, dv2, raw["xa_q_norm"], raw["xa_norm"] = _xattn_bwd(
        dx2, x1, qpre, k2, v2, p["xa_wq"], p["xa_wo"], qn4 + jnp.tile(token[0:1], (1, D // 128)), p["xa_norm"])
    d_wo = _mm_tn(o, dx2, "xa_wo_bwd_w")
    d_wq = _mm_tn(h2, dqpre, "xa_wq_bwd_w")
    d_wkv, raw["xa_k_norm"], raw["mem_norm"] = _mem_kv_bwd(mem, mh, kpre, dk2, dv2, p["xa_wkv"], kn4, p["mem_norm"])
    d_w_out = _mm_tn(ycat, dx1, "w_out_bwd_w")
    token = ship(1, {"xa_wo": d_wo, "xa_wq": d_wq, "xa_wkv": d_wkv, "w_out": d_w_out})
    dattn, dgm, raw["attn_out_norm"], raw["gmlp_out_norm"] = _mixer_out_bwd(
        dx1, attn, gm, p["w_out"], p["attn_out_norm"] + jnp.tile(token[0:1], (1, AW // 128)), p["gmlp_out_norm"])
    dgz, raw["gmlp_ws"], raw["gmlp_bs"], raw["gmlp_v_norm"] = _gmlp_bwd(dgm, gu, gvn, gz, ws, wst, bfull, p["gmlp_v_norm"])
    token = ship(2, {}, [raw["gmlp_ws"]])
    dq, dk, dv, raw["attn_sinks"] = _swa_bwd(q, k, v, dattn, p["attn_sinks"] + token[0:1, 0:8])
    grad_x, dproj, raw["mix_norm"], raw["q_norm"], raw["k_norm"] = _mixer_in_bwd(
        dq, dk, dv, dgz, qk, cos, sin, x, dx1, p["w_in"], p["mix_norm"], qn, kn)
    d_w_in = _mm_tn(dproj, h1, "w_in_bwd_w")
    return loss8[0, 0], grad_x, {"w_in": d_w_in}, raw


def _cast_shards(shards):
    def body(*refs):
        n = len(refs) // 2
        for i_ref, o_ref in zip(refs[:n], refs[n:]):
            o_ref[...] = i_ref[...].astype(BF16)

    return pl.pallas_call(body, name="cast_shards", out_shape=[_sds(s.shape, BF16) for s in shards],
                          compiler_params=pltpu.CompilerParams(vmem_limit_bytes=VMEM_LIMIT))(*shards)


HBM_SPEC = pl.BlockSpec(memory_space=pltpu.HBM)
SEM_SPEC = pl.BlockSpec(memory_space=pltpu.SEMAPHORE)


def _remote_copies(src_refs, land_refs, send_refs, recv_refs, nd):
    x, y, cc = lax.axis_index("x"), lax.axis_index("y"), lax.axis_index("c")
    me = 4 * x + 2 * y + cc
    copies = []
    for a, (src_ref, land_ref) in enumerate(zip(src_refs, land_refs)):
        for k in range(1, NDEV):
            px = 1 - x if k & 4 else x
            py = 1 - y if k & 2 else y
            pc = 1 - cc if k & 1 else cc
            copies.append(pltpu.make_async_remote_copy(
                src_ref=src_ref.at[4 * px + 2 * py + pc] if a < nd else src_ref, dst_ref=land_ref.at[me],
                send_sem=send_refs[a].at[k - 1], recv_sem=recv_refs[a].at[k - 1],
                device_id=(px, py, pc), device_id_type=pl.DeviceIdType.MESH))
    return copies


def _own_slot(src, by_dest, me):
    block = lax.dynamic_index_in_dim(src, me, 0, keepdims=True) if by_dest else src[None]
    return lax.dynamic_update_index_in_dim(lax.empty((NDEV,) + block.shape[1:], src.dtype), block, me, 0)


def _exchange_start(by_dest, for_all, me, name):
    srcs = list(by_dest) + list(for_all)
    n, nd = len(srcs), len(by_dest)
    lands = [_own_slot(s, a < nd, me) for a, s in enumerate(srcs)]

    def body(*refs):
        for cp in _remote_copies(refs[:n], refs[n:2 * n], refs[2 * n:3 * n], refs[3 * n:4 * n], nd):
            cp.start()
        refs[-1][...] = jnp.zeros((8, 128), F32)

    sems = [pltpu.SemaphoreType.DMA((NDEV - 1,))] * (2 * n)
    thru = [pltpu.HBM(v.shape, v.dtype) for v in srcs + lands]
    res = pl.pallas_call(
        body, name=name, out_shape=sems + thru + [_sds((8, 128))],
        in_specs=[HBM_SPEC] * (2 * n), out_specs=[SEM_SPEC] * (2 * n) + [HBM_SPEC] * (2 * n) + [pl.BlockSpec(memory_space=pltpu.VMEM)],
        input_output_aliases={i: 2 * n + i for i in range(2 * n)},
        compiler_params=pltpu.CompilerParams(has_side_effects=pltpu.SideEffectType.DATAFLOW_SIDE_EFFECTING))(
            *[pltpu.with_memory_space_constraint(v, pltpu.HBM) for v in srcs + lands])
    return (res[:2 * n], res[2 * n:4 * n], nd), res[-1]


def _exchange_wait(state, after, name):
    sems, thru, nd = state
    n = len(thru) // 2

    def body(*refs):
        for cp in _remote_copies(refs[:n], refs[n:2 * n], refs[2 * n:3 * n], refs[3 * n:4 * n], nd):
            cp.wait_send()
            cp.wait_recv()

    res = pl.pallas_call(
        body, name=name, out_shape=[pltpu.HBM(v.shape, v.dtype) for v in thru],
        in_specs=[HBM_SPEC] * (2 * n) + [SEM_SPEC] * (2 * n) + [pl.BlockSpec(memory_space=pl.ANY)],
        out_specs=[HBM_SPEC] * (2 * n), input_output_aliases={i: i for i in range(2 * n)},
        compiler_params=pltpu.CompilerParams(has_side_effects=pltpu.SideEffectType.DATAFLOW_SIDE_EFFECTING))(
            *thru, *sems, after)
    return res[n:]


def _adam(parts, w, m, v, name):
    def body(p_ref, w_ref, m_ref, v_ref, g_ref, d_ref, nm_ref, nv_ref):
        g = _sum_parts(p_ref)
        g_ref[...] = g
        d_ref[...], nm_ref[...], nv_ref[...] = _adam_math(g, w_ref[...], m_ref[...], v_ref[...])

    return pl.pallas_call(
        body, name=name, out_shape=[_sds(w.shape)] * 4,
        compiler_params=pltpu.CompilerParams(vmem_limit_bytes=VMEM_LIMIT))(parts, w, m, v)


GATHER_GROUPS = (("w_in",), ("w_out", "xa_wkv", "xa_wq", "xa_wo"), ("ffn_up", "ffn_conv", "ffn_down"))
SCATTER_GROUPS = (("ffn_down", "ffn_up", "ffn_conv"), ("xa_wo", "xa_wq", "xa_wkv", "w_out"), (), ("w_in",))
BIG = tuple(n for grp in GATHER_GROUPS for n in grp)
BY_COLUMN = ("w_in", "ffn_up")
VECS = (("mix_norm", D), ("q_norm", HD), ("k_norm", HD), ("attn_sinks", 8), ("gmlp_v_norm", GW), ("attn_out_norm", AW),
        ("gmlp_out_norm", GW), ("xa_norm", D), ("mem_norm", D), ("xa_q_norm", XD), ("xa_k_norm", XD), ("ffn_norm", D))
BS_ROW = 16
VEC_ROWS = 24
SMALL = tuple(n for n, _ in VECS) + ("gmlp_bs", "gmlp_ws", "ffn_conv_b")


def _pack_small(raw):
    names = [n for n, _ in VECS] + ["gmlp_bs", "conv_sums"]

    def body(*refs):
        ins = dict(zip(names, refs))
        vec_ref, cb_ref = refs[len(names):]
        vec_ref[...] = jnp.zeros_like(vec_ref)
        for r, (n, w) in enumerate(VECS):
            vec_ref[r:r + 1, 0:w] = ins[n][0:1, 0:w]
        vec_ref[BS_ROW:BS_ROW + 8, 0:BLK] = ins["gmlp_bs"][...]
        for s in range(2):
            for d in range(NG):
                cb_ref[s, d] = ins["conv_sums"][s, d, 3:4, :]

    return pl.pallas_call(body, name="pack_small", out_shape=[_sds((VEC_ROWS, D)), _sds((2, NG, 1, SW))])(
        *[raw[n] for n in names])


def _adam_math(g, w, m, v):
    nm = B1 * m + (1.0 - B1) * g
    nv = B2 * v + (1.0 - B2) * (g * g)
    m_hat = nm / (1.0 - B1 ** STEP)
    v_hat = nv / (1.0 - B2 ** STEP)
    return -LR * (m_hat / (jnp.sqrt(v_hat) + AEPS) + WD * w), nm, nv


def _sum_parts(p_ref):
    g = p_ref[0].astype(F32)
    for j in range(1, NDEV):
        g = g + p_ref[j].astype(F32)
    return g


def _adam_small(parts_vec, parts_ws, parts_cb, w, m, v):
    def body(*refs):
        pv_ref, pws_ref, pcb_ref = refs[:3]
        ins = refs[3:3 + 3 * len(SMALL)]
        outs = refs[3 + 3 * len(SMALL):]
        gv = _sum_parts(pv_ref)
        for j, n in enumerate(SMALL):
            w_ref, m_ref, v_ref = ins[3 * j:3 * j + 3]
            o = outs[4 * j:4 * j + 4]
            if n == "gmlp_ws":
                g = _sum_parts(pws_ref)
            elif n == "ffn_conv_b":
                g = _sum_parts(pcb_ref)
            elif n == "gmlp_bs":
                g = gv[BS_ROW:BS_ROW + 8, 0:BLK]
            else:
                g = gv[j:j + 1, 0:VECS[j][1]]
            lead = n in ("gmlp_ws", "gmlp_bs")
            res = (g,) + _adam_math(g, w_ref[0] if lead else w_ref[...], m_ref[0] if lead else m_ref[...],
                                    v_ref[0] if lead else v_ref[...])
            for o_ref, val in zip(o, res):
                if lead:
                    o_ref[0] = val
                else:
                    o_ref[...] = val

    args = [parts_vec, parts_ws, parts_cb] + [d[n] for n in SMALL for d in (w, m, v)]
    res = pl.pallas_call(body, name="adam_small", out_shape=[_sds(w[n].shape) for n in SMALL for _ in range(4)],
                         compiler_params=pltpu.CompilerParams(vmem_limit_bytes=VMEM_LIMIT))(*args)
    return {n: tuple(res[4 * j:4 * j + 4]) for j, n in enumerate(SMALL)}


def kernel(x, mem, positions, mix_norm, w_in, q_norm, k_norm, attn_sinks, gmlp_v_norm, gmlp_ws, gmlp_bs, attn_out_norm, gmlp_out_norm, w_out, xa_norm, mem_norm, xa_wq, xa_wkv, xa_q_norm, xa_k_norm, xa_wo, ffn_norm, ffn_up, ffn_conv, ffn_conv_b, ffn_down, loss_target, m_mix_norm, m_w_in, m_q_norm, m_k_norm, m_attn_sinks, m_gmlp_v_norm, m_gmlp_ws, m_gmlp_bs, m_attn_out_norm, m_gmlp_out_norm, m_w_out, m_xa_norm, m_mem_norm, m_xa_wq, m_xa_wkv, m_xa_q_norm, m_xa_k_norm, m_xa_wo, m_ffn_norm, m_ffn_up, m_ffn_conv, m_ffn_conv_b, m_ffn_down, v_mix_norm, v_w_in, v_q_norm, v_k_norm, v_attn_sinks, v_gmlp_v_norm, v_gmlp_ws, v_gmlp_bs, v_attn_out_norm, v_gmlp_out_norm, v_w_out, v_xa_norm, v_mem_norm, v_xa_wq, v_xa_wkv, v_xa_q_norm, v_xa_k_norm, v_xa_wo, v_ffn_norm, v_ffn_up, v_ffn_conv, v_ffn_conv_b, v_ffn_down):
    names = ("mix_norm", "w_in", "q_norm", "k_norm", "attn_sinks", "gmlp_v_norm", "gmlp_ws", "gmlp_bs", "attn_out_norm",
             "gmlp_out_norm", "w_out", "xa_norm", "mem_norm", "xa_wq", "xa_wkv", "xa_q_norm", "xa_k_norm", "xa_wo",
             "ffn_norm", "ffn_up", "ffn_conv", "ffn_conv_b", "ffn_down")
    w = dict(zip(names, (mix_norm, w_in, q_norm, k_norm, attn_sinks, gmlp_v_norm, gmlp_ws, gmlp_bs, attn_out_norm,
                         gmlp_out_norm, w_out, xa_norm, mem_norm, xa_wq, xa_wkv, xa_q_norm, xa_k_norm, xa_wo, ffn_norm,
                         ffn_up, ffn_conv, ffn_conv_b, ffn_down)))
    m = dict(zip(names, (m_mix_norm, m_w_in, m_q_norm, m_k_norm, m_attn_sinks, m_gmlp_v_norm, m_gmlp_ws, m_gmlp_bs,
                         m_attn_out_norm, m_gmlp_out_norm, m_w_out, m_xa_norm, m_mem_norm, m_xa_wq, m_xa_wkv,
                         m_xa_q_norm, m_xa_k_norm, m_xa_wo, m_ffn_norm, m_ffn_up, m_ffn_conv, m_ffn_conv_b, m_ffn_down)))
    v = dict(zip(names, (v_mix_norm, v_w_in, v_q_norm, v_k_norm, v_attn_sinks, v_gmlp_v_norm, v_gmlp_ws, v_gmlp_bs,
                         v_attn_out_norm, v_gmlp_out_norm, v_w_out, v_xa_norm, v_mem_norm, v_xa_wq, v_xa_wkv,
                         v_xa_q_norm, v_xa_k_norm, v_xa_wo, v_ffn_norm, v_ffn_up, v_ffn_conv, v_ffn_conv_b, v_ffn_down)))
    t = x.shape[1]

    me = 4 * lax.axis_index("x") + 2 * lax.axis_index("y") + lax.axis_index("c")

    def rows(a, n):
        return jnp.swapaxes(a[0], 0, 1) if n in BY_COLUMN else a[0]

    mats = [n for n in BIG if n != "ffn_conv"]
    shard = dict(zip(mats, _cast_shards([rows(w[n], n) for n in mats])), ffn_conv=w["ffn_conv"][0])
    gathers, tokens = zip(*[_exchange_start([], [shard[n] for n in grp], me, "gather_start_%d" % i)
                            for i, grp in enumerate(GATHER_GROUPS)])

    def fetch(i, after):
        after = tokens[0] + tokens[1] + tokens[2] if after is None else after
        got = dict(zip(GATHER_GROUPS[i], _exchange_wait(gathers[i], after, "gather_wait_%d" % i)))
        if "w_in" in got:
            got["w_in"] = got["w_in"].reshape(IN, D)
        for n in ("w_out", "xa_wq", "xa_wo"):
            if n in got:
                got[n] = got[n].reshape(D, D)
        if "ffn_down" in got:
            got["ffn_down"] = got["ffn_down"].reshape(NG, SW, D)
            got["ffn_conv"] = got["ffn_conv"].reshape(2, NG, 3, SW)
        return got

    scatters = []

    def ship(i, grads, for_all=()):
        by_dest = [grads[n].reshape((NDEV,) + shard[n].shape) for n in SCATTER_GROUPS[i]]
        state, token = _exchange_start(by_dest, for_all, me, "scatter_start_%d" % i)
        scatters.append(state)
        return token

    conv_b = {k: d["ffn_conv_b"].reshape(NDEV, 1, SW) for k, d in (("w", w), ("m", m), ("v", v))}
    p = {n: w[n] for n in SMALL[:-1]}
    p["gmlp_ws"], p["gmlp_bs"] = w["gmlp_ws"][0], w["gmlp_bs"][0]
    p["ffn_conv_b"] = conv_b["w"].reshape(2, NG, 1, SW)
    loss, grad_x, g, raw = _local_step(x[0], mem[0], positions.reshape(t, 1), loss_target[0], p, fetch, ship)
    loss = lax.psum(loss, AXES)

    vec, cb = _pack_small(raw)
    after = ship(3, g, [vec, cb.reshape(NDEV, 1, SW)])
    res, rest = {}, []
    for i, grp in enumerate(SCATTER_GROUPS):
        got = _exchange_wait(scatters[i], after, "scatter_wait_%d" % i)
        rest += got[len(grp):]
        for n, parts in zip(grp, got):
            out = _adam(parts, rows(w[n], n), rows(m[n], n), rows(v[n], n), "adam_" + n)
            res[n] = [jnp.swapaxes(o, 0, 1) if n in BY_COLUMN else o for o in out]
            after = out[0]
    ws_parts, vec_parts, cb_parts = rest
    small = lambda d, k: {**{n: d[n] for n in SMALL[:-1]}, "ffn_conv_b": conv_b[k]}
    res.update(_adam_small(vec_parts, ws_parts, cb_parts, small(w, "w"), small(m, "m"), small(v, "v")))

    outs = [loss, grad_x[None]]
    for j in range(4):
        outs += [res[n][j].reshape(w[n].shape) for n in names]
    return tuple(outs)
```

```python
import functools
import math

import jax
import jax.numpy as jnp
from jax import lax
from jax.experimental import pallas as pl
from jax.experimental.pallas import tpu as pltpu

F32 = jnp.float32
BF16 = jnp.bfloat16

D = 1024
HD = 64
AW = 512
KW = 128
GW = 512
IN = AW + 2 * KW + 2 * GW
BLK = 128
MEM = 256
XH = 4
XD = 256
FF = 2816
EPS = 1e-6
ROPE_THETA = 10000.0
NDEV = 8
LR, B1, B2, AEPS, WD, STEP = 0.001, 0.9, 0.999, 1e-08, 0.01, 10

TM = 512
WK = 2048
VMEM_LIMIT = 56 * 1024 * 1024
NEG = float(jnp.finfo(jnp.float32).min)
GELU_C0 = math.sqrt(2.0 / math.pi)
GELU_C1 = 0.044715
AXES = ("x", "y", "c")


def _dot(a, b):
    return jnp.dot(a, b, preferred_element_type=F32)


def _dot_nt(a, b):
    return lax.dot_general(a, b, (((1,), (1,)), ((), ())), preferred_element_type=F32)


def _dot_tn(a, b):
    return lax.dot_general(a, b, (((0,), (0,)), ((), ())), preferred_element_type=F32)


def _rs(x):
    return lax.rsqrt(jnp.mean(x * x, axis=-1, keepdims=True) + EPS)


def _rms_bwd(dy, x, r, g):
    xh = x * r
    dxh = dy * g
    dx = r * (dxh - xh * jnp.mean(dxh * xh, axis=-1, keepdims=True))
    return dx, dy * xh


def _lane(shape):
    return lax.broadcasted_iota(jnp.int32, shape, len(shape) - 1)


def _gsum64(v, ones_ref):
    w = v.shape[-1]
    ones = ones_ref[0:w, 0:w]
    hi = v.astype(BF16)
    lo = (v - hi.astype(F32)).astype(BF16)
    return _dot(hi, ones) + _dot(lo, ones)


def _head_ones():
    i = jnp.arange(AW) // HD
    return (i[:, None] == i[None, :]).astype(BF16)


def _rs64(x, ones_ref):
    return lax.rsqrt(_gsum64(x * x, ones_ref) * (1.0 / HD) + EPS)


def _rms64_bwd(dy, x, r, g, ones_ref):
    xh = x * r
    dxh = dy * g
    dx = r * (dxh - xh * (_gsum64(dxh * xh, ones_ref) * (1.0 / HD)))
    return dx, dy * xh


def _rot_half(v):
    w = v.shape[-1]
    return jnp.where((_lane(v.shape) & 32) == 0, pltpu.roll(v, w - 32, 1), pltpu.roll(v, 32, 1))


def _rope(v, cos, sin_signed):
    return v * cos + _rot_half(v) * sin_signed


def _rope_bwd(dv, cos, sin_signed):
    return dv * cos + _rot_half(dv * sin_signed)


def _gelu(z):
    return 0.5 * z * (1.0 + jnp.tanh(GELU_C0 * (z + GELU_C1 * z * z * z)))


def _gelu_grad(z):
    t = jnp.tanh(GELU_C0 * (z + GELU_C1 * z * z * z))
    return 0.5 * (1.0 + t) + 0.5 * z * (1.0 - t * t) * (GELU_C0 * (1.0 + 3.0 * GELU_C1 * z * z))


def _colsum8(v):
    s = jnp.sum(v, axis=0, keepdims=True)
    row = lax.broadcasted_iota(jnp.int32, (8, v.shape[1]), 0)
    return jnp.where(row == 0, jnp.broadcast_to(s, (8, v.shape[1])), 0.0)


def _params(n_axes=1):
    return pltpu.CompilerParams(dimension_semantics=("arbitrary",) * n_axes, vmem_limit_bytes=VMEM_LIMIT)


def _rows(tm, w):
    return pl.BlockSpec((tm, w), lambda i: (i, 0))


def _const(shape):
    nd = len(shape)
    return pl.BlockSpec(shape, lambda *_: (0,) * nd)


def _sds(shape, dtype=F32):
    return jax.ShapeDtypeStruct(shape, dtype)


def _mm_tn(a, b, name):
    g = max(a.shape[0] if a.ndim == 3 else 1, b.shape[0] if b.ndim == 3 else 1)
    t, m = a.shape[-2:]
    n = b.shape[-1]

    def body(a_ref, b_ref, o_ref, acc_ref):
        i = pl.program_id(1)

        @pl.when(i == 0)
        def _():
            acc_ref[...] = jnp.zeros_like(acc_ref)

        acc_ref[...] += _dot_tn(a_ref[...].astype(BF16), b_ref[...].astype(BF16))

        @pl.when(i == pl.num_programs(1) - 1)
        def _():
            o_ref[...] = acc_ref[...].astype(BF16)

    tk = min(t, WK)

    def spec(v):
        w = v.shape[-1]
        if v.ndim == 3:
            return pl.BlockSpec((None, tk, w), lambda j, i: (j, i, 0))
        return pl.BlockSpec((tk, w), lambda j, i: (i, 0))

    return pl.pallas_call(
        body, name=name, grid=(g, t // tk), in_specs=[spec(a), spec(b)],
        out_specs=pl.BlockSpec((None, m, n), lambda j, i: (j, 0, 0)), out_shape=_sds((g, m, n), BF16),
        scratch_shapes=[pltpu.VMEM((m, n), F32)], compiler_params=_params(2))(a, b)


def _rope_tables(pos, inv_freq):
    t = pos.shape[0]

    def body(pos_ref, f_ref, cos_ref, sin_ref):
        ang = pos_ref[...].astype(F32) * f_ref[...]
        sign = jnp.where((_lane(ang.shape) & 32) == 0, -1.0, 1.0)
        cos_ref[...] = jnp.cos(ang)
        sin_ref[...] = jnp.sin(ang) * sign

    return pl.pallas_call(
        body, name="rope_tables", grid=(t // TM,),
        in_specs=[_rows(TM, 1), _const((1, 128))], out_specs=[_rows(TM, 128), _rows(TM, 128)],
        out_shape=[_sds((t, 128)), _sds((t, 128))], compiler_params=_params())(pos, inv_freq)


def _mixer_in_fwd(x, mix_norm, w_in, qn, kn, gvw, cos, sin):
    t = x.shape[0]

    def body(x_ref, g_ref, w_ref, qn_ref, kn_ref, gvw_ref, cos_ref, sin_ref, ones_ref,
             h_ref, qk_ref, gz_ref, q_ref, k_ref, v_ref, gu_ref, gvn_ref):
        x = x_ref[...]
        h = (x * _rs(x) * g_ref[...]).astype(BF16)
        h_ref[...] = h
        proj = _dot_nt(h, w_ref[...])
        qk = proj[:, :AW + KW]
        qk_ref[...] = qk
        gz = proj[:, AW + 2 * KW:]
        gz_ref[...] = gz
        cos2, sin2 = cos_ref[...], sin_ref[...]
        q = qk[:, :AW]
        q = q * _rs64(q, ones_ref) * qn_ref[...]
        q_ref[...] = _rope(q, jnp.tile(cos2, (1, 4)), jnp.tile(sin2, (1, 4))).astype(BF16)
        k = qk[:, AW:]
        k = k * _rs64(k, ones_ref) * kn_ref[...]
        k_ref[...] = _rope(k, cos2, sin2).astype(BF16)
        v_ref[...] = proj[:, AW + KW:AW + 2 * KW].astype(BF16)
        gu_ref[...] = _gelu(gz[:, :GW])
        gv = _gelu(gz[:, GW:])
        gvn_ref[...] = (gv * _rs(gv) * gvw_ref[...]).astype(BF16)

    return pl.pallas_call(
        body, name="mixer_in_fwd", grid=(t // TM,),
        in_specs=[_rows(TM, D), _const((1, D)), _const((IN, D)), _const((1, AW)), _const((1, KW)),
                  _const((1, GW)), _rows(TM, 128), _rows(TM, 128), _const((AW, AW))],
        out_specs=[_rows(TM, D), _rows(TM, AW + KW), _rows(TM, 2 * GW), _rows(TM, AW), _rows(TM, KW),
                   _rows(TM, KW), _rows(TM, GW), _rows(TM, GW)],
        out_shape=[_sds((t, D), BF16), _sds((t, AW + KW)), _sds((t, 2 * GW)), _sds((t, AW), BF16),
                   _sds((t, KW), BF16), _sds((t, KW), BF16), _sds((t, GW)), _sds((t, GW), BF16)],
        compiler_params=_params())(x, mix_norm, w_in, qn, kn, gvw, cos, sin, _head_ones())


def _dup_half(kk, g):
    lane = _lane(kk.shape)
    other = pltpu.roll(kk, 64, 1)
    keep = (lane < 64) if g == 0 else (lane >= 64)
    return jnp.where(keep, kk, other).astype(BF16)


def _swa_mask(first_block):
    qi = lax.broadcasted_iota(jnp.int32, (4 * BLK, 2 * BLK), 0) & (BLK - 1)
    kj = lax.broadcasted_iota(jnp.int32, (4 * BLK, 2 * BLK), 1)
    diff = qi + BLK - kj
    band = (diff >= 0) & (diff < BLK)
    return band & (jnp.logical_not(first_block) | (kj >= BLK))


def _stack_heads(a2, b2):
    lo = _lane(a2.shape) < 64
    z = jnp.zeros_like(a2)
    return jnp.concatenate([jnp.where(lo, a2, z), jnp.where(lo, z, a2), jnp.where(lo, b2, z), jnp.where(lo, z, b2)], axis=0)


def _unstack_heads(o):
    lo = _lane((BLK, 128)) < 64
    return jnp.where(lo, o[0:BLK], o[BLK:2 * BLK]), jnp.where(lo, o[2 * BLK:3 * BLK], o[3 * BLK:4 * BLK])


def _sink_col(sink_ref, g):
    row = lax.broadcasted_iota(jnp.int32, (4 * BLK, 1), 0)
    s = [sink_ref[0, 4 * g + j] for j in range(4)]
    return jnp.where(row < BLK, s[0], jnp.where(row < 2 * BLK, s[1], jnp.where(row < 3 * BLK, s[2], s[3])))


def _swa_probs(qs, kd, mask, sink):
    s = _dot_nt(qs, kd) * (1.0 / math.sqrt(HD))
    s = jnp.where(mask, s, NEG)
    m = jnp.maximum(jnp.max(s, axis=-1, keepdims=True), sink)
    p = jnp.exp(s - m)
    ps = jnp.exp(sink - m)
    inv = 1.0 / (jnp.sum(p, axis=-1, keepdims=True) + ps)
    return p * inv, ps * inv


SB = 4
SB_BWD = 2


def _swa_fwd(q, k, v, sinks):
    t = q.shape[0]
    ts = min(t, SB * BLK)

    def body(sink_ref, q_ref, kc_ref, kp_ref, vc_ref, vp_ref, o_ref):
        i = pl.program_id(0)
        kk = jnp.concatenate([kp_ref[...], kc_ref[...]], axis=0).astype(F32)
        vv = jnp.concatenate([vp_ref[...], vc_ref[...]], axis=0).astype(F32)
        for b in range(ts // BLK):
            r = slice(b * BLK, (b + 1) * BLK)
            kb, vb = kk[b * BLK:(b + 2) * BLK], vv[b * BLK:(b + 2) * BLK]
            mask = _swa_mask(i == 0) if b == 0 else _swa_mask(False)
            for g in range(2):
                qs = _stack_heads(q_ref[r, 256 * g:256 * g + 128], q_ref[r, 256 * g + 128:256 * g + 256])
                pn, _ = _swa_probs(qs, _dup_half(kb, g), mask, _sink_col(sink_ref, g))
                oa, ob = _unstack_heads(_dot(pn.astype(BF16), _dup_half(vb, g)))
                o_ref[r, 256 * g:256 * g + 128] = oa
                o_ref[r, 256 * g + 128:256 * g + 256] = ob

    cur = lambda i: (i, 0)
    prev = lambda i: (jnp.maximum(i * (ts // BLK) - 1, 0), 0)
    return pl.pallas_call(
        body, name="swa_fwd", grid=(t // ts,),
        in_specs=[pl.BlockSpec(memory_space=pltpu.SMEM), pl.BlockSpec((ts, AW), cur),
                  pl.BlockSpec((ts, KW), cur), pl.BlockSpec((BLK, KW), prev),
                  pl.BlockSpec((ts, KW), cur), pl.BlockSpec((BLK, KW), prev)],
        out_specs=pl.BlockSpec((ts, AW), cur), out_shape=_sds((t, AW)),
        compiler_params=_params())(sinks, q, k, k, v, v)


def _causal_bf16(w_ref, h, transposed):
    r = lax.broadcasted_iota(jnp.int32, (BLK, BLK), 0)
    c = lax.broadcasted_iota(jnp.int32, (BLK, BLK), 1)
    keep = (r <= c) if transposed else (c <= r)
    return jnp.where(keep, w_ref[h], 0.0).astype(BF16)


def _gmlp_mix(w_ref, xin, transposed):
    lo = _lane((BLK, 128)) < 64
    wm = [_causal_bf16(w_ref, h, transposed) for h in range(8)]
    rows = []
    for c in range(xin.shape[0] // BLK):
        cols = []
        for j in range(4):
            xs = xin[c * BLK:(c + 1) * BLK, 128 * j:128 * (j + 1)]
            cols.append(jnp.where(lo, _dot(wm[2 * j], xs), _dot(wm[2 * j + 1], xs)))
        rows.append(jnp.concatenate(cols, axis=1))
    return jnp.concatenate(rows, axis=0)


def _mixer_out_fwd(attn, gvn, gu, ws, bfull, x, w_out, aon, gon, xan):
    t = x.shape[0]

    def body(a_ref, v_ref, gu_ref, ws_ref, b_ref, x_ref, w_ref, aon_ref, gon_ref, xan_ref, gm_ref, y_ref, x1_ref, h2_ref):
        a = a_ref[...]
        g = gu_ref[...] * (_gmlp_mix(ws_ref, v_ref[...], False) + jnp.tile(b_ref[...], (TM // BLK, 1)))
        gm_ref[...] = g
        y = jnp.concatenate([a * _rs(a) * aon_ref[...], g * _rs(g) * gon_ref[...]], axis=1).astype(BF16)
        y_ref[...] = y
        x1 = x_ref[...] + _dot(y, w_ref[...])
        x1_ref[...] = x1
        h2_ref[...] = (x1 * _rs(x1) * xan_ref[...]).astype(BF16)

    return pl.pallas_call(
        body, name="mixer_out_fwd", grid=(t // TM,),
        in_specs=[_rows(TM, AW), _rows(TM, GW), _rows(TM, GW), _const((8, BLK, BLK)), _const((BLK, GW)), _rows(TM, D),
                  _const((D, D)), _const((1, AW)), _const((1, GW)), _const((1, D))],
        out_specs=[_rows(TM, GW), _rows(TM, D), _rows(TM, D), _rows(TM, D)],
        out_shape=[_sds((t, GW)), _sds((t, D), BF16), _sds((t, D)), _sds((t, D), BF16)],
        compiler_params=_params())(attn, gvn, gu, ws, bfull, x, w_out, aon, gon, xan)


def _mem_kv_fwd(mem, mem_norm, wkv, kn4):
    def body(m_ref, g_ref, w_ref, kn_ref, mh_ref, kpre_ref, k_ref, v_ref):
        m = m_ref[...]
        mh = (m * _rs(m) * g_ref[...]).astype(BF16)
        mh_ref[...] = mh
        for h in range(XH):
            sl = slice(XD * h, XD * (h + 1))
            kh = _dot(mh, w_ref[h])
            kpre_ref[:, sl] = kh
            k_ref[:, sl] = (kh * _rs(kh) * kn_ref[:, sl]).astype(BF16)
            v_ref[:, sl] = _dot(mh, w_ref[XH + h]).astype(BF16)

    return pl.pallas_call(
        body, name="mem_kv_fwd",
        out_shape=[_sds((MEM, D), BF16), _sds((MEM, D)), _sds((MEM, D), BF16), _sds((MEM, D), BF16)],
        compiler_params=pltpu.CompilerParams(vmem_limit_bytes=VMEM_LIMIT))(mem, mem_norm, wkv, kn4)


def _xattn_probs(qpre_h, qn_h, k_h):
    rq = _rs(qpre_h)
    q2 = (qpre_h * rq * qn_h).astype(BF16)
    s = _dot_nt(q2, k_h) * (1.0 / math.sqrt(XD))
    p = jnp.exp(s - jnp.max(s, axis=-1, keepdims=True))
    return p * (1.0 / jnp.sum(p, axis=-1, keepdims=True)), q2, rq


def _xattn_fwd(h2, x1, wq, qn4, k2, v2, wo, ffn_norm):
    t = x1.shape[0]

    def body(h_ref, x_ref, wq_ref, qn_ref, k_ref, v_ref, wo_ref, fn_ref, qpre_ref, o_ref, x2_ref, h3_ref):
        qpre = _dot(h_ref[...], wq_ref[...])
        qpre_ref[...] = qpre
        outs = []
        for h in range(XH):
            sl = slice(XD * h, XD * (h + 1))
            pn, _, _ = _xattn_probs(qpre[:, sl], qn_ref[:, sl], k_ref[:, sl])
            outs.append(_dot(pn.astype(BF16), v_ref[:, sl]))
        o = jnp.concatenate(outs, axis=1).astype(BF16)
        o_ref[...] = o
        x2 = x_ref[...] + _dot(o, wo_ref[...])
        x2_ref[...] = x2
        h3_ref[...] = (x2 * _rs(x2) * fn_ref[...]).astype(BF16)

    return pl.pallas_call(
        body, name="xattn_fwd", grid=(t // TM,),
        in_specs=[_rows(TM, D), _rows(TM, D), _const((D, D)), _const((1, D)), _const((MEM, D)), _const((MEM, D)),
                  _const((D, D)), _const((1, D))],
        out_specs=[_rows(TM, D)] * 4,
        out_shape=[_sds((t, D)), _sds((t, D), BF16), _sds((t, D)), _sds((t, D), BF16)],
        compiler_params=_params())(h2, x1, wq, qn4, k2, v2, wo, ffn_norm)


SW = 704
NG = FF // SW
FM = 256
HALO = 16


def _resident(shape):
    nd = len(shape)
    return pl.BlockSpec(shape, lambda *_: (0,) * nd, pipeline_mode=pl.Buffered(1))


def _halo_before(i):
    return jnp.maximum(i * (FM // HALO) - 1, 0)


def _conv(e, w):
    return w[2:3, :] * e + pltpu.roll(w[1:2, :] * e + pltpu.roll(w[0:1, :] * e, 1, 0), 1, 0)


def _conv_t(dc, w):
    n = dc.shape[0]
    return w[2:3, :] * dc + pltpu.roll(w[1:2, :] * dc + pltpu.roll(w[0:1, :] * dc, n - 1, 0), n - 1, 0)


def _ffn_fwd(h3, x2, target, up, conv, conv_b, down):
    t = x2.shape[0]

    def body(h_ref, hp_ref, x_ref, t_ref, up_ref, w_ref, b_ref, dn_ref, a_ref, u_ref, gs_ref, dy_ref, loss_ref, acc_ref):
        i = pl.program_id(0)

        @pl.when(i == 0)
        def _():
            acc_ref[...] = jnp.zeros_like(acc_ref)

        before = jnp.where(i > 0, hp_ref[...], jnp.zeros_like(hp_ref))
        he = jnp.concatenate([before, h_ref[...]], axis=0)
        err = x_ref[...] - t_ref[...]
        for d in range(NG):
            c = []
            for s in range(2):
                a = _dot_nt(he, up_ref[s * NG + d])
                a_ref[s * NG + d] = a[HALO:].astype(BF16)
                c.append(_conv(a, w_ref[s, d])[HALO:] + b_ref[s, d])
            gl, gg = _gelu_and_grad(c[0])
            gs_ref[d] = gl.astype(BF16)
            gs_ref[NG + d] = (gg * c[1]).astype(BF16)
            u = (gl * c[1]).astype(BF16)
            u_ref[d] = u
            err = err + _dot(u, dn_ref[d])
        dy_ref[...] = err * (1.0 / D)
        acc_ref[...] += jnp.sum(err * err, axis=0, keepdims=True)

        @pl.when(i == pl.num_programs(0) - 1)
        def _():
            loss_ref[...] = jnp.full((8, 128), 0.5 / D, F32) * jnp.sum(acc_ref[...])

    return pl.pallas_call(
        body, name="ffn_fwd", grid=(t // FM,),
        in_specs=[_rows(FM, D), pl.BlockSpec((HALO, D), lambda i: (_halo_before(i), 0)), _rows(FM, D), _rows(FM, D),
                  _resident((NDEV, SW, D)), _resident((2, NG, 3, SW)), _resident((2, NG, 1, SW)), _resident((NG, SW, D))],
        out_specs=[pl.BlockSpec((NDEV, FM, SW), lambda i: (0, i, 0)), pl.BlockSpec((NG, FM, SW), lambda i: (0, i, 0)),
                   pl.BlockSpec((NDEV, FM, SW), lambda i: (0, i, 0)), _rows(FM, D), _const((8, 128))],
        out_shape=[_sds((NDEV, t, SW), BF16), _sds((NG, t, SW), BF16), _sds((NDEV, t, SW), BF16), _sds((t, D)),
                   _sds((8, 128))],
        scratch_shapes=[pltpu.VMEM((1, D), F32)], compiler_params=_params())(h3, h3, x2, target, up, conv, conv_b, down)


def _gelu_and_grad(z):
    z2 = z * z
    t = jnp.tanh(GELU_C0 * (z + GELU_C1 * z * z2))
    phi = 0.5 * (1.0 + t)
    return z * phi, phi + z * (1.0 - t * t) * (0.5 * GELU_C0 + (1.5 * GELU_C0 * GELU_C1) * z2)


def _ffn_bwd(dy, a, gs, x2, up, conv, down, ffn_norm):
    t = x2.shape[0]
    nt = t // FM
    n = FM + HALO

    def body(dy_ref, dyn_ref, a_ref, gs_ref, gsn_ref, x_ref, up_ref, w_ref, dn_ref, g_ref,
             dx_ref, da_ref, s_ref, dfn_ref):
        i = pl.program_id(0)

        @pl.when(i == 0)
        def _():
            s_ref[...] = jnp.zeros_like(s_ref)
            dfn_ref[...] = jnp.zeros_like(dfn_ref)

        last = i == nt - 1
        dy = dy_ref[...]
        dye = jnp.concatenate([dy, jnp.where(last, 0.0, dyn_ref[...])], axis=0).astype(BF16)
        dh = jnp.zeros((FM, D), F32)
        row = lax.broadcasted_iota(jnp.int32, (8, SW), 0)
        for d in range(NG):
            du = _dot_nt(dye, dn_ref[d])
            for s in range(2):
                j = s * NG + d
                k = NG + d if s == 0 else d
                dc = du * jnp.concatenate([gs_ref[k], gsn_ref[k]], axis=0).astype(F32)
                w = w_ref[s, d]
                tile = a_ref[j].astype(F32)
                d1 = pltpu.roll(dc, n - 1, 0)
                d2 = pltpu.roll(d1, n - 1, 0)
                da = (w[2:3, :] * dc + w[1:2, :] * d1 + w[0:1, :] * d2)[0:FM].astype(BF16)
                da_ref[j] = da
                dh = dh + _dot(da, up_ref[j])
                sums = [jnp.sum(v[0:FM] * tile, axis=0, keepdims=True) for v in (d2, d1, dc)]
                sums.append(jnp.sum(dc[0:FM], axis=0, keepdims=True))
                upd = jnp.zeros((8, SW), F32)
                for r, v in enumerate(sums):
                    upd = jnp.where(row == r, jnp.broadcast_to(v, (8, SW)), upd)
                s_ref[s, d] += upd
        x = x_ref[...]
        dx, dg = _rms_bwd(dh, x, _rs(x), g_ref[...])
        dx_ref[...] = dy + dx
        dfn_ref[...] += _colsum8(dg)

    last_halo = t // HALO - 1
    after = lambda i: jnp.minimum((i + 1) * (FM // HALO), last_halo)
    return pl.pallas_call(
        body, name="ffn_bwd", grid=(nt,),
        in_specs=[_rows(FM, D), pl.BlockSpec((HALO, D), lambda i: (after(i), 0)),
                  pl.BlockSpec((NDEV, FM, SW), lambda i: (0, i, 0)),
                  pl.BlockSpec((NDEV, FM, SW), lambda i: (0, i, 0)),
                  pl.BlockSpec((NDEV, HALO, SW), lambda i: (0, after(i), 0)),
                  _rows(FM, D), _resident((NDEV, SW, D)), _resident((2, NG, 3, SW)), _resident((NG, SW, D)), _const((1, D))],
        out_specs=[_rows(FM, D), pl.BlockSpec((NDEV, FM, SW), lambda i: (0, i, 0)), _const((2, NG, 8, SW)), _const((8, D))],
        out_shape=[_sds((t, D)), _sds((NDEV, t, SW), BF16), _sds((2, NG, 8, SW)), _sds((8, D))],
        compiler_params=_params())(dy, dy, a, gs, gs, x2, up, conv, down, ffn_norm)


BT = 512


def _xattn_bwd(dx2, x1, qpre, k2, v2, wq, wo, qn4, xan):
    t = x1.shape[0]

    def body(dx2_ref, x1_ref, qpre_ref, k_ref, v_ref, wq_ref, wo_ref, qn_ref, xan_ref,
             dx1_ref, dqpre_ref, dk_ref, dv_ref, dqn_ref, dxan_ref):
        @pl.when(pl.program_id(0) == 0)
        def _():
            for r in (dk_ref, dv_ref, dqn_ref, dxan_ref):
                r[...] = jnp.zeros_like(r)

        dx2 = dx2_ref[...]
        do = _dot_nt(dx2.astype(BF16), wo_ref[...])
        dqs = []
        for h in range(XH):
            sl = slice(XD * h, XD * (h + 1))
            qpre_h = qpre_ref[:, sl]
            pn, q2, rq = _xattn_probs(qpre_h, qn_ref[:, sl], k_ref[:, sl])
            do_h = do[:, sl].astype(BF16)
            dp = _dot_nt(do_h, v_ref[:, sl])
            ds = (pn * (dp - jnp.sum(pn * dp, axis=-1, keepdims=True)) * (1.0 / math.sqrt(XD))).astype(BF16)
            dq2 = _dot(ds, k_ref[:, sl])
            dk_ref[:, sl] += _dot_tn(ds, q2)
            dv_ref[:, sl] += _dot_tn(pn.astype(BF16), do_h)
            dqh, dg = _rms_bwd(dq2, qpre_h, rq, qn_ref[:, sl])
            dqn_ref[...] += _colsum8(dg)
            dqs.append(dqh)
        dqpre = jnp.concatenate(dqs, axis=1).astype(BF16)
        dqpre_ref[...] = dqpre
        dh2 = _dot_nt(dqpre, wq_ref[...])
        x1 = x1_ref[...]
        dx, dg = _rms_bwd(dh2, x1, _rs(x1), xan_ref[...])
        dx1_ref[...] = dx2 + dx
        dxan_ref[...] += _colsum8(dg)

    return pl.pallas_call(
        body, name="xattn_bwd", grid=(t // BT,),
        in_specs=[_rows(BT, D), _rows(BT, D), _rows(BT, D), _const((MEM, D)), _const((MEM, D)), _const((D, D)),
                  _const((D, D)), _const((1, D)), _const((1, D))],
        out_specs=[_rows(BT, D), _rows(BT, D), _const((MEM, D)), _const((MEM, D)), _const((8, XD)), _const((8, D))],
        out_shape=[_sds((t, D)), _sds((t, D), BF16), _sds((MEM, D)), _sds((MEM, D)), _sds((8, XD)), _sds((8, D))],
        compiler_params=_params())(dx2, x1, qpre, k2, v2, wq, wo, qn4, xan)


def _mem_kv_bwd(mem, mh, kpre, dk2, dv2, wkv, kn4, mem_norm):
    def body(m_ref, mh_ref, kpre_ref, dk_ref, dv_ref, w_ref, kn_ref, g_ref, dw_ref, dkn_ref, dmn_ref):
        dkn = jnp.zeros((8, XD), F32)
        dm = jnp.zeros((MEM, D), F32)
        mh = mh_ref[...]
        for h in range(XH):
            sl = slice(XD * h, XD * (h + 1))
            kh = kpre_ref[:, sl]
            dkh, dg = _rms_bwd(dk_ref[:, sl], kh, _rs(kh), kn_ref[:, sl])
            dkn = dkn + _colsum8(dg)
            dkh = dkh.astype(BF16)
            dvh = dv_ref[:, sl].astype(BF16)
            dw_ref[h] = _dot_tn(mh, dkh).astype(BF16)
            dw_ref[XH + h] = _dot_tn(mh, dvh).astype(BF16)
            dm = dm + _dot_nt(dkh, w_ref[h]) + _dot_nt(dvh, w_ref[XH + h])
        dkn_ref[...] = dkn
        m = m_ref[...]
        _, dg = _rms_bwd(dm, m, _rs(m), g_ref[...])
        dmn_ref[...] = _colsum8(dg)

    return pl.pallas_call(
        body, name="mem_kv_bwd", out_shape=[_sds((2 * XH, D, XD), BF16), _sds((8, XD)), _sds((8, D))],
        compiler_params=pltpu.CompilerParams(vmem_limit_bytes=VMEM_LIMIT))(mem, mh, kpre, dk2, dv2, wkv, kn4, mem_norm)


def _mixer_out_bwd(dx1, attn, gm, w_out, aon, gon):
    t = dx1.shape[0]

    def body(dx_ref, a_ref, g_ref, w_ref, aon_ref, gon_ref, da_ref, dg_ref, dan_ref, dgn_ref):
        @pl.when(pl.program_id(0) == 0)
        def _():
            dan_ref[...] = jnp.zeros_like(dan_ref)
            dgn_ref[...] = jnp.zeros_like(dgn_ref)

        dy = _dot_nt(dx_ref[...].astype(BF16), w_ref[...])
        a, g = a_ref[...], g_ref[...]
        da, dna = _rms_bwd(dy[:, :AW], a, _rs(a), aon_ref[...])
        dg, dng = _rms_bwd(dy[:, AW:], g, _rs(g), gon_ref[...])
        da_ref[...] = da
        dg_ref[...] = dg
        dan_ref[...] += _colsum8(dna)
        dgn_ref[...] += _colsum8(dng)

    return pl.pallas_call(
        body, name="mixer_out_bwd", grid=(t // TM,),
        in_specs=[_rows(TM, D), _rows(TM, AW), _rows(TM, GW), _const((D, D)), _const((1, AW)), _const((1, GW))],
        out_specs=[_rows(TM, AW), _rows(TM, GW), _const((8, AW)), _const((8, GW))],
        out_shape=[_sds((t, AW)), _sds((t, GW)), _sds((8, AW)), _sds((8, GW))],
        compiler_params=_params())(dx1, attn, gm, w_out, aon, gon)


def _gmlp_bwd(dgm, gu, gvn, gz, ws, wst, bfull, gvw):
    t = dgm.shape[0]
    nc = TM // BLK

    def body(dgm_ref, gu_ref, x_ref, gz_ref, w_ref, wt_ref, b_ref, gvw_ref, dgz_ref, dw_ref, db_ref, dgvw_ref,
             dbacc_ref):
        @pl.when(pl.program_id(0) == 0)
        def _():
            for r in (dw_ref, dbacc_ref, dgvw_ref):
                r[...] = jnp.zeros_like(r)

        xin = x_ref[...]
        dgm = dgm_ref[...]
        mixed = _gmlp_mix(w_ref, xin, False) + jnp.tile(b_ref[...], (nc, 1))
        dgu = dgm * mixed
        dmixed = dgm * gu_ref[...]
        lo = _lane((BLK, 128)) < 64
        dbias = jnp.zeros((BLK, GW), F32)
        for c in range(nc):
            dmc = dmixed[c * BLK:(c + 1) * BLK]
            dbias = dbias + dmc
            for j in range(4):
                dm2 = dmc[:, 128 * j:128 * (j + 1)]
                xs = xin[c * BLK:(c + 1) * BLK, 128 * j:128 * (j + 1)]
                z = jnp.zeros_like(dm2)
                dw_ref[2 * j] += _dot_nt(jnp.where(lo, dm2, z).astype(BF16), xs)
                dw_ref[2 * j + 1] += _dot_nt(jnp.where(lo, z, dm2).astype(BF16), xs)
        dbacc_ref[...] += dbias
        dgvn = _gmlp_mix(wt_ref, dmixed.astype(BF16), True)
        gz_u, gz_v = gz_ref[:, :GW], gz_ref[:, GW:]
        gv = _gelu(gz_v)
        dgv, dg = _rms_bwd(dgvn, gv, _rs(gv), gvw_ref[...])
        dgvw_ref[...] += _colsum8(dg)
        dgz_ref[:, :GW] = (dgu * _gelu_grad(gz_u)).astype(BF16)
        dgz_ref[:, GW:] = (dgv * _gelu_grad(gz_v)).astype(BF16)

        @pl.when(pl.program_id(0) == pl.num_programs(0) - 1)
        def _():
            s = dbacc_ref[...]
            sel = (lax.broadcasted_iota(jnp.int32, (8, GW), 1) // HD
                   == lax.broadcasted_iota(jnp.int32, (8, GW), 0)).astype(BF16)
            hi = s.astype(BF16)
            r1 = s - hi.astype(F32)
            mid = r1.astype(BF16)
            lo = (r1 - mid.astype(F32)).astype(BF16)
            db_ref[...] = _dot_nt(sel, hi) + _dot_nt(sel, mid) + _dot_nt(sel, lo)
            r = lax.broadcasted_iota(jnp.int32, (BLK, BLK), 0)
            c = lax.broadcasted_iota(jnp.int32, (BLK, BLK), 1)
            for h in range(8):
                dw_ref[h] = jnp.where(c <= r, dw_ref[h], 0.0)

    return pl.pallas_call(
        body, name="gmlp_bwd", grid=(t // TM,),
        in_specs=[_rows(TM, GW), _rows(TM, GW), _rows(TM, GW), _rows(TM, 2 * GW), _const((8, BLK, BLK)),
                  _const((8, BLK, BLK)), _const((BLK, GW)), _const((1, GW))],
        out_specs=[_rows(TM, 2 * GW), _const((8, BLK, BLK)), _const((8, BLK)), _const((8, GW))],
        out_shape=[_sds((t, 2 * GW), BF16), _sds((8, BLK, BLK)), _sds((8, BLK)), _sds((8, GW))],
        scratch_shapes=[pltpu.VMEM((BLK, GW), F32)],
        compiler_params=_params())(dgm, gu, gvn, gz, ws, wst, bfull, gvw)


def _fold_half(v):
    return v + pltpu.roll(v, 64, 1)


def _swa_bwd(q, k, v, dattn, sinks):
    t = q.shape[0]
    nb = t // BLK
    ts = min(t, SB_BWD * BLK)
    sb = ts // BLK
    nt = t // ts

    def body(sink_ref, q_ref, kc_ref, kp_ref, vc_ref, vp_ref, do_ref, dq_ref, dk_ref, dv_ref, ds_ref,
             ck_ref, cv_ref, sacc_ref):
        i = pl.program_id(0)

        @pl.when(i == 0)
        def _():
            ck_ref[...] = jnp.zeros_like(ck_ref)
            cv_ref[...] = jnp.zeros_like(cv_ref)
            sacc_ref[...] = jnp.zeros_like(sacc_ref)

        @pl.when(i < nt)
        def _():
            kk = jnp.concatenate([kp_ref[...], kc_ref[...]], axis=0).astype(F32)
            vv = jnp.concatenate([vp_ref[...], vc_ref[...]], axis=0).astype(F32)
            lo256 = _lane((2 * BLK, 128)) < 64
            acc_k = [jnp.zeros((BLK, 128), F32) for _ in range(sb + 1)]
            acc_v = [jnp.zeros((BLK, 128), F32) for _ in range(sb + 1)]
            for b in range(sb):
                r = slice(b * BLK, (b + 1) * BLK)
                kb, vb = kk[b * BLK:(b + 2) * BLK], vv[b * BLK:(b + 2) * BLK]
                mask = _swa_mask(i == 0) if b == 0 else _swa_mask(False)
                dkk = jnp.zeros((2 * BLK, 128), F32)
                dvv = jnp.zeros((2 * BLK, 128), F32)
                for g in range(2):
                    qs = _stack_heads(q_ref[r, 256 * g:256 * g + 128], q_ref[r, 256 * g + 128:256 * g + 256])
                    dos = _stack_heads(do_ref[r, 256 * g:256 * g + 128],
                                       do_ref[r, 256 * g + 128:256 * g + 256]).astype(BF16)
                    kd = _dup_half(kb, g)
                    pn, psn = _swa_probs(qs, kd, mask, _sink_col(sink_ref, g))
                    dp = _dot_nt(dos, _dup_half(vb, g))
                    dd = jnp.sum(pn * dp, axis=-1, keepdims=True)
                    ds = (pn * (dp - dd) * (1.0 / math.sqrt(HD))).astype(BF16)
                    sacc_ref[g] += jnp.broadcast_to(-psn * dd, (4 * BLK, 128))
                    dqa, dqb = _unstack_heads(_dot(ds, kd))
                    dq_ref[r, 256 * g:256 * g + 128] = dqa
                    dq_ref[r, 256 * g + 128:256 * g + 256] = dqb
                    dkg = _fold_half(_dot_tn(ds, qs))
                    dvg = _fold_half(_dot_tn(pn.astype(BF16), dos))
                    keep = lo256 if g == 0 else jnp.logical_not(lo256)
                    dkk = jnp.where(keep, dkg, dkk)
                    dvv = jnp.where(keep, dvg, dvv)
                acc_k[b], acc_k[b + 1] = acc_k[b] + dkk[0:BLK], acc_k[b + 1] + dkk[BLK:]
                acc_v[b], acc_v[b + 1] = acc_v[b] + dvv[0:BLK], acc_v[b + 1] + dvv[BLK:]
            for out_ref, c_ref, acc in ((dk_ref, ck_ref, acc_k), (dv_ref, cv_ref, acc_v)):
                if sb > 1:
                    out_ref[0:ts - BLK] = c_ref[0:ts - BLK]
                out_ref[ts - BLK:ts] = c_ref[ts - BLK:ts] + acc[0]
                for b in range(sb):
                    c_ref[b * BLK:(b + 1) * BLK] = acc[b + 1]

        @pl.when(i == nt)
        def _():
            dk_ref[...] = ck_ref[...]
            dv_ref[...] = cv_ref[...]
            lane = _lane((8, 128))
            acc = jnp.zeros((8, 128), F32)
            for g in range(2):
                for j in range(4):
                    val = jnp.sum(sacc_ref[g, j * BLK:(j + 1) * BLK, :], axis=0, keepdims=True)
                    acc = jnp.where(lane == 4 * g + j, jnp.broadcast_to(val, (8, 128)), acc)
            ds_ref[...] = acc

    cur = lambda i: (jnp.minimum(i, nt - 1), 0)
    before = lambda i: (jnp.clip(i * sb - 1, 0, nb - 1), 0)
    done = lambda i: (jnp.clip(i - 1, 0, nt - 1), 0)
    return pl.pallas_call(
        body, name="swa_bwd", grid=(nt + 1,),
        in_specs=[pl.BlockSpec(memory_space=pltpu.SMEM), pl.BlockSpec((ts, AW), cur),
                  pl.BlockSpec((ts, KW), cur), pl.BlockSpec((BLK, KW), before),
                  pl.BlockSpec((ts, KW), cur), pl.BlockSpec((BLK, KW), before), pl.BlockSpec((ts, AW), cur)],
        out_specs=[pl.BlockSpec((ts, AW), cur), pl.BlockSpec((ts, KW), done), pl.BlockSpec((ts, KW), done),
                   _const((8, 128))],
        out_shape=[_sds((t, AW)), _sds((t, KW)), _sds((t, KW)), _sds((8, 128))],
        scratch_shapes=[pltpu.VMEM((ts, KW), F32), pltpu.VMEM((ts, KW), F32), pltpu.VMEM((2, 4 * BLK, 128), F32)],
        compiler_params=_params())(sinks, q, k, k, v, v, dattn)


def _mixer_in_bwd(dq, dk, dv, dgz, qk, cos, sin, x, dx1, w_in, mix_norm, qn, kn):
    t = x.shape[0]

    def body(dq_ref, dk_ref, dv_ref, dgz_ref, qk_ref, cos_ref, sin_ref, x_ref, dx1_ref, w_ref, g_ref, qn_ref, kn_ref,
             ones_ref, gx_ref, dproj_ref, dmn_ref, dqn_ref, dkn_ref, qacc_ref, kacc_ref):
        i = pl.program_id(0)

        @pl.when(i == 0)
        def _():
            dmn_ref[...] = jnp.zeros_like(dmn_ref)
            qacc_ref[...] = jnp.zeros_like(qacc_ref)
            kacc_ref[...] = jnp.zeros_like(kacc_ref)

        cos2, sin2 = cos_ref[...], sin_ref[...]
        qpre, kpre = qk_ref[:, :AW], qk_ref[:, AW:]
        dqh = _rope_bwd(dq_ref[...], jnp.tile(cos2, (1, 4)), jnp.tile(sin2, (1, 4)))
        dqpre, dgq = _rms64_bwd(dqh, qpre, _rs64(qpre, ones_ref), qn_ref[...], ones_ref)
        dkh = _rope_bwd(dk_ref[...], cos2, sin2)
        dkpre, dgk = _rms64_bwd(dkh, kpre, _rs64(kpre, ones_ref), kn_ref[...], ones_ref)
        qacc_ref[...] += jnp.sum(dgq, axis=0, keepdims=True)
        kacc_ref[...] += jnp.sum(dgk, axis=0, keepdims=True)
        dproj = jnp.concatenate([dqpre.astype(BF16), dkpre.astype(BF16), dv_ref[...].astype(BF16), dgz_ref[...]], axis=1)
        dproj_ref[...] = dproj
        dh = _dot(dproj, w_ref[...])
        xv = x_ref[...]
        dx, dg = _rms_bwd(dh, xv, _rs(xv), g_ref[...])
        gx_ref[...] = dx1_ref[...] + dx
        dmn_ref[...] += _colsum8(dg)

        @pl.when(i == pl.num_programs(0) - 1)
        def _():
            qa = qacc_ref[...]
            q4 = qa[:, 0:128] + qa[:, 128:256] + qa[:, 256:384] + qa[:, 384:512]
            dqn_ref[...] = jnp.broadcast_to(_fold_half(q4), (8, 128))
            dkn_ref[...] = jnp.broadcast_to(_fold_half(kacc_ref[...]), (8, 128))

    return pl.pallas_call(
        body, name="mixer_in_bwd", grid=(t // TM,),
        in_specs=[_rows(TM, AW), _rows(TM, KW), _rows(TM, KW), _rows(TM, 2 * GW), _rows(TM, AW + KW), _rows(TM, 128),
                  _rows(TM, 128), _rows(TM, D), _rows(TM, D), _const((IN, D)), _const((1, D)), _const((1, AW)),
                  _const((1, KW)), _const((AW, AW))],
        out_specs=[_rows(TM, D), _rows(TM, IN), _const((8, D)), _const((8, 128)), _const((8, 128))],
        out_shape=[_sds((t, D)), _sds((t, IN), BF16), _sds((8, D)), _sds((8, 128)), _sds((8, 128))],
        scratch_shapes=[pltpu.VMEM((1, AW), F32), pltpu.VMEM((1, KW), F32)],
        compiler_params=_params())(dq, dk, dv, dgz, qk, cos, sin, x, dx1, w_in, mix_norm, qn, kn, _head_ones())


def _local_step(x, mem, pos, target, p, fetch, ship):
    t = x.shape[0]
    p = dict(p)
    p.update(fetch(0, None))
    inv_freq = 1.0 / (ROPE_THETA ** (jnp.arange(HD // 2, dtype=F32) * (2.0 / HD)))
    cos, sin = _rope_tables(pos, jnp.tile(inv_freq, 4).reshape(1, 128))
    qn = jnp.tile(p["q_norm"], (1, AW // HD))
    kn = jnp.tile(p["k_norm"], (1, KW // HD))
    qn4 = jnp.tile(p["xa_q_norm"], (1, XH))
    kn4 = jnp.tile(p["xa_k_norm"], (1, XH))
    ws = p["gmlp_ws"]
    wst = jnp.swapaxes(ws, 1, 2)
    bfull = jnp.repeat(p["gmlp_bs"].T, HD, axis=1)
    conv_b = p["ffn_conv_b"]

    h1, qk, gz, q, k, v, gu, gvn = _mixer_in_fwd(x, p["mix_norm"], p["w_in"], qn, kn, p["gmlp_v_norm"], cos, sin)
    attn = _swa_fwd(q, k, v, p["attn_sinks"])
    p.update(fetch(1, attn))
    gm, ycat, x1, h2 = _mixer_out_fwd(attn, gvn, gu, ws, bfull, x, p["w_out"], p["attn_out_norm"], p["gmlp_out_norm"],
                                      p["xa_norm"])
    mh, kpre, k2, v2 = _mem_kv_fwd(mem, p["mem_norm"], p["xa_wkv"], kn4)
    qpre, o, x2, h3 = _xattn_fwd(h2, x1, p["xa_wq"], qn4, k2, v2, p["xa_wo"], p["ffn_norm"])
    p.update(fetch(2, h3))
    conv = p["ffn_conv"]
    a, u, gs, dy, loss8 = _ffn_fwd(h3, x2, target, p["ffn_up"], conv, conv_b, p["ffn_down"])

    raw = {}
    d_down = _mm_tn(u, dy, "ffn_down_bwd_w")
    dx2, da, raw["conv_sums"], raw["ffn_norm"] = _ffn_bwd(dy, a, gs, x2, p["ffn_up"], conv, p["ffn_down"], p["ffn_norm"])
    d_up = _mm_tn(da, h3, "ffn_up_bwd_w")
    token = ship(0, {"ffn_down": d_down, "ffn_up": d_up, "ffn_conv": raw["conv_sums"][:, :, 0:3]})
    dx1, dqpre, dk2, dv2, raw["xa_q_norm"], raw["xa_norm"] = _xattn_bwd(
        dx2, x1, qpre, k2, v2, p["xa_wq"], p["xa_wo"], qn4 + jnp.tile(token[0:1], (1, D // 128)), p["xa_norm"])
    d_wo = _mm_tn(o, dx2, "xa_wo_bwd_w")
    d_wq = _mm_tn(h2, dqpre, "xa_wq_bwd_w")
    d_wkv, raw["xa_k_norm"], raw["mem_norm"] = _mem_kv_bwd(mem, mh, kpre, dk2, dv2, p["xa_wkv"], kn4, p["mem_norm"])
    d_w_out = _mm_tn(ycat, dx1, "w_out_bwd_w")
    token = ship(1, {"xa_wo": d_wo, "xa_wq": d_wq, "xa_wkv": d_wkv, "w_out": d_w_out})
    dattn, dgm, raw["attn_out_norm"], raw["gmlp_out_norm"] = _mixer_out_bwd(
        dx1, attn, gm, p["w_out"], p["attn_out_norm"] + jnp.tile(token[0:1], (1, AW // 128)), p["gmlp_out_norm"])
    dgz, raw["gmlp_ws"], raw["gmlp_bs"], raw["gmlp_v_norm"] = _gmlp_bwd(dgm, gu, gvn, gz, ws, wst, bfull, p["gmlp_v_norm"])
    token = ship(2, {}, [raw["gmlp_ws"]])
    dq, dk, dv, raw["attn_sinks"] = _swa_bwd(q, k, v, dattn, p["attn_sinks"] + token[0:1, 0:8])
    grad_x, dproj, raw["mix_norm"], raw["q_norm"], raw["k_norm"] = _mixer_in_bwd(
        dq, dk, dv, dgz, qk, cos, sin, x, dx1, p["w_in"], p["mix_norm"], qn, kn)
    d_w_in = _mm_tn(dproj, h1, "w_in_bwd_w")
    return loss8[0, 0], grad_x, {"w_in": d_w_in}, raw


def _cast_shards(shards):
    def body(*refs):
        n = len(refs) // 2
        for i_ref, o_ref in zip(refs[:n], refs[n:]):
            o_ref[...] = i_ref[...].astype(BF16)

    return pl.pallas_call(body, name="cast_shards", out_shape=[_sds(s.shape, BF16) for s in shards],
                          compiler_params=pltpu.CompilerParams(vmem_limit_bytes=VMEM_LIMIT))(*shards)


HBM_SPEC = pl.BlockSpec(memory_space=pltpu.HBM)
SEM_SPEC = pl.BlockSpec(memory_space=pltpu.SEMAPHORE)


ALL_K = tuple(range(1, NDEV))
CHIP_K = (1, 2, 4, 6)
RELAY_K = (2, 4, 6)


def _peer(k):
    x, y, cc = lax.axis_index("x"), lax.axis_index("y"), lax.axis_index("c")
    return 1 - x if k & 4 else x, 1 - y if k & 2 else y, 1 - cc if k & 1 else cc


def _remote_copies(src_refs, land_refs, send_refs, recv_refs, nd, ks):
    me = 4 * lax.axis_index("x") + 2 * lax.axis_index("y") + lax.axis_index("c")
    copies = []
    for a, (src_ref, land_ref) in enumerate(zip(src_refs, land_refs)):
        for j, k in enumerate(ks):
            px, py, pc = _peer(k)
            copies.append((k, pltpu.make_async_remote_copy(
                src_ref=src_ref.at[4 * px + 2 * py + pc] if a < nd else src_ref, dst_ref=land_ref.at[me],
                send_sem=send_refs[a].at[j], recv_sem=recv_refs[a].at[j],
                device_id=(px, py, pc), device_id_type=pl.DeviceIdType.MESH)))
    return copies


def _relay_copies(land_refs, send_refs, recv_refs):
    copies = []
    for a, land_ref in enumerate(land_refs):
        for j, k in enumerate(RELAY_K):
            px, py, pc = _peer(k)
            slot = land_ref.at[4 * px + 2 * py + pc]
            copies.append(pltpu.make_async_remote_copy(
                src_ref=slot, dst_ref=slot, send_sem=send_refs[a].at[j], recv_sem=recv_refs[a].at[j],
                device_id=_peer(1), device_id_type=pl.DeviceIdType.MESH))
    return copies


def _own_slot(src, by_dest, me):
    block = lax.dynamic_index_in_dim(src, me, 0, keepdims=True) if by_dest else src[None]
    return lax.dynamic_update_index_in_dim(lax.empty((NDEV,) + block.shape[1:], src.dtype), block, me, 0)


SIDE_EFFECT = pltpu.CompilerParams(has_side_effects=pltpu.SideEffectType.DATAFLOW_SIDE_EFFECTING)


def _exchange_start(by_dest, for_all, me, name, ks=ALL_K):
    srcs = list(by_dest) + list(for_all)
    n, nd = len(srcs), len(by_dest)
    lands = [_own_slot(s, a < nd, me) for a, s in enumerate(srcs)]

    def body(*refs):
        for _, cp in _remote_copies(refs[:n], refs[n:2 * n], refs[2 * n:3 * n], refs[3 * n:4 * n], nd, ks):
            cp.start()
        refs[-1][...] = jnp.zeros((8, 128), F32)

    sems = [pltpu.SemaphoreType.DMA((len(ks),))] * (2 * n)
    thru = [pltpu.HBM(v.shape, v.dtype) for v in srcs + lands]
    res = pl.pallas_call(
        body, name=name, out_shape=sems + thru + [_sds((8, 128))],
        in_specs=[HBM_SPEC] * (2 * n), out_specs=[SEM_SPEC] * (2 * n) + [HBM_SPEC] * (2 * n) + [pl.BlockSpec(memory_space=pltpu.VMEM)],
        input_output_aliases={i: 2 * n + i for i in range(2 * n)}, compiler_params=SIDE_EFFECT)(
            *[pltpu.with_memory_space_constraint(v, pltpu.HBM) for v in srcs + lands])
    return (res[:2 * n], res[2 * n:4 * n], nd, ks, None), res[-1]


def _exchange_relay(state, after, name):
    sems, thru, nd, ks, _ = state
    n = len(thru) // 2

    def body(*refs):
        for k, cp in _remote_copies(refs[:n], refs[n:2 * n], refs[2 * n:3 * n], refs[3 * n:4 * n], nd, ks):
            if k in RELAY_K:
                cp.wait_recv()
        for cp in _relay_copies(refs[n:2 * n], refs[4 * n + 1:5 * n + 1], refs[5 * n + 1:6 * n + 1]):
            cp.start()

    relay_sems = [pltpu.SemaphoreType.DMA((len(RELAY_K),))] * (2 * n)
    res = pl.pallas_call(
        body, name=name, out_shape=relay_sems + [pltpu.HBM(v.shape, v.dtype) for v in thru],
        in_specs=[HBM_SPEC] * (2 * n) + [SEM_SPEC] * (2 * n) + [pl.BlockSpec(memory_space=pl.ANY)],
        out_specs=[SEM_SPEC] * (2 * n) + [HBM_SPEC] * (2 * n),
        input_output_aliases={i: 2 * n + i for i in range(2 * n)}, compiler_params=SIDE_EFFECT)(*thru, *sems, after)
    return sems, res[2 * n:], nd, ks, res[:2 * n]


def _exchange_wait(state, after, name):
    sems, thru, nd, ks, relay_sems = state
    n = len(thru) // 2

    def body(*refs):
        for k, cp in _remote_copies(refs[:n], refs[n:2 * n], refs[2 * n:3 * n], refs[3 * n:4 * n], nd, ks):
            cp.wait_send()
            if relay_sems is None or k not in RELAY_K:
                cp.wait_recv()
        if relay_sems is not None:
            for cp in _relay_copies(refs[n:2 * n], refs[4 * n:5 * n], refs[5 * n:6 * n]):
                cp.wait_send()
                cp.wait_recv()

    extra = [] if relay_sems is None else list(relay_sems)
    res = pl.pallas_call(
        body, name=name, out_shape=[pltpu.HBM(v.shape, v.dtype) for v in thru],
        in_specs=[HBM_SPEC] * (2 * n) + [SEM_SPEC] * (2 * n + len(extra)) + [pl.BlockSpec(memory_space=pl.ANY)],
        out_specs=[HBM_SPEC] * (2 * n), input_output_aliases={i: i for i in range(2 * n)}, compiler_params=SIDE_EFFECT)(
            *thru, *sems, *extra, after)
    return res[n:]


def _adam(parts, w, m, v, name):
    def body(p_ref, w_ref, m_ref, v_ref, g_ref, d_ref, nm_ref, nv_ref):
        g = _sum_parts(p_ref)
        g_ref[...] = g
        d_ref[...], nm_ref[...], nv_ref[...] = _adam_math(g, w_ref[...], m_ref[...], v_ref[...])

    return pl.pallas_call(
        body, name=name, out_shape=[_sds(w.shape)] * 4,
        compiler_params=pltpu.CompilerParams(vmem_limit_bytes=VMEM_LIMIT))(parts, w, m, v)


GATHER_GROUPS = (("w_in",), ("w_out", "xa_wkv", "xa_wq", "xa_wo"), ("ffn_up", "ffn_conv", "ffn_down"))
SCATTER_GROUPS = (("ffn_down", "ffn_up", "ffn_conv"), ("xa_wo", "xa_wq", "xa_wkv", "w_out"), (), ("w_in",))
BIG = tuple(n for grp in GATHER_GROUPS for n in grp)
BY_COLUMN = ("w_in", "ffn_up")
VECS = (("mix_norm", D), ("q_norm", HD), ("k_norm", HD), ("attn_sinks", 8), ("gmlp_v_norm", GW), ("attn_out_norm", AW),
        ("gmlp_out_norm", GW), ("xa_norm", D), ("mem_norm", D), ("xa_q_norm", XD), ("xa_k_norm", XD), ("ffn_norm", D))
BS_ROW = 16
VEC_ROWS = 24
SMALL = tuple(n for n, _ in VECS) + ("gmlp_bs", "gmlp_ws", "ffn_conv_b")


def _pack_small(raw):
    names = [n for n, _ in VECS] + ["gmlp_bs", "conv_sums"]

    def body(*refs):
        ins = dict(zip(names, refs))
        vec_ref, cb_ref = refs[len(names):]
        vec_ref[...] = jnp.zeros_like(vec_ref)
        for r, (n, w) in enumerate(VECS):
            vec_ref[r:r + 1, 0:w] = ins[n][0:1, 0:w]
        vec_ref[BS_ROW:BS_ROW + 8, 0:BLK] = ins["gmlp_bs"][...]
        for s in range(2):
            for d in range(NG):
                cb_ref[s, d] = ins["conv_sums"][s, d, 3:4, :]

    return pl.pallas_call(body, name="pack_small", out_shape=[_sds((VEC_ROWS, D)), _sds((2, NG, 1, SW))])(
        *[raw[n] for n in names])


def _adam_math(g, w, m, v):
    nm = B1 * m + (1.0 - B1) * g
    nv = B2 * v + (1.0 - B2) * (g * g)
    m_hat = nm / (1.0 - B1 ** STEP)
    v_hat = nv / (1.0 - B2 ** STEP)
    return -LR * (m_hat / (jnp.sqrt(v_hat) + AEPS) + WD * w), nm, nv


def _sum_parts(p_ref):
    g = p_ref[0].astype(F32)
    for j in range(1, NDEV):
        g = g + p_ref[j].astype(F32)
    return g


def _adam_small(parts_vec, parts_ws, parts_cb, w, m, v):
    def body(*refs):
        pv_ref, pws_ref, pcb_ref = refs[:3]
        ins = refs[3:3 + 3 * len(SMALL)]
        outs = refs[3 + 3 * len(SMALL):]
        gv = _sum_parts(pv_ref)
        for j, n in enumerate(SMALL):
            w_ref, m_ref, v_ref = ins[3 * j:3 * j + 3]
            o = outs[4 * j:4 * j + 4]
            if n == "gmlp_ws":
                g = _sum_parts(pws_ref)
            elif n == "ffn_conv_b":
                g = _sum_parts(pcb_ref)
            elif n == "gmlp_bs":
                g = gv[BS_ROW:BS_ROW + 8, 0:BLK]
            else:
                g = gv[j:j + 1, 0:VECS[j][1]]
            lead = n in ("gmlp_ws", "gmlp_bs")
            res = (g,) + _adam_math(g, w_ref[0] if lead else w_ref[...], m_ref[0] if lead else m_ref[...],
                                    v_ref[0] if lead else v_ref[...])
            for o_ref, val in zip(o, res):
                if lead:
                    o_ref[0] = val
                else:
                    o_ref[...] = val

    args = [parts_vec, parts_ws, parts_cb] + [d[n] for n in SMALL for d in (w, m, v)]
    res = pl.pallas_call(body, name="adam_small", out_shape=[_sds(w[n].shape) for n in SMALL for _ in range(4)],
                         compiler_params=pltpu.CompilerParams(vmem_limit_bytes=VMEM_LIMIT))(*args)
    return {n: tuple(res[4 * j:4 * j + 4]) for j, n in enumerate(SMALL)}


def kernel(x, mem, positions, mix_norm, w_in, q_norm, k_norm, attn_sinks, gmlp_v_norm, gmlp_ws, gmlp_bs, attn_out_norm, gmlp_out_norm, w_out, xa_norm, mem_norm, xa_wq, xa_wkv, xa_q_norm, xa_k_norm, xa_wo, ffn_norm, ffn_up, ffn_conv, ffn_conv_b, ffn_down, loss_target, m_mix_norm, m_w_in, m_q_norm, m_k_norm, m_attn_sinks, m_gmlp_v_norm, m_gmlp_ws, m_gmlp_bs, m_attn_out_norm, m_gmlp_out_norm, m_w_out, m_xa_norm, m_mem_norm, m_xa_wq, m_xa_wkv, m_xa_q_norm, m_xa_k_norm, m_xa_wo, m_ffn_norm, m_ffn_up, m_ffn_conv, m_ffn_conv_b, m_ffn_down, v_mix_norm, v_w_in, v_q_norm, v_k_norm, v_attn_sinks, v_gmlp_v_norm, v_gmlp_ws, v_gmlp_bs, v_attn_out_norm, v_gmlp_out_norm, v_w_out, v_xa_norm, v_mem_norm, v_xa_wq, v_xa_wkv, v_xa_q_norm, v_xa_k_norm, v_xa_wo, v_ffn_norm, v_ffn_up, v_ffn_conv, v_ffn_conv_b, v_ffn_down):
    names = ("mix_norm", "w_in", "q_norm", "k_norm", "attn_sinks", "gmlp_v_norm", "gmlp_ws", "gmlp_bs", "attn_out_norm",
             "gmlp_out_norm", "w_out", "xa_norm", "mem_norm", "xa_wq", "xa_wkv", "xa_q_norm", "xa_k_norm", "xa_wo",
             "ffn_norm", "ffn_up", "ffn_conv", "ffn_conv_b", "ffn_down")
    w = dict(zip(names, (mix_norm, w_in, q_norm, k_norm, attn_sinks, gmlp_v_norm, gmlp_ws, gmlp_bs, attn_out_norm,
                         gmlp_out_norm, w_out, xa_norm, mem_norm, xa_wq, xa_wkv, xa_q_norm, xa_k_norm, xa_wo, ffn_norm,
                         ffn_up, ffn_conv, ffn_conv_b, ffn_down)))
    m = dict(zip(names, (m_mix_norm, m_w_in, m_q_norm, m_k_norm, m_attn_sinks, m_gmlp_v_norm, m_gmlp_ws, m_gmlp_bs,
                         m_attn_out_norm, m_gmlp_out_norm, m_w_out, m_xa_norm, m_mem_norm, m_xa_wq, m_xa_wkv,
                         m_xa_q_norm, m_xa_k_norm, m_xa_wo, m_ffn_norm, m_ffn_up, m_ffn_conv, m_ffn_conv_b, m_ffn_down)))
    v = dict(zip(names, (v_mix_norm, v_w_in, v_q_norm, v_k_norm, v_attn_sinks, v_gmlp_v_norm, v_gmlp_ws, v_gmlp_bs,
                         v_attn_out_norm, v_gmlp_out_norm, v_w_out, v_xa_norm, v_mem_norm, v_xa_wq, v_xa_wkv,
                         v_xa_q_norm, v_xa_k_norm, v_xa_wo, v_ffn_norm, v_ffn_up, v_ffn_conv, v_ffn_conv_b, v_ffn_down)))
    t = x.shape[1]

    me = 4 * lax.axis_index("x") + 2 * lax.axis_index("y") + lax.axis_index("c")

    def rows(a, n):
        return jnp.swapaxes(a[0], 0, 1) if n in BY_COLUMN else a[0]

    mats = [n for n in BIG if n != "ffn_conv"]
    shard = dict(zip(mats, _cast_shards([rows(w[n], n) for n in mats])), ffn_conv=w["ffn_conv"][0])
    gathers, tokens = zip(*[_exchange_start([], [shard[n] for n in grp], me, "gather_start_%d" % i,
                                            CHIP_K if i == len(GATHER_GROUPS) - 1 else ALL_K)
                            for i, grp in enumerate(GATHER_GROUPS)])

    def fetch(i, after):
        after = tokens[0] + tokens[1] + tokens[2] if after is None else after
        state = gathers[i]
        if state[3] == CHIP_K:
            state = _exchange_relay(state, after, "gather_relay_%d" % i)
        got = dict(zip(GATHER_GROUPS[i], _exchange_wait(state, after, "gather_wait_%d" % i)))
        if "w_in" in got:
            got["w_in"] = got["w_in"].reshape(IN, D)
        for n in ("w_out", "xa_wq", "xa_wo"):
            if n in got:
                got[n] = got[n].reshape(D, D)
        if "ffn_down" in got:
            got["ffn_down"] = got["ffn_down"].reshape(NG, SW, D)
            got["ffn_conv"] = got["ffn_conv"].reshape(2, NG, 3, SW)
        return got

    scatters = []

    def ship(i, grads, for_all=()):
        by_dest = [grads[n].reshape((NDEV,) + shard[n].shape) for n in SCATTER_GROUPS[i]]
        state, token = _exchange_start(by_dest, for_all, me, "scatter_start_%d" % i)
        scatters.append(state)
        return token

    conv_b = {k: d["ffn_conv_b"].reshape(NDEV, 1, SW) for k, d in (("w", w), ("m", m), ("v", v))}
    p = {n: w[n] for n in SMALL[:-1]}
    p["gmlp_ws"], p["gmlp_bs"] = w["gmlp_ws"][0], w["gmlp_bs"][0]
    p["ffn_conv_b"] = conv_b["w"].reshape(2, NG, 1, SW)
    loss, grad_x, g, raw = _local_step(x[0], mem[0], positions.reshape(t, 1), loss_target[0], p, fetch, ship)
    loss = lax.psum(loss, AXES)

    vec, cb = _pack_small(raw)
    after = ship(3, g, [vec, cb.reshape(NDEV, 1, SW)])
    res, rest = {}, []
    for i, grp in enumerate(SCATTER_GROUPS):
        got = _exchange_wait(scatters[i], after, "scatter_wait_%d" % i)
        rest += got[len(grp):]
        for n, parts in zip(grp, got):
            out = _adam(parts, rows(w[n], n), rows(m[n], n), rows(v[n], n), "adam_" + n)
            res[n] = [jnp.swapaxes(o, 0, 1) if n in BY_COLUMN else o for o in out]
            after = out[0]
    ws_parts, vec_parts, cb_parts = rest
    small = lambda d, k: {**{n: d[n] for n in SMALL[:-1]}, "ffn_conv_b": conv_b[k]}
    res.update(_adam_small(vec_parts, ws_parts, cb_parts, small(w, "w"), small(m, "m"), small(v, "v")))

    outs = [loss, grad_x[None]]
    for j in range(4):
        outs += [res[n][j].reshape(w[n].shape) for n in names]
    return tuple(outs)
```

```python
import functools
import math

import jax
import jax.numpy as jnp
from jax import lax
from jax.experimental import pallas as pl
from jax.experimental.pallas import tpu as pltpu

F32 = jnp.float32
BF16 = jnp.bfloat16

D = 1024
HD = 64
AW = 512
KW = 128
GW = 512
IN = AW + 2 * KW + 2 * GW
BLK = 128
MEM = 256
XH = 4
XD = 256
FF = 2816
EPS = 1e-6
ROPE_THETA = 10000.0
NDEV = 8
LR, B1, B2, AEPS, WD, STEP = 0.001, 0.9, 0.999, 1e-08, 0.01, 10

TM = 512
WK = 2048
VMEM_LIMIT = 56 * 1024 * 1024
NEG = float(jnp.finfo(jnp.float32).min)
GELU_C0 = math.sqrt(2.0 / math.pi)
GELU_C1 = 0.044715


def _dot(a, b):
    return jnp.dot(a, b, preferred_element_type=F32)


def _dot_nt(a, b):
    return lax.dot_general(a, b, (((1,), (1,)), ((), ())), preferred_element_type=F32)


def _dot_tn(a, b):
    return lax.dot_general(a, b, (((0,), (0,)), ((), ())), preferred_element_type=F32)


def _rs(x):
    return lax.rsqrt(jnp.mean(x * x, axis=-1, keepdims=True) + EPS)


def _rms_bwd(dy, x, r, g):
    xh = x * r
    dxh = dy * g
    dx = r * (dxh - xh * jnp.mean(dxh * xh, axis=-1, keepdims=True))
    return dx, dy * xh


def _lane(shape):
    return lax.broadcasted_iota(jnp.int32, shape, len(shape) - 1)


def _gsum64(v, ones_ref):
    w = v.shape[-1]
    ones = ones_ref[0:w, 0:w]
    hi = v.astype(BF16)
    lo = (v - hi.astype(F32)).astype(BF16)
    return _dot(hi, ones) + _dot(lo, ones)


def _head_ones():
    i = jnp.arange(AW) // HD
    return (i[:, None] == i[None, :]).astype(BF16)


def _rs64(x, ones_ref):
    return lax.rsqrt(_gsum64(x * x, ones_ref) * (1.0 / HD) + EPS)


def _rms64_bwd(dy, x, r, g, ones_ref):
    xh = x * r
    dxh = dy * g
    dx = r * (dxh - xh * (_gsum64(dxh * xh, ones_ref) * (1.0 / HD)))
    return dx, dy * xh


def _rot_half(v):
    w = v.shape[-1]
    return jnp.where((_lane(v.shape) & 32) == 0, pltpu.roll(v, w - 32, 1), pltpu.roll(v, 32, 1))


def _rope(v, cos, sin_signed):
    return v * cos + _rot_half(v) * sin_signed


def _rope_bwd(dv, cos, sin_signed):
    return dv * cos + _rot_half(dv * sin_signed)


def _gelu(z):
    return 0.5 * z * (1.0 + jnp.tanh(GELU_C0 * (z + GELU_C1 * z * z * z)))


def _gelu_grad(z):
    t = jnp.tanh(GELU_C0 * (z + GELU_C1 * z * z * z))
    return 0.5 * (1.0 + t) + 0.5 * z * (1.0 - t * t) * (GELU_C0 * (1.0 + 3.0 * GELU_C1 * z * z))


def _colsum8(v):
    s = jnp.sum(v, axis=0, keepdims=True)
    row = lax.broadcasted_iota(jnp.int32, (8, v.shape[1]), 0)
    return jnp.where(row == 0, jnp.broadcast_to(s, (8, v.shape[1])), 0.0)


def _params(n_axes=1):
    return pltpu.CompilerParams(dimension_semantics=("arbitrary",) * n_axes, vmem_limit_bytes=VMEM_LIMIT)


def _rows(tm, w):
    return pl.BlockSpec((tm, w), lambda i: (i, 0))


def _const(shape):
    nd = len(shape)
    return pl.BlockSpec(shape, lambda *_: (0,) * nd)


def _sds(shape, dtype=F32):
    return jax.ShapeDtypeStruct(shape, dtype)


def _mm_tn(a, b, name):
    g = max(a.shape[0] if a.ndim == 3 else 1, b.shape[0] if b.ndim == 3 else 1)
    t, m = a.shape[-2:]
    n = b.shape[-1]

    def body(a_ref, b_ref, o_ref, acc_ref):
        i = pl.program_id(1)

        @pl.when(i == 0)
        def _():
            acc_ref[...] = jnp.zeros_like(acc_ref)

        acc_ref[...] += _dot_tn(a_ref[...].astype(BF16), b_ref[...].astype(BF16))

        @pl.when(i == pl.num_programs(1) - 1)
        def _():
            o_ref[...] = acc_ref[...].astype(BF16)

    tk = min(t, WK)

    def spec(v):
        w = v.shape[-1]
        if v.ndim == 3:
            return pl.BlockSpec((None, tk, w), lambda j, i: (j, i, 0))
        return pl.BlockSpec((tk, w), lambda j, i: (i, 0))

    return pl.pallas_call(
        body, name=name, grid=(g, t // tk), in_specs=[spec(a), spec(b)],
        out_specs=pl.BlockSpec((None, m, n), lambda j, i: (j, 0, 0)), out_shape=_sds((g, m, n), BF16),
        scratch_shapes=[pltpu.VMEM((m, n), F32)], compiler_params=_params(2))(a, b)


def _rope_tables(pos, inv_freq):
    t = pos.shape[0]

    def body(pos_ref, f_ref, cos_ref, sin_ref):
        ang = pos_ref[...].astype(F32) * f_ref[...]
        sign = jnp.where((_lane(ang.shape) & 32) == 0, -1.0, 1.0)
        cos_ref[...] = jnp.cos(ang)
        sin_ref[...] = jnp.sin(ang) * sign

    return pl.pallas_call(
        body, name="rope_tables", grid=(t // TM,),
        in_specs=[_rows(TM, 1), _const((1, 128))], out_specs=[_rows(TM, 128), _rows(TM, 128)],
        out_shape=[_sds((t, 128)), _sds((t, 128))], compiler_params=_params())(pos, inv_freq)


def _mixer_in_fwd(x, mix_norm, w_in, qn, kn, gvw, cos, sin):
    t = x.shape[0]

    def body(x_ref, g_ref, w_ref, qn_ref, kn_ref, gvw_ref, cos_ref, sin_ref, ones_ref,
             h_ref, qk_ref, gz_ref, q_ref, k_ref, v_ref, gu_ref, gvn_ref):
        x = x_ref[...]
        h = (x * _rs(x) * g_ref[...]).astype(BF16)
        h_ref[...] = h
        proj = _dot_nt(h, w_ref[...])
        qk = proj[:, :AW + KW]
        qk_ref[...] = qk
        gz = proj[:, AW + 2 * KW:]
        gz_ref[...] = gz
        cos2, sin2 = cos_ref[...], sin_ref[...]
        q = qk[:, :AW]
        q = q * _rs64(q, ones_ref) * qn_ref[...]
        q_ref[...] = _rope(q, jnp.tile(cos2, (1, 4)), jnp.tile(sin2, (1, 4))).astype(BF16)
        k = qk[:, AW:]
        k = k * _rs64(k, ones_ref) * kn_ref[...]
        k_ref[...] = _rope(k, cos2, sin2).astype(BF16)
        v_ref[...] = proj[:, AW + KW:AW + 2 * KW].astype(BF16)
        gu_ref[...] = _gelu(gz[:, :GW])
        gv = _gelu(gz[:, GW:])
        gvn_ref[...] = (gv * _rs(gv) * gvw_ref[...]).astype(BF16)

    return pl.pallas_call(
        body, name="mixer_in_fwd", grid=(t // TM,),
        in_specs=[_rows(TM, D), _const((1, D)), _const((IN, D)), _const((1, AW)), _const((1, KW)),
                  _const((1, GW)), _rows(TM, 128), _rows(TM, 128), _const((AW, AW))],
        out_specs=[_rows(TM, D), _rows(TM, AW + KW), _rows(TM, 2 * GW), _rows(TM, AW), _rows(TM, KW),
                   _rows(TM, KW), _rows(TM, GW), _rows(TM, GW)],
        out_shape=[_sds((t, D), BF16), _sds((t, AW + KW)), _sds((t, 2 * GW)), _sds((t, AW), BF16),
                   _sds((t, KW), BF16), _sds((t, KW), BF16), _sds((t, GW)), _sds((t, GW), BF16)],
        compiler_params=_params())(x, mix_norm, w_in, qn, kn, gvw, cos, sin, _head_ones())


def _dup_half(kk, g):
    lane = _lane(kk.shape)
    other = pltpu.roll(kk, 64, 1)
    keep = (lane < 64) if g == 0 else (lane >= 64)
    return jnp.where(keep, kk, other).astype(BF16)


def _swa_mask(first_block):
    qi = lax.broadcasted_iota(jnp.int32, (4 * BLK, 2 * BLK), 0) & (BLK - 1)
    kj = lax.broadcasted_iota(jnp.int32, (4 * BLK, 2 * BLK), 1)
    diff = qi + BLK - kj
    band = (diff >= 0) & (diff < BLK)
    return band & (jnp.logical_not(first_block) | (kj >= BLK))


def _stack_heads(a2, b2):
    lo = _lane(a2.shape) < 64
    z = jnp.zeros_like(a2)
    return jnp.concatenate([jnp.where(lo, a2, z), jnp.where(lo, z, a2), jnp.where(lo, b2, z), jnp.where(lo, z, b2)], axis=0)


def _unstack_heads(o):
    lo = _lane((BLK, 128)) < 64
    return jnp.where(lo, o[0:BLK], o[BLK:2 * BLK]), jnp.where(lo, o[2 * BLK:3 * BLK], o[3 * BLK:4 * BLK])


def _sink_col(sink_ref, g):
    row = lax.broadcasted_iota(jnp.int32, (4 * BLK, 1), 0)
    s = [sink_ref[0, 4 * g + j] for j in range(4)]
    return jnp.where(row < BLK, s[0], jnp.where(row < 2 * BLK, s[1], jnp.where(row < 3 * BLK, s[2], s[3])))


def _swa_probs(qs, kd, mask, sink):
    s = _dot_nt(qs, kd) * (1.0 / math.sqrt(HD))
    s = jnp.where(mask, s, NEG)
    m = jnp.maximum(jnp.max(s, axis=-1, keepdims=True), sink)
    p = jnp.exp(s - m)
    ps = jnp.exp(sink - m)
    inv = 1.0 / (jnp.sum(p, axis=-1, keepdims=True) + ps)
    return p * inv, ps * inv


SB = 4
SB_BWD = 2


def _swa_fwd(q, k, v, sinks):
    t = q.shape[0]
    ts = min(t, SB * BLK)

    def body(sink_ref, q_ref, kc_ref, kp_ref, vc_ref, vp_ref, o_ref):
        i = pl.program_id(0)
        kk = jnp.concatenate([kp_ref[...], kc_ref[...]], axis=0).astype(F32)
        vv = jnp.concatenate([vp_ref[...], vc_ref[...]], axis=0).astype(F32)
        for b in range(ts // BLK):
            r = slice(b * BLK, (b + 1) * BLK)
            kb, vb = kk[b * BLK:(b + 2) * BLK], vv[b * BLK:(b + 2) * BLK]
            mask = _swa_mask(i == 0) if b == 0 else _swa_mask(False)
            for g in range(2):
                qs = _stack_heads(q_ref[r, 256 * g:256 * g + 128], q_ref[r, 256 * g + 128:256 * g + 256])
                pn, _ = _swa_probs(qs, _dup_half(kb, g), mask, _sink_col(sink_ref, g))
                oa, ob = _unstack_heads(_dot(pn.astype(BF16), _dup_half(vb, g)))
                o_ref[r, 256 * g:256 * g + 128] = oa
                o_ref[r, 256 * g + 128:256 * g + 256] = ob

    cur = lambda i: (i, 0)
    prev = lambda i: (jnp.maximum(i * (ts // BLK) - 1, 0), 0)
    return pl.pallas_call(
        body, name="swa_fwd", grid=(t // ts,),
        in_specs=[pl.BlockSpec(memory_space=pltpu.SMEM), pl.BlockSpec((ts, AW), cur),
                  pl.BlockSpec((ts, KW), cur), pl.BlockSpec((BLK, KW), prev),
                  pl.BlockSpec((ts, KW), cur), pl.BlockSpec((BLK, KW), prev)],
        out_specs=pl.BlockSpec((ts, AW), cur), out_shape=_sds((t, AW)),
        compiler_params=_params())(sinks, q, k, k, v, v)


def _causal_bf16(w_ref, h, transposed):
    r = lax.broadcasted_iota(jnp.int32, (BLK, BLK), 0)
    c = lax.broadcasted_iota(jnp.int32, (BLK, BLK), 1)
    keep = (r <= c) if transposed else (c <= r)
    return jnp.where(keep, w_ref[h], 0.0).astype(BF16)


def _gmlp_mix(w_ref, xin, transposed):
    lo = _lane((BLK, 128)) < 64
    wm = [_causal_bf16(w_ref, h, transposed) for h in range(8)]
    rows = []
    for c in range(xin.shape[0] // BLK):
        cols = []
        for j in range(4):
            xs = xin[c * BLK:(c + 1) * BLK, 128 * j:128 * (j + 1)]
            cols.append(jnp.where(lo, _dot(wm[2 * j], xs), _dot(wm[2 * j + 1], xs)))
        rows.append(jnp.concatenate(cols, axis=1))
    return jnp.concatenate(rows, axis=0)


def _mixer_out_fwd(attn, gvn, gu, ws, bfull, x, w_out, aon, gon, xan):
    t = x.shape[0]

    def body(a_ref, v_ref, gu_ref, ws_ref, b_ref, x_ref, w_ref, aon_ref, gon_ref, xan_ref, gm_ref, y_ref, x1_ref, h2_ref):
        a = a_ref[...]
        g = gu_ref[...] * (_gmlp_mix(ws_ref, v_ref[...], False) + jnp.tile(b_ref[...], (TM // BLK, 1)))
        gm_ref[...] = g
        y = jnp.concatenate([a * _rs(a) * aon_ref[...], g * _rs(g) * gon_ref[...]], axis=1).astype(BF16)
        y_ref[...] = y
        x1 = x_ref[...] + _dot(y, w_ref[...])
        x1_ref[...] = x1
        h2_ref[...] = (x1 * _rs(x1) * xan_ref[...]).astype(BF16)

    return pl.pallas_call(
        body, name="mixer_out_fwd", grid=(t // TM,),
        in_specs=[_rows(TM, AW), _rows(TM, GW), _rows(TM, GW), _const((8, BLK, BLK)), _const((BLK, GW)), _rows(TM, D),
                  _const((D, D)), _const((1, AW)), _const((1, GW)), _const((1, D))],
        out_specs=[_rows(TM, GW), _rows(TM, D), _rows(TM, D), _rows(TM, D)],
        out_shape=[_sds((t, GW)), _sds((t, D), BF16), _sds((t, D)), _sds((t, D), BF16)],
        compiler_params=_params())(attn, gvn, gu, ws, bfull, x, w_out, aon, gon, xan)


def _mem_kv_fwd(mem, mem_norm, wkv, kn4):
    def body(m_ref, g_ref, w_ref, kn_ref, mh_ref, kpre_ref, k_ref, v_ref):
        m = m_ref[...]
        mh = (m * _rs(m) * g_ref[...]).astype(BF16)
        mh_ref[...] = mh
        for h in range(XH):
            sl = slice(XD * h, XD * (h + 1))
            kh = _dot(mh, w_ref[h])
            kpre_ref[:, sl] = kh
            k_ref[:, sl] = (kh * _rs(kh) * kn_ref[:, sl]).astype(BF16)
            v_ref[:, sl] = _dot(mh, w_ref[XH + h]).astype(BF16)

    return pl.pallas_call(
        body, name="mem_kv_fwd",
        out_shape=[_sds((MEM, D), BF16), _sds((MEM, D)), _sds((MEM, D), BF16), _sds((MEM, D), BF16)],
        compiler_params=pltpu.CompilerParams(vmem_limit_bytes=VMEM_LIMIT))(mem, mem_norm, wkv, kn4)


def _xattn_probs(qpre_h, qn_h, k_h):
    rq = _rs(qpre_h)
    q2 = (qpre_h * rq * qn_h).astype(BF16)
    s = _dot_nt(q2, k_h) * (1.0 / math.sqrt(XD))
    p = jnp.exp(s - jnp.max(s, axis=-1, keepdims=True))
    return p * (1.0 / jnp.sum(p, axis=-1, keepdims=True)), q2, rq


def _xattn_fwd(h2, x1, wq, qn4, k2, v2, wo, ffn_norm):
    t = x1.shape[0]

    def body(h_ref, x_ref, wq_ref, qn_ref, k_ref, v_ref, wo_ref, fn_ref, qpre_ref, o_ref, x2_ref, h3_ref):
        qpre = _dot(h_ref[...], wq_ref[...])
        qpre_ref[...] = qpre
        outs = []
        for h in range(XH):
            sl = slice(XD * h, XD * (h + 1))
            pn, _, _ = _xattn_probs(qpre[:, sl], qn_ref[:, sl], k_ref[:, sl])
            outs.append(_dot(pn.astype(BF16), v_ref[:, sl]))
        o = jnp.concatenate(outs, axis=1).astype(BF16)
        o_ref[...] = o
        x2 = x_ref[...] + _dot(o, wo_ref[...])
        x2_ref[...] = x2
        h3_ref[...] = (x2 * _rs(x2) * fn_ref[...]).astype(BF16)

    return pl.pallas_call(
        body, name="xattn_fwd", grid=(t // TM,),
        in_specs=[_rows(TM, D), _rows(TM, D), _const((D, D)), _const((1, D)), _const((MEM, D)), _const((MEM, D)),
                  _const((D, D)), _const((1, D))],
        out_specs=[_rows(TM, D)] * 4,
        out_shape=[_sds((t, D)), _sds((t, D), BF16), _sds((t, D)), _sds((t, D), BF16)],
        compiler_params=_params())(h2, x1, wq, qn4, k2, v2, wo, ffn_norm)


SW = 704
NG = FF // SW
FM = 256
HALO = 16


def _resident(shape):
    nd = len(shape)
    return pl.BlockSpec(shape, lambda *_: (0,) * nd, pipeline_mode=pl.Buffered(1))


def _halo_before(i):
    return jnp.maximum(i * (FM // HALO) - 1, 0)


def _conv(e, w):
    return w[2:3, :] * e + pltpu.roll(w[1:2, :] * e + pltpu.roll(w[0:1, :] * e, 1, 0), 1, 0)


def _conv_t(dc, w):
    n = dc.shape[0]
    return w[2:3, :] * dc + pltpu.roll(w[1:2, :] * dc + pltpu.roll(w[0:1, :] * dc, n - 1, 0), n - 1, 0)


def _ffn_fwd(h3, x2, target, up, conv, conv_b, down):
    t = x2.shape[0]

    def body(h_ref, hp_ref, x_ref, t_ref, up_ref, w_ref, b_ref, dn_ref, a_ref, u_ref, gs_ref, dy_ref, loss_ref, acc_ref):
        i = pl.program_id(0)

        @pl.when(i == 0)
        def _():
            acc_ref[...] = jnp.zeros_like(acc_ref)

        before = jnp.where(i > 0, hp_ref[...], jnp.zeros_like(hp_ref))
        he = jnp.concatenate([before, h_ref[...]], axis=0)
        err = x_ref[...] - t_ref[...]
        for d in range(NG):
            c = []
            for s in range(2):
                a = _dot_nt(he, up_ref[s * NG + d])
                a_ref[s * NG + d] = a[HALO:].astype(BF16)
                c.append(_conv(a, w_ref[s, d])[HALO:] + b_ref[s, d])
            gl, gg = _gelu_and_grad(c[0])
            gs_ref[d] = gl.astype(BF16)
            gs_ref[NG + d] = (gg * c[1]).astype(BF16)
            u = (gl * c[1]).astype(BF16)
            u_ref[d] = u
            err = err + _dot(u, dn_ref[d])
        dy_ref[...] = err * (1.0 / D)
        acc_ref[...] += jnp.sum(err * err, axis=0, keepdims=True)

        @pl.when(i == pl.num_programs(0) - 1)
        def _():
            loss_ref[...] = jnp.full((8, 128), 0.5 / D, F32) * jnp.sum(acc_ref[...])

    return pl.pallas_call(
        body, name="ffn_fwd", grid=(t // FM,),
        in_specs=[_rows(FM, D), pl.BlockSpec((HALO, D), lambda i: (_halo_before(i), 0)), _rows(FM, D), _rows(FM, D),
                  _resident((NDEV, SW, D)), _resident((2, NG, 3, SW)), _resident((2, NG, 1, SW)), _resident((NG, SW, D))],
        out_specs=[pl.BlockSpec((NDEV, FM, SW), lambda i: (0, i, 0)), pl.BlockSpec((NG, FM, SW), lambda i: (0, i, 0)),
                   pl.BlockSpec((NDEV, FM, SW), lambda i: (0, i, 0)), _rows(FM, D), _const((8, 128))],
        out_shape=[_sds((NDEV, t, SW), BF16), _sds((NG, t, SW), BF16), _sds((NDEV, t, SW), BF16), _sds((t, D)),
                   _sds((8, 128))],
        scratch_shapes=[pltpu.VMEM((1, D), F32)], compiler_params=_params())(h3, h3, x2, target, up, conv, conv_b, down)


def _gelu_and_grad(z):
    z2 = z * z
    t = jnp.tanh(GELU_C0 * (z + GELU_C1 * z * z2))
    phi = 0.5 * (1.0 + t)
    return z * phi, phi + z * (1.0 - t * t) * (0.5 * GELU_C0 + (1.5 * GELU_C0 * GELU_C1) * z2)


def _ffn_bwd(dy, a, gs, x2, up, conv, down, ffn_norm):
    t = x2.shape[0]
    nt = t // FM
    n = FM + HALO

    def body(dy_ref, dyn_ref, a_ref, gs_ref, gsn_ref, x_ref, up_ref, w_ref, dn_ref, g_ref,
             dx_ref, da_ref, s_ref, dfn_ref):
        i = pl.program_id(0)

        @pl.when(i == 0)
        def _():
            s_ref[...] = jnp.zeros_like(s_ref)
            dfn_ref[...] = jnp.zeros_like(dfn_ref)

        last = i == nt - 1
        dy = dy_ref[...]
        dye = jnp.concatenate([dy, jnp.where(last, 0.0, dyn_ref[...])], axis=0).astype(BF16)
        dh = jnp.zeros((FM, D), F32)
        row = lax.broadcasted_iota(jnp.int32, (8, SW), 0)
        for d in range(NG):
            du = _dot_nt(dye, dn_ref[d])
            for s in range(2):
                j = s * NG + d
                k = NG + d if s == 0 else d
                dc = du * jnp.concatenate([gs_ref[k], gsn_ref[k]], axis=0).astype(F32)
                w = w_ref[s, d]
                tile = a_ref[j].astype(F32)
                d1 = pltpu.roll(dc, n - 1, 0)
                d2 = pltpu.roll(d1, n - 1, 0)
                da = (w[2:3, :] * dc + w[1:2, :] * d1 + w[0:1, :] * d2)[0:FM].astype(BF16)
                da_ref[j] = da
                dh = dh + _dot(da, up_ref[j])
                sums = [jnp.sum(v[0:FM] * tile, axis=0, keepdims=True) for v in (d2, d1, dc)]
                sums.append(jnp.sum(dc[0:FM], axis=0, keepdims=True))
                upd = jnp.zeros((8, SW), F32)
                for r, v in enumerate(sums):
                    upd = jnp.where(row == r, jnp.broadcast_to(v, (8, SW)), upd)
                s_ref[s, d] += upd
        x = x_ref[...]
        dx, dg = _rms_bwd(dh, x, _rs(x), g_ref[...])
        dx_ref[...] = dy + dx
        dfn_ref[...] += _colsum8(dg)

    last_halo = t // HALO - 1
    after = lambda i: jnp.minimum((i + 1) * (FM // HALO), last_halo)
    return pl.pallas_call(
        body, name="ffn_bwd", grid=(nt,),
        in_specs=[_rows(FM, D), pl.BlockSpec((HALO, D), lambda i: (after(i), 0)),
                  pl.BlockSpec((NDEV, FM, SW), lambda i: (0, i, 0)),
                  pl.BlockSpec((NDEV, FM, SW), lambda i: (0, i, 0)),
                  pl.BlockSpec((NDEV, HALO, SW), lambda i: (0, after(i), 0)),
                  _rows(FM, D), _resident((NDEV, SW, D)), _resident((2, NG, 3, SW)), _resident((NG, SW, D)), _const((1, D))],
        out_specs=[_rows(FM, D), pl.BlockSpec((NDEV, FM, SW), lambda i: (0, i, 0)), _const((2, NG, 8, SW)), _const((8, D))],
        out_shape=[_sds((t, D)), _sds((NDEV, t, SW), BF16), _sds((2, NG, 8, SW)), _sds((8, D))],
        compiler_params=_params())(dy, dy, a, gs, gs, x2, up, conv, down, ffn_norm)


BT = 512


def _xattn_bwd(dx2, x1, qpre, k2, v2, wq, wo, qn4, xan):
    t = x1.shape[0]

    def body(dx2_ref, x1_ref, qpre_ref, k_ref, v_ref, wq_ref, wo_ref, qn_ref, xan_ref,
             dx1_ref, dqpre_ref, dk_ref, dv_ref, dqn_ref, dxan_ref):
        @pl.when(pl.program_id(0) == 0)
        def _():
            for r in (dk_ref, dv_ref, dqn_ref, dxan_ref):
                r[...] = jnp.zeros_like(r)

        dx2 = dx2_ref[...]
        do = _dot_nt(dx2.astype(BF16), wo_ref[...])
        dqs = []
        for h in range(XH):
            sl = slice(XD * h, XD * (h + 1))
            qpre_h = qpre_ref[:, sl]
            pn, q2, rq = _xattn_probs(qpre_h, qn_ref[:, sl], k_ref[:, sl])
            do_h = do[:, sl].astype(BF16)
            dp = _dot_nt(do_h, v_ref[:, sl])
            ds = (pn * (dp - jnp.sum(pn * dp, axis=-1, keepdims=True)) * (1.0 / math.sqrt(XD))).astype(BF16)
            dq2 = _dot(ds, k_ref[:, sl])
            dk_ref[:, sl] += _dot_tn(ds, q2)
            dv_ref[:, sl] += _dot_tn(pn.astype(BF16), do_h)
            dqh, dg = _rms_bwd(dq2, qpre_h, rq, qn_ref[:, sl])
            dqn_ref[...] += _colsum8(dg)
            dqs.append(dqh)
        dqpre = jnp.concatenate(dqs, axis=1).astype(BF16)
        dqpre_ref[...] = dqpre
        dh2 = _dot_nt(dqpre, wq_ref[...])
        x1 = x1_ref[...]
        dx, dg = _rms_bwd(dh2, x1, _rs(x1), xan_ref[...])
        dx1_ref[...] = dx2 + dx
        dxan_ref[...] += _colsum8(dg)

    return pl.pallas_call(
        body, name="xattn_bwd", grid=(t // BT,),
        in_specs=[_rows(BT, D), _rows(BT, D), _rows(BT, D), _const((MEM, D)), _const((MEM, D)), _const((D, D)),
                  _const((D, D)), _const((1, D)), _const((1, D))],
        out_specs=[_rows(BT, D), _rows(BT, D), _const((MEM, D)), _const((MEM, D)), _const((8, XD)), _const((8, D))],
        out_shape=[_sds((t, D)), _sds((t, D), BF16), _sds((MEM, D)), _sds((MEM, D)), _sds((8, XD)), _sds((8, D))],
        compiler_params=_params())(dx2, x1, qpre, k2, v2, wq, wo, qn4, xan)


def _mem_kv_bwd(mem, mh, kpre, dk2, dv2, wkv, kn4, mem_norm):
    def body(m_ref, mh_ref, kpre_ref, dk_ref, dv_ref, w_ref, kn_ref, g_ref, dw_ref, dkn_ref, dmn_ref):
        dkn = jnp.zeros((8, XD), F32)
        dm = jnp.zeros((MEM, D), F32)
        mh = mh_ref[...]
        for h in range(XH):
            sl = slice(XD * h, XD * (h + 1))
            kh = kpre_ref[:, sl]
            dkh, dg = _rms_bwd(dk_ref[:, sl], kh, _rs(kh), kn_ref[:, sl])
            dkn = dkn + _colsum8(dg)
            dkh = dkh.astype(BF16)
            dvh = dv_ref[:, sl].astype(BF16)
            dw_ref[h] = _dot_tn(mh, dkh).astype(BF16)
            dw_ref[XH + h] = _dot_tn(mh, dvh).astype(BF16)
            dm = dm + _dot_nt(dkh, w_ref[h]) + _dot_nt(dvh, w_ref[XH + h])
        dkn_ref[...] = dkn
        m = m_ref[...]
        _, dg = _rms_bwd(dm, m, _rs(m), g_ref[...])
        dmn_ref[...] = _colsum8(dg)

    return pl.pallas_call(
        body, name="mem_kv_bwd", out_shape=[_sds((2 * XH, D, XD), BF16), _sds((8, XD)), _sds((8, D))],
        compiler_params=pltpu.CompilerParams(vmem_limit_bytes=VMEM_LIMIT))(mem, mh, kpre, dk2, dv2, wkv, kn4, mem_norm)


def _mixer_out_bwd(dx1, attn, gm, w_out, aon, gon, gu, gvn, gz, ws, wst, bfull, gvw):
    t = dx1.shape[0]
    nc = TM // BLK

    def body(dx_ref, a_ref, g_ref, wo_ref, aon_ref, gon_ref, gu_ref, x_ref, gz_ref, w_ref, wt_ref, b_ref, gvw_ref,
             da_ref, dan_ref, dgn_ref, dgz_ref, dw_ref, db_ref, dgvw_ref, dbacc_ref):
        @pl.when(pl.program_id(0) == 0)
        def _():
            for r in (dan_ref, dgn_ref, dw_ref, dbacc_ref, dgvw_ref):
                r[...] = jnp.zeros_like(r)

        dy = _dot_nt(dx_ref[...].astype(BF16), wo_ref[...])
        a, g = a_ref[...], g_ref[...]
        da, dna = _rms_bwd(dy[:, :AW], a, _rs(a), aon_ref[...])
        dgm, dng = _rms_bwd(dy[:, AW:], g, _rs(g), gon_ref[...])
        da_ref[...] = da
        dan_ref[...] += _colsum8(dna)
        dgn_ref[...] += _colsum8(dng)

        xin = x_ref[...]
        mixed = _gmlp_mix(w_ref, xin, False) + jnp.tile(b_ref[...], (nc, 1))
        dgu = dgm * mixed
        dmixed = dgm * gu_ref[...]
        lo = _lane((BLK, 128)) < 64
        dbias = jnp.zeros((BLK, GW), F32)
        for c in range(nc):
            dmc = dmixed[c * BLK:(c + 1) * BLK]
            dbias = dbias + dmc
            for j in range(4):
                dm2 = dmc[:, 128 * j:128 * (j + 1)]
                xs = xin[c * BLK:(c + 1) * BLK, 128 * j:128 * (j + 1)]
                z = jnp.zeros_like(dm2)
                dw_ref[2 * j] += _dot_nt(jnp.where(lo, dm2, z).astype(BF16), xs)
                dw_ref[2 * j + 1] += _dot_nt(jnp.where(lo, z, dm2).astype(BF16), xs)
        dbacc_ref[...] += dbias
        dgvn = _gmlp_mix(wt_ref, dmixed.astype(BF16), True)
        gz_u, gz_v = gz_ref[:, :GW], gz_ref[:, GW:]
        gv = _gelu(gz_v)
        dgv, dg = _rms_bwd(dgvn, gv, _rs(gv), gvw_ref[...])
        dgvw_ref[...] += _colsum8(dg)
        dgz_ref[:, :GW] = (dgu * _gelu_grad(gz_u)).astype(BF16)
        dgz_ref[:, GW:] = (dgv * _gelu_grad(gz_v)).astype(BF16)

        @pl.when(pl.program_id(0) == pl.num_programs(0) - 1)
        def _():
            s = dbacc_ref[...]
            sel = (lax.broadcasted_iota(jnp.int32, (8, GW), 1) // HD
                   == lax.broadcasted_iota(jnp.int32, (8, GW), 0)).astype(BF16)
            hi = s.astype(BF16)
            r1 = s - hi.astype(F32)
            mid = r1.astype(BF16)
            lo = (r1 - mid.astype(F32)).astype(BF16)
            db_ref[...] = _dot_nt(sel, hi) + _dot_nt(sel, mid) + _dot_nt(sel, lo)
            r = lax.broadcasted_iota(jnp.int32, (BLK, BLK), 0)
            c = lax.broadcasted_iota(jnp.int32, (BLK, BLK), 1)
            for h in range(8):
                dw_ref[h] = jnp.where(c <= r, dw_ref[h], 0.0)

    return pl.pallas_call(
        body, name="mixer_out_bwd", grid=(t // TM,),
        in_specs=[_rows(TM, D), _rows(TM, AW), _rows(TM, GW), _const((D, D)), _const((1, AW)), _const((1, GW)),
                  _rows(TM, GW), _rows(TM, GW), _rows(TM, 2 * GW), _const((8, BLK, BLK)), _const((8, BLK, BLK)),
                  _const((BLK, GW)), _const((1, GW))],
        out_specs=[_rows(TM, AW), _const((8, AW)), _const((8, GW)), _rows(TM, 2 * GW), _const((8, BLK, BLK)),
                   _const((8, BLK)), _const((8, GW))],
        out_shape=[_sds((t, AW)), _sds((8, AW)), _sds((8, GW)), _sds((t, 2 * GW), BF16), _sds((8, BLK, BLK)),
                   _sds((8, BLK)), _sds((8, GW))],
        scratch_shapes=[pltpu.VMEM((BLK, GW), F32)],
        compiler_params=_params())(dx1, attn, gm, w_out, aon, gon, gu, gvn, gz, ws, wst, bfull, gvw)


def _fold_half(v):
    return v + pltpu.roll(v, 64, 1)


def _swa_bwd(q, k, v, dattn, sinks):
    t = q.shape[0]
    nb = t // BLK
    ts = min(t, SB_BWD * BLK)
    sb = ts // BLK
    nt = t // ts

    def body(sink_ref, q_ref, kc_ref, kp_ref, vc_ref, vp_ref, do_ref, dq_ref, dk_ref, dv_ref, ds_ref,
             ck_ref, cv_ref, sacc_ref):
        i = pl.program_id(0)

        @pl.when(i == 0)
        def _():
            ck_ref[...] = jnp.zeros_like(ck_ref)
            cv_ref[...] = jnp.zeros_like(cv_ref)
            sacc_ref[...] = jnp.zeros_like(sacc_ref)

        @pl.when(i < nt)
        def _():
            kk = jnp.concatenate([kp_ref[...], kc_ref[...]], axis=0).astype(F32)
            vv = jnp.concatenate([vp_ref[...], vc_ref[...]], axis=0).astype(F32)
            lo256 = _lane((2 * BLK, 128)) < 64
            acc_k = [jnp.zeros((BLK, 128), F32) for _ in range(sb + 1)]
            acc_v = [jnp.zeros((BLK, 128), F32) for _ in range(sb + 1)]
            for b in range(sb):
                r = slice(b * BLK, (b + 1) * BLK)
                kb, vb = kk[b * BLK:(b + 2) * BLK], vv[b * BLK:(b + 2) * BLK]
                mask = _swa_mask(i == 0) if b == 0 else _swa_mask(False)
                dkk = jnp.zeros((2 * BLK, 128), F32)
                dvv = jnp.zeros((2 * BLK, 128), F32)
                for g in range(2):
                    qs = _stack_heads(q_ref[r, 256 * g:256 * g + 128], q_ref[r, 256 * g + 128:256 * g + 256])
                    dos = _stack_heads(do_ref[r, 256 * g:256 * g + 128],
                                       do_ref[r, 256 * g + 128:256 * g + 256]).astype(BF16)
                    kd = _dup_half(kb, g)
                    pn, psn = _swa_probs(qs, kd, mask, _sink_col(sink_ref, g))
                    dp = _dot_nt(dos, _dup_half(vb, g))
                    dd = jnp.sum(pn * dp, axis=-1, keepdims=True)
                    ds = (pn * (dp - dd) * (1.0 / math.sqrt(HD))).astype(BF16)
                    sacc_ref[g] += jnp.broadcast_to(-psn * dd, (4 * BLK, 128))
                    dqa, dqb = _unstack_heads(_dot(ds, kd))
                    dq_ref[r, 256 * g:256 * g + 128] = dqa
                    dq_ref[r, 256 * g + 128:256 * g + 256] = dqb
                    dkg = _fold_half(_dot_tn(ds, qs))
                    dvg = _fold_half(_dot_tn(pn.astype(BF16), dos))
                    keep = lo256 if g == 0 else jnp.logical_not(lo256)
                    dkk = jnp.where(keep, dkg, dkk)
                    dvv = jnp.where(keep, dvg, dvv)
                acc_k[b], acc_k[b + 1] = acc_k[b] + dkk[0:BLK], acc_k[b + 1] + dkk[BLK:]
                acc_v[b], acc_v[b + 1] = acc_v[b] + dvv[0:BLK], acc_v[b + 1] + dvv[BLK:]
            for out_ref, c_ref, acc in ((dk_ref, ck_ref, acc_k), (dv_ref, cv_ref, acc_v)):
                if sb > 1:
                    out_ref[0:ts - BLK] = c_ref[0:ts - BLK]
                out_ref[ts - BLK:ts] = c_ref[ts - BLK:ts] + acc[0]
                for b in range(sb):
                    c_ref[b * BLK:(b + 1) * BLK] = acc[b + 1]

        @pl.when(i == nt)
        def _():
            dk_ref[...] = ck_ref[...]
            dv_ref[...] = cv_ref[...]
            lane = _lane((8, 128))
            acc = jnp.zeros((8, 128), F32)
            for g in range(2):
                for j in range(4):
                    val = jnp.sum(sacc_ref[g, j * BLK:(j + 1) * BLK, :], axis=0, keepdims=True)
                    acc = jnp.where(lane == 4 * g + j, jnp.broadcast_to(val, (8, 128)), acc)
            ds_ref[...] = acc

    cur = lambda i: (jnp.minimum(i, nt - 1), 0)
    before = lambda i: (jnp.clip(i * sb - 1, 0, nb - 1), 0)
    done = lambda i: (jnp.clip(i - 1, 0, nt - 1), 0)
    return pl.pallas_call(
        body, name="swa_bwd", grid=(nt + 1,),
        in_specs=[pl.BlockSpec(memory_space=pltpu.SMEM), pl.BlockSpec((ts, AW), cur),
                  pl.BlockSpec((ts, KW), cur), pl.BlockSpec((BLK, KW), before),
                  pl.BlockSpec((ts, KW), cur), pl.BlockSpec((BLK, KW), before), pl.BlockSpec((ts, AW), cur)],
        out_specs=[pl.BlockSpec((ts, AW), cur), pl.BlockSpec((ts, KW), done), pl.BlockSpec((ts, KW), done),
                   _const((8, 128))],
        out_shape=[_sds((t, AW)), _sds((t, KW)), _sds((t, KW)), _sds((8, 128))],
        scratch_shapes=[pltpu.VMEM((ts, KW), F32), pltpu.VMEM((ts, KW), F32), pltpu.VMEM((2, 4 * BLK, 128), F32)],
        compiler_params=_params())(sinks, q, k, k, v, v, dattn)


def _mixer_in_bwd(dq, dk, dv, dgz, qk, cos, sin, x, dx1, w_in, mix_norm, qn, kn):
    t = x.shape[0]

    def body(dq_ref, dk_ref, dv_ref, dgz_ref, qk_ref, cos_ref, sin_ref, x_ref, dx1_ref, w_ref, g_ref, qn_ref, kn_ref,
             ones_ref, gx_ref, dproj_ref, dmn_ref, dqn_ref, dkn_ref, qacc_ref, kacc_ref):
        i = pl.program_id(0)

        @pl.when(i == 0)
        def _():
            dmn_ref[...] = jnp.zeros_like(dmn_ref)
            qacc_ref[...] = jnp.zeros_like(qacc_ref)
            kacc_ref[...] = jnp.zeros_like(kacc_ref)

        cos2, sin2 = cos_ref[...], sin_ref[...]
        qpre, kpre = qk_ref[:, :AW], qk_ref[:, AW:]
        dqh = _rope_bwd(dq_ref[...], jnp.tile(cos2, (1, 4)), jnp.tile(sin2, (1, 4)))
        dqpre, dgq = _rms64_bwd(dqh, qpre, _rs64(qpre, ones_ref), qn_ref[...], ones_ref)
        dkh = _rope_bwd(dk_ref[...], cos2, sin2)
        dkpre, dgk = _rms64_bwd(dkh, kpre, _rs64(kpre, ones_ref), kn_ref[...], ones_ref)
        qacc_ref[...] += jnp.sum(dgq, axis=0, keepdims=True)
        kacc_ref[...] += jnp.sum(dgk, axis=0, keepdims=True)
        dproj = jnp.concatenate([dqpre.astype(BF16), dkpre.astype(BF16), dv_ref[...].astype(BF16), dgz_ref[...]], axis=1)
        dproj_ref[...] = dproj
        dh = _dot(dproj, w_ref[...])
        xv = x_ref[...]
        dx, dg = _rms_bwd(dh, xv, _rs(xv), g_ref[...])
        gx_ref[...] = dx1_ref[...] + dx
        dmn_ref[...] += _colsum8(dg)

        @pl.when(i == pl.num_programs(0) - 1)
        def _():
            qa = qacc_ref[...]
            q4 = qa[:, 0:128] + qa[:, 128:256] + qa[:, 256:384] + qa[:, 384:512]
            dqn_ref[...] = jnp.broadcast_to(_fold_half(q4), (8, 128))
            dkn_ref[...] = jnp.broadcast_to(_fold_half(kacc_ref[...]), (8, 128))

    return pl.pallas_call(
        body, name="mixer_in_bwd", grid=(t // TM,),
        in_specs=[_rows(TM, AW), _rows(TM, KW), _rows(TM, KW), _rows(TM, 2 * GW), _rows(TM, AW + KW), _rows(TM, 128),
                  _rows(TM, 128), _rows(TM, D), _rows(TM, D), _const((IN, D)), _const((1, D)), _const((1, AW)),
                  _const((1, KW)), _const((AW, AW))],
        out_specs=[_rows(TM, D), _rows(TM, IN), _const((8, D)), _const((8, 128)), _const((8, 128))],
        out_shape=[_sds((t, D)), _sds((t, IN), BF16), _sds((8, D)), _sds((8, 128)), _sds((8, 128))],
        scratch_shapes=[pltpu.VMEM((1, AW), F32), pltpu.VMEM((1, KW), F32)],
        compiler_params=_params())(dq, dk, dv, dgz, qk, cos, sin, x, dx1, w_in, mix_norm, qn, kn, _head_ones())


def _local_step(x, mem, pos, target, p, fetch, ship):
    t = x.shape[0]
    p = dict(p)
    p.update(fetch(0, None))
    inv_freq = 1.0 / (ROPE_THETA ** (jnp.arange(HD // 2, dtype=F32) * (2.0 / HD)))
    cos, sin = _rope_tables(pos, jnp.tile(inv_freq, 4).reshape(1, 128))
    qn = jnp.tile(p["q_norm"], (1, AW // HD))
    kn = jnp.tile(p["k_norm"], (1, KW // HD))
    qn4 = jnp.tile(p["xa_q_norm"], (1, XH))
    kn4 = jnp.tile(p["xa_k_norm"], (1, XH))
    ws = p["gmlp_ws"]
    wst = jnp.swapaxes(ws, 1, 2)
    bfull = jnp.repeat(p["gmlp_bs"].T, HD, axis=1)
    conv_b = p["ffn_conv_b"]

    h1, qk, gz, q, k, v, gu, gvn = _mixer_in_fwd(x, p["mix_norm"], p["w_in"], qn, kn, p["gmlp_v_norm"], cos, sin)
    attn = _swa_fwd(q, k, v, p["attn_sinks"])
    p.update(fetch(1, attn))
    gm, ycat, x1, h2 = _mixer_out_fwd(attn, gvn, gu, ws, bfull, x, p["w_out"], p["attn_out_norm"], p["gmlp_out_norm"],
                                      p["xa_norm"])
    mh, kpre, k2, v2 = _mem_kv_fwd(mem, p["mem_norm"], p["xa_wkv"], kn4)
    qpre, o, x2, h3 = _xattn_fwd(h2, x1, p["xa_wq"], qn4, k2, v2, p["xa_wo"], p["ffn_norm"])
    p.update(fetch(2, h3))
    conv = p["ffn_conv"]
    a, u, gs, dy, loss8 = _ffn_fwd(h3, x2, target, p["ffn_up"], conv, conv_b, p["ffn_down"])

    raw = {}
    d_down = _mm_tn(u, dy, "ffn_down_bwd_w")
    dx2, da, raw["conv_sums"], raw["ffn_norm"] = _ffn_bwd(dy, a, gs, x2, p["ffn_up"], conv, p["ffn_down"], p["ffn_norm"])
    d_up = _mm_tn(da, h3, "ffn_up_bwd_w")
    token = ship(0, {"ffn_down": d_down, "ffn_up": d_up, "ffn_conv": raw["conv_sums"][:, :, 0:3]})
    dx1, dqpre, dk2, dv2, raw["xa_q_norm"], raw["xa_norm"] = _xattn_bwd(
        dx2, x1, qpre, k2, v2, p["xa_wq"], p["xa_wo"], qn4 + jnp.tile(token[0:1], (1, D // 128)), p["xa_norm"])
    d_wo = _mm_tn(o, dx2, "xa_wo_bwd_w")
    d_wq = _mm_tn(h2, dqpre, "xa_wq_bwd_w")
    d_wkv, raw["xa_k_norm"], raw["mem_norm"] = _mem_kv_bwd(mem, mh, kpre, dk2, dv2, p["xa_wkv"], kn4, p["mem_norm"])
    d_w_out = _mm_tn(ycat, dx1, "w_out_bwd_w")
    token = ship(1, {"xa_wo": d_wo, "xa_wq": d_wq, "xa_wkv": d_wkv, "w_out": d_w_out})
    (dattn, raw["attn_out_norm"], raw["gmlp_out_norm"], dgz, raw["gmlp_ws"], raw["gmlp_bs"],
     raw["gmlp_v_norm"]) = _mixer_out_bwd(
        dx1, attn, gm, p["w_out"], p["attn_out_norm"] + jnp.tile(token[0:1], (1, AW // 128)), p["gmlp_out_norm"],
        gu, gvn, gz, ws, wst, bfull, p["gmlp_v_norm"])
    token = ship(2, {}, [raw["gmlp_ws"]])
    dq, dk, dv, raw["attn_sinks"] = _swa_bwd(q, k, v, dattn, p["attn_sinks"] + token[0:1, 0:8])
    grad_x, dproj, raw["mix_norm"], raw["q_norm"], raw["k_norm"] = _mixer_in_bwd(
        dq, dk, dv, dgz, qk, cos, sin, x, dx1, p["w_in"], p["mix_norm"], qn, kn)
    d_w_in = _mm_tn(dproj, h1, "w_in_bwd_w")
    raw["loss"] = loss8
    return grad_x, {"w_in": d_w_in}, raw


def _cast_shards(shards):
    def body(*refs):
        n = len(refs) // 2
        for i_ref, o_ref in zip(refs[:n], refs[n:]):
            o_ref[...] = i_ref[...].astype(BF16)

    return pl.pallas_call(body, name="cast_shards", out_shape=[_sds(s.shape, BF16) for s in shards],
                          compiler_params=pltpu.CompilerParams(vmem_limit_bytes=VMEM_LIMIT))(*shards)


HBM_SPEC = pl.BlockSpec(memory_space=pltpu.HBM)
SEM_SPEC = pl.BlockSpec(memory_space=pltpu.SEMAPHORE)


ALL_K = tuple(range(1, NDEV))
CHIP_K = (1, 2, 4, 6)
RELAY_K = (2, 4, 6)


def _peer(k):
    x, y, cc = lax.axis_index("x"), lax.axis_index("y"), lax.axis_index("c")
    return 1 - x if k & 4 else x, 1 - y if k & 2 else y, 1 - cc if k & 1 else cc


def _remote_copies(src_refs, land_refs, send_refs, recv_refs, nd, ks):
    me = 4 * lax.axis_index("x") + 2 * lax.axis_index("y") + lax.axis_index("c")
    copies = []
    for a, (src_ref, land_ref) in enumerate(zip(src_refs, land_refs)):
        for j, k in enumerate(ks):
            px, py, pc = _peer(k)
            copies.append((k, pltpu.make_async_remote_copy(
                src_ref=src_ref.at[4 * px + 2 * py + pc] if a < nd else src_ref, dst_ref=land_ref.at[me],
                send_sem=send_refs[a].at[j], recv_sem=recv_refs[a].at[j],
                device_id=(px, py, pc), device_id_type=pl.DeviceIdType.MESH)))
    return copies


def _relay_copies(land_refs, send_refs, recv_refs):
    copies = []
    for a, land_ref in enumerate(land_refs):
        for j, k in enumerate(RELAY_K):
            px, py, pc = _peer(k)
            slot = land_ref.at[4 * px + 2 * py + pc]
            copies.append(pltpu.make_async_remote_copy(
                src_ref=slot, dst_ref=slot, send_sem=send_refs[a].at[j], recv_sem=recv_refs[a].at[j],
                device_id=_peer(1), device_id_type=pl.DeviceIdType.MESH))
    return copies


def _own_slot(src, by_dest, me):
    block = lax.dynamic_index_in_dim(src, me, 0, keepdims=True) if by_dest else src[None]
    return lax.dynamic_update_index_in_dim(lax.empty((NDEV,) + block.shape[1:], src.dtype), block, me, 0)


SIDE_EFFECT = pltpu.CompilerParams(has_side_effects=pltpu.SideEffectType.DATAFLOW_SIDE_EFFECTING)


def _exchange_start(by_dest, for_all, me, name, ks=ALL_K):
    srcs = list(by_dest) + list(for_all)
    n, nd = len(srcs), len(by_dest)
    lands = [_own_slot(s, a < nd, me) for a, s in enumerate(srcs)]

    def body(*refs):
        for _, cp in _remote_copies(refs[:n], refs[n:2 * n], refs[2 * n:3 * n], refs[3 * n:4 * n], nd, ks):
            cp.start()
        refs[-1][...] = jnp.zeros((8, 128), F32)

    sems = [pltpu.SemaphoreType.DMA((len(ks),))] * (2 * n)
    thru = [pltpu.HBM(v.shape, v.dtype) for v in srcs + lands]
    res = pl.pallas_call(
        body, name=name, out_shape=sems + thru + [_sds((8, 128))],
        in_specs=[HBM_SPEC] * (2 * n), out_specs=[SEM_SPEC] * (2 * n) + [HBM_SPEC] * (2 * n) + [pl.BlockSpec(memory_space=pltpu.VMEM)],
        input_output_aliases={i: 2 * n + i for i in range(2 * n)}, compiler_params=SIDE_EFFECT)(
            *[pltpu.with_memory_space_constraint(v, pltpu.HBM) for v in srcs + lands])
    return (res[:2 * n], res[2 * n:4 * n], nd, ks, None), res[-1]


def _exchange_relay(state, after, name):
    sems, thru, nd, ks, _ = state
    n = len(thru) // 2

    def body(*refs):
        for k, cp in _remote_copies(refs[:n], refs[n:2 * n], refs[2 * n:3 * n], refs[3 * n:4 * n], nd, ks):
            if k in RELAY_K:
                cp.wait_recv()
        for cp in _relay_copies(refs[n:2 * n], refs[4 * n + 1:5 * n + 1], refs[5 * n + 1:6 * n + 1]):
            cp.start()

    relay_sems = [pltpu.SemaphoreType.DMA((len(RELAY_K),))] * (2 * n)
    res = pl.pallas_call(
        body, name=name, out_shape=relay_sems + [pltpu.HBM(v.shape, v.dtype) for v in thru],
        in_specs=[HBM_SPEC] * (2 * n) + [SEM_SPEC] * (2 * n) + [pl.BlockSpec(memory_space=pl.ANY)],
        out_specs=[SEM_SPEC] * (2 * n) + [HBM_SPEC] * (2 * n),
        input_output_aliases={i: 2 * n + i for i in range(2 * n)}, compiler_params=SIDE_EFFECT)(*thru, *sems, after)
    return sems, res[2 * n:], nd, ks, res[:2 * n]


def _exchange_wait(state, after, name):
    sems, thru, nd, ks, relay_sems = state
    n = len(thru) // 2

    def body(*refs):
        for k, cp in _remote_copies(refs[:n], refs[n:2 * n], refs[2 * n:3 * n], refs[3 * n:4 * n], nd, ks):
            cp.wait_send()
            if relay_sems is None or k not in RELAY_K:
                cp.wait_recv()
        if relay_sems is not None:
            for cp in _relay_copies(refs[n:2 * n], refs[4 * n:5 * n], refs[5 * n:6 * n]):
                cp.wait_send()
                cp.wait_recv()

    extra = [] if relay_sems is None else list(relay_sems)
    res = pl.pallas_call(
        body, name=name, out_shape=[pltpu.HBM(v.shape, v.dtype) for v in thru],
        in_specs=[HBM_SPEC] * (2 * n) + [SEM_SPEC] * (2 * n + len(extra)) + [pl.BlockSpec(memory_space=pl.ANY)],
        out_specs=[HBM_SPEC] * (2 * n), input_output_aliases={i: i for i in range(2 * n)}, compiler_params=SIDE_EFFECT)(
            *thru, *sems, *extra, after)
    return res[n:]


def _adam(parts, w, m, v, name):
    def body(p_ref, w_ref, m_ref, v_ref, g_ref, d_ref, nm_ref, nv_ref):
        g = _sum_parts(p_ref)
        g_ref[...] = g
        d_ref[...], nm_ref[...], nv_ref[...] = _adam_math(g, w_ref[...], m_ref[...], v_ref[...])

    return pl.pallas_call(
        body, name=name, out_shape=[_sds(w.shape)] * 4,
        compiler_params=pltpu.CompilerParams(vmem_limit_bytes=VMEM_LIMIT))(parts, w, m, v)


GATHER_GROUPS = (("w_in",), ("w_out", "xa_wkv", "xa_wq", "xa_wo"), ("ffn_up", "ffn_conv", "ffn_down"))
SCATTER_GROUPS = (("ffn_down", "ffn_up", "ffn_conv"), ("xa_wo", "xa_wq", "xa_wkv", "w_out"), (), ("w_in",))
BIG = tuple(n for grp in GATHER_GROUPS for n in grp)
BY_COLUMN = ("w_in", "ffn_up")
VECS = (("mix_norm", D), ("q_norm", HD), ("k_norm", HD), ("attn_sinks", 8), ("gmlp_v_norm", GW), ("attn_out_norm", AW),
        ("gmlp_out_norm", GW), ("xa_norm", D), ("mem_norm", D), ("xa_q_norm", XD), ("xa_k_norm", XD), ("ffn_norm", D))
LOSS_ROW = len(VECS)
BS_ROW = 16
VEC_ROWS = 24
SMALL = tuple(n for n, _ in VECS) + ("gmlp_bs", "gmlp_ws", "ffn_conv_b")


def _pack_small(raw):
    names = [n for n, _ in VECS] + ["gmlp_bs", "conv_sums", "loss"]

    def body(*refs):
        ins = dict(zip(names, refs))
        vec_ref, cb_ref = refs[len(names):]
        vec_ref[...] = jnp.zeros_like(vec_ref)
        for r, (n, w) in enumerate(VECS):
            vec_ref[r:r + 1, 0:w] = ins[n][0:1, 0:w]
        vec_ref[LOSS_ROW:LOSS_ROW + 1, 0:128] = ins["loss"][0:1, :]
        vec_ref[BS_ROW:BS_ROW + 8, 0:BLK] = ins["gmlp_bs"][...]
        for s in range(2):
            for d in range(NG):
                cb_ref[s, d] = ins["conv_sums"][s, d, 3:4, :]

    return pl.pallas_call(body, name="pack_small", out_shape=[_sds((VEC_ROWS, D)), _sds((2, NG, 1, SW))])(
        *[raw[n] for n in names])


def _adam_math(g, w, m, v):
    nm = B1 * m + (1.0 - B1) * g
    nv = B2 * v + (1.0 - B2) * (g * g)
    m_hat = nm / (1.0 - B1 ** STEP)
    v_hat = nv / (1.0 - B2 ** STEP)
    return -LR * (m_hat / (jnp.sqrt(v_hat) + AEPS) + WD * w), nm, nv


def _sum_parts(p_ref):
    g = p_ref[0].astype(F32)
    for j in range(1, NDEV):
        g = g + p_ref[j].astype(F32)
    return g


def _adam_small(parts_vec, parts_ws, parts_cb, w, m, v):
    def body(*refs):
        pv_ref, pws_ref, pcb_ref = refs[:3]
        ins = refs[3:3 + 3 * len(SMALL)]
        outs = refs[3 + 3 * len(SMALL):]
        gv = _sum_parts(pv_ref)
        outs[-1][...] = gv[LOSS_ROW:LOSS_ROW + 1, 0:128]
        for j, n in enumerate(SMALL):
            w_ref, m_ref, v_ref = ins[3 * j:3 * j + 3]
            o = outs[4 * j:4 * j + 4]
            if n == "gmlp_ws":
                g = _sum_parts(pws_ref)
            elif n == "ffn_conv_b":
                g = _sum_parts(pcb_ref)
            elif n == "gmlp_bs":
                g = gv[BS_ROW:BS_ROW + 8, 0:BLK]
            else:
                g = gv[j:j + 1, 0:VECS[j][1]]
            lead = n in ("gmlp_ws", "gmlp_bs")
            res = (g,) + _adam_math(g, w_ref[0] if lead else w_ref[...], m_ref[0] if lead else m_ref[...],
                                    v_ref[0] if lead else v_ref[...])
            for o_ref, val in zip(o, res):
                if lead:
                    o_ref[0] = val
                else:
                    o_ref[...] = val

    args = [parts_vec, parts_ws, parts_cb] + [d[n] for n in SMALL for d in (w, m, v)]
    res = pl.pallas_call(body, name="adam_small",
                         out_shape=[_sds(w[n].shape) for n in SMALL for _ in range(4)] + [_sds((1, 128))],
                         compiler_params=pltpu.CompilerParams(vmem_limit_bytes=VMEM_LIMIT))(*args)
    return {n: tuple(res[4 * j:4 * j + 4]) for j, n in enumerate(SMALL)}, res[-1][0, 0]


def kernel(x, mem, positions, mix_norm, w_in, q_norm, k_norm, attn_sinks, gmlp_v_norm, gmlp_ws, gmlp_bs, attn_out_norm, gmlp_out_norm, w_out, xa_norm, mem_norm, xa_wq, xa_wkv, xa_q_norm, xa_k_norm, xa_wo, ffn_norm, ffn_up, ffn_conv, ffn_conv_b, ffn_down, loss_target, m_mix_norm, m_w_in, m_q_norm, m_k_norm, m_attn_sinks, m_gmlp_v_norm, m_gmlp_ws, m_gmlp_bs, m_attn_out_norm, m_gmlp_out_norm, m_w_out, m_xa_norm, m_mem_norm, m_xa_wq, m_xa_wkv, m_xa_q_norm, m_xa_k_norm, m_xa_wo, m_ffn_norm, m_ffn_up, m_ffn_conv, m_ffn_conv_b, m_ffn_down, v_mix_norm, v_w_in, v_q_norm, v_k_norm, v_attn_sinks, v_gmlp_v_norm, v_gmlp_ws, v_gmlp_bs, v_attn_out_norm, v_gmlp_out_norm, v_w_out, v_xa_norm, v_mem_norm, v_xa_wq, v_xa_wkv, v_xa_q_norm, v_xa_k_norm, v_xa_wo, v_ffn_norm, v_ffn_up, v_ffn_conv, v_ffn_conv_b, v_ffn_down):
    names = ("mix_norm", "w_in", "q_norm", "k_norm", "attn_sinks", "gmlp_v_norm", "gmlp_ws", "gmlp_bs", "attn_out_norm",
             "gmlp_out_norm", "w_out", "xa_norm", "mem_norm", "xa_wq", "xa_wkv", "xa_q_norm", "xa_k_norm", "xa_wo",
             "ffn_norm", "ffn_up", "ffn_conv", "ffn_conv_b", "ffn_down")
    w = dict(zip(names, (mix_norm, w_in, q_norm, k_norm, attn_sinks, gmlp_v_norm, gmlp_ws, gmlp_bs, attn_out_norm,
                         gmlp_out_norm, w_out, xa_norm, mem_norm, xa_wq, xa_wkv, xa_q_norm, xa_k_norm, xa_wo, ffn_norm,
                         ffn_up, ffn_conv, ffn_conv_b, ffn_down)))
    m = dict(zip(names, (m_mix_norm, m_w_in, m_q_norm, m_k_norm, m_attn_sinks, m_gmlp_v_norm, m_gmlp_ws, m_gmlp_bs,
                         m_attn_out_norm, m_gmlp_out_norm, m_w_out, m_xa_norm, m_mem_norm, m_xa_wq, m_xa_wkv,
                         m_xa_q_norm, m_xa_k_norm, m_xa_wo, m_ffn_norm, m_ffn_up, m_ffn_conv, m_ffn_conv_b, m_ffn_down)))
    v = dict(zip(names, (v_mix_norm, v_w_in, v_q_norm, v_k_norm, v_attn_sinks, v_gmlp_v_norm, v_gmlp_ws, v_gmlp_bs,
                         v_attn_out_norm, v_gmlp_out_norm, v_w_out, v_xa_norm, v_mem_norm, v_xa_wq, v_xa_wkv,
                         v_xa_q_norm, v_xa_k_norm, v_xa_wo, v_ffn_norm, v_ffn_up, v_ffn_conv, v_ffn_conv_b, v_ffn_down)))
    t = x.shape[1]

    me = 4 * lax.axis_index("x") + 2 * lax.axis_index("y") + lax.axis_index("c")

    def rows(a, n):
        return jnp.swapaxes(a[0], 0, 1) if n in BY_COLUMN else a[0]

    mats = [n for n in BIG if n != "ffn_conv"]
    shard = dict(zip(mats, _cast_shards([rows(w[n], n) for n in mats])), ffn_conv=w["ffn_conv"][0])
    gathers, tokens = zip(*[_exchange_start([], [shard[n] for n in grp], me, "gather_start_%d" % i,
                                            CHIP_K if i == len(GATHER_GROUPS) - 1 else ALL_K)
                            for i, grp in enumerate(GATHER_GROUPS)])

    def fetch(i, after):
        after = tokens[0] + tokens[1] + tokens[2] if after is None else after
        state = gathers[i]
        if state[3] == CHIP_K:
            state = _exchange_relay(state, after, "gather_relay_%d" % i)
        got = dict(zip(GATHER_GROUPS[i], _exchange_wait(state, after, "gather_wait_%d" % i)))
        if "w_in" in got:
            got["w_in"] = got["w_in"].reshape(IN, D)
        for n in ("w_out", "xa_wq", "xa_wo"):
            if n in got:
                got[n] = got[n].reshape(D, D)
        if "ffn_down" in got:
            got["ffn_down"] = got["ffn_down"].reshape(NG, SW, D)
            got["ffn_conv"] = got["ffn_conv"].reshape(2, NG, 3, SW)
        return got

    scatters = []

    def ship(i, grads, for_all=()):
        by_dest = [grads[n].reshape((NDEV,) + shard[n].shape) for n in SCATTER_GROUPS[i]]
        state, token = _exchange_start(by_dest, for_all, me, "scatter_start_%d" % i)
        scatters.append(state)
        return token

    conv_b = {k: d["ffn_conv_b"].reshape(NDEV, 1, SW) for k, d in (("w", w), ("m", m), ("v", v))}
    p = {n: w[n] for n in SMALL[:-1]}
    p["gmlp_ws"], p["gmlp_bs"] = w["gmlp_ws"][0], w["gmlp_bs"][0]
    p["ffn_conv_b"] = conv_b["w"].reshape(2, NG, 1, SW)
    grad_x, g, raw = _local_step(x[0], mem[0], positions.reshape(t, 1), loss_target[0], p, fetch, ship)

    vec, cb = _pack_small(raw)
    after = ship(3, g, [vec, cb.reshape(NDEV, 1, SW)])
    res, rest = {}, []
    for i, grp in enumerate(SCATTER_GROUPS):
        got = _exchange_wait(scatters[i], after, "scatter_wait_%d" % i)
        rest += got[len(grp):]
        for n, parts in zip(grp, got):
            out = _adam(parts, rows(w[n], n), rows(m[n], n), rows(v[n], n), "adam_" + n)
            res[n] = [jnp.swapaxes(o, 0, 1) if n in BY_COLUMN else o for o in out]
            after = out[0]
    ws_parts, vec_parts, cb_parts = rest
    small = lambda d, k: {**{n: d[n] for n in SMALL[:-1]}, "ffn_conv_b": conv_b[k]}
    res_small, loss = _adam_small(vec_parts, ws_parts, cb_parts, small(w, "w"), small(m, "m"), small(v, "v"))
    res.update(res_small)

    outs = [loss, grad_x[None]]
    for j in range(4):
        outs += [res[n][j].reshape(w[n].shape) for n in names]
    return tuple(outs)
```

```python
import functools
import math

import jax
import jax.numpy as jnp
from jax import lax
from jax.experimental import pallas as pl
from jax.experimental.pallas import tpu as pltpu

F32 = jnp.float32
BF16 = jnp.bfloat16

D = 1024
HD = 64
AW = 512
KW = 128
GW = 512
IN = AW + 2 * KW + 2 * GW
BLK = 128
MEM = 256
XH = 4
XD = 256
FF = 2816
EPS = 1e-6
ROPE_THETA = 10000.0
NDEV = 8
LR, B1, B2, AEPS, WD, STEP = 0.001, 0.9, 0.999, 1e-08, 0.01, 10

TM = 512
WK = 2048
VMEM_LIMIT = 56 * 1024 * 1024
NEG = float(jnp.finfo(jnp.float32).min)
GELU_C0 = math.sqrt(2.0 / math.pi)
GELU_C1 = 0.044715


def _dot(a, b):
    return jnp.dot(a, b, preferred_element_type=F32)


def _dot_nt(a, b):
    return lax.dot_general(a, b, (((1,), (1,)), ((), ())), preferred_element_type=F32)


def _dot_tn(a, b):
    return lax.dot_general(a, b, (((0,), (0,)), ((), ())), preferred_element_type=F32)


def _rs(x):
    return lax.rsqrt(jnp.mean(x * x, axis=-1, keepdims=True) + EPS)


def _rms_bwd(dy, x, r, g):
    xh = x * r
    dxh = dy * g
    dx = r * (dxh - xh * jnp.mean(dxh * xh, axis=-1, keepdims=True))
    return dx, dy * xh


def _lane(shape):
    return lax.broadcasted_iota(jnp.int32, shape, len(shape) - 1)


def _gsum64(v, ones_ref):
    w = v.shape[-1]
    ones = ones_ref[0:w, 0:w]
    hi = v.astype(BF16)
    lo = (v - hi.astype(F32)).astype(BF16)
    return _dot(hi, ones) + _dot(lo, ones)


def _head_ones():
    i = jnp.arange(AW) // HD
    return (i[:, None] == i[None, :]).astype(BF16)


def _rs64(x, ones_ref):
    return lax.rsqrt(_gsum64(x * x, ones_ref) * (1.0 / HD) + EPS)


def _rms64_bwd(dy, x, r, g, ones_ref):
    xh = x * r
    dxh = dy * g
    dx = r * (dxh - xh * (_gsum64(dxh * xh, ones_ref) * (1.0 / HD)))
    return dx, dy * xh


def _rot_half(v):
    w = v.shape[-1]
    return jnp.where((_lane(v.shape) & 32) == 0, pltpu.roll(v, w - 32, 1), pltpu.roll(v, 32, 1))


def _rope(v, cos, sin_signed):
    return v * cos + _rot_half(v) * sin_signed


def _rope_bwd(dv, cos, sin_signed):
    return dv * cos + _rot_half(dv * sin_signed)


def _gelu(z):
    return 0.5 * z * (1.0 + jnp.tanh(GELU_C0 * (z + GELU_C1 * z * z * z)))


def _gelu_grad(z):
    t = jnp.tanh(GELU_C0 * (z + GELU_C1 * z * z * z))
    return 0.5 * (1.0 + t) + 0.5 * z * (1.0 - t * t) * (GELU_C0 * (1.0 + 3.0 * GELU_C1 * z * z))


def _colsum8(v):
    s = jnp.sum(v, axis=0, keepdims=True)
    row = lax.broadcasted_iota(jnp.int32, (8, v.shape[1]), 0)
    return jnp.where(row == 0, jnp.broadcast_to(s, (8, v.shape[1])), 0.0)


def _params(n_axes=1):
    return pltpu.CompilerParams(dimension_semantics=("arbitrary",) * n_axes, vmem_limit_bytes=VMEM_LIMIT)


def _rows(tm, w):
    return pl.BlockSpec((tm, w), lambda i: (i, 0))


def _const(shape):
    nd = len(shape)
    return pl.BlockSpec(shape, lambda *_: (0,) * nd)


def _sds(shape, dtype=F32):
    return jax.ShapeDtypeStruct(shape, dtype)


def _mm_tn(a, b, name):
    g = max(a.shape[0] if a.ndim == 3 else 1, b.shape[0] if b.ndim == 3 else 1)
    t, m = a.shape[-2:]
    n = b.shape[-1]

    def body(a_ref, b_ref, o_ref, acc_ref):
        i = pl.program_id(1)

        @pl.when(i == 0)
        def _():
            acc_ref[...] = jnp.zeros_like(acc_ref)

        acc_ref[...] += _dot_tn(a_ref[...].astype(BF16), b_ref[...].astype(BF16))

        @pl.when(i == pl.num_programs(1) - 1)
        def _():
            o_ref[...] = acc_ref[...].astype(BF16)

    tk = min(t, WK)

    def spec(v):
        w = v.shape[-1]
        if v.ndim == 3:
            return pl.BlockSpec((None, tk, w), lambda j, i: (j, i, 0))
        return pl.BlockSpec((tk, w), lambda j, i: (i, 0))

    return pl.pallas_call(
        body, name=name, grid=(g, t // tk), in_specs=[spec(a), spec(b)],
        out_specs=pl.BlockSpec((None, m, n), lambda j, i: (j, 0, 0)), out_shape=_sds((g, m, n), BF16),
        scratch_shapes=[pltpu.VMEM((m, n), F32)], compiler_params=_params(2))(a, b)


def _rope_tables(pos, inv_freq):
    t = pos.shape[0]

    def body(pos_ref, f_ref, cos_ref, sin_ref):
        ang = pos_ref[...].astype(F32) * f_ref[...]
        sign = jnp.where((_lane(ang.shape) & 32) == 0, -1.0, 1.0)
        cos_ref[...] = jnp.cos(ang)
        sin_ref[...] = jnp.sin(ang) * sign

    return pl.pallas_call(
        body, name="rope_tables", grid=(t // TM,),
        in_specs=[_rows(TM, 1), _const((1, 128))], out_specs=[_rows(TM, 128), _rows(TM, 128)],
        out_shape=[_sds((t, 128)), _sds((t, 128))], compiler_params=_params())(pos, inv_freq)


def _mixer_in_fwd(x, mix_norm, w_in, qn, kn, gvw, cos, sin):
    t = x.shape[0]

    def body(x_ref, g_ref, w_ref, qn_ref, kn_ref, gvw_ref, cos_ref, sin_ref, ones_ref,
             h_ref, qk_ref, gz_ref, q_ref, k_ref, v_ref, gu_ref, gvn_ref):
        x = x_ref[...]
        h = (x * _rs(x) * g_ref[...]).astype(BF16)
        h_ref[...] = h
        proj = _dot_nt(h, w_ref[...])
        qk = proj[:, :AW + KW]
        qk_ref[...] = qk
        gz = proj[:, AW + 2 * KW:]
        gz_ref[...] = gz
        cos2, sin2 = cos_ref[...], sin_ref[...]
        q = qk[:, :AW]
        q = q * _rs64(q, ones_ref) * qn_ref[...]
        q_ref[...] = _rope(q, jnp.tile(cos2, (1, 4)), jnp.tile(sin2, (1, 4))).astype(BF16)
        k = qk[:, AW:]
        k = k * _rs64(k, ones_ref) * kn_ref[...]
        k_ref[...] = _rope(k, cos2, sin2).astype(BF16)
        v_ref[...] = proj[:, AW + KW:AW + 2 * KW].astype(BF16)
        gu_ref[...] = _gelu(gz[:, :GW])
        gv = _gelu(gz[:, GW:])
        gvn_ref[...] = (gv * _rs(gv) * gvw_ref[...]).astype(BF16)

    return pl.pallas_call(
        body, name="mixer_in_fwd", grid=(t // TM,),
        in_specs=[_rows(TM, D), _const((1, D)), _const((IN, D)), _const((1, AW)), _const((1, KW)),
                  _const((1, GW)), _rows(TM, 128), _rows(TM, 128), _const((AW, AW))],
        out_specs=[_rows(TM, D), _rows(TM, AW + KW), _rows(TM, 2 * GW), _rows(TM, AW), _rows(TM, KW),
                   _rows(TM, KW), _rows(TM, GW), _rows(TM, GW)],
        out_shape=[_sds((t, D), BF16), _sds((t, AW + KW)), _sds((t, 2 * GW)), _sds((t, AW), BF16),
                   _sds((t, KW), BF16), _sds((t, KW), BF16), _sds((t, GW)), _sds((t, GW), BF16)],
        compiler_params=_params())(x, mix_norm, w_in, qn, kn, gvw, cos, sin, _head_ones())


def _dup_half(kk, g):
    lane = _lane(kk.shape)
    other = pltpu.roll(kk, 64, 1)
    keep = (lane < 64) if g == 0 else (lane >= 64)
    return jnp.where(keep, kk, other).astype(BF16)


def _swa_mask(first_block):
    qi = lax.broadcasted_iota(jnp.int32, (4 * BLK, 2 * BLK), 0) & (BLK - 1)
    kj = lax.broadcasted_iota(jnp.int32, (4 * BLK, 2 * BLK), 1)
    diff = qi + BLK - kj
    band = (diff >= 0) & (diff < BLK)
    return band & (jnp.logical_not(first_block) | (kj >= BLK))


def _stack_heads(a2, b2):
    lo = _lane(a2.shape) < 64
    z = jnp.zeros_like(a2)
    return jnp.concatenate([jnp.where(lo, a2, z), jnp.where(lo, z, a2), jnp.where(lo, b2, z), jnp.where(lo, z, b2)], axis=0)


def _unstack_heads(o):
    lo = _lane((BLK, 128)) < 64
    return jnp.where(lo, o[0:BLK], o[BLK:2 * BLK]), jnp.where(lo, o[2 * BLK:3 * BLK], o[3 * BLK:4 * BLK])


def _sink_col(sink_ref, g):
    row = lax.broadcasted_iota(jnp.int32, (4 * BLK, 1), 0)
    s = [sink_ref[0, 4 * g + j] for j in range(4)]
    return jnp.where(row < BLK, s[0], jnp.where(row < 2 * BLK, s[1], jnp.where(row < 3 * BLK, s[2], s[3])))


def _swa_probs(qs, kd, mask, sink):
    s = _dot_nt(qs, kd) * (1.0 / math.sqrt(HD))
    s = jnp.where(mask, s, NEG)
    m = jnp.maximum(jnp.max(s, axis=-1, keepdims=True), sink)
    p = jnp.exp(s - m)
    ps = jnp.exp(sink - m)
    inv = 1.0 / (jnp.sum(p, axis=-1, keepdims=True) + ps)
    return p * inv, ps * inv


SB = 4
SB_BWD = 2


def _swa_fwd(q, k, v, sinks):
    t = q.shape[0]
    ts = min(t, SB * BLK)

    def body(sink_ref, q_ref, kc_ref, kp_ref, vc_ref, vp_ref, o_ref):
        i = pl.program_id(0)
        kk = jnp.concatenate([kp_ref[...], kc_ref[...]], axis=0).astype(F32)
        vv = jnp.concatenate([vp_ref[...], vc_ref[...]], axis=0).astype(F32)
        for b in range(ts // BLK):
            r = slice(b * BLK, (b + 1) * BLK)
            kb, vb = kk[b * BLK:(b + 2) * BLK], vv[b * BLK:(b + 2) * BLK]
            mask = _swa_mask(i == 0) if b == 0 else _swa_mask(False)
            for g in range(2):
                qs = _stack_heads(q_ref[r, 256 * g:256 * g + 128], q_ref[r, 256 * g + 128:256 * g + 256])
                pn, _ = _swa_probs(qs, _dup_half(kb, g), mask, _sink_col(sink_ref, g))
                oa, ob = _unstack_heads(_dot(pn.astype(BF16), _dup_half(vb, g)))
                o_ref[r, 256 * g:256 * g + 128] = oa
                o_ref[r, 256 * g + 128:256 * g + 256] = ob

    cur = lambda i: (i, 0)
    prev = lambda i: (jnp.maximum(i * (ts // BLK) - 1, 0), 0)
    return pl.pallas_call(
        body, name="swa_fwd", grid=(t // ts,),
        in_specs=[pl.BlockSpec(memory_space=pltpu.SMEM), pl.BlockSpec((ts, AW), cur),
                  pl.BlockSpec((ts, KW), cur), pl.BlockSpec((BLK, KW), prev),
                  pl.BlockSpec((ts, KW), cur), pl.BlockSpec((BLK, KW), prev)],
        out_specs=pl.BlockSpec((ts, AW), cur), out_shape=_sds((t, AW)),
        compiler_params=_params())(sinks, q, k, k, v, v)


def _causal_bf16(w_ref, h, transposed):
    r = lax.broadcasted_iota(jnp.int32, (BLK, BLK), 0)
    c = lax.broadcasted_iota(jnp.int32, (BLK, BLK), 1)
    keep = (r <= c) if transposed else (c <= r)
    return jnp.where(keep, w_ref[h], 0.0).astype(BF16)


def _gmlp_mix(w_ref, xin, transposed):
    lo = _lane((BLK, 128)) < 64
    wm = [_causal_bf16(w_ref, h, transposed) for h in range(8)]
    rows = []
    for c in range(xin.shape[0] // BLK):
        cols = []
        for j in range(4):
            xs = xin[c * BLK:(c + 1) * BLK, 128 * j:128 * (j + 1)]
            cols.append(jnp.where(lo, _dot(wm[2 * j], xs), _dot(wm[2 * j + 1], xs)))
        rows.append(jnp.concatenate(cols, axis=1))
    return jnp.concatenate(rows, axis=0)


def _mixer_out_fwd(attn, gvn, gu, ws, bfull, x, w_out, aon, gon, xan):
    t = x.shape[0]

    def body(a_ref, v_ref, gu_ref, ws_ref, b_ref, x_ref, w_ref, aon_ref, gon_ref, xan_ref, gm_ref, y_ref, x1_ref, h2_ref):
        a = a_ref[...]
        g = gu_ref[...] * (_gmlp_mix(ws_ref, v_ref[...], False) + jnp.tile(b_ref[...], (TM // BLK, 1)))
        gm_ref[...] = g
        y = jnp.concatenate([a * _rs(a) * aon_ref[...], g * _rs(g) * gon_ref[...]], axis=1).astype(BF16)
        y_ref[...] = y
        x1 = x_ref[...] + _dot(y, w_ref[...])
        x1_ref[...] = x1
        h2_ref[...] = (x1 * _rs(x1) * xan_ref[...]).astype(BF16)

    return pl.pallas_call(
        body, name="mixer_out_fwd", grid=(t // TM,),
        in_specs=[_rows(TM, AW), _rows(TM, GW), _rows(TM, GW), _const((8, BLK, BLK)), _const((BLK, GW)), _rows(TM, D),
                  _const((D, D)), _const((1, AW)), _const((1, GW)), _const((1, D))],
        out_specs=[_rows(TM, GW), _rows(TM, D), _rows(TM, D), _rows(TM, D)],
        out_shape=[_sds((t, GW)), _sds((t, D), BF16), _sds((t, D)), _sds((t, D), BF16)],
        compiler_params=_params())(attn, gvn, gu, ws, bfull, x, w_out, aon, gon, xan)


def _mem_kv_fwd(mem, mem_norm, wkv, kn4):
    def body(m_ref, g_ref, w_ref, kn_ref, mh_ref, kpre_ref, k_ref, v_ref):
        m = m_ref[...]
        mh = (m * _rs(m) * g_ref[...]).astype(BF16)
        mh_ref[...] = mh
        for h in range(XH):
            sl = slice(XD * h, XD * (h + 1))
            kh = _dot(mh, w_ref[h])
            kpre_ref[:, sl] = kh
            k_ref[:, sl] = (kh * _rs(kh) * kn_ref[:, sl]).astype(BF16)
            v_ref[:, sl] = _dot(mh, w_ref[XH + h]).astype(BF16)

    return pl.pallas_call(
        body, name="mem_kv_fwd",
        out_shape=[_sds((MEM, D), BF16), _sds((MEM, D)), _sds((MEM, D), BF16), _sds((MEM, D), BF16)],
        compiler_params=pltpu.CompilerParams(vmem_limit_bytes=VMEM_LIMIT))(mem, mem_norm, wkv, kn4)


def _xattn_probs(qpre_h, qn_h, k_h):
    rq = _rs(qpre_h)
    q2 = (qpre_h * rq * qn_h).astype(BF16)
    s = _dot_nt(q2, k_h) * (1.0 / math.sqrt(XD))
    p = jnp.exp(s - jnp.max(s, axis=-1, keepdims=True))
    return p * (1.0 / jnp.sum(p, axis=-1, keepdims=True)), q2, rq


def _xattn_fwd(h2, x1, wq, qn4, k2, v2, wo, ffn_norm):
    t = x1.shape[0]

    def body(h_ref, x_ref, wq_ref, qn_ref, k_ref, v_ref, wo_ref, fn_ref, qpre_ref, o_ref, x2_ref, h3_ref):
        qpre = _dot(h_ref[...], wq_ref[...])
        qpre_ref[...] = qpre
        outs = []
        for h in range(XH):
            sl = slice(XD * h, XD * (h + 1))
            pn, _, _ = _xattn_probs(qpre[:, sl], qn_ref[:, sl], k_ref[:, sl])
            outs.append(_dot(pn.astype(BF16), v_ref[:, sl]))
        o = jnp.concatenate(outs, axis=1).astype(BF16)
        o_ref[...] = o
        x2 = x_ref[...] + _dot(o, wo_ref[...])
        x2_ref[...] = x2
        h3_ref[...] = (x2 * _rs(x2) * fn_ref[...]).astype(BF16)

    return pl.pallas_call(
        body, name="xattn_fwd", grid=(t // TM,),
        in_specs=[_rows(TM, D), _rows(TM, D), _const((D, D)), _const((1, D)), _const((MEM, D)), _const((MEM, D)),
                  _const((D, D)), _const((1, D))],
        out_specs=[_rows(TM, D)] * 4,
        out_shape=[_sds((t, D)), _sds((t, D), BF16), _sds((t, D)), _sds((t, D), BF16)],
        compiler_params=_params())(h2, x1, wq, qn4, k2, v2, wo, ffn_norm)


SW = 704
NG = FF // SW
FM = 256
HALO = 16


def _resident(shape):
    nd = len(shape)
    return pl.BlockSpec(shape, lambda *_: (0,) * nd, pipeline_mode=pl.Buffered(1))


def _halo_before(i):
    return jnp.maximum(i * (FM // HALO) - 1, 0)


def _conv(e, w):
    return w[2:3, :] * e + pltpu.roll(w[1:2, :] * e + pltpu.roll(w[0:1, :] * e, 1, 0), 1, 0)


def _conv_t(dc, w):
    n = dc.shape[0]
    return w[2:3, :] * dc + pltpu.roll(w[1:2, :] * dc + pltpu.roll(w[0:1, :] * dc, n - 1, 0), n - 1, 0)


def _ffn_fwd(h3, x2, target, up, conv, conv_b, down):
    t = x2.shape[0]

    def body(h_ref, hp_ref, x_ref, t_ref, up_ref, w_ref, b_ref, dn_ref, a_ref, u_ref, gs_ref, dy_ref, loss_ref, acc_ref):
        i = pl.program_id(0)

        @pl.when(i == 0)
        def _():
            acc_ref[...] = jnp.zeros_like(acc_ref)

        before = jnp.where(i > 0, hp_ref[...], jnp.zeros_like(hp_ref))
        he = jnp.concatenate([before, h_ref[...]], axis=0)
        err = x_ref[...] - t_ref[...]
        for d in range(NG):
            c = []
            for s in range(2):
                a = _dot_nt(he, up_ref[s * NG + d])
                a_ref[s * NG + d] = a[HALO:].astype(BF16)
                c.append(_conv(a, w_ref[s, d])[HALO:] + b_ref[s, d])
            gl, gg = _gelu_and_grad(c[0])
            gs_ref[d] = gl.astype(BF16)
            gs_ref[NG + d] = (gg * c[1]).astype(BF16)
            u = (gl * c[1]).astype(BF16)
            u_ref[d] = u
            err = err + _dot(u, dn_ref[d])
        dy_ref[...] = err * (1.0 / D)
        acc_ref[...] += jnp.sum(err * err, axis=0, keepdims=True)

        @pl.when(i == pl.num_programs(0) - 1)
        def _():
            loss_ref[...] = jnp.full((8, 128), 0.5 / D, F32) * jnp.sum(acc_ref[...])

    return pl.pallas_call(
        body, name="ffn_fwd", grid=(t // FM,),
        in_specs=[_rows(FM, D), pl.BlockSpec((HALO, D), lambda i: (_halo_before(i), 0)), _rows(FM, D), _rows(FM, D),
                  _resident((NDEV, SW, D)), _resident((2, NG, 3, SW)), _resident((2, NG, 1, SW)), _resident((NG, SW, D))],
        out_specs=[pl.BlockSpec((NDEV, FM, SW), lambda i: (0, i, 0)), pl.BlockSpec((NG, FM, SW), lambda i: (0, i, 0)),
                   pl.BlockSpec((NDEV, FM, SW), lambda i: (0, i, 0)), _rows(FM, D), _const((8, 128))],
        out_shape=[_sds((NDEV, t, SW), BF16), _sds((NG, t, SW), BF16), _sds((NDEV, t, SW), BF16), _sds((t, D)),
                   _sds((8, 128))],
        scratch_shapes=[pltpu.VMEM((1, D), F32)], compiler_params=_params())(h3, h3, x2, target, up, conv, conv_b, down)


def _gelu_and_grad(z):
    z2 = z * z
    t = jnp.tanh(GELU_C0 * (z + GELU_C1 * z * z2))
    phi = 0.5 * (1.0 + t)
    return z * phi, phi + z * (1.0 - t * t) * (0.5 * GELU_C0 + (1.5 * GELU_C0 * GELU_C1) * z2)


def _ffn_bwd(dy, a, gs, x2, up, conv, down, ffn_norm):
    t = x2.shape[0]
    nt = t // FM
    n = FM + HALO

    def body(dy_ref, dyn_ref, a_ref, gs_ref, gsn_ref, x_ref, up_ref, w_ref, dn_ref, g_ref,
             dx_ref, da_ref, s_ref, dfn_ref):
        i = pl.program_id(0)

        @pl.when(i == 0)
        def _():
            s_ref[...] = jnp.zeros_like(s_ref)
            dfn_ref[...] = jnp.zeros_like(dfn_ref)

        last = i == nt - 1
        dy = dy_ref[...]
        dye = jnp.concatenate([dy, jnp.where(last, 0.0, dyn_ref[...])], axis=0).astype(BF16)
        dh = jnp.zeros((FM, D), F32)
        row = lax.broadcasted_iota(jnp.int32, (8, SW), 0)
        for d in range(NG):
            du = _dot_nt(dye, dn_ref[d])
            for s in range(2):
                j = s * NG + d
                k = NG + d if s == 0 else d
                dc = du * jnp.concatenate([gs_ref[k], gsn_ref[k]], axis=0).astype(F32)
                w = w_ref[s, d]
                tile = a_ref[j].astype(F32)
                d1 = pltpu.roll(dc, n - 1, 0)
                d2 = pltpu.roll(d1, n - 1, 0)
                da = (w[2:3, :] * dc + w[1:2, :] * d1 + w[0:1, :] * d2)[0:FM].astype(BF16)
                da_ref[j] = da
                dh = dh + _dot(da, up_ref[j])
                sums = [jnp.sum(v[0:FM] * tile, axis=0, keepdims=True) for v in (d2, d1, dc)]
                sums.append(jnp.sum(dc[0:FM], axis=0, keepdims=True))
                upd = jnp.zeros((8, SW), F32)
                for r, v in enumerate(sums):
                    upd = jnp.where(row == r, jnp.broadcast_to(v, (8, SW)), upd)
                s_ref[s, d] += upd
        x = x_ref[...]
        dx, dg = _rms_bwd(dh, x, _rs(x), g_ref[...])
        dx_ref[...] = dy + dx
        dfn_ref[...] += _colsum8(dg)

    last_halo = t // HALO - 1
    after = lambda i: jnp.minimum((i + 1) * (FM // HALO), last_halo)
    return pl.pallas_call(
        body, name="ffn_bwd", grid=(nt,),
        in_specs=[_rows(FM, D), pl.BlockSpec((HALO, D), lambda i: (after(i), 0)),
                  pl.BlockSpec((NDEV, FM, SW), lambda i: (0, i, 0)),
                  pl.BlockSpec((NDEV, FM, SW), lambda i: (0, i, 0)),
                  pl.BlockSpec((NDEV, HALO, SW), lambda i: (0, after(i), 0)),
                  _rows(FM, D), _resident((NDEV, SW, D)), _resident((2, NG, 3, SW)), _resident((NG, SW, D)), _const((1, D))],
        out_specs=[_rows(FM, D), pl.BlockSpec((NDEV, FM, SW), lambda i: (0, i, 0)), _const((2, NG, 8, SW)), _const((8, D))],
        out_shape=[_sds((t, D)), _sds((NDEV, t, SW), BF16), _sds((2, NG, 8, SW)), _sds((8, D))],
        compiler_params=_params())(dy, dy, a, gs, gs, x2, up, conv, down, ffn_norm)


BT = 512


def _xattn_bwd(dx2, x1, qpre, k2, v2, wq, wo, qn4, xan):
    t = x1.shape[0]

    def body(dx2_ref, x1_ref, qpre_ref, k_ref, v_ref, wq_ref, wo_ref, qn_ref, xan_ref,
             dx1_ref, dqpre_ref, dk_ref, dv_ref, dqn_ref, dxan_ref):
        @pl.when(pl.program_id(0) == 0)
        def _():
            for r in (dk_ref, dv_ref, dqn_ref, dxan_ref):
                r[...] = jnp.zeros_like(r)

        dx2 = dx2_ref[...]
        do = _dot_nt(dx2.astype(BF16), wo_ref[...])
        dqs = []
        for h in range(XH):
            sl = slice(XD * h, XD * (h + 1))
            qpre_h = qpre_ref[:, sl]
            pn, q2, rq = _xattn_probs(qpre_h, qn_ref[:, sl], k_ref[:, sl])
            do_h = do[:, sl].astype(BF16)
            dp = _dot_nt(do_h, v_ref[:, sl])
            ds = (pn * (dp - jnp.sum(pn * dp, axis=-1, keepdims=True)) * (1.0 / math.sqrt(XD))).astype(BF16)
            dq2 = _dot(ds, k_ref[:, sl])
            dk_ref[:, sl] += _dot_tn(ds, q2)
            dv_ref[:, sl] += _dot_tn(pn.astype(BF16), do_h)
            dqh, dg = _rms_bwd(dq2, qpre_h, rq, qn_ref[:, sl])
            dqn_ref[...] += _colsum8(dg)
            dqs.append(dqh)
        dqpre = jnp.concatenate(dqs, axis=1).astype(BF16)
        dqpre_ref[...] = dqpre
        dh2 = _dot_nt(dqpre, wq_ref[...])
        x1 = x1_ref[...]
        dx, dg = _rms_bwd(dh2, x1, _rs(x1), xan_ref[...])
        dx1_ref[...] = dx2 + dx
        dxan_ref[...] += _colsum8(dg)

    return pl.pallas_call(
        body, name="xattn_bwd", grid=(t // BT,),
        in_specs=[_rows(BT, D), _rows(BT, D), _rows(BT, D), _const((MEM, D)), _const((MEM, D)), _const((D, D)),
                  _const((D, D)), _const((1, D)), _const((1, D))],
        out_specs=[_rows(BT, D), _rows(BT, D), _const((MEM, D)), _const((MEM, D)), _const((8, XD)), _const((8, D))],
        out_shape=[_sds((t, D)), _sds((t, D), BF16), _sds((MEM, D)), _sds((MEM, D)), _sds((8, XD)), _sds((8, D))],
        compiler_params=_params())(dx2, x1, qpre, k2, v2, wq, wo, qn4, xan)


def _mem_kv_bwd(mem, mh, kpre, dk2, dv2, wkv, kn4, mem_norm):
    def body(m_ref, mh_ref, kpre_ref, dk_ref, dv_ref, w_ref, kn_ref, g_ref, dw_ref, dkn_ref, dmn_ref):
        dkn = jnp.zeros((8, XD), F32)
        dm = jnp.zeros((MEM, D), F32)
        mh = mh_ref[...]
        for h in range(XH):
            sl = slice(XD * h, XD * (h + 1))
            kh = kpre_ref[:, sl]
            dkh, dg = _rms_bwd(dk_ref[:, sl], kh, _rs(kh), kn_ref[:, sl])
            dkn = dkn + _colsum8(dg)
            dkh = dkh.astype(BF16)
            dvh = dv_ref[:, sl].astype(BF16)
            dw_ref[h] = _dot_tn(mh, dkh).astype(BF16)
            dw_ref[XH + h] = _dot_tn(mh, dvh).astype(BF16)
            dm = dm + _dot_nt(dkh, w_ref[h]) + _dot_nt(dvh, w_ref[XH + h])
        dkn_ref[...] = dkn
        m = m_ref[...]
        _, dg = _rms_bwd(dm, m, _rs(m), g_ref[...])
        dmn_ref[...] = _colsum8(dg)

    return pl.pallas_call(
        body, name="mem_kv_bwd", out_shape=[_sds((2 * XH, D, XD), BF16), _sds((8, XD)), _sds((8, D))],
        compiler_params=pltpu.CompilerParams(vmem_limit_bytes=VMEM_LIMIT))(mem, mh, kpre, dk2, dv2, wkv, kn4, mem_norm)


def _mixer_out_bwd(dx1, attn, gm, w_out, aon, gon, gu, gvn, gz, ws, wst, bfull, gvw):
    t = dx1.shape[0]
    nc = TM // BLK

    def body(dx_ref, a_ref, g_ref, wo_ref, aon_ref, gon_ref, gu_ref, x_ref, gz_ref, w_ref, wt_ref, b_ref, gvw_ref,
             da_ref, dan_ref, dgn_ref, dgz_ref, dw_ref, db_ref, dgvw_ref, dbacc_ref):
        @pl.when(pl.program_id(0) == 0)
        def _():
            for r in (dan_ref, dgn_ref, dw_ref, dbacc_ref, dgvw_ref):
                r[...] = jnp.zeros_like(r)

        dy = _dot_nt(dx_ref[...].astype(BF16), wo_ref[...])
        a, g = a_ref[...], g_ref[...]
        da, dna = _rms_bwd(dy[:, :AW], a, _rs(a), aon_ref[...])
        dgm, dng = _rms_bwd(dy[:, AW:], g, _rs(g), gon_ref[...])
        da_ref[...] = da
        dan_ref[...] += _colsum8(dna)
        dgn_ref[...] += _colsum8(dng)

        xin = x_ref[...]
        mixed = _gmlp_mix(w_ref, xin, False) + jnp.tile(b_ref[...], (nc, 1))
        dgu = dgm * mixed
        dmixed = dgm * gu_ref[...]
        lo = _lane((BLK, 128)) < 64
        dbias = jnp.zeros((BLK, GW), F32)
        for c in range(nc):
            dmc = dmixed[c * BLK:(c + 1) * BLK]
            dbias = dbias + dmc
            for j in range(4):
                dm2 = dmc[:, 128 * j:128 * (j + 1)]
                xs = xin[c * BLK:(c + 1) * BLK, 128 * j:128 * (j + 1)]
                z = jnp.zeros_like(dm2)
                dw_ref[2 * j] += _dot_nt(jnp.where(lo, dm2, z).astype(BF16), xs)
                dw_ref[2 * j + 1] += _dot_nt(jnp.where(lo, z, dm2).astype(BF16), xs)
        dbacc_ref[...] += dbias
        dgvn = _gmlp_mix(wt_ref, dmixed.astype(BF16), True)
        gz_u, gz_v = gz_ref[:, :GW], gz_ref[:, GW:]
        gv = _gelu(gz_v)
        dgv, dg = _rms_bwd(dgvn, gv, _rs(gv), gvw_ref[...])
        dgvw_ref[...] += _colsum8(dg)
        dgz_ref[:, :GW] = (dgu * _gelu_grad(gz_u)).astype(BF16)
        dgz_ref[:, GW:] = (dgv * _gelu_grad(gz_v)).astype(BF16)

        @pl.when(pl.program_id(0) == pl.num_programs(0) - 1)
        def _():
            s = dbacc_ref[...]
            sel = (lax.broadcasted_iota(jnp.int32, (8, GW), 1) // HD
                   == lax.broadcasted_iota(jnp.int32, (8, GW), 0)).astype(BF16)
            hi = s.astype(BF16)
            r1 = s - hi.astype(F32)
            mid = r1.astype(BF16)
            lo = (r1 - mid.astype(F32)).astype(BF16)
            db_ref[...] = _dot_nt(sel, hi) + _dot_nt(sel, mid) + _dot_nt(sel, lo)
            r = lax.broadcasted_iota(jnp.int32, (BLK, BLK), 0)
            c = lax.broadcasted_iota(jnp.int32, (BLK, BLK), 1)
            for h in range(8):
                dw_ref[h] = jnp.where(c <= r, dw_ref[h], 0.0)

    return pl.pallas_call(
        body, name="mixer_out_bwd", grid=(t // TM,),
        in_specs=[_rows(TM, D), _rows(TM, AW), _rows(TM, GW), _const((D, D)), _const((1, AW)), _const((1, GW)),
                  _rows(TM, GW), _rows(TM, GW), _rows(TM, 2 * GW), _const((8, BLK, BLK)), _const((8, BLK, BLK)),
                  _const((BLK, GW)), _const((1, GW))],
        out_specs=[_rows(TM, AW), _const((8, AW)), _const((8, GW)), _rows(TM, 2 * GW), _const((8, BLK, BLK)),
                   _const((8, BLK)), _const((8, GW))],
        out_shape=[_sds((t, AW)), _sds((8, AW)), _sds((8, GW)), _sds((t, 2 * GW), BF16), _sds((8, BLK, BLK)),
                   _sds((8, BLK)), _sds((8, GW))],
        scratch_shapes=[pltpu.VMEM((BLK, GW), F32)],
        compiler_params=_params())(dx1, attn, gm, w_out, aon, gon, gu, gvn, gz, ws, wst, bfull, gvw)


def _fold_half(v):
    return v + pltpu.roll(v, 64, 1)


def _swa_bwd(q, k, v, dattn, sinks):
    t = q.shape[0]
    nb = t // BLK
    ts = min(t, SB_BWD * BLK)
    sb = ts // BLK
    nt = t // ts

    def body(sink_ref, q_ref, kc_ref, kp_ref, vc_ref, vp_ref, do_ref, dq_ref, dk_ref, dv_ref, ds_ref,
             ck_ref, cv_ref, sacc_ref):
        i = pl.program_id(0)

        @pl.when(i == 0)
        def _():
            ck_ref[...] = jnp.zeros_like(ck_ref)
            cv_ref[...] = jnp.zeros_like(cv_ref)
            sacc_ref[...] = jnp.zeros_like(sacc_ref)

        @pl.when(i < nt)
        def _():
            kk = jnp.concatenate([kp_ref[...], kc_ref[...]], axis=0).astype(F32)
            vv = jnp.concatenate([vp_ref[...], vc_ref[...]], axis=0).astype(F32)
            lo256 = _lane((2 * BLK, 128)) < 64
            acc_k = [jnp.zeros((BLK, 128), F32) for _ in range(sb + 1)]
            acc_v = [jnp.zeros((BLK, 128), F32) for _ in range(sb + 1)]
            for b in range(sb):
                r = slice(b * BLK, (b + 1) * BLK)
                kb, vb = kk[b * BLK:(b + 2) * BLK], vv[b * BLK:(b + 2) * BLK]
                mask = _swa_mask(i == 0) if b == 0 else _swa_mask(False)
                dkk = jnp.zeros((2 * BLK, 128), F32)
                dvv = jnp.zeros((2 * BLK, 128), F32)
                for g in range(2):
                    qs = _stack_heads(q_ref[r, 256 * g:256 * g + 128], q_ref[r, 256 * g + 128:256 * g + 256])
                    dos = _stack_heads(do_ref[r, 256 * g:256 * g + 128],
                                       do_ref[r, 256 * g + 128:256 * g + 256]).astype(BF16)
                    kd = _dup_half(kb, g)
                    pn, psn = _swa_probs(qs, kd, mask, _sink_col(sink_ref, g))
                    dp = _dot_nt(dos, _dup_half(vb, g))
                    dd = jnp.sum(pn * dp, axis=-1, keepdims=True)
                    ds = (pn * (dp - dd) * (1.0 / math.sqrt(HD))).astype(BF16)
                    sacc_ref[g] += jnp.broadcast_to(-psn * dd, (4 * BLK, 128))
                    dqa, dqb = _unstack_heads(_dot(ds, kd))
                    dq_ref[r, 256 * g:256 * g + 128] = dqa
                    dq_ref[r, 256 * g + 128:256 * g + 256] = dqb
                    dkg = _fold_half(_dot_tn(ds, qs))
                    dvg = _fold_half(_dot_tn(pn.astype(BF16), dos))
                    keep = lo256 if g == 0 else jnp.logical_not(lo256)
                    dkk = jnp.where(keep, dkg, dkk)
                    dvv = jnp.where(keep, dvg, dvv)
                acc_k[b], acc_k[b + 1] = acc_k[b] + dkk[0:BLK], acc_k[b + 1] + dkk[BLK:]
                acc_v[b], acc_v[b + 1] = acc_v[b] + dvv[0:BLK], acc_v[b + 1] + dvv[BLK:]
            for out_ref, c_ref, acc in ((dk_ref, ck_ref, acc_k), (dv_ref, cv_ref, acc_v)):
                if sb > 1:
                    out_ref[0:ts - BLK] = c_ref[0:ts - BLK]
                out_ref[ts - BLK:ts] = c_ref[ts - BLK:ts] + acc[0]
                for b in range(sb):
                    c_ref[b * BLK:(b + 1) * BLK] = acc[b + 1]

        @pl.when(i == nt)
        def _():
            dk_ref[...] = ck_ref[...]
            dv_ref[...] = cv_ref[...]
            lane = _lane((8, 128))
            acc = jnp.zeros((8, 128), F32)
            for g in range(2):
                for j in range(4):
                    val = jnp.sum(sacc_ref[g, j * BLK:(j + 1) * BLK, :], axis=0, keepdims=True)
                    acc = jnp.where(lane == 4 * g + j, jnp.broadcast_to(val, (8, 128)), acc)
            ds_ref[...] = acc

    cur = lambda i: (jnp.minimum(i, nt - 1), 0)
    before = lambda i: (jnp.clip(i * sb - 1, 0, nb - 1), 0)
    done = lambda i: (jnp.clip(i - 1, 0, nt - 1), 0)
    return pl.pallas_call(
        body, name="swa_bwd", grid=(nt + 1,),
        in_specs=[pl.BlockSpec(memory_space=pltpu.SMEM), pl.BlockSpec((ts, AW), cur),
                  pl.BlockSpec((ts, KW), cur), pl.BlockSpec((BLK, KW), before),
                  pl.BlockSpec((ts, KW), cur), pl.BlockSpec((BLK, KW), before), pl.BlockSpec((ts, AW), cur)],
        out_specs=[pl.BlockSpec((ts, AW), cur), pl.BlockSpec((ts, KW), done), pl.BlockSpec((ts, KW), done),
                   _const((8, 128))],
        out_shape=[_sds((t, AW)), _sds((t, KW)), _sds((t, KW)), _sds((8, 128))],
        scratch_shapes=[pltpu.VMEM((ts, KW), F32), pltpu.VMEM((ts, KW), F32), pltpu.VMEM((2, 4 * BLK, 128), F32)],
        compiler_params=_params())(sinks, q, k, k, v, v, dattn)


def _mixer_in_bwd(dq, dk, dv, dgz, qk, cos, sin, x, dx1, w_in, mix_norm, qn, kn):
    t = x.shape[0]

    def body(dq_ref, dk_ref, dv_ref, dgz_ref, qk_ref, cos_ref, sin_ref, x_ref, dx1_ref, w_ref, g_ref, qn_ref, kn_ref,
             ones_ref, gx_ref, dproj_ref, dmn_ref, dqn_ref, dkn_ref, qacc_ref, kacc_ref):
        i = pl.program_id(0)

        @pl.when(i == 0)
        def _():
            dmn_ref[...] = jnp.zeros_like(dmn_ref)
            qacc_ref[...] = jnp.zeros_like(qacc_ref)
            kacc_ref[...] = jnp.zeros_like(kacc_ref)

        cos2, sin2 = cos_ref[...], sin_ref[...]
        qpre, kpre = qk_ref[:, :AW], qk_ref[:, AW:]
        dqh = _rope_bwd(dq_ref[...], jnp.tile(cos2, (1, 4)), jnp.tile(sin2, (1, 4)))
        dqpre, dgq = _rms64_bwd(dqh, qpre, _rs64(qpre, ones_ref), qn_ref[...], ones_ref)
        dkh = _rope_bwd(dk_ref[...], cos2, sin2)
        dkpre, dgk = _rms64_bwd(dkh, kpre, _rs64(kpre, ones_ref), kn_ref[...], ones_ref)
        qacc_ref[...] += jnp.sum(dgq, axis=0, keepdims=True)
        kacc_ref[...] += jnp.sum(dgk, axis=0, keepdims=True)
        dproj = jnp.concatenate([dqpre.astype(BF16), dkpre.astype(BF16), dv_ref[...].astype(BF16), dgz_ref[...]], axis=1)
        dproj_ref[...] = dproj
        dh = _dot(dproj, w_ref[...])
        xv = x_ref[...]
        dx, dg = _rms_bwd(dh, xv, _rs(xv), g_ref[...])
        gx_ref[...] = dx1_ref[...] + dx
        dmn_ref[...] += _colsum8(dg)

        @pl.when(i == pl.num_programs(0) - 1)
        def _():
            qa = qacc_ref[...]
            q4 = qa[:, 0:128] + qa[:, 128:256] + qa[:, 256:384] + qa[:, 384:512]
            dqn_ref[...] = jnp.broadcast_to(_fold_half(q4), (8, 128))
            dkn_ref[...] = jnp.broadcast_to(_fold_half(kacc_ref[...]), (8, 128))

    return pl.pallas_call(
        body, name="mixer_in_bwd", grid=(t // TM,),
        in_specs=[_rows(TM, AW), _rows(TM, KW), _rows(TM, KW), _rows(TM, 2 * GW), _rows(TM, AW + KW), _rows(TM, 128),
                  _rows(TM, 128), _rows(TM, D), _rows(TM, D), _const((IN, D)), _const((1, D)), _const((1, AW)),
                  _const((1, KW)), _const((AW, AW))],
        out_specs=[_rows(TM, D), _rows(TM, IN), _const((8, D)), _const((8, 128)), _const((8, 128))],
        out_shape=[_sds((t, D)), _sds((t, IN), BF16), _sds((8, D)), _sds((8, 128)), _sds((8, 128))],
        scratch_shapes=[pltpu.VMEM((1, AW), F32), pltpu.VMEM((1, KW), F32)],
        compiler_params=_params())(dq, dk, dv, dgz, qk, cos, sin, x, dx1, w_in, mix_norm, qn, kn, _head_ones())


def _local_step(x, mem, pos, target, p, fetch, ship):
    t = x.shape[0]
    p = dict(p)
    p.update(fetch(0, None))
    inv_freq = 1.0 / (ROPE_THETA ** (jnp.arange(HD // 2, dtype=F32) * (2.0 / HD)))
    cos, sin = _rope_tables(pos, jnp.tile(inv_freq, 4).reshape(1, 128))
    qn = jnp.tile(p["q_norm"], (1, AW // HD))
    kn = jnp.tile(p["k_norm"], (1, KW // HD))
    qn4 = jnp.tile(p["xa_q_norm"], (1, XH))
    kn4 = jnp.tile(p["xa_k_norm"], (1, XH))
    ws = p["gmlp_ws"]
    wst = jnp.swapaxes(ws, 1, 2)
    bfull = jnp.repeat(p["gmlp_bs"].T, HD, axis=1)
    conv_b = p["ffn_conv_b"]

    h1, qk, gz, q, k, v, gu, gvn = _mixer_in_fwd(x, p["mix_norm"], p["w_in"], qn, kn, p["gmlp_v_norm"], cos, sin)
    attn = _swa_fwd(q, k, v, p["attn_sinks"])
    p.update(fetch(1, attn))
    gm, ycat, x1, h2 = _mixer_out_fwd(attn, gvn, gu, ws, bfull, x, p["w_out"], p["attn_out_norm"], p["gmlp_out_norm"],
                                      p["xa_norm"])
    mh, kpre, k2, v2 = _mem_kv_fwd(mem, p["mem_norm"], p["xa_wkv"], kn4)
    qpre, o, x2, h3 = _xattn_fwd(h2, x1, p["xa_wq"], qn4, k2, v2, p["xa_wo"], p["ffn_norm"])
    p.update(fetch(2, h3))
    conv = p["ffn_conv"]
    a, u, gs, dy, loss8 = _ffn_fwd(h3, x2, target, p["ffn_up"], conv, conv_b, p["ffn_down"])

    raw = {}
    d_down = _mm_tn(u, dy, "ffn_down_bwd_w")
    dx2, da, raw["conv_sums"], raw["ffn_norm"] = _ffn_bwd(dy, a, gs, x2, p["ffn_up"], conv, p["ffn_down"], p["ffn_norm"])
    d_up = _mm_tn(da, h3, "ffn_up_bwd_w")
    token = ship(0, {"ffn_down": d_down, "ffn_up": d_up, "ffn_conv": raw["conv_sums"][:, :, 0:3]})
    dx1, dqpre, dk2, dv2, raw["xa_q_norm"], raw["xa_norm"] = _xattn_bwd(
        dx2, x1, qpre, k2, v2, p["xa_wq"], p["xa_wo"], qn4 + jnp.tile(token[0:1], (1, D // 128)), p["xa_norm"])
    d_wo = _mm_tn(o, dx2, "xa_wo_bwd_w")
    d_wq = _mm_tn(h2, dqpre, "xa_wq_bwd_w")
    d_wkv, raw["xa_k_norm"], raw["mem_norm"] = _mem_kv_bwd(mem, mh, kpre, dk2, dv2, p["xa_wkv"], kn4, p["mem_norm"])
    d_w_out = _mm_tn(ycat, dx1, "w_out_bwd_w")
    token = ship(1, {"xa_wo": d_wo, "xa_wq": d_wq, "xa_wkv": d_wkv, "w_out": d_w_out})
    (dattn, raw["attn_out_norm"], raw["gmlp_out_norm"], dgz, raw["gmlp_ws"], raw["gmlp_bs"],
     raw["gmlp_v_norm"]) = _mixer_out_bwd(
        dx1, attn, gm, p["w_out"], p["attn_out_norm"] + jnp.tile(token[0:1], (1, AW // 128)), p["gmlp_out_norm"],
        gu, gvn, gz, ws, wst, bfull, p["gmlp_v_norm"])
    token = ship(2, {}, [raw["gmlp_ws"]])
    dq, dk, dv, raw["attn_sinks"] = _swa_bwd(q, k, v, dattn, p["attn_sinks"] + token[0:1, 0:8])
    grad_x, dproj, raw["mix_norm"], raw["q_norm"], raw["k_norm"] = _mixer_in_bwd(
        dq, dk, dv, dgz, qk, cos, sin, x, dx1, p["w_in"], p["mix_norm"], qn, kn)
    d_w_in = _mm_tn(dproj, h1, "w_in_bwd_w")
    raw["loss"] = loss8
    return grad_x, {"w_in": d_w_in}, raw


def _cast_shards(shards):
    def body(*refs):
        n = len(refs) // 2
        for i_ref, o_ref in zip(refs[:n], refs[n:]):
            o_ref[...] = i_ref[...].astype(BF16)

    return pl.pallas_call(body, name="cast_shards", out_shape=[_sds(s.shape, BF16) for s in shards],
                          compiler_params=pltpu.CompilerParams(vmem_limit_bytes=VMEM_LIMIT))(*shards)


HBM_SPEC = pl.BlockSpec(memory_space=pltpu.HBM)
SEM_SPEC = pl.BlockSpec(memory_space=pltpu.SEMAPHORE)


ALL_K = tuple(range(1, NDEV))
CHIP_K = (1, 2, 4, 6)
RELAY_K = (2, 4, 6)


def _peer(k):
    x, y, cc = lax.axis_index("x"), lax.axis_index("y"), lax.axis_index("c")
    return 1 - x if k & 4 else x, 1 - y if k & 2 else y, 1 - cc if k & 1 else cc


def _remote_copies(src_refs, land_refs, send_refs, recv_refs, nd, ks):
    me = 4 * lax.axis_index("x") + 2 * lax.axis_index("y") + lax.axis_index("c")
    copies = []
    for a, (src_ref, land_ref) in enumerate(zip(src_refs, land_refs)):
        for j, k in enumerate(ks):
            px, py, pc = _peer(k)
            copies.append((k, pltpu.make_async_remote_copy(
                src_ref=src_ref.at[4 * px + 2 * py + pc] if a < nd else src_ref, dst_ref=land_ref.at[me],
                send_sem=send_refs[a].at[j], recv_sem=recv_refs[a].at[j],
                device_id=(px, py, pc), device_id_type=pl.DeviceIdType.MESH)))
    return copies


def _relay_copies(land_refs, send_refs, recv_refs):
    copies = []
    for a, land_ref in enumerate(land_refs):
        for j, k in enumerate(RELAY_K):
            px, py, pc = _peer(k)
            slot = land_ref.at[4 * px + 2 * py + pc]
            copies.append(pltpu.make_async_remote_copy(
                src_ref=slot, dst_ref=slot, send_sem=send_refs[a].at[j], recv_sem=recv_refs[a].at[j],
                device_id=_peer(1), device_id_type=pl.DeviceIdType.MESH))
    return copies


def _own_slot(src, by_dest, me):
    block = lax.dynamic_index_in_dim(src, me, 0, keepdims=True) if by_dest else src[None]
    return lax.dynamic_update_index_in_dim(lax.empty((NDEV,) + block.shape[1:], src.dtype), block, me, 0)


SIDE_EFFECT = pltpu.CompilerParams(has_side_effects=pltpu.SideEffectType.DATAFLOW_SIDE_EFFECTING)


def _exchange_start(by_dest, for_all, me, name, ks=ALL_K):
    srcs = list(by_dest) + list(for_all)
    n, nd = len(srcs), len(by_dest)
    lands = [_own_slot(s, a < nd, me) for a, s in enumerate(srcs)]

    def body(*refs):
        for _, cp in _remote_copies(refs[:n], refs[n:2 * n], refs[2 * n:3 * n], refs[3 * n:4 * n], nd, ks):
            cp.start()
        refs[-1][...] = jnp.zeros((8, 128), F32)

    sems = [pltpu.SemaphoreType.DMA((len(ks),))] * (2 * n)
    thru = [pltpu.HBM(v.shape, v.dtype) for v in srcs + lands]
    res = pl.pallas_call(
        body, name=name, out_shape=sems + thru + [_sds((8, 128))],
        in_specs=[HBM_SPEC] * (2 * n), out_specs=[SEM_SPEC] * (2 * n) + [HBM_SPEC] * (2 * n) + [pl.BlockSpec(memory_space=pltpu.VMEM)],
        input_output_aliases={i: 2 * n + i for i in range(2 * n)}, compiler_params=SIDE_EFFECT)(
            *[pltpu.with_memory_space_constraint(v, pltpu.HBM) for v in srcs + lands])
    return (res[:2 * n], res[2 * n:4 * n], nd, ks, None), res[-1]


def _exchange_relay(state, after, name):
    sems, thru, nd, ks, _ = state
    n = len(thru) // 2

    def body(*refs):
        for k, cp in _remote_copies(refs[:n], refs[n:2 * n], refs[2 * n:3 * n], refs[3 * n:4 * n], nd, ks):
            if k in RELAY_K:
                cp.wait_recv()
        for cp in _relay_copies(refs[n:2 * n], refs[4 * n + 1:5 * n + 1], refs[5 * n + 1:6 * n + 1]):
            cp.start()

    relay_sems = [pltpu.SemaphoreType.DMA((len(RELAY_K),))] * (2 * n)
    res = pl.pallas_call(
        body, name=name, out_shape=relay_sems + [pltpu.HBM(v.shape, v.dtype) for v in thru],
        in_specs=[HBM_SPEC] * (2 * n) + [SEM_SPEC] * (2 * n) + [pl.BlockSpec(memory_space=pl.ANY)],
        out_specs=[SEM_SPEC] * (2 * n) + [HBM_SPEC] * (2 * n),
        input_output_aliases={i: 2 * n + i for i in range(2 * n)}, compiler_params=SIDE_EFFECT)(*thru, *sems, after)
    return sems, res[2 * n:], nd, ks, res[:2 * n]


def _exchange_wait(state, after, name):
    sems, thru, nd, ks, relay_sems = state
    n = len(thru) // 2

    def body(*refs):
        for k, cp in _remote_copies(refs[:n], refs[n:2 * n], refs[2 * n:3 * n], refs[3 * n:4 * n], nd, ks):
            cp.wait_send()
            if relay_sems is None or k not in RELAY_K:
                cp.wait_recv()
        if relay_sems is not None:
            for cp in _relay_copies(refs[n:2 * n], refs[4 * n:5 * n], refs[5 * n:6 * n]):
                cp.wait_send()
                cp.wait_recv()

    extra = [] if relay_sems is None else list(relay_sems)
    res = pl.pallas_call(
        body, name=name, out_shape=[pltpu.HBM(v.shape, v.dtype) for v in thru],
        in_specs=[HBM_SPEC] * (2 * n) + [SEM_SPEC] * (2 * n + len(extra)) + [pl.BlockSpec(memory_space=pl.ANY)],
        out_specs=[HBM_SPEC] * (2 * n), input_output_aliases={i: i for i in range(2 * n)}, compiler_params=SIDE_EFFECT)(
            *thru, *sems, *extra, after)
    return res[n:]


def _adam(parts, w, m, v, name):
    def body(p_ref, w_ref, m_ref, v_ref, g_ref, d_ref, nm_ref, nv_ref):
        g = _sum_parts(p_ref)
        g_ref[...] = g
        d_ref[...], nm_ref[...], nv_ref[...] = _adam_math(g, w_ref[...], m_ref[...], v_ref[...])

    return pl.pallas_call(
        body, name=name, out_shape=[_sds(w.shape)] * 4,
        compiler_params=pltpu.CompilerParams(vmem_limit_bytes=VMEM_LIMIT))(parts, w, m, v)


GATHER_GROUPS = (("w_in",), ("w_out", "xa_wkv", "xa_wq", "xa_wo"), ("ffn_up", "ffn_conv", "ffn_down"))
SCATTER_GROUPS = (("ffn_down", "ffn_up", "ffn_conv"), ("xa_wo", "xa_wq", "xa_wkv", "w_out"), (), ("w_in",))
BIG = tuple(n for grp in GATHER_GROUPS for n in grp)
BY_COLUMN = ("w_in", "ffn_up")
VECS = (("mix_norm", D), ("q_norm", HD), ("k_norm", HD), ("attn_sinks", 8), ("gmlp_v_norm", GW), ("attn_out_norm", AW),
        ("gmlp_out_norm", GW), ("xa_norm", D), ("mem_norm", D), ("xa_q_norm", XD), ("xa_k_norm", XD), ("ffn_norm", D))
BS_ROW = 16
VEC_ROWS = 24
SMALL = tuple(n for n, _ in VECS) + ("gmlp_bs", "gmlp_ws", "ffn_conv_b")


def _pack_small(raw):
    names = [n for n, _ in VECS] + ["gmlp_bs", "conv_sums"]

    def body(*refs):
        ins = dict(zip(names, refs))
        vec_ref, cb_ref = refs[len(names):]
        vec_ref[...] = jnp.zeros_like(vec_ref)
        for r, (n, w) in enumerate(VECS):
            vec_ref[r:r + 1, 0:w] = ins[n][0:1, 0:w]
        vec_ref[BS_ROW:BS_ROW + 8, 0:BLK] = ins["gmlp_bs"][...]
        for s in range(2):
            for d in range(NG):
                cb_ref[s, d] = ins["conv_sums"][s, d, 3:4, :]

    return pl.pallas_call(body, name="pack_small", out_shape=[_sds((VEC_ROWS, D)), _sds((2, NG, 1, SW))])(
        *[raw[n] for n in names])


def _adam_math(g, w, m, v):
    nm = B1 * m + (1.0 - B1) * g
    nv = B2 * v + (1.0 - B2) * (g * g)
    m_hat = nm / (1.0 - B1 ** STEP)
    v_hat = nv / (1.0 - B2 ** STEP)
    return -LR * (m_hat / (jnp.sqrt(v_hat) + AEPS) + WD * w), nm, nv


def _sum_parts(p_ref):
    g = p_ref[0].astype(F32)
    for j in range(1, NDEV):
        g = g + p_ref[j].astype(F32)
    return g


def _adam_small(parts_vec, parts_ws, parts_cb, w, m, v):
    def body(*refs):
        pv_ref, pws_ref, pcb_ref = refs[:3]
        ins = refs[3:3 + 3 * len(SMALL)]
        outs = refs[3 + 3 * len(SMALL):]
        gv = _sum_parts(pv_ref)
        for j, n in enumerate(SMALL):
            w_ref, m_ref, v_ref = ins[3 * j:3 * j + 3]
            o = outs[4 * j:4 * j + 4]
            if n == "gmlp_ws":
                g = _sum_parts(pws_ref)
            elif n == "ffn_conv_b":
                g = _sum_parts(pcb_ref)
            elif n == "gmlp_bs":
                g = gv[BS_ROW:BS_ROW + 8, 0:BLK]
            else:
                g = gv[j:j + 1, 0:VECS[j][1]]
            lead = n in ("gmlp_ws", "gmlp_bs")
            res = (g,) + _adam_math(g, w_ref[0] if lead else w_ref[...], m_ref[0] if lead else m_ref[...],
                                    v_ref[0] if lead else v_ref[...])
            for o_ref, val in zip(o, res):
                if lead:
                    o_ref[0] = val
                else:
                    o_ref[...] = val

    args = [parts_vec, parts_ws, parts_cb] + [d[n] for n in SMALL for d in (w, m, v)]
    res = pl.pallas_call(body, name="adam_small", out_shape=[_sds(w[n].shape) for n in SMALL for _ in range(4)],
                         compiler_params=pltpu.CompilerParams(vmem_limit_bytes=VMEM_LIMIT))(*args)
    return {n: tuple(res[4 * j:4 * j + 4]) for j, n in enumerate(SMALL)}


def kernel(x, mem, positions, mix_norm, w_in, q_norm, k_norm, attn_sinks, gmlp_v_norm, gmlp_ws, gmlp_bs, attn_out_norm, gmlp_out_norm, w_out, xa_norm, mem_norm, xa_wq, xa_wkv, xa_q_norm, xa_k_norm, xa_wo, ffn_norm, ffn_up, ffn_conv, ffn_conv_b, ffn_down, loss_target, m_mix_norm, m_w_in, m_q_norm, m_k_norm, m_attn_sinks, m_gmlp_v_norm, m_gmlp_ws, m_gmlp_bs, m_attn_out_norm, m_gmlp_out_norm, m_w_out, m_xa_norm, m_mem_norm, m_xa_wq, m_xa_wkv, m_xa_q_norm, m_xa_k_norm, m_xa_wo, m_ffn_norm, m_ffn_up, m_ffn_conv, m_ffn_conv_b, m_ffn_down, v_mix_norm, v_w_in, v_q_norm, v_k_norm, v_attn_sinks, v_gmlp_v_norm, v_gmlp_ws, v_gmlp_bs, v_attn_out_norm, v_gmlp_out_norm, v_w_out, v_xa_norm, v_mem_norm, v_xa_wq, v_xa_wkv, v_xa_q_norm, v_xa_k_norm, v_xa_wo, v_ffn_norm, v_ffn_up, v_ffn_conv, v_ffn_conv_b, v_ffn_down):
    names = ("mix_norm", "w_in", "q_norm", "k_norm", "attn_sinks", "gmlp_v_norm", "gmlp_ws", "gmlp_bs", "attn_out_norm",
             "gmlp_out_norm", "w_out", "xa_norm", "mem_norm", "xa_wq", "xa_wkv", "xa_q_norm", "xa_k_norm", "xa_wo",
             "ffn_norm", "ffn_up", "ffn_conv", "ffn_conv_b", "ffn_down")
    w = dict(zip(names, (mix_norm, w_in, q_norm, k_norm, attn_sinks, gmlp_v_norm, gmlp_ws, gmlp_bs, attn_out_norm,
                         gmlp_out_norm, w_out, xa_norm, mem_norm, xa_wq, xa_wkv, xa_q_norm, xa_k_norm, xa_wo, ffn_norm,
                         ffn_up, ffn_conv, ffn_conv_b, ffn_down)))
    m = dict(zip(names, (m_mix_norm, m_w_in, m_q_norm, m_k_norm, m_attn_sinks, m_gmlp_v_norm, m_gmlp_ws, m_gmlp_bs,
                         m_attn_out_norm, m_gmlp_out_norm, m_w_out, m_xa_norm, m_mem_norm, m_xa_wq, m_xa_wkv,
                         m_xa_q_norm, m_xa_k_norm, m_xa_wo, m_ffn_norm, m_ffn_up, m_ffn_conv, m_ffn_conv_b, m_ffn_down)))
    v = dict(zip(names, (v_mix_norm, v_w_in, v_q_norm, v_k_norm, v_attn_sinks, v_gmlp_v_norm, v_gmlp_ws, v_gmlp_bs,
                         v_attn_out_norm, v_gmlp_out_norm, v_w_out, v_xa_norm, v_mem_norm, v_xa_wq, v_xa_wkv,
                         v_xa_q_norm, v_xa_k_norm, v_xa_wo, v_ffn_norm, v_ffn_up, v_ffn_conv, v_ffn_conv_b, v_ffn_down)))
    t = x.shape[1]

    me = 4 * lax.axis_index("x") + 2 * lax.axis_index("y") + lax.axis_index("c")

    def rows(a, n):
        return jnp.swapaxes(a[0], 0, 1) if n in BY_COLUMN else a[0]

    mats = [n for n in BIG if n != "ffn_conv"]
    shard = dict(zip(mats, _cast_shards([rows(w[n], n) for n in mats])), ffn_conv=w["ffn_conv"][0])
    gathers, tokens = zip(*[_exchange_start([], [shard[n] for n in grp], me, "gather_start_%d" % i,
                                            CHIP_K if i == len(GATHER_GROUPS) - 1 else ALL_K)
                            for i, grp in enumerate(GATHER_GROUPS)])

    def fetch(i, after):
        after = tokens[0] + tokens[1] + tokens[2] if after is None else after
        state = gathers[i]
        if state[3] == CHIP_K:
            state = _exchange_relay(state, after, "gather_relay_%d" % i)
        got = dict(zip(GATHER_GROUPS[i], _exchange_wait(state, after, "gather_wait_%d" % i)))
        if "w_in" in got:
            got["w_in"] = got["w_in"].reshape(IN, D)
        for n in ("w_out", "xa_wq", "xa_wo"):
            if n in got:
                got[n] = got[n].reshape(D, D)
        if "ffn_down" in got:
            got["ffn_down"] = got["ffn_down"].reshape(NG, SW, D)
            got["ffn_conv"] = got["ffn_conv"].reshape(2, NG, 3, SW)
        return got

    scatters = []

    def ship(i, grads, for_all=()):
        by_dest = [grads[n].reshape((NDEV,) + shard[n].shape) for n in SCATTER_GROUPS[i]]
        state, token = _exchange_start(by_dest, for_all, me, "scatter_start_%d" % i)
        scatters.append(state)
        return token

    conv_b = {k: d["ffn_conv_b"].reshape(NDEV, 1, SW) for k, d in (("w", w), ("m", m), ("v", v))}
    p = {n: w[n] for n in SMALL[:-1]}
    p["gmlp_ws"], p["gmlp_bs"] = w["gmlp_ws"][0], w["gmlp_bs"][0]
    p["ffn_conv_b"] = conv_b["w"].reshape(2, NG, 1, SW)
    grad_x, g, raw = _local_step(x[0], mem[0], positions.reshape(t, 1), loss_target[0], p, fetch, ship)
    loss = lax.psum(raw["loss"][0, 0], ("x", "y", "c"))

    vec, cb = _pack_small(raw)
    after = ship(3, g, [vec, cb.reshape(NDEV, 1, SW)])
    res, rest = {}, []
    for i, grp in enumerate(SCATTER_GROUPS):
        got = _exchange_wait(scatters[i], after, "scatter_wait_%d" % i)
        rest += got[len(grp):]
        for n, parts in zip(grp, got):
            out = _adam(parts, rows(w[n], n), rows(m[n], n), rows(v[n], n), "adam_" + n)
            res[n] = [jnp.swapaxes(o, 0, 1) if n in BY_COLUMN else o for o in out]
            after = out[0]
    ws_parts, vec_parts, cb_parts = rest
    small = lambda d, k: {**{n: d[n] for n in SMALL[:-1]}, "ffn_conv_b": conv_b[k]}
    res.update(_adam_small(vec_parts, ws_parts, cb_parts, small(w, "w"), small(m, "m"), small(v, "v")))

    outs = [loss, grad_x[None]]
    for j in range(4):
        outs += [res[n][j].reshape(w[n].shape) for n in names]
    return tuple(outs)
```

```python
import functools
import math

import jax
import jax.numpy as jnp
from jax import lax
from jax.experimental import pallas as pl
from jax.experimental.pallas import tpu as pltpu

F32 = jnp.float32
BF16 = jnp.bfloat16

D = 1024
HD = 64
AW = 512
KW = 128
GW = 512
IN = AW + 2 * KW + 2 * GW
BLK = 128
MEM = 256
XH = 4
XD = 256
FF = 2816
EPS = 1e-6
ROPE_THETA = 10000.0
NDEV = 8
LR, B1, B2, AEPS, WD, STEP = 0.001, 0.9, 0.999, 1e-08, 0.01, 10

TM = 512
WK = 2048
VMEM_LIMIT = 56 * 1024 * 1024
NEG = float(jnp.finfo(jnp.float32).min)
GELU_C0 = math.sqrt(2.0 / math.pi)
GELU_C1 = 0.044715


def _dot(a, b):
    return jnp.dot(a, b, preferred_element_type=F32)


def _dot_nt(a, b):
    return lax.dot_general(a, b, (((1,), (1,)), ((), ())), preferred_element_type=F32)


def _dot_tn(a, b):
    return lax.dot_general(a, b, (((0,), (0,)), ((), ())), preferred_element_type=F32)


def _rs(x):
    return lax.rsqrt(jnp.mean(x * x, axis=-1, keepdims=True) + EPS)


def _rms_bwd(dy, x, r, g):
    xh = x * r
    dxh = dy * g
    dx = r * (dxh - xh * jnp.mean(dxh * xh, axis=-1, keepdims=True))
    return dx, dy * xh


def _lane(shape):
    return lax.broadcasted_iota(jnp.int32, shape, len(shape) - 1)


def _gsum64(v, ones_ref):
    w = v.shape[-1]
    ones = ones_ref[0:w, 0:w]
    hi = v.astype(BF16)
    lo = (v - hi.astype(F32)).astype(BF16)
    return _dot(hi, ones) + _dot(lo, ones)


def _head_ones():
    i = jnp.arange(AW) // HD
    return (i[:, None] == i[None, :]).astype(BF16)


def _rs64(x, ones_ref):
    return lax.rsqrt(_gsum64(x * x, ones_ref) * (1.0 / HD) + EPS)


def _rms64_bwd(dy, x, r, g, ones_ref):
    xh = x * r
    dxh = dy * g
    dx = r * (dxh - xh * (_gsum64(dxh * xh, ones_ref) * (1.0 / HD)))
    return dx, dy * xh


def _rot_half(v):
    w = v.shape[-1]
    return jnp.where((_lane(v.shape) & 32) == 0, pltpu.roll(v, w - 32, 1), pltpu.roll(v, 32, 1))


def _rope(v, cos, sin_signed):
    return v * cos + _rot_half(v) * sin_signed


def _rope_bwd(dv, cos, sin_signed):
    return dv * cos + _rot_half(dv * sin_signed)


def _gelu(z):
    return 0.5 * z * (1.0 + jnp.tanh(GELU_C0 * (z + GELU_C1 * z * z * z)))


def _gelu_grad(z):
    t = jnp.tanh(GELU_C0 * (z + GELU_C1 * z * z * z))
    return 0.5 * (1.0 + t) + 0.5 * z * (1.0 - t * t) * (GELU_C0 * (1.0 + 3.0 * GELU_C1 * z * z))


def _colsum8(v):
    s = jnp.sum(v, axis=0, keepdims=True)
    row = lax.broadcasted_iota(jnp.int32, (8, v.shape[1]), 0)
    return jnp.where(row == 0, jnp.broadcast_to(s, (8, v.shape[1])), 0.0)


def _params(n_axes=1):
    return pltpu.CompilerParams(dimension_semantics=("arbitrary",) * n_axes, vmem_limit_bytes=VMEM_LIMIT)


def _rows(tm, w):
    return pl.BlockSpec((tm, w), lambda i: (i, 0))


def _const(shape):
    nd = len(shape)
    return pl.BlockSpec(shape, lambda *_: (0,) * nd)


def _sds(shape, dtype=F32):
    return jax.ShapeDtypeStruct(shape, dtype)


def _mm_tn(a, b, name):
    g = max(a.shape[0] if a.ndim == 3 else 1, b.shape[0] if b.ndim == 3 else 1)
    t, m = a.shape[-2:]
    n = b.shape[-1]

    def body(a_ref, b_ref, o_ref, acc_ref):
        i = pl.program_id(1)

        @pl.when(i == 0)
        def _():
            acc_ref[...] = jnp.zeros_like(acc_ref)

        acc_ref[...] += _dot_tn(a_ref[...].astype(BF16), b_ref[...].astype(BF16))

        @pl.when(i == pl.num_programs(1) - 1)
        def _():
            o_ref[...] = acc_ref[...].astype(BF16)

    tk = min(t, WK)

    def spec(v):
        w = v.shape[-1]
        if v.ndim == 3:
            return pl.BlockSpec((None, tk, w), lambda j, i: (j, i, 0))
        return pl.BlockSpec((tk, w), lambda j, i: (i, 0))

    return pl.pallas_call(
        body, name=name, grid=(g, t // tk), in_specs=[spec(a), spec(b)],
        out_specs=pl.BlockSpec((None, m, n), lambda j, i: (j, 0, 0)), out_shape=_sds((g, m, n), BF16),
        scratch_shapes=[pltpu.VMEM((m, n), F32)], compiler_params=_params(2))(a, b)


def _rope_tables(pos, inv_freq):
    t = pos.shape[0]

    def body(pos_ref, f_ref, cos_ref, sin_ref):
        ang = pos_ref[...].astype(F32) * f_ref[...]
        sign = jnp.where((_lane(ang.shape) & 32) == 0, -1.0, 1.0)
        cos_ref[...] = jnp.cos(ang)
        sin_ref[...] = jnp.sin(ang) * sign

    return pl.pallas_call(
        body, name="rope_tables", grid=(t // TM,),
        in_specs=[_rows(TM, 1), _const((1, 128))], out_specs=[_rows(TM, 128), _rows(TM, 128)],
        out_shape=[_sds((t, 128)), _sds((t, 128))], compiler_params=_params())(pos, inv_freq)


def _mixer_in_fwd(x, mix_norm, w_in, qn, kn, gvw, cos, sin):
    t = x.shape[0]

    def body(x_ref, g_ref, w_ref, qn_ref, kn_ref, gvw_ref, cos_ref, sin_ref, ones_ref,
             h_ref, qk_ref, gz_ref, q_ref, k_ref, v_ref, gu_ref, gvn_ref):
        x = x_ref[...]
        h = (x * _rs(x) * g_ref[...]).astype(BF16)
        h_ref[...] = h
        proj = _dot_nt(h, w_ref[...])
        qk = proj[:, :AW + KW]
        qk_ref[...] = qk
        gz = proj[:, AW + 2 * KW:]
        gz_ref[...] = gz
        cos2, sin2 = cos_ref[...], sin_ref[...]
        q = qk[:, :AW]
        q = q * _rs64(q, ones_ref) * qn_ref[...]
        q_ref[...] = _rope(q, jnp.tile(cos2, (1, 4)), jnp.tile(sin2, (1, 4))).astype(BF16)
        k = qk[:, AW:]
        k = k * _rs64(k, ones_ref) * kn_ref[...]
        k_ref[...] = _rope(k, cos2, sin2).astype(BF16)
        v_ref[...] = proj[:, AW + KW:AW + 2 * KW].astype(BF16)
        gu_ref[...] = _gelu(gz[:, :GW])
        gv = _gelu(gz[:, GW:])
        gvn_ref[...] = (gv * _rs(gv) * gvw_ref[...]).astype(BF16)

    return pl.pallas_call(
        body, name="mixer_in_fwd", grid=(t // TM,),
        in_specs=[_rows(TM, D), _const((1, D)), _const((IN, D)), _const((1, AW)), _const((1, KW)),
                  _const((1, GW)), _rows(TM, 128), _rows(TM, 128), _const((AW, AW))],
        out_specs=[_rows(TM, D), _rows(TM, AW + KW), _rows(TM, 2 * GW), _rows(TM, AW), _rows(TM, KW),
                   _rows(TM, KW), _rows(TM, GW), _rows(TM, GW)],
        out_shape=[_sds((t, D), BF16), _sds((t, AW + KW)), _sds((t, 2 * GW)), _sds((t, AW), BF16),
                   _sds((t, KW), BF16), _sds((t, KW), BF16), _sds((t, GW)), _sds((t, GW), BF16)],
        compiler_params=_params())(x, mix_norm, w_in, qn, kn, gvw, cos, sin, _head_ones())


def _dup_half(kk, g):
    lane = _lane(kk.shape)
    other = pltpu.roll(kk, 64, 1)
    keep = (lane < 64) if g == 0 else (lane >= 64)
    return jnp.where(keep, kk, other).astype(BF16)


def _swa_mask(first_block):
    qi = lax.broadcasted_iota(jnp.int32, (4 * BLK, 2 * BLK), 0) & (BLK - 1)
    kj = lax.broadcasted_iota(jnp.int32, (4 * BLK, 2 * BLK), 1)
    diff = qi + BLK - kj
    band = (diff >= 0) & (diff < BLK)
    return band & (jnp.logical_not(first_block) | (kj >= BLK))


def _stack_heads(a2, b2):
    lo = _lane(a2.shape) < 64
    z = jnp.zeros_like(a2)
    return jnp.concatenate([jnp.where(lo, a2, z), jnp.where(lo, z, a2), jnp.where(lo, b2, z), jnp.where(lo, z, b2)], axis=0)


def _unstack_heads(o):
    lo = _lane((BLK, 128)) < 64
    return jnp.where(lo, o[0:BLK], o[BLK:2 * BLK]), jnp.where(lo, o[2 * BLK:3 * BLK], o[3 * BLK:4 * BLK])


def _sink_col(sink_ref, g):
    row = lax.broadcasted_iota(jnp.int32, (4 * BLK, 1), 0)
    s = [sink_ref[0, 4 * g + j] for j in range(4)]
    return jnp.where(row < BLK, s[0], jnp.where(row < 2 * BLK, s[1], jnp.where(row < 3 * BLK, s[2], s[3])))


def _swa_probs(qs, kd, mask, sink):
    s = _dot_nt(qs, kd) * (1.0 / math.sqrt(HD))
    s = jnp.where(mask, s, NEG)
    m = jnp.maximum(jnp.max(s, axis=-1, keepdims=True), sink)
    p = jnp.exp(s - m)
    ps = jnp.exp(sink - m)
    inv = 1.0 / (jnp.sum(p, axis=-1, keepdims=True) + ps)
    return p * inv, ps * inv


SB = 4
SB_BWD = 2


def _swa_fwd(q, k, v, sinks):
    t = q.shape[0]
    ts = min(t, SB * BLK)

    def body(sink_ref, q_ref, kc_ref, kp_ref, vc_ref, vp_ref, o_ref):
        i = pl.program_id(0)
        kk = jnp.concatenate([kp_ref[...], kc_ref[...]], axis=0).astype(F32)
        vv = jnp.concatenate([vp_ref[...], vc_ref[...]], axis=0).astype(F32)
        for b in range(ts // BLK):
            r = slice(b * BLK, (b + 1) * BLK)
            kb, vb = kk[b * BLK:(b + 2) * BLK], vv[b * BLK:(b + 2) * BLK]
            mask = _swa_mask(i == 0) if b == 0 else _swa_mask(False)
            for g in range(2):
                qs = _stack_heads(q_ref[r, 256 * g:256 * g + 128], q_ref[r, 256 * g + 128:256 * g + 256])
                pn, _ = _swa_probs(qs, _dup_half(kb, g), mask, _sink_col(sink_ref, g))
                oa, ob = _unstack_heads(_dot(pn.astype(BF16), _dup_half(vb, g)))
                o_ref[r, 256 * g:256 * g + 128] = oa
                o_ref[r, 256 * g + 128:256 * g + 256] = ob

    cur = lambda i: (i, 0)
    prev = lambda i: (jnp.maximum(i * (ts // BLK) - 1, 0), 0)
    return pl.pallas_call(
        body, name="swa_fwd", grid=(t // ts,),
        in_specs=[pl.BlockSpec(memory_space=pltpu.SMEM), pl.BlockSpec((ts, AW), cur),
                  pl.BlockSpec((ts, KW), cur), pl.BlockSpec((BLK, KW), prev),
                  pl.BlockSpec((ts, KW), cur), pl.BlockSpec((BLK, KW), prev)],
        out_specs=pl.BlockSpec((ts, AW), cur), out_shape=_sds((t, AW)),
        compiler_params=_params())(sinks, q, k, k, v, v)


def _causal_bf16(w_ref, h, transposed):
    r = lax.broadcasted_iota(jnp.int32, (BLK, BLK), 0)
    c = lax.broadcasted_iota(jnp.int32, (BLK, BLK), 1)
    keep = (r <= c) if transposed else (c <= r)
    return jnp.where(keep, w_ref[h], 0.0).astype(BF16)


def _gmlp_mix(w_ref, xin, transposed):
    lo = _lane((BLK, 128)) < 64
    wm = [_causal_bf16(w_ref, h, transposed) for h in range(8)]
    rows = []
    for c in range(xin.shape[0] // BLK):
        cols = []
        for j in range(4):
            xs = xin[c * BLK:(c + 1) * BLK, 128 * j:128 * (j + 1)]
            cols.append(jnp.where(lo, _dot(wm[2 * j], xs), _dot(wm[2 * j + 1], xs)))
        rows.append(jnp.concatenate(cols, axis=1))
    return jnp.concatenate(rows, axis=0)


def _mixer_out_fwd(attn, gvn, gu, ws, bfull, x, w_out, aon, gon, xan):
    t = x.shape[0]

    def body(a_ref, v_ref, gu_ref, ws_ref, b_ref, x_ref, w_ref, aon_ref, gon_ref, xan_ref, gm_ref, y_ref, x1_ref, h2_ref):
        a = a_ref[...]
        g = gu_ref[...] * (_gmlp_mix(ws_ref, v_ref[...], False) + jnp.tile(b_ref[...], (TM // BLK, 1)))
        gm_ref[...] = g
        y = jnp.concatenate([a * _rs(a) * aon_ref[...], g * _rs(g) * gon_ref[...]], axis=1).astype(BF16)
        y_ref[...] = y
        x1 = x_ref[...] + _dot(y, w_ref[...])
        x1_ref[...] = x1
        h2_ref[...] = (x1 * _rs(x1) * xan_ref[...]).astype(BF16)

    return pl.pallas_call(
        body, name="mixer_out_fwd", grid=(t // TM,),
        in_specs=[_rows(TM, AW), _rows(TM, GW), _rows(TM, GW), _const((8, BLK, BLK)), _const((BLK, GW)), _rows(TM, D),
                  _const((D, D)), _const((1, AW)), _const((1, GW)), _const((1, D))],
        out_specs=[_rows(TM, GW), _rows(TM, D), _rows(TM, D), _rows(TM, D)],
        out_shape=[_sds((t, GW)), _sds((t, D), BF16), _sds((t, D)), _sds((t, D), BF16)],
        compiler_params=_params())(attn, gvn, gu, ws, bfull, x, w_out, aon, gon, xan)


def _mem_kv_fwd(mem, mem_norm, wkv, kn4):
    def body(m_ref, g_ref, w_ref, kn_ref, mh_ref, kpre_ref, k_ref, v_ref):
        m = m_ref[...]
        mh = (m * _rs(m) * g_ref[...]).astype(BF16)
        mh_ref[...] = mh
        for h in range(XH):
            sl = slice(XD * h, XD * (h + 1))
            kh = _dot(mh, w_ref[h])
            kpre_ref[:, sl] = kh
            k_ref[:, sl] = (kh * _rs(kh) * kn_ref[:, sl]).astype(BF16)
            v_ref[:, sl] = _dot(mh, w_ref[XH + h]).astype(BF16)

    return pl.pallas_call(
        body, name="mem_kv_fwd",
        out_shape=[_sds((MEM, D), BF16), _sds((MEM, D)), _sds((MEM, D), BF16), _sds((MEM, D), BF16)],
        compiler_params=pltpu.CompilerParams(vmem_limit_bytes=VMEM_LIMIT))(mem, mem_norm, wkv, kn4)


def _xattn_probs(qpre_h, qn_h, k_h):
    rq = _rs(qpre_h)
    q2 = (qpre_h * rq * qn_h).astype(BF16)
    s = _dot_nt(q2, k_h) * (1.0 / math.sqrt(XD))
    p = jnp.exp(s - jnp.max(s, axis=-1, keepdims=True))
    return p * (1.0 / jnp.sum(p, axis=-1, keepdims=True)), q2, rq


def _xattn_fwd(h2, x1, wq, qn4, k2, v2, wo, ffn_norm):
    t = x1.shape[0]

    def body(h_ref, x_ref, wq_ref, qn_ref, k_ref, v_ref, wo_ref, fn_ref, qpre_ref, o_ref, x2_ref, h3_ref):
        qpre = _dot(h_ref[...], wq_ref[...])
        qpre_ref[...] = qpre
        outs = []
        for h in range(XH):
            sl = slice(XD * h, XD * (h + 1))
            pn, _, _ = _xattn_probs(qpre[:, sl], qn_ref[:, sl], k_ref[:, sl])
            outs.append(_dot(pn.astype(BF16), v_ref[:, sl]))
        o = jnp.concatenate(outs, axis=1).astype(BF16)
        o_ref[...] = o
        x2 = x_ref[...] + _dot(o, wo_ref[...])
        x2_ref[...] = x2
        h3_ref[...] = (x2 * _rs(x2) * fn_ref[...]).astype(BF16)

    return pl.pallas_call(
        body, name="xattn_fwd", grid=(t // TM,),
        in_specs=[_rows(TM, D), _rows(TM, D), _const((D, D)), _const((1, D)), _const((MEM, D)), _const((MEM, D)),
                  _const((D, D)), _const((1, D))],
        out_specs=[_rows(TM, D)] * 4,
        out_shape=[_sds((t, D)), _sds((t, D), BF16), _sds((t, D)), _sds((t, D), BF16)],
        compiler_params=_params())(h2, x1, wq, qn4, k2, v2, wo, ffn_norm)


SW = 704
NG = FF // SW
FM = 256
HALO = 16


def _resident(shape):
    nd = len(shape)
    return pl.BlockSpec(shape, lambda *_: (0,) * nd, pipeline_mode=pl.Buffered(1))


def _halo_before(i):
    return jnp.maximum(i * (FM // HALO) - 1, 0)


def _conv(e, w):
    return w[2:3, :] * e + pltpu.roll(w[1:2, :] * e + pltpu.roll(w[0:1, :] * e, 1, 0), 1, 0)


def _conv_t(dc, w):
    n = dc.shape[0]
    return w[2:3, :] * dc + pltpu.roll(w[1:2, :] * dc + pltpu.roll(w[0:1, :] * dc, n - 1, 0), n - 1, 0)


def _ffn_fwd(h3, x2, target, up, conv, conv_b, down):
    t = x2.shape[0]

    def body(h_ref, hp_ref, x_ref, t_ref, up_ref, w_ref, b_ref, dn_ref, a_ref, u_ref, gs_ref, dy_ref, loss_ref, acc_ref):
        i = pl.program_id(0)

        @pl.when(i == 0)
        def _():
            acc_ref[...] = jnp.zeros_like(acc_ref)

        before = jnp.where(i > 0, hp_ref[...], jnp.zeros_like(hp_ref))
        he = jnp.concatenate([before, h_ref[...]], axis=0)
        err = x_ref[...] - t_ref[...]
        for d in range(NG):
            c = []
            for s in range(2):
                a = _dot_nt(he, up_ref[s * NG + d])
                a_ref[s * NG + d] = a[HALO:].astype(BF16)
                c.append(_conv(a, w_ref[s, d])[HALO:] + b_ref[s, d])
            gl, gg = _gelu_and_grad(c[0])
            gs_ref[d] = gl.astype(BF16)
            gs_ref[NG + d] = (gg * c[1]).astype(BF16)
            u = (gl * c[1]).astype(BF16)
            u_ref[d] = u
            err = err + _dot(u, dn_ref[d])
        dy_ref[...] = err * (1.0 / D)
        acc_ref[...] += jnp.sum(err * err, axis=0, keepdims=True)

        @pl.when(i == pl.num_programs(0) - 1)
        def _():
            loss_ref[...] = jnp.full((8, 128), 0.5 / D, F32) * jnp.sum(acc_ref[...])

    return pl.pallas_call(
        body, name="ffn_fwd", grid=(t // FM,),
        in_specs=[_rows(FM, D), pl.BlockSpec((HALO, D), lambda i: (_halo_before(i), 0)), _rows(FM, D), _rows(FM, D),
                  _resident((NDEV, SW, D)), _resident((2, NG, 3, SW)), _resident((2, NG, 1, SW)), _resident((NG, SW, D))],
        out_specs=[pl.BlockSpec((NDEV, FM, SW), lambda i: (0, i, 0)), pl.BlockSpec((NG, FM, SW), lambda i: (0, i, 0)),
                   pl.BlockSpec((NDEV, FM, SW), lambda i: (0, i, 0)), _rows(FM, D), _const((8, 128))],
        out_shape=[_sds((NDEV, t, SW), BF16), _sds((NG, t, SW), BF16), _sds((NDEV, t, SW), BF16), _sds((t, D)),
                   _sds((8, 128))],
        scratch_shapes=[pltpu.VMEM((1, D), F32)], compiler_params=_params())(h3, h3, x2, target, up, conv, conv_b, down)


def _gelu_and_grad(z):
    z2 = z * z
    t = jnp.tanh(GELU_C0 * (z + GELU_C1 * z * z2))
    phi = 0.5 * (1.0 + t)
    return z * phi, phi + z * (1.0 - t * t) * (0.5 * GELU_C0 + (1.5 * GELU_C0 * GELU_C1) * z2)


def _ffn_bwd(dy, a, gs, x2, up, conv, down, ffn_norm):
    t = x2.shape[0]
    nt = t // FM
    n = FM + HALO

    def body(dy_ref, dyn_ref, a_ref, gs_ref, gsn_ref, x_ref, up_ref, w_ref, dn_ref, g_ref,
             dx_ref, da_ref, s_ref, dfn_ref):
        i = pl.program_id(0)

        @pl.when(i == 0)
        def _():
            s_ref[...] = jnp.zeros_like(s_ref)
            dfn_ref[...] = jnp.zeros_like(dfn_ref)

        last = i == nt - 1
        dy = dy_ref[...]
        dye = jnp.concatenate([dy, jnp.where(last, 0.0, dyn_ref[...])], axis=0).astype(BF16)
        dh = jnp.zeros((FM, D), F32)
        row = lax.broadcasted_iota(jnp.int32, (8, SW), 0)
        for d in range(NG):
            du = _dot_nt(dye, dn_ref[d])
            for s in range(2):
                j = s * NG + d
                k = NG + d if s == 0 else d
                dc = du * jnp.concatenate([gs_ref[k], gsn_ref[k]], axis=0).astype(F32)
                w = w_ref[s, d]
                tile = a_ref[j].astype(F32)
                d1 = pltpu.roll(dc, n - 1, 0)
                d2 = pltpu.roll(d1, n - 1, 0)
                da = (w[2:3, :] * dc + w[1:2, :] * d1 + w[0:1, :] * d2)[0:FM].astype(BF16)
                da_ref[j] = da
                dh = dh + _dot(da, up_ref[j])
                sums = [jnp.sum(v[0:FM] * tile, axis=0, keepdims=True) for v in (d2, d1, dc)]
                sums.append(jnp.sum(dc[0:FM], axis=0, keepdims=True))
                upd = jnp.zeros((8, SW), F32)
                for r, v in enumerate(sums):
                    upd = jnp.where(row == r, jnp.broadcast_to(v, (8, SW)), upd)
                s_ref[s, d] += upd
        x = x_ref[...]
        dx, dg = _rms_bwd(dh, x, _rs(x), g_ref[...])
        dx_ref[...] = dy + dx
        dfn_ref[...] += _colsum8(dg)

    last_halo = t // HALO - 1
    after = lambda i: jnp.minimum((i + 1) * (FM // HALO), last_halo)
    return pl.pallas_call(
        body, name="ffn_bwd", grid=(nt,),
        in_specs=[_rows(FM, D), pl.BlockSpec((HALO, D), lambda i: (after(i), 0)),
                  pl.BlockSpec((NDEV, FM, SW), lambda i: (0, i, 0)),
                  pl.BlockSpec((NDEV, FM, SW), lambda i: (0, i, 0)),
                  pl.BlockSpec((NDEV, HALO, SW), lambda i: (0, after(i), 0)),
                  _rows(FM, D), _resident((NDEV, SW, D)), _resident((2, NG, 3, SW)), _resident((NG, SW, D)), _const((1, D))],
        out_specs=[_rows(FM, D), pl.BlockSpec((NDEV, FM, SW), lambda i: (0, i, 0)), _const((2, NG, 8, SW)), _const((8, D))],
        out_shape=[_sds((t, D)), _sds((NDEV, t, SW), BF16), _sds((2, NG, 8, SW)), _sds((8, D))],
        compiler_params=_params())(dy, dy, a, gs, gs, x2, up, conv, down, ffn_norm)


BT = 512


def _xattn_bwd(dx2, x1, qpre, k2, v2, wq, wo, qn4, xan):
    t = x1.shape[0]

    def body(dx2_ref, x1_ref, qpre_ref, k_ref, v_ref, wq_ref, wo_ref, qn_ref, xan_ref,
             dx1_ref, dqpre_ref, dk_ref, dv_ref, dqn_ref, dxan_ref):
        @pl.when(pl.program_id(0) == 0)
        def _():
            for r in (dk_ref, dv_ref, dqn_ref, dxan_ref):
                r[...] = jnp.zeros_like(r)

        dx2 = dx2_ref[...]
        do = _dot_nt(dx2.astype(BF16), wo_ref[...])
        dqs = []
        for h in range(XH):
            sl = slice(XD * h, XD * (h + 1))
            qpre_h = qpre_ref[:, sl]
            pn, q2, rq = _xattn_probs(qpre_h, qn_ref[:, sl], k_ref[:, sl])
            do_h = do[:, sl].astype(BF16)
            dp = _dot_nt(do_h, v_ref[:, sl])
            ds = (pn * (dp - jnp.sum(pn * dp, axis=-1, keepdims=True)) * (1.0 / math.sqrt(XD))).astype(BF16)
            dq2 = _dot(ds, k_ref[:, sl])
            dk_ref[:, sl] += _dot_tn(ds, q2)
            dv_ref[:, sl] += _dot_tn(pn.astype(BF16), do_h)
            dqh, dg = _rms_bwd(dq2, qpre_h, rq, qn_ref[:, sl])
            dqn_ref[...] += _colsum8(dg)
            dqs.append(dqh)
        dqpre = jnp.concatenate(dqs, axis=1).astype(BF16)
        dqpre_ref[...] = dqpre
        dh2 = _dot_nt(dqpre, wq_ref[...])
        x1 = x1_ref[...]
        dx, dg = _rms_bwd(dh2, x1, _rs(x1), xan_ref[...])
        dx1_ref[...] = dx2 + dx
        dxan_ref[...] += _colsum8(dg)

    return pl.pallas_call(
        body, name="xattn_bwd", grid=(t // BT,),
        in_specs=[_rows(BT, D), _rows(BT, D), _rows(BT, D), _const((MEM, D)), _const((MEM, D)), _const((D, D)),
                  _const((D, D)), _const((1, D)), _const((1, D))],
        out_specs=[_rows(BT, D), _rows(BT, D), _const((MEM, D)), _const((MEM, D)), _const((8, XD)), _const((8, D))],
        out_shape=[_sds((t, D)), _sds((t, D), BF16), _sds((MEM, D)), _sds((MEM, D)), _sds((8, XD)), _sds((8, D))],
        compiler_params=_params())(dx2, x1, qpre, k2, v2, wq, wo, qn4, xan)


def _mem_kv_bwd(mem, mh, kpre, dk2, dv2, wkv, kn4, mem_norm):
    def body(m_ref, mh_ref, kpre_ref, dk_ref, dv_ref, w_ref, kn_ref, g_ref, dw_ref, dkn_ref, dmn_ref):
        dkn = jnp.zeros((8, XD), F32)
        dm = jnp.zeros((MEM, D), F32)
        mh = mh_ref[...]
        for h in range(XH):
            sl = slice(XD * h, XD * (h + 1))
            kh = kpre_ref[:, sl]
            dkh, dg = _rms_bwd(dk_ref[:, sl], kh, _rs(kh), kn_ref[:, sl])
            dkn = dkn + _colsum8(dg)
            dkh = dkh.astype(BF16)
            dvh = dv_ref[:, sl].astype(BF16)
            dw_ref[h] = _dot_tn(mh, dkh).astype(BF16)
            dw_ref[XH + h] = _dot_tn(mh, dvh).astype(BF16)
            dm = dm + _dot_nt(dkh, w_ref[h]) + _dot_nt(dvh, w_ref[XH + h])
        dkn_ref[...] = dkn
        m = m_ref[...]
        _, dg = _rms_bwd(dm, m, _rs(m), g_ref[...])
        dmn_ref[...] = _colsum8(dg)

    return pl.pallas_call(
        body, name="mem_kv_bwd", out_shape=[_sds((2 * XH, D, XD), BF16), _sds((8, XD)), _sds((8, D))],
        compiler_params=pltpu.CompilerParams(vmem_limit_bytes=VMEM_LIMIT))(mem, mh, kpre, dk2, dv2, wkv, kn4, mem_norm)


def _mixer_out_bwd(dx1, attn, gm, w_out, aon, gon, gu, gvn, gz, ws, wst, bfull, gvw):
    t = dx1.shape[0]
    nc = TM // BLK

    def body(dx_ref, a_ref, g_ref, wo_ref, aon_ref, gon_ref, gu_ref, x_ref, gz_ref, w_ref, wt_ref, b_ref, gvw_ref,
             da_ref, dan_ref, dgn_ref, dgz_ref, dw_ref, db_ref, dgvw_ref, dbacc_ref):
        @pl.when(pl.program_id(0) == 0)
        def _():
            for r in (dan_ref, dgn_ref, dw_ref, dbacc_ref, dgvw_ref):
                r[...] = jnp.zeros_like(r)

        dy = _dot_nt(dx_ref[...].astype(BF16), wo_ref[...])
        a, g = a_ref[...], g_ref[...]
        da, dna = _rms_bwd(dy[:, :AW], a, _rs(a), aon_ref[...])
        dgm, dng = _rms_bwd(dy[:, AW:], g, _rs(g), gon_ref[...])
        da_ref[...] = da
        dan_ref[...] += _colsum8(dna)
        dgn_ref[...] += _colsum8(dng)

        xin = x_ref[...]
        mixed = _gmlp_mix(w_ref, xin, False) + jnp.tile(b_ref[...], (nc, 1))
        dgu = dgm * mixed
        dmixed = dgm * gu_ref[...]
        lo = _lane((BLK, 128)) < 64
        dbias = jnp.zeros((BLK, GW), F32)
        for c in range(nc):
            dmc = dmixed[c * BLK:(c + 1) * BLK]
            dbias = dbias + dmc
            for j in range(4):
                dm2 = dmc[:, 128 * j:128 * (j + 1)]
                xs = xin[c * BLK:(c + 1) * BLK, 128 * j:128 * (j + 1)]
                z = jnp.zeros_like(dm2)
                dw_ref[2 * j] += _dot_nt(jnp.where(lo, dm2, z).astype(BF16), xs)
                dw_ref[2 * j + 1] += _dot_nt(jnp.where(lo, z, dm2).astype(BF16), xs)
        dbacc_ref[...] += dbias
        dgvn = _gmlp_mix(wt_ref, dmixed.astype(BF16), True)
        gz_u, gz_v = gz_ref[:, :GW], gz_ref[:, GW:]
        gv = _gelu(gz_v)
        dgv, dg = _rms_bwd(dgvn, gv, _rs(gv), gvw_ref[...])
        dgvw_ref[...] += _colsum8(dg)
        dgz_ref[:, :GW] = (dgu * _gelu_grad(gz_u)).astype(BF16)
        dgz_ref[:, GW:] = (dgv * _gelu_grad(gz_v)).astype(BF16)

        @pl.when(pl.program_id(0) == pl.num_programs(0) - 1)
        def _():
            s = dbacc_ref[...]
            sel = (lax.broadcasted_iota(jnp.int32, (8, GW), 1) // HD
                   == lax.broadcasted_iota(jnp.int32, (8, GW), 0)).astype(BF16)
            hi = s.astype(BF16)
            r1 = s - hi.astype(F32)
            mid = r1.astype(BF16)
            lo = (r1 - mid.astype(F32)).astype(BF16)
            db_ref[...] = _dot_nt(sel, hi) + _dot_nt(sel, mid) + _dot_nt(sel, lo)
            r = lax.broadcasted_iota(jnp.int32, (BLK, BLK), 0)
            c = lax.broadcasted_iota(jnp.int32, (BLK, BLK), 1)
            for h in range(8):
                dw_ref[h] = jnp.where(c <= r, dw_ref[h], 0.0)

    return pl.pallas_call(
        body, name="mixer_out_bwd", grid=(t // TM,),
        in_specs=[_rows(TM, D), _rows(TM, AW), _rows(TM, GW), _const((D, D)), _const((1, AW)), _const((1, GW)),
                  _rows(TM, GW), _rows(TM, GW), _rows(TM, 2 * GW), _const((8, BLK, BLK)), _const((8, BLK, BLK)),
                  _const((BLK, GW)), _const((1, GW))],
        out_specs=[_rows(TM, AW), _const((8, AW)), _const((8, GW)), _rows(TM, 2 * GW), _const((8, BLK, BLK)),
                   _const((8, BLK)), _const((8, GW))],
        out_shape=[_sds((t, AW)), _sds((8, AW)), _sds((8, GW)), _sds((t, 2 * GW), BF16), _sds((8, BLK, BLK)),
                   _sds((8, BLK)), _sds((8, GW))],
        scratch_shapes=[pltpu.VMEM((BLK, GW), F32)],
        compiler_params=_params())(dx1, attn, gm, w_out, aon, gon, gu, gvn, gz, ws, wst, bfull, gvw)


def _fold_half(v):
    return v + pltpu.roll(v, 64, 1)


def _swa_bwd(q, k, v, dattn, sinks):
    t = q.shape[0]
    nb = t // BLK
    ts = min(t, SB_BWD * BLK)
    sb = ts // BLK
    nt = t // ts

    def body(sink_ref, q_ref, kc_ref, kp_ref, vc_ref, vp_ref, do_ref, dq_ref, dk_ref, dv_ref, ds_ref,
             ck_ref, cv_ref, sacc_ref):
        i = pl.program_id(0)

        @pl.when(i == 0)
        def _():
            ck_ref[...] = jnp.zeros_like(ck_ref)
            cv_ref[...] = jnp.zeros_like(cv_ref)
            sacc_ref[...] = jnp.zeros_like(sacc_ref)

        @pl.when(i < nt)
        def _():
            kk = jnp.concatenate([kp_ref[...], kc_ref[...]], axis=0).astype(F32)
            vv = jnp.concatenate([vp_ref[...], vc_ref[...]], axis=0).astype(F32)
            lo256 = _lane((2 * BLK, 128)) < 64
            acc_k = [jnp.zeros((BLK, 128), F32) for _ in range(sb + 1)]
            acc_v = [jnp.zeros((BLK, 128), F32) for _ in range(sb + 1)]
            for b in range(sb):
                r = slice(b * BLK, (b + 1) * BLK)
                kb, vb = kk[b * BLK:(b + 2) * BLK], vv[b * BLK:(b + 2) * BLK]
                mask = _swa_mask(i == 0) if b == 0 else _swa_mask(False)
                dkk = jnp.zeros((2 * BLK, 128), F32)
                dvv = jnp.zeros((2 * BLK, 128), F32)
                for g in range(2):
                    qs = _stack_heads(q_ref[r, 256 * g:256 * g + 128], q_ref[r, 256 * g + 128:256 * g + 256])
                    dos = _stack_heads(do_ref[r, 256 * g:256 * g + 128],
                                       do_ref[r, 256 * g + 128:256 * g + 256]).astype(BF16)
                    kd = _dup_half(kb, g)
                    pn, psn = _swa_probs(qs, kd, mask, _sink_col(sink_ref, g))
                    dp = _dot_nt(dos, _dup_half(vb, g))
                    dd = jnp.sum(pn * dp, axis=-1, keepdims=True)
                    ds = (pn * (dp - dd) * (1.0 / math.sqrt(HD))).astype(BF16)
                    sacc_ref[g] += jnp.broadcast_to(-psn * dd, (4 * BLK, 128))
                    dqa, dqb = _unstack_heads(_dot(ds, kd))
                    dq_ref[r, 256 * g:256 * g + 128] = dqa
                    dq_ref[r, 256 * g + 128:256 * g + 256] = dqb
                    dkg = _fold_half(_dot_tn(ds, qs))
                    dvg = _fold_half(_dot_tn(pn.astype(BF16), dos))
                    keep = lo256 if g == 0 else jnp.logical_not(lo256)
                    dkk = jnp.where(keep, dkg, dkk)
                    dvv = jnp.where(keep, dvg, dvv)
                acc_k[b], acc_k[b + 1] = acc_k[b] + dkk[0:BLK], acc_k[b + 1] + dkk[BLK:]
                acc_v[b], acc_v[b + 1] = acc_v[b] + dvv[0:BLK], acc_v[b + 1] + dvv[BLK:]
            for out_ref, c_ref, acc in ((dk_ref, ck_ref, acc_k), (dv_ref, cv_ref, acc_v)):
                if sb > 1:
                    out_ref[0:ts - BLK] = c_ref[0:ts - BLK]
                out_ref[ts - BLK:ts] = c_ref[ts - BLK:ts] + acc[0]
                for b in range(sb):
                    c_ref[b * BLK:(b + 1) * BLK] = acc[b + 1]

        @pl.when(i == nt)
        def _():
            dk_ref[...] = ck_ref[...]
            dv_ref[...] = cv_ref[...]
            lane = _lane((8, 128))
            acc = jnp.zeros((8, 128), F32)
            for g in range(2):
                for j in range(4):
                    val = jnp.sum(sacc_ref[g, j * BLK:(j + 1) * BLK, :], axis=0, keepdims=True)
                    acc = jnp.where(lane == 4 * g + j, jnp.broadcast_to(val, (8, 128)), acc)
            ds_ref[...] = acc

    cur = lambda i: (jnp.minimum(i, nt - 1), 0)
    before = lambda i: (jnp.clip(i * sb - 1, 0, nb - 1), 0)
    done = lambda i: (jnp.clip(i - 1, 0, nt - 1), 0)
    return pl.pallas_call(
        body, name="swa_bwd", grid=(nt + 1,),
        in_specs=[pl.BlockSpec(memory_space=pltpu.SMEM), pl.BlockSpec((ts, AW), cur),
                  pl.BlockSpec((ts, KW), cur), pl.BlockSpec((BLK, KW), before),
                  pl.BlockSpec((ts, KW), cur), pl.BlockSpec((BLK, KW), before), pl.BlockSpec((ts, AW), cur)],
        out_specs=[pl.BlockSpec((ts, AW), cur), pl.BlockSpec((ts, KW), done), pl.BlockSpec((ts, KW), done),
                   _const((8, 128))],
        out_shape=[_sds((t, AW)), _sds((t, KW)), _sds((t, KW)), _sds((8, 128))],
        scratch_shapes=[pltpu.VMEM((ts, KW), F32), pltpu.VMEM((ts, KW), F32), pltpu.VMEM((2, 4 * BLK, 128), F32)],
        compiler_params=_params())(sinks, q, k, k, v, v, dattn)


def _mixer_in_bwd(dq, dk, dv, dgz, qk, cos, sin, x, dx1, w_in, mix_norm, qn, kn):
    t = x.shape[0]

    def body(dq_ref, dk_ref, dv_ref, dgz_ref, qk_ref, cos_ref, sin_ref, x_ref, dx1_ref, w_ref, g_ref, qn_ref, kn_ref,
             ones_ref, gx_ref, dproj_ref, dmn_ref, dqn_ref, dkn_ref, qacc_ref, kacc_ref):
        i = pl.program_id(0)

        @pl.when(i == 0)
        def _():
            dmn_ref[...] = jnp.zeros_like(dmn_ref)
            qacc_ref[...] = jnp.zeros_like(qacc_ref)
            kacc_ref[...] = jnp.zeros_like(kacc_ref)

        cos2, sin2 = cos_ref[...], sin_ref[...]
        qpre, kpre = qk_ref[:, :AW], qk_ref[:, AW:]
        dqh = _rope_bwd(dq_ref[...], jnp.tile(cos2, (1, 4)), jnp.tile(sin2, (1, 4)))
        dqpre, dgq = _rms64_bwd(dqh, qpre, _rs64(qpre, ones_ref), qn_ref[...], ones_ref)
        dkh = _rope_bwd(dk_ref[...], cos2, sin2)
        dkpre, dgk = _rms64_bwd(dkh, kpre, _rs64(kpre, ones_ref), kn_ref[...], ones_ref)
        qacc_ref[...] += jnp.sum(dgq, axis=0, keepdims=True)
        kacc_ref[...] += jnp.sum(dgk, axis=0, keepdims=True)
        dproj = jnp.concatenate([dqpre.astype(BF16), dkpre.astype(BF16), dv_ref[...].astype(BF16), dgz_ref[...]], axis=1)
        dproj_ref[...] = dproj
        dh = _dot(dproj, w_ref[...])
        xv = x_ref[...]
        dx, dg = _rms_bwd(dh, xv, _rs(xv), g_ref[...])
        gx_ref[...] = dx1_ref[...] + dx
        dmn_ref[...] += _colsum8(dg)

        @pl.when(i == pl.num_programs(0) - 1)
        def _():
            qa = qacc_ref[...]
            q4 = qa[:, 0:128] + qa[:, 128:256] + qa[:, 256:384] + qa[:, 384:512]
            dqn_ref[...] = jnp.broadcast_to(_fold_half(q4), (8, 128))
            dkn_ref[...] = jnp.broadcast_to(_fold_half(kacc_ref[...]), (8, 128))

    return pl.pallas_call(
        body, name="mixer_in_bwd", grid=(t // TM,),
        in_specs=[_rows(TM, AW), _rows(TM, KW), _rows(TM, KW), _rows(TM, 2 * GW), _rows(TM, AW + KW), _rows(TM, 128),
                  _rows(TM, 128), _rows(TM, D), _rows(TM, D), _const((IN, D)), _const((1, D)), _const((1, AW)),
                  _const((1, KW)), _const((AW, AW))],
        out_specs=[_rows(TM, D), _rows(TM, IN), _const((8, D)), _const((8, 128)), _const((8, 128))],
        out_shape=[_sds((t, D)), _sds((t, IN), BF16), _sds((8, D)), _sds((8, 128)), _sds((8, 128))],
        scratch_shapes=[pltpu.VMEM((1, AW), F32), pltpu.VMEM((1, KW), F32)],
        compiler_params=_params())(dq, dk, dv, dgz, qk, cos, sin, x, dx1, w_in, mix_norm, qn, kn, _head_ones())


def _local_step(x, mem, pos, target, p, fetch, ship):
    t = x.shape[0]
    p = dict(p)
    p.update(fetch(0, None))
    inv_freq = 1.0 / (ROPE_THETA ** (jnp.arange(HD // 2, dtype=F32) * (2.0 / HD)))
    cos, sin = _rope_tables(pos, jnp.tile(inv_freq, 4).reshape(1, 128))
    qn = jnp.tile(p["q_norm"], (1, AW // HD))
    kn = jnp.tile(p["k_norm"], (1, KW // HD))
    qn4 = jnp.tile(p["xa_q_norm"], (1, XH))
    kn4 = jnp.tile(p["xa_k_norm"], (1, XH))
    ws = p["gmlp_ws"]
    wst = jnp.swapaxes(ws, 1, 2)
    bfull = jnp.repeat(p["gmlp_bs"].T, HD, axis=1)
    conv_b = p["ffn_conv_b"]

    h1, qk, gz, q, k, v, gu, gvn = _mixer_in_fwd(x, p["mix_norm"], p["w_in"], qn, kn, p["gmlp_v_norm"], cos, sin)
    attn = _swa_fwd(q, k, v, p["attn_sinks"])
    p.update(fetch(1, attn))
    gm, ycat, x1, h2 = _mixer_out_fwd(attn, gvn, gu, ws, bfull, x, p["w_out"], p["attn_out_norm"], p["gmlp_out_norm"],
                                      p["xa_norm"])
    mh, kpre, k2, v2 = _mem_kv_fwd(mem, p["mem_norm"], p["xa_wkv"], kn4)
    qpre, o, x2, h3 = _xattn_fwd(h2, x1, p["xa_wq"], qn4, k2, v2, p["xa_wo"], p["ffn_norm"])
    p.update(fetch(2, h3))
    conv = p["ffn_conv"]
    a, u, gs, dy, loss8 = _ffn_fwd(h3, x2, target, p["ffn_up"], conv, conv_b, p["ffn_down"])

    raw = {}
    d_down = _mm_tn(u, dy, "ffn_down_bwd_w")
    dx2, da, raw["conv_sums"], raw["ffn_norm"] = _ffn_bwd(dy, a, gs, x2, p["ffn_up"], conv, p["ffn_down"], p["ffn_norm"])
    d_up = _mm_tn(da, h3, "ffn_up_bwd_w")
    token = ship(0, {"ffn_down": d_down, "ffn_up": d_up, "ffn_conv": raw["conv_sums"][:, :, 0:3]})
    dx1, dqpre, dk2, dv2, raw["xa_q_norm"], raw["xa_norm"] = _xattn_bwd(
        dx2, x1, qpre, k2, v2, p["xa_wq"], p["xa_wo"], qn4 + jnp.tile(token[0:1], (1, D // 128)), p["xa_norm"])
    d_wo = _mm_tn(o, dx2, "xa_wo_bwd_w")
    d_wq = _mm_tn(h2, dqpre, "xa_wq_bwd_w")
    d_wkv, raw["xa_k_norm"], raw["mem_norm"] = _mem_kv_bwd(mem, mh, kpre, dk2, dv2, p["xa_wkv"], kn4, p["mem_norm"])
    d_w_out = _mm_tn(ycat, dx1, "w_out_bwd_w")
    (dattn, raw["attn_out_norm"], raw["gmlp_out_norm"], dgz, raw["gmlp_ws"], raw["gmlp_bs"],
     raw["gmlp_v_norm"]) = _mixer_out_bwd(dx1, attn, gm, p["w_out"], p["attn_out_norm"], p["gmlp_out_norm"],
                                          gu, gvn, gz, ws, wst, bfull, p["gmlp_v_norm"])
    token = ship(1, {"xa_wo": d_wo, "xa_wq": d_wq, "xa_wkv": d_wkv, "w_out": d_w_out}, [raw["gmlp_ws"]])
    dq, dk, dv, raw["attn_sinks"] = _swa_bwd(q, k, v, dattn, p["attn_sinks"] + token[0:1, 0:8])
    grad_x, dproj, raw["mix_norm"], raw["q_norm"], raw["k_norm"] = _mixer_in_bwd(
        dq, dk, dv, dgz, qk, cos, sin, x, dx1, p["w_in"], p["mix_norm"], qn, kn)
    d_w_in = _mm_tn(dproj, h1, "w_in_bwd_w")
    raw["loss"] = loss8
    return grad_x, {"w_in": d_w_in}, raw


def _cast_shards(shards):
    def body(*refs):
        n = len(refs) // 2
        for i_ref, o_ref in zip(refs[:n], refs[n:]):
            o_ref[...] = i_ref[...].astype(BF16)

    return pl.pallas_call(body, name="cast_shards", out_shape=[_sds(s.shape, BF16) for s in shards],
                          compiler_params=pltpu.CompilerParams(vmem_limit_bytes=VMEM_LIMIT))(*shards)


HBM_SPEC = pl.BlockSpec(memory_space=pltpu.HBM)
SEM_SPEC = pl.BlockSpec(memory_space=pltpu.SEMAPHORE)


ALL_K = tuple(range(1, NDEV))
CHIP_K = (1, 2, 4, 6)
RELAY_K = (2, 4, 6)


def _peer(k):
    x, y, cc = lax.axis_index("x"), lax.axis_index("y"), lax.axis_index("c")
    return 1 - x if k & 4 else x, 1 - y if k & 2 else y, 1 - cc if k & 1 else cc


def _remote_copies(src_refs, land_refs, send_refs, recv_refs, nd, ks):
    me = 4 * lax.axis_index("x") + 2 * lax.axis_index("y") + lax.axis_index("c")
    copies = []
    for a, (src_ref, land_ref) in enumerate(zip(src_refs, land_refs)):
        for j, k in enumerate(ks):
            px, py, pc = _peer(k)
            copies.append((k, pltpu.make_async_remote_copy(
                src_ref=src_ref.at[4 * px + 2 * py + pc] if a < nd else src_ref, dst_ref=land_ref.at[me],
                send_sem=send_refs[a].at[j], recv_sem=recv_refs[a].at[j],
                device_id=(px, py, pc), device_id_type=pl.DeviceIdType.MESH)))
    return copies


def _relay_copies(land_refs, send_refs, recv_refs):
    copies = []
    for a, land_ref in enumerate(land_refs):
        for j, k in enumerate(RELAY_K):
            px, py, pc = _peer(k)
            slot = land_ref.at[4 * px + 2 * py + pc]
            copies.append(pltpu.make_async_remote_copy(
                src_ref=slot, dst_ref=slot, send_sem=send_refs[a].at[j], recv_sem=recv_refs[a].at[j],
                device_id=_peer(1), device_id_type=pl.DeviceIdType.MESH))
    return copies


def _own_slot(src, by_dest, me):
    block = lax.dynamic_index_in_dim(src, me, 0, keepdims=True) if by_dest else src[None]
    return lax.dynamic_update_index_in_dim(lax.empty((NDEV,) + block.shape[1:], src.dtype), block, me, 0)


SIDE_EFFECT = pltpu.CompilerParams(has_side_effects=pltpu.SideEffectType.DATAFLOW_SIDE_EFFECTING)


def _exchange_start(by_dest, for_all, me, name, ks=ALL_K):
    srcs = list(by_dest) + list(for_all)
    n, nd = len(srcs), len(by_dest)
    lands = [_own_slot(s, a < nd, me) for a, s in enumerate(srcs)]

    def body(*refs):
        for _, cp in _remote_copies(refs[:n], refs[n:2 * n], refs[2 * n:3 * n], refs[3 * n:4 * n], nd, ks):
            cp.start()
        refs[-1][...] = jnp.zeros((8, 128), F32)

    sems = [pltpu.SemaphoreType.DMA((len(ks),))] * (2 * n)
    thru = [pltpu.HBM(v.shape, v.dtype) for v in srcs + lands]
    res = pl.pallas_call(
        body, name=name, out_shape=sems + thru + [_sds((8, 128))],
        in_specs=[HBM_SPEC] * (2 * n), out_specs=[SEM_SPEC] * (2 * n) + [HBM_SPEC] * (2 * n) + [pl.BlockSpec(memory_space=pltpu.VMEM)],
        input_output_aliases={i: 2 * n + i for i in range(2 * n)}, compiler_params=SIDE_EFFECT)(
            *[pltpu.with_memory_space_constraint(v, pltpu.HBM) for v in srcs + lands])
    return (res[:2 * n], res[2 * n:4 * n], nd, ks, None), res[-1]


def _exchange_relay(state, after, name):
    sems, thru, nd, ks, _ = state
    n = len(thru) // 2

    def body(*refs):
        for k, cp in _remote_copies(refs[:n], refs[n:2 * n], refs[2 * n:3 * n], refs[3 * n:4 * n], nd, ks):
            if k in RELAY_K:
                cp.wait_recv()
        for cp in _relay_copies(refs[n:2 * n], refs[4 * n + 1:5 * n + 1], refs[5 * n + 1:6 * n + 1]):
            cp.start()

    relay_sems = [pltpu.SemaphoreType.DMA((len(RELAY_K),))] * (2 * n)
    res = pl.pallas_call(
        body, name=name, out_shape=relay_sems + [pltpu.HBM(v.shape, v.dtype) for v in thru],
        in_specs=[HBM_SPEC] * (2 * n) + [SEM_SPEC] * (2 * n) + [pl.BlockSpec(memory_space=pl.ANY)],
        out_specs=[SEM_SPEC] * (2 * n) + [HBM_SPEC] * (2 * n),
        input_output_aliases={i: 2 * n + i for i in range(2 * n)}, compiler_params=SIDE_EFFECT)(*thru, *sems, after)
    return sems, res[2 * n:], nd, ks, res[:2 * n]


def _exchange_wait(state, after, name):
    sems, thru, nd, ks, relay_sems = state
    n = len(thru) // 2

    def body(*refs):
        for k, cp in _remote_copies(refs[:n], refs[n:2 * n], refs[2 * n:3 * n], refs[3 * n:4 * n], nd, ks):
            cp.wait_send()
            if relay_sems is None or k not in RELAY_K:
                cp.wait_recv()
        if relay_sems is not None:
            for cp in _relay_copies(refs[n:2 * n], refs[4 * n:5 * n], refs[5 * n:6 * n]):
                cp.wait_send()
                cp.wait_recv()

    extra = [] if relay_sems is None else list(relay_sems)
    res = pl.pallas_call(
        body, name=name, out_shape=[pltpu.HBM(v.shape, v.dtype) for v in thru],
        in_specs=[HBM_SPEC] * (2 * n) + [SEM_SPEC] * (2 * n + len(extra)) + [pl.BlockSpec(memory_space=pl.ANY)],
        out_specs=[HBM_SPEC] * (2 * n), input_output_aliases={i: i for i in range(2 * n)}, compiler_params=SIDE_EFFECT)(
            *thru, *sems, *extra, after)
    return res[n:]


def _adam(items, name):
    n = len(items)

    def body(*refs):
        for j in range(n):
            p_ref, w_ref, m_ref, v_ref = refs[4 * j:4 * j + 4]
            g_ref, d_ref, nm_ref, nv_ref = refs[4 * n + 4 * j:4 * n + 4 * j + 4]
            g = _sum_parts(p_ref)
            g_ref[...] = g
            d_ref[...], nm_ref[...], nv_ref[...] = _adam_math(g, w_ref[...], m_ref[...], v_ref[...])

    res = pl.pallas_call(
        body, name=name, out_shape=[_sds(it[1].shape) for it in items for _ in range(4)],
        compiler_params=pltpu.CompilerParams(vmem_limit_bytes=VMEM_LIMIT))(*[a for it in items for a in it])
    return [res[4 * j:4 * j + 4] for j in range(n)]


GATHER_GROUPS = (("w_in",), ("w_out", "xa_wkv", "xa_wq", "xa_wo"), ("ffn_up", "ffn_conv", "ffn_down"))
SCATTER_GROUPS = (("ffn_up", "ffn_down", "ffn_conv"), ("xa_wo", "xa_wq", "xa_wkv", "w_out"), ("w_in",))
ADAM_ALONE = ("ffn_up",)
BIG = tuple(n for grp in GATHER_GROUPS for n in grp)
BY_COLUMN = ("w_in", "ffn_up")
VECS = (("mix_norm", D), ("q_norm", HD), ("k_norm", HD), ("attn_sinks", 8), ("gmlp_v_norm", GW), ("attn_out_norm", AW),
        ("gmlp_out_norm", GW), ("xa_norm", D), ("mem_norm", D), ("xa_q_norm", XD), ("xa_k_norm", XD), ("ffn_norm", D))
BS_ROW = 16
VEC_ROWS = 24
SMALL = tuple(n for n, _ in VECS) + ("gmlp_bs", "gmlp_ws", "ffn_conv_b")


def _pack_small(raw):
    names = [n for n, _ in VECS] + ["gmlp_bs", "conv_sums"]

    def body(*refs):
        ins = dict(zip(names, refs))
        vec_ref, cb_ref = refs[len(names):]
        vec_ref[...] = jnp.zeros_like(vec_ref)
        for r, (n, w) in enumerate(VECS):
            vec_ref[r:r + 1, 0:w] = ins[n][0:1, 0:w]
        vec_ref[BS_ROW:BS_ROW + 8, 0:BLK] = ins["gmlp_bs"][...]
        for s in range(2):
            for d in range(NG):
                cb_ref[s, d] = ins["conv_sums"][s, d, 3:4, :]

    return pl.pallas_call(body, name="pack_small", out_shape=[_sds((VEC_ROWS, D)), _sds((2, NG, 1, SW))])(
        *[raw[n] for n in names])


def _adam_math(g, w, m, v):
    nm = B1 * m + (1.0 - B1) * g
    nv = B2 * v + (1.0 - B2) * (g * g)
    m_hat = nm / (1.0 - B1 ** STEP)
    v_hat = nv / (1.0 - B2 ** STEP)
    return -LR * (m_hat / (jnp.sqrt(v_hat) + AEPS) + WD * w), nm, nv


def _sum_parts(p_ref):
    g = p_ref[0].astype(F32)
    for j in range(1, NDEV):
        g = g + p_ref[j].astype(F32)
    return g


def _adam_small(parts_vec, parts_ws, parts_cb, w, m, v):
    def body(*refs):
        pv_ref, pws_ref, pcb_ref = refs[:3]
        ins = refs[3:3 + 3 * len(SMALL)]
        outs = refs[3 + 3 * len(SMALL):]
        gv = _sum_parts(pv_ref)
        for j, n in enumerate(SMALL):
            w_ref, m_ref, v_ref = ins[3 * j:3 * j + 3]
            o = outs[4 * j:4 * j + 4]
            if n == "gmlp_ws":
                g = _sum_parts(pws_ref)
            elif n == "ffn_conv_b":
                g = _sum_parts(pcb_ref)
            elif n == "gmlp_bs":
                g = gv[BS_ROW:BS_ROW + 8, 0:BLK]
            else:
                g = gv[j:j + 1, 0:VECS[j][1]]
            lead = n in ("gmlp_ws", "gmlp_bs")
            res = (g,) + _adam_math(g, w_ref[0] if lead else w_ref[...], m_ref[0] if lead else m_ref[...],
                                    v_ref[0] if lead else v_ref[...])
            for o_ref, val in zip(o, res):
                if lead:
                    o_ref[0] = val
                else:
                    o_ref[...] = val

    args = [parts_vec, parts_ws, parts_cb] + [d[n] for n in SMALL for d in (w, m, v)]
    res = pl.pallas_call(body, name="adam_small", out_shape=[_sds(w[n].shape) for n in SMALL for _ in range(4)],
                         compiler_params=pltpu.CompilerParams(vmem_limit_bytes=VMEM_LIMIT))(*args)
    return {n: tuple(res[4 * j:4 * j + 4]) for j, n in enumerate(SMALL)}


def kernel(x, mem, positions, mix_norm, w_in, q_norm, k_norm, attn_sinks, gmlp_v_norm, gmlp_ws, gmlp_bs, attn_out_norm, gmlp_out_norm, w_out, xa_norm, mem_norm, xa_wq, xa_wkv, xa_q_norm, xa_k_norm, xa_wo, ffn_norm, ffn_up, ffn_conv, ffn_conv_b, ffn_down, loss_target, m_mix_norm, m_w_in, m_q_norm, m_k_norm, m_attn_sinks, m_gmlp_v_norm, m_gmlp_ws, m_gmlp_bs, m_attn_out_norm, m_gmlp_out_norm, m_w_out, m_xa_norm, m_mem_norm, m_xa_wq, m_xa_wkv, m_xa_q_norm, m_xa_k_norm, m_xa_wo, m_ffn_norm, m_ffn_up, m_ffn_conv, m_ffn_conv_b, m_ffn_down, v_mix_norm, v_w_in, v_q_norm, v_k_norm, v_attn_sinks, v_gmlp_v_norm, v_gmlp_ws, v_gmlp_bs, v_attn_out_norm, v_gmlp_out_norm, v_w_out, v_xa_norm, v_mem_norm, v_xa_wq, v_xa_wkv, v_xa_q_norm, v_xa_k_norm, v_xa_wo, v_ffn_norm, v_ffn_up, v_ffn_conv, v_ffn_conv_b, v_ffn_down):
    names = ("mix_norm", "w_in", "q_norm", "k_norm", "attn_sinks", "gmlp_v_norm", "gmlp_ws", "gmlp_bs", "attn_out_norm",
             "gmlp_out_norm", "w_out", "xa_norm", "mem_norm", "xa_wq", "xa_wkv", "xa_q_norm", "xa_k_norm", "xa_wo",
             "ffn_norm", "ffn_up", "ffn_conv", "ffn_conv_b", "ffn_down")
    w = dict(zip(names, (mix_norm, w_in, q_norm, k_norm, attn_sinks, gmlp_v_norm, gmlp_ws, gmlp_bs, attn_out_norm,
                         gmlp_out_norm, w_out, xa_norm, mem_norm, xa_wq, xa_wkv, xa_q_norm, xa_k_norm, xa_wo, ffn_norm,
                         ffn_up, ffn_conv, ffn_conv_b, ffn_down)))
    m = dict(zip(names, (m_mix_norm, m_w_in, m_q_norm, m_k_norm, m_attn_sinks, m_gmlp_v_norm, m_gmlp_ws, m_gmlp_bs,
                         m_attn_out_norm, m_gmlp_out_norm, m_w_out, m_xa_norm, m_mem_norm, m_xa_wq, m_xa_wkv,
                         m_xa_q_norm, m_xa_k_norm, m_xa_wo, m_ffn_norm, m_ffn_up, m_ffn_conv, m_ffn_conv_b, m_ffn_down)))
    v = dict(zip(names, (v_mix_norm, v_w_in, v_q_norm, v_k_norm, v_attn_sinks, v_gmlp_v_norm, v_gmlp_ws, v_gmlp_bs,
                         v_attn_out_norm, v_gmlp_out_norm, v_w_out, v_xa_norm, v_mem_norm, v_xa_wq, v_xa_wkv,
                         v_xa_q_norm, v_xa_k_norm, v_xa_wo, v_ffn_norm, v_ffn_up, v_ffn_conv, v_ffn_conv_b, v_ffn_down)))
    t = x.shape[1]

    me = 4 * lax.axis_index("x") + 2 * lax.axis_index("y") + lax.axis_index("c")

    def rows(a, n):
        return jnp.swapaxes(a[0], 0, 1) if n in BY_COLUMN else a[0]

    mats = [n for n in BIG if n != "ffn_conv"]
    shard = dict(zip(mats, _cast_shards([rows(w[n], n) for n in mats])), ffn_conv=w["ffn_conv"][0])
    gathers, tokens = zip(*[_exchange_start([], [shard[n] for n in grp], me, "gather_start_%d" % i,
                                            CHIP_K if i == len(GATHER_GROUPS) - 1 else ALL_K)
                            for i, grp in enumerate(GATHER_GROUPS)])

    def fetch(i, after):
        after = tokens[0] + tokens[1] + tokens[2] if after is None else after
        state = gathers[i]
        if state[3] == CHIP_K:
            state = _exchange_relay(state, after, "gather_relay_%d" % i)
        got = dict(zip(GATHER_GROUPS[i], _exchange_wait(state, after, "gather_wait_%d" % i)))
        if "w_in" in got:
            got["w_in"] = got["w_in"].reshape(IN, D)
        for n in ("w_out", "xa_wq", "xa_wo"):
            if n in got:
                got[n] = got[n].reshape(D, D)
        if "ffn_down" in got:
            got["ffn_down"] = got["ffn_down"].reshape(NG, SW, D)
            got["ffn_conv"] = got["ffn_conv"].reshape(2, NG, 3, SW)
        return got

    scatters = []

    def ship(i, grads, for_all=()):
        by_dest = [grads[n].reshape((NDEV,) + shard[n].shape) for n in SCATTER_GROUPS[i]]
        state, token = _exchange_start(by_dest, for_all, me, "scatter_start_%d" % i)
        scatters.append(state)
        return token

    conv_b = {k: d["ffn_conv_b"].reshape(NDEV, 1, SW) for k, d in (("w", w), ("m", m), ("v", v))}
    p = {n: w[n] for n in SMALL[:-1]}
    p["gmlp_ws"], p["gmlp_bs"] = w["gmlp_ws"][0], w["gmlp_bs"][0]
    p["ffn_conv_b"] = conv_b["w"].reshape(2, NG, 1, SW)
    grad_x, g, raw = _local_step(x[0], mem[0], positions.reshape(t, 1), loss_target[0], p, fetch, ship)
    loss = lax.psum(raw["loss"][0, 0], ("x", "y", "c"))

    vec, cb = _pack_small(raw)
    after = ship(2, g, [vec, cb.reshape(NDEV, 1, SW)])
    res, rest = {}, []
    for i, grp in enumerate(SCATTER_GROUPS):
        got = _exchange_wait(scatters[i], after, "scatter_wait_%d" % i)
        rest += got[len(grp):]
        parts = dict(zip(grp, got))
        for batch in ([n for n in grp if n in ADAM_ALONE], [n for n in grp if n not in ADAM_ALONE]):
            if batch:
                outs = _adam([(parts[n], rows(w[n], n), rows(m[n], n), rows(v[n], n)) for n in batch], "adam_" + batch[0])
                for n, out in zip(batch, outs):
                    res[n] = [jnp.swapaxes(o, 0, 1) if n in BY_COLUMN else o for o in out]
                    after = out[0]
    ws_parts, vec_parts, cb_parts = rest
    small = lambda d, k: {**{n: d[n] for n in SMALL[:-1]}, "ffn_conv_b": conv_b[k]}
    res.update(_adam_small(vec_parts, ws_parts, cb_parts, small(w, "w"), small(m, "m"), small(v, "v")))

    outs = [loss, grad_x[None]]
    for j in range(4):
        outs += [res[n][j].reshape(w[n].shape) for n in names]
    return tuple(outs)
```

```python
import math

import jax
import jax.numpy as jnp
from jax import lax
from jax.experimental import pallas as pl
from jax.experimental.pallas import tpu as pltpu

F32 = jnp.float32
BF16 = jnp.bfloat16

D = 1024
HD = 64
AW = 512
KW = 128
GW = 512
IN = AW + 2 * KW + 2 * GW
BLK = 128
MEM = 256
XH = 4
XD = 256
FF = 2816
EPS = 1e-6
ROPE_THETA = 10000.0
NDEV = 8
LR, B1, B2, AEPS, WD, STEP = 0.001, 0.9, 0.999, 1e-08, 0.01, 10

TM = 512
WK = 2048
VMEM_LIMIT = 56 * 1024 * 1024
NEG = float(jnp.finfo(jnp.float32).min)
GELU_C0 = math.sqrt(2.0 / math.pi)
GELU_C1 = 0.044715


def _dot(a, b):
    return jnp.dot(a, b, preferred_element_type=F32)


def _dot_nt(a, b):
    return lax.dot_general(a, b, (((1,), (1,)), ((), ())), preferred_element_type=F32)


def _dot_tn(a, b):
    return lax.dot_general(a, b, (((0,), (0,)), ((), ())), preferred_element_type=F32)


def _rs(x):
    return lax.rsqrt(jnp.mean(x * x, axis=-1, keepdims=True) + EPS)


def _rms_bwd(dy, x, r, g):
    xh = x * r
    dxh = dy * g
    dx = r * (dxh - xh * jnp.mean(dxh * xh, axis=-1, keepdims=True))
    return dx, dy * xh


def _lane(shape):
    return lax.broadcasted_iota(jnp.int32, shape, len(shape) - 1)


def _gsum64(v, ones_ref):
    w = v.shape[-1]
    ones = ones_ref[0:w, 0:w]
    hi = v.astype(BF16)
    lo = (v - hi.astype(F32)).astype(BF16)
    return _dot(hi, ones) + _dot(lo, ones)


def _head_ones():
    i = jnp.arange(AW) // HD
    return (i[:, None] == i[None, :]).astype(BF16)


def _rs64(x, ones_ref):
    return lax.rsqrt(_gsum64(x * x, ones_ref) * (1.0 / HD) + EPS)


def _rms64_bwd(dy, x, r, g, ones_ref):
    xh = x * r
    dxh = dy * g
    dx = r * (dxh - xh * (_gsum64(dxh * xh, ones_ref) * (1.0 / HD)))
    return dx, dy * xh


def _rot_half(v):
    w = v.shape[-1]
    return jnp.where((_lane(v.shape) & 32) == 0, pltpu.roll(v, w - 32, 1), pltpu.roll(v, 32, 1))


def _rope(v, cos, sin_signed):
    return v * cos + _rot_half(v) * sin_signed


def _rope_bwd(dv, cos, sin_signed):
    return dv * cos + _rot_half(dv * sin_signed)


def _gelu(z):
    return 0.5 * z * (1.0 + jnp.tanh(GELU_C0 * (z + GELU_C1 * z * z * z)))


def _gelu_grad(z):
    t = jnp.tanh(GELU_C0 * (z + GELU_C1 * z * z * z))
    return 0.5 * (1.0 + t) + 0.5 * z * (1.0 - t * t) * (GELU_C0 * (1.0 + 3.0 * GELU_C1 * z * z))


def _colsum8(v):
    s = jnp.sum(v, axis=0, keepdims=True)
    row = lax.broadcasted_iota(jnp.int32, (8, v.shape[1]), 0)
    return jnp.where(row == 0, jnp.broadcast_to(s, (8, v.shape[1])), 0.0)


def _params(n_axes=1):
    return pltpu.CompilerParams(dimension_semantics=("arbitrary",) * n_axes, vmem_limit_bytes=VMEM_LIMIT)


def _rows(tm, w):
    return pl.BlockSpec((tm, w), lambda i: (i, 0))


def _const(shape):
    nd = len(shape)
    return pl.BlockSpec(shape, lambda *_: (0,) * nd)


def _sds(shape, dtype=F32):
    return jax.ShapeDtypeStruct(shape, dtype)


def _mm_tn(a, b, name):
    g = max(a.shape[0] if a.ndim == 3 else 1, b.shape[0] if b.ndim == 3 else 1)
    t, m = a.shape[-2:]
    n = b.shape[-1]

    def body(a_ref, b_ref, o_ref, acc_ref):
        i = pl.program_id(1)

        @pl.when(i == 0)
        def _():
            acc_ref[...] = jnp.zeros_like(acc_ref)

        acc_ref[...] += _dot_tn(a_ref[...].astype(BF16), b_ref[...].astype(BF16))

        @pl.when(i == pl.num_programs(1) - 1)
        def _():
            o_ref[...] = acc_ref[...].astype(BF16)

    tk = min(t, WK)

    def spec(v):
        w = v.shape[-1]
        if v.ndim == 3:
            return pl.BlockSpec((None, tk, w), lambda j, i: (j, i, 0))
        return pl.BlockSpec((tk, w), lambda j, i: (i, 0))

    return pl.pallas_call(
        body, name=name, grid=(g, t // tk), in_specs=[spec(a), spec(b)],
        out_specs=pl.BlockSpec((None, m, n), lambda j, i: (j, 0, 0)), out_shape=_sds((g, m, n), BF16),
        scratch_shapes=[pltpu.VMEM((m, n), F32)], compiler_params=_params(2))(a, b)


def _rope_tables(pos, inv_freq):
    t = pos.shape[0]

    def body(pos_ref, f_ref, cos_ref, sin_ref):
        ang = pos_ref[...].astype(F32) * f_ref[...]
        sign = jnp.where((_lane(ang.shape) & 32) == 0, -1.0, 1.0)
        cos_ref[...] = jnp.cos(ang)
        sin_ref[...] = jnp.sin(ang) * sign

    return pl.pallas_call(
        body, name="rope_tables", grid=(t // TM,),
        in_specs=[_rows(TM, 1), _const((1, 128))], out_specs=[_rows(TM, 128), _rows(TM, 128)],
        out_shape=[_sds((t, 128)), _sds((t, 128))], compiler_params=_params())(pos, inv_freq)


def _mixer_in_fwd(x, mix_norm, w_in, qn, kn, gvw, cos, sin):
    t = x.shape[0]

    def body(x_ref, g_ref, w_ref, qn_ref, kn_ref, gvw_ref, cos_ref, sin_ref, ones_ref,
             h_ref, qk_ref, gz_ref, q_ref, k_ref, v_ref, gvn_ref):
        x = x_ref[...]
        h = (x * _rs(x) * g_ref[...]).astype(BF16)
        h_ref[...] = h
        proj = _dot_nt(h, w_ref[...])
        qk = proj[:, :AW + KW]
        qk_ref[...] = qk
        gz = proj[:, AW + 2 * KW:]
        gz_ref[...] = gz
        cos2, sin2 = cos_ref[...], sin_ref[...]
        q = qk[:, :AW]
        q = q * _rs64(q, ones_ref) * qn_ref[...]
        q_ref[...] = _rope(q, jnp.tile(cos2, (1, 4)), jnp.tile(sin2, (1, 4))).astype(BF16)
        k = qk[:, AW:]
        k = k * _rs64(k, ones_ref) * kn_ref[...]
        k_ref[...] = _rope(k, cos2, sin2).astype(BF16)
        v_ref[...] = proj[:, AW + KW:AW + 2 * KW].astype(BF16)
        gv = _gelu(gz[:, GW:])
        gvn_ref[...] = (gv * _rs(gv) * gvw_ref[...]).astype(BF16)

    return pl.pallas_call(
        body, name="mixer_in_fwd", grid=(t // TM,),
        in_specs=[_rows(TM, D), _const((1, D)), _const((IN, D)), _const((1, AW)), _const((1, KW)),
                  _const((1, GW)), _rows(TM, 128), _rows(TM, 128), _const((AW, AW))],
        out_specs=[_rows(TM, D), _rows(TM, AW + KW), _rows(TM, 2 * GW), _rows(TM, AW), _rows(TM, KW),
                   _rows(TM, KW), _rows(TM, GW)],
        out_shape=[_sds((t, D), BF16), _sds((t, AW + KW)), _sds((t, 2 * GW)), _sds((t, AW), BF16),
                   _sds((t, KW), BF16), _sds((t, KW), BF16), _sds((t, GW), BF16)],
        compiler_params=_params())(x, mix_norm, w_in, qn, kn, gvw, cos, sin, _head_ones())


def _dup_half(kk, g):
    lane = _lane(kk.shape)
    other = pltpu.roll(kk, 64, 1)
    keep = (lane < 64) if g == 0 else (lane >= 64)
    return jnp.where(keep, kk, other).astype(BF16)


def _swa_mask(first_block):
    qi = lax.broadcasted_iota(jnp.int32, (4 * BLK, 2 * BLK), 0) & (BLK - 1)
    kj = lax.broadcasted_iota(jnp.int32, (4 * BLK, 2 * BLK), 1)
    diff = qi + BLK - kj
    band = (diff >= 0) & (diff < BLK)
    return band & (jnp.logical_not(first_block) | (kj >= BLK))


def _stack_heads(a2, b2):
    lo = _lane(a2.shape) < 64
    z = jnp.zeros_like(a2)
    return jnp.concatenate([jnp.where(lo, a2, z), jnp.where(lo, z, a2), jnp.where(lo, b2, z), jnp.where(lo, z, b2)], axis=0)


def _unstack_heads(o):
    lo = _lane((BLK, 128)) < 64
    return jnp.where(lo, o[0:BLK], o[BLK:2 * BLK]), jnp.where(lo, o[2 * BLK:3 * BLK], o[3 * BLK:4 * BLK])


def _sink_col(sink_ref, g):
    row = lax.broadcasted_iota(jnp.int32, (4 * BLK, 1), 0)
    s = [sink_ref[0, 4 * g + j] for j in range(4)]
    return jnp.where(row < BLK, s[0], jnp.where(row < 2 * BLK, s[1], jnp.where(row < 3 * BLK, s[2], s[3])))


def _swa_probs(qs, kd, mask, sink):
    s = _dot_nt(qs, kd) * (1.0 / math.sqrt(HD))
    s = jnp.where(mask, s, NEG)
    m = jnp.maximum(jnp.max(s, axis=-1, keepdims=True), sink)
    p = jnp.exp(s - m)
    ps = jnp.exp(sink - m)
    inv = 1.0 / (jnp.sum(p, axis=-1, keepdims=True) + ps)
    return p * inv, ps * inv


SB = 4
SB_BWD = 2


def _swa_fwd(q, k, v, sinks):
    t = q.shape[0]
    ts = min(t, SB * BLK)

    def body(sink_ref, q_ref, kc_ref, kp_ref, vc_ref, vp_ref, o_ref):
        i = pl.program_id(0)
        kk = jnp.concatenate([kp_ref[...], kc_ref[...]], axis=0).astype(F32)
        vv = jnp.concatenate([vp_ref[...], vc_ref[...]], axis=0).astype(F32)
        for b in range(ts // BLK):
            r = slice(b * BLK, (b + 1) * BLK)
            kb, vb = kk[b * BLK:(b + 2) * BLK], vv[b * BLK:(b + 2) * BLK]
            mask = _swa_mask(i == 0) if b == 0 else _swa_mask(False)
            for g in range(2):
                qs = _stack_heads(q_ref[r, 256 * g:256 * g + 128], q_ref[r, 256 * g + 128:256 * g + 256])
                pn, _ = _swa_probs(qs, _dup_half(kb, g), mask, _sink_col(sink_ref, g))
                oa, ob = _unstack_heads(_dot(pn.astype(BF16), _dup_half(vb, g)))
                o_ref[r, 256 * g:256 * g + 128] = oa
                o_ref[r, 256 * g + 128:256 * g + 256] = ob

    cur = lambda i: (i, 0)
    prev = lambda i: (jnp.maximum(i * (ts // BLK) - 1, 0), 0)
    return pl.pallas_call(
        body, name="swa_fwd", grid=(t // ts,),
        in_specs=[pl.BlockSpec(memory_space=pltpu.SMEM), pl.BlockSpec((ts, AW), cur),
                  pl.BlockSpec((ts, KW), cur), pl.BlockSpec((BLK, KW), prev),
                  pl.BlockSpec((ts, KW), cur), pl.BlockSpec((BLK, KW), prev)],
        out_specs=pl.BlockSpec((ts, AW), cur), out_shape=_sds((t, AW)),
        compiler_params=_params())(sinks, q, k, k, v, v)


def _causal_bf16(w_ref, h, transposed):
    r = lax.broadcasted_iota(jnp.int32, (BLK, BLK), 0)
    c = lax.broadcasted_iota(jnp.int32, (BLK, BLK), 1)
    keep = (r <= c) if transposed else (c <= r)
    return jnp.where(keep, w_ref[h], 0.0).astype(BF16)


def _gmlp_mix(w_ref, xin, transposed):
    lo = _lane((BLK, 128)) < 64
    wm = [_causal_bf16(w_ref, h, transposed) for h in range(8)]
    rows = []
    for c in range(xin.shape[0] // BLK):
        cols = []
        for j in range(4):
            xs = xin[c * BLK:(c + 1) * BLK, 128 * j:128 * (j + 1)]
            cols.append(jnp.where(lo, _dot(wm[2 * j], xs), _dot(wm[2 * j + 1], xs)))
        rows.append(jnp.concatenate(cols, axis=1))
    return jnp.concatenate(rows, axis=0)


def _mixer_out_fwd(attn, gvn, gz, ws, bfull, x, w_out, aon, gon, xan):
    t = x.shape[0]

    def body(a_ref, v_ref, gzu_ref, ws_ref, b_ref, x_ref, w_ref, aon_ref, gon_ref, xan_ref, gm_ref, y_ref, x1_ref, h2_ref):
        a = a_ref[...]
        g = _gelu(gzu_ref[...]) * (_gmlp_mix(ws_ref, v_ref[...], False) + jnp.tile(b_ref[...], (TM // BLK, 1)))
        gm_ref[...] = g
        y = jnp.concatenate([a * _rs(a) * aon_ref[...], g * _rs(g) * gon_ref[...]], axis=1).astype(BF16)
        y_ref[...] = y
        x1 = x_ref[...] + _dot(y, w_ref[...])
        x1_ref[...] = x1
        h2_ref[...] = (x1 * _rs(x1) * xan_ref[...]).astype(BF16)

    return pl.pallas_call(
        body, name="mixer_out_fwd", grid=(t // TM,),
        in_specs=[_rows(TM, AW), _rows(TM, GW), _rows(TM, GW), _const((8, BLK, BLK)), _const((BLK, GW)), _rows(TM, D),
                  _const((D, D)), _const((1, AW)), _const((1, GW)), _const((1, D))],
        out_specs=[_rows(TM, GW), _rows(TM, D), _rows(TM, D), _rows(TM, D)],
        out_shape=[_sds((t, GW)), _sds((t, D), BF16), _sds((t, D)), _sds((t, D), BF16)],
        compiler_params=_params())(attn, gvn, gz, ws, bfull, x, w_out, aon, gon, xan)


def _mem_kv_fwd(mem, mem_norm, wkv, kn4):
    def body(m_ref, g_ref, w_ref, kn_ref, mh_ref, kpre_ref, k_ref, v_ref):
        m = m_ref[...]
        mh = (m * _rs(m) * g_ref[...]).astype(BF16)
        mh_ref[...] = mh
        for h in range(XH):
            sl = slice(XD * h, XD * (h + 1))
            kh = _dot(mh, w_ref[h])
            kpre_ref[:, sl] = kh
            k_ref[:, sl] = (kh * _rs(kh) * kn_ref[:, sl]).astype(BF16)
            v_ref[:, sl] = _dot(mh, w_ref[XH + h]).astype(BF16)

    return pl.pallas_call(
        body, name="mem_kv_fwd",
        out_shape=[_sds((MEM, D), BF16), _sds((MEM, D)), _sds((MEM, D), BF16), _sds((MEM, D), BF16)],
        compiler_params=pltpu.CompilerParams(vmem_limit_bytes=VMEM_LIMIT))(mem, mem_norm, wkv, kn4)


def _xattn_probs(qpre_h, qn_h, k_h):
    rq = _rs(qpre_h)
    q2 = (qpre_h * rq * qn_h).astype(BF16)
    s = _dot_nt(q2, k_h) * (1.0 / math.sqrt(XD))
    p = jnp.exp(s - jnp.max(s, axis=-1, keepdims=True))
    return p * (1.0 / jnp.sum(p, axis=-1, keepdims=True)), q2, rq


def _xattn_fwd(h2, x1, wq, qn4, k2, v2, wo, ffn_norm):
    t = x1.shape[0]

    def body(h_ref, x_ref, wq_ref, qn_ref, k_ref, v_ref, wo_ref, fn_ref, qpre_ref, o_ref, x2_ref, h3_ref):
        qpre = _dot(h_ref[...], wq_ref[...])
        qpre_ref[...] = qpre
        outs = []
        for h in range(XH):
            sl = slice(XD * h, XD * (h + 1))
            pn, _, _ = _xattn_probs(qpre[:, sl], qn_ref[:, sl], k_ref[:, sl])
            outs.append(_dot(pn.astype(BF16), v_ref[:, sl]))
        o = jnp.concatenate(outs, axis=1).astype(BF16)
        o_ref[...] = o
        x2 = x_ref[...] + _dot(o, wo_ref[...])
        x2_ref[...] = x2
        h3_ref[...] = (x2 * _rs(x2) * fn_ref[...]).astype(BF16)

    return pl.pallas_call(
        body, name="xattn_fwd", grid=(t // TM,),
        in_specs=[_rows(TM, D), _rows(TM, D), _const((D, D)), _const((1, D)), _const((MEM, D)), _const((MEM, D)),
                  _const((D, D)), _const((1, D))],
        out_specs=[_rows(TM, D)] * 4,
        out_shape=[_sds((t, D)), _sds((t, D), BF16), _sds((t, D)), _sds((t, D), BF16)],
        compiler_params=_params())(h2, x1, wq, qn4, k2, v2, wo, ffn_norm)


SW = 704
NG = FF // SW
FM = 256
HALO = 16


def _resident(shape):
    nd = len(shape)
    return pl.BlockSpec(shape, lambda *_: (0,) * nd, pipeline_mode=pl.Buffered(1))


def _halo_before(i):
    return jnp.maximum(i * (FM // HALO) - 1, 0)


def _conv(e, w):
    return w[2:3, :] * e + pltpu.roll(w[1:2, :] * e + pltpu.roll(w[0:1, :] * e, 1, 0), 1, 0)


def _ffn_fwd(h3, x2, target, up, conv, conv_b, down):
    t = x2.shape[0]

    def body(h_ref, hp_ref, x_ref, t_ref, up_ref, w_ref, b_ref, dn_ref, a_ref, u_ref, gs_ref, dy_ref, loss_ref, acc_ref):
        i = pl.program_id(0)

        @pl.when(i == 0)
        def _():
            acc_ref[...] = jnp.zeros_like(acc_ref)

        before = jnp.where(i > 0, hp_ref[...], jnp.zeros_like(hp_ref))
        he = jnp.concatenate([before, h_ref[...]], axis=0)
        err = x_ref[...] - t_ref[...]
        for d in range(NG):
            c = []
            for s in range(2):
                a = _dot_nt(he, up_ref[s * NG + d])
                a_ref[s * NG + d] = a[HALO:].astype(BF16)
                c.append(_conv(a, w_ref[s, d])[HALO:] + b_ref[s, d])
            gl, gg = _gelu_and_grad(c[0])
            gs_ref[d] = gl.astype(BF16)
            gs_ref[NG + d] = (gg * c[1]).astype(BF16)
            u = (gl * c[1]).astype(BF16)
            u_ref[d] = u
            err = err + _dot(u, dn_ref[d])
        dy_ref[...] = err * (1.0 / D)
        acc_ref[...] += jnp.sum(err * err, axis=0, keepdims=True)

        @pl.when(i == pl.num_programs(0) - 1)
        def _():
            loss_ref[...] = jnp.full((8, 128), 0.5 / D, F32) * jnp.sum(acc_ref[...])

    return pl.pallas_call(
        body, name="ffn_fwd", grid=(t // FM,),
        in_specs=[_rows(FM, D), pl.BlockSpec((HALO, D), lambda i: (_halo_before(i), 0)), _rows(FM, D), _rows(FM, D),
                  _resident((NDEV, SW, D)), _resident((2, NG, 3, SW)), _resident((2, NG, 1, SW)), _resident((NG, SW, D))],
        out_specs=[pl.BlockSpec((NDEV, FM, SW), lambda i: (0, i, 0)), pl.BlockSpec((NG, FM, SW), lambda i: (0, i, 0)),
                   pl.BlockSpec((NDEV, FM, SW), lambda i: (0, i, 0)), _rows(FM, D), _const((8, 128))],
        out_shape=[_sds((NDEV, t, SW), BF16), _sds((NG, t, SW), BF16), _sds((NDEV, t, SW), BF16), _sds((t, D)),
                   _sds((8, 128))],
        scratch_shapes=[pltpu.VMEM((1, D), F32)], compiler_params=_params())(h3, h3, x2, target, up, conv, conv_b, down)


def _gelu_and_grad(z):
    z2 = z * z
    t = jnp.tanh(GELU_C0 * (z + GELU_C1 * z * z2))
    phi = 0.5 * (1.0 + t)
    return z * phi, phi + z * (1.0 - t * t) * (0.5 * GELU_C0 + (1.5 * GELU_C0 * GELU_C1) * z2)


def _ffn_bwd(dy, a, gs, x2, up, conv, down, ffn_norm):
    t = x2.shape[0]
    nt = t // FM
    n = FM + HALO

    def body(dy_ref, dyn_ref, a_ref, gs_ref, gsn_ref, x_ref, up_ref, w_ref, dn_ref, g_ref,
             dx_ref, da_ref, s_ref, dfn_ref):
        i = pl.program_id(0)

        @pl.when(i == 0)
        def _():
            s_ref[...] = jnp.zeros_like(s_ref)
            dfn_ref[...] = jnp.zeros_like(dfn_ref)

        last = i == nt - 1
        dy = dy_ref[...]
        dye = jnp.concatenate([dy, jnp.where(last, 0.0, dyn_ref[...])], axis=0).astype(BF16)
        dh = jnp.zeros((FM, D), F32)
        row = lax.broadcasted_iota(jnp.int32, (8, SW), 0)
        for d in range(NG):
            du = _dot_nt(dye, dn_ref[d])
            for s in range(2):
                j = s * NG + d
                k = NG + d if s == 0 else d
                dc = du * jnp.concatenate([gs_ref[k], gsn_ref[k]], axis=0).astype(F32)
                w = w_ref[s, d]
                tile = a_ref[j].astype(F32)
                d1 = pltpu.roll(dc, n - 1, 0)
                d2 = pltpu.roll(d1, n - 1, 0)
                da = (w[2:3, :] * dc + w[1:2, :] * d1 + w[0:1, :] * d2)[0:FM].astype(BF16)
                da_ref[j] = da
                dh = dh + _dot(da, up_ref[j])
                sums = [jnp.sum(v[0:FM] * tile, axis=0, keepdims=True) for v in (d2, d1, dc)]
                sums.append(jnp.sum(dc[0:FM], axis=0, keepdims=True))
                upd = jnp.zeros((8, SW), F32)
                for r, v in enumerate(sums):
                    upd = jnp.where(row == r, jnp.broadcast_to(v, (8, SW)), upd)
                s_ref[s, d] += upd
        x = x_ref[...]
        dx, dg = _rms_bwd(dh, x, _rs(x), g_ref[...])
        dx_ref[...] = dy + dx
        dfn_ref[...] += _colsum8(dg)

    last_halo = t // HALO - 1
    after = lambda i: jnp.minimum((i + 1) * (FM // HALO), last_halo)
    return pl.pallas_call(
        body, name="ffn_bwd", grid=(nt,),
        in_specs=[_rows(FM, D), pl.BlockSpec((HALO, D), lambda i: (after(i), 0)),
                  pl.BlockSpec((NDEV, FM, SW), lambda i: (0, i, 0)),
                  pl.BlockSpec((NDEV, FM, SW), lambda i: (0, i, 0)),
                  pl.BlockSpec((NDEV, HALO, SW), lambda i: (0, after(i), 0)),
                  _rows(FM, D), _resident((NDEV, SW, D)), _resident((2, NG, 3, SW)), _resident((NG, SW, D)), _const((1, D))],
        out_specs=[_rows(FM, D), pl.BlockSpec((NDEV, FM, SW), lambda i: (0, i, 0)), _const((2, NG, 8, SW)), _const((8, D))],
        out_shape=[_sds((t, D)), _sds((NDEV, t, SW), BF16), _sds((2, NG, 8, SW)), _sds((8, D))],
        compiler_params=_params())(dy, dy, a, gs, gs, x2, up, conv, down, ffn_norm)


BT = 512


def _xattn_bwd(dx2, x1, qpre, k2, v2, wq, wo, qn4, xan):
    t = x1.shape[0]

    def body(dx2_ref, x1_ref, qpre_ref, k_ref, v_ref, wq_ref, wo_ref, qn_ref, xan_ref,
             dx1_ref, dqpre_ref, dk_ref, dv_ref, dqn_ref, dxan_ref):
        @pl.when(pl.program_id(0) == 0)
        def _():
            for r in (dk_ref, dv_ref, dqn_ref, dxan_ref):
                r[...] = jnp.zeros_like(r)

        dx2 = dx2_ref[...]
        do = _dot_nt(dx2.astype(BF16), wo_ref[...])
        dqs = []
        for h in range(XH):
            sl = slice(XD * h, XD * (h + 1))
            qpre_h = qpre_ref[:, sl]
            pn, q2, rq = _xattn_probs(qpre_h, qn_ref[:, sl], k_ref[:, sl])
            do_h = do[:, sl].astype(BF16)
            dp = _dot_nt(do_h, v_ref[:, sl])
            ds = (pn * (dp - jnp.sum(pn * dp, axis=-1, keepdims=True)) * (1.0 / math.sqrt(XD))).astype(BF16)
            dq2 = _dot(ds, k_ref[:, sl])
            dk_ref[:, sl] += _dot_tn(ds, q2)
            dv_ref[:, sl] += _dot_tn(pn.astype(BF16), do_h)
            dqh, dg = _rms_bwd(dq2, qpre_h, rq, qn_ref[:, sl])
            dqn_ref[...] += _colsum8(dg)
            dqs.append(dqh)
        dqpre = jnp.concatenate(dqs, axis=1).astype(BF16)
        dqpre_ref[...] = dqpre
        dh2 = _dot_nt(dqpre, wq_ref[...])
        x1 = x1_ref[...]
        dx, dg = _rms_bwd(dh2, x1, _rs(x1), xan_ref[...])
        dx1_ref[...] = dx2 + dx
        dxan_ref[...] += _colsum8(dg)

    return pl.pallas_call(
        body, name="xattn_bwd", grid=(t // BT,),
        in_specs=[_rows(BT, D), _rows(BT, D), _rows(BT, D), _const((MEM, D)), _const((MEM, D)), _const((D, D)),
                  _const((D, D)), _const((1, D)), _const((1, D))],
        out_specs=[_rows(BT, D), _rows(BT, D), _const((MEM, D)), _const((MEM, D)), _const((8, XD)), _const((8, D))],
        out_shape=[_sds((t, D)), _sds((t, D), BF16), _sds((MEM, D)), _sds((MEM, D)), _sds((8, XD)), _sds((8, D))],
        compiler_params=_params())(dx2, x1, qpre, k2, v2, wq, wo, qn4, xan)


def _mem_kv_bwd(mem, mh, kpre, dk2, dv2, wkv, kn4, mem_norm):
    def body(m_ref, mh_ref, kpre_ref, dk_ref, dv_ref, w_ref, kn_ref, g_ref, dw_ref, dkn_ref, dmn_ref):
        dkn = jnp.zeros((8, XD), F32)
        dm = jnp.zeros((MEM, D), F32)
        mh = mh_ref[...]
        for h in range(XH):
            sl = slice(XD * h, XD * (h + 1))
            kh = kpre_ref[:, sl]
            dkh, dg = _rms_bwd(dk_ref[:, sl], kh, _rs(kh), kn_ref[:, sl])
            dkn = dkn + _colsum8(dg)
            dkh = dkh.astype(BF16)
            dvh = dv_ref[:, sl].astype(BF16)
            dw_ref[h] = _dot_tn(mh, dkh).astype(BF16)
            dw_ref[XH + h] = _dot_tn(mh, dvh).astype(BF16)
            dm = dm + _dot_nt(dkh, w_ref[h]) + _dot_nt(dvh, w_ref[XH + h])
        dkn_ref[...] = dkn
        m = m_ref[...]
        _, dg = _rms_bwd(dm, m, _rs(m), g_ref[...])
        dmn_ref[...] = _colsum8(dg)

    return pl.pallas_call(
        body, name="mem_kv_bwd", out_shape=[_sds((2 * XH, D, XD), BF16), _sds((8, XD)), _sds((8, D))],
        compiler_params=pltpu.CompilerParams(vmem_limit_bytes=VMEM_LIMIT))(mem, mh, kpre, dk2, dv2, wkv, kn4, mem_norm)


def _mixer_out_bwd(dx1, attn, gm, w_out, aon, gon, gvn, gz, ws, wst, bfull, gvw):
    t = dx1.shape[0]
    nc = TM // BLK

    def body(dx_ref, a_ref, g_ref, wo_ref, aon_ref, gon_ref, x_ref, gz_ref, w_ref, wt_ref, b_ref, gvw_ref,
             da_ref, dan_ref, dgn_ref, dgz_ref, dw_ref, db_ref, dgvw_ref, dbacc_ref):
        @pl.when(pl.program_id(0) == 0)
        def _():
            for r in (dan_ref, dgn_ref, dw_ref, dbacc_ref, dgvw_ref):
                r[...] = jnp.zeros_like(r)

        dy = _dot_nt(dx_ref[...].astype(BF16), wo_ref[...])
        a, g = a_ref[...], g_ref[...]
        da, dna = _rms_bwd(dy[:, :AW], a, _rs(a), aon_ref[...])
        dgm, dng = _rms_bwd(dy[:, AW:], g, _rs(g), gon_ref[...])
        da_ref[...] = da
        dan_ref[...] += _colsum8(dna)
        dgn_ref[...] += _colsum8(dng)

        xin = x_ref[...]
        mixed = _gmlp_mix(w_ref, xin, False) + jnp.tile(b_ref[...], (nc, 1))
        dgu = dgm * mixed
        dmixed = dgm * _gelu(gz_ref[:, :GW])
        lo = _lane((BLK, 128)) < 64
        dbias = jnp.zeros((BLK, GW), F32)
        for c in range(nc):
            dmc = dmixed[c * BLK:(c + 1) * BLK]
            dbias = dbias + dmc
            for j in range(4):
                dm2 = dmc[:, 128 * j:128 * (j + 1)]
                xs = xin[c * BLK:(c + 1) * BLK, 128 * j:128 * (j + 1)]
                z = jnp.zeros_like(dm2)
                dw_ref[2 * j] += _dot_nt(jnp.where(lo, dm2, z).astype(BF16), xs)
                dw_ref[2 * j + 1] += _dot_nt(jnp.where(lo, z, dm2).astype(BF16), xs)
        dbacc_ref[...] += dbias
        dgvn = _gmlp_mix(wt_ref, dmixed.astype(BF16), True)
        gz_u, gz_v = gz_ref[:, :GW], gz_ref[:, GW:]
        gv = _gelu(gz_v)
        dgv, dg = _rms_bwd(dgvn, gv, _rs(gv), gvw_ref[...])
        dgvw_ref[...] += _colsum8(dg)
        dgz_ref[:, :GW] = (dgu * _gelu_grad(gz_u)).astype(BF16)
        dgz_ref[:, GW:] = (dgv * _gelu_grad(gz_v)).astype(BF16)

        @pl.when(pl.program_id(0) == pl.num_programs(0) - 1)
        def _():
            s = dbacc_ref[...]
            sel = (lax.broadcasted_iota(jnp.int32, (8, GW), 1) // HD
                   == lax.broadcasted_iota(jnp.int32, (8, GW), 0)).astype(BF16)
            hi = s.astype(BF16)
            r1 = s - hi.astype(F32)
            mid = r1.astype(BF16)
            lo = (r1 - mid.astype(F32)).astype(BF16)
            db_ref[...] = _dot_nt(sel, hi) + _dot_nt(sel, mid) + _dot_nt(sel, lo)
            r = lax.broadcasted_iota(jnp.int32, (BLK, BLK), 0)
            c = lax.broadcasted_iota(jnp.int32, (BLK, BLK), 1)
            for h in range(8):
                dw_ref[h] = jnp.where(c <= r, dw_ref[h], 0.0)

    return pl.pallas_call(
        body, name="mixer_out_bwd", grid=(t // TM,),
        in_specs=[_rows(TM, D), _rows(TM, AW), _rows(TM, GW), _const((D, D)), _const((1, AW)), _const((1, GW)),
                  _rows(TM, GW), _rows(TM, 2 * GW), _const((8, BLK, BLK)), _const((8, BLK, BLK)),
                  _const((BLK, GW)), _const((1, GW))],
        out_specs=[_rows(TM, AW), _const((8, AW)), _const((8, GW)), _rows(TM, 2 * GW), _const((8, BLK, BLK)),
                   _const((8, BLK)), _const((8, GW))],
        out_shape=[_sds((t, AW)), _sds((8, AW)), _sds((8, GW)), _sds((t, 2 * GW), BF16), _sds((8, BLK, BLK)),
                   _sds((8, BLK)), _sds((8, GW))],
        scratch_shapes=[pltpu.VMEM((BLK, GW), F32)],
        compiler_params=_params())(dx1, attn, gm, w_out, aon, gon, gvn, gz, ws, wst, bfull, gvw)


def _fold_half(v):
    return v + pltpu.roll(v, 64, 1)


def _swa_bwd(q, k, v, dattn, sinks):
    t = q.shape[0]
    nb = t // BLK
    ts = min(t, SB_BWD * BLK)
    sb = ts // BLK
    nt = t // ts

    def body(sink_ref, q_ref, kc_ref, kp_ref, vc_ref, vp_ref, do_ref, dq_ref, dk_ref, dv_ref, ds_ref,
             ck_ref, cv_ref, sacc_ref):
        i = pl.program_id(0)

        @pl.when(i == 0)
        def _():
            ck_ref[...] = jnp.zeros_like(ck_ref)
            cv_ref[...] = jnp.zeros_like(cv_ref)
            sacc_ref[...] = jnp.zeros_like(sacc_ref)

        @pl.when(i < nt)
        def _():
            kk = jnp.concatenate([kp_ref[...], kc_ref[...]], axis=0).astype(F32)
            vv = jnp.concatenate([vp_ref[...], vc_ref[...]], axis=0).astype(F32)
            lo256 = _lane((2 * BLK, 128)) < 64
            acc_k = [jnp.zeros((BLK, 128), F32) for _ in range(sb + 1)]
            acc_v = [jnp.zeros((BLK, 128), F32) for _ in range(sb + 1)]
            for b in range(sb):
                r = slice(b * BLK, (b + 1) * BLK)
                kb, vb = kk[b * BLK:(b + 2) * BLK], vv[b * BLK:(b + 2) * BLK]
                mask = _swa_mask(i == 0) if b == 0 else _swa_mask(False)
                dkk = jnp.zeros((2 * BLK, 128), F32)
                dvv = jnp.zeros((2 * BLK, 128), F32)
                for g in range(2):
                    qs = _stack_heads(q_ref[r, 256 * g:256 * g + 128], q_ref[r, 256 * g + 128:256 * g + 256])
                    dos = _stack_heads(do_ref[r, 256 * g:256 * g + 128],
                                       do_ref[r, 256 * g + 128:256 * g + 256]).astype(BF16)
                    kd = _dup_half(kb, g)
                    pn, psn = _swa_probs(qs, kd, mask, _sink_col(sink_ref, g))
                    dp = _dot_nt(dos, _dup_half(vb, g))
                    dd = jnp.sum(pn * dp, axis=-1, keepdims=True)
                    ds = (pn * (dp - dd) * (1.0 / math.sqrt(HD))).astype(BF16)
                    sacc_ref[g] += jnp.broadcast_to(-psn * dd, (4 * BLK, 128))
                    dqa, dqb = _unstack_heads(_dot(ds, kd))
                    dq_ref[r, 256 * g:256 * g + 128] = dqa
                    dq_ref[r, 256 * g + 128:256 * g + 256] = dqb
                    dkg = _fold_half(_dot_tn(ds, qs))
                    dvg = _fold_half(_dot_tn(pn.astype(BF16), dos))
                    keep = lo256 if g == 0 else jnp.logical_not(lo256)
                    dkk = jnp.where(keep, dkg, dkk)
                    dvv = jnp.where(keep, dvg, dvv)
                acc_k[b], acc_k[b + 1] = acc_k[b] + dkk[0:BLK], acc_k[b + 1] + dkk[BLK:]
                acc_v[b], acc_v[b + 1] = acc_v[b] + dvv[0:BLK], acc_v[b + 1] + dvv[BLK:]
            for out_ref, c_ref, acc in ((dk_ref, ck_ref, acc_k), (dv_ref, cv_ref, acc_v)):
                if sb > 1:
                    out_ref[0:ts - BLK] = c_ref[0:ts - BLK]
                out_ref[ts - BLK:ts] = c_ref[ts - BLK:ts] + acc[0]
                for b in range(sb):
                    c_ref[b * BLK:(b + 1) * BLK] = acc[b + 1]

        @pl.when(i == nt)
        def _():
            dk_ref[...] = ck_ref[...]
            dv_ref[...] = cv_ref[...]
            lane = _lane((8, 128))
            acc = jnp.zeros((8, 128), F32)
            for g in range(2):
                for j in range(4):
                    val = jnp.sum(sacc_ref[g, j * BLK:(j + 1) * BLK, :], axis=0, keepdims=True)
                    acc = jnp.where(lane == 4 * g + j, jnp.broadcast_to(val, (8, 128)), acc)
            ds_ref[...] = acc

    cur = lambda i: (jnp.minimum(i, nt - 1), 0)
    before = lambda i: (jnp.clip(i * sb - 1, 0, nb - 1), 0)
    done = lambda i: (jnp.clip(i - 1, 0, nt - 1), 0)
    return pl.pallas_call(
        body, name="swa_bwd", grid=(nt + 1,),
        in_specs=[pl.BlockSpec(memory_space=pltpu.SMEM), pl.BlockSpec((ts, AW), cur),
                  pl.BlockSpec((ts, KW), cur), pl.BlockSpec((BLK, KW), before),
                  pl.BlockSpec((ts, KW), cur), pl.BlockSpec((BLK, KW), before), pl.BlockSpec((ts, AW), cur)],
        out_specs=[pl.BlockSpec((ts, AW), cur), pl.BlockSpec((ts, KW), done), pl.BlockSpec((ts, KW), done),
                   _const((8, 128))],
        out_shape=[_sds((t, AW)), _sds((t, KW)), _sds((t, KW)), _sds((8, 128))],
        scratch_shapes=[pltpu.VMEM((ts, KW), F32), pltpu.VMEM((ts, KW), F32), pltpu.VMEM((2, 4 * BLK, 128), F32)],
        compiler_params=_params())(sinks, q, k, k, v, v, dattn)


def _mixer_in_bwd(dq, dk, dv, dgz, qk, cos, sin, x, dx1, w_in, mix_norm, qn, kn):
    t = x.shape[0]

    def body(dq_ref, dk_ref, dv_ref, dgz_ref, qk_ref, cos_ref, sin_ref, x_ref, dx1_ref, w_ref, g_ref, qn_ref, kn_ref,
             ones_ref, gx_ref, dproj_ref, dmn_ref, dqn_ref, dkn_ref, qacc_ref, kacc_ref):
        i = pl.program_id(0)

        @pl.when(i == 0)
        def _():
            dmn_ref[...] = jnp.zeros_like(dmn_ref)
            qacc_ref[...] = jnp.zeros_like(qacc_ref)
            kacc_ref[...] = jnp.zeros_like(kacc_ref)

        cos2, sin2 = cos_ref[...], sin_ref[...]
        qpre, kpre = qk_ref[:, :AW], qk_ref[:, AW:]
        dqh = _rope_bwd(dq_ref[...], jnp.tile(cos2, (1, 4)), jnp.tile(sin2, (1, 4)))
        dqpre, dgq = _rms64_bwd(dqh, qpre, _rs64(qpre, ones_ref), qn_ref[...], ones_ref)
        dkh = _rope_bwd(dk_ref[...], cos2, sin2)
        dkpre, dgk = _rms64_bwd(dkh, kpre, _rs64(kpre, ones_ref), kn_ref[...], ones_ref)
        qacc_ref[...] += jnp.sum(dgq, axis=0, keepdims=True)
        kacc_ref[...] += jnp.sum(dgk, axis=0, keepdims=True)
        dproj = jnp.concatenate([dqpre.astype(BF16), dkpre.astype(BF16), dv_ref[...].astype(BF16), dgz_ref[...]], axis=1)
        dproj_ref[...] = dproj
        dh = _dot(dproj, w_ref[...])
        xv = x_ref[...]
        dx, dg = _rms_bwd(dh, xv, _rs(xv), g_ref[...])
        gx_ref[...] = dx1_ref[...] + dx
        dmn_ref[...] += _colsum8(dg)

        @pl.when(i == pl.num_programs(0) - 1)
        def _():
            qa = qacc_ref[...]
            q4 = qa[:, 0:128] + qa[:, 128:256] + qa[:, 256:384] + qa[:, 384:512]
            dqn_ref[...] = jnp.broadcast_to(_fold_half(q4), (8, 128))
            dkn_ref[...] = jnp.broadcast_to(_fold_half(kacc_ref[...]), (8, 128))

    return pl.pallas_call(
        body, name="mixer_in_bwd", grid=(t // TM,),
        in_specs=[_rows(TM, AW), _rows(TM, KW), _rows(TM, KW), _rows(TM, 2 * GW), _rows(TM, AW + KW), _rows(TM, 128),
                  _rows(TM, 128), _rows(TM, D), _rows(TM, D), _const((IN, D)), _const((1, D)), _const((1, AW)),
                  _const((1, KW)), _const((AW, AW))],
        out_specs=[_rows(TM, D), _rows(TM, IN), _const((8, D)), _const((8, 128)), _const((8, 128))],
        out_shape=[_sds((t, D)), _sds((t, IN), BF16), _sds((8, D)), _sds((8, 128)), _sds((8, 128))],
        scratch_shapes=[pltpu.VMEM((1, AW), F32), pltpu.VMEM((1, KW), F32)],
        compiler_params=_params())(dq, dk, dv, dgz, qk, cos, sin, x, dx1, w_in, mix_norm, qn, kn, _head_ones())


def _local_step(x, mem, pos, target, p, fetch, ship):
    t = x.shape[0]
    p = dict(p)
    p.update(fetch(0, None))
    inv_freq = 1.0 / (ROPE_THETA ** (jnp.arange(HD // 2, dtype=F32) * (2.0 / HD)))
    cos, sin = _rope_tables(pos, jnp.tile(inv_freq, 4).reshape(1, 128))
    qn = jnp.tile(p["q_norm"], (1, AW // HD))
    kn = jnp.tile(p["k_norm"], (1, KW // HD))
    qn4 = jnp.tile(p["xa_q_norm"], (1, XH))
    kn4 = jnp.tile(p["xa_k_norm"], (1, XH))
    ws = p["gmlp_ws"]
    wst = jnp.swapaxes(ws, 1, 2)
    bfull = jnp.repeat(p["gmlp_bs"].T, HD, axis=1)
    conv_b = p["ffn_conv_b"]

    h1, qk, gz, q, k, v, gvn = _mixer_in_fwd(x, p["mix_norm"], p["w_in"], qn, kn, p["gmlp_v_norm"], cos, sin)
    attn = _swa_fwd(q, k, v, p["attn_sinks"])
    p.update(fetch(1, attn))
    gm, ycat, x1, h2 = _mixer_out_fwd(attn, gvn, gz, ws, bfull, x, p["w_out"], p["attn_out_norm"], p["gmlp_out_norm"],
                                      p["xa_norm"])
    mh, kpre, k2, v2 = _mem_kv_fwd(mem, p["mem_norm"], p["xa_wkv"], kn4)
    qpre, o, x2, h3 = _xattn_fwd(h2, x1, p["xa_wq"], qn4, k2, v2, p["xa_wo"], p["ffn_norm"])
    p.update(fetch(2, h3))
    conv = p["ffn_conv"]
    a, u, gs, dy, loss8 = _ffn_fwd(h3, x2, target, p["ffn_up"], conv, conv_b, p["ffn_down"])

    raw = {}
    d_down = _mm_tn(u, dy, "ffn_down_bwd_w")
    dx2, da, raw["conv_sums"], raw["ffn_norm"] = _ffn_bwd(dy, a, gs, x2, p["ffn_up"], conv, p["ffn_down"], p["ffn_norm"])
    d_up = _mm_tn(da, h3, "ffn_up_bwd_w")
    token = ship(0, {"ffn_down": d_down, "ffn_up": d_up, "ffn_conv": raw["conv_sums"][:, :, 0:3]})
    dx1, dqpre, dk2, dv2, raw["xa_q_norm"], raw["xa_norm"] = _xattn_bwd(
        dx2, x1, qpre, k2, v2, p["xa_wq"], p["xa_wo"], qn4 + jnp.tile(token[0:1], (1, D // 128)), p["xa_norm"])
    d_wo = _mm_tn(o, dx2, "xa_wo_bwd_w")
    d_wq = _mm_tn(h2, dqpre, "xa_wq_bwd_w")
    d_wkv, raw["xa_k_norm"], raw["mem_norm"] = _mem_kv_bwd(mem, mh, kpre, dk2, dv2, p["xa_wkv"], kn4, p["mem_norm"])
    d_w_out = _mm_tn(ycat, dx1, "w_out_bwd_w")
    (dattn, raw["attn_out_norm"], raw["gmlp_out_norm"], dgz, raw["gmlp_ws"], raw["gmlp_bs"],
     raw["gmlp_v_norm"]) = _mixer_out_bwd(dx1, attn, gm, p["w_out"], p["attn_out_norm"], p["gmlp_out_norm"],
                                          gvn, gz, ws, wst, bfull, p["gmlp_v_norm"])
    token = ship(1, {"xa_wo": d_wo, "xa_wq": d_wq, "xa_wkv": d_wkv, "w_out": d_w_out}, [raw["gmlp_ws"]])
    dq, dk, dv, raw["attn_sinks"] = _swa_bwd(q, k, v, dattn, p["attn_sinks"] + token[0:1, 0:8])
    grad_x, dproj, raw["mix_norm"], raw["q_norm"], raw["k_norm"] = _mixer_in_bwd(
        dq, dk, dv, dgz, qk, cos, sin, x, dx1, p["w_in"], p["mix_norm"], qn, kn)
    d_w_in = _mm_tn(dproj, h1, "w_in_bwd_w")
    raw["loss"] = loss8
    return grad_x, {"w_in": d_w_in}, raw


def _cast_shards(shards):
    def body(*refs):
        n = len(refs) // 2
        for i_ref, o_ref in zip(refs[:n], refs[n:]):
            o_ref[...] = i_ref[...].astype(BF16)

    return pl.pallas_call(body, name="cast_shards", out_shape=[_sds(s.shape, BF16) for s in shards],
                          compiler_params=pltpu.CompilerParams(vmem_limit_bytes=VMEM_LIMIT))(*shards)


HBM_SPEC = pl.BlockSpec(memory_space=pltpu.HBM)
SEM_SPEC = pl.BlockSpec(memory_space=pltpu.SEMAPHORE)


ALL_K = tuple(range(1, NDEV))
CHIP_K = (1, 2, 4, 6)
RELAY_K = (2, 4, 6)


def _peer(k):
    x, y, cc = lax.axis_index("x"), lax.axis_index("y"), lax.axis_index("c")
    return 1 - x if k & 4 else x, 1 - y if k & 2 else y, 1 - cc if k & 1 else cc


def _remote_copies(src_refs, land_refs, send_refs, recv_refs, nd, ks):
    me = 4 * lax.axis_index("x") + 2 * lax.axis_index("y") + lax.axis_index("c")
    copies = []
    for a, (src_ref, land_ref) in enumerate(zip(src_refs, land_refs)):
        for j, k in enumerate(ks):
            px, py, pc = _peer(k)
            copies.append((k, pltpu.make_async_remote_copy(
                src_ref=src_ref.at[4 * px + 2 * py + pc] if a < nd else src_ref, dst_ref=land_ref.at[me],
                send_sem=send_refs[a].at[j], recv_sem=recv_refs[a].at[j],
                device_id=(px, py, pc), device_id_type=pl.DeviceIdType.MESH)))
    return copies


def _relay_copies(land_refs, send_refs, recv_refs):
    copies = []
    for a, land_ref in enumerate(land_refs):
        for j, k in enumerate(RELAY_K):
            px, py, pc = _peer(k)
            slot = land_ref.at[4 * px + 2 * py + pc]
            copies.append(pltpu.make_async_remote_copy(
                src_ref=slot, dst_ref=slot, send_sem=send_refs[a].at[j], recv_sem=recv_refs[a].at[j],
                device_id=_peer(1), device_id_type=pl.DeviceIdType.MESH))
    return copies


def _own_slot(src, by_dest, me):
    block = lax.dynamic_index_in_dim(src, me, 0, keepdims=True) if by_dest else src[None]
    return lax.dynamic_update_index_in_dim(lax.empty((NDEV,) + block.shape[1:], src.dtype), block, me, 0)


SIDE_EFFECT = pltpu.CompilerParams(has_side_effects=pltpu.SideEffectType.DATAFLOW_SIDE_EFFECTING)


def _exchange_start(by_dest, for_all, me, name, ks=ALL_K):
    srcs = list(by_dest) + list(for_all)
    n, nd = len(srcs), len(by_dest)
    lands = [_own_slot(s, a < nd, me) for a, s in enumerate(srcs)]

    def body(*refs):
        for _, cp in _remote_copies(refs[:n], refs[n:2 * n], refs[2 * n:3 * n], refs[3 * n:4 * n], nd, ks):
            cp.start()
        refs[-1][...] = jnp.zeros((8, 128), F32)

    sems = [pltpu.SemaphoreType.DMA((len(ks),))] * (2 * n)
    thru = [pltpu.HBM(v.shape, v.dtype) for v in srcs + lands]
    res = pl.pallas_call(
        body, name=name, out_shape=sems + thru + [_sds((8, 128))],
        in_specs=[HBM_SPEC] * (2 * n), out_specs=[SEM_SPEC] * (2 * n) + [HBM_SPEC] * (2 * n) + [pl.BlockSpec(memory_space=pltpu.VMEM)],
        input_output_aliases={i: 2 * n + i for i in range(2 * n)}, compiler_params=SIDE_EFFECT)(
            *[pltpu.with_memory_space_constraint(v, pltpu.HBM) for v in srcs + lands])
    return (res[:2 * n], res[2 * n:4 * n], nd, ks, None), res[-1]


def _exchange_relay(state, after, name):
    sems, thru, nd, ks, _ = state
    n = len(thru) // 2

    def body(*refs):
        for k, cp in _remote_copies(refs[:n], refs[n:2 * n], refs[2 * n:3 * n], refs[3 * n:4 * n], nd, ks):
            if k in RELAY_K:
                cp.wait_recv()
        for cp in _relay_copies(refs[n:2 * n], refs[4 * n + 1:5 * n + 1], refs[5 * n + 1:6 * n + 1]):
            cp.start()

    relay_sems = [pltpu.SemaphoreType.DMA((len(RELAY_K),))] * (2 * n)
    res = pl.pallas_call(
        body, name=name, out_shape=relay_sems + [pltpu.HBM(v.shape, v.dtype) for v in thru],
        in_specs=[HBM_SPEC] * (2 * n) + [SEM_SPEC] * (2 * n) + [pl.BlockSpec(memory_space=pl.ANY)],
        out_specs=[SEM_SPEC] * (2 * n) + [HBM_SPEC] * (2 * n),
        input_output_aliases={i: 2 * n + i for i in range(2 * n)}, compiler_params=SIDE_EFFECT)(*thru, *sems, after)
    return sems, res[2 * n:], nd, ks, res[:2 * n]


def _exchange_wait(state, after, name):
    sems, thru, nd, ks, relay_sems = state
    n = len(thru) // 2

    def body(*refs):
        for k, cp in _remote_copies(refs[:n], refs[n:2 * n], refs[2 * n:3 * n], refs[3 * n:4 * n], nd, ks):
            cp.wait_send()
            if relay_sems is None or k not in RELAY_K:
                cp.wait_recv()
        if relay_sems is not None:
            for cp in _relay_copies(refs[n:2 * n], refs[4 * n:5 * n], refs[5 * n:6 * n]):
                cp.wait_send()
                cp.wait_recv()

    extra = [] if relay_sems is None else list(relay_sems)
    res = pl.pallas_call(
        body, name=name, out_shape=[pltpu.HBM(v.shape, v.dtype) for v in thru],
        in_specs=[HBM_SPEC] * (2 * n) + [SEM_SPEC] * (2 * n + len(extra)) + [pl.BlockSpec(memory_space=pl.ANY)],
        out_specs=[HBM_SPEC] * (2 * n), input_output_aliases={i: i for i in range(2 * n)}, compiler_params=SIDE_EFFECT)(
            *thru, *sems, *extra, after)
    return res[n:]


def _adam(items, name):
    n = len(items)

    def body(*refs):
        for j in range(n):
            p_ref, w_ref, m_ref, v_ref = refs[4 * j:4 * j + 4]
            g_ref, d_ref, nm_ref, nv_ref = refs[4 * n + 4 * j:4 * n + 4 * j + 4]
            g = _sum_parts(p_ref)
            g_ref[...] = g
            d_ref[...], nm_ref[...], nv_ref[...] = _adam_math(g, w_ref[...], m_ref[...], v_ref[...])

    res = pl.pallas_call(
        body, name=name, out_shape=[_sds(it[1].shape) for it in items for _ in range(4)],
        compiler_params=pltpu.CompilerParams(vmem_limit_bytes=VMEM_LIMIT))(*[a for it in items for a in it])
    return [res[4 * j:4 * j + 4] for j in range(n)]


GATHER_GROUPS = (("w_in",), ("w_out", "xa_wkv", "xa_wq", "xa_wo"), ("ffn_up", "ffn_conv", "ffn_down"))
SCATTER_GROUPS = (("ffn_up", "ffn_down", "ffn_conv"), ("xa_wo", "xa_wq", "xa_wkv", "w_out"), ("w_in",))
ADAM_ALONE = ("ffn_up",)
BIG = tuple(n for grp in GATHER_GROUPS for n in grp)
BY_COLUMN = ("w_in", "ffn_up")
VECS = (("mix_norm", D), ("q_norm", HD), ("k_norm", HD), ("attn_sinks", 8), ("gmlp_v_norm", GW), ("attn_out_norm", AW),
        ("gmlp_out_norm", GW), ("xa_norm", D), ("mem_norm", D), ("xa_q_norm", XD), ("xa_k_norm", XD), ("ffn_norm", D))
BS_ROW = 16
VEC_ROWS = 24
SMALL = tuple(n for n, _ in VECS) + ("gmlp_bs", "gmlp_ws", "ffn_conv_b")


def _pack_small(raw):
    names = [n for n, _ in VECS] + ["gmlp_bs", "conv_sums"]

    def body(*refs):
        ins = dict(zip(names, refs))
        vec_ref, cb_ref = refs[len(names):]
        vec_ref[...] = jnp.zeros_like(vec_ref)
        for r, (n, w) in enumerate(VECS):
            vec_ref[r:r + 1, 0:w] = ins[n][0:1, 0:w]
        vec_ref[BS_ROW:BS_ROW + 8, 0:BLK] = ins["gmlp_bs"][...]
        for s in range(2):
            for d in range(NG):
                cb_ref[s, d] = ins["conv_sums"][s, d, 3:4, :]

    return pl.pallas_call(body, name="pack_small", out_shape=[_sds((VEC_ROWS, D)), _sds((2, NG, 1, SW))])(
        *[raw[n] for n in names])


def _adam_math(g, w, m, v):
    nm = B1 * m + (1.0 - B1) * g
    nv = B2 * v + (1.0 - B2) * (g * g)
    m_hat = nm / (1.0 - B1 ** STEP)
    v_hat = nv / (1.0 - B2 ** STEP)
    return -LR * (m_hat / (jnp.sqrt(v_hat) + AEPS) + WD * w), nm, nv


def _sum_parts(p_ref):
    g = p_ref[0].astype(F32)
    for j in range(1, NDEV):
        g = g + p_ref[j].astype(F32)
    return g


def _adam_small(parts_vec, parts_ws, parts_cb, w, m, v):
    def body(*refs):
        pv_ref, pws_ref, pcb_ref = refs[:3]
        ins = refs[3:3 + 3 * len(SMALL)]
        outs = refs[3 + 3 * len(SMALL):]
        gv = _sum_parts(pv_ref)
        for j, n in enumerate(SMALL):
            w_ref, m_ref, v_ref = ins[3 * j:3 * j + 3]
            o = outs[4 * j:4 * j + 4]
            if n == "gmlp_ws":
                g = _sum_parts(pws_ref)
            elif n == "ffn_conv_b":
                g = _sum_parts(pcb_ref)
            elif n == "gmlp_bs":
                g = gv[BS_ROW:BS_ROW + 8, 0:BLK]
            else:
                g = gv[j:j + 1, 0:VECS[j][1]]
            lead = n in ("gmlp_ws", "gmlp_bs")
            res = (g,) + _adam_math(g, w_ref[0] if lead else w_ref[...], m_ref[0] if lead else m_ref[...],
                                    v_ref[0] if lead else v_ref[...])
            for o_ref, val in zip(o, res):
                if lead:
                    o_ref[0] = val
                else:
                    o_ref[...] = val

    args = [parts_vec, parts_ws, parts_cb] + [d[n] for n in SMALL for d in (w, m, v)]
    res = pl.pallas_call(body, name="adam_small", out_shape=[_sds(w[n].shape) for n in SMALL for _ in range(4)],
                         compiler_params=pltpu.CompilerParams(vmem_limit_bytes=VMEM_LIMIT))(*args)
    return {n: tuple(res[4 * j:4 * j + 4]) for j, n in enumerate(SMALL)}


def kernel(x, mem, positions, mix_norm, w_in, q_norm, k_norm, attn_sinks, gmlp_v_norm, gmlp_ws, gmlp_bs, attn_out_norm, gmlp_out_norm, w_out, xa_norm, mem_norm, xa_wq, xa_wkv, xa_q_norm, xa_k_norm, xa_wo, ffn_norm, ffn_up, ffn_conv, ffn_conv_b, ffn_down, loss_target, m_mix_norm, m_w_in, m_q_norm, m_k_norm, m_attn_sinks, m_gmlp_v_norm, m_gmlp_ws, m_gmlp_bs, m_attn_out_norm, m_gmlp_out_norm, m_w_out, m_xa_norm, m_mem_norm, m_xa_wq, m_xa_wkv, m_xa_q_norm, m_xa_k_norm, m_xa_wo, m_ffn_norm, m_ffn_up, m_ffn_conv, m_ffn_conv_b, m_ffn_down, v_mix_norm, v_w_in, v_q_norm, v_k_norm, v_attn_sinks, v_gmlp_v_norm, v_gmlp_ws, v_gmlp_bs, v_attn_out_norm, v_gmlp_out_norm, v_w_out, v_xa_norm, v_mem_norm, v_xa_wq, v_xa_wkv, v_xa_q_norm, v_xa_k_norm, v_xa_wo, v_ffn_norm, v_ffn_up, v_ffn_conv, v_ffn_conv_b, v_ffn_down):
    names = ("mix_norm", "w_in", "q_norm", "k_norm", "attn_sinks", "gmlp_v_norm", "gmlp_ws", "gmlp_bs", "attn_out_norm",
             "gmlp_out_norm", "w_out", "xa_norm", "mem_norm", "xa_wq", "xa_wkv", "xa_q_norm", "xa_k_norm", "xa_wo",
             "ffn_norm", "ffn_up", "ffn_conv", "ffn_conv_b", "ffn_down")
    w = dict(zip(names, (mix_norm, w_in, q_norm, k_norm, attn_sinks, gmlp_v_norm, gmlp_ws, gmlp_bs, attn_out_norm,
                         gmlp_out_norm, w_out, xa_norm, mem_norm, xa_wq, xa_wkv, xa_q_norm, xa_k_norm, xa_wo, ffn_norm,
                         ffn_up, ffn_conv, ffn_conv_b, ffn_down)))
    m = dict(zip(names, (m_mix_norm, m_w_in, m_q_norm, m_k_norm, m_attn_sinks, m_gmlp_v_norm, m_gmlp_ws, m_gmlp_bs,
                         m_attn_out_norm, m_gmlp_out_norm, m_w_out, m_xa_norm, m_mem_norm, m_xa_wq, m_xa_wkv,
                         m_xa_q_norm, m_xa_k_norm, m_xa_wo, m_ffn_norm, m_ffn_up, m_ffn_conv, m_ffn_conv_b, m_ffn_down)))
    v = dict(zip(names, (v_mix_norm, v_w_in, v_q_norm, v_k_norm, v_attn_sinks, v_gmlp_v_norm, v_gmlp_ws, v_gmlp_bs,
                         v_attn_out_norm, v_gmlp_out_norm, v_w_out, v_xa_norm, v_mem_norm, v_xa_wq, v_xa_wkv,
                         v_xa_q_norm, v_xa_k_norm, v_xa_wo, v_ffn_norm, v_ffn_up, v_ffn_conv, v_ffn_conv_b, v_ffn_down)))
    t = x.shape[1]

    me = 4 * lax.axis_index("x") + 2 * lax.axis_index("y") + lax.axis_index("c")

    def rows(a, n):
        return jnp.swapaxes(a[0], 0, 1) if n in BY_COLUMN else a[0]

    mats = [n for n in BIG if n != "ffn_conv"]
    shard = dict(zip(mats, _cast_shards([rows(w[n], n) for n in mats])), ffn_conv=w["ffn_conv"][0])
    gathers, tokens = zip(*[_exchange_start([], [shard[n] for n in grp], me, "gather_start_%d" % i,
                                            CHIP_K if i == len(GATHER_GROUPS) - 1 else ALL_K)
                            for i, grp in enumerate(GATHER_GROUPS)])

    def fetch(i, after):
        after = tokens[0] + tokens[1] + tokens[2] if after is None else after
        state = gathers[i]
        if state[3] == CHIP_K:
            state = _exchange_relay(state, after, "gather_relay_%d" % i)
        got = dict(zip(GATHER_GROUPS[i], _exchange_wait(state, after, "gather_wait_%d" % i)))
        if "w_in" in got:
            got["w_in"] = got["w_in"].reshape(IN, D)
        for n in ("w_out", "xa_wq", "xa_wo"):
            if n in got:
                got[n] = got[n].reshape(D, D)
        if "ffn_down" in got:
            got["ffn_down"] = got["ffn_down"].reshape(NG, SW, D)
            got["ffn_conv"] = got["ffn_conv"].reshape(2, NG, 3, SW)
        return got

    scatters = []

    def ship(i, grads, for_all=()):
        by_dest = [grads[n].reshape((NDEV,) + shard[n].shape) for n in SCATTER_GROUPS[i]]
        state, token = _exchange_start(by_dest, for_all, me, "scatter_start_%d" % i)
        scatters.append(state)
        return token

    conv_b = {k: d["ffn_conv_b"].reshape(NDEV, 1, SW) for k, d in (("w", w), ("m", m), ("v", v))}
    p = {n: w[n] for n in SMALL[:-1]}
    p["gmlp_ws"], p["gmlp_bs"] = w["gmlp_ws"][0], w["gmlp_bs"][0]
    p["ffn_conv_b"] = conv_b["w"].reshape(2, NG, 1, SW)
    grad_x, g, raw = _local_step(x[0], mem[0], positions.reshape(t, 1), loss_target[0], p, fetch, ship)
    loss = lax.psum(raw["loss"][0, 0], ("x", "y", "c"))

    vec, cb = _pack_small(raw)
    after = ship(2, g, [vec, cb.reshape(NDEV, 1, SW)])
    res, rest = {}, []
    for i, grp in enumerate(SCATTER_GROUPS):
        got = _exchange_wait(scatters[i], after, "scatter_wait_%d" % i)
        rest += got[len(grp):]
        parts = dict(zip(grp, got))
        for batch in ([n for n in grp if n in ADAM_ALONE], [n for n in grp if n not in ADAM_ALONE]):
            if batch:
                outs = _adam([(parts[n], rows(w[n], n), rows(m[n], n), rows(v[n], n)) for n in batch], "adam_" + batch[0])
                for n, out in zip(batch, outs):
                    res[n] = [jnp.swapaxes(o, 0, 1) if n in BY_COLUMN else o for o in out]
                    after = out[0]
    ws_parts, vec_parts, cb_parts = rest
    small = lambda d, k: {**{n: d[n] for n in SMALL[:-1]}, "ffn_conv_b": conv_b[k]}
    res.update(_adam_small(vec_parts, ws_parts, cb_parts, small(w, "w"), small(m, "m"), small(v, "v")))

    outs = [loss, grad_x[None]]
    for j in range(4):
        outs += [res[n][j].reshape(w[n].shape) for n in names]
    return tuple(outs)
```

```python
import math

import jax
import jax.numpy as jnp
from jax import lax
from jax.experimental import pallas as pl
from jax.experimental.pallas import tpu as pltpu

F32 = jnp.float32
BF16 = jnp.bfloat16

D = 1024
HD = 64
AW = 512
KW = 128
GW = 512
IN = AW + 2 * KW + 2 * GW
BLK = 128
MEM = 256
XH = 4
XD = 256
FF = 2816
EPS = 1e-6
ROPE_THETA = 10000.0
NDEV = 8
LR, B1, B2, AEPS, WD, STEP = 0.001, 0.9, 0.999, 1e-08, 0.01, 10

TM = 512
WK = (2048, 4096)
WK_VMEM = 40 * 1024 * 1024
VMEM_LIMIT = 56 * 1024 * 1024
NEG = float(jnp.finfo(jnp.float32).min)
GELU_C0 = math.sqrt(2.0 / math.pi)
GELU_C1 = 0.044715


def _dot(a, b):
    return jnp.dot(a, b, preferred_element_type=F32)


def _dot_nt(a, b):
    return lax.dot_general(a, b, (((1,), (1,)), ((), ())), preferred_element_type=F32)


def _dot_tn(a, b):
    return lax.dot_general(a, b, (((0,), (0,)), ((), ())), preferred_element_type=F32)


def _rs(x):
    return lax.rsqrt(jnp.mean(x * x, axis=-1, keepdims=True) + EPS)


def _rms_bwd(dy, x, r, g):
    xh = x * r
    dxh = dy * g
    dx = r * (dxh - xh * jnp.mean(dxh * xh, axis=-1, keepdims=True))
    return dx, dy * xh


def _lane(shape):
    return lax.broadcasted_iota(jnp.int32, shape, len(shape) - 1)


def _gsum64(v, ones_ref):
    w = v.shape[-1]
    ones = ones_ref[0:w, 0:w]
    hi = v.astype(BF16)
    lo = (v - hi.astype(F32)).astype(BF16)
    return _dot(hi, ones) + _dot(lo, ones)


def _head_ones():
    i = jnp.arange(AW) // HD
    return (i[:, None] == i[None, :]).astype(BF16)


def _rs64(x, ones_ref):
    return lax.rsqrt(_gsum64(x * x, ones_ref) * (1.0 / HD) + EPS)


def _rms64_bwd(dy, x, r, g, ones_ref):
    xh = x * r
    dxh = dy * g
    dx = r * (dxh - xh * (_gsum64(dxh * xh, ones_ref) * (1.0 / HD)))
    return dx, dy * xh


def _rot_half(v):
    w = v.shape[-1]
    return jnp.where((_lane(v.shape) & 32) == 0, pltpu.roll(v, w - 32, 1), pltpu.roll(v, 32, 1))


def _rope(v, cos, sin_signed):
    return v * cos + _rot_half(v) * sin_signed


def _rope_bwd(dv, cos, sin_signed):
    return dv * cos + _rot_half(dv * sin_signed)


def _gelu(z):
    return 0.5 * z * (1.0 + jnp.tanh(GELU_C0 * (z + GELU_C1 * z * z * z)))


def _gelu_grad(z):
    t = jnp.tanh(GELU_C0 * (z + GELU_C1 * z * z * z))
    return 0.5 * (1.0 + t) + 0.5 * z * (1.0 - t * t) * (GELU_C0 * (1.0 + 3.0 * GELU_C1 * z * z))


def _colsum8(v):
    s = jnp.sum(v, axis=0, keepdims=True)
    row = lax.broadcasted_iota(jnp.int32, (8, v.shape[1]), 0)
    return jnp.where(row == 0, jnp.broadcast_to(s, (8, v.shape[1])), 0.0)


def _params(n_axes=1):
    return pltpu.CompilerParams(dimension_semantics=("arbitrary",) * n_axes, vmem_limit_bytes=VMEM_LIMIT)


def _rows(tm, w):
    return pl.BlockSpec((tm, w), lambda i: (i, 0))


def _const(shape):
    nd = len(shape)
    return pl.BlockSpec(shape, lambda *_: (0,) * nd)


def _sds(shape, dtype=F32):
    return jax.ShapeDtypeStruct(shape, dtype)


def _mm_tn(a, b, name):
    g = max(a.shape[0] if a.ndim == 3 else 1, b.shape[0] if b.ndim == 3 else 1)
    t, m = a.shape[-2:]
    n = b.shape[-1]

    def body(a_ref, b_ref, o_ref, acc_ref):
        i = pl.program_id(1)

        @pl.when(i == 0)
        def _():
            acc_ref[...] = jnp.zeros_like(acc_ref)

        acc_ref[...] += _dot_tn(a_ref[...].astype(BF16), b_ref[...].astype(BF16))

        @pl.when(i == pl.num_programs(1) - 1)
        def _():
            o_ref[...] = acc_ref[...].astype(BF16)

    def vmem(tk):
        return 2 * tk * (m * a.dtype.itemsize + n * b.dtype.itemsize) + m * n * (4 + 2 * 2)

    tk = min(t, max(k for k in WK if k == WK[0] or vmem(k) <= WK_VMEM))

    def spec(v):
        w = v.shape[-1]
        if v.ndim == 3:
            return pl.BlockSpec((None, tk, w), lambda j, i: (j, i, 0))
        return pl.BlockSpec((tk, w), lambda j, i: (i, 0))

    return pl.pallas_call(
        body, name=name, grid=(g, t // tk), in_specs=[spec(a), spec(b)],
        out_specs=pl.BlockSpec((None, m, n), lambda j, i: (j, 0, 0)), out_shape=_sds((g, m, n), BF16),
        scratch_shapes=[pltpu.VMEM((m, n), F32)], compiler_params=_params(2))(a, b)


def _rope_tables(pos, inv_freq):
    t = pos.shape[0]

    def body(pos_ref, f_ref, cos_ref, sin_ref):
        ang = pos_ref[...].astype(F32) * f_ref[...]
        sign = jnp.where((_lane(ang.shape) & 32) == 0, -1.0, 1.0)
        cos_ref[...] = jnp.cos(ang)
        sin_ref[...] = jnp.sin(ang) * sign

    return pl.pallas_call(
        body, name="rope_tables", grid=(t // TM,),
        in_specs=[_rows(TM, 1), _const((1, 128))], out_specs=[_rows(TM, 128), _rows(TM, 128)],
        out_shape=[_sds((t, 128)), _sds((t, 128))], compiler_params=_params())(pos, inv_freq)


def _mixer_in_fwd(x, mix_norm, w_in, qn, kn, gvw, cos, sin):
    t = x.shape[0]

    def body(x_ref, g_ref, w_ref, qn_ref, kn_ref, gvw_ref, cos_ref, sin_ref, ones_ref,
             h_ref, qk_ref, gz_ref, q_ref, k_ref, v_ref, gvn_ref):
        x = x_ref[...]
        h = (x * _rs(x) * g_ref[...]).astype(BF16)
        h_ref[...] = h
        proj = _dot_nt(h, w_ref[...])
        qk = proj[:, :AW + KW]
        qk_ref[...] = qk
        gz = proj[:, AW + 2 * KW:]
        gz_ref[...] = gz
        cos2, sin2 = cos_ref[...], sin_ref[...]
        q = qk[:, :AW]
        q = q * _rs64(q, ones_ref) * qn_ref[...]
        q_ref[...] = _rope(q, jnp.tile(cos2, (1, 4)), jnp.tile(sin2, (1, 4))).astype(BF16)
        k = qk[:, AW:]
        k = k * _rs64(k, ones_ref) * kn_ref[...]
        k_ref[...] = _rope(k, cos2, sin2).astype(BF16)
        v_ref[...] = proj[:, AW + KW:AW + 2 * KW].astype(BF16)
        gv = _gelu(gz[:, GW:])
        gvn_ref[...] = (gv * _rs(gv) * gvw_ref[...]).astype(BF16)

    return pl.pallas_call(
        body, name="mixer_in_fwd", grid=(t // TM,),
        in_specs=[_rows(TM, D), _const((1, D)), _const((IN, D)), _const((1, AW)), _const((1, KW)),
                  _const((1, GW)), _rows(TM, 128), _rows(TM, 128), _const((AW, AW))],
        out_specs=[_rows(TM, D), _rows(TM, AW + KW), _rows(TM, 2 * GW), _rows(TM, AW), _rows(TM, KW),
                   _rows(TM, KW), _rows(TM, GW)],
        out_shape=[_sds((t, D), BF16), _sds((t, AW + KW)), _sds((t, 2 * GW)), _sds((t, AW), BF16),
                   _sds((t, KW), BF16), _sds((t, KW), BF16), _sds((t, GW), BF16)],
        compiler_params=_params())(x, mix_norm, w_in, qn, kn, gvw, cos, sin, _head_ones())


def _dup_half(kk, g):
    lane = _lane(kk.shape)
    other = pltpu.roll(kk, 64, 1)
    keep = (lane < 64) if g == 0 else (lane >= 64)
    return jnp.where(keep, kk, other).astype(BF16)


def _swa_mask(first_block):
    qi = lax.broadcasted_iota(jnp.int32, (4 * BLK, 2 * BLK), 0) & (BLK - 1)
    kj = lax.broadcasted_iota(jnp.int32, (4 * BLK, 2 * BLK), 1)
    diff = qi + BLK - kj
    band = (diff >= 0) & (diff < BLK)
    return band & (jnp.logical_not(first_block) | (kj >= BLK))


def _stack_heads(a2, b2):
    lo = _lane(a2.shape) < 64
    z = jnp.zeros_like(a2)
    return jnp.concatenate([jnp.where(lo, a2, z), jnp.where(lo, z, a2), jnp.where(lo, b2, z), jnp.where(lo, z, b2)], axis=0)


def _unstack_heads(o):
    lo = _lane((BLK, 128)) < 64
    return jnp.where(lo, o[0:BLK], o[BLK:2 * BLK]), jnp.where(lo, o[2 * BLK:3 * BLK], o[3 * BLK:4 * BLK])


def _sink_col(sink_ref, g):
    row = lax.broadcasted_iota(jnp.int32, (4 * BLK, 1), 0)
    s = [sink_ref[0, 4 * g + j] for j in range(4)]
    return jnp.where(row < BLK, s[0], jnp.where(row < 2 * BLK, s[1], jnp.where(row < 3 * BLK, s[2], s[3])))


def _swa_probs(qs, kd, mask, sink):
    s = _dot_nt(qs, kd) * (1.0 / math.sqrt(HD))
    s = jnp.where(mask, s, NEG)
    m = jnp.maximum(jnp.max(s, axis=-1, keepdims=True), sink)
    p = jnp.exp(s - m)
    ps = jnp.exp(sink - m)
    inv = 1.0 / (jnp.sum(p, axis=-1, keepdims=True) + ps)
    return p * inv, ps * inv


SB = 4
SB_BWD = 2


def _swa_fwd(q, k, v, sinks):
    t = q.shape[0]
    ts = min(t, SB * BLK)

    def body(sink_ref, q_ref, kc_ref, kp_ref, vc_ref, vp_ref, o_ref):
        i = pl.program_id(0)
        kk = jnp.concatenate([kp_ref[...], kc_ref[...]], axis=0).astype(F32)
        vv = jnp.concatenate([vp_ref[...], vc_ref[...]], axis=0).astype(F32)
        for b in range(ts // BLK):
            r = slice(b * BLK, (b + 1) * BLK)
            kb, vb = kk[b * BLK:(b + 2) * BLK], vv[b * BLK:(b + 2) * BLK]
            mask = _swa_mask(i == 0) if b == 0 else _swa_mask(False)
            for g in range(2):
                qs = _stack_heads(q_ref[r, 256 * g:256 * g + 128], q_ref[r, 256 * g + 128:256 * g + 256])
                pn, _ = _swa_probs(qs, _dup_half(kb, g), mask, _sink_col(sink_ref, g))
                oa, ob = _unstack_heads(_dot(pn.astype(BF16), _dup_half(vb, g)))
                o_ref[r, 256 * g:256 * g + 128] = oa
                o_ref[r, 256 * g + 128:256 * g + 256] = ob

    cur = lambda i: (i, 0)
    prev = lambda i: (jnp.maximum(i * (ts // BLK) - 1, 0), 0)
    return pl.pallas_call(
        body, name="swa_fwd", grid=(t // ts,),
        in_specs=[pl.BlockSpec(memory_space=pltpu.SMEM), pl.BlockSpec((ts, AW), cur),
                  pl.BlockSpec((ts, KW), cur), pl.BlockSpec((BLK, KW), prev),
                  pl.BlockSpec((ts, KW), cur), pl.BlockSpec((BLK, KW), prev)],
        out_specs=pl.BlockSpec((ts, AW), cur), out_shape=_sds((t, AW)),
        compiler_params=_params())(sinks, q, k, k, v, v)


def _causal_bf16(w_ref, h, transposed):
    r = lax.broadcasted_iota(jnp.int32, (BLK, BLK), 0)
    c = lax.broadcasted_iota(jnp.int32, (BLK, BLK), 1)
    keep = (r <= c) if transposed else (c <= r)
    return jnp.where(keep, w_ref[h], 0.0).astype(BF16)


def _gmlp_mix(w_ref, xin, transposed):
    lo = _lane((BLK, 128)) < 64
    wm = [_causal_bf16(w_ref, h, transposed) for h in range(8)]
    rows = []
    for c in range(xin.shape[0] // BLK):
        cols = []
        for j in range(4):
            xs = xin[c * BLK:(c + 1) * BLK, 128 * j:128 * (j + 1)]
            cols.append(jnp.where(lo, _dot(wm[2 * j], xs), _dot(wm[2 * j + 1], xs)))
        rows.append(jnp.concatenate(cols, axis=1))
    return jnp.concatenate(rows, axis=0)


def _mixer_out_fwd(attn, gvn, gz, ws, bfull, x, w_out, aon, gon, xan):
    t = x.shape[0]

    def body(a_ref, v_ref, gzu_ref, ws_ref, b_ref, x_ref, w_ref, aon_ref, gon_ref, xan_ref, gm_ref, y_ref, x1_ref, h2_ref):
        a = a_ref[...]
        g = _gelu(gzu_ref[...]) * (_gmlp_mix(ws_ref, v_ref[...], False) + jnp.tile(b_ref[...], (TM // BLK, 1)))
        gm_ref[...] = g
        y = jnp.concatenate([a * _rs(a) * aon_ref[...], g * _rs(g) * gon_ref[...]], axis=1).astype(BF16)
        y_ref[...] = y
        x1 = x_ref[...] + _dot(y, w_ref[...])
        x1_ref[...] = x1
        h2_ref[...] = (x1 * _rs(x1) * xan_ref[...]).astype(BF16)

    return pl.pallas_call(
        body, name="mixer_out_fwd", grid=(t // TM,),
        in_specs=[_rows(TM, AW), _rows(TM, GW), _rows(TM, GW), _const((8, BLK, BLK)), _const((BLK, GW)), _rows(TM, D),
                  _const((D, D)), _const((1, AW)), _const((1, GW)), _const((1, D))],
        out_specs=[_rows(TM, GW), _rows(TM, D), _rows(TM, D), _rows(TM, D)],
        out_shape=[_sds((t, GW)), _sds((t, D), BF16), _sds((t, D)), _sds((t, D), BF16)],
        compiler_params=_params())(attn, gvn, gz, ws, bfull, x, w_out, aon, gon, xan)


def _mem_kv_fwd(mem, mem_norm, wkv, kn4):
    def body(m_ref, g_ref, w_ref, kn_ref, mh_ref, kpre_ref, k_ref, v_ref):
        m = m_ref[...]
        mh = (m * _rs(m) * g_ref[...]).astype(BF16)
        mh_ref[...] = mh
        for h in range(XH):
            sl = slice(XD * h, XD * (h + 1))
            kh = _dot(mh, w_ref[h])
            kpre_ref[:, sl] = kh
            k_ref[:, sl] = (kh * _rs(kh) * kn_ref[:, sl]).astype(BF16)
            v_ref[:, sl] = _dot(mh, w_ref[XH + h]).astype(BF16)

    return pl.pallas_call(
        body, name="mem_kv_fwd",
        out_shape=[_sds((MEM, D), BF16), _sds((MEM, D)), _sds((MEM, D), BF16), _sds((MEM, D), BF16)],
        compiler_params=pltpu.CompilerParams(vmem_limit_bytes=VMEM_LIMIT))(mem, mem_norm, wkv, kn4)


def _xattn_probs(qpre_h, qn_h, k_h):
    rq = _rs(qpre_h)
    q2 = (qpre_h * rq * qn_h).astype(BF16)
    s = _dot_nt(q2, k_h) * (1.0 / math.sqrt(XD))
    p = jnp.exp(s - jnp.max(s, axis=-1, keepdims=True))
    return p * (1.0 / jnp.sum(p, axis=-1, keepdims=True)), q2, rq


def _xattn_fwd(h2, x1, wq, qn4, k2, v2, wo, ffn_norm):
    t = x1.shape[0]

    def body(h_ref, x_ref, wq_ref, qn_ref, k_ref, v_ref, wo_ref, fn_ref, qpre_ref, o_ref, x2_ref, h3_ref):
        qpre = _dot(h_ref[...], wq_ref[...])
        qpre_ref[...] = qpre
        outs = []
        for h in range(XH):
            sl = slice(XD * h, XD * (h + 1))
            pn, _, _ = _xattn_probs(qpre[:, sl], qn_ref[:, sl], k_ref[:, sl])
            outs.append(_dot(pn.astype(BF16), v_ref[:, sl]))
        o = jnp.concatenate(outs, axis=1).astype(BF16)
        o_ref[...] = o
        x2 = x_ref[...] + _dot(o, wo_ref[...])
        x2_ref[...] = x2
        h3_ref[...] = (x2 * _rs(x2) * fn_ref[...]).astype(BF16)

    return pl.pallas_call(
        body, name="xattn_fwd", grid=(t // TM,),
        in_specs=[_rows(TM, D), _rows(TM, D), _const((D, D)), _const((1, D)), _const((MEM, D)), _const((MEM, D)),
                  _const((D, D)), _const((1, D))],
        out_specs=[_rows(TM, D)] * 4,
        out_shape=[_sds((t, D)), _sds((t, D), BF16), _sds((t, D)), _sds((t, D), BF16)],
        compiler_params=_params())(h2, x1, wq, qn4, k2, v2, wo, ffn_norm)


SW = 704
NG = FF // SW
FM = 256
HALO = 16


def _resident(shape):
    nd = len(shape)
    return pl.BlockSpec(shape, lambda *_: (0,) * nd, pipeline_mode=pl.Buffered(1))


def _conv(e, w):
    return w[2:3, :] * e + pltpu.roll(w[1:2, :] * e + pltpu.roll(w[0:1, :] * e, 1, 0), 1, 0)


def _ffn_fwd(h3, x2, target, up, conv, conv_b, down):
    t = x2.shape[0]

    def body(h_ref, x_ref, t_ref, up_ref, w_ref, b_ref, dn_ref, a_ref, u_ref, gs_ref, dy_ref, loss_ref, acc_ref, tail_ref):
        i = pl.program_id(0)

        @pl.when(i == 0)
        def _():
            acc_ref[...] = jnp.zeros_like(acc_ref)
            tail_ref[...] = jnp.zeros_like(tail_ref)

        h = h_ref[...]
        err = x_ref[...] - t_ref[...]
        for d in range(NG):
            c = []
            for s in range(2):
                j = s * NG + d
                a = _dot_nt(h, up_ref[j])
                a_ref[j] = a.astype(BF16)
                c.append(_conv(jnp.concatenate([tail_ref[j], a], axis=0), w_ref[s, d])[8:] + b_ref[s, d])
                tail_ref[j] = a[FM - 8:FM]
            gl, gg = _gelu_and_grad(c[0])
            gs_ref[d] = gl.astype(BF16)
            gs_ref[NG + d] = (gg * c[1]).astype(BF16)
            u = (gl * c[1]).astype(BF16)
            u_ref[d] = u
            err = err + _dot(u, dn_ref[d])
        dy_ref[...] = err * (1.0 / D)
        acc_ref[...] += jnp.sum(err * err, axis=0, keepdims=True)

        @pl.when(i == pl.num_programs(0) - 1)
        def _():
            loss_ref[...] = jnp.full((8, 128), 0.5 / D, F32) * jnp.sum(acc_ref[...])

    return pl.pallas_call(
        body, name="ffn_fwd", grid=(t // FM,),
        in_specs=[_rows(FM, D), _rows(FM, D), _rows(FM, D),
                  _resident((NDEV, SW, D)), _resident((2, NG, 3, SW)), _resident((2, NG, 1, SW)), _resident((NG, SW, D))],
        out_specs=[pl.BlockSpec((NDEV, FM, SW), lambda i: (0, i, 0)), pl.BlockSpec((NG, FM, SW), lambda i: (0, i, 0)),
                   pl.BlockSpec((NDEV, FM, SW), lambda i: (0, i, 0)), _rows(FM, D), _const((8, 128))],
        out_shape=[_sds((NDEV, t, SW), BF16), _sds((NG, t, SW), BF16), _sds((NDEV, t, SW), BF16), _sds((t, D)),
                   _sds((8, 128))],
        scratch_shapes=[pltpu.VMEM((1, D), F32), pltpu.VMEM((NDEV, 8, SW), F32)],
        compiler_params=_params())(h3, x2, target, up, conv, conv_b, down)


def _gelu_and_grad(z):
    z2 = z * z
    t = jnp.tanh(GELU_C0 * (z + GELU_C1 * z * z2))
    phi = 0.5 * (1.0 + t)
    return z * phi, phi + z * (1.0 - t * t) * (0.5 * GELU_C0 + (1.5 * GELU_C0 * GELU_C1) * z2)


def _ffn_bwd(dy, a, gs, x2, up, conv, down, ffn_norm):
    t = x2.shape[0]
    nt = t // FM
    n = FM + HALO

    def body(dy_ref, dyn_ref, a_ref, gs_ref, gsn_ref, x_ref, up_ref, w_ref, dn_ref, g_ref,
             dx_ref, da_ref, s_ref, dfn_ref):
        i = pl.program_id(0)

        @pl.when(i == 0)
        def _():
            s_ref[...] = jnp.zeros_like(s_ref)
            dfn_ref[...] = jnp.zeros_like(dfn_ref)

        last = i == nt - 1
        dy = dy_ref[...]
        dye = jnp.concatenate([dy, jnp.where(last, 0.0, dyn_ref[...])], axis=0).astype(BF16)
        dh = jnp.zeros((FM, D), F32)
        row = lax.broadcasted_iota(jnp.int32, (8, SW), 0)
        for d in range(NG):
            du = _dot_nt(dye, dn_ref[d])
            for s in range(2):
                j = s * NG + d
                k = NG + d if s == 0 else d
                dc = du * jnp.concatenate([gs_ref[k], gsn_ref[k]], axis=0).astype(F32)
                w = w_ref[s, d]
                tile = a_ref[j].astype(F32)
                d1 = pltpu.roll(dc, n - 1, 0)
                d2 = pltpu.roll(d1, n - 1, 0)
                da = (w[2:3, :] * dc + w[1:2, :] * d1 + w[0:1, :] * d2)[0:FM].astype(BF16)
                da_ref[j] = da
                dh = dh + _dot(da, up_ref[j])
                sums = [jnp.sum(v[0:FM] * tile, axis=0, keepdims=True) for v in (d2, d1, dc)]
                sums.append(jnp.sum(dc[0:FM], axis=0, keepdims=True))
                upd = jnp.zeros((8, SW), F32)
                for r, v in enumerate(sums):
                    upd = jnp.where(row == r, jnp.broadcast_to(v, (8, SW)), upd)
                s_ref[s, d] += upd
        x = x_ref[...]
        dx, dg = _rms_bwd(dh, x, _rs(x), g_ref[...])
        dx_ref[...] = dy + dx
        dfn_ref[...] += _colsum8(dg)

    last_halo = t // HALO - 1
    after = lambda i: jnp.minimum((i + 1) * (FM // HALO), last_halo)
    return pl.pallas_call(
        body, name="ffn_bwd", grid=(nt,),
        in_specs=[_rows(FM, D), pl.BlockSpec((HALO, D), lambda i: (after(i), 0)),
                  pl.BlockSpec((NDEV, FM, SW), lambda i: (0, i, 0)),
                  pl.BlockSpec((NDEV, FM, SW), lambda i: (0, i, 0)),
                  pl.BlockSpec((NDEV, HALO, SW), lambda i: (0, after(i), 0)),
                  _rows(FM, D), _resident((NDEV, SW, D)), _resident((2, NG, 3, SW)), _resident((NG, SW, D)), _const((1, D))],
        out_specs=[_rows(FM, D), pl.BlockSpec((NDEV, FM, SW), lambda i: (0, i, 0)), _const((2, NG, 8, SW)), _const((8, D))],
        out_shape=[_sds((t, D)), _sds((NDEV, t, SW), BF16), _sds((2, NG, 8, SW)), _sds((8, D))],
        compiler_params=_params())(dy, dy, a, gs, gs, x2, up, conv, down, ffn_norm)


BT = 512


def _xattn_bwd(dx2, x1, qpre, k2, v2, wq, wo, qn4, xan):
    t = x1.shape[0]

    def body(dx2_ref, x1_ref, qpre_ref, k_ref, v_ref, wq_ref, wo_ref, qn_ref, xan_ref,
             dx1_ref, dqpre_ref, dk_ref, dv_ref, dqn_ref, dxan_ref):
        @pl.when(pl.program_id(0) == 0)
        def _():
            for r in (dk_ref, dv_ref, dqn_ref, dxan_ref):
                r[...] = jnp.zeros_like(r)

        dx2 = dx2_ref[...]
        do = _dot_nt(dx2.astype(BF16), wo_ref[...])
        dqs = []
        for h in range(XH):
            sl = slice(XD * h, XD * (h + 1))
            qpre_h = qpre_ref[:, sl]
            pn, q2, rq = _xattn_probs(qpre_h, qn_ref[:, sl], k_ref[:, sl])
            do_h = do[:, sl].astype(BF16)
            dp = _dot_nt(do_h, v_ref[:, sl])
            ds = (pn * (dp - jnp.sum(pn * dp, axis=-1, keepdims=True)) * (1.0 / math.sqrt(XD))).astype(BF16)
            dq2 = _dot(ds, k_ref[:, sl])
            dk_ref[:, sl] += _dot_tn(ds, q2)
            dv_ref[:, sl] += _dot_tn(pn.astype(BF16), do_h)
            dqh, dg = _rms_bwd(dq2, qpre_h, rq, qn_ref[:, sl])
            dqn_ref[...] += _colsum8(dg)
            dqs.append(dqh)
        dqpre = jnp.concatenate(dqs, axis=1).astype(BF16)
        dqpre_ref[...] = dqpre
        dh2 = _dot_nt(dqpre, wq_ref[...])
        x1 = x1_ref[...]
        dx, dg = _rms_bwd(dh2, x1, _rs(x1), xan_ref[...])
        dx1_ref[...] = dx2 + dx
        dxan_ref[...] += _colsum8(dg)

    return pl.pallas_call(
        body, name="xattn_bwd", grid=(t // BT,),
        in_specs=[_rows(BT, D), _rows(BT, D), _rows(BT, D), _const((MEM, D)), _const((MEM, D)), _const((D, D)),
                  _const((D, D)), _const((1, D)), _const((1, D))],
        out_specs=[_rows(BT, D), _rows(BT, D), _const((MEM, D)), _const((MEM, D)), _const((8, XD)), _const((8, D))],
        out_shape=[_sds((t, D)), _sds((t, D), BF16), _sds((MEM, D)), _sds((MEM, D)), _sds((8, XD)), _sds((8, D))],
        compiler_params=_params())(dx2, x1, qpre, k2, v2, wq, wo, qn4, xan)


def _mem_kv_bwd(mem, mh, kpre, dk2, dv2, wkv, kn4, mem_norm):
    def body(m_ref, mh_ref, kpre_ref, dk_ref, dv_ref, w_ref, kn_ref, g_ref, dw_ref, dkn_ref, dmn_ref):
        dkn = jnp.zeros((8, XD), F32)
        dm = jnp.zeros((MEM, D), F32)
        mh = mh_ref[...]
        for h in range(XH):
            sl = slice(XD * h, XD * (h + 1))
            kh = kpre_ref[:, sl]
            dkh, dg = _rms_bwd(dk_ref[:, sl], kh, _rs(kh), kn_ref[:, sl])
            dkn = dkn + _colsum8(dg)
            dkh = dkh.astype(BF16)
            dvh = dv_ref[:, sl].astype(BF16)
            dw_ref[h] = _dot_tn(mh, dkh).astype(BF16)
            dw_ref[XH + h] = _dot_tn(mh, dvh).astype(BF16)
            dm = dm + _dot_nt(dkh, w_ref[h]) + _dot_nt(dvh, w_ref[XH + h])
        dkn_ref[...] = dkn
        m = m_ref[...]
        _, dg = _rms_bwd(dm, m, _rs(m), g_ref[...])
        dmn_ref[...] = _colsum8(dg)

    return pl.pallas_call(
        body, name="mem_kv_bwd", out_shape=[_sds((2 * XH, D, XD), BF16), _sds((8, XD)), _sds((8, D))],
        compiler_params=pltpu.CompilerParams(vmem_limit_bytes=VMEM_LIMIT))(mem, mh, kpre, dk2, dv2, wkv, kn4, mem_norm)


def _mixer_out_bwd(dx1, attn, gm, w_out, aon, gon, gvn, gz, ws, wst, bfull, gvw):
    t = dx1.shape[0]
    nc = TM // BLK

    def body(dx_ref, a_ref, g_ref, wo_ref, aon_ref, gon_ref, x_ref, gz_ref, w_ref, wt_ref, b_ref, gvw_ref,
             da_ref, dan_ref, dgn_ref, dgz_ref, dw_ref, db_ref, dgvw_ref, dbacc_ref):
        @pl.when(pl.program_id(0) == 0)
        def _():
            for r in (dan_ref, dgn_ref, dw_ref, dbacc_ref, dgvw_ref):
                r[...] = jnp.zeros_like(r)

        dy = _dot_nt(dx_ref[...].astype(BF16), wo_ref[...])
        a, g = a_ref[...], g_ref[...]
        da, dna = _rms_bwd(dy[:, :AW], a, _rs(a), aon_ref[...])
        dgm, dng = _rms_bwd(dy[:, AW:], g, _rs(g), gon_ref[...])
        da_ref[...] = da
        dan_ref[...] += _colsum8(dna)
        dgn_ref[...] += _colsum8(dng)

        xin = x_ref[...]
        mixed = _gmlp_mix(w_ref, xin, False) + jnp.tile(b_ref[...], (nc, 1))
        dgu = dgm * mixed
        dmixed = dgm * _gelu(gz_ref[:, :GW])
        lo = _lane((BLK, 128)) < 64
        dbias = jnp.zeros((BLK, GW), F32)
        for c in range(nc):
            dmc = dmixed[c * BLK:(c + 1) * BLK]
            dbias = dbias + dmc
            for j in range(4):
                dm2 = dmc[:, 128 * j:128 * (j + 1)]
                xs = xin[c * BLK:(c + 1) * BLK, 128 * j:128 * (j + 1)]
                z = jnp.zeros_like(dm2)
                dw_ref[2 * j] += _dot_nt(jnp.where(lo, dm2, z).astype(BF16), xs)
                dw_ref[2 * j + 1] += _dot_nt(jnp.where(lo, z, dm2).astype(BF16), xs)
        dbacc_ref[...] += dbias
        dgvn = _gmlp_mix(wt_ref, dmixed.astype(BF16), True)
        gz_u, gz_v = gz_ref[:, :GW], gz_ref[:, GW:]
        gv = _gelu(gz_v)
        dgv, dg = _rms_bwd(dgvn, gv, _rs(gv), gvw_ref[...])
        dgvw_ref[...] += _colsum8(dg)
        dgz_ref[:, :GW] = (dgu * _gelu_grad(gz_u)).astype(BF16)
        dgz_ref[:, GW:] = (dgv * _gelu_grad(gz_v)).astype(BF16)

        @pl.when(pl.program_id(0) == pl.num_programs(0) - 1)
        def _():
            s = dbacc_ref[...]
            sel = (lax.broadcasted_iota(jnp.int32, (8, GW), 1) // HD
                   == lax.broadcasted_iota(jnp.int32, (8, GW), 0)).astype(BF16)
            hi = s.astype(BF16)
            r1 = s - hi.astype(F32)
            mid = r1.astype(BF16)
            lo = (r1 - mid.astype(F32)).astype(BF16)
            db_ref[...] = _dot_nt(sel, hi) + _dot_nt(sel, mid) + _dot_nt(sel, lo)
            r = lax.broadcasted_iota(jnp.int32, (BLK, BLK), 0)
            c = lax.broadcasted_iota(jnp.int32, (BLK, BLK), 1)
            for h in range(8):
                dw_ref[h] = jnp.where(c <= r, dw_ref[h], 0.0)

    return pl.pallas_call(
        body, name="mixer_out_bwd", grid=(t // TM,),
        in_specs=[_rows(TM, D), _rows(TM, AW), _rows(TM, GW), _const((D, D)), _const((1, AW)), _const((1, GW)),
                  _rows(TM, GW), _rows(TM, 2 * GW), _const((8, BLK, BLK)), _const((8, BLK, BLK)),
                  _const((BLK, GW)), _const((1, GW))],
        out_specs=[_rows(TM, AW), _const((8, AW)), _const((8, GW)), _rows(TM, 2 * GW), _const((8, BLK, BLK)),
                   _const((8, BLK)), _const((8, GW))],
        out_shape=[_sds((t, AW)), _sds((8, AW)), _sds((8, GW)), _sds((t, 2 * GW), BF16), _sds((8, BLK, BLK)),
                   _sds((8, BLK)), _sds((8, GW))],
        scratch_shapes=[pltpu.VMEM((BLK, GW), F32)],
        compiler_params=_params())(dx1, attn, gm, w_out, aon, gon, gvn, gz, ws, wst, bfull, gvw)


def _fold_half(v):
    return v + pltpu.roll(v, 64, 1)


def _swa_bwd(q, k, v, dattn, sinks):
    t = q.shape[0]
    nb = t // BLK
    ts = min(t, SB_BWD * BLK)
    sb = ts // BLK
    nt = t // ts

    def body(sink_ref, q_ref, kc_ref, kp_ref, vc_ref, vp_ref, do_ref, dq_ref, dk_ref, dv_ref, ds_ref,
             ck_ref, cv_ref, sacc_ref):
        i = pl.program_id(0)

        @pl.when(i == 0)
        def _():
            ck_ref[...] = jnp.zeros_like(ck_ref)
            cv_ref[...] = jnp.zeros_like(cv_ref)
            sacc_ref[...] = jnp.zeros_like(sacc_ref)

        @pl.when(i < nt)
        def _():
            kk = jnp.concatenate([kp_ref[...], kc_ref[...]], axis=0).astype(F32)
            vv = jnp.concatenate([vp_ref[...], vc_ref[...]], axis=0).astype(F32)
            lo256 = _lane((2 * BLK, 128)) < 64
            acc_k = [jnp.zeros((BLK, 128), F32) for _ in range(sb + 1)]
            acc_v = [jnp.zeros((BLK, 128), F32) for _ in range(sb + 1)]
            for b in range(sb):
                r = slice(b * BLK, (b + 1) * BLK)
                kb, vb = kk[b * BLK:(b + 2) * BLK], vv[b * BLK:(b + 2) * BLK]
                mask = _swa_mask(i == 0) if b == 0 else _swa_mask(False)
                dkk = jnp.zeros((2 * BLK, 128), F32)
                dvv = jnp.zeros((2 * BLK, 128), F32)
                for g in range(2):
                    qs = _stack_heads(q_ref[r, 256 * g:256 * g + 128], q_ref[r, 256 * g + 128:256 * g + 256])
                    dos = _stack_heads(do_ref[r, 256 * g:256 * g + 128],
                                       do_ref[r, 256 * g + 128:256 * g + 256]).astype(BF16)
                    kd = _dup_half(kb, g)
                    pn, psn = _swa_probs(qs, kd, mask, _sink_col(sink_ref, g))
                    dp = _dot_nt(dos, _dup_half(vb, g))
                    dd = jnp.sum(pn * dp, axis=-1, keepdims=True)
                    ds = (pn * (dp - dd) * (1.0 / math.sqrt(HD))).astype(BF16)
                    sacc_ref[g] += jnp.broadcast_to(-psn * dd, (4 * BLK, 128))
                    dqa, dqb = _unstack_heads(_dot(ds, kd))
                    dq_ref[r, 256 * g:256 * g + 128] = dqa
                    dq_ref[r, 256 * g + 128:256 * g + 256] = dqb
                    dkg = _fold_half(_dot_tn(ds, qs))
                    dvg = _fold_half(_dot_tn(pn.astype(BF16), dos))
                    keep = lo256 if g == 0 else jnp.logical_not(lo256)
                    dkk = jnp.where(keep, dkg, dkk)
                    dvv = jnp.where(keep, dvg, dvv)
                acc_k[b], acc_k[b + 1] = acc_k[b] + dkk[0:BLK], acc_k[b + 1] + dkk[BLK:]
                acc_v[b], acc_v[b + 1] = acc_v[b] + dvv[0:BLK], acc_v[b + 1] + dvv[BLK:]
            for out_ref, c_ref, acc in ((dk_ref, ck_ref, acc_k), (dv_ref, cv_ref, acc_v)):
                if sb > 1:
                    out_ref[0:ts - BLK] = c_ref[0:ts - BLK]
                out_ref[ts - BLK:ts] = c_ref[ts - BLK:ts] + acc[0]
                for b in range(sb):
                    c_ref[b * BLK:(b + 1) * BLK] = acc[b + 1]

        @pl.when(i == nt)
        def _():
            dk_ref[...] = ck_ref[...]
            dv_ref[...] = cv_ref[...]
            lane = _lane((8, 128))
            acc = jnp.zeros((8, 128), F32)
            for g in range(2):
                for j in range(4):
                    val = jnp.sum(sacc_ref[g, j * BLK:(j + 1) * BLK, :], axis=0, keepdims=True)
                    acc = jnp.where(lane == 4 * g + j, jnp.broadcast_to(val, (8, 128)), acc)
            ds_ref[...] = acc

    cur = lambda i: (jnp.minimum(i, nt - 1), 0)
    before = lambda i: (jnp.clip(i * sb - 1, 0, nb - 1), 0)
    done = lambda i: (jnp.clip(i - 1, 0, nt - 1), 0)
    return pl.pallas_call(
        body, name="swa_bwd", grid=(nt + 1,),
        in_specs=[pl.BlockSpec(memory_space=pltpu.SMEM), pl.BlockSpec((ts, AW), cur),
                  pl.BlockSpec((ts, KW), cur), pl.BlockSpec((BLK, KW), before),
                  pl.BlockSpec((ts, KW), cur), pl.BlockSpec((BLK, KW), before), pl.BlockSpec((ts, AW), cur)],
        out_specs=[pl.BlockSpec((ts, AW), cur), pl.BlockSpec((ts, KW), done), pl.BlockSpec((ts, KW), done),
                   _const((8, 128))],
        out_shape=[_sds((t, AW)), _sds((t, KW)), _sds((t, KW)), _sds((8, 128))],
        scratch_shapes=[pltpu.VMEM((ts, KW), F32), pltpu.VMEM((ts, KW), F32), pltpu.VMEM((2, 4 * BLK, 128), F32)],
        compiler_params=_params())(sinks, q, k, k, v, v, dattn)


def _mixer_in_bwd(dq, dk, dv, dgz, qk, cos, sin, x, dx1, w_in, mix_norm, qn, kn):
    t = x.shape[0]

    def body(dq_ref, dk_ref, dv_ref, dgz_ref, qk_ref, cos_ref, sin_ref, x_ref, dx1_ref, w_ref, g_ref, qn_ref, kn_ref,
             ones_ref, gx_ref, dproj_ref, dmn_ref, dqn_ref, dkn_ref, qacc_ref, kacc_ref):
        i = pl.program_id(0)

        @pl.when(i == 0)
        def _():
            dmn_ref[...] = jnp.zeros_like(dmn_ref)
            qacc_ref[...] = jnp.zeros_like(qacc_ref)
            kacc_ref[...] = jnp.zeros_like(kacc_ref)

        cos2, sin2 = cos_ref[...], sin_ref[...]
        qpre, kpre = qk_ref[:, :AW], qk_ref[:, AW:]
        dqh = _rope_bwd(dq_ref[...], jnp.tile(cos2, (1, 4)), jnp.tile(sin2, (1, 4)))
        dqpre, dgq = _rms64_bwd(dqh, qpre, _rs64(qpre, ones_ref), qn_ref[...], ones_ref)
        dkh = _rope_bwd(dk_ref[...], cos2, sin2)
        dkpre, dgk = _rms64_bwd(dkh, kpre, _rs64(kpre, ones_ref), kn_ref[...], ones_ref)
        qacc_ref[...] += jnp.sum(dgq, axis=0, keepdims=True)
        kacc_ref[...] += jnp.sum(dgk, axis=0, keepdims=True)
        dproj = jnp.concatenate([dqpre.astype(BF16), dkpre.astype(BF16), dv_ref[...].astype(BF16), dgz_ref[...]], axis=1)
        dproj_ref[...] = dproj
        dh = _dot(dproj, w_ref[...])
        xv = x_ref[...]
        dx, dg = _rms_bwd(dh, xv, _rs(xv), g_ref[...])
        gx_ref[...] = dx1_ref[...] + dx
        dmn_ref[...] += _colsum8(dg)

        @pl.when(i == pl.num_programs(0) - 1)
        def _():
            qa = qacc_ref[...]
            q4 = qa[:, 0:128] + qa[:, 128:256] + qa[:, 256:384] + qa[:, 384:512]
            dqn_ref[...] = jnp.broadcast_to(_fold_half(q4), (8, 128))
            dkn_ref[...] = jnp.broadcast_to(_fold_half(kacc_ref[...]), (8, 128))

    return pl.pallas_call(
        body, name="mixer_in_bwd", grid=(t // TM,),
        in_specs=[_rows(TM, AW), _rows(TM, KW), _rows(TM, KW), _rows(TM, 2 * GW), _rows(TM, AW + KW), _rows(TM, 128),
                  _rows(TM, 128), _rows(TM, D), _rows(TM, D), _const((IN, D)), _const((1, D)), _const((1, AW)),
                  _const((1, KW)), _const((AW, AW))],
        out_specs=[_rows(TM, D), _rows(TM, IN), _const((8, D)), _const((8, 128)), _const((8, 128))],
        out_shape=[_sds((t, D)), _sds((t, IN), BF16), _sds((8, D)), _sds((8, 128)), _sds((8, 128))],
        scratch_shapes=[pltpu.VMEM((1, AW), F32), pltpu.VMEM((1, KW), F32)],
        compiler_params=_params())(dq, dk, dv, dgz, qk, cos, sin, x, dx1, w_in, mix_norm, qn, kn, _head_ones())


def _local_step(x, mem, pos, target, p, fetch, ship):
    t = x.shape[0]
    p = dict(p)
    p.update(fetch(0, None))
    inv_freq = 1.0 / (ROPE_THETA ** (jnp.arange(HD // 2, dtype=F32) * (2.0 / HD)))
    cos, sin = _rope_tables(pos, jnp.tile(inv_freq, 4).reshape(1, 128))
    qn = jnp.tile(p["q_norm"], (1, AW // HD))
    kn = jnp.tile(p["k_norm"], (1, KW // HD))
    qn4 = jnp.tile(p["xa_q_norm"], (1, XH))
    kn4 = jnp.tile(p["xa_k_norm"], (1, XH))
    ws = p["gmlp_ws"]
    wst = jnp.swapaxes(ws, 1, 2)
    bfull = jnp.repeat(p["gmlp_bs"].T, HD, axis=1)
    conv_b = p["ffn_conv_b"]

    h1, qk, gz, q, k, v, gvn = _mixer_in_fwd(x, p["mix_norm"], p["w_in"], qn, kn, p["gmlp_v_norm"], cos, sin)
    attn = _swa_fwd(q, k, v, p["attn_sinks"])
    p.update(fetch(1, attn))
    gm, ycat, x1, h2 = _mixer_out_fwd(attn, gvn, gz, ws, bfull, x, p["w_out"], p["attn_out_norm"], p["gmlp_out_norm"],
                                      p["xa_norm"])
    mh, kpre, k2, v2 = _mem_kv_fwd(mem, p["mem_norm"], p["xa_wkv"], kn4)
    qpre, o, x2, h3 = _xattn_fwd(h2, x1, p["xa_wq"], qn4, k2, v2, p["xa_wo"], p["ffn_norm"])
    p.update(fetch(2, h3))
    conv = p["ffn_conv"]
    a, u, gs, dy, loss8 = _ffn_fwd(h3, x2, target, p["ffn_up"], conv, conv_b, p["ffn_down"])

    raw = {}
    d_down = _mm_tn(u, dy, "ffn_down_bwd_w")
    dx2, da, raw["conv_sums"], raw["ffn_norm"] = _ffn_bwd(dy, a, gs, x2, p["ffn_up"], conv, p["ffn_down"], p["ffn_norm"])
    d_up = _mm_tn(da, h3, "ffn_up_bwd_w")
    token = ship(0, {"ffn_down": d_down, "ffn_up": d_up, "ffn_conv": raw["conv_sums"][:, :, 0:3]})
    dx1, dqpre, dk2, dv2, raw["xa_q_norm"], raw["xa_norm"] = _xattn_bwd(
        dx2, x1, qpre, k2, v2, p["xa_wq"], p["xa_wo"], qn4 + jnp.tile(token[0:1], (1, D // 128)), p["xa_norm"])
    d_wo = _mm_tn(o, dx2, "xa_wo_bwd_w")
    d_wq = _mm_tn(h2, dqpre, "xa_wq_bwd_w")
    d_wkv, raw["xa_k_norm"], raw["mem_norm"] = _mem_kv_bwd(mem, mh, kpre, dk2, dv2, p["xa_wkv"], kn4, p["mem_norm"])
    d_w_out = _mm_tn(ycat, dx1, "w_out_bwd_w")
    (dattn, raw["attn_out_norm"], raw["gmlp_out_norm"], dgz, raw["gmlp_ws"], raw["gmlp_bs"],
     raw["gmlp_v_norm"]) = _mixer_out_bwd(dx1, attn, gm, p["w_out"], p["attn_out_norm"], p["gmlp_out_norm"],
                                          gvn, gz, ws, wst, bfull, p["gmlp_v_norm"])
    token = ship(1, {"xa_wo": d_wo, "xa_wq": d_wq, "xa_wkv": d_wkv, "w_out": d_w_out}, [raw["gmlp_ws"]])
    dq, dk, dv, raw["attn_sinks"] = _swa_bwd(q, k, v, dattn, p["attn_sinks"] + token[0:1, 0:8])
    grad_x, dproj, raw["mix_norm"], raw["q_norm"], raw["k_norm"] = _mixer_in_bwd(
        dq, dk, dv, dgz, qk, cos, sin, x, dx1, p["w_in"], p["mix_norm"], qn, kn)
    d_w_in = _mm_tn(dproj, h1, "w_in_bwd_w")
    raw["loss"] = loss8
    return grad_x, {"w_in": d_w_in}, raw


def _cast_shards(shards):
    def body(*refs):
        n = len(refs) // 2
        for i_ref, o_ref in zip(refs[:n], refs[n:]):
            o_ref[...] = i_ref[...].astype(BF16)

    return pl.pallas_call(body, name="cast_shards", out_shape=[_sds(s.shape, BF16) for s in shards],
                          compiler_params=pltpu.CompilerParams(vmem_limit_bytes=VMEM_LIMIT))(*shards)


HBM_SPEC = pl.BlockSpec(memory_space=pltpu.HBM)
SEM_SPEC = pl.BlockSpec(memory_space=pltpu.SEMAPHORE)


ALL_K = tuple(range(1, NDEV))
CHIP_K = (1, 2, 4, 6)
RELAY_K = (2, 4, 6)


def _peer(k):
    x, y, cc = lax.axis_index("x"), lax.axis_index("y"), lax.axis_index("c")
    return 1 - x if k & 4 else x, 1 - y if k & 2 else y, 1 - cc if k & 1 else cc


def _remote_copies(src_refs, land_refs, send_refs, recv_refs, nd, ks):
    me = 4 * lax.axis_index("x") + 2 * lax.axis_index("y") + lax.axis_index("c")
    copies = []
    for a, (src_ref, land_ref) in enumerate(zip(src_refs, land_refs)):
        for j, k in enumerate(ks):
            px, py, pc = _peer(k)
            copies.append((k, pltpu.make_async_remote_copy(
                src_ref=src_ref.at[4 * px + 2 * py + pc] if a < nd else src_ref, dst_ref=land_ref.at[me],
                send_sem=send_refs[a].at[j], recv_sem=recv_refs[a].at[j],
                device_id=(px, py, pc), device_id_type=pl.DeviceIdType.MESH)))
    return copies


def _relay_copies(land_refs, send_refs, recv_refs):
    copies = []
    for a, land_ref in enumerate(land_refs):
        for j, k in enumerate(RELAY_K):
            px, py, pc = _peer(k)
            slot = land_ref.at[4 * px + 2 * py + pc]
            copies.append(pltpu.make_async_remote_copy(
                src_ref=slot, dst_ref=slot, send_sem=send_refs[a].at[j], recv_sem=recv_refs[a].at[j],
                device_id=_peer(1), device_id_type=pl.DeviceIdType.MESH))
    return copies


def _own_slot(src, by_dest, me):
    block = lax.dynamic_index_in_dim(src, me, 0, keepdims=True) if by_dest else src[None]
    return lax.dynamic_update_index_in_dim(lax.empty((NDEV,) + block.shape[1:], src.dtype), block, me, 0)


SIDE_EFFECT = pltpu.CompilerParams(has_side_effects=pltpu.SideEffectType.DATAFLOW_SIDE_EFFECTING)


def _exchange_start(by_dest, for_all, me, name, ks=ALL_K):
    srcs = list(by_dest) + list(for_all)
    n, nd = len(srcs), len(by_dest)
    lands = [_own_slot(s, a < nd, me) for a, s in enumerate(srcs)]

    def body(*refs):
        for _, cp in _remote_copies(refs[:n], refs[n:2 * n], refs[2 * n:3 * n], refs[3 * n:4 * n], nd, ks):
            cp.start()
        refs[-1][...] = jnp.zeros((8, 128), F32)

    sems = [pltpu.SemaphoreType.DMA((len(ks),))] * (2 * n)
    thru = [pltpu.HBM(v.shape, v.dtype) for v in srcs + lands]
    res = pl.pallas_call(
        body, name=name, out_shape=sems + thru + [_sds((8, 128))],
        in_specs=[HBM_SPEC] * (2 * n), out_specs=[SEM_SPEC] * (2 * n) + [HBM_SPEC] * (2 * n) + [pl.BlockSpec(memory_space=pltpu.VMEM)],
        input_output_aliases={i: 2 * n + i for i in range(2 * n)}, compiler_params=SIDE_EFFECT)(
            *[pltpu.with_memory_space_constraint(v, pltpu.HBM) for v in srcs + lands])
    return (res[:2 * n], res[2 * n:4 * n], nd, ks, None), res[-1]


def _exchange_relay(state, after, name):
    sems, thru, nd, ks, _ = state
    n = len(thru) // 2

    def body(*refs):
        for k, cp in _remote_copies(refs[:n], refs[n:2 * n], refs[2 * n:3 * n], refs[3 * n:4 * n], nd, ks):
            if k in RELAY_K:
                cp.wait_recv()
        for cp in _relay_copies(refs[n:2 * n], refs[4 * n + 1:5 * n + 1], refs[5 * n + 1:6 * n + 1]):
            cp.start()

    relay_sems = [pltpu.SemaphoreType.DMA((len(RELAY_K),))] * (2 * n)
    res = pl.pallas_call(
        body, name=name, out_shape=relay_sems + [pltpu.HBM(v.shape, v.dtype) for v in thru],
        in_specs=[HBM_SPEC] * (2 * n) + [SEM_SPEC] * (2 * n) + [pl.BlockSpec(memory_space=pl.ANY)],
        out_specs=[SEM_SPEC] * (2 * n) + [HBM_SPEC] * (2 * n),
        input_output_aliases={i: 2 * n + i for i in range(2 * n)}, compiler_params=SIDE_EFFECT)(*thru, *sems, after)
    return sems, res[2 * n:], nd, ks, res[:2 * n]


def _exchange_wait(state, after, name):
    sems, thru, nd, ks, relay_sems = state
    n = len(thru) // 2

    def body(*refs):
        for k, cp in _remote_copies(refs[:n], refs[n:2 * n], refs[2 * n:3 * n], refs[3 * n:4 * n], nd, ks):
            cp.wait_send()
            if relay_sems is None or k not in RELAY_K:
                cp.wait_recv()
        if relay_sems is not None:
            for cp in _relay_copies(refs[n:2 * n], refs[4 * n:5 * n], refs[5 * n:6 * n]):
                cp.wait_send()
                cp.wait_recv()

    extra = [] if relay_sems is None else list(relay_sems)
    res = pl.pallas_call(
        body, name=name, out_shape=[pltpu.HBM(v.shape, v.dtype) for v in thru],
        in_specs=[HBM_SPEC] * (2 * n) + [SEM_SPEC] * (2 * n + len(extra)) + [pl.BlockSpec(memory_space=pl.ANY)],
        out_specs=[HBM_SPEC] * (2 * n), input_output_aliases={i: i for i in range(2 * n)}, compiler_params=SIDE_EFFECT)(
            *thru, *sems, *extra, after)
    return res[n:]


def _adam(items, name):
    n = len(items)

    def body(*refs):
        for j in range(n):
            p_ref, w_ref, m_ref, v_ref = refs[4 * j:4 * j + 4]
            g_ref, d_ref, nm_ref, nv_ref = refs[4 * n + 4 * j:4 * n + 4 * j + 4]
            g = _sum_parts(p_ref)
            g_ref[...] = g
            d_ref[...], nm_ref[...], nv_ref[...] = _adam_math(g, w_ref[...], m_ref[...], v_ref[...])

    res = pl.pallas_call(
        body, name=name, out_shape=[_sds(it[1].shape) for it in items for _ in range(4)],
        compiler_params=pltpu.CompilerParams(vmem_limit_bytes=VMEM_LIMIT))(*[a for it in items for a in it])
    return [res[4 * j:4 * j + 4] for j in range(n)]


GATHER_GROUPS = (("w_in",), ("w_out", "xa_wkv", "xa_wq", "xa_wo"), ("ffn_up", "ffn_conv", "ffn_down"))
SCATTER_GROUPS = (("ffn_up", "ffn_down", "ffn_conv"), ("xa_wo", "xa_wq", "xa_wkv", "w_out"), ("w_in",))
ADAM_ALONE = ("ffn_up",)
BIG = tuple(n for grp in GATHER_GROUPS for n in grp)
BY_COLUMN = ("w_in", "ffn_up")
VECS = (("mix_norm", D), ("q_norm", HD), ("k_norm", HD), ("attn_sinks", 8), ("gmlp_v_norm", GW), ("attn_out_norm", AW),
        ("gmlp_out_norm", GW), ("xa_norm", D), ("mem_norm", D), ("xa_q_norm", XD), ("xa_k_norm", XD), ("ffn_norm", D))
BS_ROW = 16
VEC_ROWS = 24
SMALL = tuple(n for n, _ in VECS) + ("gmlp_bs", "gmlp_ws", "ffn_conv_b")


def _pack_small(raw):
    names = [n for n, _ in VECS] + ["gmlp_bs", "conv_sums"]

    def body(*refs):
        ins = dict(zip(names, refs))
        vec_ref, cb_ref = refs[len(names):]
        vec_ref[...] = jnp.zeros_like(vec_ref)
        for r, (n, w) in enumerate(VECS):
            vec_ref[r:r + 1, 0:w] = ins[n][0:1, 0:w]
        vec_ref[BS_ROW:BS_ROW + 8, 0:BLK] = ins["gmlp_bs"][...]
        for s in range(2):
            for d in range(NG):
                cb_ref[s, d] = ins["conv_sums"][s, d, 3:4, :]

    return pl.pallas_call(body, name="pack_small", out_shape=[_sds((VEC_ROWS, D)), _sds((2, NG, 1, SW))])(
        *[raw[n] for n in names])


def _adam_math(g, w, m, v):
    nm = B1 * m + (1.0 - B1) * g
    nv = B2 * v + (1.0 - B2) * (g * g)
    m_hat = nm / (1.0 - B1 ** STEP)
    v_hat = nv / (1.0 - B2 ** STEP)
    return -LR * (m_hat / (jnp.sqrt(v_hat) + AEPS) + WD * w), nm, nv


def _sum_parts(p_ref):
    g = p_ref[0].astype(F32)
    for j in range(1, NDEV):
        g = g + p_ref[j].astype(F32)
    return g


def _adam_small(parts_vec, parts_ws, parts_cb, w, m, v):
    def body(*refs):
        pv_ref, pws_ref, pcb_ref = refs[:3]
        ins = refs[3:3 + 3 * len(SMALL)]
        outs = refs[3 + 3 * len(SMALL):]
        gv = _sum_parts(pv_ref)
        for j, n in enumerate(SMALL):
            w_ref, m_ref, v_ref = ins[3 * j:3 * j + 3]
            o = outs[4 * j:4 * j + 4]
            if n == "gmlp_ws":
                g = _sum_parts(pws_ref)
            elif n == "ffn_conv_b":
                g = _sum_parts(pcb_ref)
            elif n == "gmlp_bs":
                g = gv[BS_ROW:BS_ROW + 8, 0:BLK]
            else:
                g = gv[j:j + 1, 0:VECS[j][1]]
            lead = n in ("gmlp_ws", "gmlp_bs")
            res = (g,) + _adam_math(g, w_ref[0] if lead else w_ref[...], m_ref[0] if lead else m_ref[...],
                                    v_ref[0] if lead else v_ref[...])
            for o_ref, val in zip(o, res):
                if lead:
                    o_ref[0] = val
                else:
                    o_ref[...] = val

    args = [parts_vec, parts_ws, parts_cb] + [d[n] for n in SMALL for d in (w, m, v)]
    res = pl.pallas_call(body, name="adam_small", out_shape=[_sds(w[n].shape) for n in SMALL for _ in range(4)],
                         compiler_params=pltpu.CompilerParams(vmem_limit_bytes=VMEM_LIMIT))(*args)
    return {n: tuple(res[4 * j:4 * j + 4]) for j, n in enumerate(SMALL)}


def kernel(x, mem, positions, mix_norm, w_in, q_norm, k_norm, attn_sinks, gmlp_v_norm, gmlp_ws, gmlp_bs, attn_out_norm, gmlp_out_norm, w_out, xa_norm, mem_norm, xa_wq, xa_wkv, xa_q_norm, xa_k_norm, xa_wo, ffn_norm, ffn_up, ffn_conv, ffn_conv_b, ffn_down, loss_target, m_mix_norm, m_w_in, m_q_norm, m_k_norm, m_attn_sinks, m_gmlp_v_norm, m_gmlp_ws, m_gmlp_bs, m_attn_out_norm, m_gmlp_out_norm, m_w_out, m_xa_norm, m_mem_norm, m_xa_wq, m_xa_wkv, m_xa_q_norm, m_xa_k_norm, m_xa_wo, m_ffn_norm, m_ffn_up, m_ffn_conv, m_ffn_conv_b, m_ffn_down, v_mix_norm, v_w_in, v_q_norm, v_k_norm, v_attn_sinks, v_gmlp_v_norm, v_gmlp_ws, v_gmlp_bs, v_attn_out_norm, v_gmlp_out_norm, v_w_out, v_xa_norm, v_mem_norm, v_xa_wq, v_xa_wkv, v_xa_q_norm, v_xa_k_norm, v_xa_wo, v_ffn_norm, v_ffn_up, v_ffn_conv, v_ffn_conv_b, v_ffn_down):
    names = ("mix_norm", "w_in", "q_norm", "k_norm", "attn_sinks", "gmlp_v_norm", "gmlp_ws", "gmlp_bs", "attn_out_norm",
             "gmlp_out_norm", "w_out", "xa_norm", "mem_norm", "xa_wq", "xa_wkv", "xa_q_norm", "xa_k_norm", "xa_wo",
             "ffn_norm", "ffn_up", "ffn_conv", "ffn_conv_b", "ffn_down")
    w = dict(zip(names, (mix_norm, w_in, q_norm, k_norm, attn_sinks, gmlp_v_norm, gmlp_ws, gmlp_bs, attn_out_norm,
                         gmlp_out_norm, w_out, xa_norm, mem_norm, xa_wq, xa_wkv, xa_q_norm, xa_k_norm, xa_wo, ffn_norm,
                         ffn_up, ffn_conv, ffn_conv_b, ffn_down)))
    m = dict(zip(names, (m_mix_norm, m_w_in, m_q_norm, m_k_norm, m_attn_sinks, m_gmlp_v_norm, m_gmlp_ws, m_gmlp_bs,
                         m_attn_out_norm, m_gmlp_out_norm, m_w_out, m_xa_norm, m_mem_norm, m_xa_wq, m_xa_wkv,
                         m_xa_q_norm, m_xa_k_norm, m_xa_wo, m_ffn_norm, m_ffn_up, m_ffn_conv, m_ffn_conv_b, m_ffn_down)))
    v = dict(zip(names, (v_mix_norm, v_w_in, v_q_norm, v_k_norm, v_attn_sinks, v_gmlp_v_norm, v_gmlp_ws, v_gmlp_bs,
                         v_attn_out_norm, v_gmlp_out_norm, v_w_out, v_xa_norm, v_mem_norm, v_xa_wq, v_xa_wkv,
                         v_xa_q_norm, v_xa_k_norm, v_xa_wo, v_ffn_norm, v_ffn_up, v_ffn_conv, v_ffn_conv_b, v_ffn_down)))
    t = x.shape[1]

    me = 4 * lax.axis_index("x") + 2 * lax.axis_index("y") + lax.axis_index("c")

    def rows(a, n):
        return jnp.swapaxes(a[0], 0, 1) if n in BY_COLUMN else a[0]

    mats = [n for n in BIG if n != "ffn_conv"]
    shard = dict(zip(mats, _cast_shards([rows(w[n], n) for n in mats])), ffn_conv=w["ffn_conv"][0])
    gathers, tokens = zip(*[_exchange_start([], [shard[n] for n in grp], me, "gather_start_%d" % i,
                                            CHIP_K if i == len(GATHER_GROUPS) - 1 else ALL_K)
                            for i, grp in enumerate(GATHER_GROUPS)])

    def fetch(i, after):
        after = tokens[0] + tokens[1] + tokens[2] if after is None else after
        state = gathers[i]
        if state[3] == CHIP_K:
            state = _exchange_relay(state, after, "gather_relay_%d" % i)
        got = dict(zip(GATHER_GROUPS[i], _exchange_wait(state, after, "gather_wait_%d" % i)))
        if "w_in" in got:
            got["w_in"] = got["w_in"].reshape(IN, D)
        for n in ("w_out", "xa_wq", "xa_wo"):
            if n in got:
                got[n] = got[n].reshape(D, D)
        if "ffn_down" in got:
            got["ffn_down"] = got["ffn_down"].reshape(NG, SW, D)
            got["ffn_conv"] = got["ffn_conv"].reshape(2, NG, 3, SW)
        return got

    scatters = []

    def ship(i, grads, for_all=()):
        by_dest = [grads[n].reshape((NDEV,) + shard[n].shape) for n in SCATTER_GROUPS[i]]
        state, token = _exchange_start(by_dest, for_all, me, "scatter_start_%d" % i)
        scatters.append(state)
        return token

    conv_b = {k: d["ffn_conv_b"].reshape(NDEV, 1, SW) for k, d in (("w", w), ("m", m), ("v", v))}
    p = {n: w[n] for n in SMALL[:-1]}
    p["gmlp_ws"], p["gmlp_bs"] = w["gmlp_ws"][0], w["gmlp_bs"][0]
    p["ffn_conv_b"] = conv_b["w"].reshape(2, NG, 1, SW)
    grad_x, g, raw = _local_step(x[0], mem[0], positions.reshape(t, 1), loss_target[0], p, fetch, ship)
    loss = lax.psum(raw["loss"][0, 0], ("x", "y", "c"))

    vec, cb = _pack_small(raw)
    after = ship(2, g, [vec, cb.reshape(NDEV, 1, SW)])
    res, rest = {}, []
    for i, grp in enumerate(SCATTER_GROUPS):
        got = _exchange_wait(scatters[i], after, "scatter_wait_%d" % i)
        rest += got[len(grp):]
        parts = dict(zip(grp, got))
        for batch in ([n for n in grp if n in ADAM_ALONE], [n for n in grp if n not in ADAM_ALONE]):
            if batch:
                outs = _adam([(parts[n], rows(w[n], n), rows(m[n], n), rows(v[n], n)) for n in batch], "adam_" + batch[0])
                for n, out in zip(batch, outs):
                    res[n] = [jnp.swapaxes(o, 0, 1) if n in BY_COLUMN else o for o in out]
                    after = out[0]
    ws_parts, vec_parts, cb_parts = rest
    small = lambda d, k: {**{n: d[n] for n in SMALL[:-1]}, "ffn_conv_b": conv_b[k]}
    res.update(_adam_small(vec_parts, ws_parts, cb_parts, small(w, "w"), small(m, "m"), small(v, "v")))

    outs = [loss, grad_x[None]]
    for j in range(4):
        outs += [res[n][j].reshape(w[n].shape) for n in names]
    return tuple(outs)
```

```python
import math

import jax
import jax.numpy as jnp
from jax import lax
from jax.experimental import pallas as pl
from jax.experimental.pallas import tpu as pltpu

F32 = jnp.float32
BF16 = jnp.bfloat16

D = 1024
HD = 64
AW = 512
KW = 128
GW = 512
IN = AW + 2 * KW + 2 * GW
BLK = 128
MEM = 256
XH = 4
XD = 256
FF = 2816
EPS = 1e-6
ROPE_THETA = 10000.0
NDEV = 8
LR, B1, B2, AEPS, WD, STEP = 0.001, 0.9, 0.999, 1e-08, 0.01, 10

TM = 512
WK = (2048, 4096)
WK_VMEM = 40 * 1024 * 1024
VMEM_LIMIT = 56 * 1024 * 1024
NEG = float(jnp.finfo(jnp.float32).min)
GELU_C0 = math.sqrt(2.0 / math.pi)
GELU_C1 = 0.044715


def _dot(a, b):
    return jnp.dot(a, b, preferred_element_type=F32)


def _dot_nt(a, b):
    return lax.dot_general(a, b, (((1,), (1,)), ((), ())), preferred_element_type=F32)


def _dot_tn(a, b):
    return lax.dot_general(a, b, (((0,), (0,)), ((), ())), preferred_element_type=F32)


def _rs(x):
    return lax.rsqrt(jnp.mean(x * x, axis=-1, keepdims=True) + EPS)


def _rms_bwd(dy, x, r, g):
    xh = x * r
    dxh = dy * g
    dx = r * (dxh - xh * jnp.mean(dxh * xh, axis=-1, keepdims=True))
    return dx, dy * xh


def _lane(shape):
    return lax.broadcasted_iota(jnp.int32, shape, len(shape) - 1)


def _gsum64(v, ones_ref):
    w = v.shape[-1]
    ones = ones_ref[0:w, 0:w]
    hi = v.astype(BF16)
    lo = (v - hi.astype(F32)).astype(BF16)
    return _dot(hi, ones) + _dot(lo, ones)


def _head_ones():
    i = jnp.arange(AW) // HD
    return (i[:, None] == i[None, :]).astype(BF16)


def _rs64(x, ones_ref):
    return lax.rsqrt(_gsum64(x * x, ones_ref) * (1.0 / HD) + EPS)


def _rms64_bwd(dy, x, r, g, ones_ref):
    xh = x * r
    dxh = dy * g
    dx = r * (dxh - xh * (_gsum64(dxh * xh, ones_ref) * (1.0 / HD)))
    return dx, dy * xh


def _rot_half(v):
    w = v.shape[-1]
    return jnp.where((_lane(v.shape) & 32) == 0, pltpu.roll(v, w - 32, 1), pltpu.roll(v, 32, 1))


def _rope(v, cos, sin_signed):
    return v * cos + _rot_half(v) * sin_signed


def _rope_bwd(dv, cos, sin_signed):
    return dv * cos + _rot_half(dv * sin_signed)


def _gelu(z):
    return 0.5 * z * (1.0 + jnp.tanh(GELU_C0 * (z + GELU_C1 * z * z * z)))


def _gelu_grad(z):
    t = jnp.tanh(GELU_C0 * (z + GELU_C1 * z * z * z))
    return 0.5 * (1.0 + t) + 0.5 * z * (1.0 - t * t) * (GELU_C0 * (1.0 + 3.0 * GELU_C1 * z * z))


def _colsum8(v):
    s = jnp.sum(v, axis=0, keepdims=True)
    row = lax.broadcasted_iota(jnp.int32, (8, v.shape[1]), 0)
    return jnp.where(row == 0, jnp.broadcast_to(s, (8, v.shape[1])), 0.0)


def _params(n_axes=1):
    return pltpu.CompilerParams(dimension_semantics=("arbitrary",) * n_axes, vmem_limit_bytes=VMEM_LIMIT)


def _rows(tm, w):
    return pl.BlockSpec((tm, w), lambda i: (i, 0))


def _const(shape):
    nd = len(shape)
    return pl.BlockSpec(shape, lambda *_: (0,) * nd)


def _sds(shape, dtype=F32):
    return jax.ShapeDtypeStruct(shape, dtype)


def _mm_tn(a, b, name):
    g = max(a.shape[0] if a.ndim == 3 else 1, b.shape[0] if b.ndim == 3 else 1)
    t, m = a.shape[-2:]
    n = b.shape[-1]

    def body(a_ref, b_ref, o_ref, acc_ref):
        i = pl.program_id(1)

        @pl.when(i == 0)
        def _():
            acc_ref[...] = jnp.zeros_like(acc_ref)

        acc_ref[...] += _dot_tn(a_ref[...].astype(BF16), b_ref[...].astype(BF16))

        @pl.when(i == pl.num_programs(1) - 1)
        def _():
            o_ref[...] = acc_ref[...].astype(BF16)

    def vmem(tk):
        return 2 * tk * (m * a.dtype.itemsize + n * b.dtype.itemsize) + m * n * (4 + 2 * 2)

    tk = min(t, max(k for k in WK if k == WK[0] or vmem(k) <= WK_VMEM))

    def spec(v):
        w = v.shape[-1]
        if v.ndim == 3:
            return pl.BlockSpec((None, tk, w), lambda j, i: (j, i, 0))
        return pl.BlockSpec((tk, w), lambda j, i: (i, 0))

    return pl.pallas_call(
        body, name=name, grid=(g, t // tk), in_specs=[spec(a), spec(b)],
        out_specs=pl.BlockSpec((None, m, n), lambda j, i: (j, 0, 0)), out_shape=_sds((g, m, n), BF16),
        scratch_shapes=[pltpu.VMEM((m, n), F32)], compiler_params=_params(2))(a, b)


def _rope_tables(pos, inv_freq):
    t = pos.shape[0]

    def body(pos_ref, f_ref, cos_ref, sin_ref):
        ang = pos_ref[...].astype(F32) * f_ref[...]
        sign = jnp.where((_lane(ang.shape) & 32) == 0, -1.0, 1.0)
        cos_ref[...] = jnp.cos(ang)
        sin_ref[...] = jnp.sin(ang) * sign

    return pl.pallas_call(
        body, name="rope_tables", grid=(t // TM,),
        in_specs=[_rows(TM, 1), _const((1, 128))], out_specs=[_rows(TM, 128), _rows(TM, 128)],
        out_shape=[_sds((t, 128)), _sds((t, 128))], compiler_params=_params())(pos, inv_freq)


def _mixer_in_fwd(x, mix_norm, w_in, qn, kn, gvw, cos, sin):
    t = x.shape[0]

    def body(x_ref, g_ref, w_ref, qn_ref, kn_ref, gvw_ref, cos_ref, sin_ref, ones_ref,
             h_ref, qk_ref, gz_ref, q_ref, k_ref, v_ref, gvn_ref):
        x = x_ref[...]
        h = (x * _rs(x) * g_ref[...]).astype(BF16)
        h_ref[...] = h
        proj = _dot_nt(h, w_ref[...])
        qk = proj[:, :AW + KW]
        qk_ref[...] = qk
        gz = proj[:, AW + 2 * KW:]
        gz_ref[...] = gz
        cos2, sin2 = cos_ref[...], sin_ref[...]
        q = qk[:, :AW]
        q = q * _rs64(q, ones_ref) * qn_ref[...]
        q_ref[...] = _rope(q, jnp.tile(cos2, (1, 4)), jnp.tile(sin2, (1, 4))).astype(BF16)
        k = qk[:, AW:]
        k = k * _rs64(k, ones_ref) * kn_ref[...]
        k_ref[...] = _rope(k, cos2, sin2).astype(BF16)
        v_ref[...] = proj[:, AW + KW:AW + 2 * KW].astype(BF16)
        gv = _gelu(gz[:, GW:])
        gvn_ref[...] = (gv * _rs(gv) * gvw_ref[...]).astype(BF16)

    return pl.pallas_call(
        body, name="mixer_in_fwd", grid=(t // TM,),
        in_specs=[_rows(TM, D), _const((1, D)), _const((IN, D)), _const((1, AW)), _const((1, KW)),
                  _const((1, GW)), _rows(TM, 128), _rows(TM, 128), _const((AW, AW))],
        out_specs=[_rows(TM, D), _rows(TM, AW + KW), _rows(TM, 2 * GW), _rows(TM, AW), _rows(TM, KW),
                   _rows(TM, KW), _rows(TM, GW)],
        out_shape=[_sds((t, D), BF16), _sds((t, AW + KW)), _sds((t, 2 * GW)), _sds((t, AW), BF16),
                   _sds((t, KW), BF16), _sds((t, KW), BF16), _sds((t, GW), BF16)],
        compiler_params=_params())(x, mix_norm, w_in, qn, kn, gvw, cos, sin, _head_ones())


def _dup_half(kk, g):
    lane = _lane(kk.shape)
    other = pltpu.roll(kk, 64, 1)
    keep = (lane < 64) if g == 0 else (lane >= 64)
    return jnp.where(keep, kk, other).astype(BF16)


def _swa_mask(first_block):
    qi = lax.broadcasted_iota(jnp.int32, (4 * BLK, 2 * BLK), 0) & (BLK - 1)
    kj = lax.broadcasted_iota(jnp.int32, (4 * BLK, 2 * BLK), 1)
    diff = qi + BLK - kj
    band = (diff >= 0) & (diff < BLK)
    return band & (jnp.logical_not(first_block) | (kj >= BLK))


def _stack_heads(a2, b2):
    lo = _lane(a2.shape) < 64
    z = jnp.zeros_like(a2)
    return jnp.concatenate([jnp.where(lo, a2, z), jnp.where(lo, z, a2), jnp.where(lo, b2, z), jnp.where(lo, z, b2)], axis=0)


def _unstack_heads(o):
    lo = _lane((BLK, 128)) < 64
    return jnp.where(lo, o[0:BLK], o[BLK:2 * BLK]), jnp.where(lo, o[2 * BLK:3 * BLK], o[3 * BLK:4 * BLK])


def _sink_col(sink_ref, g):
    row = lax.broadcasted_iota(jnp.int32, (4 * BLK, 1), 0)
    s = [sink_ref[0, 4 * g + j] for j in range(4)]
    return jnp.where(row < BLK, s[0], jnp.where(row < 2 * BLK, s[1], jnp.where(row < 3 * BLK, s[2], s[3])))


def _swa_probs(qs, kd, mask, sink):
    s = _dot_nt(qs, kd) * (1.0 / math.sqrt(HD))
    s = jnp.where(mask, s, NEG)
    m = jnp.maximum(jnp.max(s, axis=-1, keepdims=True), sink)
    p = jnp.exp(s - m)
    ps = jnp.exp(sink - m)
    inv = 1.0 / (jnp.sum(p, axis=-1, keepdims=True) + ps)
    return p * inv, ps * inv


SB = 4
SB_BWD = 2


def _swa_fwd(q, k, v, sinks):
    t = q.shape[0]
    ts = min(t, SB * BLK)

    def body(sink_ref, q_ref, kc_ref, kp_ref, vc_ref, vp_ref, o_ref):
        i = pl.program_id(0)
        kk = jnp.concatenate([kp_ref[...], kc_ref[...]], axis=0).astype(F32)
        vv = jnp.concatenate([vp_ref[...], vc_ref[...]], axis=0).astype(F32)
        for b in range(ts // BLK):
            r = slice(b * BLK, (b + 1) * BLK)
            kb, vb = kk[b * BLK:(b + 2) * BLK], vv[b * BLK:(b + 2) * BLK]
            mask = _swa_mask(i == 0) if b == 0 else _swa_mask(False)
            for g in range(2):
                qs = _stack_heads(q_ref[r, 256 * g:256 * g + 128], q_ref[r, 256 * g + 128:256 * g + 256])
                pn, _ = _swa_probs(qs, _dup_half(kb, g), mask, _sink_col(sink_ref, g))
                oa, ob = _unstack_heads(_dot(pn.astype(BF16), _dup_half(vb, g)))
                o_ref[r, 256 * g:256 * g + 128] = oa
                o_ref[r, 256 * g + 128:256 * g + 256] = ob

    cur = lambda i: (i, 0)
    prev = lambda i: (jnp.maximum(i * (ts // BLK) - 1, 0), 0)
    return pl.pallas_call(
        body, name="swa_fwd", grid=(t // ts,),
        in_specs=[pl.BlockSpec(memory_space=pltpu.SMEM), pl.BlockSpec((ts, AW), cur),
                  pl.BlockSpec((ts, KW), cur), pl.BlockSpec((BLK, KW), prev),
                  pl.BlockSpec((ts, KW), cur), pl.BlockSpec((BLK, KW), prev)],
        out_specs=pl.BlockSpec((ts, AW), cur), out_shape=_sds((t, AW)),
        compiler_params=_params())(sinks, q, k, k, v, v)


def _causal_bf16(w_ref, h, transposed):
    r = lax.broadcasted_iota(jnp.int32, (BLK, BLK), 0)
    c = lax.broadcasted_iota(jnp.int32, (BLK, BLK), 1)
    keep = (r <= c) if transposed else (c <= r)
    return jnp.where(keep, w_ref[h], 0.0).astype(BF16)


def _gmlp_mix(w_ref, xin, transposed):
    lo = _lane((BLK, 128)) < 64
    wm = [_causal_bf16(w_ref, h, transposed) for h in range(8)]
    rows = []
    for c in range(xin.shape[0] // BLK):
        cols = []
        for j in range(4):
            xs = xin[c * BLK:(c + 1) * BLK, 128 * j:128 * (j + 1)]
            cols.append(jnp.where(lo, _dot(wm[2 * j], xs), _dot(wm[2 * j + 1], xs)))
        rows.append(jnp.concatenate(cols, axis=1))
    return jnp.concatenate(rows, axis=0)


def _mixer_out_fwd(attn, gvn, gz, ws, bfull, x, w_out, aon, gon, xan):
    t = x.shape[0]

    def body(a_ref, v_ref, gzu_ref, ws_ref, b_ref, x_ref, w_ref, aon_ref, gon_ref, xan_ref, gm_ref, y_ref, x1_ref, h2_ref):
        a = a_ref[...]
        g = _gelu(gzu_ref[...]) * (_gmlp_mix(ws_ref, v_ref[...], False) + jnp.tile(b_ref[...], (TM // BLK, 1)))
        gm_ref[...] = g
        y = jnp.concatenate([a * _rs(a) * aon_ref[...], g * _rs(g) * gon_ref[...]], axis=1).astype(BF16)
        y_ref[...] = y
        x1 = x_ref[...] + _dot(y, w_ref[...])
        x1_ref[...] = x1
        h2_ref[...] = (x1 * _rs(x1) * xan_ref[...]).astype(BF16)

    return pl.pallas_call(
        body, name="mixer_out_fwd", grid=(t // TM,),
        in_specs=[_rows(TM, AW), _rows(TM, GW), _rows(TM, GW), _const((8, BLK, BLK)), _const((BLK, GW)), _rows(TM, D),
                  _const((D, D)), _const((1, AW)), _const((1, GW)), _const((1, D))],
        out_specs=[_rows(TM, GW), _rows(TM, D), _rows(TM, D), _rows(TM, D)],
        out_shape=[_sds((t, GW)), _sds((t, D), BF16), _sds((t, D)), _sds((t, D), BF16)],
        compiler_params=_params())(attn, gvn, gz, ws, bfull, x, w_out, aon, gon, xan)


def _mem_kv_fwd(mem, mem_norm, wkv, kn4):
    def body(m_ref, g_ref, w_ref, kn_ref, mh_ref, kpre_ref, k_ref, v_ref):
        m = m_ref[...]
        mh = (m * _rs(m) * g_ref[...]).astype(BF16)
        mh_ref[...] = mh
        for h in range(XH):
            sl = slice(XD * h, XD * (h + 1))
            kh = _dot(mh, w_ref[h])
            kpre_ref[:, sl] = kh
            k_ref[:, sl] = (kh * _rs(kh) * kn_ref[:, sl]).astype(BF16)
            v_ref[:, sl] = _dot(mh, w_ref[XH + h]).astype(BF16)

    return pl.pallas_call(
        body, name="mem_kv_fwd",
        out_shape=[_sds((MEM, D), BF16), _sds((MEM, D)), _sds((MEM, D), BF16), _sds((MEM, D), BF16)],
        compiler_params=pltpu.CompilerParams(vmem_limit_bytes=VMEM_LIMIT))(mem, mem_norm, wkv, kn4)


def _xattn_probs(qpre_h, qn_h, k_h):
    rq = _rs(qpre_h)
    q2 = (qpre_h * rq * qn_h).astype(BF16)
    s = _dot_nt(q2, k_h) * (1.0 / math.sqrt(XD))
    p = jnp.exp(s - jnp.max(s, axis=-1, keepdims=True))
    return p * (1.0 / jnp.sum(p, axis=-1, keepdims=True)), q2, rq


def _xattn_fwd(h2, x1, wq, qn4, k2, v2, wo, ffn_norm):
    t = x1.shape[0]

    def body(h_ref, x_ref, wq_ref, qn_ref, k_ref, v_ref, wo_ref, fn_ref, qpre_ref, o_ref, x2_ref, h3_ref):
        qpre = _dot(h_ref[...], wq_ref[...])
        qpre_ref[...] = qpre
        outs = []
        for h in range(XH):
            sl = slice(XD * h, XD * (h + 1))
            pn, _, _ = _xattn_probs(qpre[:, sl], qn_ref[:, sl], k_ref[:, sl])
            outs.append(_dot(pn.astype(BF16), v_ref[:, sl]))
        o = jnp.concatenate(outs, axis=1).astype(BF16)
        o_ref[...] = o
        x2 = x_ref[...] + _dot(o, wo_ref[...])
        x2_ref[...] = x2
        h3_ref[...] = (x2 * _rs(x2) * fn_ref[...]).astype(BF16)

    return pl.pallas_call(
        body, name="xattn_fwd", grid=(t // TM,),
        in_specs=[_rows(TM, D), _rows(TM, D), _const((D, D)), _const((1, D)), _const((MEM, D)), _const((MEM, D)),
                  _const((D, D)), _const((1, D))],
        out_specs=[_rows(TM, D)] * 4,
        out_shape=[_sds((t, D)), _sds((t, D), BF16), _sds((t, D)), _sds((t, D), BF16)],
        compiler_params=_params())(h2, x1, wq, qn4, k2, v2, wo, ffn_norm)


SW = 704
NG = FF // SW
FM = 256


def _resident(shape):
    nd = len(shape)
    return pl.BlockSpec(shape, lambda *_: (0,) * nd, pipeline_mode=pl.Buffered(1))


def _conv(e, w):
    return w[2:3, :] * e + pltpu.roll(w[1:2, :] * e + pltpu.roll(w[0:1, :] * e, 1, 0), 1, 0)


def _ffn_fwd(h3, x2, target, up, conv, conv_b, down):
    t = x2.shape[0]

    def body(h_ref, x_ref, t_ref, up_ref, w_ref, b_ref, dn_ref, a_ref, u_ref, gs_ref, dy_ref, loss_ref, acc_ref, tail_ref):
        i = pl.program_id(0)

        @pl.when(i == 0)
        def _():
            acc_ref[...] = jnp.zeros_like(acc_ref)
            tail_ref[...] = jnp.zeros_like(tail_ref)

        h = h_ref[...]
        err = x_ref[...] - t_ref[...]
        for d in range(NG):
            c = []
            for s in range(2):
                j = s * NG + d
                a = _dot_nt(h, up_ref[j])
                a_ref[j] = a.astype(BF16)
                c.append(_conv(jnp.concatenate([tail_ref[j], a], axis=0), w_ref[s, d])[8:] + b_ref[s, d])
                tail_ref[j] = a[FM - 8:FM]
            gl, gg = _gelu_and_grad(c[0])
            gs_ref[d] = gl.astype(BF16)
            gs_ref[NG + d] = (gg * c[1]).astype(BF16)
            u = (gl * c[1]).astype(BF16)
            u_ref[d] = u
            err = err + _dot(u, dn_ref[d])
        dy_ref[...] = err * (1.0 / D)
        acc_ref[...] += jnp.sum(err * err, axis=0, keepdims=True)

        @pl.when(i == pl.num_programs(0) - 1)
        def _():
            loss_ref[...] = jnp.full((8, 128), 0.5 / D, F32) * jnp.sum(acc_ref[...])

    return pl.pallas_call(
        body, name="ffn_fwd", grid=(t // FM,),
        in_specs=[_rows(FM, D), _rows(FM, D), _rows(FM, D),
                  _resident((NDEV, SW, D)), _resident((2, NG, 3, SW)), _resident((2, NG, 1, SW)), _resident((NG, SW, D))],
        out_specs=[pl.BlockSpec((NDEV, FM, SW), lambda i: (0, i, 0)), pl.BlockSpec((NG, FM, SW), lambda i: (0, i, 0)),
                   pl.BlockSpec((NDEV, FM, SW), lambda i: (0, i, 0)), _rows(FM, D), _const((8, 128))],
        out_shape=[_sds((NDEV, t, SW), BF16), _sds((NG, t, SW), BF16), _sds((NDEV, t, SW), BF16), _sds((t, D)),
                   _sds((8, 128))],
        scratch_shapes=[pltpu.VMEM((1, D), F32), pltpu.VMEM((NDEV, 8, SW), F32)],
        compiler_params=_params())(h3, x2, target, up, conv, conv_b, down)


def _gelu_and_grad(z):
    z2 = z * z
    t = jnp.tanh(GELU_C0 * (z + GELU_C1 * z * z2))
    phi = 0.5 * (1.0 + t)
    return z * phi, phi + z * (1.0 - t * t) * (0.5 * GELU_C0 + (1.5 * GELU_C0 * GELU_C1) * z2)


def _ffn_bwd(dy, a, gs, x2, up, conv, down, ffn_norm):
    t = x2.shape[0]
    nt = t // FM
    n = FM + 8

    def body(dy_ref, a_ref, gs_ref, x_ref, up_ref, w_ref, dn_ref, g_ref, dx_ref, da_ref, s_ref, dfn_ref, head_ref):
        @pl.when(pl.program_id(0) == 0)
        def _():
            s_ref[...] = jnp.zeros_like(s_ref)
            dfn_ref[...] = jnp.zeros_like(dfn_ref)
            head_ref[...] = jnp.zeros_like(head_ref)

        dy = dy_ref[...]
        dyb = dy.astype(BF16)
        dh = jnp.zeros((FM, D), F32)
        row = lax.broadcasted_iota(jnp.int32, (8, SW), 0)
        for d in range(NG):
            du = _dot_nt(dyb, dn_ref[d])
            for s in range(2):
                j = s * NG + d
                k = NG + d if s == 0 else d
                dc0 = du * gs_ref[k].astype(F32)
                dc = jnp.concatenate([dc0, head_ref[j]], axis=0)
                head_ref[j] = dc0[0:8]
                w = w_ref[s, d]
                tile = a_ref[j].astype(F32)
                d1 = pltpu.roll(dc, n - 1, 0)
                d2 = pltpu.roll(d1, n - 1, 0)
                da = (w[2:3, :] * dc + w[1:2, :] * d1 + w[0:1, :] * d2)[0:FM].astype(BF16)
                da_ref[j] = da
                dh = dh + _dot(da, up_ref[j])
                sums = [jnp.sum(v[0:FM] * tile, axis=0, keepdims=True) for v in (d2, d1, dc)]
                sums.append(jnp.sum(dc[0:FM], axis=0, keepdims=True))
                upd = jnp.zeros((8, SW), F32)
                for r, v in enumerate(sums):
                    upd = jnp.where(row == r, jnp.broadcast_to(v, (8, SW)), upd)
                s_ref[s, d] += upd
        x = x_ref[...]
        dx, dg = _rms_bwd(dh, x, _rs(x), g_ref[...])
        dx_ref[...] = dy + dx
        dfn_ref[...] += _colsum8(dg)

    rows = pl.BlockSpec((FM, D), lambda i: (nt - 1 - i, 0))
    blocks = pl.BlockSpec((NDEV, FM, SW), lambda i: (0, nt - 1 - i, 0))
    return pl.pallas_call(
        body, name="ffn_bwd", grid=(nt,),
        in_specs=[rows, blocks, blocks, rows, _resident((NDEV, SW, D)), _resident((2, NG, 3, SW)), _resident((NG, SW, D)),
                  _const((1, D))],
        out_specs=[rows, blocks, _const((2, NG, 8, SW)), _const((8, D))],
        out_shape=[_sds((t, D)), _sds((NDEV, t, SW), BF16), _sds((2, NG, 8, SW)), _sds((8, D))],
        scratch_shapes=[pltpu.VMEM((NDEV, 8, SW), F32)],
        compiler_params=_params())(dy, a, gs, x2, up, conv, down, ffn_norm)


BT = 512


def _xattn_bwd(dx2, x1, qpre, k2, v2, wq, wo, qn4, xan):
    t = x1.shape[0]

    def body(dx2_ref, x1_ref, qpre_ref, k_ref, v_ref, wq_ref, wo_ref, qn_ref, xan_ref,
             dx1_ref, dqpre_ref, dk_ref, dv_ref, dqn_ref, dxan_ref):
        @pl.when(pl.program_id(0) == 0)
        def _():
            for r in (dk_ref, dv_ref, dqn_ref, dxan_ref):
                r[...] = jnp.zeros_like(r)

        dx2 = dx2_ref[...]
        do = _dot_nt(dx2.astype(BF16), wo_ref[...])
        dqs = []
        for h in range(XH):
            sl = slice(XD * h, XD * (h + 1))
            qpre_h = qpre_ref[:, sl]
            pn, q2, rq = _xattn_probs(qpre_h, qn_ref[:, sl], k_ref[:, sl])
            do_h = do[:, sl].astype(BF16)
            dp = _dot_nt(do_h, v_ref[:, sl])
            ds = (pn * (dp - jnp.sum(pn * dp, axis=-1, keepdims=True)) * (1.0 / math.sqrt(XD))).astype(BF16)
            dq2 = _dot(ds, k_ref[:, sl])
            dk_ref[:, sl] += _dot_tn(ds, q2)
            dv_ref[:, sl] += _dot_tn(pn.astype(BF16), do_h)
            dqh, dg = _rms_bwd(dq2, qpre_h, rq, qn_ref[:, sl])
            dqn_ref[...] += _colsum8(dg)
            dqs.append(dqh)
        dqpre = jnp.concatenate(dqs, axis=1).astype(BF16)
        dqpre_ref[...] = dqpre
        dh2 = _dot_nt(dqpre, wq_ref[...])
        x1 = x1_ref[...]
        dx, dg = _rms_bwd(dh2, x1, _rs(x1), xan_ref[...])
        dx1_ref[...] = dx2 + dx
        dxan_ref[...] += _colsum8(dg)

    return pl.pallas_call(
        body, name="xattn_bwd", grid=(t // BT,),
        in_specs=[_rows(BT, D), _rows(BT, D), _rows(BT, D), _const((MEM, D)), _const((MEM, D)), _const((D, D)),
                  _const((D, D)), _const((1, D)), _const((1, D))],
        out_specs=[_rows(BT, D), _rows(BT, D), _const((MEM, D)), _const((MEM, D)), _const((8, XD)), _const((8, D))],
        out_shape=[_sds((t, D)), _sds((t, D), BF16), _sds((MEM, D)), _sds((MEM, D)), _sds((8, XD)), _sds((8, D))],
        compiler_params=_params())(dx2, x1, qpre, k2, v2, wq, wo, qn4, xan)


def _mem_kv_bwd(mem, mh, kpre, dk2, dv2, wkv, kn4, mem_norm):
    def body(m_ref, mh_ref, kpre_ref, dk_ref, dv_ref, w_ref, kn_ref, g_ref, dw_ref, dkn_ref, dmn_ref):
        dkn = jnp.zeros((8, XD), F32)
        dm = jnp.zeros((MEM, D), F32)
        mh = mh_ref[...]
        for h in range(XH):
            sl = slice(XD * h, XD * (h + 1))
            kh = kpre_ref[:, sl]
            dkh, dg = _rms_bwd(dk_ref[:, sl], kh, _rs(kh), kn_ref[:, sl])
            dkn = dkn + _colsum8(dg)
            dkh = dkh.astype(BF16)
            dvh = dv_ref[:, sl].astype(BF16)
            dw_ref[h] = _dot_tn(mh, dkh).astype(BF16)
            dw_ref[XH + h] = _dot_tn(mh, dvh).astype(BF16)
            dm = dm + _dot_nt(dkh, w_ref[h]) + _dot_nt(dvh, w_ref[XH + h])
        dkn_ref[...] = dkn
        m = m_ref[...]
        _, dg = _rms_bwd(dm, m, _rs(m), g_ref[...])
        dmn_ref[...] = _colsum8(dg)

    return pl.pallas_call(
        body, name="mem_kv_bwd", out_shape=[_sds((2 * XH, D, XD), BF16), _sds((8, XD)), _sds((8, D))],
        compiler_params=pltpu.CompilerParams(vmem_limit_bytes=VMEM_LIMIT))(mem, mh, kpre, dk2, dv2, wkv, kn4, mem_norm)


def _mixer_out_bwd(dx1, attn, gm, w_out, aon, gon, gvn, gz, ws, wst, bfull, gvw):
    t = dx1.shape[0]
    nc = TM // BLK

    def body(dx_ref, a_ref, g_ref, wo_ref, aon_ref, gon_ref, x_ref, gz_ref, w_ref, wt_ref, b_ref, gvw_ref,
             da_ref, dan_ref, dgn_ref, dgz_ref, dw_ref, db_ref, dgvw_ref, dbacc_ref):
        @pl.when(pl.program_id(0) == 0)
        def _():
            for r in (dan_ref, dgn_ref, dw_ref, dbacc_ref, dgvw_ref):
                r[...] = jnp.zeros_like(r)

        dy = _dot_nt(dx_ref[...].astype(BF16), wo_ref[...])
        a, g = a_ref[...], g_ref[...]
        da, dna = _rms_bwd(dy[:, :AW], a, _rs(a), aon_ref[...])
        dgm, dng = _rms_bwd(dy[:, AW:], g, _rs(g), gon_ref[...])
        da_ref[...] = da
        dan_ref[...] += _colsum8(dna)
        dgn_ref[...] += _colsum8(dng)

        xin = x_ref[...]
        mixed = _gmlp_mix(w_ref, xin, False) + jnp.tile(b_ref[...], (nc, 1))
        dgu = dgm * mixed
        dmixed = dgm * _gelu(gz_ref[:, :GW])
        lo = _lane((BLK, 128)) < 64
        dbias = jnp.zeros((BLK, GW), F32)
        for c in range(nc):
            dmc = dmixed[c * BLK:(c + 1) * BLK]
            dbias = dbias + dmc
            for j in range(4):
                dm2 = dmc[:, 128 * j:128 * (j + 1)]
                xs = xin[c * BLK:(c + 1) * BLK, 128 * j:128 * (j + 1)]
                z = jnp.zeros_like(dm2)
                dw_ref[2 * j] += _dot_nt(jnp.where(lo, dm2, z).astype(BF16), xs)
                dw_ref[2 * j + 1] += _dot_nt(jnp.where(lo, z, dm2).astype(BF16), xs)
        dbacc_ref[...] += dbias
        dgvn = _gmlp_mix(wt_ref, dmixed.astype(BF16), True)
        gz_u, gz_v = gz_ref[:, :GW], gz_ref[:, GW:]
        gv = _gelu(gz_v)
        dgv, dg = _rms_bwd(dgvn, gv, _rs(gv), gvw_ref[...])
        dgvw_ref[...] += _colsum8(dg)
        dgz_ref[:, :GW] = (dgu * _gelu_grad(gz_u)).astype(BF16)
        dgz_ref[:, GW:] = (dgv * _gelu_grad(gz_v)).astype(BF16)

        @pl.when(pl.program_id(0) == pl.num_programs(0) - 1)
        def _():
            s = dbacc_ref[...]
            sel = (lax.broadcasted_iota(jnp.int32, (8, GW), 1) // HD
                   == lax.broadcasted_iota(jnp.int32, (8, GW), 0)).astype(BF16)
            hi = s.astype(BF16)
            r1 = s - hi.astype(F32)
            mid = r1.astype(BF16)
            lo = (r1 - mid.astype(F32)).astype(BF16)
            db_ref[...] = _dot_nt(sel, hi) + _dot_nt(sel, mid) + _dot_nt(sel, lo)
            r = lax.broadcasted_iota(jnp.int32, (BLK, BLK), 0)
            c = lax.broadcasted_iota(jnp.int32, (BLK, BLK), 1)
            for h in range(8):
                dw_ref[h] = jnp.where(c <= r, dw_ref[h], 0.0)

    return pl.pallas_call(
        body, name="mixer_out_bwd", grid=(t // TM,),
        in_specs=[_rows(TM, D), _rows(TM, AW), _rows(TM, GW), _const((D, D)), _const((1, AW)), _const((1, GW)),
                  _rows(TM, GW), _rows(TM, 2 * GW), _const((8, BLK, BLK)), _const((8, BLK, BLK)),
                  _const((BLK, GW)), _const((1, GW))],
        out_specs=[_rows(TM, AW), _const((8, AW)), _const((8, GW)), _rows(TM, 2 * GW), _const((8, BLK, BLK)),
                   _const((8, BLK)), _const((8, GW))],
        out_shape=[_sds((t, AW)), _sds((8, AW)), _sds((8, GW)), _sds((t, 2 * GW), BF16), _sds((8, BLK, BLK)),
                   _sds((8, BLK)), _sds((8, GW))],
        scratch_shapes=[pltpu.VMEM((BLK, GW), F32)],
        compiler_params=_params())(dx1, attn, gm, w_out, aon, gon, gvn, gz, ws, wst, bfull, gvw)


def _fold_half(v):
    return v + pltpu.roll(v, 64, 1)


def _swa_bwd(q, k, v, dattn, sinks):
    t = q.shape[0]
    nb = t // BLK
    ts = min(t, SB_BWD * BLK)
    sb = ts // BLK
    nt = t // ts

    def body(sink_ref, q_ref, kc_ref, kp_ref, vc_ref, vp_ref, do_ref, dq_ref, dk_ref, dv_ref, ds_ref,
             ck_ref, cv_ref, sacc_ref):
        i = pl.program_id(0)

        @pl.when(i == 0)
        def _():
            ck_ref[...] = jnp.zeros_like(ck_ref)
            cv_ref[...] = jnp.zeros_like(cv_ref)
            sacc_ref[...] = jnp.zeros_like(sacc_ref)

        @pl.when(i < nt)
        def _():
            kk = jnp.concatenate([kp_ref[...], kc_ref[...]], axis=0).astype(F32)
            vv = jnp.concatenate([vp_ref[...], vc_ref[...]], axis=0).astype(F32)
            lo256 = _lane((2 * BLK, 128)) < 64
            acc_k = [jnp.zeros((BLK, 128), F32) for _ in range(sb + 1)]
            acc_v = [jnp.zeros((BLK, 128), F32) for _ in range(sb + 1)]
            for b in range(sb):
                r = slice(b * BLK, (b + 1) * BLK)
                kb, vb = kk[b * BLK:(b + 2) * BLK], vv[b * BLK:(b + 2) * BLK]
                mask = _swa_mask(i == 0) if b == 0 else _swa_mask(False)
                dkk = jnp.zeros((2 * BLK, 128), F32)
                dvv = jnp.zeros((2 * BLK, 128), F32)
                for g in range(2):
                    qs = _stack_heads(q_ref[r, 256 * g:256 * g + 128], q_ref[r, 256 * g + 128:256 * g + 256])
                    dos = _stack_heads(do_ref[r, 256 * g:256 * g + 128],
                                       do_ref[r, 256 * g + 128:256 * g + 256]).astype(BF16)
                    kd = _dup_half(kb, g)
                    pn, psn = _swa_probs(qs, kd, mask, _sink_col(sink_ref, g))
                    dp = _dot_nt(dos, _dup_half(vb, g))
                    dd = jnp.sum(pn * dp, axis=-1, keepdims=True)
                    ds = (pn * (dp - dd) * (1.0 / math.sqrt(HD))).astype(BF16)
                    sacc_ref[g] += jnp.broadcast_to(-psn * dd, (4 * BLK, 128))
                    dqa, dqb = _unstack_heads(_dot(ds, kd))
                    dq_ref[r, 256 * g:256 * g + 128] = dqa
                    dq_ref[r, 256 * g + 128:256 * g + 256] = dqb
                    dkg = _fold_half(_dot_tn(ds, qs))
                    dvg = _fold_half(_dot_tn(pn.astype(BF16), dos))
                    keep = lo256 if g == 0 else jnp.logical_not(lo256)
                    dkk = jnp.where(keep, dkg, dkk)
                    dvv = jnp.where(keep, dvg, dvv)
                acc_k[b], acc_k[b + 1] = acc_k[b] + dkk[0:BLK], acc_k[b + 1] + dkk[BLK:]
                acc_v[b], acc_v[b + 1] = acc_v[b] + dvv[0:BLK], acc_v[b + 1] + dvv[BLK:]
            for out_ref, c_ref, acc in ((dk_ref, ck_ref, acc_k), (dv_ref, cv_ref, acc_v)):
                if sb > 1:
                    out_ref[0:ts - BLK] = c_ref[0:ts - BLK]
                out_ref[ts - BLK:ts] = c_ref[ts - BLK:ts] + acc[0]
                for b in range(sb):
                    c_ref[b * BLK:(b + 1) * BLK] = acc[b + 1]

        @pl.when(i == nt)
        def _():
            dk_ref[...] = ck_ref[...]
            dv_ref[...] = cv_ref[...]
            lane = _lane((8, 128))
            acc = jnp.zeros((8, 128), F32)
            for g in range(2):
                for j in range(4):
                    val = jnp.sum(sacc_ref[g, j * BLK:(j + 1) * BLK, :], axis=0, keepdims=True)
                    acc = jnp.where(lane == 4 * g + j, jnp.broadcast_to(val, (8, 128)), acc)
            ds_ref[...] = acc

    cur = lambda i: (jnp.minimum(i, nt - 1), 0)
    before = lambda i: (jnp.clip(i * sb - 1, 0, nb - 1), 0)
    done = lambda i: (jnp.clip(i - 1, 0, nt - 1), 0)
    return pl.pallas_call(
        body, name="swa_bwd", grid=(nt + 1,),
        in_specs=[pl.BlockSpec(memory_space=pltpu.SMEM), pl.BlockSpec((ts, AW), cur),
                  pl.BlockSpec((ts, KW), cur), pl.BlockSpec((BLK, KW), before),
                  pl.BlockSpec((ts, KW), cur), pl.BlockSpec((BLK, KW), before), pl.BlockSpec((ts, AW), cur)],
        out_specs=[pl.BlockSpec((ts, AW), cur), pl.BlockSpec((ts, KW), done), pl.BlockSpec((ts, KW), done),
                   _const((8, 128))],
        out_shape=[_sds((t, AW)), _sds((t, KW)), _sds((t, KW)), _sds((8, 128))],
        scratch_shapes=[pltpu.VMEM((ts, KW), F32), pltpu.VMEM((ts, KW), F32), pltpu.VMEM((2, 4 * BLK, 128), F32)],
        compiler_params=_params())(sinks, q, k, k, v, v, dattn)


def _mixer_in_bwd(dq, dk, dv, dgz, qk, cos, sin, x, dx1, w_in, mix_norm, qn, kn):
    t = x.shape[0]

    def body(dq_ref, dk_ref, dv_ref, dgz_ref, qk_ref, cos_ref, sin_ref, x_ref, dx1_ref, w_ref, g_ref, qn_ref, kn_ref,
             ones_ref, gx_ref, dproj_ref, dmn_ref, dqn_ref, dkn_ref, qacc_ref, kacc_ref):
        i = pl.program_id(0)

        @pl.when(i == 0)
        def _():
            dmn_ref[...] = jnp.zeros_like(dmn_ref)
            qacc_ref[...] = jnp.zeros_like(qacc_ref)
            kacc_ref[...] = jnp.zeros_like(kacc_ref)

        cos2, sin2 = cos_ref[...], sin_ref[...]
        qpre, kpre = qk_ref[:, :AW], qk_ref[:, AW:]
        dqh = _rope_bwd(dq_ref[...], jnp.tile(cos2, (1, 4)), jnp.tile(sin2, (1, 4)))
        dqpre, dgq = _rms64_bwd(dqh, qpre, _rs64(qpre, ones_ref), qn_ref[...], ones_ref)
        dkh = _rope_bwd(dk_ref[...], cos2, sin2)
        dkpre, dgk = _rms64_bwd(dkh, kpre, _rs64(kpre, ones_ref), kn_ref[...], ones_ref)
        qacc_ref[...] += jnp.sum(dgq, axis=0, keepdims=True)
        kacc_ref[...] += jnp.sum(dgk, axis=0, keepdims=True)
        dproj = jnp.concatenate([dqpre.astype(BF16), dkpre.astype(BF16), dv_ref[...].astype(BF16), dgz_ref[...]], axis=1)
        dproj_ref[...] = dproj
        dh = _dot(dproj, w_ref[...])
        xv = x_ref[...]
        dx, dg = _rms_bwd(dh, xv, _rs(xv), g_ref[...])
        gx_ref[...] = dx1_ref[...] + dx
        dmn_ref[...] += _colsum8(dg)

        @pl.when(i == pl.num_programs(0) - 1)
        def _():
            qa = qacc_ref[...]
            q4 = qa[:, 0:128] + qa[:, 128:256] + qa[:, 256:384] + qa[:, 384:512]
            dqn_ref[...] = jnp.broadcast_to(_fold_half(q4), (8, 128))
            dkn_ref[...] = jnp.broadcast_to(_fold_half(kacc_ref[...]), (8, 128))

    return pl.pallas_call(
        body, name="mixer_in_bwd", grid=(t // TM,),
        in_specs=[_rows(TM, AW), _rows(TM, KW), _rows(TM, KW), _rows(TM, 2 * GW), _rows(TM, AW + KW), _rows(TM, 128),
                  _rows(TM, 128), _rows(TM, D), _rows(TM, D), _const((IN, D)), _const((1, D)), _const((1, AW)),
                  _const((1, KW)), _const((AW, AW))],
        out_specs=[_rows(TM, D), _rows(TM, IN), _const((8, D)), _const((8, 128)), _const((8, 128))],
        out_shape=[_sds((t, D)), _sds((t, IN), BF16), _sds((8, D)), _sds((8, 128)), _sds((8, 128))],
        scratch_shapes=[pltpu.VMEM((1, AW), F32), pltpu.VMEM((1, KW), F32)],
        compiler_params=_params())(dq, dk, dv, dgz, qk, cos, sin, x, dx1, w_in, mix_norm, qn, kn, _head_ones())


def _local_step(x, mem, pos, target, p, fetch, ship):
    t = x.shape[0]
    p = dict(p)
    p.update(fetch(0, None))
    inv_freq = 1.0 / (ROPE_THETA ** (jnp.arange(HD // 2, dtype=F32) * (2.0 / HD)))
    cos, sin = _rope_tables(pos, jnp.tile(inv_freq, 4).reshape(1, 128))
    qn = jnp.tile(p["q_norm"], (1, AW // HD))
    kn = jnp.tile(p["k_norm"], (1, KW // HD))
    qn4 = jnp.tile(p["xa_q_norm"], (1, XH))
    kn4 = jnp.tile(p["xa_k_norm"], (1, XH))
    ws = p["gmlp_ws"]
    wst = jnp.swapaxes(ws, 1, 2)
    bfull = jnp.repeat(p["gmlp_bs"].T, HD, axis=1)
    conv_b = p["ffn_conv_b"]

    h1, qk, gz, q, k, v, gvn = _mixer_in_fwd(x, p["mix_norm"], p["w_in"], qn, kn, p["gmlp_v_norm"], cos, sin)
    attn = _swa_fwd(q, k, v, p["attn_sinks"])
    p.update(fetch(1, attn))
    gm, ycat, x1, h2 = _mixer_out_fwd(attn, gvn, gz, ws, bfull, x, p["w_out"], p["attn_out_norm"], p["gmlp_out_norm"],
                                      p["xa_norm"])
    mh, kpre, k2, v2 = _mem_kv_fwd(mem, p["mem_norm"], p["xa_wkv"], kn4)
    qpre, o, x2, h3 = _xattn_fwd(h2, x1, p["xa_wq"], qn4, k2, v2, p["xa_wo"], p["ffn_norm"])
    p.update(fetch(2, h3))
    conv = p["ffn_conv"]
    a, u, gs, dy, loss8 = _ffn_fwd(h3, x2, target, p["ffn_up"], conv, conv_b, p["ffn_down"])

    raw = {}
    d_down = _mm_tn(u, dy, "ffn_down_bwd_w")
    dx2, da, raw["conv_sums"], raw["ffn_norm"] = _ffn_bwd(dy, a, gs, x2, p["ffn_up"], conv, p["ffn_down"], p["ffn_norm"])
    d_up = _mm_tn(da, h3, "ffn_up_bwd_w")
    token = ship(0, {"ffn_down": d_down, "ffn_up": d_up, "ffn_conv": raw["conv_sums"][:, :, 0:3]})
    dx1, dqpre, dk2, dv2, raw["xa_q_norm"], raw["xa_norm"] = _xattn_bwd(
        dx2, x1, qpre, k2, v2, p["xa_wq"], p["xa_wo"], qn4 + jnp.tile(token[0:1], (1, D // 128)), p["xa_norm"])
    d_wo = _mm_tn(o, dx2, "xa_wo_bwd_w")
    d_wq = _mm_tn(h2, dqpre, "xa_wq_bwd_w")
    d_wkv, raw["xa_k_norm"], raw["mem_norm"] = _mem_kv_bwd(mem, mh, kpre, dk2, dv2, p["xa_wkv"], kn4, p["mem_norm"])
    d_w_out = _mm_tn(ycat, dx1, "w_out_bwd_w")
    (dattn, raw["attn_out_norm"], raw["gmlp_out_norm"], dgz, raw["gmlp_ws"], raw["gmlp_bs"],
     raw["gmlp_v_norm"]) = _mixer_out_bwd(dx1, attn, gm, p["w_out"], p["attn_out_norm"], p["gmlp_out_norm"],
                                          gvn, gz, ws, wst, bfull, p["gmlp_v_norm"])
    token = ship(1, {"xa_wo": d_wo, "xa_wq": d_wq, "xa_wkv": d_wkv, "w_out": d_w_out}, [raw["gmlp_ws"]])
    dq, dk, dv, raw["attn_sinks"] = _swa_bwd(q, k, v, dattn, p["attn_sinks"] + token[0:1, 0:8])
    grad_x, dproj, raw["mix_norm"], raw["q_norm"], raw["k_norm"] = _mixer_in_bwd(
        dq, dk, dv, dgz, qk, cos, sin, x, dx1, p["w_in"], p["mix_norm"], qn, kn)
    d_w_in = _mm_tn(dproj, h1, "w_in_bwd_w")
    raw["loss"] = loss8
    return grad_x, {"w_in": d_w_in}, raw


def _cast_shards(shards):
    def body(*refs):
        n = len(refs) // 2
        for i_ref, o_ref in zip(refs[:n], refs[n:]):
            o_ref[...] = i_ref[...].astype(BF16)

    return pl.pallas_call(body, name="cast_shards", out_shape=[_sds(s.shape, BF16) for s in shards],
                          compiler_params=pltpu.CompilerParams(vmem_limit_bytes=VMEM_LIMIT))(*shards)


HBM_SPEC = pl.BlockSpec(memory_space=pltpu.HBM)
SEM_SPEC = pl.BlockSpec(memory_space=pltpu.SEMAPHORE)


ALL_K = tuple(range(1, NDEV))
CHIP_K = (1, 2, 4, 6)
RELAY_K = (2, 4, 6)


def _peer(k):
    x, y, cc = lax.axis_index("x"), lax.axis_index("y"), lax.axis_index("c")
    return 1 - x if k & 4 else x, 1 - y if k & 2 else y, 1 - cc if k & 1 else cc


def _remote_copies(src_refs, land_refs, send_refs, recv_refs, nd, ks):
    me = 4 * lax.axis_index("x") + 2 * lax.axis_index("y") + lax.axis_index("c")
    copies = []
    for a, (src_ref, land_ref) in enumerate(zip(src_refs, land_refs)):
        for j, k in enumerate(ks):
            px, py, pc = _peer(k)
            copies.append((k, pltpu.make_async_remote_copy(
                src_ref=src_ref.at[4 * px + 2 * py + pc] if a < nd else src_ref, dst_ref=land_ref.at[me],
                send_sem=send_refs[a].at[j], recv_sem=recv_refs[a].at[j],
                device_id=(px, py, pc), device_id_type=pl.DeviceIdType.MESH)))
    return copies


def _relay_copies(land_refs, send_refs, recv_refs):
    copies = []
    for a, land_ref in enumerate(land_refs):
        for j, k in enumerate(RELAY_K):
            px, py, pc = _peer(k)
            slot = land_ref.at[4 * px + 2 * py + pc]
            copies.append(pltpu.make_async_remote_copy(
                src_ref=slot, dst_ref=slot, send_sem=send_refs[a].at[j], recv_sem=recv_refs[a].at[j],
                device_id=_peer(1), device_id_type=pl.DeviceIdType.MESH))
    return copies


def _own_slot(src, by_dest, me):
    block = lax.dynamic_index_in_dim(src, me, 0, keepdims=True) if by_dest else src[None]
    return lax.dynamic_update_index_in_dim(lax.empty((NDEV,) + block.shape[1:], src.dtype), block, me, 0)


SIDE_EFFECT = pltpu.CompilerParams(has_side_effects=pltpu.SideEffectType.DATAFLOW_SIDE_EFFECTING)


def _exchange_start(by_dest, for_all, me, name, ks=ALL_K):
    srcs = list(by_dest) + list(for_all)
    n, nd = len(srcs), len(by_dest)
    lands = [_own_slot(s, a < nd, me) for a, s in enumerate(srcs)]

    def body(*refs):
        for _, cp in _remote_copies(refs[:n], refs[n:2 * n], refs[2 * n:3 * n], refs[3 * n:4 * n], nd, ks):
            cp.start()
        refs[-1][...] = jnp.zeros((8, 128), F32)

    sems = [pltpu.SemaphoreType.DMA((len(ks),))] * (2 * n)
    thru = [pltpu.HBM(v.shape, v.dtype) for v in srcs + lands]
    res = pl.pallas_call(
        body, name=name, out_shape=sems + thru + [_sds((8, 128))],
        in_specs=[HBM_SPEC] * (2 * n), out_specs=[SEM_SPEC] * (2 * n) + [HBM_SPEC] * (2 * n) + [pl.BlockSpec(memory_space=pltpu.VMEM)],
        input_output_aliases={i: 2 * n + i for i in range(2 * n)}, compiler_params=SIDE_EFFECT)(
            *[pltpu.with_memory_space_constraint(v, pltpu.HBM) for v in srcs + lands])
    return (res[:2 * n], res[2 * n:4 * n], nd, ks, None), res[-1]


def _exchange_relay(state, after, name):
    sems, thru, nd, ks, _ = state
    n = len(thru) // 2

    def body(*refs):
        for k, cp in _remote_copies(refs[:n], refs[n:2 * n], refs[2 * n:3 * n], refs[3 * n:4 * n], nd, ks):
            if k in RELAY_K:
                cp.wait_recv()
        for cp in _relay_copies(refs[n:2 * n], refs[4 * n + 1:5 * n + 1], refs[5 * n + 1:6 * n + 1]):
            cp.start()

    relay_sems = [pltpu.SemaphoreType.DMA((len(RELAY_K),))] * (2 * n)
    res = pl.pallas_call(
        body, name=name, out_shape=relay_sems + [pltpu.HBM(v.shape, v.dtype) for v in thru],
        in_specs=[HBM_SPEC] * (2 * n) + [SEM_SPEC] * (2 * n) + [pl.BlockSpec(memory_space=pl.ANY)],
        out_specs=[SEM_SPEC] * (2 * n) + [HBM_SPEC] * (2 * n),
        input_output_aliases={i: 2 * n + i for i in range(2 * n)}, compiler_params=SIDE_EFFECT)(*thru, *sems, after)
    return sems, res[2 * n:], nd, ks, res[:2 * n]


def _exchange_wait(state, after, name):
    sems, thru, nd, ks, relay_sems = state
    n = len(thru) // 2

    def body(*refs):
        for k, cp in _remote_copies(refs[:n], refs[n:2 * n], refs[2 * n:3 * n], refs[3 * n:4 * n], nd, ks):
            cp.wait_send()
            if relay_sems is None or k not in RELAY_K:
                cp.wait_recv()
        if relay_sems is not None:
            for cp in _relay_copies(refs[n:2 * n], refs[4 * n:5 * n], refs[5 * n:6 * n]):
                cp.wait_send()
                cp.wait_recv()

    extra = [] if relay_sems is None else list(relay_sems)
    res = pl.pallas_call(
        body, name=name, out_shape=[pltpu.HBM(v.shape, v.dtype) for v in thru],
        in_specs=[HBM_SPEC] * (2 * n) + [SEM_SPEC] * (2 * n + len(extra)) + [pl.BlockSpec(memory_space=pl.ANY)],
        out_specs=[HBM_SPEC] * (2 * n), input_output_aliases={i: i for i in range(2 * n)}, compiler_params=SIDE_EFFECT)(
            *thru, *sems, *extra, after)
    return res[n:]


def _adam(items, name):
    n = len(items)

    def body(*refs):
        for j in range(n):
            p_ref, w_ref, m_ref, v_ref = refs[4 * j:4 * j + 4]
            g_ref, d_ref, nm_ref, nv_ref = refs[4 * n + 4 * j:4 * n + 4 * j + 4]
            g = _sum_parts(p_ref)
            g_ref[...] = g
            d_ref[...], nm_ref[...], nv_ref[...] = _adam_math(g, w_ref[...], m_ref[...], v_ref[...])

    res = pl.pallas_call(
        body, name=name, out_shape=[_sds(it[1].shape) for it in items for _ in range(4)],
        compiler_params=pltpu.CompilerParams(vmem_limit_bytes=VMEM_LIMIT))(*[a for it in items for a in it])
    return [res[4 * j:4 * j + 4] for j in range(n)]


GATHER_GROUPS = (("w_in",), ("w_out", "xa_wkv", "xa_wq", "xa_wo"), ("ffn_up", "ffn_conv", "ffn_down"))
SCATTER_GROUPS = (("ffn_up", "ffn_down", "ffn_conv"), ("xa_wo", "xa_wq", "xa_wkv", "w_out"), ("w_in",))
ADAM_ALONE = ("ffn_up",)
BIG = tuple(n for grp in GATHER_GROUPS for n in grp)
BY_COLUMN = ("w_in", "ffn_up")
VECS = (("mix_norm", D), ("q_norm", HD), ("k_norm", HD), ("attn_sinks", 8), ("gmlp_v_norm", GW), ("attn_out_norm", AW),
        ("gmlp_out_norm", GW), ("xa_norm", D), ("mem_norm", D), ("xa_q_norm", XD), ("xa_k_norm", XD), ("ffn_norm", D))
BS_ROW = 16
VEC_ROWS = 24
SMALL = tuple(n for n, _ in VECS) + ("gmlp_bs", "gmlp_ws", "ffn_conv_b")


def _pack_small(raw):
    names = [n for n, _ in VECS] + ["gmlp_bs", "conv_sums"]

    def body(*refs):
        ins = dict(zip(names, refs))
        vec_ref, cb_ref = refs[len(names):]
        vec_ref[...] = jnp.zeros_like(vec_ref)
        for r, (n, w) in enumerate(VECS):
            vec_ref[r:r + 1, 0:w] = ins[n][0:1, 0:w]
        vec_ref[BS_ROW:BS_ROW + 8, 0:BLK] = ins["gmlp_bs"][...]
        for s in range(2):
            for d in range(NG):
                cb_ref[s, d] = ins["conv_sums"][s, d, 3:4, :]

    return pl.pallas_call(body, name="pack_small", out_shape=[_sds((VEC_ROWS, D)), _sds((2, NG, 1, SW))])(
        *[raw[n] for n in names])


def _adam_math(g, w, m, v):
    nm = B1 * m + (1.0 - B1) * g
    nv = B2 * v + (1.0 - B2) * (g * g)
    m_hat = nm / (1.0 - B1 ** STEP)
    v_hat = nv / (1.0 - B2 ** STEP)
    return -LR * (m_hat / (jnp.sqrt(v_hat) + AEPS) + WD * w), nm, nv


def _sum_parts(p_ref):
    g = p_ref[0].astype(F32)
    for j in range(1, NDEV):
        g = g + p_ref[j].astype(F32)
    return g


def _adam_small(parts_vec, parts_ws, parts_cb, w, m, v):
    def body(*refs):
        pv_ref, pws_ref, pcb_ref = refs[:3]
        ins = refs[3:3 + 3 * len(SMALL)]
        outs = refs[3 + 3 * len(SMALL):]
        gv = _sum_parts(pv_ref)
        for j, n in enumerate(SMALL):
            w_ref, m_ref, v_ref = ins[3 * j:3 * j + 3]
            o = outs[4 * j:4 * j + 4]
            if n == "gmlp_ws":
                g = _sum_parts(pws_ref)
            elif n == "ffn_conv_b":
                g = _sum_parts(pcb_ref)
            elif n == "gmlp_bs":
                g = gv[BS_ROW:BS_ROW + 8, 0:BLK]
            else:
                g = gv[j:j + 1, 0:VECS[j][1]]
            lead = n in ("gmlp_ws", "gmlp_bs")
            res = (g,) + _adam_math(g, w_ref[0] if lead else w_ref[...], m_ref[0] if lead else m_ref[...],
                                    v_ref[0] if lead else v_ref[...])
            for o_ref, val in zip(o, res):
                if lead:
                    o_ref[0] = val
                else:
                    o_ref[...] = val

    args = [parts_vec, parts_ws, parts_cb] + [d[n] for n in SMALL for d in (w, m, v)]
    res = pl.pallas_call(body, name="adam_small", out_shape=[_sds(w[n].shape) for n in SMALL for _ in range(4)],
                         compiler_params=pltpu.CompilerParams(vmem_limit_bytes=VMEM_LIMIT))(*args)
    return {n: tuple(res[4 * j:4 * j + 4]) for j, n in enumerate(SMALL)}


def kernel(x, mem, positions, mix_norm, w_in, q_norm, k_norm, attn_sinks, gmlp_v_norm, gmlp_ws, gmlp_bs, attn_out_norm, gmlp_out_norm, w_out, xa_norm, mem_norm, xa_wq, xa_wkv, xa_q_norm, xa_k_norm, xa_wo, ffn_norm, ffn_up, ffn_conv, ffn_conv_b, ffn_down, loss_target, m_mix_norm, m_w_in, m_q_norm, m_k_norm, m_attn_sinks, m_gmlp_v_norm, m_gmlp_ws, m_gmlp_bs, m_attn_out_norm, m_gmlp_out_norm, m_w_out, m_xa_norm, m_mem_norm, m_xa_wq, m_xa_wkv, m_xa_q_norm, m_xa_k_norm, m_xa_wo, m_ffn_norm, m_ffn_up, m_ffn_conv, m_ffn_conv_b, m_ffn_down, v_mix_norm, v_w_in, v_q_norm, v_k_norm, v_attn_sinks, v_gmlp_v_norm, v_gmlp_ws, v_gmlp_bs, v_attn_out_norm, v_gmlp_out_norm, v_w_out, v_xa_norm, v_mem_norm, v_xa_wq, v_xa_wkv, v_xa_q_norm, v_xa_k_norm, v_xa_wo, v_ffn_norm, v_ffn_up, v_ffn_conv, v_ffn_conv_b, v_ffn_down):
    names = ("mix_norm", "w_in", "q_norm", "k_norm", "attn_sinks", "gmlp_v_norm", "gmlp_ws", "gmlp_bs", "attn_out_norm",
             "gmlp_out_norm", "w_out", "xa_norm", "mem_norm", "xa_wq", "xa_wkv", "xa_q_norm", "xa_k_norm", "xa_wo",
             "ffn_norm", "ffn_up", "ffn_conv", "ffn_conv_b", "ffn_down")
    w = dict(zip(names, (mix_norm, w_in, q_norm, k_norm, attn_sinks, gmlp_v_norm, gmlp_ws, gmlp_bs, attn_out_norm,
                         gmlp_out_norm, w_out, xa_norm, mem_norm, xa_wq, xa_wkv, xa_q_norm, xa_k_norm, xa_wo, ffn_norm,
                         ffn_up, ffn_conv, ffn_conv_b, ffn_down)))
    m = dict(zip(names, (m_mix_norm, m_w_in, m_q_norm, m_k_norm, m_attn_sinks, m_gmlp_v_norm, m_gmlp_ws, m_gmlp_bs,
                         m_attn_out_norm, m_gmlp_out_norm, m_w_out, m_xa_norm, m_mem_norm, m_xa_wq, m_xa_wkv,
                         m_xa_q_norm, m_xa_k_norm, m_xa_wo, m_ffn_norm, m_ffn_up, m_ffn_conv, m_ffn_conv_b, m_ffn_down)))
    v = dict(zip(names, (v_mix_norm, v_w_in, v_q_norm, v_k_norm, v_attn_sinks, v_gmlp_v_norm, v_gmlp_ws, v_gmlp_bs,
                         v_attn_out_norm, v_gmlp_out_norm, v_w_out, v_xa_norm, v_mem_norm, v_xa_wq, v_xa_wkv,
                         v_xa_q_norm, v_xa_k_norm, v_xa_wo, v_ffn_norm, v_ffn_up, v_ffn_conv, v_ffn_conv_b, v_ffn_down)))
    t = x.shape[1]

    me = 4 * lax.axis_index("x") + 2 * lax.axis_index("y") + lax.axis_index("c")

    def rows(a, n):
        return jnp.swapaxes(a[0], 0, 1) if n in BY_COLUMN else a[0]

    mats = [n for n in BIG if n != "ffn_conv"]
    shard = dict(zip(mats, _cast_shards([rows(w[n], n) for n in mats])), ffn_conv=w["ffn_conv"][0])
    gathers, tokens = zip(*[_exchange_start([], [shard[n] for n in grp], me, "gather_start_%d" % i,
                                            CHIP_K if i == len(GATHER_GROUPS) - 1 else ALL_K)
                            for i, grp in enumerate(GATHER_GROUPS)])

    def fetch(i, after):
        after = tokens[0] + tokens[1] + tokens[2] if after is None else after
        state = gathers[i]
        if state[3] == CHIP_K:
            state = _exchange_relay(state, after, "gather_relay_%d" % i)
        got = dict(zip(GATHER_GROUPS[i], _exchange_wait(state, after, "gather_wait_%d" % i)))
        if "w_in" in got:
            got["w_in"] = got["w_in"].reshape(IN, D)
        for n in ("w_out", "xa_wq", "xa_wo"):
            if n in got:
                got[n] = got[n].reshape(D, D)
        if "ffn_down" in got:
            got["ffn_down"] = got["ffn_down"].reshape(NG, SW, D)
            got["ffn_conv"] = got["ffn_conv"].reshape(2, NG, 3, SW)
        return got

    scatters = []

    def ship(i, grads, for_all=()):
        by_dest = [grads[n].reshape((NDEV,) + shard[n].shape) for n in SCATTER_GROUPS[i]]
        state, token = _exchange_start(by_dest, for_all, me, "scatter_start_%d" % i)
        scatters.append(state)
        return token

    conv_b = {k: d["ffn_conv_b"].reshape(NDEV, 1, SW) for k, d in (("w", w), ("m", m), ("v", v))}
    p = {n: w[n] for n in SMALL[:-1]}
    p["gmlp_ws"], p["gmlp_bs"] = w["gmlp_ws"][0], w["gmlp_bs"][0]
    p["ffn_conv_b"] = conv_b["w"].reshape(2, NG, 1, SW)
    grad_x, g, raw = _local_step(x[0], mem[0], positions.reshape(t, 1), loss_target[0], p, fetch, ship)
    loss = lax.psum(raw["loss"][0, 0], ("x", "y", "c"))

    vec, cb = _pack_small(raw)
    after = ship(2, g, [vec, cb.reshape(NDEV, 1, SW)])
    res, rest = {}, []
    for i, grp in enumerate(SCATTER_GROUPS):
        got = _exchange_wait(scatters[i], after, "scatter_wait_%d" % i)
        rest += got[len(grp):]
        parts = dict(zip(grp, got))
        for batch in ([n for n in grp if n in ADAM_ALONE], [n for n in grp if n not in ADAM_ALONE]):
            if batch:
                outs = _adam([(parts[n], rows(w[n], n), rows(m[n], n), rows(v[n], n)) for n in batch], "adam_" + batch[0])
                for n, out in zip(batch, outs):
                    res[n] = [jnp.swapaxes(o, 0, 1) if n in BY_COLUMN else o for o in out]
                    after = out[0]
    ws_parts, vec_parts, cb_parts = rest
    small = lambda d, k: {**{n: d[n] for n in SMALL[:-1]}, "ffn_conv_b": conv_b[k]}
    res.update(_adam_small(vec_parts, ws_parts, cb_parts, small(w, "w"), small(m, "m"), small(v, "v")))

    outs = [loss, grad_x[None]]
    for j in range(4):
        outs += [res[n][j].reshape(w[n].shape) for n in names]
    return tuple(outs)
```

```python
import math

import jax
import jax.numpy as jnp
from jax import lax
from jax.experimental import pallas as pl
from jax.experimental.pallas import tpu as pltpu

F32 = jnp.float32
BF16 = jnp.bfloat16

D = 1024
HD = 64
AW = 512
KW = 128
GW = 512
IN = AW + 2 * KW + 2 * GW
BLK = 128
MEM = 256
XH = 4
XD = 256
FF = 2816
EPS = 1e-6
ROPE_THETA = 10000.0
NDEV = 8
LR, B1, B2, AEPS, WD, STEP = 0.001, 0.9, 0.999, 1e-08, 0.01, 10

TM = 512
WK = (2048, 4096)
WK_VMEM = 40 * 1024 * 1024
VMEM_LIMIT = 56 * 1024 * 1024
NEG = float(jnp.finfo(jnp.float32).min)
GELU_C0 = math.sqrt(2.0 / math.pi)
GELU_C1 = 0.044715


def _dot(a, b):
    return jnp.dot(a, b, preferred_element_type=F32)


def _dot_nt(a, b):
    return lax.dot_general(a, b, (((1,), (1,)), ((), ())), preferred_element_type=F32)


def _dot_tn(a, b):
    return lax.dot_general(a, b, (((0,), (0,)), ((), ())), preferred_element_type=F32)


def _rs(x):
    return lax.rsqrt(jnp.mean(x * x, axis=-1, keepdims=True) + EPS)


def _rms_bwd(dy, x, r, g):
    xh = x * r
    dxh = dy * g
    dx = r * (dxh - xh * jnp.mean(dxh * xh, axis=-1, keepdims=True))
    return dx, dy * xh


def _lane(shape):
    return lax.broadcasted_iota(jnp.int32, shape, len(shape) - 1)


def _gsum64(v, ones_ref):
    w = v.shape[-1]
    ones = ones_ref[0:w, 0:w]
    hi = v.astype(BF16)
    lo = (v - hi.astype(F32)).astype(BF16)
    return _dot(hi, ones) + _dot(lo, ones)


def _head_ones():
    i = jnp.arange(AW) // HD
    return (i[:, None] == i[None, :]).astype(BF16)


def _rs64(x, ones_ref):
    return lax.rsqrt(_gsum64(x * x, ones_ref) * (1.0 / HD) + EPS)


def _rms64_bwd(dy, x, r, g, ones_ref):
    xh = x * r
    dxh = dy * g
    dx = r * (dxh - xh * (_gsum64(dxh * xh, ones_ref) * (1.0 / HD)))
    return dx, dy * xh


def _rot_half(v):
    w = v.shape[-1]
    return jnp.where((_lane(v.shape) & 32) == 0, pltpu.roll(v, w - 32, 1), pltpu.roll(v, 32, 1))


def _rope(v, cos, sin_signed):
    return v * cos + _rot_half(v) * sin_signed


def _rope_bwd(dv, cos, sin_signed):
    return dv * cos + _rot_half(dv * sin_signed)


def _gelu(z):
    return 0.5 * z * (1.0 + jnp.tanh(GELU_C0 * (z + GELU_C1 * z * z * z)))


def _gelu_grad(z):
    t = jnp.tanh(GELU_C0 * (z + GELU_C1 * z * z * z))
    return 0.5 * (1.0 + t) + 0.5 * z * (1.0 - t * t) * (GELU_C0 * (1.0 + 3.0 * GELU_C1 * z * z))


def _colsum8(v):
    s = jnp.sum(v, axis=0, keepdims=True)
    row = lax.broadcasted_iota(jnp.int32, (8, v.shape[1]), 0)
    return jnp.where(row == 0, jnp.broadcast_to(s, (8, v.shape[1])), 0.0)


def _params(n_axes=1):
    return pltpu.CompilerParams(dimension_semantics=("arbitrary",) * n_axes, vmem_limit_bytes=VMEM_LIMIT)


def _rows(tm, w):
    return pl.BlockSpec((tm, w), lambda i: (i, 0))


def _const(shape):
    nd = len(shape)
    return pl.BlockSpec(shape, lambda *_: (0,) * nd)


def _sds(shape, dtype=F32):
    return jax.ShapeDtypeStruct(shape, dtype)


def _mm_tn(a, b, name):
    g = max(a.shape[0] if a.ndim == 3 else 1, b.shape[0] if b.ndim == 3 else 1)
    t, m = a.shape[-2:]
    n = b.shape[-1]

    def body(a_ref, b_ref, o_ref, acc_ref):
        i = pl.program_id(1)

        @pl.when(i == 0)
        def _():
            acc_ref[...] = jnp.zeros_like(acc_ref)

        acc_ref[...] += _dot_tn(a_ref[...].astype(BF16), b_ref[...].astype(BF16))

        @pl.when(i == pl.num_programs(1) - 1)
        def _():
            o_ref[...] = acc_ref[...].astype(BF16)

    def vmem(tk):
        return 2 * tk * (m * a.dtype.itemsize + n * b.dtype.itemsize) + m * n * (4 + 2 * 2)

    tk = min(t, max(k for k in WK if k == WK[0] or vmem(k) <= WK_VMEM))

    def spec(v):
        w = v.shape[-1]
        if v.ndim == 3:
            return pl.BlockSpec((None, tk, w), lambda j, i: (j, i, 0))
        return pl.BlockSpec((tk, w), lambda j, i: (i, 0))

    return pl.pallas_call(
        body, name=name, grid=(g, t // tk), in_specs=[spec(a), spec(b)],
        out_specs=pl.BlockSpec((None, m, n), lambda j, i: (j, 0, 0)), out_shape=_sds((g, m, n), BF16),
        scratch_shapes=[pltpu.VMEM((m, n), F32)], compiler_params=_params(2))(a, b)


def _rope_tables(pos, inv_freq):
    t = pos.shape[0]

    def body(pos_ref, f_ref, cos_ref, sin_ref):
        ang = pos_ref[...].astype(F32) * f_ref[...]
        sign = jnp.where((_lane(ang.shape) & 32) == 0, -1.0, 1.0)
        cos_ref[...] = jnp.cos(ang)
        sin_ref[...] = jnp.sin(ang) * sign

    return pl.pallas_call(
        body, name="rope_tables", grid=(t // TM,),
        in_specs=[_rows(TM, 1), _const((1, 128))], out_specs=[_rows(TM, 128), _rows(TM, 128)],
        out_shape=[_sds((t, 128)), _sds((t, 128))], compiler_params=_params())(pos, inv_freq)


def _mixer_in_fwd(x, mix_norm, w_in, qn, kn, gvw, cos, sin):
    t = x.shape[0]

    def body(x_ref, g_ref, w_ref, qn_ref, kn_ref, gvw_ref, cos_ref, sin_ref, ones_ref,
             h_ref, qk_ref, gz_ref, q_ref, k_ref, v_ref, gvn_ref):
        x = x_ref[...]
        h = (x * _rs(x) * g_ref[...]).astype(BF16)
        h_ref[...] = h
        proj = _dot_nt(h, w_ref[...])
        qk = proj[:, :AW + KW]
        qk_ref[...] = qk
        gz = proj[:, AW + 2 * KW:]
        gz_ref[...] = gz
        cos2, sin2 = cos_ref[...], sin_ref[...]
        q = qk[:, :AW]
        q = q * _rs64(q, ones_ref) * qn_ref[...]
        q_ref[...] = _rope(q, jnp.tile(cos2, (1, 4)), jnp.tile(sin2, (1, 4))).astype(BF16)
        k = qk[:, AW:]
        k = k * _rs64(k, ones_ref) * kn_ref[...]
        k_ref[...] = _rope(k, cos2, sin2).astype(BF16)
        v_ref[...] = proj[:, AW + KW:AW + 2 * KW].astype(BF16)
        gv = _gelu(gz[:, GW:])
        gvn_ref[...] = (gv * _rs(gv) * gvw_ref[...]).astype(BF16)

    return pl.pallas_call(
        body, name="mixer_in_fwd", grid=(t // TM,),
        in_specs=[_rows(TM, D), _const((1, D)), _const((IN, D)), _const((1, AW)), _const((1, KW)),
                  _const((1, GW)), _rows(TM, 128), _rows(TM, 128), _const((AW, AW))],
        out_specs=[_rows(TM, D), _rows(TM, AW + KW), _rows(TM, 2 * GW), _rows(TM, AW), _rows(TM, KW),
                   _rows(TM, KW), _rows(TM, GW)],
        out_shape=[_sds((t, D), BF16), _sds((t, AW + KW)), _sds((t, 2 * GW)), _sds((t, AW), BF16),
                   _sds((t, KW), BF16), _sds((t, KW), BF16), _sds((t, GW), BF16)],
        compiler_params=_params())(x, mix_norm, w_in, qn, kn, gvw, cos, sin, _head_ones())


def _dup_half(kk, g):
    lane = _lane(kk.shape)
    other = pltpu.roll(kk, 64, 1)
    keep = (lane < 64) if g == 0 else (lane >= 64)
    return jnp.where(keep, kk, other).astype(BF16)


def _swa_mask(first_block):
    qi = lax.broadcasted_iota(jnp.int32, (4 * BLK, 2 * BLK), 0) & (BLK - 1)
    kj = lax.broadcasted_iota(jnp.int32, (4 * BLK, 2 * BLK), 1)
    diff = qi + BLK - kj
    band = (diff >= 0) & (diff < BLK)
    return band & (jnp.logical_not(first_block) | (kj >= BLK))


def _stack_heads(a2, b2):
    lo = _lane(a2.shape) < 64
    z = jnp.zeros_like(a2)
    return jnp.concatenate([jnp.where(lo, a2, z), jnp.where(lo, z, a2), jnp.where(lo, b2, z), jnp.where(lo, z, b2)], axis=0)


def _unstack_heads(o):
    lo = _lane((BLK, 128)) < 64
    return jnp.where(lo, o[0:BLK], o[BLK:2 * BLK]), jnp.where(lo, o[2 * BLK:3 * BLK], o[3 * BLK:4 * BLK])


def _sink_col(sink_ref, g):
    row = lax.broadcasted_iota(jnp.int32, (4 * BLK, 1), 0)
    s = [sink_ref[0, 4 * g + j] for j in range(4)]
    return jnp.where(row < BLK, s[0], jnp.where(row < 2 * BLK, s[1], jnp.where(row < 3 * BLK, s[2], s[3])))


def _swa_probs(qs, kd, mask, sink):
    s = _dot_nt(qs, kd) * (1.0 / math.sqrt(HD))
    s = jnp.where(mask, s, NEG)
    m = jnp.maximum(jnp.max(s, axis=-1, keepdims=True), sink)
    p = jnp.exp(s - m)
    ps = jnp.exp(sink - m)
    inv = 1.0 / (jnp.sum(p, axis=-1, keepdims=True) + ps)
    return p * inv, ps * inv


SB = 8
SB_BWD = 2


def _swa_fwd(q, k, v, sinks):
    t = q.shape[0]
    ts = min(t, SB * BLK)

    def body(sink_ref, q_ref, kc_ref, kp_ref, vc_ref, vp_ref, o_ref):
        i = pl.program_id(0)
        kk = jnp.concatenate([kp_ref[...], kc_ref[...]], axis=0).astype(F32)
        vv = jnp.concatenate([vp_ref[...], vc_ref[...]], axis=0).astype(F32)
        for b in range(ts // BLK):
            r = slice(b * BLK, (b + 1) * BLK)
            kb, vb = kk[b * BLK:(b + 2) * BLK], vv[b * BLK:(b + 2) * BLK]
            mask = _swa_mask(i == 0) if b == 0 else _swa_mask(False)
            for g in range(2):
                qs = _stack_heads(q_ref[r, 256 * g:256 * g + 128], q_ref[r, 256 * g + 128:256 * g + 256])
                pn, _ = _swa_probs(qs, _dup_half(kb, g), mask, _sink_col(sink_ref, g))
                oa, ob = _unstack_heads(_dot(pn.astype(BF16), _dup_half(vb, g)))
                o_ref[r, 256 * g:256 * g + 128] = oa
                o_ref[r, 256 * g + 128:256 * g + 256] = ob

    cur = lambda i: (i, 0)
    prev = lambda i: (jnp.maximum(i * (ts // BLK) - 1, 0), 0)
    return pl.pallas_call(
        body, name="swa_fwd", grid=(t // ts,),
        in_specs=[pl.BlockSpec(memory_space=pltpu.SMEM), pl.BlockSpec((ts, AW), cur),
                  pl.BlockSpec((ts, KW), cur), pl.BlockSpec((BLK, KW), prev),
                  pl.BlockSpec((ts, KW), cur), pl.BlockSpec((BLK, KW), prev)],
        out_specs=pl.BlockSpec((ts, AW), cur), out_shape=_sds((t, AW)),
        compiler_params=_params())(sinks, q, k, k, v, v)


def _causal_bf16(w_ref, h, transposed):
    r = lax.broadcasted_iota(jnp.int32, (BLK, BLK), 0)
    c = lax.broadcasted_iota(jnp.int32, (BLK, BLK), 1)
    keep = (r <= c) if transposed else (c <= r)
    return jnp.where(keep, w_ref[h], 0.0).astype(BF16)


def _gmlp_mix(w_ref, xin, transposed):
    lo = _lane((BLK, 128)) < 64
    wm = [_causal_bf16(w_ref, h, transposed) for h in range(8)]
    rows = []
    for c in range(xin.shape[0] // BLK):
        cols = []
        for j in range(4):
            xs = xin[c * BLK:(c + 1) * BLK, 128 * j:128 * (j + 1)]
            cols.append(jnp.where(lo, _dot(wm[2 * j], xs), _dot(wm[2 * j + 1], xs)))
        rows.append(jnp.concatenate(cols, axis=1))
    return jnp.concatenate(rows, axis=0)


def _mixer_out_fwd(attn, gvn, gz, ws, bfull, x, w_out, aon, gon, xan):
    t = x.shape[0]

    def body(a_ref, v_ref, gzu_ref, ws_ref, b_ref, x_ref, w_ref, aon_ref, gon_ref, xan_ref, gm_ref, y_ref, x1_ref, h2_ref):
        a = a_ref[...]
        g = _gelu(gzu_ref[...]) * (_gmlp_mix(ws_ref, v_ref[...], False) + jnp.tile(b_ref[...], (TM // BLK, 1)))
        gm_ref[...] = g
        y = jnp.concatenate([a * _rs(a) * aon_ref[...], g * _rs(g) * gon_ref[...]], axis=1).astype(BF16)
        y_ref[...] = y
        x1 = x_ref[...] + _dot(y, w_ref[...])
        x1_ref[...] = x1
        h2_ref[...] = (x1 * _rs(x1) * xan_ref[...]).astype(BF16)

    return pl.pallas_call(
        body, name="mixer_out_fwd", grid=(t // TM,),
        in_specs=[_rows(TM, AW), _rows(TM, GW), _rows(TM, GW), _const((8, BLK, BLK)), _const((BLK, GW)), _rows(TM, D),
                  _const((D, D)), _const((1, AW)), _const((1, GW)), _const((1, D))],
        out_specs=[_rows(TM, GW), _rows(TM, D), _rows(TM, D), _rows(TM, D)],
        out_shape=[_sds((t, GW)), _sds((t, D), BF16), _sds((t, D)), _sds((t, D), BF16)],
        compiler_params=_params())(attn, gvn, gz, ws, bfull, x, w_out, aon, gon, xan)


def _mem_kv_fwd(mem, mem_norm, wkv, kn4):
    def body(m_ref, g_ref, w_ref, kn_ref, mh_ref, kpre_ref, k_ref, v_ref):
        m = m_ref[...]
        mh = (m * _rs(m) * g_ref[...]).astype(BF16)
        mh_ref[...] = mh
        for h in range(XH):
            sl = slice(XD * h, XD * (h + 1))
            kh = _dot(mh, w_ref[h])
            kpre_ref[:, sl] = kh
            k_ref[:, sl] = (kh * _rs(kh) * kn_ref[:, sl]).astype(BF16)
            v_ref[:, sl] = _dot(mh, w_ref[XH + h]).astype(BF16)

    return pl.pallas_call(
        body, name="mem_kv_fwd",
        out_shape=[_sds((MEM, D), BF16), _sds((MEM, D)), _sds((MEM, D), BF16), _sds((MEM, D), BF16)],
        compiler_params=pltpu.CompilerParams(vmem_limit_bytes=VMEM_LIMIT))(mem, mem_norm, wkv, kn4)


def _xattn_probs(qpre_h, qn_h, k_h):
    rq = _rs(qpre_h)
    q2 = (qpre_h * rq * qn_h).astype(BF16)
    s = _dot_nt(q2, k_h) * (1.0 / math.sqrt(XD))
    p = jnp.exp(s - jnp.max(s, axis=-1, keepdims=True))
    return p * (1.0 / jnp.sum(p, axis=-1, keepdims=True)), q2, rq


def _xattn_fwd(h2, x1, wq, qn4, k2, v2, wo, ffn_norm):
    t = x1.shape[0]

    def body(h_ref, x_ref, wq_ref, qn_ref, k_ref, v_ref, wo_ref, fn_ref, qpre_ref, o_ref, x2_ref, h3_ref):
        qpre = _dot(h_ref[...], wq_ref[...])
        qpre_ref[...] = qpre
        outs = []
        for h in range(XH):
            sl = slice(XD * h, XD * (h + 1))
            pn, _, _ = _xattn_probs(qpre[:, sl], qn_ref[:, sl], k_ref[:, sl])
            outs.append(_dot(pn.astype(BF16), v_ref[:, sl]))
        o = jnp.concatenate(outs, axis=1).astype(BF16)
        o_ref[...] = o
        x2 = x_ref[...] + _dot(o, wo_ref[...])
        x2_ref[...] = x2
        h3_ref[...] = (x2 * _rs(x2) * fn_ref[...]).astype(BF16)

    return pl.pallas_call(
        body, name="xattn_fwd", grid=(t // TM,),
        in_specs=[_rows(TM, D), _rows(TM, D), _const((D, D)), _const((1, D)), _const((MEM, D)), _const((MEM, D)),
                  _const((D, D)), _const((1, D))],
        out_specs=[_rows(TM, D)] * 4,
        out_shape=[_sds((t, D)), _sds((t, D), BF16), _sds((t, D)), _sds((t, D), BF16)],
        compiler_params=_params())(h2, x1, wq, qn4, k2, v2, wo, ffn_norm)


SW = 704
NG = FF // SW
FM = 256


def _resident(shape):
    nd = len(shape)
    return pl.BlockSpec(shape, lambda *_: (0,) * nd, pipeline_mode=pl.Buffered(1))


def _conv(e, w):
    return w[2:3, :] * e + pltpu.roll(w[1:2, :] * e + pltpu.roll(w[0:1, :] * e, 1, 0), 1, 0)


def _ffn_fwd(h3, x2, target, up, conv, conv_b, down):
    t = x2.shape[0]

    def body(h_ref, x_ref, t_ref, up_ref, w_ref, b_ref, dn_ref, a_ref, u_ref, gs_ref, dy_ref, loss_ref, acc_ref, tail_ref):
        i = pl.program_id(0)

        @pl.when(i == 0)
        def _():
            acc_ref[...] = jnp.zeros_like(acc_ref)
            tail_ref[...] = jnp.zeros_like(tail_ref)

        h = h_ref[...]
        err = x_ref[...] - t_ref[...]
        for d in range(NG):
            c = []
            for s in range(2):
                j = s * NG + d
                a = _dot_nt(h, up_ref[j])
                a_ref[j] = a.astype(BF16)
                c.append(_conv(jnp.concatenate([tail_ref[j], a], axis=0), w_ref[s, d])[8:] + b_ref[s, d])
                tail_ref[j] = a[FM - 8:FM]
            gl, gg = _gelu_and_grad(c[0])
            gs_ref[d] = gl.astype(BF16)
            gs_ref[NG + d] = (gg * c[1]).astype(BF16)
            u = (gl * c[1]).astype(BF16)
            u_ref[d] = u
            err = err + _dot(u, dn_ref[d])
        dy_ref[...] = err * (1.0 / D)
        acc_ref[...] += jnp.sum(err * err, axis=0, keepdims=True)

        @pl.when(i == pl.num_programs(0) - 1)
        def _():
            loss_ref[...] = jnp.full((8, 128), 0.5 / D, F32) * jnp.sum(acc_ref[...])

    return pl.pallas_call(
        body, name="ffn_fwd", grid=(t // FM,),
        in_specs=[_rows(FM, D), _rows(FM, D), _rows(FM, D),
                  _resident((NDEV, SW, D)), _resident((2, NG, 3, SW)), _resident((2, NG, 1, SW)), _resident((NG, SW, D))],
        out_specs=[pl.BlockSpec((NDEV, FM, SW), lambda i: (0, i, 0)), pl.BlockSpec((NG, FM, SW), lambda i: (0, i, 0)),
                   pl.BlockSpec((NDEV, FM, SW), lambda i: (0, i, 0)), _rows(FM, D), _const((8, 128))],
        out_shape=[_sds((NDEV, t, SW), BF16), _sds((NG, t, SW), BF16), _sds((NDEV, t, SW), BF16), _sds((t, D)),
                   _sds((8, 128))],
        scratch_shapes=[pltpu.VMEM((1, D), F32), pltpu.VMEM((NDEV, 8, SW), F32)],
        compiler_params=_params())(h3, x2, target, up, conv, conv_b, down)


def _gelu_and_grad(z):
    z2 = z * z
    t = jnp.tanh(GELU_C0 * (z + GELU_C1 * z * z2))
    phi = 0.5 * (1.0 + t)
    return z * phi, phi + z * (1.0 - t * t) * (0.5 * GELU_C0 + (1.5 * GELU_C0 * GELU_C1) * z2)


def _ffn_bwd(dy, a, gs, x2, up, conv, down, ffn_norm):
    t = x2.shape[0]
    nt = t // FM
    n = FM + 8

    def body(dy_ref, a_ref, gs_ref, x_ref, up_ref, w_ref, dn_ref, g_ref, dx_ref, da_ref, s_ref, dfn_ref, head_ref):
        @pl.when(pl.program_id(0) == 0)
        def _():
            s_ref[...] = jnp.zeros_like(s_ref)
            dfn_ref[...] = jnp.zeros_like(dfn_ref)
            head_ref[...] = jnp.zeros_like(head_ref)

        dy = dy_ref[...]
        dyb = dy.astype(BF16)
        dh = jnp.zeros((FM, D), F32)
        row = lax.broadcasted_iota(jnp.int32, (8, SW), 0)
        for d in range(NG):
            du = _dot_nt(dyb, dn_ref[d])
            for s in range(2):
                j = s * NG + d
                k = NG + d if s == 0 else d
                dc0 = du * gs_ref[k].astype(F32)
                dc = jnp.concatenate([dc0, head_ref[j]], axis=0)
                head_ref[j] = dc0[0:8]
                w = w_ref[s, d]
                tile = a_ref[j].astype(F32)
                d1 = pltpu.roll(dc, n - 1, 0)
                d2 = pltpu.roll(d1, n - 1, 0)
                da = (w[2:3, :] * dc + w[1:2, :] * d1 + w[0:1, :] * d2)[0:FM].astype(BF16)
                da_ref[j] = da
                dh = dh + _dot(da, up_ref[j])
                sums = [jnp.sum(v[0:FM] * tile, axis=0, keepdims=True) for v in (d2, d1, dc)]
                sums.append(jnp.sum(dc[0:FM], axis=0, keepdims=True))
                upd = jnp.zeros((8, SW), F32)
                for r, v in enumerate(sums):
                    upd = jnp.where(row == r, jnp.broadcast_to(v, (8, SW)), upd)
                s_ref[s, d] += upd
        x = x_ref[...]
        dx, dg = _rms_bwd(dh, x, _rs(x), g_ref[...])
        dx_ref[...] = dy + dx
        dfn_ref[...] += _colsum8(dg)

    rows = pl.BlockSpec((FM, D), lambda i: (nt - 1 - i, 0))
    blocks = pl.BlockSpec((NDEV, FM, SW), lambda i: (0, nt - 1 - i, 0))
    return pl.pallas_call(
        body, name="ffn_bwd", grid=(nt,),
        in_specs=[rows, blocks, blocks, rows, _resident((NDEV, SW, D)), _resident((2, NG, 3, SW)), _resident((NG, SW, D)),
                  _const((1, D))],
        out_specs=[rows, blocks, _const((2, NG, 8, SW)), _const((8, D))],
        out_shape=[_sds((t, D)), _sds((NDEV, t, SW), BF16), _sds((2, NG, 8, SW)), _sds((8, D))],
        scratch_shapes=[pltpu.VMEM((NDEV, 8, SW), F32)],
        compiler_params=_params())(dy, a, gs, x2, up, conv, down, ffn_norm)


BT = 512


def _xattn_bwd(dx2, x1, qpre, k2, v2, wq, wo, qn4, xan):
    t = x1.shape[0]

    def body(dx2_ref, x1_ref, qpre_ref, k_ref, v_ref, wq_ref, wo_ref, qn_ref, xan_ref,
             dx1_ref, dqpre_ref, dk_ref, dv_ref, dqn_ref, dxan_ref):
        @pl.when(pl.program_id(0) == 0)
        def _():
            for r in (dk_ref, dv_ref, dqn_ref, dxan_ref):
                r[...] = jnp.zeros_like(r)

        dx2 = dx2_ref[...]
        do = _dot_nt(dx2.astype(BF16), wo_ref[...])
        dqs = []
        for h in range(XH):
            sl = slice(XD * h, XD * (h + 1))
            qpre_h = qpre_ref[:, sl]
            pn, q2, rq = _xattn_probs(qpre_h, qn_ref[:, sl], k_ref[:, sl])
            do_h = do[:, sl].astype(BF16)
            dp = _dot_nt(do_h, v_ref[:, sl])
            ds = (pn * (dp - jnp.sum(pn * dp, axis=-1, keepdims=True)) * (1.0 / math.sqrt(XD))).astype(BF16)
            dq2 = _dot(ds, k_ref[:, sl])
            dk_ref[:, sl] += _dot_tn(ds, q2)
            dv_ref[:, sl] += _dot_tn(pn.astype(BF16), do_h)
            dqh, dg = _rms_bwd(dq2, qpre_h, rq, qn_ref[:, sl])
            dqn_ref[...] += _colsum8(dg)
            dqs.append(dqh)
        dqpre = jnp.concatenate(dqs, axis=1).astype(BF16)
        dqpre_ref[...] = dqpre
        dh2 = _dot_nt(dqpre, wq_ref[...])
        x1 = x1_ref[...]
        dx, dg = _rms_bwd(dh2, x1, _rs(x1), xan_ref[...])
        dx1_ref[...] = dx2 + dx
        dxan_ref[...] += _colsum8(dg)

    return pl.pallas_call(
        body, name="xattn_bwd", grid=(t // BT,),
        in_specs=[_rows(BT, D), _rows(BT, D), _rows(BT, D), _const((MEM, D)), _const((MEM, D)), _const((D, D)),
                  _const((D, D)), _const((1, D)), _const((1, D))],
        out_specs=[_rows(BT, D), _rows(BT, D), _const((MEM, D)), _const((MEM, D)), _const((8, XD)), _const((8, D))],
        out_shape=[_sds((t, D)), _sds((t, D), BF16), _sds((MEM, D)), _sds((MEM, D)), _sds((8, XD)), _sds((8, D))],
        compiler_params=_params())(dx2, x1, qpre, k2, v2, wq, wo, qn4, xan)


def _mem_kv_bwd(mem, mh, kpre, dk2, dv2, wkv, kn4, mem_norm):
    def body(m_ref, mh_ref, kpre_ref, dk_ref, dv_ref, w_ref, kn_ref, g_ref, dw_ref, dkn_ref, dmn_ref):
        dkn = jnp.zeros((8, XD), F32)
        dm = jnp.zeros((MEM, D), F32)
        mh = mh_ref[...]
        for h in range(XH):
            sl = slice(XD * h, XD * (h + 1))
            kh = kpre_ref[:, sl]
            dkh, dg = _rms_bwd(dk_ref[:, sl], kh, _rs(kh), kn_ref[:, sl])
            dkn = dkn + _colsum8(dg)
            dkh = dkh.astype(BF16)
            dvh = dv_ref[:, sl].astype(BF16)
            dw_ref[h] = _dot_tn(mh, dkh).astype(BF16)
            dw_ref[XH + h] = _dot_tn(mh, dvh).astype(BF16)
            dm = dm + _dot_nt(dkh, w_ref[h]) + _dot_nt(dvh, w_ref[XH + h])
        dkn_ref[...] = dkn
        m = m_ref[...]
        _, dg = _rms_bwd(dm, m, _rs(m), g_ref[...])
        dmn_ref[...] = _colsum8(dg)

    return pl.pallas_call(
        body, name="mem_kv_bwd", out_shape=[_sds((2 * XH, D, XD), BF16), _sds((8, XD)), _sds((8, D))],
        compiler_params=pltpu.CompilerParams(vmem_limit_bytes=VMEM_LIMIT))(mem, mh, kpre, dk2, dv2, wkv, kn4, mem_norm)


def _mixer_out_bwd(dx1, attn, gm, w_out, aon, gon, gvn, gz, ws, wst, bfull, gvw):
    t = dx1.shape[0]
    nc = TM // BLK

    def body(dx_ref, a_ref, g_ref, wo_ref, aon_ref, gon_ref, x_ref, gz_ref, w_ref, wt_ref, b_ref, gvw_ref,
             da_ref, dan_ref, dgn_ref, dgz_ref, dw_ref, db_ref, dgvw_ref, dbacc_ref):
        @pl.when(pl.program_id(0) == 0)
        def _():
            for r in (dan_ref, dgn_ref, dw_ref, dbacc_ref, dgvw_ref):
                r[...] = jnp.zeros_like(r)

        dy = _dot_nt(dx_ref[...].astype(BF16), wo_ref[...])
        a, g = a_ref[...], g_ref[...]
        da, dna = _rms_bwd(dy[:, :AW], a, _rs(a), aon_ref[...])
        dgm, dng = _rms_bwd(dy[:, AW:], g, _rs(g), gon_ref[...])
        da_ref[...] = da
        dan_ref[...] += _colsum8(dna)
        dgn_ref[...] += _colsum8(dng)

        xin = x_ref[...]
        mixed = _gmlp_mix(w_ref, xin, False) + jnp.tile(b_ref[...], (nc, 1))
        dgu = dgm * mixed
        dmixed = dgm * _gelu(gz_ref[:, :GW])
        lo = _lane((BLK, 128)) < 64
        dbias = jnp.zeros((BLK, GW), F32)
        for c in range(nc):
            dmc = dmixed[c * BLK:(c + 1) * BLK]
            dbias = dbias + dmc
            for j in range(4):
                dm2 = dmc[:, 128 * j:128 * (j + 1)]
                xs = xin[c * BLK:(c + 1) * BLK, 128 * j:128 * (j + 1)]
                z = jnp.zeros_like(dm2)
                dw_ref[2 * j] += _dot_nt(jnp.where(lo, dm2, z).astype(BF16), xs)
                dw_ref[2 * j + 1] += _dot_nt(jnp.where(lo, z, dm2).astype(BF16), xs)
        dbacc_ref[...] += dbias
        dgvn = _gmlp_mix(wt_ref, dmixed.astype(BF16), True)
        gz_u, gz_v = gz_ref[:, :GW], gz_ref[:, GW:]
        gv = _gelu(gz_v)
        dgv, dg = _rms_bwd(dgvn, gv, _rs(gv), gvw_ref[...])
        dgvw_ref[...] += _colsum8(dg)
        dgz_ref[:, :GW] = (dgu * _gelu_grad(gz_u)).astype(BF16)
        dgz_ref[:, GW:] = (dgv * _gelu_grad(gz_v)).astype(BF16)

        @pl.when(pl.program_id(0) == pl.num_programs(0) - 1)
        def _():
            s = dbacc_ref[...]
            sel = (lax.broadcasted_iota(jnp.int32, (8, GW), 1) // HD
                   == lax.broadcasted_iota(jnp.int32, (8, GW), 0)).astype(BF16)
            hi = s.astype(BF16)
            r1 = s - hi.astype(F32)
            mid = r1.astype(BF16)
            lo = (r1 - mid.astype(F32)).astype(BF16)
            db_ref[...] = _dot_nt(sel, hi) + _dot_nt(sel, mid) + _dot_nt(sel, lo)
            r = lax.broadcasted_iota(jnp.int32, (BLK, BLK), 0)
            c = lax.broadcasted_iota(jnp.int32, (BLK, BLK), 1)
            for h in range(8):
                dw_ref[h] = jnp.where(c <= r, dw_ref[h], 0.0)

    return pl.pallas_call(
        body, name="mixer_out_bwd", grid=(t // TM,),
        in_specs=[_rows(TM, D), _rows(TM, AW), _rows(TM, GW), _const((D, D)), _const((1, AW)), _const((1, GW)),
                  _rows(TM, GW), _rows(TM, 2 * GW), _const((8, BLK, BLK)), _const((8, BLK, BLK)),
                  _const((BLK, GW)), _const((1, GW))],
        out_specs=[_rows(TM, AW), _const((8, AW)), _const((8, GW)), _rows(TM, 2 * GW), _const((8, BLK, BLK)),
                   _const((8, BLK)), _const((8, GW))],
        out_shape=[_sds((t, AW)), _sds((8, AW)), _sds((8, GW)), _sds((t, 2 * GW), BF16), _sds((8, BLK, BLK)),
                   _sds((8, BLK)), _sds((8, GW))],
        scratch_shapes=[pltpu.VMEM((BLK, GW), F32)],
        compiler_params=_params())(dx1, attn, gm, w_out, aon, gon, gvn, gz, ws, wst, bfull, gvw)


def _fold_half(v):
    return v + pltpu.roll(v, 64, 1)


def _swa_bwd(q, k, v, dattn, sinks):
    t = q.shape[0]
    nb = t // BLK
    ts = min(t, SB_BWD * BLK)
    sb = ts // BLK
    nt = t // ts

    def body(sink_ref, q_ref, kc_ref, kp_ref, vc_ref, vp_ref, do_ref, dq_ref, dk_ref, dv_ref, ds_ref,
             ck_ref, cv_ref, sacc_ref):
        i = pl.program_id(0)

        @pl.when(i == 0)
        def _():
            ck_ref[...] = jnp.zeros_like(ck_ref)
            cv_ref[...] = jnp.zeros_like(cv_ref)
            sacc_ref[...] = jnp.zeros_like(sacc_ref)

        @pl.when(i < nt)
        def _():
            kk = jnp.concatenate([kp_ref[...], kc_ref[...]], axis=0).astype(F32)
            vv = jnp.concatenate([vp_ref[...], vc_ref[...]], axis=0).astype(F32)
            lo256 = _lane((2 * BLK, 128)) < 64
            acc_k = [jnp.zeros((BLK, 128), F32) for _ in range(sb + 1)]
            acc_v = [jnp.zeros((BLK, 128), F32) for _ in range(sb + 1)]
            for b in range(sb):
                r = slice(b * BLK, (b + 1) * BLK)
                kb, vb = kk[b * BLK:(b + 2) * BLK], vv[b * BLK:(b + 2) * BLK]
                mask = _swa_mask(i == 0) if b == 0 else _swa_mask(False)
                dkk = jnp.zeros((2 * BLK, 128), F32)
                dvv = jnp.zeros((2 * BLK, 128), F32)
                for g in range(2):
                    qs = _stack_heads(q_ref[r, 256 * g:256 * g + 128], q_ref[r, 256 * g + 128:256 * g + 256])
                    dos = _stack_heads(do_ref[r, 256 * g:256 * g + 128],
                                       do_ref[r, 256 * g + 128:256 * g + 256]).astype(BF16)
                    kd = _dup_half(kb, g)
                    pn, psn = _swa_probs(qs, kd, mask, _sink_col(sink_ref, g))
                    dp = _dot_nt(dos, _dup_half(vb, g))
                    dd = jnp.sum(pn * dp, axis=-1, keepdims=True)
                    ds = (pn * (dp - dd) * (1.0 / math.sqrt(HD))).astype(BF16)
                    sacc_ref[g] += jnp.broadcast_to(-psn * dd, (4 * BLK, 128))
                    dqa, dqb = _unstack_heads(_dot(ds, kd))
                    dq_ref[r, 256 * g:256 * g + 128] = dqa
                    dq_ref[r, 256 * g + 128:256 * g + 256] = dqb
                    dkg = _fold_half(_dot_tn(ds, qs))
                    dvg = _fold_half(_dot_tn(pn.astype(BF16), dos))
                    keep = lo256 if g == 0 else jnp.logical_not(lo256)
                    dkk = jnp.where(keep, dkg, dkk)
                    dvv = jnp.where(keep, dvg, dvv)
                acc_k[b], acc_k[b + 1] = acc_k[b] + dkk[0:BLK], acc_k[b + 1] + dkk[BLK:]
                acc_v[b], acc_v[b + 1] = acc_v[b] + dvv[0:BLK], acc_v[b + 1] + dvv[BLK:]
            for out_ref, c_ref, acc in ((dk_ref, ck_ref, acc_k), (dv_ref, cv_ref, acc_v)):
                if sb > 1:
                    out_ref[0:ts - BLK] = c_ref[0:ts - BLK]
                out_ref[ts - BLK:ts] = c_ref[ts - BLK:ts] + acc[0]
                for b in range(sb):
                    c_ref[b * BLK:(b + 1) * BLK] = acc[b + 1]

        @pl.when(i == nt)
        def _():
            dk_ref[...] = ck_ref[...]
            dv_ref[...] = cv_ref[...]
            lane = _lane((8, 128))
            acc = jnp.zeros((8, 128), F32)
            for g in range(2):
                for j in range(4):
                    val = jnp.sum(sacc_ref[g, j * BLK:(j + 1) * BLK, :], axis=0, keepdims=True)
                    acc = jnp.where(lane == 4 * g + j, jnp.broadcast_to(val, (8, 128)), acc)
            ds_ref[...] = acc

    cur = lambda i: (jnp.minimum(i, nt - 1), 0)
    before = lambda i: (jnp.clip(i * sb - 1, 0, nb - 1), 0)
    done = lambda i: (jnp.clip(i - 1, 0, nt - 1), 0)
    return pl.pallas_call(
        body, name="swa_bwd", grid=(nt + 1,),
        in_specs=[pl.BlockSpec(memory_space=pltpu.SMEM), pl.BlockSpec((ts, AW), cur),
                  pl.BlockSpec((ts, KW), cur), pl.BlockSpec((BLK, KW), before),
                  pl.BlockSpec((ts, KW), cur), pl.BlockSpec((BLK, KW), before), pl.BlockSpec((ts, AW), cur)],
        out_specs=[pl.BlockSpec((ts, AW), cur), pl.BlockSpec((ts, KW), done), pl.BlockSpec((ts, KW), done),
                   _const((8, 128))],
        out_shape=[_sds((t, AW)), _sds((t, KW)), _sds((t, KW)), _sds((8, 128))],
        scratch_shapes=[pltpu.VMEM((ts, KW), F32), pltpu.VMEM((ts, KW), F32), pltpu.VMEM((2, 4 * BLK, 128), F32)],
        compiler_params=_params())(sinks, q, k, k, v, v, dattn)


def _mixer_in_bwd(dq, dk, dv, dgz, qk, cos, sin, x, dx1, w_in, mix_norm, qn, kn):
    t = x.shape[0]

    def body(dq_ref, dk_ref, dv_ref, dgz_ref, qk_ref, cos_ref, sin_ref, x_ref, dx1_ref, w_ref, g_ref, qn_ref, kn_ref,
             ones_ref, gx_ref, dproj_ref, dmn_ref, dqn_ref, dkn_ref, qacc_ref, kacc_ref):
        i = pl.program_id(0)

        @pl.when(i == 0)
        def _():
            dmn_ref[...] = jnp.zeros_like(dmn_ref)
            qacc_ref[...] = jnp.zeros_like(qacc_ref)
            kacc_ref[...] = jnp.zeros_like(kacc_ref)

        cos2, sin2 = cos_ref[...], sin_ref[...]
        qpre, kpre = qk_ref[:, :AW], qk_ref[:, AW:]
        dqh = _rope_bwd(dq_ref[...], jnp.tile(cos2, (1, 4)), jnp.tile(sin2, (1, 4)))
        dqpre, dgq = _rms64_bwd(dqh, qpre, _rs64(qpre, ones_ref), qn_ref[...], ones_ref)
        dkh = _rope_bwd(dk_ref[...], cos2, sin2)
        dkpre, dgk = _rms64_bwd(dkh, kpre, _rs64(kpre, ones_ref), kn_ref[...], ones_ref)
        qacc_ref[...] += jnp.sum(dgq, axis=0, keepdims=True)
        kacc_ref[...] += jnp.sum(dgk, axis=0, keepdims=True)
        dproj = jnp.concatenate([dqpre.astype(BF16), dkpre.astype(BF16), dv_ref[...].astype(BF16), dgz_ref[...]], axis=1)
        dproj_ref[...] = dproj
        dh = _dot(dproj, w_ref[...])
        xv = x_ref[...]
        dx, dg = _rms_bwd(dh, xv, _rs(xv), g_ref[...])
        gx_ref[...] = dx1_ref[...] + dx
        dmn_ref[...] += _colsum8(dg)

        @pl.when(i == pl.num_programs(0) - 1)
        def _():
            qa = qacc_ref[...]
            q4 = qa[:, 0:128] + qa[:, 128:256] + qa[:, 256:384] + qa[:, 384:512]
            dqn_ref[...] = jnp.broadcast_to(_fold_half(q4), (8, 128))
            dkn_ref[...] = jnp.broadcast_to(_fold_half(kacc_ref[...]), (8, 128))

    return pl.pallas_call(
        body, name="mixer_in_bwd", grid=(t // TM,),
        in_specs=[_rows(TM, AW), _rows(TM, KW), _rows(TM, KW), _rows(TM, 2 * GW), _rows(TM, AW + KW), _rows(TM, 128),
                  _rows(TM, 128), _rows(TM, D), _rows(TM, D), _const((IN, D)), _const((1, D)), _const((1, AW)),
                  _const((1, KW)), _const((AW, AW))],
        out_specs=[_rows(TM, D), _rows(TM, IN), _const((8, D)), _const((8, 128)), _const((8, 128))],
        out_shape=[_sds((t, D)), _sds((t, IN), BF16), _sds((8, D)), _sds((8, 128)), _sds((8, 128))],
        scratch_shapes=[pltpu.VMEM((1, AW), F32), pltpu.VMEM((1, KW), F32)],
        compiler_params=_params())(dq, dk, dv, dgz, qk, cos, sin, x, dx1, w_in, mix_norm, qn, kn, _head_ones())


def _local_step(x, mem, pos, target, p, fetch, ship):
    t = x.shape[0]
    p = dict(p)
    p.update(fetch(0, None))
    inv_freq = 1.0 / (ROPE_THETA ** (jnp.arange(HD // 2, dtype=F32) * (2.0 / HD)))
    cos, sin = _rope_tables(pos, jnp.tile(inv_freq, 4).reshape(1, 128))
    qn = jnp.tile(p["q_norm"], (1, AW // HD))
    kn = jnp.tile(p["k_norm"], (1, KW // HD))
    qn4 = jnp.tile(p["xa_q_norm"], (1, XH))
    kn4 = jnp.tile(p["xa_k_norm"], (1, XH))
    ws = p["gmlp_ws"]
    wst = jnp.swapaxes(ws, 1, 2)
    bfull = jnp.repeat(p["gmlp_bs"].T, HD, axis=1)
    conv_b = p["ffn_conv_b"]

    h1, qk, gz, q, k, v, gvn = _mixer_in_fwd(x, p["mix_norm"], p["w_in"], qn, kn, p["gmlp_v_norm"], cos, sin)
    attn = _swa_fwd(q, k, v, p["attn_sinks"])
    p.update(fetch(1, attn))
    gm, ycat, x1, h2 = _mixer_out_fwd(attn, gvn, gz, ws, bfull, x, p["w_out"], p["attn_out_norm"], p["gmlp_out_norm"],
                                      p["xa_norm"])
    mh, kpre, k2, v2 = _mem_kv_fwd(mem, p["mem_norm"], p["xa_wkv"], kn4)
    qpre, o, x2, h3 = _xattn_fwd(h2, x1, p["xa_wq"], qn4, k2, v2, p["xa_wo"], p["ffn_norm"])
    p.update(fetch(2, h3))
    conv = p["ffn_conv"]
    a, u, gs, dy, loss8 = _ffn_fwd(h3, x2, target, p["ffn_up"], conv, conv_b, p["ffn_down"])

    raw = {}
    d_down = _mm_tn(u, dy, "ffn_down_bwd_w")
    dx2, da, raw["conv_sums"], raw["ffn_norm"] = _ffn_bwd(dy, a, gs, x2, p["ffn_up"], conv, p["ffn_down"], p["ffn_norm"])
    d_up = _mm_tn(da, h3, "ffn_up_bwd_w")
    token = ship(0, {"ffn_down": d_down, "ffn_up": d_up, "ffn_conv": raw["conv_sums"][:, :, 0:3]})
    dx1, dqpre, dk2, dv2, raw["xa_q_norm"], raw["xa_norm"] = _xattn_bwd(
        dx2, x1, qpre, k2, v2, p["xa_wq"], p["xa_wo"], qn4 + jnp.tile(token[0:1], (1, D // 128)), p["xa_norm"])
    d_wo = _mm_tn(o, dx2, "xa_wo_bwd_w")
    d_wq = _mm_tn(h2, dqpre, "xa_wq_bwd_w")
    d_wkv, raw["xa_k_norm"], raw["mem_norm"] = _mem_kv_bwd(mem, mh, kpre, dk2, dv2, p["xa_wkv"], kn4, p["mem_norm"])
    d_w_out = _mm_tn(ycat, dx1, "w_out_bwd_w")
    (dattn, raw["attn_out_norm"], raw["gmlp_out_norm"], dgz, raw["gmlp_ws"], raw["gmlp_bs"],
     raw["gmlp_v_norm"]) = _mixer_out_bwd(dx1, attn, gm, p["w_out"], p["attn_out_norm"], p["gmlp_out_norm"],
                                          gvn, gz, ws, wst, bfull, p["gmlp_v_norm"])
    token = ship(1, {"xa_wo": d_wo, "xa_wq": d_wq, "xa_wkv": d_wkv, "w_out": d_w_out}, [raw["gmlp_ws"]])
    dq, dk, dv, raw["attn_sinks"] = _swa_bwd(q, k, v, dattn, p["attn_sinks"] + token[0:1, 0:8])
    grad_x, dproj, raw["mix_norm"], raw["q_norm"], raw["k_norm"] = _mixer_in_bwd(
        dq, dk, dv, dgz, qk, cos, sin, x, dx1, p["w_in"], p["mix_norm"], qn, kn)
    d_w_in = _mm_tn(dproj, h1, "w_in_bwd_w")
    raw["loss"] = loss8
    return grad_x, {"w_in": d_w_in}, raw


def _cast_shards(shards):
    def body(*refs):
        n = len(refs) // 2
        for i_ref, o_ref in zip(refs[:n], refs[n:]):
            o_ref[...] = i_ref[...].astype(BF16)

    return pl.pallas_call(body, name="cast_shards", out_shape=[_sds(s.shape, BF16) for s in shards],
                          compiler_params=pltpu.CompilerParams(vmem_limit_bytes=VMEM_LIMIT))(*shards)


HBM_SPEC = pl.BlockSpec(memory_space=pltpu.HBM)
SEM_SPEC = pl.BlockSpec(memory_space=pltpu.SEMAPHORE)


ALL_K = tuple(range(1, NDEV))
CHIP_K = (1, 2, 4, 6)
RELAY_K = (2, 4, 6)


def _peer(k):
    x, y, cc = lax.axis_index("x"), lax.axis_index("y"), lax.axis_index("c")
    return 1 - x if k & 4 else x, 1 - y if k & 2 else y, 1 - cc if k & 1 else cc


def _remote_copies(src_refs, land_refs, send_refs, recv_refs, nd, ks):
    me = 4 * lax.axis_index("x") + 2 * lax.axis_index("y") + lax.axis_index("c")
    copies = []
    for a, (src_ref, land_ref) in enumerate(zip(src_refs, land_refs)):
        for j, k in enumerate(ks):
            px, py, pc = _peer(k)
            copies.append((k, pltpu.make_async_remote_copy(
                src_ref=src_ref.at[4 * px + 2 * py + pc] if a < nd else src_ref, dst_ref=land_ref.at[me],
                send_sem=send_refs[a].at[j], recv_sem=recv_refs[a].at[j],
                device_id=(px, py, pc), device_id_type=pl.DeviceIdType.MESH)))
    return copies


def _relay_copies(land_refs, send_refs, recv_refs):
    copies = []
    for a, land_ref in enumerate(land_refs):
        for j, k in enumerate(RELAY_K):
            px, py, pc = _peer(k)
            slot = land_ref.at[4 * px + 2 * py + pc]
            copies.append(pltpu.make_async_remote_copy(
                src_ref=slot, dst_ref=slot, send_sem=send_refs[a].at[j], recv_sem=recv_refs[a].at[j],
                device_id=_peer(1), device_id_type=pl.DeviceIdType.MESH))
    return copies


def _own_slot(src, by_dest, me):
    block = lax.dynamic_index_in_dim(src, me, 0, keepdims=True) if by_dest else src[None]
    return lax.dynamic_update_index_in_dim(lax.empty((NDEV,) + block.shape[1:], src.dtype), block, me, 0)


SIDE_EFFECT = pltpu.CompilerParams(has_side_effects=pltpu.SideEffectType.DATAFLOW_SIDE_EFFECTING)


def _exchange_start(by_dest, for_all, me, name, ks=ALL_K):
    srcs = list(by_dest) + list(for_all)
    n, nd = len(srcs), len(by_dest)
    lands = [_own_slot(s, a < nd, me) for a, s in enumerate(srcs)]

    def body(*refs):
        for _, cp in _remote_copies(refs[:n], refs[n:2 * n], refs[2 * n:3 * n], refs[3 * n:4 * n], nd, ks):
            cp.start()
        refs[-1][...] = jnp.zeros((8, 128), F32)

    sems = [pltpu.SemaphoreType.DMA((len(ks),))] * (2 * n)
    thru = [pltpu.HBM(v.shape, v.dtype) for v in srcs + lands]
    res = pl.pallas_call(
        body, name=name, out_shape=sems + thru + [_sds((8, 128))],
        in_specs=[HBM_SPEC] * (2 * n), out_specs=[SEM_SPEC] * (2 * n) + [HBM_SPEC] * (2 * n) + [pl.BlockSpec(memory_space=pltpu.VMEM)],
        input_output_aliases={i: 2 * n + i for i in range(2 * n)}, compiler_params=SIDE_EFFECT)(
            *[pltpu.with_memory_space_constraint(v, pltpu.HBM) for v in srcs + lands])
    return (res[:2 * n], res[2 * n:4 * n], nd, ks, None), res[-1]


def _exchange_relay(state, after, name):
    sems, thru, nd, ks, _ = state
    n = len(thru) // 2

    def body(*refs):
        for k, cp in _remote_copies(refs[:n], refs[n:2 * n], refs[2 * n:3 * n], refs[3 * n:4 * n], nd, ks):
            if k in RELAY_K:
                cp.wait_recv()
        for cp in _relay_copies(refs[n:2 * n], refs[4 * n + 1:5 * n + 1], refs[5 * n + 1:6 * n + 1]):
            cp.start()

    relay_sems = [pltpu.SemaphoreType.DMA((len(RELAY_K),))] * (2 * n)
    res = pl.pallas_call(
        body, name=name, out_shape=relay_sems + [pltpu.HBM(v.shape, v.dtype) for v in thru],
        in_specs=[HBM_SPEC] * (2 * n) + [SEM_SPEC] * (2 * n) + [pl.BlockSpec(memory_space=pl.ANY)],
        out_specs=[SEM_SPEC] * (2 * n) + [HBM_SPEC] * (2 * n),
        input_output_aliases={i: 2 * n + i for i in range(2 * n)}, compiler_params=SIDE_EFFECT)(*thru, *sems, after)
    return sems, res[2 * n:], nd, ks, res[:2 * n]


def _exchange_wait(state, after, name):
    sems, thru, nd, ks, relay_sems = state
    n = len(thru) // 2

    def body(*refs):
        for k, cp in _remote_copies(refs[:n], refs[n:2 * n], refs[2 * n:3 * n], refs[3 * n:4 * n], nd, ks):
            cp.wait_send()
            if relay_sems is None or k not in RELAY_K:
                cp.wait_recv()
        if relay_sems is not None:
            for cp in _relay_copies(refs[n:2 * n], refs[4 * n:5 * n], refs[5 * n:6 * n]):
                cp.wait_send()
                cp.wait_recv()

    extra = [] if relay_sems is None else list(relay_sems)
    res = pl.pallas_call(
        body, name=name, out_shape=[pltpu.HBM(v.shape, v.dtype) for v in thru],
        in_specs=[HBM_SPEC] * (2 * n) + [SEM_SPEC] * (2 * n + len(extra)) + [pl.BlockSpec(memory_space=pl.ANY)],
        out_specs=[HBM_SPEC] * (2 * n), input_output_aliases={i: i for i in range(2 * n)}, compiler_params=SIDE_EFFECT)(
            *thru, *sems, *extra, after)
    return res[n:]


def _adam(items, name):
    n = len(items)

    def body(*refs):
        for j in range(n):
            p_ref, w_ref, m_ref, v_ref = refs[4 * j:4 * j + 4]
            g_ref, d_ref, nm_ref, nv_ref = refs[4 * n + 4 * j:4 * n + 4 * j + 4]
            g = _sum_parts(p_ref)
            g_ref[...] = g
            d_ref[...], nm_ref[...], nv_ref[...] = _adam_math(g, w_ref[...], m_ref[...], v_ref[...])

    res = pl.pallas_call(
        body, name=name, out_shape=[_sds(it[1].shape) for it in items for _ in range(4)],
        compiler_params=pltpu.CompilerParams(vmem_limit_bytes=VMEM_LIMIT))(*[a for it in items for a in it])
    return [res[4 * j:4 * j + 4] for j in range(n)]


GATHER_GROUPS = (("w_in",), ("w_out", "xa_wkv", "xa_wq", "xa_wo"), ("ffn_up", "ffn_conv", "ffn_down"))
SCATTER_GROUPS = (("ffn_up", "ffn_down", "ffn_conv"), ("xa_wo", "xa_wq", "xa_wkv", "w_out"), ("w_in",))
ADAM_ALONE = ("ffn_up",)
BIG = tuple(n for grp in GATHER_GROUPS for n in grp)
BY_COLUMN = ("w_in", "ffn_up")
VECS = (("mix_norm", D), ("q_norm", HD), ("k_norm", HD), ("attn_sinks", 8), ("gmlp_v_norm", GW), ("attn_out_norm", AW),
        ("gmlp_out_norm", GW), ("xa_norm", D), ("mem_norm", D), ("xa_q_norm", XD), ("xa_k_norm", XD), ("ffn_norm", D))
BS_ROW = 16
VEC_ROWS = 24
SMALL = tuple(n for n, _ in VECS) + ("gmlp_bs", "gmlp_ws", "ffn_conv_b")


def _pack_small(raw):
    names = [n for n, _ in VECS] + ["gmlp_bs", "conv_sums"]

    def body(*refs):
        ins = dict(zip(names, refs))
        vec_ref, cb_ref = refs[len(names):]
        vec_ref[...] = jnp.zeros_like(vec_ref)
        for r, (n, w) in enumerate(VECS):
            vec_ref[r:r + 1, 0:w] = ins[n][0:1, 0:w]
        vec_ref[BS_ROW:BS_ROW + 8, 0:BLK] = ins["gmlp_bs"][...]
        for s in range(2):
            for d in range(NG):
                cb_ref[s, d] = ins["conv_sums"][s, d, 3:4, :]

    return pl.pallas_call(body, name="pack_small", out_shape=[_sds((VEC_ROWS, D)), _sds((2, NG, 1, SW))])(
        *[raw[n] for n in names])


def _adam_math(g, w, m, v):
    nm = B1 * m + (1.0 - B1) * g
    nv = B2 * v + (1.0 - B2) * (g * g)
    m_hat = nm / (1.0 - B1 ** STEP)
    v_hat = nv / (1.0 - B2 ** STEP)
    return -LR * (m_hat / (jnp.sqrt(v_hat) + AEPS) + WD * w), nm, nv


def _sum_parts(p_ref):
    g = p_ref[0].astype(F32)
    for j in range(1, NDEV):
        g = g + p_ref[j].astype(F32)
    return g


def _adam_small(parts_vec, parts_ws, parts_cb, w, m, v):
    def body(*refs):
        pv_ref, pws_ref, pcb_ref = refs[:3]
        ins = refs[3:3 + 3 * len(SMALL)]
        outs = refs[3 + 3 * len(SMALL):]
        gv = _sum_parts(pv_ref)
        for j, n in enumerate(SMALL):
            w_ref, m_ref, v_ref = ins[3 * j:3 * j + 3]
            o = outs[4 * j:4 * j + 4]
            if n == "gmlp_ws":
                g = _sum_parts(pws_ref)
            elif n == "ffn_conv_b":
                g = _sum_parts(pcb_ref)
            elif n == "gmlp_bs":
                g = gv[BS_ROW:BS_ROW + 8, 0:BLK]
            else:
                g = gv[j:j + 1, 0:VECS[j][1]]
            lead = n in ("gmlp_ws", "gmlp_bs")
            res = (g,) + _adam_math(g, w_ref[0] if lead else w_ref[...], m_ref[0] if lead else m_ref[...],
                                    v_ref[0] if lead else v_ref[...])
            for o_ref, val in zip(o, res):
                if lead:
                    o_ref[0] = val
                else:
                    o_ref[...] = val

    args = [parts_vec, parts_ws, parts_cb] + [d[n] for n in SMALL for d in (w, m, v)]
    res = pl.pallas_call(body, name="adam_small", out_shape=[_sds(w[n].shape) for n in SMALL for _ in range(4)],
                         compiler_params=pltpu.CompilerParams(vmem_limit_bytes=VMEM_LIMIT))(*args)
    return {n: tuple(res[4 * j:4 * j + 4]) for j, n in enumerate(SMALL)}


def kernel(x, mem, positions, mix_norm, w_in, q_norm, k_norm, attn_sinks, gmlp_v_norm, gmlp_ws, gmlp_bs, attn_out_norm, gmlp_out_norm, w_out, xa_norm, mem_norm, xa_wq, xa_wkv, xa_q_norm, xa_k_norm, xa_wo, ffn_norm, ffn_up, ffn_conv, ffn_conv_b, ffn_down, loss_target, m_mix_norm, m_w_in, m_q_norm, m_k_norm, m_attn_sinks, m_gmlp_v_norm, m_gmlp_ws, m_gmlp_bs, m_attn_out_norm, m_gmlp_out_norm, m_w_out, m_xa_norm, m_mem_norm, m_xa_wq, m_xa_wkv, m_xa_q_norm, m_xa_k_norm, m_xa_wo, m_ffn_norm, m_ffn_up, m_ffn_conv, m_ffn_conv_b, m_ffn_down, v_mix_norm, v_w_in, v_q_norm, v_k_norm, v_attn_sinks, v_gmlp_v_norm, v_gmlp_ws, v_gmlp_bs, v_attn_out_norm, v_gmlp_out_norm, v_w_out, v_xa_norm, v_mem_norm, v_xa_wq, v_xa_wkv, v_xa_q_norm, v_xa_k_norm, v_xa_wo, v_ffn_norm, v_ffn_up, v_ffn_conv, v_ffn_conv_b, v_ffn_down):
    names = ("mix_norm", "w_in", "q_norm", "k_norm", "attn_sinks", "gmlp_v_norm", "gmlp_ws", "gmlp_bs", "attn_out_norm",
             "gmlp_out_norm", "w_out", "xa_norm", "mem_norm", "xa_wq", "xa_wkv", "xa_q_norm", "xa_k_norm", "xa_wo",
             "ffn_norm", "ffn_up", "ffn_conv", "ffn_conv_b", "ffn_down")
    w = dict(zip(names, (mix_norm, w_in, q_norm, k_norm, attn_sinks, gmlp_v_norm, gmlp_ws, gmlp_bs, attn_out_norm,
                         gmlp_out_norm, w_out, xa_norm, mem_norm, xa_wq, xa_wkv, xa_q_norm, xa_k_norm, xa_wo, ffn_norm,
                         ffn_up, ffn_conv, ffn_conv_b, ffn_down)))
    m = dict(zip(names, (m_mix_norm, m_w_in, m_q_norm, m_k_norm, m_attn_sinks, m_gmlp_v_norm, m_gmlp_ws, m_gmlp_bs,
                         m_attn_out_norm, m_gmlp_out_norm, m_w_out, m_xa_norm, m_mem_norm, m_xa_wq, m_xa_wkv,
                         m_xa_q_norm, m_xa_k_norm, m_xa_wo, m_ffn_norm, m_ffn_up, m_ffn_conv, m_ffn_conv_b, m_ffn_down)))
    v = dict(zip(names, (v_mix_norm, v_w_in, v_q_norm, v_k_norm, v_attn_sinks, v_gmlp_v_norm, v_gmlp_ws, v_gmlp_bs,
                         v_attn_out_norm, v_gmlp_out_norm, v_w_out, v_xa_norm, v_mem_norm, v_xa_wq, v_xa_wkv,
                         v_xa_q_norm, v_xa_k_norm, v_xa_wo, v_ffn_norm, v_ffn_up, v_ffn_conv, v_ffn_conv_b, v_ffn_down)))
    t = x.shape[1]

    me = 4 * lax.axis_index("x") + 2 * lax.axis_index("y") + lax.axis_index("c")

    def rows(a, n):
        return jnp.swapaxes(a[0], 0, 1) if n in BY_COLUMN else a[0]

    mats = [n for n in BIG if n != "ffn_conv"]
    shard = dict(zip(mats, _cast_shards([rows(w[n], n) for n in mats])), ffn_conv=w["ffn_conv"][0])
    gathers, tokens = zip(*[_exchange_start([], [shard[n] for n in grp], me, "gather_start_%d" % i,
                                            CHIP_K if i == len(GATHER_GROUPS) - 1 else ALL_K)
                            for i, grp in enumerate(GATHER_GROUPS)])

    def fetch(i, after):
        after = tokens[0] + tokens[1] + tokens[2] if after is None else after
        state = gathers[i]
        if state[3] == CHIP_K:
            state = _exchange_relay(state, after, "gather_relay_%d" % i)
        got = dict(zip(GATHER_GROUPS[i], _exchange_wait(state, after, "gather_wait_%d" % i)))
        if "w_in" in got:
            got["w_in"] = got["w_in"].reshape(IN, D)
        for n in ("w_out", "xa_wq", "xa_wo"):
            if n in got:
                got[n] = got[n].reshape(D, D)
        if "ffn_down" in got:
            got["ffn_down"] = got["ffn_down"].reshape(NG, SW, D)
            got["ffn_conv"] = got["ffn_conv"].reshape(2, NG, 3, SW)
        return got

    scatters = []

    def ship(i, grads, for_all=()):
        by_dest = [grads[n].reshape((NDEV,) + shard[n].shape) for n in SCATTER_GROUPS[i]]
        state, token = _exchange_start(by_dest, for_all, me, "scatter_start_%d" % i)
        scatters.append(state)
        return token

    conv_b = {k: d["ffn_conv_b"].reshape(NDEV, 1, SW) for k, d in (("w", w), ("m", m), ("v", v))}
    p = {n: w[n] for n in SMALL[:-1]}
    p["gmlp_ws"], p["gmlp_bs"] = w["gmlp_ws"][0], w["gmlp_bs"][0]
    p["ffn_conv_b"] = conv_b["w"].reshape(2, NG, 1, SW)
    grad_x, g, raw = _local_step(x[0], mem[0], positions.reshape(t, 1), loss_target[0], p, fetch, ship)
    loss = lax.psum(raw["loss"][0, 0], ("x", "y", "c"))

    vec, cb = _pack_small(raw)
    after = ship(2, g, [vec, cb.reshape(NDEV, 1, SW)])
    res, rest = {}, []
    for i, grp in enumerate(SCATTER_GROUPS):
        got = _exchange_wait(scatters[i], after, "scatter_wait_%d" % i)
        rest += got[len(grp):]
        parts = dict(zip(grp, got))
        for batch in ([n for n in grp if n in ADAM_ALONE], [n for n in grp if n not in ADAM_ALONE]):
            if batch:
                outs = _adam([(parts[n], rows(w[n], n), rows(m[n], n), rows(v[n], n)) for n in batch], "adam_" + batch[0])
                for n, out in zip(batch, outs):
                    res[n] = [jnp.swapaxes(o, 0, 1) if n in BY_COLUMN else o for o in out]
                    after = out[0]
    ws_parts, vec_parts, cb_parts = rest
    small = lambda d, k: {**{n: d[n] for n in SMALL[:-1]}, "ffn_conv_b": conv_b[k]}
    res.update(_adam_small(vec_parts, ws_parts, cb_parts, small(w, "w"), small(m, "m"), small(v, "v")))

    outs = [loss, grad_x[None]]
    for j in range(4):
        outs += [res[n][j].reshape(w[n].shape) for n in names]
    return tuple(outs)
```

```python
import math

import jax
import jax.numpy as jnp
from jax import lax
from jax.experimental import pallas as pl
from jax.experimental.pallas import tpu as pltpu

F32 = jnp.float32
BF16 = jnp.bfloat16

D = 1024
HD = 64
AW = 512
KW = 128
GW = 512
IN = AW + 2 * KW + 2 * GW
BLK = 128
MEM = 256
XH = 4
XD = 256
FF = 2816
EPS = 1e-6
ROPE_THETA = 10000.0
NDEV = 8
LR, B1, B2, AEPS, WD, STEP = 0.001, 0.9, 0.999, 1e-08, 0.01, 10

TM = 512
WK = (2048, 4096)
WK_VMEM = 40 * 1024 * 1024
VMEM_LIMIT = 56 * 1024 * 1024
NEG = float(jnp.finfo(jnp.float32).min)
GELU_C0 = math.sqrt(2.0 / math.pi)
GELU_C1 = 0.044715


def _dot(a, b):
    return jnp.dot(a, b, preferred_element_type=F32)


def _dot_nt(a, b):
    return lax.dot_general(a, b, (((1,), (1,)), ((), ())), preferred_element_type=F32)


def _dot_tn(a, b):
    return lax.dot_general(a, b, (((0,), (0,)), ((), ())), preferred_element_type=F32)


def _rs(x):
    return lax.rsqrt(jnp.mean(x * x, axis=-1, keepdims=True) + EPS)


def _rms_bwd(dy, x, r, g):
    xh = x * r
    dxh = dy * g
    dx = r * (dxh - xh * jnp.mean(dxh * xh, axis=-1, keepdims=True))
    return dx, dy * xh


def _lane(shape):
    return lax.broadcasted_iota(jnp.int32, shape, len(shape) - 1)


def _gsum64(v, ones_ref):
    w = v.shape[-1]
    ones = ones_ref[0:w, 0:w]
    hi = v.astype(BF16)
    lo = (v - hi.astype(F32)).astype(BF16)
    return _dot(hi, ones) + _dot(lo, ones)


def _head_ones():
    i = jnp.arange(AW) // HD
    return (i[:, None] == i[None, :]).astype(BF16)


def _rs64(x, ones_ref):
    return lax.rsqrt(_gsum64(x * x, ones_ref) * (1.0 / HD) + EPS)


def _rms64_bwd(dy, x, r, g, ones_ref):
    xh = x * r
    dxh = dy * g
    dx = r * (dxh - xh * (_gsum64(dxh * xh, ones_ref) * (1.0 / HD)))
    return dx, dy * xh


def _rot_half(v):
    w = v.shape[-1]
    return jnp.where((_lane(v.shape) & 32) == 0, pltpu.roll(v, w - 32, 1), pltpu.roll(v, 32, 1))


def _rope(v, cos, sin_signed):
    return v * cos + _rot_half(v) * sin_signed


def _rope_bwd(dv, cos, sin_signed):
    return dv * cos + _rot_half(dv * sin_signed)


def _gelu(z):
    return 0.5 * z * (1.0 + jnp.tanh(GELU_C0 * (z + GELU_C1 * z * z * z)))


def _gelu_grad(z):
    t = jnp.tanh(GELU_C0 * (z + GELU_C1 * z * z * z))
    return 0.5 * (1.0 + t) + 0.5 * z * (1.0 - t * t) * (GELU_C0 * (1.0 + 3.0 * GELU_C1 * z * z))


def _colsum8(v):
    s = jnp.sum(v, axis=0, keepdims=True)
    row = lax.broadcasted_iota(jnp.int32, (8, v.shape[1]), 0)
    return jnp.where(row == 0, jnp.broadcast_to(s, (8, v.shape[1])), 0.0)


def _params(n_axes=1):
    return pltpu.CompilerParams(dimension_semantics=("arbitrary",) * n_axes, vmem_limit_bytes=VMEM_LIMIT)


def _rows(tm, w):
    return pl.BlockSpec((tm, w), lambda i: (i, 0))


def _const(shape):
    nd = len(shape)
    return pl.BlockSpec(shape, lambda *_: (0,) * nd)


def _sds(shape, dtype=F32):
    return jax.ShapeDtypeStruct(shape, dtype)


def _mm_tn(a, b, name):
    g = max(a.shape[0] if a.ndim == 3 else 1, b.shape[0] if b.ndim == 3 else 1)
    t, m = a.shape[-2:]
    n = b.shape[-1]

    def body(a_ref, b_ref, o_ref, acc_ref):
        i = pl.program_id(1)

        @pl.when(i == 0)
        def _():
            acc_ref[...] = jnp.zeros_like(acc_ref)

        acc_ref[...] += _dot_tn(a_ref[...].astype(BF16), b_ref[...].astype(BF16))

        @pl.when(i == pl.num_programs(1) - 1)
        def _():
            o_ref[...] = acc_ref[...].astype(BF16)

    def vmem(tk):
        return 2 * tk * (m * a.dtype.itemsize + n * b.dtype.itemsize) + m * n * (4 + 2 * 2)

    tk = min(t, max(k for k in WK if k == WK[0] or vmem(k) <= WK_VMEM))

    def spec(v):
        w = v.shape[-1]
        if v.ndim == 3:
            return pl.BlockSpec((None, tk, w), lambda j, i: (j, i, 0))
        return pl.BlockSpec((tk, w), lambda j, i: (i, 0))

    return pl.pallas_call(
        body, name=name, grid=(g, t // tk), in_specs=[spec(a), spec(b)],
        out_specs=pl.BlockSpec((None, m, n), lambda j, i: (j, 0, 0)), out_shape=_sds((g, m, n), BF16),
        scratch_shapes=[pltpu.VMEM((m, n), F32)], compiler_params=_params(2))(a, b)


def _rope_tables(pos, inv_freq):
    t = pos.shape[0]

    def body(pos_ref, f_ref, cos_ref, sin_ref):
        ang = pos_ref[...].astype(F32) * f_ref[...]
        sign = jnp.where((_lane(ang.shape) & 32) == 0, -1.0, 1.0)
        cos_ref[...] = jnp.cos(ang)
        sin_ref[...] = jnp.sin(ang) * sign

    return pl.pallas_call(
        body, name="rope_tables", grid=(t // TM,),
        in_specs=[_rows(TM, 1), _const((1, 128))], out_specs=[_rows(TM, 128), _rows(TM, 128)],
        out_shape=[_sds((t, 128)), _sds((t, 128))], compiler_params=_params())(pos, inv_freq)


def _mixer_in_fwd(x, mix_norm, w_in, qn, kn, gvw, cos, sin):
    t = x.shape[0]

    def body(x_ref, g_ref, w_ref, qn_ref, kn_ref, gvw_ref, cos_ref, sin_ref, ones_ref,
             h_ref, qk_ref, gz_ref, q_ref, k_ref, v_ref, gvn_ref):
        x = x_ref[...]
        h = (x * _rs(x) * g_ref[...]).astype(BF16)
        h_ref[...] = h
        proj = _dot_nt(h, w_ref[...])
        qk = proj[:, :AW + KW]
        qk_ref[...] = qk
        gz = proj[:, AW + 2 * KW:]
        gz_ref[...] = gz
        cos2, sin2 = cos_ref[...], sin_ref[...]
        q = qk[:, :AW]
        q = q * _rs64(q, ones_ref) * qn_ref[...]
        q_ref[...] = _rope(q, jnp.tile(cos2, (1, 4)), jnp.tile(sin2, (1, 4))).astype(BF16)
        k = qk[:, AW:]
        k = k * _rs64(k, ones_ref) * kn_ref[...]
        k_ref[...] = _rope(k, cos2, sin2).astype(BF16)
        v_ref[...] = proj[:, AW + KW:AW + 2 * KW].astype(BF16)
        gv = _gelu(gz[:, GW:])
        gvn_ref[...] = (gv * _rs(gv) * gvw_ref[...]).astype(BF16)

    return pl.pallas_call(
        body, name="mixer_in_fwd", grid=(t // TM,),
        in_specs=[_rows(TM, D), _const((1, D)), _const((IN, D)), _const((1, AW)), _const((1, KW)),
                  _const((1, GW)), _rows(TM, 128), _rows(TM, 128), _const((AW, AW))],
        out_specs=[_rows(TM, D), _rows(TM, AW + KW), _rows(TM, 2 * GW), _rows(TM, AW), _rows(TM, KW),
                   _rows(TM, KW), _rows(TM, GW)],
        out_shape=[_sds((t, D), BF16), _sds((t, AW + KW)), _sds((t, 2 * GW)), _sds((t, AW), BF16),
                   _sds((t, KW), BF16), _sds((t, KW), BF16), _sds((t, GW), BF16)],
        compiler_params=_params())(x, mix_norm, w_in, qn, kn, gvw, cos, sin, _head_ones())


def _dup_half(kk, g):
    lane = _lane(kk.shape)
    other = pltpu.roll(kk, 64, 1)
    keep = (lane < 64) if g == 0 else (lane >= 64)
    return jnp.where(keep, kk, other).astype(BF16)


def _swa_mask(first_block):
    qi = lax.broadcasted_iota(jnp.int32, (4 * BLK, 2 * BLK), 0) & (BLK - 1)
    kj = lax.broadcasted_iota(jnp.int32, (4 * BLK, 2 * BLK), 1)
    diff = qi + BLK - kj
    band = (diff >= 0) & (diff < BLK)
    return band & (jnp.logical_not(first_block) | (kj >= BLK))


def _stack_heads(a2, b2):
    lo = _lane(a2.shape) < 64
    z = jnp.zeros_like(a2)
    return jnp.concatenate([jnp.where(lo, a2, z), jnp.where(lo, z, a2), jnp.where(lo, b2, z), jnp.where(lo, z, b2)], axis=0)


def _unstack_heads(o):
    lo = _lane((BLK, 128)) < 64
    return jnp.where(lo, o[0:BLK], o[BLK:2 * BLK]), jnp.where(lo, o[2 * BLK:3 * BLK], o[3 * BLK:4 * BLK])


def _sink_col(sink_ref, g):
    row = lax.broadcasted_iota(jnp.int32, (4 * BLK, 1), 0)
    s = [sink_ref[0, 4 * g + j] for j in range(4)]
    return jnp.where(row < BLK, s[0], jnp.where(row < 2 * BLK, s[1], jnp.where(row < 3 * BLK, s[2], s[3])))


def _swa_probs(qs, kd, mask, sink):
    s = _dot_nt(qs, kd) * (1.0 / math.sqrt(HD))
    s = jnp.where(mask, s, NEG)
    m = jnp.maximum(jnp.max(s, axis=-1, keepdims=True), sink)
    p = jnp.exp(s - m)
    ps = jnp.exp(sink - m)
    inv = 1.0 / (jnp.sum(p, axis=-1, keepdims=True) + ps)
    return p * inv, ps * inv


SB = 8
SB_BWD = 2


def _swa_fwd(q, k, v, sinks):
    t = q.shape[0]
    ts = min(t, SB * BLK)

    def body(sink_ref, q_ref, kc_ref, kp_ref, vc_ref, vp_ref, o_ref):
        i = pl.program_id(0)
        kk = jnp.concatenate([kp_ref[...], kc_ref[...]], axis=0).astype(F32)
        vv = jnp.concatenate([vp_ref[...], vc_ref[...]], axis=0).astype(F32)
        for b in range(ts // BLK):
            r = slice(b * BLK, (b + 1) * BLK)
            kb, vb = kk[b * BLK:(b + 2) * BLK], vv[b * BLK:(b + 2) * BLK]
            mask = _swa_mask(i == 0) if b == 0 else _swa_mask(False)
            for g in range(2):
                qs = _stack_heads(q_ref[r, 256 * g:256 * g + 128], q_ref[r, 256 * g + 128:256 * g + 256])
                pn, _ = _swa_probs(qs, _dup_half(kb, g), mask, _sink_col(sink_ref, g))
                oa, ob = _unstack_heads(_dot(pn.astype(BF16), _dup_half(vb, g)))
                o_ref[r, 256 * g:256 * g + 128] = oa
                o_ref[r, 256 * g + 128:256 * g + 256] = ob

    cur = lambda i: (i, 0)
    prev = lambda i: (jnp.maximum(i * (ts // BLK) - 1, 0), 0)
    return pl.pallas_call(
        body, name="swa_fwd", grid=(t // ts,),
        in_specs=[pl.BlockSpec(memory_space=pltpu.SMEM), pl.BlockSpec((ts, AW), cur),
                  pl.BlockSpec((ts, KW), cur), pl.BlockSpec((BLK, KW), prev),
                  pl.BlockSpec((ts, KW), cur), pl.BlockSpec((BLK, KW), prev)],
        out_specs=pl.BlockSpec((ts, AW), cur), out_shape=_sds((t, AW)),
        compiler_params=_params())(sinks, q, k, k, v, v)


def _causal_bf16(w_ref, h, transposed):
    r = lax.broadcasted_iota(jnp.int32, (BLK, BLK), 0)
    c = lax.broadcasted_iota(jnp.int32, (BLK, BLK), 1)
    keep = (r <= c) if transposed else (c <= r)
    return jnp.where(keep, w_ref[h], 0.0).astype(BF16)


def _gmlp_mix(w_ref, xin, transposed):
    lo = _lane((BLK, 128)) < 64
    wm = [_causal_bf16(w_ref, h, transposed) for h in range(8)]
    rows = []
    for c in range(xin.shape[0] // BLK):
        cols = []
        for j in range(4):
            xs = xin[c * BLK:(c + 1) * BLK, 128 * j:128 * (j + 1)]
            cols.append(jnp.where(lo, _dot(wm[2 * j], xs), _dot(wm[2 * j + 1], xs)))
        rows.append(jnp.concatenate(cols, axis=1))
    return jnp.concatenate(rows, axis=0)


def _mixer_out_fwd(attn, gvn, gz, ws, bfull, x, w_out, aon, gon, xan):
    t = x.shape[0]

    def body(a_ref, v_ref, gzu_ref, ws_ref, b_ref, x_ref, w_ref, aon_ref, gon_ref, xan_ref, gm_ref, y_ref, x1_ref, h2_ref):
        a = a_ref[...]
        g = _gelu(gzu_ref[...]) * (_gmlp_mix(ws_ref, v_ref[...], False) + jnp.tile(b_ref[...], (TM // BLK, 1)))
        gm_ref[...] = g
        y = jnp.concatenate([a * _rs(a) * aon_ref[...], g * _rs(g) * gon_ref[...]], axis=1).astype(BF16)
        y_ref[...] = y
        x1 = x_ref[...] + _dot(y, w_ref[...])
        x1_ref[...] = x1
        h2_ref[...] = (x1 * _rs(x1) * xan_ref[...]).astype(BF16)

    return pl.pallas_call(
        body, name="mixer_out_fwd", grid=(t // TM,),
        in_specs=[_rows(TM, AW), _rows(TM, GW), _rows(TM, GW), _const((8, BLK, BLK)), _const((BLK, GW)), _rows(TM, D),
                  _const((D, D)), _const((1, AW)), _const((1, GW)), _const((1, D))],
        out_specs=[_rows(TM, GW), _rows(TM, D), _rows(TM, D), _rows(TM, D)],
        out_shape=[_sds((t, GW)), _sds((t, D), BF16), _sds((t, D)), _sds((t, D), BF16)],
        compiler_params=_params())(attn, gvn, gz, ws, bfull, x, w_out, aon, gon, xan)


def _mem_kv_fwd(mem, mem_norm, wkv, kn4):
    def body(m_ref, g_ref, w_ref, kn_ref, mh_ref, kpre_ref, k_ref, v_ref):
        m = m_ref[...]
        mh = (m * _rs(m) * g_ref[...]).astype(BF16)
        mh_ref[...] = mh
        for h in range(XH):
            sl = slice(XD * h, XD * (h + 1))
            kh = _dot(mh, w_ref[h])
            kpre_ref[:, sl] = kh
            k_ref[:, sl] = (kh * _rs(kh) * kn_ref[:, sl]).astype(BF16)
            v_ref[:, sl] = _dot(mh, w_ref[XH + h]).astype(BF16)

    return pl.pallas_call(
        body, name="mem_kv_fwd",
        out_shape=[_sds((MEM, D), BF16), _sds((MEM, D)), _sds((MEM, D), BF16), _sds((MEM, D), BF16)],
        compiler_params=pltpu.CompilerParams(vmem_limit_bytes=VMEM_LIMIT))(mem, mem_norm, wkv, kn4)


def _xattn_probs(qpre_h, qn_h, k_h):
    rq = _rs(qpre_h)
    q2 = (qpre_h * rq * qn_h).astype(BF16)
    s = _dot_nt(q2, k_h) * (1.0 / math.sqrt(XD))
    p = jnp.exp(s - jnp.max(s, axis=-1, keepdims=True))
    return p * (1.0 / jnp.sum(p, axis=-1, keepdims=True)), q2, rq


def _xattn_fwd(h2, x1, wq, qn4, k2, v2, wo, ffn_norm):
    t = x1.shape[0]

    def body(h_ref, x_ref, wq_ref, qn_ref, k_ref, v_ref, wo_ref, fn_ref, qpre_ref, o_ref, x2_ref, h3_ref):
        qpre = _dot(h_ref[...], wq_ref[...])
        qpre_ref[...] = qpre
        outs = []
        for h in range(XH):
            sl = slice(XD * h, XD * (h + 1))
            pn, _, _ = _xattn_probs(qpre[:, sl], qn_ref[:, sl], k_ref[:, sl])
            outs.append(_dot(pn.astype(BF16), v_ref[:, sl]))
        o = jnp.concatenate(outs, axis=1).astype(BF16)
        o_ref[...] = o
        x2 = x_ref[...] + _dot(o, wo_ref[...])
        x2_ref[...] = x2
        h3_ref[...] = (x2 * _rs(x2) * fn_ref[...]).astype(BF16)

    return pl.pallas_call(
        body, name="xattn_fwd", grid=(t // TM,),
        in_specs=[_rows(TM, D), _rows(TM, D), _const((D, D)), _const((1, D)), _const((MEM, D)), _const((MEM, D)),
                  _const((D, D)), _const((1, D))],
        out_specs=[_rows(TM, D)] * 4,
        out_shape=[_sds((t, D)), _sds((t, D), BF16), _sds((t, D)), _sds((t, D), BF16)],
        compiler_params=_params())(h2, x1, wq, qn4, k2, v2, wo, ffn_norm)


SW = 704
NG = FF // SW
FM = 256


def _resident(shape):
    nd = len(shape)
    return pl.BlockSpec(shape, lambda *_: (0,) * nd, pipeline_mode=pl.Buffered(1))


def _conv(e, w):
    return w[2:3, :] * e + pltpu.roll(w[1:2, :] * e + pltpu.roll(w[0:1, :] * e, 1, 0), 1, 0)


def _ffn_fwd(h3, x2, target, up, conv, conv_b, down):
    t = x2.shape[0]

    def body(h_ref, x_ref, t_ref, up_ref, w_ref, b_ref, dn_ref, a_ref, u_ref, gs_ref, dy_ref, loss_ref, acc_ref, tail_ref):
        i = pl.program_id(0)

        @pl.when(i == 0)
        def _():
            acc_ref[...] = jnp.zeros_like(acc_ref)
            tail_ref[...] = jnp.zeros_like(tail_ref)

        h = h_ref[...]
        err = x_ref[...] - t_ref[...]
        for d in range(NG):
            c = []
            for s in range(2):
                j = s * NG + d
                a = _dot_nt(h, up_ref[j])
                a_ref[j] = a.astype(BF16)
                c.append(_conv(jnp.concatenate([tail_ref[j], a], axis=0), w_ref[s, d])[8:] + b_ref[s, d])
                tail_ref[j] = a[FM - 8:FM]
            gl, gg = _gelu_and_grad(c[0])
            gs_ref[d] = gl.astype(BF16)
            gs_ref[NG + d] = (gg * c[1]).astype(BF16)
            u = (gl * c[1]).astype(BF16)
            u_ref[d] = u
            err = err + _dot(u, dn_ref[d])
        dy_ref[...] = err * (1.0 / D)
        acc_ref[...] += jnp.sum(err * err, axis=0, keepdims=True)

        @pl.when(i == pl.num_programs(0) - 1)
        def _():
            loss_ref[...] = jnp.full((8, 128), 0.5 / D, F32) * jnp.sum(acc_ref[...])

    return pl.pallas_call(
        body, name="ffn_fwd", grid=(t // FM,),
        in_specs=[_rows(FM, D), _rows(FM, D), _rows(FM, D),
                  _resident((NDEV, SW, D)), _resident((2, NG, 3, SW)), _resident((2, NG, 1, SW)), _resident((NG, SW, D))],
        out_specs=[pl.BlockSpec((NDEV, FM, SW), lambda i: (0, i, 0)), pl.BlockSpec((NG, FM, SW), lambda i: (0, i, 0)),
                   pl.BlockSpec((NDEV, FM, SW), lambda i: (0, i, 0)), _rows(FM, D), _const((8, 128))],
        out_shape=[_sds((NDEV, t, SW), BF16), _sds((NG, t, SW), BF16), _sds((NDEV, t, SW), BF16), _sds((t, D)),
                   _sds((8, 128))],
        scratch_shapes=[pltpu.VMEM((1, D), F32), pltpu.VMEM((NDEV, 8, SW), F32)],
        compiler_params=_params())(h3, x2, target, up, conv, conv_b, down)


def _gelu_and_grad(z):
    z2 = z * z
    t = jnp.tanh(GELU_C0 * (z + GELU_C1 * z * z2))
    phi = 0.5 * (1.0 + t)
    return z * phi, phi + z * (1.0 - t * t) * (0.5 * GELU_C0 + (1.5 * GELU_C0 * GELU_C1) * z2)


def _ffn_bwd(dy, a, gs, x2, up, conv, down, ffn_norm):
    t = x2.shape[0]
    nt = t // FM
    n = FM + 8

    def body(dy_ref, a_ref, gs_ref, x_ref, up_ref, w_ref, dn_ref, g_ref, dx_ref, da_ref, s_ref, dfn_ref, head_ref):
        @pl.when(pl.program_id(0) == 0)
        def _():
            s_ref[...] = jnp.zeros_like(s_ref)
            dfn_ref[...] = jnp.zeros_like(dfn_ref)
            head_ref[...] = jnp.zeros_like(head_ref)

        dy = dy_ref[...]
        dyb = dy.astype(BF16)
        dh = jnp.zeros((FM, D), F32)
        row = lax.broadcasted_iota(jnp.int32, (8, SW), 0)
        for d in range(NG):
            du = _dot_nt(dyb, dn_ref[d])
            for s in range(2):
                j = s * NG + d
                k = NG + d if s == 0 else d
                dc0 = du * gs_ref[k].astype(F32)
                dc = jnp.concatenate([dc0, head_ref[j]], axis=0)
                head_ref[j] = dc0[0:8]
                w = w_ref[s, d]
                tile = a_ref[j].astype(F32)
                d1 = pltpu.roll(dc, n - 1, 0)
                d2 = pltpu.roll(d1, n - 1, 0)
                da = (w[2:3, :] * dc + w[1:2, :] * d1 + w[0:1, :] * d2)[0:FM].astype(BF16)
                da_ref[j] = da
                dh = dh + _dot(da, up_ref[j])
                sums = [jnp.sum(v[0:FM] * tile, axis=0, keepdims=True) for v in (d2, d1, dc)]
                sums.append(jnp.sum(dc[0:FM], axis=0, keepdims=True))
                upd = jnp.zeros((8, SW), F32)
                for r, v in enumerate(sums):
                    upd = jnp.where(row == r, jnp.broadcast_to(v, (8, SW)), upd)
                s_ref[s, d] += upd
        x = x_ref[...]
        dx, dg = _rms_bwd(dh, x, _rs(x), g_ref[...])
        dx_ref[...] = dy + dx
        dfn_ref[...] += _colsum8(dg)

    rows = pl.BlockSpec((FM, D), lambda i: (nt - 1 - i, 0))
    blocks = pl.BlockSpec((NDEV, FM, SW), lambda i: (0, nt - 1 - i, 0))
    return pl.pallas_call(
        body, name="ffn_bwd", grid=(nt,),
        in_specs=[rows, blocks, blocks, rows, _resident((NDEV, SW, D)), _resident((2, NG, 3, SW)), _resident((NG, SW, D)),
                  _const((1, D))],
        out_specs=[rows, blocks, _const((2, NG, 8, SW)), _const((8, D))],
        out_shape=[_sds((t, D)), _sds((NDEV, t, SW), BF16), _sds((2, NG, 8, SW)), _sds((8, D))],
        scratch_shapes=[pltpu.VMEM((NDEV, 8, SW), F32)],
        compiler_params=_params())(dy, a, gs, x2, up, conv, down, ffn_norm)


BT = 512


def _xattn_bwd(dx2, x1, qpre, k2, v2, wq, wo, qn4, xan):
    t = x1.shape[0]

    def body(dx2_ref, x1_ref, qpre_ref, k_ref, v_ref, wq_ref, wo_ref, qn_ref, xan_ref,
             dx1_ref, dqpre_ref, dk_ref, dv_ref, dqn_ref, dxan_ref):
        @pl.when(pl.program_id(0) == 0)
        def _():
            for r in (dk_ref, dv_ref, dqn_ref, dxan_ref):
                r[...] = jnp.zeros_like(r)

        dx2 = dx2_ref[...]
        do = _dot_nt(dx2.astype(BF16), wo_ref[...])
        dqs = []
        for h in range(XH):
            sl = slice(XD * h, XD * (h + 1))
            qpre_h = qpre_ref[:, sl]
            pn, q2, rq = _xattn_probs(qpre_h, qn_ref[:, sl], k_ref[:, sl])
            do_h = do[:, sl].astype(BF16)
            dp = _dot_nt(do_h, v_ref[:, sl])
            ds = (pn * (dp - jnp.sum(pn * dp, axis=-1, keepdims=True)) * (1.0 / math.sqrt(XD))).astype(BF16)
            dq2 = _dot(ds, k_ref[:, sl])
            dk_ref[:, sl] += _dot_tn(ds, q2)
            dv_ref[:, sl] += _dot_tn(pn.astype(BF16), do_h)
            dqh, dg = _rms_bwd(dq2, qpre_h, rq, qn_ref[:, sl])
            dqn_ref[...] += _colsum8(dg)
            dqs.append(dqh)
        dqpre = jnp.concatenate(dqs, axis=1).astype(BF16)
        dqpre_ref[...] = dqpre
        dh2 = _dot_nt(dqpre, wq_ref[...])
        x1 = x1_ref[...]
        dx, dg = _rms_bwd(dh2, x1, _rs(x1), xan_ref[...])
        dx1_ref[...] = dx2 + dx
        dxan_ref[...] += _colsum8(dg)

    return pl.pallas_call(
        body, name="xattn_bwd", grid=(t // BT,),
        in_specs=[_rows(BT, D), _rows(BT, D), _rows(BT, D), _const((MEM, D)), _const((MEM, D)), _const((D, D)),
                  _const((D, D)), _const((1, D)), _const((1, D))],
        out_specs=[_rows(BT, D), _rows(BT, D), _const((MEM, D)), _const((MEM, D)), _const((8, XD)), _const((8, D))],
        out_shape=[_sds((t, D)), _sds((t, D), BF16), _sds((MEM, D)), _sds((MEM, D)), _sds((8, XD)), _sds((8, D))],
        compiler_params=_params())(dx2, x1, qpre, k2, v2, wq, wo, qn4, xan)


def _mem_kv_bwd(mem, mh, kpre, dk2, dv2, wkv, kn4, mem_norm):
    def body(m_ref, mh_ref, kpre_ref, dk_ref, dv_ref, w_ref, kn_ref, g_ref, dw_ref, dkn_ref, dmn_ref):
        dkn = jnp.zeros((8, XD), F32)
        dm = jnp.zeros((MEM, D), F32)
        mh = mh_ref[...]
        for h in range(XH):
            sl = slice(XD * h, XD * (h + 1))
            kh = kpre_ref[:, sl]
            dkh, dg = _rms_bwd(dk_ref[:, sl], kh, _rs(kh), kn_ref[:, sl])
            dkn = dkn + _colsum8(dg)
            dkh = dkh.astype(BF16)
            dvh = dv_ref[:, sl].astype(BF16)
            dw_ref[h] = _dot_tn(mh, dkh).astype(BF16)
            dw_ref[XH + h] = _dot_tn(mh, dvh).astype(BF16)
            dm = dm + _dot_nt(dkh, w_ref[h]) + _dot_nt(dvh, w_ref[XH + h])
        dkn_ref[...] = dkn
        m = m_ref[...]
        _, dg = _rms_bwd(dm, m, _rs(m), g_ref[...])
        dmn_ref[...] = _colsum8(dg)

    return pl.pallas_call(
        body, name="mem_kv_bwd", out_shape=[_sds((2 * XH, D, XD), BF16), _sds((8, XD)), _sds((8, D))],
        compiler_params=pltpu.CompilerParams(vmem_limit_bytes=VMEM_LIMIT))(mem, mh, kpre, dk2, dv2, wkv, kn4, mem_norm)


def _mixer_out_bwd(dx1, attn, gm, w_out, aon, gon, gvn, gz, ws, wst, bfull, gvw):
    t = dx1.shape[0]
    nc = TM // BLK

    def body(dx_ref, a_ref, g_ref, wo_ref, aon_ref, gon_ref, x_ref, gz_ref, w_ref, wt_ref, b_ref, gvw_ref,
             da_ref, dan_ref, dgn_ref, dgz_ref, dw_ref, db_ref, dgvw_ref, dbacc_ref):
        @pl.when(pl.program_id(0) == 0)
        def _():
            for r in (dan_ref, dgn_ref, dw_ref, dbacc_ref, dgvw_ref):
                r[...] = jnp.zeros_like(r)

        dy = _dot_nt(dx_ref[...].astype(BF16), wo_ref[...])
        a, g = a_ref[...], g_ref[...]
        da, dna = _rms_bwd(dy[:, :AW], a, _rs(a), aon_ref[...])
        dgm, dng = _rms_bwd(dy[:, AW:], g, _rs(g), gon_ref[...])
        da_ref[...] = da
        dan_ref[...] += _colsum8(dna)
        dgn_ref[...] += _colsum8(dng)

        xin = x_ref[...]
        mixed = _gmlp_mix(w_ref, xin, False) + jnp.tile(b_ref[...], (nc, 1))
        dgu = dgm * mixed
        dmixed = dgm * _gelu(gz_ref[:, :GW])
        lo = _lane((BLK, 128)) < 64
        dbias = jnp.zeros((BLK, GW), F32)
        for c in range(nc):
            dmc = dmixed[c * BLK:(c + 1) * BLK]
            dbias = dbias + dmc
            for j in range(4):
                dm2 = dmc[:, 128 * j:128 * (j + 1)]
                xs = xin[c * BLK:(c + 1) * BLK, 128 * j:128 * (j + 1)]
                z = jnp.zeros_like(dm2)
                dw_ref[2 * j] += _dot_nt(jnp.where(lo, dm2, z).astype(BF16), xs)
                dw_ref[2 * j + 1] += _dot_nt(jnp.where(lo, z, dm2).astype(BF16), xs)
        dbacc_ref[...] += dbias
        dgvn = _gmlp_mix(wt_ref, dmixed.astype(BF16), True)
        gz_u, gz_v = gz_ref[:, :GW], gz_ref[:, GW:]
        gv = _gelu(gz_v)
        dgv, dg = _rms_bwd(dgvn, gv, _rs(gv), gvw_ref[...])
        dgvw_ref[...] += _colsum8(dg)
        dgz_ref[:, :GW] = (dgu * _gelu_grad(gz_u)).astype(BF16)
        dgz_ref[:, GW:] = (dgv * _gelu_grad(gz_v)).astype(BF16)

        @pl.when(pl.program_id(0) == pl.num_programs(0) - 1)
        def _():
            s = dbacc_ref[...]
            sel = (lax.broadcasted_iota(jnp.int32, (8, GW), 1) // HD
                   == lax.broadcasted_iota(jnp.int32, (8, GW), 0)).astype(BF16)
            hi = s.astype(BF16)
            r1 = s - hi.astype(F32)
            mid = r1.astype(BF16)
            lo = (r1 - mid.astype(F32)).astype(BF16)
            db_ref[...] = _dot_nt(sel, hi) + _dot_nt(sel, mid) + _dot_nt(sel, lo)
            r = lax.broadcasted_iota(jnp.int32, (BLK, BLK), 0)
            c = lax.broadcasted_iota(jnp.int32, (BLK, BLK), 1)
            for h in range(8):
                dw_ref[h] = jnp.where(c <= r, dw_ref[h], 0.0)

    return pl.pallas_call(
        body, name="mixer_out_bwd", grid=(t // TM,),
        in_specs=[_rows(TM, D), _rows(TM, AW), _rows(TM, GW), _const((D, D)), _const((1, AW)), _const((1, GW)),
                  _rows(TM, GW), _rows(TM, 2 * GW), _const((8, BLK, BLK)), _const((8, BLK, BLK)),
                  _const((BLK, GW)), _const((1, GW))],
        out_specs=[_rows(TM, AW), _const((8, AW)), _const((8, GW)), _rows(TM, 2 * GW), _const((8, BLK, BLK)),
                   _const((8, BLK)), _const((8, GW))],
        out_shape=[_sds((t, AW)), _sds((8, AW)), _sds((8, GW)), _sds((t, 2 * GW), BF16), _sds((8, BLK, BLK)),
                   _sds((8, BLK)), _sds((8, GW))],
        scratch_shapes=[pltpu.VMEM((BLK, GW), F32)],
        compiler_params=_params())(dx1, attn, gm, w_out, aon, gon, gvn, gz, ws, wst, bfull, gvw)


def _fold_half(v):
    return v + pltpu.roll(v, 64, 1)


def _swa_bwd(q, k, v, dattn, sinks):
    t = q.shape[0]
    nb = t // BLK
    ts = min(t, SB_BWD * BLK)
    sb = ts // BLK
    nt = t // ts

    def body(sink_ref, q_ref, kc_ref, kp_ref, vc_ref, vp_ref, do_ref, dq_ref, dk_ref, dv_ref, ds_ref,
             ck_ref, cv_ref, sacc_ref):
        i = pl.program_id(0)

        @pl.when(i == 0)
        def _():
            ck_ref[...] = jnp.zeros_like(ck_ref)
            cv_ref[...] = jnp.zeros_like(cv_ref)
            sacc_ref[...] = jnp.zeros_like(sacc_ref)

        @pl.when(i < nt)
        def _():
            kk = jnp.concatenate([kp_ref[...], kc_ref[...]], axis=0).astype(F32)
            vv = jnp.concatenate([vp_ref[...], vc_ref[...]], axis=0).astype(F32)
            lo256 = _lane((2 * BLK, 128)) < 64
            acc_k = [jnp.zeros((BLK, 128), F32) for _ in range(sb + 1)]
            acc_v = [jnp.zeros((BLK, 128), F32) for _ in range(sb + 1)]
            for b in range(sb):
                r = slice(b * BLK, (b + 1) * BLK)
                kb, vb = kk[b * BLK:(b + 2) * BLK], vv[b * BLK:(b + 2) * BLK]
                mask = _swa_mask(i == 0) if b == 0 else _swa_mask(False)
                dkk = jnp.zeros((2 * BLK, 128), F32)
                dvv = jnp.zeros((2 * BLK, 128), F32)
                for g in range(2):
                    qs = _stack_heads(q_ref[r, 256 * g:256 * g + 128], q_ref[r, 256 * g + 128:256 * g + 256])
                    dos = _stack_heads(do_ref[r, 256 * g:256 * g + 128],
                                       do_ref[r, 256 * g + 128:256 * g + 256]).astype(BF16)
                    kd = _dup_half(kb, g)
                    pn, psn = _swa_probs(qs, kd, mask, _sink_col(sink_ref, g))
                    dp = _dot_nt(dos, _dup_half(vb, g))
                    dd = jnp.sum(pn * dp, axis=-1, keepdims=True)
                    ds = (pn * (dp - dd) * (1.0 / math.sqrt(HD))).astype(BF16)
                    sacc_ref[g] += jnp.broadcast_to(-psn * dd, (4 * BLK, 128))
                    dqa, dqb = _unstack_heads(_dot(ds, kd))
                    dq_ref[r, 256 * g:256 * g + 128] = dqa
                    dq_ref[r, 256 * g + 128:256 * g + 256] = dqb
                    dkg = _fold_half(_dot_tn(ds, qs))
                    dvg = _fold_half(_dot_tn(pn.astype(BF16), dos))
                    keep = lo256 if g == 0 else jnp.logical_not(lo256)
                    dkk = jnp.where(keep, dkg, dkk)
                    dvv = jnp.where(keep, dvg, dvv)
                acc_k[b], acc_k[b + 1] = acc_k[b] + dkk[0:BLK], acc_k[b + 1] + dkk[BLK:]
                acc_v[b], acc_v[b + 1] = acc_v[b] + dvv[0:BLK], acc_v[b + 1] + dvv[BLK:]
            for out_ref, c_ref, acc in ((dk_ref, ck_ref, acc_k), (dv_ref, cv_ref, acc_v)):
                if sb > 1:
                    out_ref[0:ts - BLK] = c_ref[0:ts - BLK]
                out_ref[ts - BLK:ts] = c_ref[ts - BLK:ts] + acc[0]
                for b in range(sb):
                    c_ref[b * BLK:(b + 1) * BLK] = acc[b + 1]

        @pl.when(i == nt)
        def _():
            dk_ref[...] = ck_ref[...]
            dv_ref[...] = cv_ref[...]
            lane = _lane((8, 128))
            acc = jnp.zeros((8, 128), F32)
            for g in range(2):
                for j in range(4):
                    val = jnp.sum(sacc_ref[g, j * BLK:(j + 1) * BLK, :], axis=0, keepdims=True)
                    acc = jnp.where(lane == 4 * g + j, jnp.broadcast_to(val, (8, 128)), acc)
            ds_ref[...] = acc

    cur = lambda i: (jnp.minimum(i, nt - 1), 0)
    before = lambda i: (jnp.clip(i * sb - 1, 0, nb - 1), 0)
    done = lambda i: (jnp.clip(i - 1, 0, nt - 1), 0)
    return pl.pallas_call(
        body, name="swa_bwd", grid=(nt + 1,),
        in_specs=[pl.BlockSpec(memory_space=pltpu.SMEM), pl.BlockSpec((ts, AW), cur),
                  pl.BlockSpec((ts, KW), cur), pl.BlockSpec((BLK, KW), before),
                  pl.BlockSpec((ts, KW), cur), pl.BlockSpec((BLK, KW), before), pl.BlockSpec((ts, AW), cur)],
        out_specs=[pl.BlockSpec((ts, AW), cur), pl.BlockSpec((ts, KW), done), pl.BlockSpec((ts, KW), done),
                   _const((8, 128))],
        out_shape=[_sds((t, AW)), _sds((t, KW)), _sds((t, KW)), _sds((8, 128))],
        scratch_shapes=[pltpu.VMEM((ts, KW), F32), pltpu.VMEM((ts, KW), F32), pltpu.VMEM((2, 4 * BLK, 128), F32)],
        compiler_params=_params())(sinks, q, k, k, v, v, dattn)


def _mixer_in_bwd(dq, dk, dv, dgz, qk, cos, sin, x, dx1, w_in, mix_norm, qn, kn):
    t = x.shape[0]

    def body(dq_ref, dk_ref, dv_ref, dgz_ref, qk_ref, cos_ref, sin_ref, x_ref, dx1_ref, w_ref, g_ref, qn_ref, kn_ref,
             ones_ref, gx_ref, dproj_ref, dmn_ref, dqn_ref, dkn_ref, qacc_ref, kacc_ref):
        i = pl.program_id(0)

        @pl.when(i == 0)
        def _():
            dmn_ref[...] = jnp.zeros_like(dmn_ref)
            qacc_ref[...] = jnp.zeros_like(qacc_ref)
            kacc_ref[...] = jnp.zeros_like(kacc_ref)

        cos2, sin2 = cos_ref[...], sin_ref[...]
        qpre, kpre = qk_ref[:, :AW], qk_ref[:, AW:]
        dqh = _rope_bwd(dq_ref[...], jnp.tile(cos2, (1, 4)), jnp.tile(sin2, (1, 4)))
        dqpre, dgq = _rms64_bwd(dqh, qpre, _rs64(qpre, ones_ref), qn_ref[...], ones_ref)
        dkh = _rope_bwd(dk_ref[...], cos2, sin2)
        dkpre, dgk = _rms64_bwd(dkh, kpre, _rs64(kpre, ones_ref), kn_ref[...], ones_ref)
        qacc_ref[...] += jnp.sum(dgq, axis=0, keepdims=True)
        kacc_ref[...] += jnp.sum(dgk, axis=0, keepdims=True)
        dproj = jnp.concatenate([dqpre.astype(BF16), dkpre.astype(BF16), dv_ref[...].astype(BF16), dgz_ref[...]], axis=1)
        dproj_ref[...] = dproj
        dh = _dot(dproj, w_ref[...])
        xv = x_ref[...]
        dx, dg = _rms_bwd(dh, xv, _rs(xv), g_ref[...])
        gx_ref[...] = dx1_ref[...] + dx
        dmn_ref[...] += _colsum8(dg)

        @pl.when(i == pl.num_programs(0) - 1)
        def _():
            qa = qacc_ref[...]
            q4 = qa[:, 0:128] + qa[:, 128:256] + qa[:, 256:384] + qa[:, 384:512]
            dqn_ref[...] = jnp.broadcast_to(_fold_half(q4), (8, 128))
            dkn_ref[...] = jnp.broadcast_to(_fold_half(kacc_ref[...]), (8, 128))

    return pl.pallas_call(
        body, name="mixer_in_bwd", grid=(t // TM,),
        in_specs=[_rows(TM, AW), _rows(TM, KW), _rows(TM, KW), _rows(TM, 2 * GW), _rows(TM, AW + KW), _rows(TM, 128),
                  _rows(TM, 128), _rows(TM, D), _rows(TM, D), _const((IN, D)), _const((1, D)), _const((1, AW)),
                  _const((1, KW)), _const((AW, AW))],
        out_specs=[_rows(TM, D), _rows(TM, IN), _const((8, D)), _const((8, 128)), _const((8, 128))],
        out_shape=[_sds((t, D)), _sds((t, IN), BF16), _sds((8, D)), _sds((8, 128)), _sds((8, 128))],
        scratch_shapes=[pltpu.VMEM((1, AW), F32), pltpu.VMEM((1, KW), F32)],
        compiler_params=_params())(dq, dk, dv, dgz, qk, cos, sin, x, dx1, w_in, mix_norm, qn, kn, _head_ones())


def _local_step(x, mem, pos, target, p, fetch, ship):
    t = x.shape[0]
    p = dict(p)
    p.update(fetch(0, None))
    inv_freq = 1.0 / (ROPE_THETA ** (jnp.arange(HD // 2, dtype=F32) * (2.0 / HD)))
    cos, sin = _rope_tables(pos, jnp.tile(inv_freq, 4).reshape(1, 128))
    qn = jnp.tile(p["q_norm"], (1, AW // HD))
    kn = jnp.tile(p["k_norm"], (1, KW // HD))
    qn4 = jnp.tile(p["xa_q_norm"], (1, XH))
    kn4 = jnp.tile(p["xa_k_norm"], (1, XH))
    ws = p["gmlp_ws"]
    wst = jnp.swapaxes(ws, 1, 2)
    bfull = jnp.repeat(p["gmlp_bs"].T, HD, axis=1)
    conv_b = p["ffn_conv_b"]

    h1, qk, gz, q, k, v, gvn = _mixer_in_fwd(x, p["mix_norm"], p["w_in"], qn, kn, p["gmlp_v_norm"], cos, sin)
    attn = _swa_fwd(q, k, v, p["attn_sinks"])
    p.update(fetch(1, attn))
    gm, ycat, x1, h2 = _mixer_out_fwd(attn, gvn, gz, ws, bfull, x, p["w_out"], p["attn_out_norm"], p["gmlp_out_norm"],
                                      p["xa_norm"])
    mh, kpre, k2, v2 = _mem_kv_fwd(mem, p["mem_norm"], p["xa_wkv"], kn4)
    qpre, o, x2, h3 = _xattn_fwd(h2, x1, p["xa_wq"], qn4, k2, v2, p["xa_wo"], p["ffn_norm"])
    p.update(fetch(2, h3))
    conv = p["ffn_conv"]
    a, u, gs, dy, loss8 = _ffn_fwd(h3, x2, target, p["ffn_up"], conv, conv_b, p["ffn_down"])

    raw = {}
    d_down = _mm_tn(u, dy, "ffn_down_bwd_w")
    dx2, da, raw["conv_sums"], raw["ffn_norm"] = _ffn_bwd(dy, a, gs, x2, p["ffn_up"], conv, p["ffn_down"], p["ffn_norm"])
    d_up = _mm_tn(da, h3, "ffn_up_bwd_w")
    token = ship(0, {"ffn_down": d_down, "ffn_up": d_up, "ffn_conv": raw["conv_sums"][:, :, 0:3]})
    dx1, dqpre, dk2, dv2, raw["xa_q_norm"], raw["xa_norm"] = _xattn_bwd(
        dx2, x1, qpre, k2, v2, p["xa_wq"], p["xa_wo"], qn4 + jnp.tile(token[0:1], (1, D // 128)), p["xa_norm"])
    d_wo = _mm_tn(o, dx2, "xa_wo_bwd_w")
    d_wq = _mm_tn(h2, dqpre, "xa_wq_bwd_w")
    d_wkv, raw["xa_k_norm"], raw["mem_norm"] = _mem_kv_bwd(mem, mh, kpre, dk2, dv2, p["xa_wkv"], kn4, p["mem_norm"])
    d_w_out = _mm_tn(ycat, dx1, "w_out_bwd_w")
    (dattn, raw["attn_out_norm"], raw["gmlp_out_norm"], dgz, raw["gmlp_ws"], raw["gmlp_bs"],
     raw["gmlp_v_norm"]) = _mixer_out_bwd(dx1, attn, gm, p["w_out"], p["attn_out_norm"], p["gmlp_out_norm"],
                                          gvn, gz, ws, wst, bfull, p["gmlp_v_norm"])
    token = ship(1, {"xa_wo": d_wo, "xa_wq": d_wq, "xa_wkv": d_wkv, "w_out": d_w_out}, [raw["gmlp_ws"]])
    dq, dk, dv, raw["attn_sinks"] = _swa_bwd(q, k, v, dattn, p["attn_sinks"] + token[0:1, 0:8])
    grad_x, dproj, raw["mix_norm"], raw["q_norm"], raw["k_norm"] = _mixer_in_bwd(
        dq, dk, dv, dgz, qk, cos, sin, x, dx1, p["w_in"], p["mix_norm"], qn, kn)
    d_w_in = _mm_tn(dproj, h1, "w_in_bwd_w")
    raw["loss"] = loss8
    return grad_x, {"w_in": d_w_in}, raw


def _cast_shards(shards):
    def body(*refs):
        n = len(refs) // 2
        for i_ref, o_ref in zip(refs[:n], refs[n:]):
            o_ref[...] = i_ref[...].astype(BF16)

    return pl.pallas_call(body, name="cast_shards", out_shape=[_sds(s.shape, BF16) for s in shards],
                          compiler_params=pltpu.CompilerParams(vmem_limit_bytes=VMEM_LIMIT))(*shards)


HBM_SPEC = pl.BlockSpec(memory_space=pltpu.HBM)
SEM_SPEC = pl.BlockSpec(memory_space=pltpu.SEMAPHORE)


ALL_K = tuple(range(1, NDEV))
CHIP_K = (1, 2, 4, 6)
RELAY_K = (2, 4, 6)


def _peer(k):
    x, y, cc = lax.axis_index("x"), lax.axis_index("y"), lax.axis_index("c")
    return 1 - x if k & 4 else x, 1 - y if k & 2 else y, 1 - cc if k & 1 else cc


def _remote_copies(src_refs, land_refs, send_refs, recv_refs, nd, ks):
    me = 4 * lax.axis_index("x") + 2 * lax.axis_index("y") + lax.axis_index("c")
    copies = []
    for a, (src_ref, land_ref) in enumerate(zip(src_refs, land_refs)):
        for j, k in enumerate(ks):
            px, py, pc = _peer(k)
            copies.append((k, pltpu.make_async_remote_copy(
                src_ref=src_ref.at[4 * px + 2 * py + pc] if a < nd else src_ref, dst_ref=land_ref.at[me],
                send_sem=send_refs[a].at[j], recv_sem=recv_refs[a].at[j],
                device_id=(px, py, pc), device_id_type=pl.DeviceIdType.MESH)))
    return copies


def _relay_copies(land_refs, send_refs, recv_refs):
    copies = []
    for a, land_ref in enumerate(land_refs):
        for j, k in enumerate(RELAY_K):
            px, py, pc = _peer(k)
            slot = land_ref.at[4 * px + 2 * py + pc]
            copies.append(pltpu.make_async_remote_copy(
                src_ref=slot, dst_ref=slot, send_sem=send_refs[a].at[j], recv_sem=recv_refs[a].at[j],
                device_id=_peer(1), device_id_type=pl.DeviceIdType.MESH))
    return copies


def _own_slot(src, by_dest, me):
    block = lax.dynamic_index_in_dim(src, me, 0, keepdims=True) if by_dest else src[None]
    return lax.dynamic_update_index_in_dim(lax.empty((NDEV,) + block.shape[1:], src.dtype), block, me, 0)


SIDE_EFFECT = pltpu.CompilerParams(has_side_effects=pltpu.SideEffectType.DATAFLOW_SIDE_EFFECTING)


def _exchange_start(by_dest, for_all, me, name, ks=ALL_K):
    srcs = list(by_dest) + list(for_all)
    n, nd = len(srcs), len(by_dest)
    lands = [_own_slot(s, a < nd, me) for a, s in enumerate(srcs)]

    def body(*refs):
        for _, cp in _remote_copies(refs[:n], refs[n:2 * n], refs[2 * n:3 * n], refs[3 * n:4 * n], nd, ks):
            cp.start()
        refs[-1][...] = jnp.zeros((8, 128), F32)

    sems = [pltpu.SemaphoreType.DMA((len(ks),))] * (2 * n)
    thru = [pltpu.HBM(v.shape, v.dtype) for v in srcs + lands]
    res = pl.pallas_call(
        body, name=name, out_shape=sems + thru + [_sds((8, 128))],
        in_specs=[HBM_SPEC] * (2 * n), out_specs=[SEM_SPEC] * (2 * n) + [HBM_SPEC] * (2 * n) + [pl.BlockSpec(memory_space=pltpu.VMEM)],
        input_output_aliases={i: 2 * n + i for i in range(2 * n)}, compiler_params=SIDE_EFFECT)(
            *[pltpu.with_memory_space_constraint(v, pltpu.HBM) for v in srcs + lands])
    return (res[:2 * n], res[2 * n:4 * n], nd, ks, None), res[-1]


def _exchange_relay(state, after, name):
    sems, thru, nd, ks, _ = state
    n = len(thru) // 2

    def body(*refs):
        for k, cp in _remote_copies(refs[:n], refs[n:2 * n], refs[2 * n:3 * n], refs[3 * n:4 * n], nd, ks):
            if k in RELAY_K:
                cp.wait_recv()
        for cp in _relay_copies(refs[n:2 * n], refs[4 * n + 1:5 * n + 1], refs[5 * n + 1:6 * n + 1]):
            cp.start()

    relay_sems = [pltpu.SemaphoreType.DMA((len(RELAY_K),))] * (2 * n)
    res = pl.pallas_call(
        body, name=name, out_shape=relay_sems + [pltpu.HBM(v.shape, v.dtype) for v in thru],
        in_specs=[HBM_SPEC] * (2 * n) + [SEM_SPEC] * (2 * n) + [pl.BlockSpec(memory_space=pl.ANY)],
        out_specs=[SEM_SPEC] * (2 * n) + [HBM_SPEC] * (2 * n),
        input_output_aliases={i: 2 * n + i for i in range(2 * n)}, compiler_params=SIDE_EFFECT)(*thru, *sems, after)
    return sems, res[2 * n:], nd, ks, res[:2 * n]


def _exchange_wait(state, after, name):
    sems, thru, nd, ks, relay_sems = state
    n = len(thru) // 2

    def body(*refs):
        for k, cp in _remote_copies(refs[:n], refs[n:2 * n], refs[2 * n:3 * n], refs[3 * n:4 * n], nd, ks):
            cp.wait_send()
            if relay_sems is None or k not in RELAY_K:
                cp.wait_recv()
        if relay_sems is not None:
            for cp in _relay_copies(refs[n:2 * n], refs[4 * n:5 * n], refs[5 * n:6 * n]):
                cp.wait_send()
                cp.wait_recv()

    extra = [] if relay_sems is None else list(relay_sems)
    res = pl.pallas_call(
        body, name=name, out_shape=[pltpu.HBM(v.shape, v.dtype) for v in thru],
        in_specs=[HBM_SPEC] * (2 * n) + [SEM_SPEC] * (2 * n + len(extra)) + [pl.BlockSpec(memory_space=pl.ANY)],
        out_specs=[HBM_SPEC] * (2 * n), input_output_aliases={i: i for i in range(2 * n)}, compiler_params=SIDE_EFFECT)(
            *thru, *sems, *extra, after)
    return res[n:]


def _adam(items, name):
    n = len(items)

    def body(*refs):
        for j in range(n):
            p_ref, w_ref, m_ref, v_ref = refs[4 * j:4 * j + 4]
            g_ref, d_ref, nm_ref, nv_ref = refs[4 * n + 4 * j:4 * n + 4 * j + 4]
            g = _sum_parts(p_ref)
            g_ref[...] = g
            d_ref[...], nm_ref[...], nv_ref[...] = _adam_math(g, w_ref[...], m_ref[...], v_ref[...])

    res = pl.pallas_call(
        body, name=name, out_shape=[_sds(it[1].shape) for it in items for _ in range(4)],
        compiler_params=pltpu.CompilerParams(vmem_limit_bytes=VMEM_LIMIT))(*[a for it in items for a in it])
    return [res[4 * j:4 * j + 4] for j in range(n)]


GATHER_GROUPS = (("w_in",), ("w_out", "xa_wkv", "xa_wq", "xa_wo"), ("ffn_up", "ffn_conv", "ffn_down"))
SCATTER_GROUPS = (("ffn_up", "ffn_down", "ffn_conv"), ("xa_wo", "xa_wq", "xa_wkv", "w_out"), ("w_in",))
ADAM_ALONE = ("ffn_up",)
BIG = tuple(n for grp in GATHER_GROUPS for n in grp)
BY_COLUMN = ("w_in", "ffn_up")
VECS = (("mix_norm", D), ("q_norm", HD), ("k_norm", HD), ("attn_sinks", 8), ("gmlp_v_norm", GW), ("attn_out_norm", AW),
        ("gmlp_out_norm", GW), ("xa_norm", D), ("mem_norm", D), ("xa_q_norm", XD), ("xa_k_norm", XD), ("ffn_norm", D))
BS_ROW = 16
VEC_ROWS = 24
SMALL = tuple(n for n, _ in VECS) + ("gmlp_bs", "gmlp_ws", "ffn_conv_b")


def _pack_small(raw):
    names = [n for n, _ in VECS] + ["gmlp_bs", "conv_sums"]

    def body(*refs):
        ins = dict(zip(names, refs))
        vec_ref, cb_ref = refs[len(names):]
        vec_ref[...] = jnp.zeros_like(vec_ref)
        for r, (n, w) in enumerate(VECS):
            vec_ref[r:r + 1, 0:w] = ins[n][0:1, 0:w]
        vec_ref[BS_ROW:BS_ROW + 8, 0:BLK] = ins["gmlp_bs"][...]
        for s in range(2):
            for d in range(NG):
                cb_ref[s, d] = ins["conv_sums"][s, d, 3:4, :]

    return pl.pallas_call(body, name="pack_small", out_shape=[_sds((VEC_ROWS, D)), _sds((2, NG, 1, SW))])(
        *[raw[n] for n in names])


def _adam_math(g, w, m, v):
    nm = B1 * m + (1.0 - B1) * g
    nv = B2 * v + (1.0 - B2) * (g * g)
    m_hat = nm / (1.0 - B1 ** STEP)
    v_hat = nv / (1.0 - B2 ** STEP)
    return -LR * (m_hat / (jnp.sqrt(v_hat) + AEPS) + WD * w), nm, nv


def _sum_parts(p_ref):
    g = p_ref[0].astype(F32)
    for j in range(1, NDEV):
        g = g + p_ref[j].astype(F32)
    return g


def _adam_small(parts_vec, parts_ws, parts_cb, w, m, v):
    def body(*refs):
        pv_ref, pws_ref, pcb_ref = refs[:3]
        ins = refs[3:3 + 3 * len(SMALL)]
        outs = refs[3 + 3 * len(SMALL):]
        gv = _sum_parts(pv_ref)
        for j, n in enumerate(SMALL):
            w_ref, m_ref, v_ref = ins[3 * j:3 * j + 3]
            o = outs[4 * j:4 * j + 4]
            if n == "gmlp_ws":
                g = _sum_parts(pws_ref)
            elif n == "ffn_conv_b":
                g = _sum_parts(pcb_ref)
            elif n == "gmlp_bs":
                g = gv[BS_ROW:BS_ROW + 8, 0:BLK]
            else:
                g = gv[j:j + 1, 0:VECS[j][1]]
            lead = n in ("gmlp_ws", "gmlp_bs")
            res = (g,) + _adam_math(g, w_ref[0] if lead else w_ref[...], m_ref[0] if lead else m_ref[...],
                                    v_ref[0] if lead else v_ref[...])
            for o_ref, val in zip(o, res):
                if lead:
                    o_ref[0] = val
                else:
                    o_ref[...] = val

    args = [parts_vec, parts_ws, parts_cb] + [d[n] for n in SMALL for d in (w, m, v)]
    res = pl.pallas_call(body, name="adam_small", out_shape=[_sds(w[n].shape) for n in SMALL for _ in range(4)],
                         compiler_params=pltpu.CompilerParams(vmem_limit_bytes=VMEM_LIMIT))(*args)
    return {n: tuple(res[4 * j:4 * j + 4]) for j, n in enumerate(SMALL)}


def kernel(x, mem, positions, mix_norm, w_in, q_norm, k_norm, attn_sinks, gmlp_v_norm, gmlp_ws, gmlp_bs, attn_out_norm, gmlp_out_norm, w_out, xa_norm, mem_norm, xa_wq, xa_wkv, xa_q_norm, xa_k_norm, xa_wo, ffn_norm, ffn_up, ffn_conv, ffn_conv_b, ffn_down, loss_target, m_mix_norm, m_w_in, m_q_norm, m_k_norm, m_attn_sinks, m_gmlp_v_norm, m_gmlp_ws, m_gmlp_bs, m_attn_out_norm, m_gmlp_out_norm, m_w_out, m_xa_norm, m_mem_norm, m_xa_wq, m_xa_wkv, m_xa_q_norm, m_xa_k_norm, m_xa_wo, m_ffn_norm, m_ffn_up, m_ffn_conv, m_ffn_conv_b, m_ffn_down, v_mix_norm, v_w_in, v_q_norm, v_k_norm, v_attn_sinks, v_gmlp_v_norm, v_gmlp_ws, v_gmlp_bs, v_attn_out_norm, v_gmlp_out_norm, v_w_out, v_xa_norm, v_mem_norm, v_xa_wq, v_xa_wkv, v_xa_q_norm, v_xa_k_norm, v_xa_wo, v_ffn_norm, v_ffn_up, v_ffn_conv, v_ffn_conv_b, v_ffn_down):
    names = ("mix_norm", "w_in", "q_norm", "k_norm", "attn_sinks", "gmlp_v_norm", "gmlp_ws", "gmlp_bs", "attn_out_norm",
             "gmlp_out_norm", "w_out", "xa_norm", "mem_norm", "xa_wq", "xa_wkv", "xa_q_norm", "xa_k_norm", "xa_wo",
             "ffn_norm", "ffn_up", "ffn_conv", "ffn_conv_b", "ffn_down")
    w = dict(zip(names, (mix_norm, w_in, q_norm, k_norm, attn_sinks, gmlp_v_norm, gmlp_ws, gmlp_bs, attn_out_norm,
                         gmlp_out_norm, w_out, xa_norm, mem_norm, xa_wq, xa_wkv, xa_q_norm, xa_k_norm, xa_wo, ffn_norm,
                         ffn_up, ffn_conv, ffn_conv_b, ffn_down)))
    m = dict(zip(names, (m_mix_norm, m_w_in, m_q_norm, m_k_norm, m_attn_sinks, m_gmlp_v_norm, m_gmlp_ws, m_gmlp_bs,
                         m_attn_out_norm, m_gmlp_out_norm, m_w_out, m_xa_norm, m_mem_norm, m_xa_wq, m_xa_wkv,
                         m_xa_q_norm, m_xa_k_norm, m_xa_wo, m_ffn_norm, m_ffn_up, m_ffn_conv, m_ffn_conv_b, m_ffn_down)))
    v = dict(zip(names, (v_mix_norm, v_w_in, v_q_norm, v_k_norm, v_attn_sinks, v_gmlp_v_norm, v_gmlp_ws, v_gmlp_bs,
                         v_attn_out_norm, v_gmlp_out_norm, v_w_out, v_xa_norm, v_mem_norm, v_xa_wq, v_xa_wkv,
                         v_xa_q_norm, v_xa_k_norm, v_xa_wo, v_ffn_norm, v_ffn_up, v_ffn_conv, v_ffn_conv_b, v_ffn_down)))
    t = x.shape[1]

    me = 4 * lax.axis_index("x") + 2 * lax.axis_index("y") + lax.axis_index("c")

    def rows(a, n):
        return jnp.swapaxes(a[0], 0, 1) if n in BY_COLUMN else a[0]

    mats = [n for n in BIG if n != "ffn_conv"]
    shard = dict(zip(mats, _cast_shards([rows(w[n], n) for n in mats])), ffn_conv=w["ffn_conv"][0])
    gathers, tokens = zip(*[_exchange_start([], [shard[n] for n in grp], me, "gather_start_%d" % i,
                                            CHIP_K if i > 0 else ALL_K)
                            for i, grp in enumerate(GATHER_GROUPS)])

    def fetch(i, after):
        after = tokens[0] + tokens[1] + tokens[2] if after is None else after
        state = gathers[i]
        if state[3] == CHIP_K:
            state = _exchange_relay(state, after, "gather_relay_%d" % i)
        got = dict(zip(GATHER_GROUPS[i], _exchange_wait(state, after, "gather_wait_%d" % i)))
        if "w_in" in got:
            got["w_in"] = got["w_in"].reshape(IN, D)
        for n in ("w_out", "xa_wq", "xa_wo"):
            if n in got:
                got[n] = got[n].reshape(D, D)
        if "ffn_down" in got:
            got["ffn_down"] = got["ffn_down"].reshape(NG, SW, D)
            got["ffn_conv"] = got["ffn_conv"].reshape(2, NG, 3, SW)
        return got

    scatters = []

    def ship(i, grads, for_all=()):
        by_dest = [grads[n].reshape((NDEV,) + shard[n].shape) for n in SCATTER_GROUPS[i]]
        state, token = _exchange_start(by_dest, for_all, me, "scatter_start_%d" % i)
        scatters.append(state)
        return token

    conv_b = {k: d["ffn_conv_b"].reshape(NDEV, 1, SW) for k, d in (("w", w), ("m", m), ("v", v))}
    p = {n: w[n] for n in SMALL[:-1]}
    p["gmlp_ws"], p["gmlp_bs"] = w["gmlp_ws"][0], w["gmlp_bs"][0]
    p["ffn_conv_b"] = conv_b["w"].reshape(2, NG, 1, SW)
    grad_x, g, raw = _local_step(x[0], mem[0], positions.reshape(t, 1), loss_target[0], p, fetch, ship)
    loss = lax.psum(raw["loss"][0, 0], ("x", "y", "c"))

    vec, cb = _pack_small(raw)
    after = ship(2, g, [vec, cb.reshape(NDEV, 1, SW)])
    res, rest = {}, []
    for i, grp in enumerate(SCATTER_GROUPS):
        got = _exchange_wait(scatters[i], after, "scatter_wait_%d" % i)
        rest += got[len(grp):]
        parts = dict(zip(grp, got))
        for batch in ([n for n in grp if n in ADAM_ALONE], [n for n in grp if n not in ADAM_ALONE]):
            if batch:
                outs = _adam([(parts[n], rows(w[n], n), rows(m[n], n), rows(v[n], n)) for n in batch], "adam_" + batch[0])
                for n, out in zip(batch, outs):
                    res[n] = [jnp.swapaxes(o, 0, 1) if n in BY_COLUMN else o for o in out]
                    after = out[0]
    ws_parts, vec_parts, cb_parts = rest
    small = lambda d, k: {**{n: d[n] for n in SMALL[:-1]}, "ffn_conv_b": conv_b[k]}
    res.update(_adam_small(vec_parts, ws_parts, cb_parts, small(w, "w"), small(m, "m"), small(v, "v")))

    outs = [loss, grad_x[None]]
    for j in range(4):
        outs += [res[n][j].reshape(w[n].shape) for n in names]
    return tuple(outs)
```

```python
import math

import jax
import jax.numpy as jnp
from jax import lax
from jax.experimental import pallas as pl
from jax.experimental.pallas import tpu as pltpu

F32 = jnp.float32
BF16 = jnp.bfloat16

D = 1024
HD = 64
AW = 512
KW = 128
GW = 512
IN = AW + 2 * KW + 2 * GW
BLK = 128
MEM = 256
XH = 4
XD = 256
FF = 2816
EPS = 1e-6
ROPE_THETA = 10000.0
NDEV = 8
LR, B1, B2, AEPS, WD, STEP = 0.001, 0.9, 0.999, 1e-08, 0.01, 10

TM = 512
WK = (2048, 4096)
WK_VMEM = 40 * 1024 * 1024
VMEM_LIMIT = 56 * 1024 * 1024
NEG = float(jnp.finfo(jnp.float32).min)
GELU_C0 = math.sqrt(2.0 / math.pi)
GELU_C1 = 0.044715


def _dot(a, b):
    return jnp.dot(a, b, preferred_element_type=F32)


def _dot_nt(a, b):
    return lax.dot_general(a, b, (((1,), (1,)), ((), ())), preferred_element_type=F32)


def _dot_tn(a, b):
    return lax.dot_general(a, b, (((0,), (0,)), ((), ())), preferred_element_type=F32)


def _rs(x):
    return lax.rsqrt(jnp.mean(x * x, axis=-1, keepdims=True) + EPS)


def _rms_bwd(dy, x, r, g):
    xh = x * r
    dxh = dy * g
    dx = r * (dxh - xh * jnp.mean(dxh * xh, axis=-1, keepdims=True))
    return dx, dy * xh


def _lane(shape):
    return lax.broadcasted_iota(jnp.int32, shape, len(shape) - 1)


def _gsum64(v, ones_ref):
    w = v.shape[-1]
    ones = ones_ref[0:w, 0:w]
    hi = v.astype(BF16)
    lo = (v - hi.astype(F32)).astype(BF16)
    return _dot(hi, ones) + _dot(lo, ones)


def _head_ones():
    i = jnp.arange(AW) // HD
    return (i[:, None] == i[None, :]).astype(BF16)


def _rs64(x, ones_ref):
    return lax.rsqrt(_gsum64(x * x, ones_ref) * (1.0 / HD) + EPS)


def _rms64_bwd(dy, x, r, g, ones_ref):
    xh = x * r
    dxh = dy * g
    dx = r * (dxh - xh * (_gsum64(dxh * xh, ones_ref) * (1.0 / HD)))
    return dx, dy * xh


def _rot_half(v):
    w = v.shape[-1]
    return jnp.where((_lane(v.shape) & 32) == 0, pltpu.roll(v, w - 32, 1), pltpu.roll(v, 32, 1))


def _rope(v, cos, sin_signed):
    return v * cos + _rot_half(v) * sin_signed


def _rope_bwd(dv, cos, sin_signed):
    return dv * cos + _rot_half(dv * sin_signed)


def _gelu(z):
    return 0.5 * z * (1.0 + jnp.tanh(GELU_C0 * (z + GELU_C1 * z * z * z)))


def _gelu_grad(z):
    t = jnp.tanh(GELU_C0 * (z + GELU_C1 * z * z * z))
    return 0.5 * (1.0 + t) + 0.5 * z * (1.0 - t * t) * (GELU_C0 * (1.0 + 3.0 * GELU_C1 * z * z))


def _colsum8(v):
    s = jnp.sum(v, axis=0, keepdims=True)
    row = lax.broadcasted_iota(jnp.int32, (8, v.shape[1]), 0)
    return jnp.where(row == 0, jnp.broadcast_to(s, (8, v.shape[1])), 0.0)


def _params(n_axes=1):
    return pltpu.CompilerParams(dimension_semantics=("arbitrary",) * n_axes, vmem_limit_bytes=VMEM_LIMIT)


def _rows(tm, w):
    return pl.BlockSpec((tm, w), lambda i: (i, 0))


def _const(shape):
    nd = len(shape)
    return pl.BlockSpec(shape, lambda *_: (0,) * nd)


def _sds(shape, dtype=F32):
    return jax.ShapeDtypeStruct(shape, dtype)


def _mm_tn(a, b, name):
    g = max(a.shape[0] if a.ndim == 3 else 1, b.shape[0] if b.ndim == 3 else 1)
    t, m = a.shape[-2:]
    n = b.shape[-1]

    def body(a_ref, b_ref, o_ref, acc_ref):
        i = pl.program_id(1)

        @pl.when(i == 0)
        def _():
            acc_ref[...] = jnp.zeros_like(acc_ref)

        acc_ref[...] += _dot_tn(a_ref[...].astype(BF16), b_ref[...].astype(BF16))

        @pl.when(i == pl.num_programs(1) - 1)
        def _():
            o_ref[...] = acc_ref[...].astype(BF16)

    def vmem(tk):
        return 2 * tk * (m * a.dtype.itemsize + n * b.dtype.itemsize) + m * n * (4 + 2 * 2)

    tk = min(t, max(k for k in WK if k == WK[0] or vmem(k) <= WK_VMEM))

    def spec(v):
        w = v.shape[-1]
        if v.ndim == 3:
            return pl.BlockSpec((None, tk, w), lambda j, i: (j, i, 0))
        return pl.BlockSpec((tk, w), lambda j, i: (i, 0))

    return pl.pallas_call(
        body, name=name, grid=(g, t // tk), in_specs=[spec(a), spec(b)],
        out_specs=pl.BlockSpec((None, m, n), lambda j, i: (j, 0, 0)), out_shape=_sds((g, m, n), BF16),
        scratch_shapes=[pltpu.VMEM((m, n), F32)], compiler_params=_params(2))(a, b)


def _rope_tables(pos, inv_freq):
    t = pos.shape[0]

    def body(pos_ref, f_ref, cos_ref, sin_ref):
        ang = pos_ref[...].astype(F32) * f_ref[...]
        sign = jnp.where((_lane(ang.shape) & 32) == 0, -1.0, 1.0)
        cos_ref[...] = jnp.cos(ang)
        sin_ref[...] = jnp.sin(ang) * sign

    return pl.pallas_call(
        body, name="rope_tables", grid=(t // TM,),
        in_specs=[_rows(TM, 1), _const((1, 128))], out_specs=[_rows(TM, 128), _rows(TM, 128)],
        out_shape=[_sds((t, 128)), _sds((t, 128))], compiler_params=_params())(pos, inv_freq)


def _mixer_in_fwd(x, mix_norm, w_in, qn, kn, gvw, cos, sin):
    t = x.shape[0]

    def body(x_ref, g_ref, w_ref, qn_ref, kn_ref, gvw_ref, cos_ref, sin_ref, ones_ref,
             h_ref, qk_ref, gz_ref, q_ref, k_ref, v_ref, gvn_ref):
        x = x_ref[...]
        h = (x * _rs(x) * g_ref[...]).astype(BF16)
        h_ref[...] = h
        proj = _dot_nt(h, w_ref[...])
        qk = proj[:, :AW + KW]
        qk_ref[...] = qk
        gz = proj[:, AW + 2 * KW:]
        gz_ref[...] = gz
        cos2, sin2 = cos_ref[...], sin_ref[...]
        q = qk[:, :AW]
        q = q * _rs64(q, ones_ref) * qn_ref[...]
        q_ref[...] = _rope(q, jnp.tile(cos2, (1, 4)), jnp.tile(sin2, (1, 4))).astype(BF16)
        k = qk[:, AW:]
        k = k * _rs64(k, ones_ref) * kn_ref[...]
        k_ref[...] = _rope(k, cos2, sin2).astype(BF16)
        v_ref[...] = proj[:, AW + KW:AW + 2 * KW].astype(BF16)
        gv = _gelu(gz[:, GW:])
        gvn_ref[...] = (gv * _rs(gv) * gvw_ref[...]).astype(BF16)

    return pl.pallas_call(
        body, name="mixer_in_fwd", grid=(t // TM,),
        in_specs=[_rows(TM, D), _const((1, D)), _const((IN, D)), _const((1, AW)), _const((1, KW)),
                  _const((1, GW)), _rows(TM, 128), _rows(TM, 128), _const((AW, AW))],
        out_specs=[_rows(TM, D), _rows(TM, AW + KW), _rows(TM, 2 * GW), _rows(TM, AW), _rows(TM, KW),
                   _rows(TM, KW), _rows(TM, GW)],
        out_shape=[_sds((t, D), BF16), _sds((t, AW + KW)), _sds((t, 2 * GW)), _sds((t, AW), BF16),
                   _sds((t, KW), BF16), _sds((t, KW), BF16), _sds((t, GW), BF16)],
        compiler_params=_params())(x, mix_norm, w_in, qn, kn, gvw, cos, sin, _head_ones())


def _dup_half(kk, g):
    lane = _lane(kk.shape)
    other = pltpu.roll(kk, 64, 1)
    keep = (lane < 64) if g == 0 else (lane >= 64)
    return jnp.where(keep, kk, other).astype(BF16)


def _swa_mask(first_block):
    qi = lax.broadcasted_iota(jnp.int32, (4 * BLK, 2 * BLK), 0) & (BLK - 1)
    kj = lax.broadcasted_iota(jnp.int32, (4 * BLK, 2 * BLK), 1)
    diff = qi + BLK - kj
    band = (diff >= 0) & (diff < BLK)
    return band & (jnp.logical_not(first_block) | (kj >= BLK))


def _stack_heads(a2, b2):
    lo = _lane(a2.shape) < 64
    z = jnp.zeros_like(a2)
    return jnp.concatenate([jnp.where(lo, a2, z), jnp.where(lo, z, a2), jnp.where(lo, b2, z), jnp.where(lo, z, b2)], axis=0)


def _unstack_heads(o):
    lo = _lane((BLK, 128)) < 64
    return jnp.where(lo, o[0:BLK], o[BLK:2 * BLK]), jnp.where(lo, o[2 * BLK:3 * BLK], o[3 * BLK:4 * BLK])


def _sink_col(sink_ref, g):
    row = lax.broadcasted_iota(jnp.int32, (4 * BLK, 1), 0)
    s = [sink_ref[0, 4 * g + j] for j in range(4)]
    return jnp.where(row < BLK, s[0], jnp.where(row < 2 * BLK, s[1], jnp.where(row < 3 * BLK, s[2], s[3])))


def _swa_probs(qs, kd, mask, sink):
    s = _dot_nt(qs, kd) * (1.0 / math.sqrt(HD))
    s = jnp.where(mask, s, NEG)
    m = jnp.maximum(jnp.max(s, axis=-1, keepdims=True), sink)
    p = jnp.exp(s - m)
    ps = jnp.exp(sink - m)
    inv = 1.0 / (jnp.sum(p, axis=-1, keepdims=True) + ps)
    return p * inv, ps * inv


SB = 8
SB_BWD = 2


def _swa_fwd(q, k, v, sinks):
    t = q.shape[0]
    ts = min(t, SB * BLK)

    def body(sink_ref, q_ref, kc_ref, kp_ref, vc_ref, vp_ref, o_ref):
        i = pl.program_id(0)
        kk = jnp.concatenate([kp_ref[...], kc_ref[...]], axis=0).astype(F32)
        vv = jnp.concatenate([vp_ref[...], vc_ref[...]], axis=0).astype(F32)
        for b in range(ts // BLK):
            r = slice(b * BLK, (b + 1) * BLK)
            kb, vb = kk[b * BLK:(b + 2) * BLK], vv[b * BLK:(b + 2) * BLK]
            mask = _swa_mask(i == 0) if b == 0 else _swa_mask(False)
            for g in range(2):
                qs = _stack_heads(q_ref[r, 256 * g:256 * g + 128], q_ref[r, 256 * g + 128:256 * g + 256])
                pn, _ = _swa_probs(qs, _dup_half(kb, g), mask, _sink_col(sink_ref, g))
                oa, ob = _unstack_heads(_dot(pn.astype(BF16), _dup_half(vb, g)))
                o_ref[r, 256 * g:256 * g + 128] = oa
                o_ref[r, 256 * g + 128:256 * g + 256] = ob

    cur = lambda i: (i, 0)
    prev = lambda i: (jnp.maximum(i * (ts // BLK) - 1, 0), 0)
    return pl.pallas_call(
        body, name="swa_fwd", grid=(t // ts,),
        in_specs=[pl.BlockSpec(memory_space=pltpu.SMEM), pl.BlockSpec((ts, AW), cur),
                  pl.BlockSpec((ts, KW), cur), pl.BlockSpec((BLK, KW), prev),
                  pl.BlockSpec((ts, KW), cur), pl.BlockSpec((BLK, KW), prev)],
        out_specs=pl.BlockSpec((ts, AW), cur), out_shape=_sds((t, AW)),
        compiler_params=_params())(sinks, q, k, k, v, v)


def _causal_bf16(w_ref, h, transposed):
    r = lax.broadcasted_iota(jnp.int32, (BLK, BLK), 0)
    c = lax.broadcasted_iota(jnp.int32, (BLK, BLK), 1)
    keep = (r <= c) if transposed else (c <= r)
    return jnp.where(keep, w_ref[h], 0.0).astype(BF16)


def _gmlp_mix(w_ref, xin, transposed):
    lo = _lane((BLK, 128)) < 64
    wm = [_causal_bf16(w_ref, h, transposed) for h in range(8)]
    rows = []
    for c in range(xin.shape[0] // BLK):
        cols = []
        for j in range(4):
            xs = xin[c * BLK:(c + 1) * BLK, 128 * j:128 * (j + 1)]
            cols.append(jnp.where(lo, _dot(wm[2 * j], xs), _dot(wm[2 * j + 1], xs)))
        rows.append(jnp.concatenate(cols, axis=1))
    return jnp.concatenate(rows, axis=0)


def _mixer_out_fwd(attn, gvn, gz, ws, bfull, x, w_out, aon, gon, xan):
    t = x.shape[0]

    def body(a_ref, v_ref, gzu_ref, ws_ref, b_ref, x_ref, w_ref, aon_ref, gon_ref, xan_ref, gm_ref, y_ref, x1_ref, h2_ref):
        a = a_ref[...]
        g = _gelu(gzu_ref[...]) * (_gmlp_mix(ws_ref, v_ref[...], False) + jnp.tile(b_ref[...], (TM // BLK, 1)))
        gm_ref[...] = g
        y = jnp.concatenate([a * _rs(a) * aon_ref[...], g * _rs(g) * gon_ref[...]], axis=1).astype(BF16)
        y_ref[...] = y
        x1 = x_ref[...] + _dot(y, w_ref[...])
        x1_ref[...] = x1
        h2_ref[...] = (x1 * _rs(x1) * xan_ref[...]).astype(BF16)

    return pl.pallas_call(
        body, name="mixer_out_fwd", grid=(t // TM,),
        in_specs=[_rows(TM, AW), _rows(TM, GW), _rows(TM, GW), _const((8, BLK, BLK)), _const((BLK, GW)), _rows(TM, D),
                  _const((D, D)), _const((1, AW)), _const((1, GW)), _const((1, D))],
        out_specs=[_rows(TM, GW), _rows(TM, D), _rows(TM, D), _rows(TM, D)],
        out_shape=[_sds((t, GW)), _sds((t, D), BF16), _sds((t, D)), _sds((t, D), BF16)],
        compiler_params=_params())(attn, gvn, gz, ws, bfull, x, w_out, aon, gon, xan)


def _mem_kv_fwd(mem, mem_norm, wkv, kn4):
    def body(m_ref, g_ref, w_ref, kn_ref, mh_ref, kpre_ref, k_ref, v_ref):
        m = m_ref[...]
        mh = (m * _rs(m) * g_ref[...]).astype(BF16)
        mh_ref[...] = mh
        for h in range(XH):
            sl = slice(XD * h, XD * (h + 1))
            kh = _dot(mh, w_ref[h])
            kpre_ref[:, sl] = kh
            k_ref[:, sl] = (kh * _rs(kh) * kn_ref[:, sl]).astype(BF16)
            v_ref[:, sl] = _dot(mh, w_ref[XH + h]).astype(BF16)

    return pl.pallas_call(
        body, name="mem_kv_fwd",
        out_shape=[_sds((MEM, D), BF16), _sds((MEM, D)), _sds((MEM, D), BF16), _sds((MEM, D), BF16)],
        compiler_params=pltpu.CompilerParams(vmem_limit_bytes=VMEM_LIMIT))(mem, mem_norm, wkv, kn4)


def _xattn_probs(qpre_h, qn_h, k_h):
    rq = _rs(qpre_h)
    q2 = (qpre_h * rq * qn_h).astype(BF16)
    s = _dot_nt(q2, k_h) * (1.0 / math.sqrt(XD))
    p = jnp.exp(s - jnp.max(s, axis=-1, keepdims=True))
    return p * (1.0 / jnp.sum(p, axis=-1, keepdims=True)), q2, rq


def _xattn_fwd(h2, x1, wq, qn4, k2, v2, wo, ffn_norm):
    t = x1.shape[0]

    def body(h_ref, x_ref, wq_ref, qn_ref, k_ref, v_ref, wo_ref, fn_ref, qpre_ref, o_ref, x2_ref, h3_ref):
        qpre = _dot(h_ref[...], wq_ref[...])
        qpre_ref[...] = qpre
        outs = []
        for h in range(XH):
            sl = slice(XD * h, XD * (h + 1))
            pn, _, _ = _xattn_probs(qpre[:, sl], qn_ref[:, sl], k_ref[:, sl])
            outs.append(_dot(pn.astype(BF16), v_ref[:, sl]))
        o = jnp.concatenate(outs, axis=1).astype(BF16)
        o_ref[...] = o
        x2 = x_ref[...] + _dot(o, wo_ref[...])
        x2_ref[...] = x2
        h3_ref[...] = (x2 * _rs(x2) * fn_ref[...]).astype(BF16)

    return pl.pallas_call(
        body, name="xattn_fwd", grid=(t // TM,),
        in_specs=[_rows(TM, D), _rows(TM, D), _const((D, D)), _const((1, D)), _const((MEM, D)), _const((MEM, D)),
                  _const((D, D)), _const((1, D))],
        out_specs=[_rows(TM, D)] * 4,
        out_shape=[_sds((t, D)), _sds((t, D), BF16), _sds((t, D)), _sds((t, D), BF16)],
        compiler_params=_params())(h2, x1, wq, qn4, k2, v2, wo, ffn_norm)


SW = 704
NG = FF // SW
FM = 256


def _resident(shape):
    nd = len(shape)
    return pl.BlockSpec(shape, lambda *_: (0,) * nd, pipeline_mode=pl.Buffered(1))


def _conv(e, w):
    return w[2:3, :] * e + pltpu.roll(w[1:2, :] * e + pltpu.roll(w[0:1, :] * e, 1, 0), 1, 0)


def _ffn_fwd(h3, x2, target, up, conv, conv_b, down):
    t = x2.shape[0]

    def body(h_ref, x_ref, t_ref, up_ref, w_ref, b_ref, dn_ref, a_ref, u_ref, gs_ref, dy_ref, loss_ref, acc_ref, tail_ref):
        i = pl.program_id(0)

        @pl.when(i == 0)
        def _():
            acc_ref[...] = jnp.zeros_like(acc_ref)
            tail_ref[...] = jnp.zeros_like(tail_ref)

        h = h_ref[...]
        err = x_ref[...] - t_ref[...]
        for d in range(NG):
            c = []
            for s in range(2):
                j = s * NG + d
                a = _dot_nt(h, up_ref[j])
                a_ref[j] = a.astype(BF16)
                c.append(_conv(jnp.concatenate([tail_ref[j], a], axis=0), w_ref[s, d])[8:] + b_ref[s, d])
                tail_ref[j] = a[FM - 8:FM]
            gl, gg = _gelu_and_grad(c[0])
            gs_ref[d] = gl.astype(BF16)
            gs_ref[NG + d] = (gg * c[1]).astype(BF16)
            u = (gl * c[1]).astype(BF16)
            u_ref[d] = u
            err = err + _dot(u, dn_ref[d])
        dy_ref[...] = err * (1.0 / D)
        acc_ref[...] += jnp.sum(err * err, axis=0, keepdims=True)

        @pl.when(i == pl.num_programs(0) - 1)
        def _():
            loss_ref[...] = jnp.full((8, 128), 0.5 / D, F32) * jnp.sum(acc_ref[...])

    return pl.pallas_call(
        body, name="ffn_fwd", grid=(t // FM,),
        in_specs=[_rows(FM, D), _rows(FM, D), _rows(FM, D),
                  _resident((NDEV, SW, D)), _resident((2, NG, 3, SW)), _resident((2, NG, 1, SW)), _resident((NG, SW, D))],
        out_specs=[pl.BlockSpec((NDEV, FM, SW), lambda i: (0, i, 0)), pl.BlockSpec((NG, FM, SW), lambda i: (0, i, 0)),
                   pl.BlockSpec((NDEV, FM, SW), lambda i: (0, i, 0)), _rows(FM, D), _const((8, 128))],
        out_shape=[_sds((NDEV, t, SW), BF16), _sds((NG, t, SW), BF16), _sds((NDEV, t, SW), BF16), _sds((t, D)),
                   _sds((8, 128))],
        scratch_shapes=[pltpu.VMEM((1, D), F32), pltpu.VMEM((NDEV, 8, SW), F32)],
        compiler_params=_params())(h3, x2, target, up, conv, conv_b, down)


def _gelu_and_grad(z):
    z2 = z * z
    t = jnp.tanh(GELU_C0 * (z + GELU_C1 * z * z2))
    phi = 0.5 * (1.0 + t)
    return z * phi, phi + z * (1.0 - t * t) * (0.5 * GELU_C0 + (1.5 * GELU_C0 * GELU_C1) * z2)


def _ffn_bwd(dy, a, gs, x2, up, conv, down, ffn_norm):
    t = x2.shape[0]
    nt = t // FM
    n = FM + 8

    def body(dy_ref, a_ref, gs_ref, x_ref, up_ref, w_ref, dn_ref, g_ref, dx_ref, da_ref, s_ref, dfn_ref, head_ref):
        @pl.when(pl.program_id(0) == 0)
        def _():
            s_ref[...] = jnp.zeros_like(s_ref)
            dfn_ref[...] = jnp.zeros_like(dfn_ref)
            head_ref[...] = jnp.zeros_like(head_ref)

        dy = dy_ref[...]
        dyb = dy.astype(BF16)
        dh = jnp.zeros((FM, D), F32)
        row = lax.broadcasted_iota(jnp.int32, (8, SW), 0)
        for d in range(NG):
            du = _dot_nt(dyb, dn_ref[d])
            for s in range(2):
                j = s * NG + d
                k = NG + d if s == 0 else d
                dc0 = du * gs_ref[k].astype(F32)
                dc = jnp.concatenate([dc0, head_ref[j]], axis=0)
                head_ref[j] = dc0[0:8]
                w = w_ref[s, d]
                tile = a_ref[j].astype(F32)
                d1 = pltpu.roll(dc, n - 1, 0)
                d2 = pltpu.roll(d1, n - 1, 0)
                da = (w[2:3, :] * dc + w[1:2, :] * d1 + w[0:1, :] * d2)[0:FM].astype(BF16)
                da_ref[j] = da
                dh = dh + _dot(da, up_ref[j])
                sums = [jnp.sum(v[0:FM] * tile, axis=0, keepdims=True) for v in (d2, d1, dc)]
                sums.append(jnp.sum(dc[0:FM], axis=0, keepdims=True))
                upd = jnp.zeros((8, SW), F32)
                for r, v in enumerate(sums):
                    upd = jnp.where(row == r, jnp.broadcast_to(v, (8, SW)), upd)
                s_ref[s, d] += upd
        x = x_ref[...]
        dx, dg = _rms_bwd(dh, x, _rs(x), g_ref[...])
        dx_ref[...] = dy + dx
        dfn_ref[...] += _colsum8(dg)

    rows = pl.BlockSpec((FM, D), lambda i: (nt - 1 - i, 0))
    blocks = pl.BlockSpec((NDEV, FM, SW), lambda i: (0, nt - 1 - i, 0))
    return pl.pallas_call(
        body, name="ffn_bwd", grid=(nt,),
        in_specs=[rows, blocks, blocks, rows, _resident((NDEV, SW, D)), _resident((2, NG, 3, SW)), _resident((NG, SW, D)),
                  _const((1, D))],
        out_specs=[rows, blocks, _const((2, NG, 8, SW)), _const((8, D))],
        out_shape=[_sds((t, D)), _sds((NDEV, t, SW), BF16), _sds((2, NG, 8, SW)), _sds((8, D))],
        scratch_shapes=[pltpu.VMEM((NDEV, 8, SW), F32)],
        compiler_params=_params())(dy, a, gs, x2, up, conv, down, ffn_norm)


BT = 512


def _xattn_bwd(dx2, x1, qpre, k2, v2, wq, wo, qn4, xan):
    t = x1.shape[0]

    def body(dx2_ref, x1_ref, qpre_ref, k_ref, v_ref, wq_ref, wo_ref, qn_ref, xan_ref,
             dx1_ref, dqpre_ref, dk_ref, dv_ref, dqn_ref, dxan_ref):
        @pl.when(pl.program_id(0) == 0)
        def _():
            for r in (dk_ref, dv_ref, dqn_ref, dxan_ref):
                r[...] = jnp.zeros_like(r)

        dx2 = dx2_ref[...]
        do = _dot_nt(dx2.astype(BF16), wo_ref[...])
        dqs = []
        for h in range(XH):
            sl = slice(XD * h, XD * (h + 1))
            qpre_h = qpre_ref[:, sl]
            pn, q2, rq = _xattn_probs(qpre_h, qn_ref[:, sl], k_ref[:, sl])
            do_h = do[:, sl].astype(BF16)
            dp = _dot_nt(do_h, v_ref[:, sl])
            ds = (pn * (dp - jnp.sum(pn * dp, axis=-1, keepdims=True)) * (1.0 / math.sqrt(XD))).astype(BF16)
            dq2 = _dot(ds, k_ref[:, sl])
            dk_ref[:, sl] += _dot_tn(ds, q2)
            dv_ref[:, sl] += _dot_tn(pn.astype(BF16), do_h)
            dqh, dg = _rms_bwd(dq2, qpre_h, rq, qn_ref[:, sl])
            dqn_ref[...] += _colsum8(dg)
            dqs.append(dqh)
        dqpre = jnp.concatenate(dqs, axis=1).astype(BF16)
        dqpre_ref[...] = dqpre
        dh2 = _dot_nt(dqpre, wq_ref[...])
        x1 = x1_ref[...]
        dx, dg = _rms_bwd(dh2, x1, _rs(x1), xan_ref[...])
        dx1_ref[...] = dx2 + dx
        dxan_ref[...] += _colsum8(dg)

    return pl.pallas_call(
        body, name="xattn_bwd", grid=(t // BT,),
        in_specs=[_rows(BT, D), _rows(BT, D), _rows(BT, D), _const((MEM, D)), _const((MEM, D)), _const((D, D)),
                  _const((D, D)), _const((1, D)), _const((1, D))],
        out_specs=[_rows(BT, D), _rows(BT, D), _const((MEM, D)), _const((MEM, D)), _const((8, XD)), _const((8, D))],
        out_shape=[_sds((t, D)), _sds((t, D), BF16), _sds((MEM, D)), _sds((MEM, D)), _sds((8, XD)), _sds((8, D))],
        compiler_params=_params())(dx2, x1, qpre, k2, v2, wq, wo, qn4, xan)


def _mem_kv_bwd(mem, mh, kpre, dk2, dv2, wkv, kn4, mem_norm):
    def body(m_ref, mh_ref, kpre_ref, dk_ref, dv_ref, w_ref, kn_ref, g_ref, dw_ref, dkn_ref, dmn_ref):
        dkn = jnp.zeros((8, XD), F32)
        dm = jnp.zeros((MEM, D), F32)
        mh = mh_ref[...]
        for h in range(XH):
            sl = slice(XD * h, XD * (h + 1))
            kh = kpre_ref[:, sl]
            dkh, dg = _rms_bwd(dk_ref[:, sl], kh, _rs(kh), kn_ref[:, sl])
            dkn = dkn + _colsum8(dg)
            dkh = dkh.astype(BF16)
            dvh = dv_ref[:, sl].astype(BF16)
            dw_ref[h] = _dot_tn(mh, dkh).astype(BF16)
            dw_ref[XH + h] = _dot_tn(mh, dvh).astype(BF16)
            dm = dm + _dot_nt(dkh, w_ref[h]) + _dot_nt(dvh, w_ref[XH + h])
        dkn_ref[...] = dkn
        m = m_ref[...]
        _, dg = _rms_bwd(dm, m, _rs(m), g_ref[...])
        dmn_ref[...] = _colsum8(dg)

    return pl.pallas_call(
        body, name="mem_kv_bwd", out_shape=[_sds((2 * XH, D, XD), BF16), _sds((8, XD)), _sds((8, D))],
        compiler_params=pltpu.CompilerParams(vmem_limit_bytes=VMEM_LIMIT))(mem, mh, kpre, dk2, dv2, wkv, kn4, mem_norm)


def _mixer_out_bwd(dx1, attn, gm, w_out, aon, gon, gvn, gz, ws, wst, bfull, gvw):
    t = dx1.shape[0]
    nc = TM // BLK

    def body(dx_ref, a_ref, g_ref, wo_ref, aon_ref, gon_ref, x_ref, gz_ref, w_ref, wt_ref, b_ref, gvw_ref,
             da_ref, dan_ref, dgn_ref, dgz_ref, dw_ref, db_ref, dgvw_ref, dbacc_ref):
        @pl.when(pl.program_id(0) == 0)
        def _():
            for r in (dan_ref, dgn_ref, dw_ref, dbacc_ref, dgvw_ref):
                r[...] = jnp.zeros_like(r)

        dy = _dot_nt(dx_ref[...].astype(BF16), wo_ref[...])
        a, g = a_ref[...], g_ref[...]
        da, dna = _rms_bwd(dy[:, :AW], a, _rs(a), aon_ref[...])
        dgm, dng = _rms_bwd(dy[:, AW:], g, _rs(g), gon_ref[...])
        da_ref[...] = da
        dan_ref[...] += _colsum8(dna)
        dgn_ref[...] += _colsum8(dng)

        xin = x_ref[...]
        mixed = _gmlp_mix(w_ref, xin, False) + jnp.tile(b_ref[...], (nc, 1))
        dgu = dgm * mixed
        dmixed = dgm * _gelu(gz_ref[:, :GW])
        lo = _lane((BLK, 128)) < 64
        dbias = jnp.zeros((BLK, GW), F32)
        for c in range(nc):
            dmc = dmixed[c * BLK:(c + 1) * BLK]
            dbias = dbias + dmc
            for j in range(4):
                dm2 = dmc[:, 128 * j:128 * (j + 1)]
                xs = xin[c * BLK:(c + 1) * BLK, 128 * j:128 * (j + 1)]
                z = jnp.zeros_like(dm2)
                dw_ref[2 * j] += _dot_nt(jnp.where(lo, dm2, z).astype(BF16), xs)
                dw_ref[2 * j + 1] += _dot_nt(jnp.where(lo, z, dm2).astype(BF16), xs)
        dbacc_ref[...] += dbias
        dgvn = _gmlp_mix(wt_ref, dmixed.astype(BF16), True)
        gz_u, gz_v = gz_ref[:, :GW], gz_ref[:, GW:]
        gv = _gelu(gz_v)
        dgv, dg = _rms_bwd(dgvn, gv, _rs(gv), gvw_ref[...])
        dgvw_ref[...] += _colsum8(dg)
        dgz_ref[:, :GW] = (dgu * _gelu_grad(gz_u)).astype(BF16)
        dgz_ref[:, GW:] = (dgv * _gelu_grad(gz_v)).astype(BF16)

        @pl.when(pl.program_id(0) == pl.num_programs(0) - 1)
        def _():
            s = dbacc_ref[...]
            sel = (lax.broadcasted_iota(jnp.int32, (8, GW), 1) // HD
                   == lax.broadcasted_iota(jnp.int32, (8, GW), 0)).astype(BF16)
            hi = s.astype(BF16)
            r1 = s - hi.astype(F32)
            mid = r1.astype(BF16)
            lo = (r1 - mid.astype(F32)).astype(BF16)
            db_ref[...] = _dot_nt(sel, hi) + _dot_nt(sel, mid) + _dot_nt(sel, lo)
            r = lax.broadcasted_iota(jnp.int32, (BLK, BLK), 0)
            c = lax.broadcasted_iota(jnp.int32, (BLK, BLK), 1)
            for h in range(8):
                dw_ref[h] = jnp.where(c <= r, dw_ref[h], 0.0)

    return pl.pallas_call(
        body, name="mixer_out_bwd", grid=(t // TM,),
        in_specs=[_rows(TM, D), _rows(TM, AW), _rows(TM, GW), _const((D, D)), _const((1, AW)), _const((1, GW)),
                  _rows(TM, GW), _rows(TM, 2 * GW), _const((8, BLK, BLK)), _const((8, BLK, BLK)),
                  _const((BLK, GW)), _const((1, GW))],
        out_specs=[_rows(TM, AW), _const((8, AW)), _const((8, GW)), _rows(TM, 2 * GW), _const((8, BLK, BLK)),
                   _const((8, BLK)), _const((8, GW))],
        out_shape=[_sds((t, AW)), _sds((8, AW)), _sds((8, GW)), _sds((t, 2 * GW), BF16), _sds((8, BLK, BLK)),
                   _sds((8, BLK)), _sds((8, GW))],
        scratch_shapes=[pltpu.VMEM((BLK, GW), F32)],
        compiler_params=_params())(dx1, attn, gm, w_out, aon, gon, gvn, gz, ws, wst, bfull, gvw)


def _fold_half(v):
    return v + pltpu.roll(v, 64, 1)


def _swa_bwd(q, k, v, dattn, sinks):
    t = q.shape[0]
    nb = t // BLK
    ts = min(t, SB_BWD * BLK)
    sb = ts // BLK
    nt = t // ts

    def body(sink_ref, q_ref, kc_ref, kp_ref, vc_ref, vp_ref, do_ref, dq_ref, dk_ref, dv_ref, ds_ref,
             ck_ref, cv_ref, sacc_ref):
        i = pl.program_id(0)

        @pl.when(i == 0)
        def _():
            ck_ref[...] = jnp.zeros_like(ck_ref)
            cv_ref[...] = jnp.zeros_like(cv_ref)
            sacc_ref[...] = jnp.zeros_like(sacc_ref)

        @pl.when(i < nt)
        def _():
            kk = jnp.concatenate([kp_ref[...], kc_ref[...]], axis=0).astype(F32)
            vv = jnp.concatenate([vp_ref[...], vc_ref[...]], axis=0).astype(F32)
            lo256 = _lane((2 * BLK, 128)) < 64
            acc_k = [jnp.zeros((BLK, 128), F32) for _ in range(sb + 1)]
            acc_v = [jnp.zeros((BLK, 128), F32) for _ in range(sb + 1)]
            for b in range(sb):
                r = slice(b * BLK, (b + 1) * BLK)
                kb, vb = kk[b * BLK:(b + 2) * BLK], vv[b * BLK:(b + 2) * BLK]
                mask = _swa_mask(i == 0) if b == 0 else _swa_mask(False)
                dkk = jnp.zeros((2 * BLK, 128), F32)
                dvv = jnp.zeros((2 * BLK, 128), F32)
                for g in range(2):
                    qs = _stack_heads(q_ref[r, 256 * g:256 * g + 128], q_ref[r, 256 * g + 128:256 * g + 256])
                    dos = _stack_heads(do_ref[r, 256 * g:256 * g + 128],
                                       do_ref[r, 256 * g + 128:256 * g + 256]).astype(BF16)
                    kd = _dup_half(kb, g)
                    pn, psn = _swa_probs(qs, kd, mask, _sink_col(sink_ref, g))
                    dp = _dot_nt(dos, _dup_half(vb, g))
                    dd = jnp.sum(pn * dp, axis=-1, keepdims=True)
                    ds = (pn * (dp - dd) * (1.0 / math.sqrt(HD))).astype(BF16)
                    sacc_ref[g] += jnp.broadcast_to(-psn * dd, (4 * BLK, 128))
                    dqa, dqb = _unstack_heads(_dot(ds, kd))
                    dq_ref[r, 256 * g:256 * g + 128] = dqa
                    dq_ref[r, 256 * g + 128:256 * g + 256] = dqb
                    dkg = _fold_half(_dot_tn(ds, qs))
                    dvg = _fold_half(_dot_tn(pn.astype(BF16), dos))
                    keep = lo256 if g == 0 else jnp.logical_not(lo256)
                    dkk = jnp.where(keep, dkg, dkk)
                    dvv = jnp.where(keep, dvg, dvv)
                acc_k[b], acc_k[b + 1] = acc_k[b] + dkk[0:BLK], acc_k[b + 1] + dkk[BLK:]
                acc_v[b], acc_v[b + 1] = acc_v[b] + dvv[0:BLK], acc_v[b + 1] + dvv[BLK:]
            for out_ref, c_ref, acc in ((dk_ref, ck_ref, acc_k), (dv_ref, cv_ref, acc_v)):
                if sb > 1:
                    out_ref[0:ts - BLK] = c_ref[0:ts - BLK]
                out_ref[ts - BLK:ts] = c_ref[ts - BLK:ts] + acc[0]
                for b in range(sb):
                    c_ref[b * BLK:(b + 1) * BLK] = acc[b + 1]

        @pl.when(i == nt)
        def _():
            dk_ref[...] = ck_ref[...]
            dv_ref[...] = cv_ref[...]
            lane = _lane((8, 128))
            acc = jnp.zeros((8, 128), F32)
            for g in range(2):
                for j in range(4):
                    val = jnp.sum(sacc_ref[g, j * BLK:(j + 1) * BLK, :], axis=0, keepdims=True)
                    acc = jnp.where(lane == 4 * g + j, jnp.broadcast_to(val, (8, 128)), acc)
            ds_ref[...] = acc

    cur = lambda i: (jnp.minimum(i, nt - 1), 0)
    before = lambda i: (jnp.clip(i * sb - 1, 0, nb - 1), 0)
    done = lambda i: (jnp.clip(i - 1, 0, nt - 1), 0)
    return pl.pallas_call(
        body, name="swa_bwd", grid=(nt + 1,),
        in_specs=[pl.BlockSpec(memory_space=pltpu.SMEM), pl.BlockSpec((ts, AW), cur),
                  pl.BlockSpec((ts, KW), cur), pl.BlockSpec((BLK, KW), before),
                  pl.BlockSpec((ts, KW), cur), pl.BlockSpec((BLK, KW), before), pl.BlockSpec((ts, AW), cur)],
        out_specs=[pl.BlockSpec((ts, AW), cur), pl.BlockSpec((ts, KW), done), pl.BlockSpec((ts, KW), done),
                   _const((8, 128))],
        out_shape=[_sds((t, AW)), _sds((t, KW)), _sds((t, KW)), _sds((8, 128))],
        scratch_shapes=[pltpu.VMEM((ts, KW), F32), pltpu.VMEM((ts, KW), F32), pltpu.VMEM((2, 4 * BLK, 128), F32)],
        compiler_params=_params())(sinks, q, k, k, v, v, dattn)


def _mixer_in_bwd(dq, dk, dv, dgz, qk, cos, sin, x, dx1, w_in, mix_norm, qn, kn):
    t = x.shape[0]

    def body(dq_ref, dk_ref, dv_ref, dgz_ref, qk_ref, cos_ref, sin_ref, x_ref, dx1_ref, w_ref, g_ref, qn_ref, kn_ref,
             ones_ref, gx_ref, dproj_ref, dmn_ref, dqn_ref, dkn_ref, qacc_ref, kacc_ref, xbuf_ref, dbuf_ref, ring_sems):
        i = pl.program_id(0)
        nt = t // TM

        def fetch(s, slot):
            rows = pl.ds(pl.multiple_of(s * TM, TM), TM)
            return (pltpu.make_async_copy(x_ref.at[rows], xbuf_ref.at[slot], ring_sems.at[0, slot]),
                    pltpu.make_async_copy(dx1_ref.at[rows], dbuf_ref.at[slot], ring_sems.at[1, slot]))

        @pl.when(i == 0)
        def _():
            for s in range(min(3, nt)):
                for cp in fetch(s, s):
                    cp.start()

        slot = i % 3
        for cp in fetch(i, slot):
            cp.wait()

        @pl.when(i == 0)
        def _():
            dmn_ref[...] = jnp.zeros_like(dmn_ref)
            qacc_ref[...] = jnp.zeros_like(qacc_ref)
            kacc_ref[...] = jnp.zeros_like(kacc_ref)

        cos2, sin2 = cos_ref[...], sin_ref[...]
        qpre, kpre = qk_ref[:, :AW], qk_ref[:, AW:]
        dqh = _rope_bwd(dq_ref[...], jnp.tile(cos2, (1, 4)), jnp.tile(sin2, (1, 4)))
        dqpre, dgq = _rms64_bwd(dqh, qpre, _rs64(qpre, ones_ref), qn_ref[...], ones_ref)
        dkh = _rope_bwd(dk_ref[...], cos2, sin2)
        dkpre, dgk = _rms64_bwd(dkh, kpre, _rs64(kpre, ones_ref), kn_ref[...], ones_ref)
        qacc_ref[...] += jnp.sum(dgq, axis=0, keepdims=True)
        kacc_ref[...] += jnp.sum(dgk, axis=0, keepdims=True)
        dproj = jnp.concatenate([dqpre.astype(BF16), dkpre.astype(BF16), dv_ref[...].astype(BF16), dgz_ref[...]], axis=1)
        dproj_ref[...] = dproj
        dh = _dot(dproj, w_ref[...])
        xv = xbuf_ref[slot]
        dx, dg = _rms_bwd(dh, xv, _rs(xv), g_ref[...])
        gx_ref[...] = dbuf_ref[slot] + dx
        dmn_ref[...] += _colsum8(dg)

        @pl.when(i + 3 < nt)
        def _():
            for cp in fetch(i + 3, slot):
                cp.start()

        @pl.when(i == pl.num_programs(0) - 1)
        def _():
            qa = qacc_ref[...]
            q4 = qa[:, 0:128] + qa[:, 128:256] + qa[:, 256:384] + qa[:, 384:512]
            dqn_ref[...] = jnp.broadcast_to(_fold_half(q4), (8, 128))
            dkn_ref[...] = jnp.broadcast_to(_fold_half(kacc_ref[...]), (8, 128))

    return pl.pallas_call(
        body, name="mixer_in_bwd", grid=(t // TM,),
        in_specs=[_rows(TM, AW), _rows(TM, KW), _rows(TM, KW), _rows(TM, 2 * GW), _rows(TM, AW + KW), _rows(TM, 128),
                  _rows(TM, 128), pl.BlockSpec(memory_space=pl.ANY), pl.BlockSpec(memory_space=pl.ANY), _const((IN, D)),
                  _const((1, D)), _const((1, AW)), _const((1, KW)), _const((AW, AW))],
        out_specs=[_rows(TM, D), _rows(TM, IN), _const((8, D)), _const((8, 128)), _const((8, 128))],
        out_shape=[_sds((t, D)), _sds((t, IN), BF16), _sds((8, D)), _sds((8, 128)), _sds((8, 128))],
        scratch_shapes=[pltpu.VMEM((1, AW), F32), pltpu.VMEM((1, KW), F32), pltpu.VMEM((3, TM, D), F32),
                        pltpu.VMEM((3, TM, D), F32), pltpu.SemaphoreType.DMA((2, 3))],
        compiler_params=_params())(dq, dk, dv, dgz, qk, cos, sin, x, dx1, w_in, mix_norm, qn, kn, _head_ones())


def _local_step(x, mem, pos, target, p, fetch, ship):
    t = x.shape[0]
    p = dict(p)
    p.update(fetch(0, None))
    inv_freq = 1.0 / (ROPE_THETA ** (jnp.arange(HD // 2, dtype=F32) * (2.0 / HD)))
    cos, sin = _rope_tables(pos, jnp.tile(inv_freq, 4).reshape(1, 128))
    qn = jnp.tile(p["q_norm"], (1, AW // HD))
    kn = jnp.tile(p["k_norm"], (1, KW // HD))
    qn4 = jnp.tile(p["xa_q_norm"], (1, XH))
    kn4 = jnp.tile(p["xa_k_norm"], (1, XH))
    ws = p["gmlp_ws"]
    wst = jnp.swapaxes(ws, 1, 2)
    bfull = jnp.repeat(p["gmlp_bs"].T, HD, axis=1)
    conv_b = p["ffn_conv_b"]

    h1, qk, gz, q, k, v, gvn = _mixer_in_fwd(x, p["mix_norm"], p["w_in"], qn, kn, p["gmlp_v_norm"], cos, sin)
    attn = _swa_fwd(q, k, v, p["attn_sinks"])
    p.update(fetch(1, attn))
    gm, ycat, x1, h2 = _mixer_out_fwd(attn, gvn, gz, ws, bfull, x, p["w_out"], p["attn_out_norm"], p["gmlp_out_norm"],
                                      p["xa_norm"])
    mh, kpre, k2, v2 = _mem_kv_fwd(mem, p["mem_norm"], p["xa_wkv"], kn4)
    qpre, o, x2, h3 = _xattn_fwd(h2, x1, p["xa_wq"], qn4, k2, v2, p["xa_wo"], p["ffn_norm"])
    p.update(fetch(2, h3))
    conv = p["ffn_conv"]
    a, u, gs, dy, loss8 = _ffn_fwd(h3, x2, target, p["ffn_up"], conv, conv_b, p["ffn_down"])

    raw = {}
    d_down = _mm_tn(u, dy, "ffn_down_bwd_w")
    dx2, da, raw["conv_sums"], raw["ffn_norm"] = _ffn_bwd(dy, a, gs, x2, p["ffn_up"], conv, p["ffn_down"], p["ffn_norm"])
    d_up = _mm_tn(da, h3, "ffn_up_bwd_w")
    token = ship(0, {"ffn_down": d_down, "ffn_up": d_up, "ffn_conv": raw["conv_sums"][:, :, 0:3]})
    dx1, dqpre, dk2, dv2, raw["xa_q_norm"], raw["xa_norm"] = _xattn_bwd(
        dx2, x1, qpre, k2, v2, p["xa_wq"], p["xa_wo"], qn4 + jnp.tile(token[0:1], (1, D // 128)), p["xa_norm"])
    d_wo = _mm_tn(o, dx2, "xa_wo_bwd_w")
    d_wq = _mm_tn(h2, dqpre, "xa_wq_bwd_w")
    d_wkv, raw["xa_k_norm"], raw["mem_norm"] = _mem_kv_bwd(mem, mh, kpre, dk2, dv2, p["xa_wkv"], kn4, p["mem_norm"])
    d_w_out = _mm_tn(ycat, dx1, "w_out_bwd_w")
    (dattn, raw["attn_out_norm"], raw["gmlp_out_norm"], dgz, raw["gmlp_ws"], raw["gmlp_bs"],
     raw["gmlp_v_norm"]) = _mixer_out_bwd(dx1, attn, gm, p["w_out"], p["attn_out_norm"], p["gmlp_out_norm"],
                                          gvn, gz, ws, wst, bfull, p["gmlp_v_norm"])
    token = ship(1, {"xa_wo": d_wo, "xa_wq": d_wq, "xa_wkv": d_wkv, "w_out": d_w_out}, [raw["gmlp_ws"]])
    dq, dk, dv, raw["attn_sinks"] = _swa_bwd(q, k, v, dattn, p["attn_sinks"] + token[0:1, 0:8])
    grad_x, dproj, raw["mix_norm"], raw["q_norm"], raw["k_norm"] = _mixer_in_bwd(
        dq, dk, dv, dgz, qk, cos, sin, x, dx1, p["w_in"], p["mix_norm"], qn, kn)
    d_w_in = _mm_tn(dproj, h1, "w_in_bwd_w")
    raw["loss"] = loss8
    return grad_x, {"w_in": d_w_in}, raw


def _cast_shards(shards):
    def body(*refs):
        n = len(refs) // 2
        for i_ref, o_ref in zip(refs[:n], refs[n:]):
            o_ref[...] = i_ref[...].astype(BF16)

    return pl.pallas_call(body, name="cast_shards", out_shape=[_sds(s.shape, BF16) for s in shards],
                          compiler_params=pltpu.CompilerParams(vmem_limit_bytes=VMEM_LIMIT))(*shards)


HBM_SPEC = pl.BlockSpec(memory_space=pltpu.HBM)
SEM_SPEC = pl.BlockSpec(memory_space=pltpu.SEMAPHORE)


ALL_K = tuple(range(1, NDEV))
CHIP_K = (1, 2, 4, 6)
RELAY_K = (2, 4, 6)


def _peer(k):
    x, y, cc = lax.axis_index("x"), lax.axis_index("y"), lax.axis_index("c")
    return 1 - x if k & 4 else x, 1 - y if k & 2 else y, 1 - cc if k & 1 else cc


def _remote_copies(src_refs, land_refs, send_refs, recv_refs, nd, ks):
    me = 4 * lax.axis_index("x") + 2 * lax.axis_index("y") + lax.axis_index("c")
    copies = []
    for a, (src_ref, land_ref) in enumerate(zip(src_refs, land_refs)):
        for j, k in enumerate(ks):
            px, py, pc = _peer(k)
            copies.append((k, pltpu.make_async_remote_copy(
                src_ref=src_ref.at[4 * px + 2 * py + pc] if a < nd else src_ref, dst_ref=land_ref.at[me],
                send_sem=send_refs[a].at[j], recv_sem=recv_refs[a].at[j],
                device_id=(px, py, pc), device_id_type=pl.DeviceIdType.MESH)))
    return copies


def _relay_copies(land_refs, send_refs, recv_refs):
    copies = []
    for a, land_ref in enumerate(land_refs):
        for j, k in enumerate(RELAY_K):
            px, py, pc = _peer(k)
            slot = land_ref.at[4 * px + 2 * py + pc]
            copies.append(pltpu.make_async_remote_copy(
                src_ref=slot, dst_ref=slot, send_sem=send_refs[a].at[j], recv_sem=recv_refs[a].at[j],
                device_id=_peer(1), device_id_type=pl.DeviceIdType.MESH))
    return copies


def _own_slot(src, by_dest, me):
    block = lax.dynamic_index_in_dim(src, me, 0, keepdims=True) if by_dest else src[None]
    return lax.dynamic_update_index_in_dim(lax.empty((NDEV,) + block.shape[1:], src.dtype), block, me, 0)


SIDE_EFFECT = pltpu.CompilerParams(has_side_effects=pltpu.SideEffectType.DATAFLOW_SIDE_EFFECTING)


def _exchange_start(by_dest, for_all, me, name, ks=ALL_K):
    srcs = list(by_dest) + list(for_all)
    n, nd = len(srcs), len(by_dest)
    lands = [_own_slot(s, a < nd, me) for a, s in enumerate(srcs)]

    def body(*refs):
        for _, cp in _remote_copies(refs[:n], refs[n:2 * n], refs[2 * n:3 * n], refs[3 * n:4 * n], nd, ks):
            cp.start()
        refs[-1][...] = jnp.zeros((8, 128), F32)

    sems = [pltpu.SemaphoreType.DMA((len(ks),))] * (2 * n)
    thru = [pltpu.HBM(v.shape, v.dtype) for v in srcs + lands]
    res = pl.pallas_call(
        body, name=name, out_shape=sems + thru + [_sds((8, 128))],
        in_specs=[HBM_SPEC] * (2 * n), out_specs=[SEM_SPEC] * (2 * n) + [HBM_SPEC] * (2 * n) + [pl.BlockSpec(memory_space=pltpu.VMEM)],
        input_output_aliases={i: 2 * n + i for i in range(2 * n)}, compiler_params=SIDE_EFFECT)(
            *[pltpu.with_memory_space_constraint(v, pltpu.HBM) for v in srcs + lands])
    return (res[:2 * n], res[2 * n:4 * n], nd, ks, None), res[-1]


def _exchange_relay(state, after, name):
    sems, thru, nd, ks, _ = state
    n = len(thru) // 2

    def body(*refs):
        for k, cp in _remote_copies(refs[:n], refs[n:2 * n], refs[2 * n:3 * n], refs[3 * n:4 * n], nd, ks):
            if k in RELAY_K:
                cp.wait_recv()
        for cp in _relay_copies(refs[n:2 * n], refs[4 * n + 1:5 * n + 1], refs[5 * n + 1:6 * n + 1]):
            cp.start()

    relay_sems = [pltpu.SemaphoreType.DMA((len(RELAY_K),))] * (2 * n)
    res = pl.pallas_call(
        body, name=name, out_shape=relay_sems + [pltpu.HBM(v.shape, v.dtype) for v in thru],
        in_specs=[HBM_SPEC] * (2 * n) + [SEM_SPEC] * (2 * n) + [pl.BlockSpec(memory_space=pl.ANY)],
        out_specs=[SEM_SPEC] * (2 * n) + [HBM_SPEC] * (2 * n),
        input_output_aliases={i: 2 * n + i for i in range(2 * n)}, compiler_params=SIDE_EFFECT)(*thru, *sems, after)
    return sems, res[2 * n:], nd, ks, res[:2 * n]


def _exchange_wait(state, after, name):
    sems, thru, nd, ks, relay_sems = state
    n = len(thru) // 2

    def body(*refs):
        for k, cp in _remote_copies(refs[:n], refs[n:2 * n], refs[2 * n:3 * n], refs[3 * n:4 * n], nd, ks):
            cp.wait_send()
            if relay_sems is None or k not in RELAY_K:
                cp.wait_recv()
        if relay_sems is not None:
            for cp in _relay_copies(refs[n:2 * n], refs[4 * n:5 * n], refs[5 * n:6 * n]):
                cp.wait_send()
                cp.wait_recv()

    extra = [] if relay_sems is None else list(relay_sems)
    res = pl.pallas_call(
        body, name=name, out_shape=[pltpu.HBM(v.shape, v.dtype) for v in thru],
        in_specs=[HBM_SPEC] * (2 * n) + [SEM_SPEC] * (2 * n + len(extra)) + [pl.BlockSpec(memory_space=pl.ANY)],
        out_specs=[HBM_SPEC] * (2 * n), input_output_aliases={i: i for i in range(2 * n)}, compiler_params=SIDE_EFFECT)(
            *thru, *sems, *extra, after)
    return res[n:]


def _adam(items, name):
    n = len(items)

    def body(*refs):
        for j in range(n):
            p_ref, w_ref, m_ref, v_ref = refs[4 * j:4 * j + 4]
            g_ref, d_ref, nm_ref, nv_ref = refs[4 * n + 4 * j:4 * n + 4 * j + 4]
            g = _sum_parts(p_ref)
            g_ref[...] = g
            d_ref[...], nm_ref[...], nv_ref[...] = _adam_math(g, w_ref[...], m_ref[...], v_ref[...])

    res = pl.pallas_call(
        body, name=name, out_shape=[_sds(it[1].shape) for it in items for _ in range(4)],
        compiler_params=pltpu.CompilerParams(vmem_limit_bytes=VMEM_LIMIT))(*[a for it in items for a in it])
    return [res[4 * j:4 * j + 4] for j in range(n)]


GATHER_GROUPS = (("w_in",), ("w_out", "xa_wkv", "xa_wq", "xa_wo"), ("ffn_up", "ffn_conv", "ffn_down"))
SCATTER_GROUPS = (("ffn_up", "ffn_down", "ffn_conv"), ("xa_wo", "xa_wq", "xa_wkv", "w_out"), ("w_in",))
ADAM_ALONE = ("ffn_up",)
BIG = tuple(n for grp in GATHER_GROUPS for n in grp)
BY_COLUMN = ("w_in", "ffn_up")
VECS = (("mix_norm", D), ("q_norm", HD), ("k_norm", HD), ("attn_sinks", 8), ("gmlp_v_norm", GW), ("attn_out_norm", AW),
        ("gmlp_out_norm", GW), ("xa_norm", D), ("mem_norm", D), ("xa_q_norm", XD), ("xa_k_norm", XD), ("ffn_norm", D))
BS_ROW = 16
VEC_ROWS = 24
SMALL = tuple(n for n, _ in VECS) + ("gmlp_bs", "gmlp_ws", "ffn_conv_b")


def _pack_small(raw):
    names = [n for n, _ in VECS] + ["gmlp_bs", "conv_sums"]

    def body(*refs):
        ins = dict(zip(names, refs))
        vec_ref, cb_ref = refs[len(names):]
        vec_ref[...] = jnp.zeros_like(vec_ref)
        for r, (n, w) in enumerate(VECS):
            vec_ref[r:r + 1, 0:w] = ins[n][0:1, 0:w]
        vec_ref[BS_ROW:BS_ROW + 8, 0:BLK] = ins["gmlp_bs"][...]
        for s in range(2):
            for d in range(NG):
                cb_ref[s, d] = ins["conv_sums"][s, d, 3:4, :]

    return pl.pallas_call(body, name="pack_small", out_shape=[_sds((VEC_ROWS, D)), _sds((2, NG, 1, SW))])(
        *[raw[n] for n in names])


def _adam_math(g, w, m, v):
    nm = B1 * m + (1.0 - B1) * g
    nv = B2 * v + (1.0 - B2) * (g * g)
    m_hat = nm / (1.0 - B1 ** STEP)
    v_hat = nv / (1.0 - B2 ** STEP)
    return -LR * (m_hat / (jnp.sqrt(v_hat) + AEPS) + WD * w), nm, nv


def _sum_parts(p_ref):
    g = p_ref[0].astype(F32)
    for j in range(1, NDEV):
        g = g + p_ref[j].astype(F32)
    return g


def _adam_small(parts_vec, parts_ws, parts_cb, w, m, v):
    def body(*refs):
        pv_ref, pws_ref, pcb_ref = refs[:3]
        ins = refs[3:3 + 3 * len(SMALL)]
        outs = refs[3 + 3 * len(SMALL):]
        gv = _sum_parts(pv_ref)
        for j, n in enumerate(SMALL):
            w_ref, m_ref, v_ref = ins[3 * j:3 * j + 3]
            o = outs[4 * j:4 * j + 4]
            if n == "gmlp_ws":
                g = _sum_parts(pws_ref)
            elif n == "ffn_conv_b":
                g = _sum_parts(pcb_ref)
            elif n == "gmlp_bs":
                g = gv[BS_ROW:BS_ROW + 8, 0:BLK]
            else:
                g = gv[j:j + 1, 0:VECS[j][1]]
            lead = n in ("gmlp_ws", "gmlp_bs")
            res = (g,) + _adam_math(g, w_ref[0] if lead else w_ref[...], m_ref[0] if lead else m_ref[...],
                                    v_ref[0] if lead else v_ref[...])
            for o_ref, val in zip(o, res):
                if lead:
                    o_ref[0] = val
                else:
                    o_ref[...] = val

    args = [parts_vec, parts_ws, parts_cb] + [d[n] for n in SMALL for d in (w, m, v)]
    res = pl.pallas_call(body, name="adam_small", out_shape=[_sds(w[n].shape) for n in SMALL for _ in range(4)],
                         compiler_params=pltpu.CompilerParams(vmem_limit_bytes=VMEM_LIMIT))(*args)
    return {n: tuple(res[4 * j:4 * j + 4]) for j, n in enumerate(SMALL)}


def kernel(x, mem, positions, mix_norm, w_in, q_norm, k_norm, attn_sinks, gmlp_v_norm, gmlp_ws, gmlp_bs, attn_out_norm, gmlp_out_norm, w_out, xa_norm, mem_norm, xa_wq, xa_wkv, xa_q_norm, xa_k_norm, xa_wo, ffn_norm, ffn_up, ffn_conv, ffn_conv_b, ffn_down, loss_target, m_mix_norm, m_w_in, m_q_norm, m_k_norm, m_attn_sinks, m_gmlp_v_norm, m_gmlp_ws, m_gmlp_bs, m_attn_out_norm, m_gmlp_out_norm, m_w_out, m_xa_norm, m_mem_norm, m_xa_wq, m_xa_wkv, m_xa_q_norm, m_xa_k_norm, m_xa_wo, m_ffn_norm, m_ffn_up, m_ffn_conv, m_ffn_conv_b, m_ffn_down, v_mix_norm, v_w_in, v_q_norm, v_k_norm, v_attn_sinks, v_gmlp_v_norm, v_gmlp_ws, v_gmlp_bs, v_attn_out_norm, v_gmlp_out_norm, v_w_out, v_xa_norm, v_mem_norm, v_xa_wq, v_xa_wkv, v_xa_q_norm, v_xa_k_norm, v_xa_wo, v_ffn_norm, v_ffn_up, v_ffn_conv, v_ffn_conv_b, v_ffn_down):
    names = ("mix_norm", "w_in", "q_norm", "k_norm", "attn_sinks", "gmlp_v_norm", "gmlp_ws", "gmlp_bs", "attn_out_norm",
             "gmlp_out_norm", "w_out", "xa_norm", "mem_norm", "xa_wq", "xa_wkv", "xa_q_norm", "xa_k_norm", "xa_wo",
             "ffn_norm", "ffn_up", "ffn_conv", "ffn_conv_b", "ffn_down")
    w = dict(zip(names, (mix_norm, w_in, q_norm, k_norm, attn_sinks, gmlp_v_norm, gmlp_ws, gmlp_bs, attn_out_norm,
                         gmlp_out_norm, w_out, xa_norm, mem_norm, xa_wq, xa_wkv, xa_q_norm, xa_k_norm, xa_wo, ffn_norm,
                         ffn_up, ffn_conv, ffn_conv_b, ffn_down)))
    m = dict(zip(names, (m_mix_norm, m_w_in, m_q_norm, m_k_norm, m_attn_sinks, m_gmlp_v_norm, m_gmlp_ws, m_gmlp_bs,
                         m_attn_out_norm, m_gmlp_out_norm, m_w_out, m_xa_norm, m_mem_norm, m_xa_wq, m_xa_wkv,
                         m_xa_q_norm, m_xa_k_norm, m_xa_wo, m_ffn_norm, m_ffn_up, m_ffn_conv, m_ffn_conv_b, m_ffn_down)))
    v = dict(zip(names, (v_mix_norm, v_w_in, v_q_norm, v_k_norm, v_attn_sinks, v_gmlp_v_norm, v_gmlp_ws, v_gmlp_bs,
                         v_attn_out_norm, v_gmlp_out_norm, v_w_out, v_xa_norm, v_mem_norm, v_xa_wq, v_xa_wkv,
                         v_xa_q_norm, v_xa_k_norm, v_xa_wo, v_ffn_norm, v_ffn_up, v_ffn_conv, v_ffn_conv_b, v_ffn_down)))
    t = x.shape[1]

    me = 4 * lax.axis_index("x") + 2 * lax.axis_index("y") + lax.axis_index("c")

    def rows(a, n):
        return jnp.swapaxes(a[0], 0, 1) if n in BY_COLUMN else a[0]

    mats = [n for n in BIG if n != "ffn_conv"]
    shard = dict(zip(mats, _cast_shards([rows(w[n], n) for n in mats])), ffn_conv=w["ffn_conv"][0])
    gathers, tokens = zip(*[_exchange_start([], [shard[n] for n in grp], me, "gather_start_%d" % i,
                                            CHIP_K if i == len(GATHER_GROUPS) - 1 else ALL_K)
                            for i, grp in enumerate(GATHER_GROUPS)])

    def fetch(i, after):
        after = tokens[0] + tokens[1] + tokens[2] if after is None else after
        state = gathers[i]
        if state[3] == CHIP_K:
            state = _exchange_relay(state, after, "gather_relay_%d" % i)
        got = dict(zip(GATHER_GROUPS[i], _exchange_wait(state, after, "gather_wait_%d" % i)))
        if "w_in" in got:
            got["w_in"] = got["w_in"].reshape(IN, D)
        for n in ("w_out", "xa_wq", "xa_wo"):
            if n in got:
                got[n] = got[n].reshape(D, D)
        if "ffn_down" in got:
            got["ffn_down"] = got["ffn_down"].reshape(NG, SW, D)
            got["ffn_conv"] = got["ffn_conv"].reshape(2, NG, 3, SW)
        return got

    scatters = []

    def ship(i, grads, for_all=()):
        by_dest = [grads[n].reshape((NDEV,) + shard[n].shape) for n in SCATTER_GROUPS[i]]
        state, token = _exchange_start(by_dest, for_all, me, "scatter_start_%d" % i)
        scatters.append(state)
        return token

    conv_b = {k: d["ffn_conv_b"].reshape(NDEV, 1, SW) for k, d in (("w", w), ("m", m), ("v", v))}
    p = {n: w[n] for n in SMALL[:-1]}
    p["gmlp_ws"], p["gmlp_bs"] = w["gmlp_ws"][0], w["gmlp_bs"][0]
    p["ffn_conv_b"] = conv_b["w"].reshape(2, NG, 1, SW)
    grad_x, g, raw = _local_step(x[0], mem[0], positions.reshape(t, 1), loss_target[0], p, fetch, ship)
    loss = lax.psum(raw["loss"][0, 0], ("x", "y", "c"))

    vec, cb = _pack_small(raw)
    after = ship(2, g, [vec, cb.reshape(NDEV, 1, SW)])
    res, rest = {}, []
    for i, grp in enumerate(SCATTER_GROUPS):
        got = _exchange_wait(scatters[i], after, "scatter_wait_%d" % i)
        rest += got[len(grp):]
        parts = dict(zip(grp, got))
        for batch in ([n for n in grp if n in ADAM_ALONE], [n for n in grp if n not in ADAM_ALONE]):
            if batch:
                outs = _adam([(parts[n], rows(w[n], n), rows(m[n], n), rows(v[n], n)) for n in batch], "adam_" + batch[0])
                for n, out in zip(batch, outs):
                    res[n] = [jnp.swapaxes(o, 0, 1) if n in BY_COLUMN else o for o in out]
                    after = out[0]
    ws_parts, vec_parts, cb_parts = rest
    small = lambda d, k: {**{n: d[n] for n in SMALL[:-1]}, "ffn_conv_b": conv_b[k]}
    res.update(_adam_small(vec_parts, ws_parts, cb_parts, small(w, "w"), small(m, "m"), small(v, "v")))

    outs = [loss, grad_x[None]]
    for j in range(4):
        outs += [res[n][j].reshape(w[n].shape) for n in names]
    return tuple(outs)
```
